```python
import math
import jax, jax.numpy as jnp
from jax import lax
import numpy as np

D_MODEL = 1024
BATCH = 2
SEQ = 8192
DEPTH = 1

HEAD_DIM = 64
SB_HEADS = D_MODEL // (2 * HEAD_DIM)
SWA_HEADS = D_MODEL // (2 * HEAD_DIM)
SWA_KV_HEADS = max(1, SWA_HEADS // 4)
SB_WIDTH = SB_HEADS * HEAD_DIM
SWA_WIDTH = SWA_HEADS * HEAD_DIM
SWA_KV_WIDTH = SWA_KV_HEADS * HEAD_DIM
MIX_WIDTH = SB_WIDTH + SWA_WIDTH
IN_COLS = 3 * SB_WIDTH + SWA_WIDTH + 2 * SWA_KV_WIDTH
BLOCK = 128
WINDOW = 128
REL_BUCKETS = 32
REL_MAX_DIST = 128
D_FF = -((-8 * D_MODEL) // (3 * 256)) * 256
ALPHA = (2 * DEPTH) ** 0.25
BETA_INIT = (8 * DEPTH) ** -0.25
LN_EPS = 1e-5
RMS_EPS = 1e-6

kernel_name = "stickbreak_swa_sink_hybrid_deepnorm"


def layer_norm(x, g, b):
    xf = x.astype(jnp.float32)
    mu = jnp.mean(xf, axis=-1, keepdims=True)
    var = jnp.mean(jnp.square(xf - mu), axis=-1, keepdims=True)
    return ((xf - mu) * lax.rsqrt(var + LN_EPS)).astype(x.dtype) * g + b


def rms_norm(x, g):
    xf = x.astype(jnp.float32)
    y = xf * lax.rsqrt(jnp.mean(jnp.square(xf), axis=-1, keepdims=True) + RMS_EPS)
    return y.astype(x.dtype) * g


def t5_causal_bucket(distance):
    exact = REL_BUCKETS // 2
    d = jnp.maximum(distance, 0)
    d_f = jnp.maximum(d, 1).astype(jnp.float32)
    large = exact + (jnp.log(d_f / exact) / math.log(REL_MAX_DIST / exact)
                     * (REL_BUCKETS - exact)).astype(jnp.int32)
    large = jnp.minimum(large, REL_BUCKETS - 1)
    return jnp.where(d < exact, d, large)


def stick_breaking_attention(q, k, v):
    B, S, H, Dh = q.shape
    nb = S // BLOCK
    scale = Dh ** -0.5
    qb = q.reshape(B, nb, BLOCK, H, Dh).transpose(1, 0, 3, 2, 4)
    key_pos = jnp.arange(S)

    def one_block(args):
        q_blk, i = args
        z = jnp.einsum('bhqd,bkhd->bhqk', q_blk, k).astype(jnp.float32) * scale
        q_pos = i * BLOCK + jnp.arange(BLOCK)
        causal = key_pos[None, :] < q_pos[:, None]
        log_beta = jax.nn.log_sigmoid(z)
        log_1m_beta = jnp.where(causal, jax.nn.log_sigmoid(-z), 0.0)
        suffix = lax.cumsum(log_1m_beta, axis=3, reverse=True) - log_1m_beta
        attn = jnp.where(causal, jnp.exp(log_beta + suffix), 0.0)
        return jnp.einsum('bhqk,bkhd->bqhd', attn.astype(v.dtype), v)

    out = lax.map(one_block, (qb, jnp.arange(nb)))
    return out.transpose(1, 0, 2, 3, 4).reshape(B, S, H * Dh).astype(v.dtype)


def sliding_window_attention(q, k, v, sinks, rel_bias):
    B, S, H, Dh = q.shape
    KVH = k.shape[2]
    G = H // KVH
    nb = S // BLOCK
    scale = Dh ** -0.5
    qb = q.reshape(B, nb, BLOCK, KVH, G, Dh)

    def band(t):
        tb = t.reshape(B, nb, BLOCK, KVH, Dh)
        prev = jnp.pad(tb[:, :-1], ((0, 0), (1, 0), (0, 0), (0, 0), (0, 0)))
        return jnp.concatenate([prev, tb], axis=2)

    kb, vb = band(k), band(v)
    logits = jnp.einsum('bnqhgd,bnchd->bnhgqc', qb, kb).astype(jnp.float32) * scale

    qi = jnp.arange(BLOCK)[:, None]
    cj = jnp.arange(2 * BLOCK)[None, :]
    dist = qi + BLOCK - cj
    key_abs = jnp.arange(nb)[:, None, None] * BLOCK - BLOCK + cj[None]
    valid = (dist >= 0)[None] & (dist < WINDOW)[None] & (key_abs >= 0)
    bias = rel_bias.astype(jnp.float32)[t5_causal_bucket(dist)]
    bias = bias.transpose(2, 0, 1).reshape(KVH, G, BLOCK, 2 * BLOCK)
    logits = jnp.where(valid[None, :, None, None], logits + bias, -jnp.inf)

    sink = sinks.astype(jnp.float32).reshape(1, 1, KVH, G, 1, 1)
    m = jnp.maximum(jnp.max(logits, axis=-1, keepdims=True), sink)
    p = jnp.exp(logits - m)
    denom = jnp.sum(p, axis=-1, keepdims=True) + jnp.exp(sink - m)
    o = jnp.einsum('bnhgqc,bnchd->bnqhgd', (p / denom).astype(v.dtype), vb)
    return o.reshape(B, S, H * Dh).astype(v.dtype)


def hybrid_mixer(h, w_in, sb_norm_g, swa_norm_g, sinks, rel_bias, w_out):
    B, S, _ = h.shape
    proj = h @ w_in
    o1 = SB_WIDTH
    o2 = o1 + SB_WIDTH
    o3 = o2 + SB_WIDTH
    o4 = o3 + SWA_WIDTH
    o5 = o4 + SWA_KV_WIDTH
    q_sb, k_sb, v_sb, q_sw, k_sw, v_sw = jnp.split(proj, [o1, o2, o3, o4, o5], axis=-1)
    sb_out = stick_breaking_attention(
        q_sb.reshape(B, S, SB_HEADS, HEAD_DIM),
        k_sb.reshape(B, S, SB_HEADS, HEAD_DIM),
        v_sb.reshape(B, S, SB_HEADS, HEAD_DIM))
    swa_out = sliding_window_attention(
        q_sw.reshape(B, S, SWA_HEADS, HEAD_DIM),
        k_sw.reshape(B, S, SWA_KV_HEADS, HEAD_DIM),
        v_sw.reshape(B, S, SWA_KV_HEADS, HEAD_DIM),
        sinks, rel_bias)
    merged = jnp.concatenate([rms_norm(sb_out, sb_norm_g),
                              rms_norm(swa_out, swa_norm_g)], axis=-1)
    return merged @ w_out


def swiglu_ffn(h, w_gate_up, w_down):
    gate, up = jnp.split(h @ w_gate_up, 2, axis=-1)
    return (jax.nn.silu(gate) * up) @ w_down


def setup_inputs(seed: int = 0) -> dict:
    key = jax.random.key(seed)
    ks = jax.random.split(key, 16)
    f32 = jnp.float32
    n = lambda k, shape, s: jax.random.normal(k, shape, f32) * s
    return {
        "x": n(ks[0], (BATCH, SEQ, D_MODEL), 1.0),
        "ln_in_g": 1.0 + n(ks[1], (D_MODEL,), 0.02),
        "ln_in_b": n(ks[2], (D_MODEL,), 0.02),
        "w_in": n(ks[3], (DEPTH, D_MODEL, IN_COLS), D_MODEL ** -0.5),
        "sb_norm_g": 1.0 + n(ks[4], (DEPTH, SB_WIDTH), 0.02),
        "swa_norm_g": 1.0 + n(ks[5], (DEPTH, SWA_WIDTH), 0.02),
        "sinks": n(ks[6], (DEPTH, SWA_HEADS), 0.5),
        "rel_bias": n(ks[7], (REL_BUCKETS, SWA_HEADS), 0.5),
        "w_out": n(ks[8], (DEPTH, MIX_WIDTH, D_MODEL), MIX_WIDTH ** -0.5 * BETA_INIT),
        "ln1_g": 1.0 + n(ks[9], (DEPTH, D_MODEL), 0.02),
        "ln1_b": n(ks[10], (DEPTH, D_MODEL), 0.02),
        "w_gate_up": n(ks[11], (DEPTH, D_MODEL, 2 * D_FF), D_MODEL ** -0.5),
        "w_down": n(ks[12], (DEPTH, D_FF, D_MODEL), D_FF ** -0.5 * BETA_INIT),
        "ln2_g": 1.0 + n(ks[13], (DEPTH, D_MODEL), 0.02),
        "ln2_b": n(ks[14], (DEPTH, D_MODEL), 0.02),
    }


def reference(x, ln_in_g, ln_in_b, w_in, sb_norm_g, swa_norm_g, sinks, rel_bias,
              w_out, ln1_g, ln1_b, w_gate_up, w_down, ln2_g, ln2_b):
    h = layer_norm(x, ln_in_g, ln_in_b)
    for l in range(DEPTH):
        mix = hybrid_mixer(h, w_in[l], sb_norm_g[l], swa_norm_g[l], sinks[l], rel_bias, w_out[l])
        h = layer_norm(ALPHA * h + mix, ln1_g[l], ln1_b[l])
        ffn = swiglu_ffn(h, w_gate_up[l], w_down[l])
        h = layer_norm(ALPHA * h + ffn, ln2_g[l], ln2_b[l])
    return h
```

```python
import functools
import math

import jax
import jax.numpy as jnp
from jax import lax
from jax.experimental import pallas as pl
from jax.experimental.pallas import tpu as pltpu

F32 = jnp.float32
BF16 = jnp.bfloat16

HEAD_DIM = 64
SB_HEADS = 8
SWA_HEADS = 8
SWA_KV_HEADS = 2
SWA_GROUP = SWA_HEADS // SWA_KV_HEADS
WINDOW = 128
REL_BUCKETS = 32
REL_MAX_DIST = 128
LN_EPS = 1e-5
RMS_EPS = 1e-6
DEPTH = 1
ALPHA = (2 * DEPTH) ** 0.25
QK_SCALE = HEAD_DIM ** -0.5

LANES = 128
ATT_BLOCK = 128
ROW_TILE = 512
VMEM_LIMIT = 56 * 1024 * 1024

LOG_F32_TINY = -88.0
MASK_NEG = -1e30


def _layer_norm(x, g, b):
    mu = jnp.mean(x, axis=-1, keepdims=True)
    xc = x - mu
    var = jnp.mean(xc * xc, axis=-1, keepdims=True)
    return xc * lax.rsqrt(var + LN_EPS) * g + b


def _rms_norm(x, g):
    ms = jnp.mean(x * x, axis=-1, keepdims=True)
    return x * lax.rsqrt(ms + RMS_EPS) * g


def _dot(a, b):
    return jnp.dot(a, b, preferred_element_type=F32)


def _dot_nt(a, b):
    return lax.dot_general(a, b, (((1,), (1,)), ((), ())), preferred_element_type=F32)


def _inproj_kernel(col_splits, x_ref, g_ref, b_ref, w_ref, *out_refs):
    h = _layer_norm(x_ref[...], g_ref[...], b_ref[...]).astype(BF16)
    for (lo, hi, scale), o_ref in zip(col_splits, out_refs):
        p = _dot(h, w_ref[:, lo:hi])
        if scale != 1.0:
            p = p * scale
        o_ref[...] = p.astype(o_ref.dtype)


def _in_projection(x2, g, b, w_bf16, col_splits):
    n, d = x2.shape
    outs = [jax.ShapeDtypeStruct((n, hi - lo), BF16) for lo, hi, _ in col_splits]
    row = lambda i: (i, 0)
    fixed = lambda i: (0, 0)
    return pl.pallas_call(
        functools.partial(_inproj_kernel, col_splits),
        grid=(n // ROW_TILE,),
        in_specs=[
            pl.BlockSpec((ROW_TILE, d), row),
            pl.BlockSpec((1, d), fixed),
            pl.BlockSpec((1, d), fixed),
            pl.BlockSpec(w_bf16.shape, fixed),
        ],
        out_specs=[pl.BlockSpec((ROW_TILE, hi - lo), row) for lo, hi, _ in col_splits],
        out_shape=outs,
        compiler_params=pltpu.CompilerParams(
            dimension_semantics=("arbitrary",), vmem_limit_bytes=VMEM_LIMIT),
        name="ln_in_proj",
    )(x2, g, b, w_bf16)


def _sb_kernel(q_ref, k_ref, v_ref, o_ref, acc_ref, car_ref):
    qi = pl.program_id(2)
    tq = ATT_BLOCK
    lane = lax.broadcasted_iota(jnp.int32, (tq, LANES), 1)
    row = lax.broadcasted_iota(jnp.int32, (tq, LANES), 0)
    q = q_ref[0]
    zero = jnp.zeros_like(q)
    q_heads = (jnp.where(lane < HEAD_DIM, q, zero), jnp.where(lane >= HEAD_DIM, q, zero))

    rj = lax.broadcasted_iota(jnp.int32, (ATT_BLOCK, 2 * ATT_BLOCK), 0)
    cs = lax.broadcasted_iota(jnp.int32, (ATT_BLOCK, 2 * ATT_BLOCK), 1)
    u2 = jnp.where((cs >= ATT_BLOCK) | (rj > cs), 1.0, 0.0).astype(BF16)

    acc_ref[...] = jnp.zeros_like(acc_ref)
    car_ref[...] = jnp.zeros_like(car_ref)

    def body(state):
        kb, _ = state
        start = pl.multiple_of(kb * ATT_BLOCK, ATT_BLOCK)
        k = k_ref[0, pl.ds(start, ATT_BLOCK), :]
        v = v_ref[0, pl.ds(start, ATT_BLOCK), :]
        causal = (kb * ATT_BLOCK + lane) < (qi * tq + row)
        bound = None
        for e in range(2):
            z = _dot_nt(q_heads[e], k)
            sp = jnp.maximum(z, 0.0) + jnp.log1p(jnp.exp(-jnp.abs(z)))
            l1m = jnp.where(causal, -sp, 0.0)
            hi = l1m.astype(BF16)
            lo = (l1m - hi.astype(F32)).astype(BF16)
            s2 = _dot(hi, u2) + _dot(lo, u2)
            car = car_ref[e]
            log_a = (z - sp) + s2[:, :ATT_BLOCK] + car
            a = jnp.where(causal, jnp.exp(log_a), 0.0).astype(BF16)
            acc_ref[e] += _dot(a, v)
            car = car + s2[:, ATT_BLOCK:]
            car_ref[e] = car
            m = jnp.max(car)
            bound = m if bound is None else jnp.maximum(bound, m)
        return kb - 1, bound

    def cond(state):
        kb, bound = state
        return jnp.logical_and(kb >= 0, bound > LOG_F32_TINY)

    lax.while_loop(cond, body, (qi, jnp.float32(0.0)))
    o_ref[0] = jnp.where(lane < HEAD_DIM, acc_ref[0], acc_ref[1]).astype(o_ref.dtype)


def _sb_attention(q, k, v):
    b, s, w = q.shape
    pairs = w // LANES
    nq = s // ATT_BLOCK
    return pl.pallas_call(
        _sb_kernel,
        grid=(b, pairs, nq),
        in_specs=[
            pl.BlockSpec((1, ATT_BLOCK, LANES), lambda bi, p, i: (bi, i, p)),
            pl.BlockSpec((1, s, LANES), lambda bi, p, i: (bi, 0, p)),
            pl.BlockSpec((1, s, LANES), lambda bi, p, i: (bi, 0, p)),
        ],
        out_specs=pl.BlockSpec((1, ATT_BLOCK, LANES), lambda bi, p, i: (bi, i, p)),
        out_shape=jax.ShapeDtypeStruct((b, s, w), BF16),
        scratch_shapes=[
            pltpu.VMEM((2, ATT_BLOCK, LANES), F32),
            pltpu.VMEM((2, ATT_BLOCK, LANES), F32),
        ],
        compiler_params=pltpu.CompilerParams(
            dimension_semantics=("arbitrary", "arbitrary", "arbitrary"),
            vmem_limit_bytes=VMEM_LIMIT),
        name="stick_breaking_attention",
    )(q, k, v)


def _t5_causal_bucket(distance):
    exact = REL_BUCKETS // 2
    d = jnp.maximum(distance, 0)
    d_f = jnp.maximum(d, 1).astype(F32)
    large = exact + (jnp.log(d_f / exact) / math.log(REL_MAX_DIST / exact)
                     * (REL_BUCKETS - exact)).astype(jnp.int32)
    large = jnp.minimum(large, REL_BUCKETS - 1)
    return jnp.where(d < exact, d, large)


def _swa_kernel(sinks_ref, relb_ref, q_ref, kc_ref, kp_ref, vc_ref, vp_ref, o_ref, bias_ref):
    bi = pl.program_id(0)
    qi = pl.program_id(1)
    blk = ATT_BLOCK

    @pl.when(jnp.logical_and(bi == 0, qi == 0))
    def _fill_bias():
        r = lax.broadcasted_iota(jnp.int32, (blk, 2 * blk), 0)
        c = lax.broadcasted_iota(jnp.int32, (blk, 2 * blk), 1)
        dist = r + blk - c
        bucket = _t5_causal_bucket(dist)
        valid = jnp.logical_and(dist >= 0, dist < WINDOW)
        for h in range(SWA_HEADS):
            bias = jnp.zeros((blk, 2 * blk), F32)
            for bk in range(REL_BUCKETS):
                bias = jnp.where(bucket == bk, relb_ref[bk, h], bias)
            bias_ref[h] = jnp.where(valid, bias, MASK_NEG)

    q = q_ref[0]
    prev_pen = jnp.where(qi > 0, 0.0, MASK_NEG).astype(F32)
    for j in range(SWA_KV_HEADS):
        ks = slice(j * HEAD_DIM, (j + 1) * HEAD_DIM)
        kc, kp = kc_ref[0, :, ks], kp_ref[0, :, ks]
        vc, vp = vc_ref[0, :, ks], vp_ref[0, :, ks]
        for g in range(SWA_GROUP):
            h = j * SWA_GROUP + g
            hs = slice(h * HEAD_DIM, (h + 1) * HEAD_DIM)
            qh = q[:, hs]
            s_prev = _dot_nt(qh, kp) + bias_ref[h, :, :blk] + prev_pen
            s_cur = _dot_nt(qh, kc) + bias_ref[h, :, blk:]
            sink = sinks_ref[h]
            m = jnp.maximum(jnp.max(s_prev, axis=-1, keepdims=True),
                            jnp.max(s_cur, axis=-1, keepdims=True))
            m = jnp.maximum(m, sink)
            p_prev = jnp.exp(s_prev - m)
            p_cur = jnp.exp(s_cur - m)
            denom = (jnp.sum(p_prev, axis=-1, keepdims=True)
                     + jnp.sum(p_cur, axis=-1, keepdims=True) + jnp.exp(sink - m))
            o = _dot(p_prev.astype(BF16), vp) + _dot(p_cur.astype(BF16), vc)
            o_ref[0, :, hs] = (o / denom).astype(o_ref.dtype)


def _swa_attention(q, k, v, sinks, rel_bias):
    b, s, w = q.shape
    kvw = k.shape[-1]
    nq = s // ATT_BLOCK
    cur = lambda bi, i: (bi, i, 0)
    prev = lambda bi, i: (bi, jnp.maximum(i - 1, 0), 0)
    smem = pl.BlockSpec(memory_space=pltpu.SMEM)
    return pl.pallas_call(
        _swa_kernel,
        grid=(b, nq),
        in_specs=[
            smem, smem,
            pl.BlockSpec((1, ATT_BLOCK, w), cur),
            pl.BlockSpec((1, ATT_BLOCK, kvw), cur),
            pl.BlockSpec((1, ATT_BLOCK, kvw), prev),
            pl.BlockSpec((1, ATT_BLOCK, kvw), cur),
            pl.BlockSpec((1, ATT_BLOCK, kvw), prev),
        ],
        out_specs=pl.BlockSpec((1, ATT_BLOCK, w), cur),
        out_shape=jax.ShapeDtypeStruct((b, s, w), BF16),
        scratch_shapes=[pltpu.VMEM((SWA_HEADS, ATT_BLOCK, 2 * ATT_BLOCK), F32)],
        compiler_params=pltpu.CompilerParams(
            dimension_semantics=("arbitrary", "arbitrary"), vmem_limit_bytes=VMEM_LIMIT),
        name="sliding_window_attention",
    )(sinks, rel_bias, q, k, k, v, v)


def _mix_kernel(x_ref, sb_ref, sw_ref, ling_ref, linb_ref, sbg_ref, swg_ref, w_ref,
                l1g_ref, l1b_ref, o_ref):
    h = _layer_norm(x_ref[...], ling_ref[...], linb_ref[...])
    sb = _rms_norm(sb_ref[...].astype(F32), sbg_ref[...]).astype(BF16)
    sw = _rms_norm(sw_ref[...].astype(F32), swg_ref[...]).astype(BF16)
    n_sb = sb.shape[-1]
    mix = _dot(sb, w_ref[:n_sb, :]) + _dot(sw, w_ref[n_sb:, :])
    o_ref[...] = _layer_norm(ALPHA * h + mix, l1g_ref[...], l1b_ref[...])


def _mix_block(x2, sb2, sw2, ln_in_g, ln_in_b, sb_g, sw_g, w_out_bf16, ln1_g, ln1_b):
    n, d = x2.shape
    row = lambda i: (i, 0)
    fixed = lambda i: (0, 0)
    vec = lambda a: pl.BlockSpec((1, a.shape[-1]), fixed)
    return pl.pallas_call(
        _mix_kernel,
        grid=(n // ROW_TILE,),
        in_specs=[
            pl.BlockSpec((ROW_TILE, d), row),
            pl.BlockSpec((ROW_TILE, sb2.shape[-1]), row),
            pl.BlockSpec((ROW_TILE, sw2.shape[-1]), row),
            vec(ln_in_g), vec(ln_in_b), vec(sb_g), vec(sw_g),
            pl.BlockSpec(w_out_bf16.shape, fixed),
            vec(ln1_g), vec(ln1_b),
        ],
        out_specs=pl.BlockSpec((ROW_TILE, d), row),
        out_shape=jax.ShapeDtypeStruct((n, d), F32),
        compiler_params=pltpu.CompilerParams(
            dimension_semantics=("arbitrary",), vmem_limit_bytes=VMEM_LIMIT),
        name="mix_out_proj_ln1",
    )(x2, sb2, sw2, ln_in_g, ln_in_b, sb_g, sw_g, w_out_bf16, ln1_g, ln1_b)


def _ffn_kernel(n_chunks, h_ref, wgu_ref, wd_ref, g_ref, b_ref, o_ref):
    h = h_ref[...]
    hb = h.astype(BF16)
    d_ff = wd_ref.shape[0]
    ck = d_ff // n_chunks
    acc = None
    for c in range(n_chunks):
        gate = _dot(hb, wgu_ref[:, c * ck:(c + 1) * ck])
        up = _dot(hb, wgu_ref[:, d_ff + c * ck:d_ff + (c + 1) * ck])
        act = (gate / (1.0 + jnp.exp(-gate)) * up).astype(BF16)
        part = _dot(act, wd_ref[c * ck:(c + 1) * ck, :])
        acc = part if acc is None else acc + part
    o_ref[...] = _layer_norm(ALPHA * h + acc, g_ref[...], b_ref[...])


def _ffn_block(h1, w_gate_up_bf16, w_down_bf16, ln2_g, ln2_b):
    n, d = h1.shape
    d_ff = w_down_bf16.shape[0]
    n_chunks = 2 if d_ff % (2 * LANES) == 0 else 1
    row = lambda i: (i, 0)
    fixed = lambda i: (0, 0)
    once = pl.Buffered(1)
    return pl.pallas_call(
        functools.partial(_ffn_kernel, n_chunks),
        grid=(n // ROW_TILE,),
        in_specs=[
            pl.BlockSpec((ROW_TILE, d), row),
            pl.BlockSpec(w_gate_up_bf16.shape, fixed, pipeline_mode=once),
            pl.BlockSpec(w_down_bf16.shape, fixed, pipeline_mode=once),
            pl.BlockSpec((1, d), fixed),
            pl.BlockSpec((1, d), fixed),
        ],
        out_specs=pl.BlockSpec((ROW_TILE, d), row),
        out_shape=jax.ShapeDtypeStruct((n, d), F32),
        compiler_params=pltpu.CompilerParams(
            dimension_semantics=("arbitrary",), vmem_limit_bytes=VMEM_LIMIT),
        name="swiglu_ffn_ln2",
    )(h1, w_gate_up_bf16, w_down_bf16, ln2_g, ln2_b)


def kernel(x, ln_in_g, ln_in_b, w_in, sb_norm_g, swa_norm_g, sinks, rel_bias, w_out,
           ln1_g, ln1_b, w_gate_up, w_down, ln2_g, ln2_b):
    b, s, d = x.shape
    assert w_in.shape[0] == DEPTH == 1
    sb_w = SB_HEADS * HEAD_DIM
    sw_w = SWA_HEADS * HEAD_DIM
    kv_w = SWA_KV_HEADS * HEAD_DIM
    assert w_in.shape[-1] == 3 * sb_w + sw_w + 2 * kv_w
    assert s % ATT_BLOCK == 0 and (b * s) % ROW_TILE == 0 and WINDOW <= ATT_BLOCK

    x2 = x.reshape(b * s, d)
    row_vec = lambda a: a.reshape(1, -1)
    ling, linb = row_vec(ln_in_g), row_vec(ln_in_b)

    o1, o2, o3 = sb_w, 2 * sb_w, 3 * sb_w
    o4 = o3 + sw_w
    o5 = o4 + kv_w
    splits = ((0, o1, QK_SCALE), (o1, o2, 1.0), (o2, o3, 1.0),
              (o3, o4, QK_SCALE), (o4, o5, 1.0), (o5, o5 + kv_w, 1.0))
    q_sb, k_sb, v_sb, q_sw, k_sw, v_sw = _in_projection(
        x2, ling, linb, w_in[0].astype(BF16), splits)

    to3 = lambda a: a.reshape(b, s, a.shape[-1])
    sb_out = _sb_attention(to3(q_sb), to3(k_sb), to3(v_sb))
    sw_out = _swa_attention(to3(q_sw), to3(k_sw), to3(v_sw), sinks[0], rel_bias)

    h1 = _mix_block(x2, sb_out.reshape(b * s, sb_w), sw_out.reshape(b * s, sw_w),
                    ling, linb, row_vec(sb_norm_g[0]), row_vec(swa_norm_g[0]),
                    w_out[0].astype(BF16), row_vec(ln1_g[0]), row_vec(ln1_b[0]))
    out = _ffn_block(h1, w_gate_up[0].astype(BF16), w_down[0].astype(BF16),
                     row_vec(ln2_g[0]), row_vec(ln2_b[0]))
    return out.reshape(b, s, d)
```

```python
import functools
import math

import jax
import jax.numpy as jnp
from jax import lax
from jax.experimental import pallas as pl
from jax.experimental.pallas import tpu as pltpu

F32 = jnp.float32
BF16 = jnp.bfloat16

HEAD_DIM = 64
SB_HEADS = 8
SWA_HEADS = 8
SWA_KV_HEADS = 2
WINDOW = 128
REL_BUCKETS = 32
REL_MAX_DIST = 128
LN_EPS = 1e-5
RMS_EPS = 1e-6
DEPTH = 1
ALPHA = (2 * DEPTH) ** 0.25
QK_SCALE = HEAD_DIM ** -0.5
LOG2E = math.log2(math.e)

LANES = 128
ATT_BLOCK = 128
SB_WINDOW = 2 * ATT_BLOCK
ROW_TILE = 512
VMEM_LIMIT = 56 * 1024 * 1024

SB_DEAD_LOG2 = 127.0
MASK_NEG = -1e30


def _layer_norm(x, g, b):
    mu = jnp.mean(x, axis=-1, keepdims=True)
    xc = x - mu
    var = jnp.mean(xc * xc, axis=-1, keepdims=True)
    return xc * lax.rsqrt(var + LN_EPS) * g + b


def _rms_norm(x, g):
    ms = jnp.mean(x * x, axis=-1, keepdims=True)
    return x * lax.rsqrt(ms + RMS_EPS) * g


def _dot(a, b):
    return jnp.dot(a, b, preferred_element_type=F32)


def _dot_nt(a, b):
    return lax.dot_general(a, b, (((1,), (1,)), ((), ())), preferred_element_type=F32)


def _inproj_kernel(splits, x_ref, g_ref, b_ref, w_ref, qsb_ref, ksb_ref, vsb_ref, qsw_ref, kv_ref):
    h = _layer_norm(x_ref[...], g_ref[...], b_ref[...]).astype(BF16)
    proj = lambda name: _dot(h, w_ref[:, splits[name][0]:splits[name][1]])
    qsb_ref[...] = (proj("q_sb") * QK_SCALE).astype(BF16)
    ksb_ref[...] = proj("k_sb").astype(BF16)
    vsb_ref[...] = proj("v_sb").astype(BF16)
    qsw_ref[...] = (proj("q_sw") * QK_SCALE).astype(BF16)
    k = proj("k_sw")
    v = proj("v_sw")
    kvw = k.shape[-1]
    kv_ref[:, 0 * kvw:1 * kvw] = k.astype(BF16)
    kv_ref[:, 1 * kvw:2 * kvw] = pltpu.roll(k, HEAD_DIM, axis=1).astype(BF16)
    kv_ref[:, 2 * kvw:3 * kvw] = v.astype(BF16)
    kv_ref[:, 3 * kvw:4 * kvw] = pltpu.roll(v, HEAD_DIM, axis=1).astype(BF16)


def _in_projection(x2, g, b, w_bf16, splits):
    n, d = x2.shape
    width = lambda name: splits[name][1] - splits[name][0]
    out_w = [width("q_sb"), width("k_sb"), width("v_sb"), width("q_sw"), 4 * width("k_sw")]
    row = lambda i: (i, 0)
    fixed = lambda i: (0, 0)
    return pl.pallas_call(
        functools.partial(_inproj_kernel, splits),
        grid=(n // ROW_TILE,),
        in_specs=[
            pl.BlockSpec((ROW_TILE, d), row),
            pl.BlockSpec((1, d), fixed),
            pl.BlockSpec((1, d), fixed),
            pl.BlockSpec(w_bf16.shape, fixed),
        ],
        out_specs=[pl.BlockSpec((ROW_TILE, w), row) for w in out_w],
        out_shape=[jax.ShapeDtypeStruct((n, w), BF16) for w in out_w],
        compiler_params=pltpu.CompilerParams(
            dimension_semantics=("arbitrary",), vmem_limit_bytes=VMEM_LIMIT),
        name="ln_in_proj",
    )(x2, g, b, w_bf16)


def _split_heads(x, lane):
    zero = jnp.zeros_like(x)
    return jnp.concatenate([jnp.where(lane < HEAD_DIM, x, zero),
                            jnp.where(lane >= HEAD_DIM, x, zero)], axis=0)


def _merge_heads(o2, lane):
    t = o2.shape[0] // 2
    return jnp.where(lane < HEAD_DIM, o2[:t], o2[t:])


def _sb_step(qq, k, v, tri, valid, carried):
    zl = _dot_nt(qq, k) * LOG2E
    c = jnp.maximum(zl, 0.0) + jnp.log2(1.0 + jnp.exp2(-jnp.abs(zl)))
    cm = jnp.where(valid, c, 0.0)
    hi = cm.astype(BF16)
    lo = (cm - hi.astype(F32)).astype(BF16)
    later = _dot(hi, tri) + _dot(lo, tri)
    log_a = (zl - c) - later
    if carried is not None:
        log_a = log_a - jnp.concatenate([carried] * (log_a.shape[1] // LANES), axis=1)
    a = jnp.where(valid, jnp.exp2(log_a), 0.0).astype(BF16)
    total = later[:, :LANES] + cm[:, :LANES]
    return _dot(a, v), total


def _sb_kernel(q_ref, k_ref, v_ref, o_ref, tri_ref, acc_ref, car_ref):
    bi = pl.program_id(0)
    qi = pl.program_id(1)
    t, w = ATT_BLOCK, SB_WINDOW
    pairs = q_ref.shape[-1] // LANES

    @pl.when(jnp.logical_and(bi == 0, qi == 0))
    def _fill_tri():
        j = lax.broadcasted_iota(jnp.int32, (w, w), 0)
        s = lax.broadcasted_iota(jnp.int32, (w, w), 1)
        tri_ref[...] = jnp.where(j > s, 1.0, 0.0).astype(BF16)

    lane = lax.broadcasted_iota(jnp.int32, (t, LANES), 1)
    row2 = lax.broadcasted_iota(jnp.int32, (2 * t, w), 0)
    col2 = lax.broadcasted_iota(jnp.int32, (2 * t, w), 1)
    q_pos = qi * t + (row2 & (t - 1))
    start0 = pl.multiple_of(jnp.maximum(qi - 1, 0) * t, t)
    valid0 = (start0 + col2) < q_pos
    tri = tri_ref[...]

    def operands(p, start):
        cols = slice(p * LANES, (p + 1) * LANES)
        return (_split_heads(q_ref[0, :, cols], lane),
                k_ref[0, pl.ds(start, w), cols], v_ref[0, pl.ds(start, w), cols])

    live = None
    for p in range(pairs):
        qq, k, v = operands(p, start0)
        o2, total = _sb_step(qq, k, v, tri, valid0, None)
        o_ref[0, :, p * LANES:(p + 1) * LANES] = _merge_heads(o2, lane).astype(o_ref.dtype)
        acc_ref[p] = o2
        car_ref[p] = total
        m = jnp.min(total[:, 0:1])
        live = m if live is None else jnp.minimum(live, m)

    @pl.when(jnp.logical_and(start0 > 0, live < SB_DEAD_LOG2))
    def _walk_back():
        for p in range(pairs):
            car_ref[p] = jnp.broadcast_to(car_ref[p][:, 0:1], (2 * t, LANES))

        def body(state):
            done_from, _ = state
            start = pl.multiple_of(jnp.maximum(done_from - w, 0), t)
            valid = (start + col2) < done_from
            nxt = None
            for p in range(pairs):
                qq, k, v = operands(p, start)
                carried = car_ref[p]
                o2, total = _sb_step(qq, k, v, tri, valid, carried)
                acc_ref[p] += o2
                carried = carried + total[:, 0:1]
                car_ref[p] = carried
                m = jnp.min(carried)
                nxt = m if nxt is None else jnp.minimum(nxt, m)
            return start, nxt

        def cond(state):
            done_from, nxt = state
            return jnp.logical_and(done_from > 0, nxt < SB_DEAD_LOG2)

        lax.while_loop(cond, body, (start0, live))
        for p in range(pairs):
            o_ref[0, :, p * LANES:(p + 1) * LANES] = _merge_heads(acc_ref[p], lane).astype(o_ref.dtype)


def _sb_attention(q, k, v):
    b, s, wd = q.shape
    pairs = wd // LANES
    nq = s // ATT_BLOCK
    tile = lambda bi, i: (bi, i, 0)
    whole = lambda bi, i: (bi, 0, 0)
    once = pl.Buffered(1)
    return pl.pallas_call(
        _sb_kernel,
        grid=(b, nq),
        in_specs=[
            pl.BlockSpec((1, ATT_BLOCK, wd), tile),
            pl.BlockSpec((1, s, wd), whole, pipeline_mode=once),
            pl.BlockSpec((1, s, wd), whole, pipeline_mode=once),
        ],
        out_specs=pl.BlockSpec((1, ATT_BLOCK, wd), tile),
        out_shape=jax.ShapeDtypeStruct((b, s, wd), BF16),
        scratch_shapes=[
            pltpu.VMEM((SB_WINDOW, SB_WINDOW), BF16),
            pltpu.VMEM((pairs, 2 * ATT_BLOCK, LANES), F32),
            pltpu.VMEM((pairs, 2 * ATT_BLOCK, LANES), F32),
        ],
        compiler_params=pltpu.CompilerParams(
            dimension_semantics=("arbitrary", "arbitrary"), vmem_limit_bytes=VMEM_LIMIT),
        name="stick_breaking_attention",
    )(q, k, v)


def _t5_causal_bucket(distance):
    exact = REL_BUCKETS // 2
    d = jnp.maximum(distance, 0)
    d_f = jnp.maximum(d, 1).astype(F32)
    large = exact + (jnp.log(d_f / exact) / math.log(REL_MAX_DIST / exact)
                     * (REL_BUCKETS - exact)).astype(jnp.int32)
    large = jnp.minimum(large, REL_BUCKETS - 1)
    return jnp.where(d < exact, d, large)


def _swa_kernel(sinks_ref, relb_ref, q_ref, kvc_ref, kvp_ref, o_ref, bias_ref):
    bi = pl.program_id(0)
    qi = pl.program_id(1)
    t = ATT_BLOCK
    kvw = kvc_ref.shape[-1] // 4
    group = SWA_HEADS // SWA_KV_HEADS

    @pl.when(jnp.logical_and(bi == 0, qi == 0))
    def _fill_bias():
        r = lax.broadcasted_iota(jnp.int32, (t, 2 * t), 0)
        c = lax.broadcasted_iota(jnp.int32, (t, 2 * t), 1)
        dist = r + t - c
        bucket = _t5_causal_bucket(dist)
        valid = jnp.logical_and(dist >= 0, dist < WINDOW)
        for h in range(SWA_HEADS):
            bias = jnp.zeros((t, 2 * t), F32)
            for bk in range(REL_BUCKETS):
                bias = jnp.where(bucket == bk, relb_ref[bk, h], bias)
            bias_ref[h] = jnp.where(valid, bias, MASK_NEG)

    lane = lax.broadcasted_iota(jnp.int32, (t, LANES), 1)
    row2 = lax.broadcasted_iota(jnp.int32, (2 * t, 1), 0)
    col2 = lax.broadcasted_iota(jnp.int32, (2 * t, 2 * t), 1)
    prev_pen = jnp.where(jnp.logical_and(col2 < t, qi == 0), MASK_NEG, 0.0)

    def kv_window(variant):
        cols = slice(variant * kvw, (variant + 1) * kvw)
        return jnp.concatenate([kvp_ref[0, :, cols], kvc_ref[0, :, cols]], axis=0)

    results = {}
    for half in range(2):
        for j in range(SWA_KV_HEADS):
            heads = [h for h in range(j * group, (j + 1) * group) if h % 2 == half]
            swapped = int(j != half)
            kk = kv_window(0 + swapped)
            vv = kv_window(2 + swapped)
            in_half = (lane >= HEAD_DIM) if half else (lane < HEAD_DIM)
            qq = jnp.concatenate(
                [jnp.where(in_half, q_ref[0, :, (h // 2) * LANES:(h // 2 + 1) * LANES], 0)
                 for h in heads], axis=0)
            bias = jnp.concatenate([bias_ref[h] for h in heads], axis=0)
            s = _dot_nt(qq, kk) + bias + prev_pen
            sink = jnp.where(row2 < t, sinks_ref[heads[0]], sinks_ref[heads[1]])
            m = jnp.maximum(jnp.max(s, axis=-1, keepdims=True), sink)
            p = jnp.exp(s - m)
            denom = jnp.sum(p, axis=-1, keepdims=True) + jnp.exp(sink - m)
            o = _dot(p.astype(BF16), vv) * (1.0 / denom)
            results[heads[0]] = o[:t]
            results[heads[1]] = o[t:]

    for pair in range(SWA_HEADS // 2):
        o_ref[0, :, pair * LANES:(pair + 1) * LANES] = jnp.where(
            lane < HEAD_DIM, results[2 * pair], results[2 * pair + 1]).astype(o_ref.dtype)


def _swa_attention(q, kv, sinks, rel_bias):
    b, s, wd = q.shape
    nq = s // ATT_BLOCK
    cur = lambda bi, i: (bi, i, 0)
    prev = lambda bi, i: (bi, jnp.maximum(i - 1, 0), 0)
    smem = pl.BlockSpec(memory_space=pltpu.SMEM)
    return pl.pallas_call(
        _swa_kernel,
        grid=(b, nq),
        in_specs=[
            smem, smem,
            pl.BlockSpec((1, ATT_BLOCK, wd), cur),
            pl.BlockSpec((1, ATT_BLOCK, kv.shape[-1]), cur),
            pl.BlockSpec((1, ATT_BLOCK, kv.shape[-1]), prev),
        ],
        out_specs=pl.BlockSpec((1, ATT_BLOCK, wd), cur),
        out_shape=jax.ShapeDtypeStruct((b, s, wd), BF16),
        scratch_shapes=[pltpu.VMEM((SWA_HEADS, ATT_BLOCK, 2 * ATT_BLOCK), F32)],
        compiler_params=pltpu.CompilerParams(
            dimension_semantics=("arbitrary", "arbitrary"), vmem_limit_bytes=VMEM_LIMIT),
        name="sliding_window_attention",
    )(sinks, rel_bias, q, kv, kv)


def _mix_kernel(x_ref, sb_ref, sw_ref, ling_ref, linb_ref, sbg_ref, swg_ref, w_ref,
                l1g_ref, l1b_ref, o_ref):
    h = _layer_norm(x_ref[...], ling_ref[...], linb_ref[...])
    sb = _rms_norm(sb_ref[...].astype(F32), sbg_ref[...]).astype(BF16)
    sw = _rms_norm(sw_ref[...].astype(F32), swg_ref[...]).astype(BF16)
    n_sb = sb.shape[-1]
    mix = _dot(sb, w_ref[:n_sb, :]) + _dot(sw, w_ref[n_sb:, :])
    o_ref[...] = _layer_norm(ALPHA * h + mix, l1g_ref[...], l1b_ref[...])


def _mix_block(x2, sb2, sw2, ln_in_g, ln_in_b, sb_g, sw_g, w_out_bf16, ln1_g, ln1_b):
    n, d = x2.shape
    row = lambda i: (i, 0)
    fixed = lambda i: (0, 0)
    vec = lambda a: pl.BlockSpec((1, a.shape[-1]), fixed)
    return pl.pallas_call(
        _mix_kernel,
        grid=(n // ROW_TILE,),
        in_specs=[
            pl.BlockSpec((ROW_TILE, d), row),
            pl.BlockSpec((ROW_TILE, sb2.shape[-1]), row),
            pl.BlockSpec((ROW_TILE, sw2.shape[-1]), row),
            vec(ln_in_g), vec(ln_in_b), vec(sb_g), vec(sw_g),
            pl.BlockSpec(w_out_bf16.shape, fixed),
            vec(ln1_g), vec(ln1_b),
        ],
        out_specs=pl.BlockSpec((ROW_TILE, d), row),
        out_shape=jax.ShapeDtypeStruct((n, d), F32),
        compiler_params=pltpu.CompilerParams(
            dimension_semantics=("arbitrary",), vmem_limit_bytes=VMEM_LIMIT),
        name="mix_out_proj_ln1",
    )(x2, sb2, sw2, ln_in_g, ln_in_b, sb_g, sw_g, w_out_bf16, ln1_g, ln1_b)


def _ffn_kernel(n_chunks, h_ref, wgu_ref, wd_ref, g_ref, b_ref, o_ref):
    h = h_ref[...]
    hb = h.astype(BF16)
    d_ff = wd_ref.shape[0]
    ck = d_ff // n_chunks
    acc = None
    for c in range(n_chunks):
        gate = _dot(hb, wgu_ref[:, c * ck:(c + 1) * ck])
        up = _dot(hb, wgu_ref[:, d_ff + c * ck:d_ff + (c + 1) * ck])
        act = (gate / (1.0 + jnp.exp(-gate)) * up).astype(BF16)
        part = _dot(act, wd_ref[c * ck:(c + 1) * ck, :])
        acc = part if acc is None else acc + part
    o_ref[...] = _layer_norm(ALPHA * h + acc, g_ref[...], b_ref[...])


def _ffn_block(h1, w_gate_up_bf16, w_down_bf16, ln2_g, ln2_b):
    n, d = h1.shape
    d_ff = w_down_bf16.shape[0]
    n_chunks = 2 if d_ff % (2 * LANES) == 0 else 1
    row = lambda i: (i, 0)
    fixed = lambda i: (0, 0)
    once = pl.Buffered(1)
    return pl.pallas_call(
        functools.partial(_ffn_kernel, n_chunks),
        grid=(n // ROW_TILE,),
        in_specs=[
            pl.BlockSpec((ROW_TILE, d), row),
            pl.BlockSpec(w_gate_up_bf16.shape, fixed, pipeline_mode=once),
            pl.BlockSpec(w_down_bf16.shape, fixed, pipeline_mode=once),
            pl.BlockSpec((1, d), fixed),
            pl.BlockSpec((1, d), fixed),
        ],
        out_specs=pl.BlockSpec((ROW_TILE, d), row),
        out_shape=jax.ShapeDtypeStruct((n, d), F32),
        compiler_params=pltpu.CompilerParams(
            dimension_semantics=("arbitrary",), vmem_limit_bytes=VMEM_LIMIT),
        name="swiglu_ffn_ln2",
    )(h1, w_gate_up_bf16, w_down_bf16, ln2_g, ln2_b)


def kernel(x, ln_in_g, ln_in_b, w_in, sb_norm_g, swa_norm_g, sinks, rel_bias, w_out,
           ln1_g, ln1_b, w_gate_up, w_down, ln2_g, ln2_b):
    b, s, d = x.shape
    assert w_in.shape[0] == DEPTH == 1
    sb_w = SB_HEADS * HEAD_DIM
    sw_w = SWA_HEADS * HEAD_DIM
    kv_w = SWA_KV_HEADS * HEAD_DIM
    assert w_in.shape[-1] == 3 * sb_w + sw_w + 2 * kv_w
    assert kv_w == LANES and SWA_HEADS // SWA_KV_HEADS == 4
    assert s % SB_WINDOW == 0 and (b * s) % ROW_TILE == 0 and WINDOW <= ATT_BLOCK

    x2 = x.reshape(b * s, d)
    row_vec = lambda a: a.reshape(1, -1)
    ling, linb = row_vec(ln_in_g), row_vec(ln_in_b)

    names = ("q_sb", "k_sb", "v_sb", "q_sw", "k_sw", "v_sw")
    widths = (sb_w, sb_w, sb_w, sw_w, kv_w, kv_w)
    splits, lo = {}, 0
    for name, wd in zip(names, widths):
        splits[name] = (lo, lo + wd)
        lo += wd
    q_sb, k_sb, v_sb, q_sw, kv_sw = _in_projection(x2, ling, linb, w_in[0].astype(BF16), splits)

    to3 = lambda a: a.reshape(b, s, a.shape[-1])
    sb_out = _sb_attention(to3(q_sb), to3(k_sb), to3(v_sb))
    sw_out = _swa_attention(to3(q_sw), to3(kv_sw), sinks[0], rel_bias)

    h1 = _mix_block(x2, sb_out.reshape(b * s, sb_w), sw_out.reshape(b * s, sw_w),
                    ling, linb, row_vec(sb_norm_g[0]), row_vec(swa_norm_g[0]),
                    w_out[0].astype(BF16), row_vec(ln1_g[0]), row_vec(ln1_b[0]))
    out = _ffn_block(h1, w_gate_up[0].astype(BF16), w_down[0].astype(BF16),
                     row_vec(ln2_g[0]), row_vec(ln2_b[0]))
    return out.reshape(b, s, d)
```

```python
import functools
import math

import jax
import jax.numpy as jnp
from jax import lax
from jax.experimental import pallas as pl
from jax.experimental.pallas import tpu as pltpu

F32 = jnp.float32
BF16 = jnp.bfloat16

HEAD_DIM = 64
SB_HEADS = 8
SWA_HEADS = 8
SWA_KV_HEADS = 2
WINDOW = 128
REL_BUCKETS = 32
REL_MAX_DIST = 128
LN_EPS = 1e-5
RMS_EPS = 1e-6
DEPTH = 1
ALPHA = (2 * DEPTH) ** 0.25
QK_SCALE = HEAD_DIM ** -0.5
LOG2E = math.log2(math.e)

LANES = 128
ATT_BLOCK = 128
SB_WINDOW = 2 * ATT_BLOCK
ROW_TILE = 512
VMEM_LIMIT = 56 * 1024 * 1024

SB_DEAD_LOG2 = 127.0
MASK_NEG = -1e30


def _layer_norm(x, g, b):
    mu = jnp.mean(x, axis=-1, keepdims=True)
    xc = x - mu
    var = jnp.mean(xc * xc, axis=-1, keepdims=True)
    return xc * lax.rsqrt(var + LN_EPS) * g + b


def _rms_norm(x, g):
    ms = jnp.mean(x * x, axis=-1, keepdims=True)
    return x * lax.rsqrt(ms + RMS_EPS) * g


def _dot(a, b):
    return jnp.dot(a, b, preferred_element_type=F32)


def _dot_nt(a, b):
    return lax.dot_general(a, b, (((1,), (1,)), ((), ())), preferred_element_type=F32)


def _inproj_kernel(splits, x_ref, g_ref, b_ref, w_ref, qsb_ref, ksb_ref, vsb_ref, qsw_ref, kv_ref):
    h = _layer_norm(x_ref[...], g_ref[...], b_ref[...]).astype(BF16)
    proj = lambda name: _dot(h, w_ref[:, splits[name][0]:splits[name][1]])
    qsb_ref[...] = (proj("q_sb") * QK_SCALE).astype(BF16)
    ksb_ref[...] = proj("k_sb").astype(BF16)
    vsb_ref[...] = proj("v_sb").astype(BF16)
    qsw_ref[...] = (proj("q_sw") * QK_SCALE).astype(BF16)
    k = proj("k_sw")
    v = proj("v_sw")
    kvw = k.shape[-1]
    kv_ref[:, 0 * kvw:1 * kvw] = k.astype(BF16)
    kv_ref[:, 1 * kvw:2 * kvw] = pltpu.roll(k, HEAD_DIM, axis=1).astype(BF16)
    kv_ref[:, 2 * kvw:3 * kvw] = v.astype(BF16)
    kv_ref[:, 3 * kvw:4 * kvw] = pltpu.roll(v, HEAD_DIM, axis=1).astype(BF16)


def _in_projection(x2, g, b, w_bf16, splits):
    n, d = x2.shape
    width = lambda name: splits[name][1] - splits[name][0]
    out_w = [width("q_sb"), width("k_sb"), width("v_sb"), width("q_sw"), 4 * width("k_sw")]
    row = lambda i: (i, 0)
    fixed = lambda i: (0, 0)
    return pl.pallas_call(
        functools.partial(_inproj_kernel, splits),
        grid=(n // ROW_TILE,),
        in_specs=[
            pl.BlockSpec((ROW_TILE, d), row),
            pl.BlockSpec((1, d), fixed),
            pl.BlockSpec((1, d), fixed),
            pl.BlockSpec(w_bf16.shape, fixed),
        ],
        out_specs=[pl.BlockSpec((ROW_TILE, w), row) for w in out_w],
        out_shape=[jax.ShapeDtypeStruct((n, w), BF16) for w in out_w],
        compiler_params=pltpu.CompilerParams(
            dimension_semantics=("arbitrary",), vmem_limit_bytes=VMEM_LIMIT),
        name="ln_in_proj",
    )(x2, g, b, w_bf16)


def _split_heads(x, lane):
    zero = jnp.zeros_like(x)
    return jnp.concatenate([jnp.where(lane < HEAD_DIM, x, zero),
                            jnp.where(lane >= HEAD_DIM, x, zero)], axis=0)


def _merge_heads(o2, lane):
    t = o2.shape[0] // 2
    return jnp.where(lane < HEAD_DIM, o2[:t], o2[t:])


def _bits(x):
    return pltpu.bitcast(x, jnp.uint32)


def _sb_masks(valid):
    return jnp.where(valid, LOG2E, 0.0), jnp.where(valid, 0.0, MASK_NEG)


def _sb_scores(qq, k, keep):
    z = _dot_nt(qq, k)
    zl = z * LOG2E
    neg_abs = pltpu.bitcast(_bits(zl) | jnp.uint32(0x80000000), F32)
    c_nat = jnp.maximum(z, 0.0) + jnp.log(1.0 + jnp.exp2(neg_abs))
    cm = c_nat * keep
    hi = pltpu.bitcast(_bits(cm) & jnp.uint32(0xFFFF0000), F32)
    lo = cm - hi
    hilo = jnp.concatenate([hi.astype(BF16), lo.astype(BF16)], axis=1)
    return zl - cm, hilo, cm[:, :LANES]


def _sb_weights(scores, tri2, kill, carried):
    log_beta, hilo, cm0 = scores
    later = _dot(hilo, tri2)
    log_a = (log_beta - later) + kill
    if carried is not None:
        log_a = log_a - jnp.concatenate([carried] * (log_a.shape[1] // LANES), axis=1)
    return jnp.exp2(log_a).astype(BF16), later[:, :LANES] + cm0


def _software_pipeline(n, stages):
    vals = [dict() for _ in stages]
    for step in range(n + len(stages) - 1):
        for s, stage in enumerate(stages):
            i = step - s
            if 0 <= i < n:
                vals[s][i] = stage(i, vals[s - 1].pop(i)) if s else stage(i)
    return vals[-1]


def _sb_kernel(q_ref, k_ref, v_ref, o_ref, tri_ref, mask_ref, acc_ref, car_ref):
    bi = pl.program_id(0)
    qi = pl.program_id(1)
    t, w = ATT_BLOCK, SB_WINDOW
    pairs = q_ref.shape[-1] // LANES

    @pl.when(jnp.logical_and(bi == 0, qi == 0))
    def _fill_tri():
        j = lax.broadcasted_iota(jnp.int32, (2 * w, w), 0) & (w - 1)
        s = lax.broadcasted_iota(jnp.int32, (2 * w, w), 1)
        tri_ref[...] = jnp.where(j > s, 1.0, 0.0).astype(BF16)

    lane = lax.broadcasted_iota(jnp.int32, (t, LANES), 1)
    start0 = pl.multiple_of(jnp.maximum(qi - 1, 0) * t, t)
    lanes_of = lambda p: slice(p * LANES, (p + 1) * LANES)

    def key_positions(start):
        return start + lax.broadcasted_iota(jnp.int32, (2 * t, w), 1)

    @pl.when(qi <= 1)
    def _fill_masks():
        row = lax.broadcasted_iota(jnp.int32, (2 * t, w), 0) & (t - 1)
        mask_ref[0], mask_ref[1] = _sb_masks(key_positions(start0) < qi * t + row)

    def scores(p, start, keep):
        qq = _split_heads(q_ref[0, :, lanes_of(p)], lane)
        return _sb_scores(qq, k_ref[0, pl.ds(start, w), lanes_of(p)], keep)

    def first_weights(p, sc):
        return _sb_weights(sc, tri_ref[...], mask_ref[1], None)

    def first_values(p, aw):
        a, total = aw
        o2 = _dot(a, v_ref[0, pl.ds(start0, w), lanes_of(p)])
        o_ref[0, :, lanes_of(p)] = _merge_heads(o2, lane).astype(o_ref.dtype)
        acc_ref[p] = o2
        car_ref[p] = total
        return jnp.min(total[:, 0:1])

    first = _software_pipeline(
        pairs, [lambda p: scores(p, start0, mask_ref[0]), first_weights, first_values])
    live = functools.reduce(jnp.minimum, [first[p] for p in range(pairs)])

    @pl.when(jnp.logical_and(start0 > 0, live < SB_DEAD_LOG2))
    def _walk_back():
        for p in range(pairs):
            car_ref[p] = jnp.broadcast_to(car_ref[p][:, 0:1], (2 * t, LANES))

        def body(state):
            done_from, _ = state
            start = pl.multiple_of(jnp.maximum(done_from - w, 0), t)
            keep, kill = _sb_masks(key_positions(start) < done_from)

            def weights(p, sc):
                return _sb_weights(sc, tri_ref[...], kill, car_ref[p])

            def values(p, aw):
                a, total = aw
                acc_ref[p] += _dot(a, v_ref[0, pl.ds(start, w), lanes_of(p)])
                carried = car_ref[p] + total[:, 0:1]
                car_ref[p] = carried
                return jnp.min(carried)

            mins = _software_pipeline(
                pairs, [lambda p: scores(p, start, keep), weights, values])
            return start, functools.reduce(jnp.minimum, [mins[p] for p in range(pairs)])

        def cond(state):
            done_from, nxt = state
            return jnp.logical_and(done_from > 0, nxt < SB_DEAD_LOG2)

        lax.while_loop(cond, body, (start0, live))
        for p in range(pairs):
            o_ref[0, :, lanes_of(p)] = _merge_heads(acc_ref[p], lane).astype(o_ref.dtype)


def _sb_attention(q, k, v):
    b, s, wd = q.shape
    pairs = wd // LANES
    nq = s // ATT_BLOCK
    tile = lambda bi, i: (bi, i, 0)
    whole = lambda bi, i: (bi, 0, 0)
    once = pl.Buffered(1)
    return pl.pallas_call(
        _sb_kernel,
        grid=(b, nq),
        in_specs=[
            pl.BlockSpec((1, ATT_BLOCK, wd), tile),
            pl.BlockSpec((1, s, wd), whole, pipeline_mode=once),
            pl.BlockSpec((1, s, wd), whole, pipeline_mode=once),
        ],
        out_specs=pl.BlockSpec((1, ATT_BLOCK, wd), tile),
        out_shape=jax.ShapeDtypeStruct((b, s, wd), BF16),
        scratch_shapes=[
            pltpu.VMEM((2 * SB_WINDOW, SB_WINDOW), BF16),
            pltpu.VMEM((2, 2 * ATT_BLOCK, SB_WINDOW), F32),
            pltpu.VMEM((pairs, 2 * ATT_BLOCK, LANES), F32),
            pltpu.VMEM((pairs, 2 * ATT_BLOCK, LANES), F32),
        ],
        compiler_params=pltpu.CompilerParams(
            dimension_semantics=("arbitrary", "arbitrary"), vmem_limit_bytes=VMEM_LIMIT),
        name="stick_breaking_attention",
    )(q, k, v)


def _t5_causal_bucket(distance):
    exact = REL_BUCKETS // 2
    d = jnp.maximum(distance, 0)
    d_f = jnp.maximum(d, 1).astype(F32)
    large = exact + (jnp.log(d_f / exact) / math.log(REL_MAX_DIST / exact)
                     * (REL_BUCKETS - exact)).astype(jnp.int32)
    large = jnp.minimum(large, REL_BUCKETS - 1)
    return jnp.where(d < exact, d, large)


def _swa_kernel(sinks_ref, relb_ref, q_ref, kvc_ref, kvp_ref, o_ref, bias_ref):
    bi = pl.program_id(0)
    qi = pl.program_id(1)
    t = ATT_BLOCK
    kvw = kvc_ref.shape[-1] // 4
    group = SWA_HEADS // SWA_KV_HEADS

    @pl.when(jnp.logical_and(bi == 0, qi == 0))
    def _fill_bias():
        r = lax.broadcasted_iota(jnp.int32, (t, 2 * t), 0)
        c = lax.broadcasted_iota(jnp.int32, (t, 2 * t), 1)
        dist = r + t - c
        bucket = _t5_causal_bucket(dist)
        valid = jnp.logical_and(dist >= 0, dist < WINDOW)
        for h in range(SWA_HEADS):
            bias = jnp.zeros((t, 2 * t), F32)
            for bk in range(REL_BUCKETS):
                bias = jnp.where(bucket == bk, relb_ref[bk, h], bias)
            bias_ref[h] = jnp.where(valid, bias, MASK_NEG)

    lane = lax.broadcasted_iota(jnp.int32, (t, LANES), 1)
    row2 = lax.broadcasted_iota(jnp.int32, (2 * t, 1), 0)
    col2 = lax.broadcasted_iota(jnp.int32, (2 * t, 2 * t), 1)
    prev_pen = jnp.where(jnp.logical_and(col2 < t, qi == 0), MASK_NEG, 0.0)

    def kv_window(variant):
        cols = slice(variant * kvw, (variant + 1) * kvw)
        return jnp.concatenate([kvp_ref[0, :, cols], kvc_ref[0, :, cols]], axis=0)

    items = [(half, j) for half in range(2) for j in range(SWA_KV_HEADS)]
    heads_of = lambda half, j: [h for h in range(j * group, (j + 1) * group) if h % 2 == half]

    def probabilities(i):
        half, j = items[i]
        heads = heads_of(half, j)
        in_half = (lane >= HEAD_DIM) if half else (lane < HEAD_DIM)
        qq = jnp.concatenate(
            [jnp.where(in_half, q_ref[0, :, (h // 2) * LANES:(h // 2 + 1) * LANES], 0)
             for h in heads], axis=0)
        bias = jnp.concatenate([bias_ref[h] for h in heads], axis=0)
        s = _dot_nt(qq, kv_window(0 + int(j != half))) + bias + prev_pen
        sink = jnp.where(row2 < t, sinks_ref[heads[0]], sinks_ref[heads[1]])
        m = jnp.maximum(jnp.max(s, axis=-1, keepdims=True), sink)
        p = jnp.exp(s - m)
        denom = jnp.sum(p, axis=-1, keepdims=True) + jnp.exp(sink - m)
        return p.astype(BF16), 1.0 / denom

    def values(i, pd):
        half, j = items[i]
        p, inv_denom = pd
        return _dot(p, kv_window(2 + int(j != half))) * inv_denom

    outs = _software_pipeline(len(items), [probabilities, values])
    results = {}
    for i, (half, j) in enumerate(items):
        h0, h1 = heads_of(half, j)
        results[h0], results[h1] = outs[i][:t], outs[i][t:]

    for pair in range(SWA_HEADS // 2):
        o_ref[0, :, pair * LANES:(pair + 1) * LANES] = jnp.where(
            lane < HEAD_DIM, results[2 * pair], results[2 * pair + 1]).astype(o_ref.dtype)


def _swa_attention(q, kv, sinks, rel_bias):
    b, s, wd = q.shape
    nq = s // ATT_BLOCK
    cur = lambda bi, i: (bi, i, 0)
    prev = lambda bi, i: (bi, jnp.maximum(i - 1, 0), 0)
    smem = pl.BlockSpec(memory_space=pltpu.SMEM)
    return pl.pallas_call(
        _swa_kernel,
        grid=(b, nq),
        in_specs=[
            smem, smem,
            pl.BlockSpec((1, ATT_BLOCK, wd), cur),
            pl.BlockSpec((1, ATT_BLOCK, kv.shape[-1]), cur),
            pl.BlockSpec((1, ATT_BLOCK, kv.shape[-1]), prev),
        ],
        out_specs=pl.BlockSpec((1, ATT_BLOCK, wd), cur),
        out_shape=jax.ShapeDtypeStruct((b, s, wd), BF16),
        scratch_shapes=[pltpu.VMEM((SWA_HEADS, ATT_BLOCK, 2 * ATT_BLOCK), F32)],
        compiler_params=pltpu.CompilerParams(
            dimension_semantics=("arbitrary", "arbitrary"), vmem_limit_bytes=VMEM_LIMIT),
        name="sliding_window_attention",
    )(sinks, rel_bias, q, kv, kv)


def _mix_kernel(x_ref, sb_ref, sw_ref, ling_ref, linb_ref, sbg_ref, swg_ref, w_ref,
                l1g_ref, l1b_ref, o_ref):
    h = _layer_norm(x_ref[...], ling_ref[...], linb_ref[...])
    sb = _rms_norm(sb_ref[...].astype(F32), sbg_ref[...]).astype(BF16)
    sw = _rms_norm(sw_ref[...].astype(F32), swg_ref[...]).astype(BF16)
    n_sb = sb.shape[-1]
    mix = _dot(sb, w_ref[:n_sb, :]) + _dot(sw, w_ref[n_sb:, :])
    o_ref[...] = _layer_norm(ALPHA * h + mix, l1g_ref[...], l1b_ref[...])


def _mix_block(x2, sb2, sw2, ln_in_g, ln_in_b, sb_g, sw_g, w_out_bf16, ln1_g, ln1_b):
    n, d = x2.shape
    row = lambda i: (i, 0)
    fixed = lambda i: (0, 0)
    vec = lambda a: pl.BlockSpec((1, a.shape[-1]), fixed)
    return pl.pallas_call(
        _mix_kernel,
        grid=(n // ROW_TILE,),
        in_specs=[
            pl.BlockSpec((ROW_TILE, d), row),
            pl.BlockSpec((ROW_TILE, sb2.shape[-1]), row),
            pl.BlockSpec((ROW_TILE, sw2.shape[-1]), row),
            vec(ln_in_g), vec(ln_in_b), vec(sb_g), vec(sw_g),
            pl.BlockSpec(w_out_bf16.shape, fixed),
            vec(ln1_g), vec(ln1_b),
        ],
        out_specs=pl.BlockSpec((ROW_TILE, d), row),
        out_shape=jax.ShapeDtypeStruct((n, d), F32),
        compiler_params=pltpu.CompilerParams(
            dimension_semantics=("arbitrary",), vmem_limit_bytes=VMEM_LIMIT),
        name="mix_out_proj_ln1",
    )(x2, sb2, sw2, ln_in_g, ln_in_b, sb_g, sw_g, w_out_bf16, ln1_g, ln1_b)


def _ffn_kernel(n_chunks, h_ref, wgu_ref, wd_ref, g_ref, b_ref, o_ref):
    h = h_ref[...]
    hb = h.astype(BF16)
    d_ff = wd_ref.shape[0]
    ck = d_ff // n_chunks
    acc = None
    for c in range(n_chunks):
        gate = _dot(hb, wgu_ref[:, c * ck:(c + 1) * ck])
        up = _dot(hb, wgu_ref[:, d_ff + c * ck:d_ff + (c + 1) * ck])
        act = (gate / (1.0 + jnp.exp(-gate)) * up).astype(BF16)
        part = _dot(act, wd_ref[c * ck:(c + 1) * ck, :])
        acc = part if acc is None else acc + part
    o_ref[...] = _layer_norm(ALPHA * h + acc, g_ref[...], b_ref[...])


def _ffn_block(h1, w_gate_up_bf16, w_down_bf16, ln2_g, ln2_b):
    n, d = h1.shape
    d_ff = w_down_bf16.shape[0]
    n_chunks = 2 if d_ff % (2 * LANES) == 0 else 1
    row = lambda i: (i, 0)
    fixed = lambda i: (0, 0)
    once = pl.Buffered(1)
    return pl.pallas_call(
        functools.partial(_ffn_kernel, n_chunks),
        grid=(n // ROW_TILE,),
        in_specs=[
            pl.BlockSpec((ROW_TILE, d), row),
            pl.BlockSpec(w_gate_up_bf16.shape, fixed, pipeline_mode=once),
            pl.BlockSpec(w_down_bf16.shape, fixed, pipeline_mode=once),
            pl.BlockSpec((1, d), fixed),
            pl.BlockSpec((1, d), fixed),
        ],
        out_specs=pl.BlockSpec((ROW_TILE, d), row),
        out_shape=jax.ShapeDtypeStruct((n, d), F32),
        compiler_params=pltpu.CompilerParams(
            dimension_semantics=("arbitrary",), vmem_limit_bytes=VMEM_LIMIT),
        name="swiglu_ffn_ln2",
    )(h1, w_gate_up_bf16, w_down_bf16, ln2_g, ln2_b)


def kernel(x, ln_in_g, ln_in_b, w_in, sb_norm_g, swa_norm_g, sinks, rel_bias, w_out,
           ln1_g, ln1_b, w_gate_up, w_down, ln2_g, ln2_b):
    b, s, d = x.shape
    assert w_in.shape[0] == DEPTH == 1
    sb_w = SB_HEADS * HEAD_DIM
    sw_w = SWA_HEADS * HEAD_DIM
    kv_w = SWA_KV_HEADS * HEAD_DIM
    assert w_in.shape[-1] == 3 * sb_w + sw_w + 2 * kv_w
    assert kv_w == LANES and SWA_HEADS // SWA_KV_HEADS == 4
    assert s % SB_WINDOW == 0 and (b * s) % ROW_TILE == 0 and WINDOW <= ATT_BLOCK

    x2 = x.reshape(b * s, d)
    row_vec = lambda a: a.reshape(1, -1)
    ling, linb = row_vec(ln_in_g), row_vec(ln_in_b)

    names = ("q_sb", "k_sb", "v_sb", "q_sw", "k_sw", "v_sw")
    widths = (sb_w, sb_w, sb_w, sw_w, kv_w, kv_w)
    splits, lo = {}, 0
    for name, wd in zip(names, widths):
        splits[name] = (lo, lo + wd)
        lo += wd
    q_sb, k_sb, v_sb, q_sw, kv_sw = _in_projection(x2, ling, linb, w_in[0].astype(BF16), splits)

    to3 = lambda a: a.reshape(b, s, a.shape[-1])
    sb_out = _sb_attention(to3(q_sb), to3(k_sb), to3(v_sb))
    sw_out = _swa_attention(to3(q_sw), to3(kv_sw), sinks[0], rel_bias)

    h1 = _mix_block(x2, sb_out.reshape(b * s, sb_w), sw_out.reshape(b * s, sw_w),
                    ling, linb, row_vec(sb_norm_g[0]), row_vec(swa_norm_g[0]),
                    w_out[0].astype(BF16), row_vec(ln1_g[0]), row_vec(ln1_b[0]))
    out = _ffn_block(h1, w_gate_up[0].astype(BF16), w_down[0].astype(BF16),
                     row_vec(ln2_g[0]), row_vec(ln2_b[0]))
    return out.reshape(b, s, d)
```

```python
import functools
import math

import jax
import jax.numpy as jnp
from jax import lax
from jax.experimental import pallas as pl
from jax.experimental.pallas import tpu as pltpu

F32 = jnp.float32
BF16 = jnp.bfloat16

HEAD_DIM = 64
SB_HEADS = 8
SWA_HEADS = 8
SWA_KV_HEADS = 2
WINDOW = 128
REL_BUCKETS = 32
REL_MAX_DIST = 128
LN_EPS = 1e-5
RMS_EPS = 1e-6
DEPTH = 1
ALPHA = (2 * DEPTH) ** 0.25
QK_SCALE = HEAD_DIM ** -0.5
LOG2E = math.log2(math.e)

LANES = 128
ATT_BLOCK = 128
SB_WINDOW = 256
SB_Q_TILE = 64
ROW_TILE = 512
VMEM_LIMIT = 56 * 1024 * 1024

SB_DEAD_LOG2 = 127.0
MASK_NEG = -1e30


def _layer_norm(x, g, b):
    mu = jnp.mean(x, axis=-1, keepdims=True)
    xc = x - mu
    var = jnp.mean(xc * xc, axis=-1, keepdims=True)
    return xc * lax.rsqrt(var + LN_EPS) * g + b


def _rms_norm(x, g):
    ms = jnp.mean(x * x, axis=-1, keepdims=True)
    return x * lax.rsqrt(ms + RMS_EPS) * g


def _dot(a, b):
    return jnp.dot(a, b, preferred_element_type=F32)


def _dot_nt(a, b):
    return lax.dot_general(a, b, (((1,), (1,)), ((), ())), preferred_element_type=F32)


def _inproj_kernel(splits, x_ref, g_ref, b_ref, w_ref,
                   h_ref, qsb_ref, ksb_ref, vsb_ref, qsw_ref, kv_ref):
    h32 = _layer_norm(x_ref[...], g_ref[...], b_ref[...])
    h_ref[...] = h32
    h = h32.astype(BF16)
    proj = lambda lo, hi: _dot(h, w_ref[:, lo:hi])
    qsb_ref[...] = (proj(*splits["q_sb"]) * QK_SCALE).astype(BF16)
    ksb_ref[...] = proj(*splits["k_sb"]).astype(BF16)
    vsb_ref[...] = proj(*splits["v_sb"]).astype(BF16)
    qsw_ref[...] = (proj(*splits["q_sw"]) * QK_SCALE).astype(BF16)
    assert splits["k_sw"][1] == splits["v_sw"][0]
    kv = proj(splits["k_sw"][0], splits["v_sw"][1])
    kvw = kv.shape[-1] // 2
    k, v = kv[:, :kvw], kv[:, kvw:]
    kv_ref[:, 0 * kvw:1 * kvw] = k.astype(BF16)
    kv_ref[:, 1 * kvw:2 * kvw] = pltpu.roll(k, HEAD_DIM, axis=1).astype(BF16)
    kv_ref[:, 2 * kvw:3 * kvw] = v.astype(BF16)
    kv_ref[:, 3 * kvw:4 * kvw] = pltpu.roll(v, HEAD_DIM, axis=1).astype(BF16)


def _in_projection(x2, g, b, w_bf16, splits):
    n, d = x2.shape
    width = lambda name: splits[name][1] - splits[name][0]
    out_w = [width("q_sb"), width("k_sb"), width("v_sb"), width("q_sw"), 4 * width("k_sw")]
    row = lambda i: (i, 0)
    fixed = lambda i: (0, 0)
    return pl.pallas_call(
        functools.partial(_inproj_kernel, splits),
        grid=(n // ROW_TILE,),
        in_specs=[
            pl.BlockSpec((ROW_TILE, d), row),
            pl.BlockSpec((1, d), fixed),
            pl.BlockSpec((1, d), fixed),
            pl.BlockSpec(w_bf16.shape, fixed),
        ],
        out_specs=[pl.BlockSpec((ROW_TILE, d), row)]
        + [pl.BlockSpec((ROW_TILE, w), row) for w in out_w],
        out_shape=[jax.ShapeDtypeStruct((n, d), F32)]
        + [jax.ShapeDtypeStruct((n, w), BF16) for w in out_w],
        compiler_params=pltpu.CompilerParams(
            dimension_semantics=("arbitrary",), vmem_limit_bytes=VMEM_LIMIT),
        name="ln_in_proj",
    )(x2, g, b, w_bf16)


def _split_heads(x, lane):
    zero = jnp.zeros_like(x)
    return jnp.concatenate([jnp.where(lane < HEAD_DIM, x, zero),
                            jnp.where(lane >= HEAD_DIM, x, zero)], axis=0)


def _merge_heads(o2, lane):
    t = o2.shape[0] // 2
    return jnp.where(lane < HEAD_DIM, o2[:t], o2[t:])


def _bits(x):
    return pltpu.bitcast(x, jnp.uint32)


def _sb_masks(valid):
    return jnp.where(valid, LOG2E, 0.0), jnp.where(valid, 0.0, MASK_NEG)


def _sb_scores(qq, k, keep):
    z = _dot_nt(qq, k)
    zl = z * LOG2E
    neg_abs = pltpu.bitcast(_bits(zl) | jnp.uint32(0x80000000), F32)
    c_nat = jnp.maximum(z, 0.0) + jnp.log(1.0 + jnp.exp2(neg_abs))
    cm = c_nat * keep
    hi = pltpu.bitcast(_bits(cm) & jnp.uint32(0xFFFF0000), F32)
    lo = cm - hi
    return zl - cm, hi.astype(BF16), lo.astype(BF16), cm[:, :LANES]


def _sb_weights(scores, tri, kills, carrieds):
    sums = _dot(jnp.concatenate([x for sc in scores for x in sc[1:3]], axis=0), tri)
    rows = scores[0][0].shape[0]
    out = []
    for n, ((log_beta, _, _, cm0), kill, carried) in enumerate(zip(scores, kills, carrieds)):
        later = sums[2 * n * rows:(2 * n + 1) * rows] + sums[(2 * n + 1) * rows:(2 * n + 2) * rows]
        log_a = (log_beta - later) + kill
        if carried is not None:
            log_a = log_a - jnp.concatenate([carried] * (log_a.shape[1] // LANES), axis=1)
        out.append((jnp.exp2(log_a).astype(BF16), later[:, :LANES] + cm0))
    return out


def _software_pipeline(n, stages):
    vals = [dict() for _ in stages]
    for step in range(n + len(stages) - 1):
        for s, stage in enumerate(stages):
            i = step - s
            if 0 <= i < n:
                vals[s][i] = stage(i, vals[s - 1].pop(i)) if s else stage(i)
    return vals[-1]


def _sb_kernel(q_ref, k_ref, v_ref, o_ref, tri_ref, mask_ref, acc_ref, car_ref):
    bi = pl.program_id(0)
    qi = pl.program_id(1)
    tq, w = SB_Q_TILE, SB_WINDOW
    subs = q_ref.shape[1] // tq
    pairs = q_ref.shape[-1] // LANES
    row0 = qi * (subs * tq)

    @pl.when(jnp.logical_and(bi == 0, qi == 0))
    def _fill_tri():
        j = lax.broadcasted_iota(jnp.int32, (w, w), 0)
        s = lax.broadcasted_iota(jnp.int32, (w, w), 1)
        tri_ref[...] = jnp.where(j > s, 1.0, 0.0).astype(BF16)

    lane = lax.broadcasted_iota(jnp.int32, (tq, LANES), 1)
    lanes_of = lambda p: slice(p * LANES, (p + 1) * LANES)
    rows_of = lambda h: slice(h * tq, (h + 1) * tq)
    first_start = lambda h: pl.multiple_of(jnp.maximum(row0 + (h + 1) * tq - w, 0), tq)

    def key_positions(start):
        return start + lax.broadcasted_iota(jnp.int32, (2 * tq, w), 1)

    @pl.when(row0 <= w)
    def _fill_masks():
        row = lax.broadcasted_iota(jnp.int32, (2 * tq, w), 0) & (tq - 1)
        for h in range(subs):
            mask_ref[h, 0], mask_ref[h, 1] = _sb_masks(
                key_positions(first_start(h)) < row0 + h * tq + row)

    def scores(p, starts, keeps):
        return [_sb_scores(_split_heads(q_ref[0, rows_of(h), lanes_of(p)], lane),
                           k_ref[0, pl.ds(starts[h], w), lanes_of(p)], keeps[h])
                for h in range(subs)]

    first_starts = [first_start(h) for h in range(subs)]

    def first_scores(p):
        return scores(p, first_starts, [mask_ref[h, 0] for h in range(subs)])

    def first_weights(p, sc):
        return _sb_weights(sc, tri_ref[...], [mask_ref[h, 1] for h in range(subs)], [None] * subs)

    def first_values(p, aws):
        least = None
        for h, (a, total) in enumerate(aws):
            o2 = _dot(a, v_ref[0, pl.ds(first_starts[h], w), lanes_of(p)])
            o_ref[0, rows_of(h), lanes_of(p)] = _merge_heads(o2, lane).astype(o_ref.dtype)
            acc_ref[p, h] = o2
            car_ref[p, h] = total
            m = jnp.min(total[:, 0:1])
            least = m if least is None else jnp.minimum(least, m)
        return least

    first = _software_pipeline(pairs, [first_scores, first_weights, first_values])
    live = functools.reduce(jnp.minimum, [first[p] for p in range(pairs)])
    last_start = first_starts[-1]

    @pl.when(jnp.logical_and(last_start > 0, live < SB_DEAD_LOG2))
    def _walk_back():
        for p in range(pairs):
            for h in range(subs):
                car_ref[p, h] = jnp.broadcast_to(car_ref[p, h][:, 0:1], (2 * tq, LANES))

        def body(state):
            step, _ = state
            done = [jnp.maximum(first_start(h) - step * w, 0) for h in range(subs)]
            starts = [pl.multiple_of(jnp.maximum(d - w, 0), tq) for d in done]
            masks = [_sb_masks(key_positions(s) < d) for s, d in zip(starts, done)]

            def more_scores(p):
                return scores(p, starts, [m[0] for m in masks])

            def weights(p, sc):
                return _sb_weights(sc, tri_ref[...], [m[1] for m in masks],
                                   [car_ref[p, h] for h in range(subs)])

            def values(p, aws):
                least = None
                for h, (a, total) in enumerate(aws):
                    acc_ref[p, h] += _dot(a, v_ref[0, pl.ds(starts[h], w), lanes_of(p)])
                    carried = car_ref[p, h] + total[:, 0:1]
                    car_ref[p, h] = carried
                    m = jnp.min(carried)
                    least = m if least is None else jnp.minimum(least, m)
                return least

            mins = _software_pipeline(pairs, [more_scores, weights, values])
            return step + 1, functools.reduce(jnp.minimum, [mins[p] for p in range(pairs)])

        def cond(state):
            step, nxt = state
            return jnp.logical_and(last_start - step * w > 0, nxt < SB_DEAD_LOG2)

        lax.while_loop(cond, body, (jnp.int32(0), live))
        for p in range(pairs):
            for h in range(subs):
                o_ref[0, rows_of(h), lanes_of(p)] = _merge_heads(
                    acc_ref[p, h], lane).astype(o_ref.dtype)


def _sb_attention(q, k, v):
    b, s, wd = q.shape
    pairs = wd // LANES
    subs = ATT_BLOCK // SB_Q_TILE
    nq = s // ATT_BLOCK
    tile = lambda bi, i: (bi, i, 0)
    whole = lambda bi, i: (bi, 0, 0)
    once = pl.Buffered(1)
    return pl.pallas_call(
        _sb_kernel,
        grid=(b, nq),
        in_specs=[
            pl.BlockSpec((1, ATT_BLOCK, wd), tile),
            pl.BlockSpec((1, s, wd), whole, pipeline_mode=once),
            pl.BlockSpec((1, s, wd), whole, pipeline_mode=once),
        ],
        out_specs=pl.BlockSpec((1, ATT_BLOCK, wd), tile),
        out_shape=jax.ShapeDtypeStruct((b, s, wd), BF16),
        scratch_shapes=[
            pltpu.VMEM((SB_WINDOW, SB_WINDOW), BF16),
            pltpu.VMEM((subs, 2, 2 * SB_Q_TILE, SB_WINDOW), F32),
            pltpu.VMEM((pairs, subs, 2 * SB_Q_TILE, LANES), F32),
            pltpu.VMEM((pairs, subs, 2 * SB_Q_TILE, LANES), F32),
        ],
        compiler_params=pltpu.CompilerParams(
            dimension_semantics=("arbitrary", "arbitrary"), vmem_limit_bytes=VMEM_LIMIT),
        name="stick_breaking_attention",
    )(q, k, v)


def _t5_causal_bucket(distance):
    exact = REL_BUCKETS // 2
    d = jnp.maximum(distance, 0)
    d_f = jnp.maximum(d, 1).astype(F32)
    large = exact + (jnp.log(d_f / exact) / math.log(REL_MAX_DIST / exact)
                     * (REL_BUCKETS - exact)).astype(jnp.int32)
    large = jnp.minimum(large, REL_BUCKETS - 1)
    return jnp.where(d < exact, d, large)


def _swa_kernel(sinks_ref, relb_ref, q_ref, kvc_ref, kvp_ref, o_ref, bias_ref):
    bi = pl.program_id(0)
    qi = pl.program_id(1)
    t = ATT_BLOCK
    kvw = kvc_ref.shape[-1] // 4
    group = SWA_HEADS // SWA_KV_HEADS

    @pl.when(jnp.logical_and(bi == 0, qi == 0))
    def _fill_bias():
        r = lax.broadcasted_iota(jnp.int32, (t, 2 * t), 0)
        c = lax.broadcasted_iota(jnp.int32, (t, 2 * t), 1)
        dist = r + t - c
        bucket = _t5_causal_bucket(dist)
        valid = jnp.logical_and(dist >= 0, dist < WINDOW)
        for h in range(SWA_HEADS):
            bias = jnp.zeros((t, 2 * t), F32)
            for bk in range(REL_BUCKETS):
                bias = jnp.where(bucket == bk, relb_ref[bk, h], bias)
            bias_ref[h] = jnp.where(valid, bias, MASK_NEG)

    lane = lax.broadcasted_iota(jnp.int32, (t, LANES), 1)
    row2 = lax.broadcasted_iota(jnp.int32, (2 * t, 1), 0)
    col2 = lax.broadcasted_iota(jnp.int32, (2 * t, 2 * t), 1)
    prev_pen = jnp.where(jnp.logical_and(col2 < t, qi == 0), MASK_NEG, 0.0)

    def kv_window(variant):
        cols = slice(variant * kvw, (variant + 1) * kvw)
        return jnp.concatenate([kvp_ref[0, :, cols], kvc_ref[0, :, cols]], axis=0)

    items = [(half, j) for half in range(2) for j in range(SWA_KV_HEADS)]
    heads_of = lambda half, j: [h for h in range(j * group, (j + 1) * group) if h % 2 == half]

    def probabilities(i):
        half, j = items[i]
        heads = heads_of(half, j)
        in_half = (lane >= HEAD_DIM) if half else (lane < HEAD_DIM)
        qq = jnp.concatenate(
            [jnp.where(in_half, q_ref[0, :, (h // 2) * LANES:(h // 2 + 1) * LANES], 0)
             for h in heads], axis=0)
        bias = jnp.concatenate([bias_ref[h] for h in heads], axis=0)
        s = _dot_nt(qq, kv_window(0 + int(j != half))) + bias + prev_pen
        sink = jnp.where(row2 < t, sinks_ref[heads[0]], sinks_ref[heads[1]])
        m = jnp.maximum(jnp.max(s, axis=-1, keepdims=True), sink)
        p = jnp.exp(s - m)
        denom = jnp.sum(p, axis=-1, keepdims=True) + jnp.exp(sink - m)
        return p.astype(BF16), 1.0 / denom

    def values(i, pd):
        half, j = items[i]
        p, inv_denom = pd
        return _dot(p, kv_window(2 + int(j != half))) * inv_denom

    outs = _software_pipeline(len(items), [probabilities, values])
    results = {}
    for i, (half, j) in enumerate(items):
        h0, h1 = heads_of(half, j)
        results[h0], results[h1] = outs[i][:t], outs[i][t:]

    for pair in range(SWA_HEADS // 2):
        o_ref[0, :, pair * LANES:(pair + 1) * LANES] = jnp.where(
            lane < HEAD_DIM, results[2 * pair], results[2 * pair + 1]).astype(o_ref.dtype)


def _swa_attention(q, kv, sinks, rel_bias):
    b, s, wd = q.shape
    nq = s // ATT_BLOCK
    cur = lambda bi, i: (bi, i, 0)
    prev = lambda bi, i: (bi, jnp.maximum(i - 1, 0), 0)
    smem = pl.BlockSpec(memory_space=pltpu.SMEM)
    return pl.pallas_call(
        _swa_kernel,
        grid=(b, nq),
        in_specs=[
            smem, smem,
            pl.BlockSpec((1, ATT_BLOCK, wd), cur),
            pl.BlockSpec((1, ATT_BLOCK, kv.shape[-1]), cur),
            pl.BlockSpec((1, ATT_BLOCK, kv.shape[-1]), prev),
        ],
        out_specs=pl.BlockSpec((1, ATT_BLOCK, wd), cur),
        out_shape=jax.ShapeDtypeStruct((b, s, wd), BF16),
        scratch_shapes=[pltpu.VMEM((SWA_HEADS, ATT_BLOCK, 2 * ATT_BLOCK), F32)],
        compiler_params=pltpu.CompilerParams(
            dimension_semantics=("arbitrary", "arbitrary"), vmem_limit_bytes=VMEM_LIMIT),
        name="sliding_window_attention",
    )(sinks, rel_bias, q, kv, kv)


def _mix_kernel(h_ref, sb_ref, sw_ref, sbg_ref, swg_ref, w_ref, l1g_ref, l1b_ref, o_ref):
    parts = 2
    rows = h_ref.shape[0] // parts
    part = lambda i: slice(i * rows, (i + 1) * rows)

    def merged(i):
        sb = _rms_norm(sb_ref[part(i), :].astype(F32), sbg_ref[...]).astype(BF16)
        sw = _rms_norm(sw_ref[part(i), :].astype(F32), swg_ref[...]).astype(BF16)
        return jnp.concatenate([sb, sw], axis=1)

    def projected(i, m):
        return _dot(m, w_ref[...])

    def finished(i, mix):
        o_ref[part(i), :] = _layer_norm(ALPHA * h_ref[part(i), :] + mix, l1g_ref[...], l1b_ref[...])

    _software_pipeline(parts, [merged, projected, finished])


def _mix_block(h, sb2, sw2, sb_g, sw_g, w_out_bf16, ln1_g, ln1_b):
    n, d = h.shape
    row = lambda i: (i, 0)
    fixed = lambda i: (0, 0)
    vec = lambda a: pl.BlockSpec((1, a.shape[-1]), fixed)
    return pl.pallas_call(
        _mix_kernel,
        grid=(n // ROW_TILE,),
        in_specs=[
            pl.BlockSpec((ROW_TILE, d), row),
            pl.BlockSpec((ROW_TILE, sb2.shape[-1]), row),
            pl.BlockSpec((ROW_TILE, sw2.shape[-1]), row),
            vec(sb_g), vec(sw_g),
            pl.BlockSpec(w_out_bf16.shape, fixed),
            vec(ln1_g), vec(ln1_b),
        ],
        out_specs=pl.BlockSpec((ROW_TILE, d), row),
        out_shape=jax.ShapeDtypeStruct((n, d), F32),
        compiler_params=pltpu.CompilerParams(
            dimension_semantics=("arbitrary",), vmem_limit_bytes=VMEM_LIMIT),
        name="mix_out_proj_ln1",
    )(h, sb2, sw2, sb_g, sw_g, w_out_bf16, ln1_g, ln1_b)


def _ffn_kernel(n_chunks, h_ref, wgu_ref, wd_ref, g_ref, b_ref, o_ref):
    h = h_ref[...]
    hb = h.astype(BF16)
    d_ff = wd_ref.shape[0]
    ck = d_ff // n_chunks
    acc = None
    for c in range(n_chunks):
        gate = _dot(hb, wgu_ref[:, c * ck:(c + 1) * ck])
        up = _dot(hb, wgu_ref[:, d_ff + c * ck:d_ff + (c + 1) * ck])
        act = (gate / (1.0 + jnp.exp(-gate)) * up).astype(BF16)
        part = _dot(act, wd_ref[c * ck:(c + 1) * ck, :])
        acc = part if acc is None else acc + part
    o_ref[...] = _layer_norm(ALPHA * h + acc, g_ref[...], b_ref[...])


def _ffn_block(h1, w_gate_up_bf16, w_down_bf16, ln2_g, ln2_b):
    n, d = h1.shape
    d_ff = w_down_bf16.shape[0]
    n_chunks = 2 if d_ff % (2 * LANES) == 0 else 1
    row = lambda i: (i, 0)
    fixed = lambda i: (0, 0)
    once = pl.Buffered(1)
    return pl.pallas_call(
        functools.partial(_ffn_kernel, n_chunks),
        grid=(n // ROW_TILE,),
        in_specs=[
            pl.BlockSpec((ROW_TILE, d), row),
            pl.BlockSpec(w_gate_up_bf16.shape, fixed, pipeline_mode=once),
            pl.BlockSpec(w_down_bf16.shape, fixed, pipeline_mode=once),
            pl.BlockSpec((1, d), fixed),
            pl.BlockSpec((1, d), fixed),
        ],
        out_specs=pl.BlockSpec((ROW_TILE, d), row),
        out_shape=jax.ShapeDtypeStruct((n, d), F32),
        compiler_params=pltpu.CompilerParams(
            dimension_semantics=("arbitrary",), vmem_limit_bytes=VMEM_LIMIT),
        name="swiglu_ffn_ln2",
    )(h1, w_gate_up_bf16, w_down_bf16, ln2_g, ln2_b)


def kernel(x, ln_in_g, ln_in_b, w_in, sb_norm_g, swa_norm_g, sinks, rel_bias, w_out,
           ln1_g, ln1_b, w_gate_up, w_down, ln2_g, ln2_b):
    b, s, d = x.shape
    assert w_in.shape[0] == DEPTH == 1
    sb_w = SB_HEADS * HEAD_DIM
    sw_w = SWA_HEADS * HEAD_DIM
    kv_w = SWA_KV_HEADS * HEAD_DIM
    assert w_in.shape[-1] == 3 * sb_w + sw_w + 2 * kv_w
    assert kv_w == LANES and SWA_HEADS // SWA_KV_HEADS == 4
    assert s % SB_WINDOW == 0 and (b * s) % ROW_TILE == 0 and WINDOW <= ATT_BLOCK

    x2 = x.reshape(b * s, d)
    row_vec = lambda a: a.reshape(1, -1)

    names = ("q_sb", "k_sb", "v_sb", "q_sw", "k_sw", "v_sw")
    widths = (sb_w, sb_w, sb_w, sw_w, kv_w, kv_w)
    splits, lo = {}, 0
    for name, wd in zip(names, widths):
        splits[name] = (lo, lo + wd)
        lo += wd
    h, q_sb, k_sb, v_sb, q_sw, kv_sw = _in_projection(
        x2, row_vec(ln_in_g), row_vec(ln_in_b), w_in[0].astype(BF16), splits)

    to3 = lambda a: a.reshape(b, s, a.shape[-1])
    sb_out = _sb_attention(to3(q_sb), to3(k_sb), to3(v_sb))
    sw_out = _swa_attention(to3(q_sw), to3(kv_sw), sinks[0], rel_bias)

    h1 = _mix_block(h, sb_out.reshape(b * s, sb_w), sw_out.reshape(b * s, sw_w),
                    row_vec(sb_norm_g[0]), row_vec(swa_norm_g[0]),
                    w_out[0].astype(BF16), row_vec(ln1_g[0]), row_vec(ln1_b[0]))
    out = _ffn_block(h1, w_gate_up[0].astype(BF16), w_down[0].astype(BF16),
                     row_vec(ln2_g[0]), row_vec(ln2_b[0]))
    return out.reshape(b, s, d)
```

```python
import functools
import math

import jax
import jax.numpy as jnp
from jax import lax
from jax.experimental import pallas as pl
from jax.experimental.pallas import tpu as pltpu

F32 = jnp.float32
BF16 = jnp.bfloat16

HEAD_DIM = 64
SB_HEADS = 8
SWA_HEADS = 8
SWA_KV_HEADS = 2
WINDOW = 128
REL_BUCKETS = 32
REL_MAX_DIST = 128
LN_EPS = 1e-5
RMS_EPS = 1e-6
DEPTH = 1
ALPHA = (2 * DEPTH) ** 0.25
QK_SCALE = HEAD_DIM ** -0.5
LOG2E = math.log2(math.e)

LANES = 128
ATT_BLOCK = 128
SWA_STEP_ROWS = 256
SB_WINDOW = 256
SB_Q_TILE = 64
SB_STEP_ROWS = 256
ROW_TILE = 1024
PROJ_PARTS = 2
MIX_PART_ROWS = 256
FFN_ROW_TILE = 1024
FFN_PARTS = 2
FFN_CHUNK = 1024
VMEM_LIMIT = 56 * 1024 * 1024

SB_DEAD_LOG2 = 127.0
MASK_NEG = -1e30


def _layer_norm(x, g, b):
    mu = jnp.mean(x, axis=-1, keepdims=True)
    xc = x - mu
    var = jnp.mean(xc * xc, axis=-1, keepdims=True)
    return xc * lax.rsqrt(var + LN_EPS) * g + b


def _rms_norm(x, g):
    ms = jnp.mean(x * x, axis=-1, keepdims=True)
    return x * lax.rsqrt(ms + RMS_EPS) * g


def _dot(a, b):
    return jnp.dot(a, b, preferred_element_type=F32)


def _dot_nt(a, b):
    return lax.dot_general(a, b, (((1,), (1,)), ((), ())), preferred_element_type=F32)


def _inproj_kernel(splits, x_ref, g_ref, b_ref, w_ref,
                   h_ref, qsb_ref, ksb_ref, vsb_ref, qsw_ref, kv_ref):
    rows = x_ref.shape[0] // PROJ_PARTS
    part = lambda i: slice(i * rows, (i + 1) * rows)
    assert splits["k_sw"][1] == splits["v_sw"][0]

    def normed(i):
        h32 = _layer_norm(x_ref[part(i), :], g_ref[...], b_ref[...])
        h_ref[part(i), :] = h32
        return h32.astype(BF16)

    def projected(i, h):
        r = part(i)
        proj = lambda lo, hi: _dot(h, w_ref[:, lo:hi])
        qsb_ref[r, :] = (proj(*splits["q_sb"]) * QK_SCALE).astype(BF16)
        ksb_ref[r, :] = proj(*splits["k_sb"]).astype(BF16)
        vsb_ref[r, :] = proj(*splits["v_sb"]).astype(BF16)
        qsw_ref[r, :] = (proj(*splits["q_sw"]) * QK_SCALE).astype(BF16)
        kv = proj(splits["k_sw"][0], splits["v_sw"][1])
        kvw = kv.shape[-1] // 2
        k, v = kv[:, :kvw], kv[:, kvw:]
        kv_ref[r, 0 * kvw:1 * kvw] = k.astype(BF16)
        kv_ref[r, 1 * kvw:2 * kvw] = pltpu.roll(k, HEAD_DIM, axis=1).astype(BF16)
        kv_ref[r, 2 * kvw:3 * kvw] = v.astype(BF16)
        kv_ref[r, 3 * kvw:4 * kvw] = pltpu.roll(v, HEAD_DIM, axis=1).astype(BF16)

    _software_pipeline(PROJ_PARTS, [normed, projected])


def _in_projection(x2, g, b, w_bf16, splits):
    n, d = x2.shape
    width = lambda name: splits[name][1] - splits[name][0]
    out_w = [width("q_sb"), width("k_sb"), width("v_sb"), width("q_sw"), 4 * width("k_sw")]
    row = lambda i: (i, 0)
    fixed = lambda i: (0, 0)
    return pl.pallas_call(
        functools.partial(_inproj_kernel, splits),
        grid=(n // ROW_TILE,),
        in_specs=[
            pl.BlockSpec((ROW_TILE, d), row),
            pl.BlockSpec((1, d), fixed),
            pl.BlockSpec((1, d), fixed),
            pl.BlockSpec(w_bf16.shape, fixed),
        ],
        out_specs=[pl.BlockSpec((ROW_TILE, d), row)]
        + [pl.BlockSpec((ROW_TILE, w), row) for w in out_w],
        out_shape=[jax.ShapeDtypeStruct((n, d), F32)]
        + [jax.ShapeDtypeStruct((n, w), BF16) for w in out_w],
        compiler_params=pltpu.CompilerParams(
            dimension_semantics=("arbitrary",), vmem_limit_bytes=VMEM_LIMIT),
        name="ln_in_proj",
    )(x2, g, b, w_bf16)


def _split_heads(x, lane):
    zero = jnp.zeros_like(x)
    return jnp.concatenate([jnp.where(lane < HEAD_DIM, x, zero),
                            jnp.where(lane >= HEAD_DIM, x, zero)], axis=0)


def _merge_heads(o2, lane):
    t = o2.shape[0] // 2
    return jnp.where(lane < HEAD_DIM, o2[:t], o2[t:])


def _bits(x):
    return pltpu.bitcast(x, jnp.uint32)


def _sb_masks(valid):
    return jnp.where(valid, LOG2E, 0.0), jnp.where(valid, 0.0, MASK_NEG)


def _sb_scores(qq, k, keep):
    z = _dot_nt(qq, k)
    zl = z * LOG2E
    neg_abs = pltpu.bitcast(_bits(zl) | jnp.uint32(0x80000000), F32)
    c_nat = jnp.maximum(z, 0.0) + jnp.log(1.0 + jnp.exp2(neg_abs))
    cm = c_nat * keep
    hi = pltpu.bitcast(_bits(cm) & jnp.uint32(0xFFFF0000), F32)
    lo = cm - hi
    return zl - cm, hi.astype(BF16), lo.astype(BF16), cm[:, :LANES]


def _sb_weights(scores, tri, kills, carrieds):
    sums = _dot(jnp.concatenate([x for sc in scores for x in sc[1:3]], axis=0), tri)
    rows = scores[0][0].shape[0]
    out = []
    for n, ((log_beta, _, _, cm0), kill, carried) in enumerate(zip(scores, kills, carrieds)):
        later = sums[2 * n * rows:(2 * n + 1) * rows] + sums[(2 * n + 1) * rows:(2 * n + 2) * rows]
        log_a = (log_beta - later) + kill
        if carried is not None:
            log_a = log_a - jnp.concatenate([carried] * (log_a.shape[1] // LANES), axis=1)
        out.append((jnp.exp2(log_a).astype(BF16), later[:, :LANES] + cm0))
    return out


def _software_pipeline(n, stages):
    vals = [dict() for _ in stages]
    for step in range(n + len(stages) - 1):
        for s, stage in enumerate(stages):
            i = step - s
            if 0 <= i < n:
                vals[s][i] = stage(i, vals[s - 1].pop(i)) if s else stage(i)
    return vals[-1]


def _sb_kernel(q_ref, k_ref, v_ref, o_ref, tri_ref, mask_ref, acc_ref, car_ref):
    bi = pl.program_id(0)
    qi = pl.program_id(1)
    tq, w = SB_Q_TILE, SB_WINDOW
    subs = q_ref.shape[1] // tq
    pairs = q_ref.shape[-1] // LANES
    row0 = qi * (subs * tq)

    @pl.when(jnp.logical_and(bi == 0, qi == 0))
    def _fill_tri():
        j = lax.broadcasted_iota(jnp.int32, (w, w), 0)
        s = lax.broadcasted_iota(jnp.int32, (w, w), 1)
        tri_ref[...] = jnp.where(j > s, 1.0, 0.0).astype(BF16)

    lane = lax.broadcasted_iota(jnp.int32, (tq, LANES), 1)
    lanes_of = lambda p: slice(p * LANES, (p + 1) * LANES)
    rows_of = lambda h: slice(h * tq, (h + 1) * tq)
    first_start = lambda h: pl.multiple_of(jnp.maximum(row0 + (h + 1) * tq - w, 0), tq)

    def key_positions(start):
        return start + lax.broadcasted_iota(jnp.int32, (2 * tq, w), 1)

    @pl.when(row0 <= w)
    def _fill_masks():
        row = lax.broadcasted_iota(jnp.int32, (2 * tq, w), 0) & (tq - 1)
        for h in range(subs):
            mask_ref[h, 0], mask_ref[h, 1] = _sb_masks(
                key_positions(first_start(h)) < row0 + h * tq + row)

    def scores(p, starts, keeps):
        return [_sb_scores(_split_heads(q_ref[0, rows_of(h), lanes_of(p)], lane),
                           k_ref[0, pl.ds(starts[h], w), lanes_of(p)], keeps[h])
                for h in range(subs)]

    first_starts = [first_start(h) for h in range(subs)]

    def first_scores(p):
        return scores(p, first_starts, [mask_ref[h, 0] for h in range(subs)])

    def first_weights(p, sc):
        return _sb_weights(sc, tri_ref[...], [mask_ref[h, 1] for h in range(subs)], [None] * subs)

    def first_values(p, aws):
        least = None
        for h, (a, total) in enumerate(aws):
            o2 = _dot(a, v_ref[0, pl.ds(first_starts[h], w), lanes_of(p)])
            o_ref[0, rows_of(h), lanes_of(p)] = _merge_heads(o2, lane).astype(o_ref.dtype)
            acc_ref[p, h] = o2
            car_ref[p, h] = total
            m = jnp.min(total[:, 0:1])
            least = m if least is None else jnp.minimum(least, m)
        return least

    first = _software_pipeline(pairs, [first_scores, first_weights, first_values])
    live = functools.reduce(jnp.minimum, [first[p] for p in range(pairs)])
    last_start = first_starts[-1]

    @pl.when(jnp.logical_and(last_start > 0, live < SB_DEAD_LOG2))
    def _walk_back():
        for p in range(pairs):
            for h in range(subs):
                car_ref[p, h] = jnp.broadcast_to(car_ref[p, h][:, 0:1], (2 * tq, LANES))

        def body(state):
            step, _ = state
            done = [jnp.maximum(first_start(h) - step * w, 0) for h in range(subs)]
            starts = [pl.multiple_of(jnp.maximum(d - w, 0), tq) for d in done]
            masks = [_sb_masks(key_positions(s) < d) for s, d in zip(starts, done)]

            def more_scores(p):
                return scores(p, starts, [m[0] for m in masks])

            def weights(p, sc):
                return _sb_weights(sc, tri_ref[...], [m[1] for m in masks],
                                   [car_ref[p, h] for h in range(subs)])

            def values(p, aws):
                least = None
                for h, (a, total) in enumerate(aws):
                    acc_ref[p, h] += _dot(a, v_ref[0, pl.ds(starts[h], w), lanes_of(p)])
                    carried = car_ref[p, h] + total[:, 0:1]
                    car_ref[p, h] = carried
                    m = jnp.min(carried)
                    least = m if least is None else jnp.minimum(least, m)
                return least

            mins = _software_pipeline(pairs, [more_scores, weights, values])
            return step + 1, functools.reduce(jnp.minimum, [mins[p] for p in range(pairs)])

        def cond(state):
            step, nxt = state
            return jnp.logical_and(last_start - step * w > 0, nxt < SB_DEAD_LOG2)

        lax.while_loop(cond, body, (jnp.int32(0), live))
        for p in range(pairs):
            for h in range(subs):
                o_ref[0, rows_of(h), lanes_of(p)] = _merge_heads(
                    acc_ref[p, h], lane).astype(o_ref.dtype)


def _sb_attention(q, k, v):
    b, s, wd = q.shape
    pairs = wd // LANES
    subs = SB_STEP_ROWS // SB_Q_TILE
    nq = s // SB_STEP_ROWS
    tile = lambda bi, i: (bi, i, 0)
    whole = lambda bi, i: (bi, 0, 0)
    once = pl.Buffered(1)
    return pl.pallas_call(
        _sb_kernel,
        grid=(b, nq),
        in_specs=[
            pl.BlockSpec((1, SB_STEP_ROWS, wd), tile),
            pl.BlockSpec((1, s, wd), whole, pipeline_mode=once),
            pl.BlockSpec((1, s, wd), whole, pipeline_mode=once),
        ],
        out_specs=pl.BlockSpec((1, SB_STEP_ROWS, wd), tile),
        out_shape=jax.ShapeDtypeStruct((b, s, wd), BF16),
        scratch_shapes=[
            pltpu.VMEM((SB_WINDOW, SB_WINDOW), BF16),
            pltpu.VMEM((subs, 2, 2 * SB_Q_TILE, SB_WINDOW), F32),
            pltpu.VMEM((pairs, subs, 2 * SB_Q_TILE, LANES), F32),
            pltpu.VMEM((pairs, subs, 2 * SB_Q_TILE, LANES), F32),
        ],
        compiler_params=pltpu.CompilerParams(
            dimension_semantics=("arbitrary", "arbitrary"), vmem_limit_bytes=VMEM_LIMIT),
        name="stick_breaking_attention",
    )(q, k, v)


def _t5_causal_bucket(distance):
    exact = REL_BUCKETS // 2
    d = jnp.maximum(distance, 0)
    d_f = jnp.maximum(d, 1).astype(F32)
    large = exact + (jnp.log(d_f / exact) / math.log(REL_MAX_DIST / exact)
                     * (REL_BUCKETS - exact)).astype(jnp.int32)
    large = jnp.minimum(large, REL_BUCKETS - 1)
    return jnp.where(d < exact, d, large)


def _swa_kernel(sinks_ref, relb_ref, q_ref, kvc_ref, kvp_ref, o_ref, bias_ref):
    bi = pl.program_id(0)
    qi = pl.program_id(1)
    t = ATT_BLOCK
    kvw = kvc_ref.shape[-1] // 4
    group = SWA_HEADS // SWA_KV_HEADS

    @pl.when(jnp.logical_and(bi == 0, qi == 0))
    def _fill_bias():
        r = lax.broadcasted_iota(jnp.int32, (t, 2 * t), 0)
        c = lax.broadcasted_iota(jnp.int32, (t, 2 * t), 1)
        dist = r + t - c
        bucket = _t5_causal_bucket(dist)
        valid = jnp.logical_and(dist >= 0, dist < WINDOW)
        for h in range(SWA_HEADS):
            bias = jnp.zeros((t, 2 * t), F32)
            for bk in range(REL_BUCKETS):
                bias = jnp.where(bucket == bk, relb_ref[bk, h], bias)
            bias_ref[h] = jnp.where(valid, bias, MASK_NEG)

    tiles = q_ref.shape[1] // t
    rows_of = lambda n: slice(n * t, (n + 1) * t)
    lane = lax.broadcasted_iota(jnp.int32, (t, LANES), 1)
    row2 = lax.broadcasted_iota(jnp.int32, (2 * t, 1), 0)
    col2 = lax.broadcasted_iota(jnp.int32, (2 * t, 2 * t), 1)
    prev_pen = jnp.where(jnp.logical_and(col2 < t, qi == 0), MASK_NEG, 0.0)

    def kv_window(n, variant):
        cols = slice(variant * kvw, (variant + 1) * kvw)
        before = kvc_ref[0, rows_of(n - 1), cols] if n else kvp_ref[0, :, cols]
        return jnp.concatenate([before, kvc_ref[0, rows_of(n), cols]], axis=0)

    items = [(n, half, j) for n in range(tiles) for half in range(2) for j in range(SWA_KV_HEADS)]
    heads_of = lambda half, j: [h for h in range(j * group, (j + 1) * group) if h % 2 == half]

    def probabilities(i):
        n, half, j = items[i]
        heads = heads_of(half, j)
        in_half = (lane >= HEAD_DIM) if half else (lane < HEAD_DIM)
        qq = jnp.concatenate(
            [jnp.where(in_half, q_ref[0, rows_of(n), (h // 2) * LANES:(h // 2 + 1) * LANES], 0)
             for h in heads], axis=0)
        s = _dot_nt(qq, kv_window(n, 0 + int(j != half)))
        s = s + jnp.concatenate([bias_ref[h] for h in heads], axis=0)
        if n == 0:
            s = s + prev_pen
        sink = jnp.where(row2 < t, sinks_ref[heads[0]], sinks_ref[heads[1]])
        m = jnp.maximum(jnp.max(s, axis=-1, keepdims=True), sink)
        p = jnp.exp(s - m)
        denom = jnp.sum(p, axis=-1, keepdims=True) + jnp.exp(sink - m)
        return p.astype(BF16), 1.0 / denom

    def values(i, pd):
        n, half, j = items[i]
        p, inv_denom = pd
        return _dot(p, kv_window(n, 2 + int(j != half))) * inv_denom

    outs = _software_pipeline(len(items), [probabilities, values])
    results = {}
    for i, (n, half, j) in enumerate(items):
        h0, h1 = heads_of(half, j)
        results[n, h0], results[n, h1] = outs[i][:t], outs[i][t:]

    for n in range(tiles):
        for pair in range(SWA_HEADS // 2):
            o_ref[0, rows_of(n), pair * LANES:(pair + 1) * LANES] = jnp.where(
                lane < HEAD_DIM, results[n, 2 * pair], results[n, 2 * pair + 1]).astype(o_ref.dtype)


def _swa_attention(q, kv, sinks, rel_bias):
    b, s, wd = q.shape
    tiles = SWA_STEP_ROWS // ATT_BLOCK
    nq = s // SWA_STEP_ROWS
    cur = lambda bi, i: (bi, i, 0)
    prev = lambda bi, i: (bi, jnp.maximum(i * tiles - 1, 0), 0)
    smem = pl.BlockSpec(memory_space=pltpu.SMEM)
    return pl.pallas_call(
        _swa_kernel,
        grid=(b, nq),
        in_specs=[
            smem, smem,
            pl.BlockSpec((1, SWA_STEP_ROWS, wd), cur),
            pl.BlockSpec((1, SWA_STEP_ROWS, kv.shape[-1]), cur),
            pl.BlockSpec((1, ATT_BLOCK, kv.shape[-1]), prev),
        ],
        out_specs=pl.BlockSpec((1, SWA_STEP_ROWS, wd), cur),
        out_shape=jax.ShapeDtypeStruct((b, s, wd), BF16),
        scratch_shapes=[pltpu.VMEM((SWA_HEADS, ATT_BLOCK, 2 * ATT_BLOCK), F32)],
        compiler_params=pltpu.CompilerParams(
            dimension_semantics=("arbitrary", "arbitrary"), vmem_limit_bytes=VMEM_LIMIT),
        name="sliding_window_attention",
    )(sinks, rel_bias, q, kv, kv)


def _mix_kernel(h_ref, sb_ref, sw_ref, sbg_ref, swg_ref, w_ref, l1g_ref, l1b_ref, o_ref):
    rows = MIX_PART_ROWS
    parts = h_ref.shape[0] // rows
    part = lambda i: slice(i * rows, (i + 1) * rows)

    def merged(i):
        sb = _rms_norm(sb_ref[part(i), :].astype(F32), sbg_ref[...]).astype(BF16)
        sw = _rms_norm(sw_ref[part(i), :].astype(F32), swg_ref[...]).astype(BF16)
        return jnp.concatenate([sb, sw], axis=1)

    def projected(i, m):
        return _dot(m, w_ref[...])

    def finished(i, mix):
        o_ref[part(i), :] = _layer_norm(ALPHA * h_ref[part(i), :] + mix, l1g_ref[...], l1b_ref[...])

    _software_pipeline(parts, [merged, projected, finished])


def _mix_block(h, sb2, sw2, sb_g, sw_g, w_out_bf16, ln1_g, ln1_b):
    n, d = h.shape
    row = lambda i: (i, 0)
    fixed = lambda i: (0, 0)
    vec = lambda a: pl.BlockSpec((1, a.shape[-1]), fixed)
    return pl.pallas_call(
        _mix_kernel,
        grid=(n // ROW_TILE,),
        in_specs=[
            pl.BlockSpec((ROW_TILE, d), row),
            pl.BlockSpec((ROW_TILE, sb2.shape[-1]), row),
            pl.BlockSpec((ROW_TILE, sw2.shape[-1]), row),
            vec(sb_g), vec(sw_g),
            pl.BlockSpec(w_out_bf16.shape, fixed),
            vec(ln1_g), vec(ln1_b),
        ],
        out_specs=pl.BlockSpec((ROW_TILE, d), row),
        out_shape=jax.ShapeDtypeStruct((n, d), F32),
        compiler_params=pltpu.CompilerParams(
            dimension_semantics=("arbitrary",), vmem_limit_bytes=VMEM_LIMIT),
        name="mix_out_proj_ln1",
    )(h, sb2, sw2, sb_g, sw_g, w_out_bf16, ln1_g, ln1_b)


def _ffn_kernel(h_ref, wgu_ref, wd_ref, g_ref, b_ref, o_ref):
    d_ff = wd_ref.shape[0]
    bounds = list(range(0, d_ff, FFN_CHUNK)) + [d_ff]
    rows = h_ref.shape[0] // FFN_PARTS
    part = lambda i: slice(i * rows, (i + 1) * rows)

    def residual_plus_ffn(i):
        h = h_ref[part(i), :]
        hb = h.astype(BF16)
        acc = ALPHA * h
        for lo, hi in zip(bounds[:-1], bounds[1:]):
            gate = _dot(hb, wgu_ref[:, lo:hi])
            up = _dot(hb, wgu_ref[:, d_ff + lo:d_ff + hi])
            act = (gate / (1.0 + jnp.exp(-gate)) * up).astype(BF16)
            acc = acc + _dot(act, wd_ref[lo:hi, :])
        return acc

    def normed(i, y):
        o_ref[part(i), :] = _layer_norm(y, g_ref[...], b_ref[...])

    _software_pipeline(FFN_PARTS, [residual_plus_ffn, normed])


def _ffn_block(h1, w_gate_up_bf16, w_down_bf16, ln2_g, ln2_b):
    n, d = h1.shape
    d_ff = w_down_bf16.shape[0]
    row = lambda i: (i, 0)
    fixed = lambda i: (0, 0)
    once = pl.Buffered(1)
    return pl.pallas_call(
        _ffn_kernel,
        grid=(n // FFN_ROW_TILE,),
        in_specs=[
            pl.BlockSpec((FFN_ROW_TILE, d), row),
            pl.BlockSpec(w_gate_up_bf16.shape, fixed, pipeline_mode=once),
            pl.BlockSpec(w_down_bf16.shape, fixed, pipeline_mode=once),
            pl.BlockSpec((1, d), fixed),
            pl.BlockSpec((1, d), fixed),
        ],
        out_specs=pl.BlockSpec((FFN_ROW_TILE, d), row),
        out_shape=jax.ShapeDtypeStruct((n, d), F32),
        compiler_params=pltpu.CompilerParams(
            dimension_semantics=("arbitrary",), vmem_limit_bytes=VMEM_LIMIT),
        name="swiglu_ffn_ln2",
    )(h1, w_gate_up_bf16, w_down_bf16, ln2_g, ln2_b)


def kernel(x, ln_in_g, ln_in_b, w_in, sb_norm_g, swa_norm_g, sinks, rel_bias, w_out,
           ln1_g, ln1_b, w_gate_up, w_down, ln2_g, ln2_b):
    b, s, d = x.shape
    assert w_in.shape[0] == DEPTH == 1
    sb_w = SB_HEADS * HEAD_DIM
    sw_w = SWA_HEADS * HEAD_DIM
    kv_w = SWA_KV_HEADS * HEAD_DIM
    assert w_in.shape[-1] == 3 * sb_w + sw_w + 2 * kv_w
    assert kv_w == LANES and SWA_HEADS // SWA_KV_HEADS == 4
    assert s % SB_WINDOW == 0 and s % SB_STEP_ROWS == 0 and s % SWA_STEP_ROWS == 0
    assert (b * s) % ROW_TILE == 0 and (b * s) % FFN_ROW_TILE == 0 and WINDOW <= ATT_BLOCK

    x2 = x.reshape(b * s, d)
    row_vec = lambda a: a.reshape(1, -1)

    names = ("q_sb", "k_sb", "v_sb", "q_sw", "k_sw", "v_sw")
    widths = (sb_w, sb_w, sb_w, sw_w, kv_w, kv_w)
    splits, lo = {}, 0
    for name, wd in zip(names, widths):
        splits[name] = (lo, lo + wd)
        lo += wd
    h, q_sb, k_sb, v_sb, q_sw, kv_sw = _in_projection(
        x2, row_vec(ln_in_g), row_vec(ln_in_b), w_in[0].astype(BF16), splits)

    to3 = lambda a: a.reshape(b, s, a.shape[-1])
    sb_out = _sb_attention(to3(q_sb), to3(k_sb), to3(v_sb))
    sw_out = _swa_attention(to3(q_sw), to3(kv_sw), sinks[0], rel_bias)

    h1 = _mix_block(h, sb_out.reshape(b * s, sb_w), sw_out.reshape(b * s, sw_w),
                    row_vec(sb_norm_g[0]), row_vec(swa_norm_g[0]),
                    w_out[0].astype(BF16), row_vec(ln1_g[0]), row_vec(ln1_b[0]))
    out = _ffn_block(h1, w_gate_up[0].astype(BF16), w_down[0].astype(BF16),
                     row_vec(ln2_g[0]), row_vec(ln2_b[0]))
    return out.reshape(b, s, d)
```

```python
import functools
import math

import jax
import jax.numpy as jnp
from jax import lax
from jax.experimental import pallas as pl
from jax.experimental.pallas import tpu as pltpu

F32 = jnp.float32
BF16 = jnp.bfloat16

HEAD_DIM = 64
SB_HEADS = 8
SWA_HEADS = 8
SWA_KV_HEADS = 2
WINDOW = 128
REL_BUCKETS = 32
REL_MAX_DIST = 128
LN_EPS = 1e-5
RMS_EPS = 1e-6
DEPTH = 1
ALPHA = (2 * DEPTH) ** 0.25
QK_SCALE = HEAD_DIM ** -0.5
LOG2E = math.log2(math.e)

LANES = 128
ATT_BLOCK = 128
SWA_STEP_ROWS = 256
SB_WINDOW = 256
SB_Q_TILE = 64
SB_STEP_ROWS = 256
ROW_TILE = 1024
PROJ_PARTS = 2
MIX_PART_ROWS = 256
FFN_ROW_TILE = 1024
FFN_PARTS = 2
FFN_CHUNK = 1024
VMEM_LIMIT = 56 * 1024 * 1024

SB_DEAD_LOG2 = 127.0
MASK_NEG = -1e30


def _layer_norm(x, g, b):
    mu = jnp.mean(x, axis=-1, keepdims=True)
    xc = x - mu
    var = jnp.mean(xc * xc, axis=-1, keepdims=True)
    return xc * lax.rsqrt(var + LN_EPS) * g + b


def _rms_norm(x, g):
    ms = jnp.mean(x * x, axis=-1, keepdims=True)
    return x * lax.rsqrt(ms + RMS_EPS) * g


def _dot(a, b):
    return jnp.dot(a, b, preferred_element_type=F32)


def _dot_nt(a, b):
    return lax.dot_general(a, b, (((1,), (1,)), ((), ())), preferred_element_type=F32)


def _inproj_kernel(splits, n_later, x_ref, g_ref, b_ref, w32_ref, *refs):
    later_in, refs = refs[:n_later], refs[n_later:]
    h_ref, qsb_ref, ksb_ref, vsb_ref, qsw_ref, kv_ref = refs[:6]
    later_out, w_ref = refs[6:6 + n_later], refs[6 + n_later]
    rows = x_ref.shape[0] // PROJ_PARTS
    part = lambda i: slice(i * rows, (i + 1) * rows)
    assert splits["k_sw"][1] == splits["v_sw"][0]

    @pl.when(pl.program_id(0) == 0)
    def _convert_own_weight():
        w_ref[...] = w32_ref[...].astype(BF16)

    for src, dst in zip(later_in, later_out):
        dst[...] = src[...].astype(BF16)

    def normed(i):
        h32 = _layer_norm(x_ref[part(i), :], g_ref[...], b_ref[...])
        h_ref[part(i), :] = h32
        return h32.astype(BF16)

    def projected(i, h):
        r = part(i)
        proj = lambda lo, hi: _dot(h, w_ref[:, lo:hi])
        qsb_ref[r, :] = (proj(*splits["q_sb"]) * QK_SCALE).astype(BF16)
        ksb_ref[r, :] = proj(*splits["k_sb"]).astype(BF16)
        vsb_ref[r, :] = proj(*splits["v_sb"]).astype(BF16)
        qsw_ref[r, :] = (proj(*splits["q_sw"]) * QK_SCALE).astype(BF16)
        kv = proj(splits["k_sw"][0], splits["v_sw"][1])
        kvw = kv.shape[-1] // 2
        k, v = kv[:, :kvw], kv[:, kvw:]
        kv_ref[r, 0 * kvw:1 * kvw] = k.astype(BF16)
        kv_ref[r, 1 * kvw:2 * kvw] = pltpu.roll(k, HEAD_DIM, axis=1).astype(BF16)
        kv_ref[r, 2 * kvw:3 * kvw] = v.astype(BF16)
        kv_ref[r, 3 * kvw:4 * kvw] = pltpu.roll(v, HEAD_DIM, axis=1).astype(BF16)

    _software_pipeline(PROJ_PARTS, [normed, projected])


def _in_projection(x2, g, b, w_in, splits, later_weights):
    n, d = x2.shape
    steps = n // ROW_TILE
    width = lambda name: splits[name][1] - splits[name][0]
    out_w = [width("q_sb"), width("k_sb"), width("v_sb"), width("q_sw"), 4 * width("k_sw")]
    row = lambda i: (i, 0)
    fixed = lambda i: (0, 0)
    assert all(w.shape[0] % (16 * steps) == 0 for w in later_weights)
    later_specs = [pl.BlockSpec((w.shape[0] // steps, w.shape[1]), row) for w in later_weights]
    return pl.pallas_call(
        functools.partial(_inproj_kernel, splits, len(later_weights)),
        grid=(steps,),
        in_specs=[
            pl.BlockSpec((ROW_TILE, d), row),
            pl.BlockSpec((1, d), fixed),
            pl.BlockSpec((1, d), fixed),
            pl.BlockSpec(w_in.shape, fixed, pipeline_mode=pl.Buffered(1)),
        ] + later_specs,
        out_specs=[pl.BlockSpec((ROW_TILE, d), row)]
        + [pl.BlockSpec((ROW_TILE, w), row) for w in out_w] + later_specs,
        out_shape=[jax.ShapeDtypeStruct((n, d), F32)]
        + [jax.ShapeDtypeStruct((n, w), BF16) for w in out_w]
        + [jax.ShapeDtypeStruct(w.shape, BF16) for w in later_weights],
        scratch_shapes=[pltpu.VMEM(w_in.shape, BF16)],
        compiler_params=pltpu.CompilerParams(
            dimension_semantics=("arbitrary",), vmem_limit_bytes=VMEM_LIMIT),
        name="ln_in_proj",
    )(x2, g, b, w_in, *later_weights)


def _split_heads(x, lane):
    zero = jnp.zeros_like(x)
    return jnp.concatenate([jnp.where(lane < HEAD_DIM, x, zero),
                            jnp.where(lane >= HEAD_DIM, x, zero)], axis=0)


def _merge_heads(o2, lane):
    t = o2.shape[0] // 2
    return jnp.where(lane < HEAD_DIM, o2[:t], o2[t:])


def _bits(x):
    return pltpu.bitcast(x, jnp.uint32)


def _sb_masks(valid):
    return jnp.where(valid, LOG2E, 0.0), jnp.where(valid, 0.0, MASK_NEG)


def _sb_scores(qq, k, keep):
    z = _dot_nt(qq, k)
    zl = z * LOG2E
    neg_abs = pltpu.bitcast(_bits(zl) | jnp.uint32(0x80000000), F32)
    c_nat = jnp.maximum(z, 0.0) + jnp.log(1.0 + jnp.exp2(neg_abs))
    cm = c_nat * keep
    hi = pltpu.bitcast(_bits(cm) & jnp.uint32(0xFFFF0000), F32)
    lo = cm - hi
    return zl - cm, hi.astype(BF16), lo.astype(BF16), cm[:, :LANES]


def _sb_weights(scores, tri, kills, carrieds):
    sums = _dot(jnp.concatenate([x for sc in scores for x in sc[1:3]], axis=0), tri)
    rows = scores[0][0].shape[0]
    out = []
    for n, ((log_beta, _, _, cm0), kill, carried) in enumerate(zip(scores, kills, carrieds)):
        later = sums[2 * n * rows:(2 * n + 1) * rows] + sums[(2 * n + 1) * rows:(2 * n + 2) * rows]
        log_a = (log_beta - later) + kill
        if carried is not None:
            log_a = log_a - jnp.concatenate([carried] * (log_a.shape[1] // LANES), axis=1)
        out.append((jnp.exp2(log_a).astype(BF16), later[:, :LANES] + cm0))
    return out


def _software_pipeline(n, stages):
    vals = [dict() for _ in stages]
    for step in range(n + len(stages) - 1):
        for s, stage in enumerate(stages):
            i = step - s
            if 0 <= i < n:
                vals[s][i] = stage(i, vals[s - 1].pop(i)) if s else stage(i)
    return vals[-1]


def _sb_kernel(q_ref, k_ref, v_ref, o_ref, tri_ref, mask_ref, acc_ref, car_ref):
    bi = pl.program_id(0)
    qi = pl.program_id(1)
    tq, w = SB_Q_TILE, SB_WINDOW
    subs = q_ref.shape[1] // tq
    pairs = q_ref.shape[-1] // LANES
    row0 = qi * (subs * tq)

    @pl.when(jnp.logical_and(bi == 0, qi == 0))
    def _fill_tri():
        j = lax.broadcasted_iota(jnp.int32, (w, w), 0)
        s = lax.broadcasted_iota(jnp.int32, (w, w), 1)
        tri_ref[...] = jnp.where(j > s, 1.0, 0.0).astype(BF16)

    lane = lax.broadcasted_iota(jnp.int32, (tq, LANES), 1)
    lanes_of = lambda p: slice(p * LANES, (p + 1) * LANES)
    rows_of = lambda h: slice(h * tq, (h + 1) * tq)
    first_start = lambda h: pl.multiple_of(jnp.maximum(row0 + (h + 1) * tq - w, 0), tq)

    def key_positions(start):
        return start + lax.broadcasted_iota(jnp.int32, (2 * tq, w), 1)

    @pl.when(row0 <= w)
    def _fill_masks():
        row = lax.broadcasted_iota(jnp.int32, (2 * tq, w), 0) & (tq - 1)
        for h in range(subs):
            mask_ref[h, 0], mask_ref[h, 1] = _sb_masks(
                key_positions(first_start(h)) < row0 + h * tq + row)

    def scores(p, starts, keeps):
        return [_sb_scores(_split_heads(q_ref[0, rows_of(h), lanes_of(p)], lane),
                           k_ref[0, pl.ds(starts[h], w), lanes_of(p)], keeps[h])
                for h in range(subs)]

    first_starts = [first_start(h) for h in range(subs)]

    def first_scores(p):
        return scores(p, first_starts, [mask_ref[h, 0] for h in range(subs)])

    def first_weights(p, sc):
        return _sb_weights(sc, tri_ref[...], [mask_ref[h, 1] for h in range(subs)], [None] * subs)

    def first_values(p, aws):
        least = None
        for h, (a, total) in enumerate(aws):
            o2 = _dot(a, v_ref[0, pl.ds(first_starts[h], w), lanes_of(p)])
            o_ref[0, rows_of(h), lanes_of(p)] = _merge_heads(o2, lane).astype(o_ref.dtype)
            acc_ref[p, h] = o2
            car_ref[p, h] = total
            m = jnp.min(total[:, 0:1])
            least = m if least is None else jnp.minimum(least, m)
        return least

    first = _software_pipeline(pairs, [first_scores, first_weights, first_values])
    live = functools.reduce(jnp.minimum, [first[p] for p in range(pairs)])
    last_start = first_starts[-1]

    @pl.when(jnp.logical_and(last_start > 0, live < SB_DEAD_LOG2))
    def _walk_back():
        for p in range(pairs):
            for h in range(subs):
                car_ref[p, h] = jnp.broadcast_to(car_ref[p, h][:, 0:1], (2 * tq, LANES))

        def body(state):
            step, _ = state
            done = [jnp.maximum(first_start(h) - step * w, 0) for h in range(subs)]
            starts = [pl.multiple_of(jnp.maximum(d - w, 0), tq) for d in done]
            masks = [_sb_masks(key_positions(s) < d) for s, d in zip(starts, done)]

            def more_scores(p):
                return scores(p, starts, [m[0] for m in masks])

            def weights(p, sc):
                return _sb_weights(sc, tri_ref[...], [m[1] for m in masks],
                                   [car_ref[p, h] for h in range(subs)])

            def values(p, aws):
                least = None
                for h, (a, total) in enumerate(aws):
                    acc_ref[p, h] += _dot(a, v_ref[0, pl.ds(starts[h], w), lanes_of(p)])
                    carried = car_ref[p, h] + total[:, 0:1]
                    car_ref[p, h] = carried
                    m = jnp.min(carried)
                    least = m if least is None else jnp.minimum(least, m)
                return least

            mins = _software_pipeline(pairs, [more_scores, weights, values])
            return step + 1, functools.reduce(jnp.minimum, [mins[p] for p in range(pairs)])

        def cond(state):
            step, nxt = state
            return jnp.logical_and(last_start - step * w > 0, nxt < SB_DEAD_LOG2)

        lax.while_loop(cond, body, (jnp.int32(0), live))
        for p in range(pairs):
            for h in range(subs):
                o_ref[0, rows_of(h), lanes_of(p)] = _merge_heads(
                    acc_ref[p, h], lane).astype(o_ref.dtype)


def _sb_attention(q, k, v):
    b, s, wd = q.shape
    pairs = wd // LANES
    subs = SB_STEP_ROWS // SB_Q_TILE
    nq = s // SB_STEP_ROWS
    tile = lambda bi, i: (bi, i, 0)
    whole = lambda bi, i: (bi, 0, 0)
    once = pl.Buffered(1)
    return pl.pallas_call(
        _sb_kernel,
        grid=(b, nq),
        in_specs=[
            pl.BlockSpec((1, SB_STEP_ROWS, wd), tile),
            pl.BlockSpec((1, s, wd), whole, pipeline_mode=once),
            pl.BlockSpec((1, s, wd), whole, pipeline_mode=once),
        ],
        out_specs=pl.BlockSpec((1, SB_STEP_ROWS, wd), tile),
        out_shape=jax.ShapeDtypeStruct((b, s, wd), BF16),
        scratch_shapes=[
            pltpu.VMEM((SB_WINDOW, SB_WINDOW), BF16),
            pltpu.VMEM((subs, 2, 2 * SB_Q_TILE, SB_WINDOW), F32),
            pltpu.VMEM((pairs, subs, 2 * SB_Q_TILE, LANES), F32),
            pltpu.VMEM((pairs, subs, 2 * SB_Q_TILE, LANES), F32),
        ],
        compiler_params=pltpu.CompilerParams(
            dimension_semantics=("arbitrary", "arbitrary"), vmem_limit_bytes=VMEM_LIMIT),
        name="stick_breaking_attention",
    )(q, k, v)


def _t5_causal_bucket(distance):
    exact = REL_BUCKETS // 2
    d = jnp.maximum(distance, 0)
    d_f = jnp.maximum(d, 1).astype(F32)
    large = exact + (jnp.log(d_f / exact) / math.log(REL_MAX_DIST / exact)
                     * (REL_BUCKETS - exact)).astype(jnp.int32)
    large = jnp.minimum(large, REL_BUCKETS - 1)
    return jnp.where(d < exact, d, large)


def _swa_kernel(sinks_ref, relb_ref, q_ref, kvc_ref, kvp_ref, o_ref, bias_ref):
    bi = pl.program_id(0)
    qi = pl.program_id(1)
    t = ATT_BLOCK
    kvw = kvc_ref.shape[-1] // 4
    group = SWA_HEADS // SWA_KV_HEADS

    @pl.when(jnp.logical_and(bi == 0, qi == 0))
    def _fill_bias():
        r = lax.broadcasted_iota(jnp.int32, (t, 2 * t), 0)
        c = lax.broadcasted_iota(jnp.int32, (t, 2 * t), 1)
        dist = r + t - c
        bucket = _t5_causal_bucket(dist)
        valid = jnp.logical_and(dist >= 0, dist < WINDOW)
        for h in range(SWA_HEADS):
            bias = jnp.zeros((t, 2 * t), F32)
            for bk in range(REL_BUCKETS):
                bias = jnp.where(bucket == bk, relb_ref[bk, h], bias)
            bias_ref[h] = jnp.where(valid, bias, MASK_NEG)

    tiles = q_ref.shape[1] // t
    rows_of = lambda n: slice(n * t, (n + 1) * t)
    lane = lax.broadcasted_iota(jnp.int32, (t, LANES), 1)
    row2 = lax.broadcasted_iota(jnp.int32, (2 * t, 1), 0)
    col2 = lax.broadcasted_iota(jnp.int32, (2 * t, 2 * t), 1)
    prev_pen = jnp.where(jnp.logical_and(col2 < t, qi == 0), MASK_NEG, 0.0)

    def kv_window(n, variant):
        cols = slice(variant * kvw, (variant + 1) * kvw)
        before = kvc_ref[0, rows_of(n - 1), cols] if n else kvp_ref[0, :, cols]
        return jnp.concatenate([before, kvc_ref[0, rows_of(n), cols]], axis=0)

    items = [(n, half, j) for n in range(tiles) for half in range(2) for j in range(SWA_KV_HEADS)]
    heads_of = lambda half, j: [h for h in range(j * group, (j + 1) * group) if h % 2 == half]

    def probabilities(i):
        n, half, j = items[i]
        heads = heads_of(half, j)
        in_half = (lane >= HEAD_DIM) if half else (lane < HEAD_DIM)
        qq = jnp.concatenate(
            [jnp.where(in_half, q_ref[0, rows_of(n), (h // 2) * LANES:(h // 2 + 1) * LANES], 0)
             for h in heads], axis=0)
        s = _dot_nt(qq, kv_window(n, 0 + int(j != half)))
        s = s + jnp.concatenate([bias_ref[h] for h in heads], axis=0)
        if n == 0:
            s = s + prev_pen
        sink = jnp.where(row2 < t, sinks_ref[heads[0]], sinks_ref[heads[1]])
        m = jnp.maximum(jnp.max(s, axis=-1, keepdims=True), sink)
        p = jnp.exp(s - m)
        denom = jnp.sum(p, axis=-1, keepdims=True) + jnp.exp(sink - m)
        return p.astype(BF16), 1.0 / denom

    def values(i, pd):
        n, half, j = items[i]
        p, inv_denom = pd
        return _dot(p, kv_window(n, 2 + int(j != half))) * inv_denom

    outs = _software_pipeline(len(items), [probabilities, values])
    results = {}
    for i, (n, half, j) in enumerate(items):
        h0, h1 = heads_of(half, j)
        results[n, h0], results[n, h1] = outs[i][:t], outs[i][t:]

    for n in range(tiles):
        for pair in range(SWA_HEADS // 2):
            o_ref[0, rows_of(n), pair * LANES:(pair + 1) * LANES] = jnp.where(
                lane < HEAD_DIM, results[n, 2 * pair], results[n, 2 * pair + 1]).astype(o_ref.dtype)


def _swa_attention(q, kv, sinks, rel_bias):
    b, s, wd = q.shape
    tiles = SWA_STEP_ROWS // ATT_BLOCK
    nq = s // SWA_STEP_ROWS
    cur = lambda bi, i: (bi, i, 0)
    prev = lambda bi, i: (bi, jnp.maximum(i * tiles - 1, 0), 0)
    smem = pl.BlockSpec(memory_space=pltpu.SMEM)
    return pl.pallas_call(
        _swa_kernel,
        grid=(b, nq),
        in_specs=[
            smem, smem,
            pl.BlockSpec((1, SWA_STEP_ROWS, wd), cur),
            pl.BlockSpec((1, SWA_STEP_ROWS, kv.shape[-1]), cur),
            pl.BlockSpec((1, ATT_BLOCK, kv.shape[-1]), prev),
        ],
        out_specs=pl.BlockSpec((1, SWA_STEP_ROWS, wd), cur),
        out_shape=jax.ShapeDtypeStruct((b, s, wd), BF16),
        scratch_shapes=[pltpu.VMEM((SWA_HEADS, ATT_BLOCK, 2 * ATT_BLOCK), F32)],
        compiler_params=pltpu.CompilerParams(
            dimension_semantics=("arbitrary", "arbitrary"), vmem_limit_bytes=VMEM_LIMIT),
        name="sliding_window_attention",
    )(sinks, rel_bias, q, kv, kv)


def _mix_kernel(h_ref, sb_ref, sw_ref, sbg_ref, swg_ref, w_ref, l1g_ref, l1b_ref, o_ref):
    rows = MIX_PART_ROWS
    parts = h_ref.shape[0] // rows
    part = lambda i: slice(i * rows, (i + 1) * rows)

    def merged(i):
        sb = _rms_norm(sb_ref[part(i), :].astype(F32), sbg_ref[...]).astype(BF16)
        sw = _rms_norm(sw_ref[part(i), :].astype(F32), swg_ref[...]).astype(BF16)
        return jnp.concatenate([sb, sw], axis=1)

    def projected(i, m):
        return _dot(m, w_ref[...])

    def finished(i, mix):
        o_ref[part(i), :] = _layer_norm(ALPHA * h_ref[part(i), :] + mix, l1g_ref[...], l1b_ref[...])

    _software_pipeline(parts, [merged, projected, finished])


def _mix_block(h, sb2, sw2, sb_g, sw_g, w_out_bf16, ln1_g, ln1_b):
    n, d = h.shape
    row = lambda i: (i, 0)
    fixed = lambda i: (0, 0)
    vec = lambda a: pl.BlockSpec((1, a.shape[-1]), fixed)
    return pl.pallas_call(
        _mix_kernel,
        grid=(n // ROW_TILE,),
        in_specs=[
            pl.BlockSpec((ROW_TILE, d), row),
            pl.BlockSpec((ROW_TILE, sb2.shape[-1]), row),
            pl.BlockSpec((ROW_TILE, sw2.shape[-1]), row),
            vec(sb_g), vec(sw_g),
            pl.BlockSpec(w_out_bf16.shape, fixed),
            vec(ln1_g), vec(ln1_b),
        ],
        out_specs=pl.BlockSpec((ROW_TILE, d), row),
        out_shape=jax.ShapeDtypeStruct((n, d), F32),
        compiler_params=pltpu.CompilerParams(
            dimension_semantics=("arbitrary",), vmem_limit_bytes=VMEM_LIMIT),
        name="mix_out_proj_ln1",
    )(h, sb2, sw2, sb_g, sw_g, w_out_bf16, ln1_g, ln1_b)


def _ffn_kernel(h_ref, wgu_ref, wd_ref, g_ref, b_ref, o_ref):
    d_ff = wd_ref.shape[0]
    bounds = list(range(0, d_ff, FFN_CHUNK)) + [d_ff]
    rows = h_ref.shape[0] // FFN_PARTS
    part = lambda i: slice(i * rows, (i + 1) * rows)
    items = [(i, c) for i in range(FFN_PARTS) for c in range(len(bounds) - 1)]
    acc = {}

    def activation(n):
        i, c = items[n]
        lo, hi = bounds[c], bounds[c + 1]
        hb = h_ref[part(i), :].astype(BF16)
        gate = _dot(hb, wgu_ref[:, lo:hi])
        up = _dot(hb, wgu_ref[:, d_ff + lo:d_ff + hi])
        return (gate / (1.0 + jnp.exp(-gate)) * up).astype(BF16)

    def down(n, act):
        i, c = items[n]
        prev = acc[i] if c else ALPHA * h_ref[part(i), :]
        acc[i] = prev + _dot(act, wd_ref[bounds[c]:bounds[c + 1], :])
        return i if c == len(bounds) - 2 else None

    def normed(n, i):
        if i is not None:
            o_ref[part(i), :] = _layer_norm(acc.pop(i), g_ref[...], b_ref[...])

    _software_pipeline(len(items), [activation, down, normed])


def _ffn_block(h1, w_gate_up_bf16, w_down_bf16, ln2_g, ln2_b):
    n, d = h1.shape
    d_ff = w_down_bf16.shape[0]
    row = lambda i: (i, 0)
    fixed = lambda i: (0, 0)
    once = pl.Buffered(1)
    return pl.pallas_call(
        _ffn_kernel,
        grid=(n // FFN_ROW_TILE,),
        in_specs=[
            pl.BlockSpec((FFN_ROW_TILE, d), row),
            pl.BlockSpec(w_gate_up_bf16.shape, fixed, pipeline_mode=once),
            pl.BlockSpec(w_down_bf16.shape, fixed, pipeline_mode=once),
            pl.BlockSpec((1, d), fixed),
            pl.BlockSpec((1, d), fixed),
        ],
        out_specs=pl.BlockSpec((FFN_ROW_TILE, d), row),
        out_shape=jax.ShapeDtypeStruct((n, d), F32),
        compiler_params=pltpu.CompilerParams(
            dimension_semantics=("arbitrary",), vmem_limit_bytes=VMEM_LIMIT),
        name="swiglu_ffn_ln2",
    )(h1, w_gate_up_bf16, w_down_bf16, ln2_g, ln2_b)


def kernel(x, ln_in_g, ln_in_b, w_in, sb_norm_g, swa_norm_g, sinks, rel_bias, w_out,
           ln1_g, ln1_b, w_gate_up, w_down, ln2_g, ln2_b):
    b, s, d = x.shape
    assert w_in.shape[0] == DEPTH == 1
    sb_w = SB_HEADS * HEAD_DIM
    sw_w = SWA_HEADS * HEAD_DIM
    kv_w = SWA_KV_HEADS * HEAD_DIM
    assert w_in.shape[-1] == 3 * sb_w + sw_w + 2 * kv_w
    assert kv_w == LANES and SWA_HEADS // SWA_KV_HEADS == 4
    assert s % SB_WINDOW == 0 and s % SB_STEP_ROWS == 0 and s % SWA_STEP_ROWS == 0
    assert (b * s) % ROW_TILE == 0 and (b * s) % FFN_ROW_TILE == 0 and WINDOW <= ATT_BLOCK

    x2 = x.reshape(b * s, d)
    row_vec = lambda a: a.reshape(1, -1)

    names = ("q_sb", "k_sb", "v_sb", "q_sw", "k_sw", "v_sw")
    widths = (sb_w, sb_w, sb_w, sw_w, kv_w, kv_w)
    splits, lo = {}, 0
    for name, wd in zip(names, widths):
        splits[name] = (lo, lo + wd)
        lo += wd
    h, q_sb, k_sb, v_sb, q_sw, kv_sw, w_out_b, w_gate_up_b, w_down_b = _in_projection(
        x2, row_vec(ln_in_g), row_vec(ln_in_b), w_in[0], splits,
        [w_out[0], w_gate_up[0], w_down[0]])

    to3 = lambda a: a.reshape(b, s, a.shape[-1])
    sb_out = _sb_attention(to3(q_sb), to3(k_sb), to3(v_sb))
    sw_out = _swa_attention(to3(q_sw), to3(kv_sw), sinks[0], rel_bias)

    h1 = _mix_block(h, sb_out.reshape(b * s, sb_w), sw_out.reshape(b * s, sw_w),
                    row_vec(sb_norm_g[0]), row_vec(swa_norm_g[0]),
                    w_out_b, row_vec(ln1_g[0]), row_vec(ln1_b[0]))
    out = _ffn_block(h1, w_gate_up_b, w_down_b, row_vec(ln2_g[0]), row_vec(ln2_b[0]))
    return out.reshape(b, s, d)
```

```python
import functools
import math

import jax
import jax.numpy as jnp
import numpy as np
from jax import lax
from jax.experimental import pallas as pl
from jax.experimental.pallas import tpu as pltpu

F32 = jnp.float32
BF16 = jnp.bfloat16

HEAD_DIM = 64
SB_HEADS = 8
SWA_HEADS = 8
SWA_KV_HEADS = 2
WINDOW = 128
REL_BUCKETS = 32
REL_MAX_DIST = 128
LN_EPS = 1e-5
RMS_EPS = 1e-6
DEPTH = 1
ALPHA = (2 * DEPTH) ** 0.25
QK_SCALE = HEAD_DIM ** -0.5
LOG2E = math.log2(math.e)

LANES = 128
ATT_BLOCK = 128
SWA_STEP_ROWS = 256
SB_WINDOW = 256
SB_Q_TILE = 64
SB_STEP_ROWS = 256
ROW_TILE = 1024
PROJ_PARTS = 2
MIX_PART_ROWS = 256
FFN_ROW_TILE = 1024
FFN_PARTS = 2
FFN_CHUNK = 1024
VMEM_LIMIT = 56 * 1024 * 1024

SB_DEAD_LOG2 = 127.0
MASK_NEG = -1e30


def _layer_norm(x, g, b):
    mu = jnp.mean(x, axis=-1, keepdims=True)
    xc = x - mu
    var = jnp.mean(xc * xc, axis=-1, keepdims=True)
    return xc * lax.rsqrt(var + LN_EPS) * g + b


def _rms_norm(x, g):
    ms = jnp.mean(x * x, axis=-1, keepdims=True)
    return x * lax.rsqrt(ms + RMS_EPS) * g


def _dot(a, b):
    return jnp.dot(a, b, preferred_element_type=F32)


def _dot_nt(a, b):
    return lax.dot_general(a, b, (((1,), (1,)), ((), ())), preferred_element_type=F32)


def _inproj_kernel(splits, x_ref, g_ref, b_ref, w32_ref,
                   ha_ref, qsb_ref, ksb_ref, vsb_ref, qsw_ref, kv_ref, w_ref):
    rows = x_ref.shape[0] // PROJ_PARTS
    part = lambda i: slice(i * rows, (i + 1) * rows)
    assert splits["k_sw"][1] == splits["v_sw"][0]

    @pl.when(pl.program_id(0) == 0)
    def _convert_weight():
        w_ref[...] = w32_ref[...].astype(BF16)

    def normed(i):
        h32 = _layer_norm(x_ref[part(i), :], g_ref[...], b_ref[...])
        ha_ref[part(i), :] = ALPHA * h32
        return h32.astype(BF16)

    def projected(i, h):
        r = part(i)
        proj = lambda lo, hi: _dot(h, w_ref[:, lo:hi])
        qsb_ref[r, :] = (proj(*splits["q_sb"]) * QK_SCALE).astype(BF16)
        ksb_ref[r, :] = proj(*splits["k_sb"]).astype(BF16)
        vsb_ref[r, :] = proj(*splits["v_sb"]).astype(BF16)
        qsw_ref[r, :] = (proj(*splits["q_sw"]) * QK_SCALE).astype(BF16)
        kv = proj(splits["k_sw"][0], splits["v_sw"][1])
        kvw = kv.shape[-1] // 2
        k, v = kv[:, :kvw], kv[:, kvw:]
        kv_ref[r, 0 * kvw:1 * kvw] = k.astype(BF16)
        kv_ref[r, 1 * kvw:2 * kvw] = pltpu.roll(k, HEAD_DIM, axis=1).astype(BF16)
        kv_ref[r, 2 * kvw:3 * kvw] = v.astype(BF16)
        kv_ref[r, 3 * kvw:4 * kvw] = pltpu.roll(v, HEAD_DIM, axis=1).astype(BF16)

    _software_pipeline(PROJ_PARTS, [normed, projected])


def _in_projection(x2, g, b, w_in, splits):
    n, d = x2.shape
    width = lambda name: splits[name][1] - splits[name][0]
    out_w = [width("q_sb"), width("k_sb"), width("v_sb"), width("q_sw"), 4 * width("k_sw")]
    row = lambda i: (i, 0)
    fixed = lambda i: (0, 0)
    return pl.pallas_call(
        functools.partial(_inproj_kernel, splits),
        grid=(n // ROW_TILE,),
        in_specs=[
            pl.BlockSpec((ROW_TILE, d), row),
            pl.BlockSpec((1, d), fixed),
            pl.BlockSpec((1, d), fixed),
            pl.BlockSpec(w_in.shape, fixed, pipeline_mode=pl.Buffered(1)),
        ],
        out_specs=[pl.BlockSpec((ROW_TILE, d), row)]
        + [pl.BlockSpec((ROW_TILE, w), row) for w in out_w],
        out_shape=[jax.ShapeDtypeStruct((n, d), F32)]
        + [jax.ShapeDtypeStruct((n, w), BF16) for w in out_w],
        scratch_shapes=[pltpu.VMEM(w_in.shape, BF16)],
        compiler_params=pltpu.CompilerParams(
            dimension_semantics=("arbitrary",), vmem_limit_bytes=VMEM_LIMIT),
        name="ln_in_proj",
    )(x2, g, b, w_in)


def _conversion_blocks(weights, steps):
    sizes = []
    for w in weights:
        rows = w.shape[0]
        size = next(r for r in range(16, rows + 1, 16) if rows % r == 0 and rows // r <= steps)
        sizes.append(size)
    return sizes


def _split_heads(x, lane):
    zero = jnp.zeros_like(x)
    return jnp.concatenate([jnp.where(lane < HEAD_DIM, x, zero),
                            jnp.where(lane >= HEAD_DIM, x, zero)], axis=0)


def _merge_heads(o2, lane):
    t = o2.shape[0] // 2
    return jnp.where(lane < HEAD_DIM, o2[:t], o2[t:])


def _sb_masks(valid):
    return jnp.where(valid, LOG2E, 0.0), jnp.where(valid, 0.0, MASK_NEG)


def _sb_scores(qq, k, keep):
    z = _dot_nt(qq, k)
    zl = z * LOG2E
    c_nat = jnp.maximum(z, 0.0) + jnp.log(1.0 + jnp.exp2(-jnp.abs(zl)))
    cm = c_nat * keep
    hi = cm.astype(BF16)
    lo = (cm - hi.astype(F32)).astype(BF16)
    return zl - cm, hi, lo, cm[:, :LANES]


def _sb_weights(scores, tri, kills, carrieds):
    sums = _dot(jnp.concatenate([x for sc in scores for x in sc[1:3]], axis=0), tri)
    rows = scores[0][0].shape[0]
    out = []
    for n, ((log_beta, _, _, cm0), kill, carried) in enumerate(zip(scores, kills, carrieds)):
        later = sums[2 * n * rows:(2 * n + 1) * rows] + sums[(2 * n + 1) * rows:(2 * n + 2) * rows]
        log_a = (log_beta - later) + kill
        if carried is not None:
            log_a = log_a - jnp.concatenate([carried] * (log_a.shape[1] // LANES), axis=1)
        out.append((jnp.exp2(log_a).astype(BF16), later[:, :LANES] + cm0))
    return out


def _software_pipeline(n, stages):
    vals = [dict() for _ in stages]
    for step in range(n + len(stages) - 1):
        for s, stage in enumerate(stages):
            i = step - s
            if 0 <= i < n:
                vals[s][i] = stage(i, vals[s - 1].pop(i)) if s else stage(i)
    return vals[-1]


def _sb_kernel(n_conv, q_ref, k_ref, v_ref, *refs):
    conv_in, o_ref, conv_out = refs[:n_conv], refs[n_conv], refs[n_conv + 1:2 * n_conv + 1]
    tri_ref, mask_ref, acc_ref, car_ref = refs[2 * n_conv + 1:]
    for src, dst in zip(conv_in, conv_out):
        dst[...] = src[...].astype(BF16)

    bi = pl.program_id(0)
    qi = pl.program_id(1)
    tq, w = SB_Q_TILE, SB_WINDOW
    subs = q_ref.shape[1] // tq
    pairs = q_ref.shape[-1] // LANES
    row0 = qi * (subs * tq)

    @pl.when(jnp.logical_and(bi == 0, qi == 0))
    def _fill_tri():
        j = lax.broadcasted_iota(jnp.int32, (w, w), 0)
        s = lax.broadcasted_iota(jnp.int32, (w, w), 1)
        tri_ref[...] = jnp.where(j > s, 1.0, 0.0).astype(BF16)

    lane = lax.broadcasted_iota(jnp.int32, (tq, LANES), 1)
    lanes_of = lambda p: slice(p * LANES, (p + 1) * LANES)
    rows_of = lambda h: slice(h * tq, (h + 1) * tq)
    first_start = lambda h: pl.multiple_of(jnp.maximum(row0 + (h + 1) * tq - w, 0), tq)

    def key_positions(start):
        return start + lax.broadcasted_iota(jnp.int32, (2 * tq, w), 1)

    @pl.when(row0 <= w)
    def _fill_masks():
        row = lax.broadcasted_iota(jnp.int32, (2 * tq, w), 0) & (tq - 1)
        for h in range(subs):
            mask_ref[h, 0], mask_ref[h, 1] = _sb_masks(
                key_positions(first_start(h)) < row0 + h * tq + row)

    def scores(p, starts, keeps):
        return [_sb_scores(_split_heads(q_ref[0, rows_of(h), lanes_of(p)], lane),
                           k_ref[0, pl.ds(starts[h], w), lanes_of(p)], keeps[h])
                for h in range(subs)]

    first_starts = [first_start(h) for h in range(subs)]

    def first_scores(p):
        return scores(p, first_starts, [mask_ref[h, 0] for h in range(subs)])

    def first_weights(p, sc):
        return _sb_weights(sc, tri_ref[...], [mask_ref[h, 1] for h in range(subs)], [None] * subs)

    def first_values(p, aws):
        least = None
        for h, (a, total) in enumerate(aws):
            o2 = _dot(a, v_ref[0, pl.ds(first_starts[h], w), lanes_of(p)])
            o_ref[0, rows_of(h), lanes_of(p)] = _merge_heads(o2, lane).astype(o_ref.dtype)
            acc_ref[p, h] = o2
            car_ref[p, h] = total
            m = jnp.min(total[:, 0:1])
            least = m if least is None else jnp.minimum(least, m)
        return least

    first = _software_pipeline(pairs, [first_scores, first_weights, first_values])
    live = functools.reduce(jnp.minimum, [first[p] for p in range(pairs)])
    last_start = first_starts[-1]

    @pl.when(jnp.logical_and(last_start > 0, live < SB_DEAD_LOG2))
    def _walk_back():
        for p in range(pairs):
            for h in range(subs):
                car_ref[p, h] = jnp.broadcast_to(car_ref[p, h][:, 0:1], (2 * tq, LANES))

        def body(state):
            step, _ = state
            done = [jnp.maximum(first_start(h) - step * w, 0) for h in range(subs)]
            starts = [pl.multiple_of(jnp.maximum(d - w, 0), tq) for d in done]
            masks = [_sb_masks(key_positions(s) < d) for s, d in zip(starts, done)]

            def more_scores(p):
                return scores(p, starts, [m[0] for m in masks])

            def weights(p, sc):
                return _sb_weights(sc, tri_ref[...], [m[1] for m in masks],
                                   [car_ref[p, h] for h in range(subs)])

            def values(p, aws):
                least = None
                for h, (a, total) in enumerate(aws):
                    acc_ref[p, h] += _dot(a, v_ref[0, pl.ds(starts[h], w), lanes_of(p)])
                    carried = car_ref[p, h] + total[:, 0:1]
                    car_ref[p, h] = carried
                    m = jnp.min(carried)
                    least = m if least is None else jnp.minimum(least, m)
                return least

            mins = _software_pipeline(pairs, [more_scores, weights, values])
            return step + 1, functools.reduce(jnp.minimum, [mins[p] for p in range(pairs)])

        def cond(state):
            step, nxt = state
            return jnp.logical_and(last_start - step * w > 0, nxt < SB_DEAD_LOG2)

        lax.while_loop(cond, body, (jnp.int32(0), live))
        for p in range(pairs):
            for h in range(subs):
                o_ref[0, rows_of(h), lanes_of(p)] = _merge_heads(
                    acc_ref[p, h], lane).astype(o_ref.dtype)


def _sb_attention(q, k, v, weights):
    b, s, wd = q.shape
    pairs = wd // LANES
    subs = SB_STEP_ROWS // SB_Q_TILE
    nq = s // SB_STEP_ROWS
    tile = lambda bi, i: (bi, i, 0)
    whole = lambda bi, i: (bi, 0, 0)
    once = pl.Buffered(1)
    conv_specs = [
        pl.BlockSpec((size, w.shape[1]),
                     lambda bi, i, last=w.shape[0] // size - 1: (jnp.minimum(bi * nq + i, last), 0))
        for w, size in zip(weights, _conversion_blocks(weights, b * nq))]
    return pl.pallas_call(
        functools.partial(_sb_kernel, len(weights)),
        grid=(b, nq),
        in_specs=[
            pl.BlockSpec((1, SB_STEP_ROWS, wd), tile),
            pl.BlockSpec((1, s, wd), whole, pipeline_mode=once),
            pl.BlockSpec((1, s, wd), whole, pipeline_mode=once),
        ] + conv_specs,
        out_specs=[pl.BlockSpec((1, SB_STEP_ROWS, wd), tile)] + conv_specs,
        out_shape=[jax.ShapeDtypeStruct((b, s, wd), BF16)]
        + [jax.ShapeDtypeStruct(w.shape, BF16) for w in weights],
        scratch_shapes=[
            pltpu.VMEM((SB_WINDOW, SB_WINDOW), BF16),
            pltpu.VMEM((subs, 2, 2 * SB_Q_TILE, SB_WINDOW), F32),
            pltpu.VMEM((pairs, subs, 2 * SB_Q_TILE, LANES), F32),
            pltpu.VMEM((pairs, subs, 2 * SB_Q_TILE, LANES), F32),
        ],
        compiler_params=pltpu.CompilerParams(
            dimension_semantics=("arbitrary", "arbitrary"), vmem_limit_bytes=VMEM_LIMIT),
        name="stick_breaking_attention",
    )(q, k, v, *weights)


def _t5_log_bucket_starts():
    exact = REL_BUCKETS // 2
    d = np.arange(exact, 2 * REL_MAX_DIST, dtype=np.float32)
    large = exact + (np.log(d / np.float32(exact)) / np.float32(math.log(REL_MAX_DIST / exact))
                     * np.float32(REL_BUCKETS - exact)).astype(np.int32)
    large = np.minimum(large, REL_BUCKETS - 1)
    assert large[0] == exact and np.all(np.diff(large) >= 0)
    return [int(d[np.argmax(large >= bk)]) for bk in range(exact + 1, REL_BUCKETS)]


def _t5_causal_bucket(distance):
    d = jnp.maximum(distance, 0)
    bucket = jnp.minimum(d, REL_BUCKETS // 2)
    for start in _t5_log_bucket_starts():
        bucket = bucket + jnp.where(d >= start, 1, 0)
    return bucket


def _swa_kernel(sinks_ref, relb_ref, q_ref, kvc_ref, kvp_ref, o_ref, bias_ref):
    bi = pl.program_id(0)
    qi = pl.program_id(1)
    t = ATT_BLOCK
    kvw = kvc_ref.shape[-1] // 4
    group = SWA_HEADS // SWA_KV_HEADS

    @pl.when(jnp.logical_and(bi == 0, qi == 0))
    def _fill_bias():
        r = lax.broadcasted_iota(jnp.int32, (t, 2 * t), 0)
        c = lax.broadcasted_iota(jnp.int32, (t, 2 * t), 1)
        dist = r + t - c
        bucket = _t5_causal_bucket(dist)
        valid = jnp.logical_and(dist >= 0, dist < WINDOW)
        for h in range(SWA_HEADS):
            bias = jnp.zeros((t, 2 * t), F32)
            for bk in range(REL_BUCKETS):
                bias = jnp.where(bucket == bk, relb_ref[bk, h], bias)
            bias_ref[h] = jnp.where(valid, bias, MASK_NEG)

    tiles = q_ref.shape[1] // t
    rows_of = lambda n: slice(n * t, (n + 1) * t)
    lane = lax.broadcasted_iota(jnp.int32, (t, LANES), 1)
    row2 = lax.broadcasted_iota(jnp.int32, (2 * t, 1), 0)
    col2 = lax.broadcasted_iota(jnp.int32, (2 * t, 2 * t), 1)
    prev_pen = jnp.where(jnp.logical_and(col2 < t, qi == 0), MASK_NEG, 0.0)

    def kv_window(n, variant):
        cols = slice(variant * kvw, (variant + 1) * kvw)
        before = kvc_ref[0, rows_of(n - 1), cols] if n else kvp_ref[0, :, cols]
        return jnp.concatenate([before, kvc_ref[0, rows_of(n), cols]], axis=0)

    items = [(n, half, j) for n in range(tiles) for half in range(2) for j in range(SWA_KV_HEADS)]
    heads_of = lambda half, j: [h for h in range(j * group, (j + 1) * group) if h % 2 == half]

    def probabilities(i):
        n, half, j = items[i]
        heads = heads_of(half, j)
        in_half = (lane >= HEAD_DIM) if half else (lane < HEAD_DIM)
        qq = jnp.concatenate(
            [jnp.where(in_half, q_ref[0, rows_of(n), (h // 2) * LANES:(h // 2 + 1) * LANES], 0)
             for h in heads], axis=0)
        s = _dot_nt(qq, kv_window(n, 0 + int(j != half)))
        s = s + jnp.concatenate([bias_ref[h] for h in heads], axis=0)
        if n == 0:
            s = s + prev_pen
        sink = jnp.where(row2 < t, sinks_ref[heads[0]], sinks_ref[heads[1]])
        m = jnp.maximum(jnp.max(s, axis=-1, keepdims=True), sink)
        p = jnp.exp(s - m)
        denom = jnp.sum(p, axis=-1, keepdims=True) + jnp.exp(sink - m)
        return p.astype(BF16), 1.0 / denom

    def values(i, pd):
        n, half, j = items[i]
        p, inv_denom = pd
        return _dot(p, kv_window(n, 2 + int(j != half))) * inv_denom

    outs = _software_pipeline(len(items), [probabilities, values])
    results = {}
    for i, (n, half, j) in enumerate(items):
        h0, h1 = heads_of(half, j)
        results[n, h0], results[n, h1] = outs[i][:t], outs[i][t:]

    for n in range(tiles):
        for pair in range(SWA_HEADS // 2):
            o_ref[0, rows_of(n), pair * LANES:(pair + 1) * LANES] = jnp.where(
                lane < HEAD_DIM, results[n, 2 * pair], results[n, 2 * pair + 1]).astype(o_ref.dtype)


def _swa_attention(q, kv, sinks, rel_bias):
    b, s, wd = q.shape
    tiles = SWA_STEP_ROWS // ATT_BLOCK
    nq = s // SWA_STEP_ROWS
    cur = lambda bi, i: (bi, i, 0)
    prev = lambda bi, i: (bi, jnp.maximum(i * tiles - 1, 0), 0)
    smem = pl.BlockSpec(memory_space=pltpu.SMEM)
    return pl.pallas_call(
        _swa_kernel,
        grid=(b, nq),
        in_specs=[
            smem, smem,
            pl.BlockSpec((1, SWA_STEP_ROWS, wd), cur),
            pl.BlockSpec((1, SWA_STEP_ROWS, kv.shape[-1]), cur),
            pl.BlockSpec((1, ATT_BLOCK, kv.shape[-1]), prev),
        ],
        out_specs=pl.BlockSpec((1, SWA_STEP_ROWS, wd), cur),
        out_shape=jax.ShapeDtypeStruct((b, s, wd), BF16),
        scratch_shapes=[pltpu.VMEM((SWA_HEADS, ATT_BLOCK, 2 * ATT_BLOCK), F32)],
        compiler_params=pltpu.CompilerParams(
            dimension_semantics=("arbitrary", "arbitrary"), vmem_limit_bytes=VMEM_LIMIT),
        name="sliding_window_attention",
    )(sinks, rel_bias, q, kv, kv)


def _mix_ffn_kernel(ha_ref, sb_ref, sw_ref, sbg_ref, swg_ref, wo_ref, l1g_ref, l1b_ref,
                    wgu_ref, wd_ref, l2g_ref, l2b_ref, o_ref):
    d_ff = wd_ref.shape[0]
    bounds = list(range(0, d_ff, FFN_CHUNK)) + [d_ff]
    n_chunks = len(bounds) - 1
    rows = ha_ref.shape[0] // FFN_PARTS
    part = lambda i: slice(i * rows, (i + 1) * rows)
    mix, h1b, acc = {}, {}, {}

    def out_projection(i):
        sb = _rms_norm(sb_ref[part(i), :].astype(F32), sbg_ref[...]).astype(BF16)
        sw = _rms_norm(sw_ref[part(i), :].astype(F32), swg_ref[...]).astype(BF16)
        mix[i] = _dot(jnp.concatenate([sb, sw], axis=1), wo_ref[...])

    def first_norm(i):
        h1 = _layer_norm(ha_ref[part(i), :] + mix.pop(i), l1g_ref[...], l1b_ref[...])
        h1b[i] = h1.astype(BF16)
        acc[i] = ALPHA * h1

    items = [(i, c) for i in range(FFN_PARTS) for c in range(n_chunks)]

    def activation(n):
        i, c = items[n]
        if c == 0:
            first_norm(i)
        lo, hi = bounds[c], bounds[c + 1]
        gate = _dot(h1b[i], wgu_ref[:, lo:hi])
        up = _dot(h1b[i], wgu_ref[:, d_ff + lo:d_ff + hi])
        return (gate / (1.0 + jnp.exp(-gate)) * up).astype(BF16)

    def down(n, act):
        i, c = items[n]
        acc[i] = acc[i] + _dot(act, wd_ref[bounds[c]:bounds[c + 1], :])
        return i if c == n_chunks - 1 else None

    def second_norm(n, i):
        if i is not None:
            o_ref[part(i), :] = _layer_norm(acc.pop(i), l2g_ref[...], l2b_ref[...])

    for i in range(FFN_PARTS):
        out_projection(i)
    _software_pipeline(len(items), [activation, down, second_norm])


def _mix_ffn_block(ha, sb2, sw2, sb_g, sw_g, w_out_b, ln1_g, ln1_b, w_gate_up_b, w_down_b,
                   ln2_g, ln2_b):
    n, d = ha.shape
    row = lambda i: (i, 0)
    fixed = lambda i: (0, 0)
    vec = lambda a: pl.BlockSpec((1, a.shape[-1]), fixed)
    weight = lambda a: pl.BlockSpec(a.shape, fixed, pipeline_mode=pl.Buffered(1))
    return pl.pallas_call(
        _mix_ffn_kernel,
        grid=(n // FFN_ROW_TILE,),
        in_specs=[
            pl.BlockSpec((FFN_ROW_TILE, d), row),
            pl.BlockSpec((FFN_ROW_TILE, sb2.shape[-1]), row),
            pl.BlockSpec((FFN_ROW_TILE, sw2.shape[-1]), row),
            vec(sb_g), vec(sw_g), weight(w_out_b), vec(ln1_g), vec(ln1_b),
            weight(w_gate_up_b), weight(w_down_b), vec(ln2_g), vec(ln2_b),
        ],
        out_specs=pl.BlockSpec((FFN_ROW_TILE, d), row),
        out_shape=jax.ShapeDtypeStruct((n, d), F32),
        compiler_params=pltpu.CompilerParams(
            dimension_semantics=("arbitrary",), vmem_limit_bytes=VMEM_LIMIT),
        name="mix_ffn",
    )(ha, sb2, sw2, sb_g, sw_g, w_out_b, ln1_g, ln1_b, w_gate_up_b, w_down_b, ln2_g, ln2_b)


def kernel(x, ln_in_g, ln_in_b, w_in, sb_norm_g, swa_norm_g, sinks, rel_bias, w_out,
           ln1_g, ln1_b, w_gate_up, w_down, ln2_g, ln2_b):
    b, s, d = x.shape
    assert w_in.shape[0] == DEPTH == 1
    sb_w = SB_HEADS * HEAD_DIM
    sw_w = SWA_HEADS * HEAD_DIM
    kv_w = SWA_KV_HEADS * HEAD_DIM
    assert w_in.shape[-1] == 3 * sb_w + sw_w + 2 * kv_w
    assert kv_w == LANES and SWA_HEADS // SWA_KV_HEADS == 4
    assert s % SB_WINDOW == 0 and s % SB_STEP_ROWS == 0 and s % SWA_STEP_ROWS == 0
    assert (b * s) % ROW_TILE == 0 and (b * s) % FFN_ROW_TILE == 0 and WINDOW <= ATT_BLOCK

    x2 = x.reshape(b * s, d)
    row_vec = lambda a: a.reshape(1, -1)

    names = ("q_sb", "k_sb", "v_sb", "q_sw", "k_sw", "v_sw")
    widths = (sb_w, sb_w, sb_w, sw_w, kv_w, kv_w)
    splits, lo = {}, 0
    for name, wd in zip(names, widths):
        splits[name] = (lo, lo + wd)
        lo += wd
    ha, q_sb, k_sb, v_sb, q_sw, kv_sw = _in_projection(
        x2, row_vec(ln_in_g), row_vec(ln_in_b), w_in[0], splits)

    to3 = lambda a: a.reshape(b, s, a.shape[-1])
    sb_out, w_out_b, w_gate_up_b, w_down_b = _sb_attention(
        to3(q_sb), to3(k_sb), to3(v_sb), [w_out[0], w_gate_up[0], w_down[0]])
    sw_out = _swa_attention(to3(q_sw), to3(kv_sw), sinks[0], rel_bias)

    out = _mix_ffn_block(ha, sb_out.reshape(b * s, sb_w), sw_out.reshape(b * s, sw_w),
                         row_vec(sb_norm_g[0]), row_vec(swa_norm_g[0]), w_out_b,
                         row_vec(ln1_g[0]), row_vec(ln1_b[0]), w_gate_up_b, w_down_b,
                         row_vec(ln2_g[0]), row_vec(ln2_b[0]))
    return out.reshape(b, s, d)
```

```python
import functools
import math

import jax
import jax.numpy as jnp
import numpy as np
from jax import lax
from jax.experimental import pallas as pl
from jax.experimental.pallas import tpu as pltpu

F32 = jnp.float32
BF16 = jnp.bfloat16

HEAD_DIM = 64
SB_HEADS = 8
SWA_HEADS = 8
SWA_KV_HEADS = 2
WINDOW = 128
REL_BUCKETS = 32
REL_MAX_DIST = 128
LN_EPS = 1e-5
RMS_EPS = 1e-6
DEPTH = 1
ALPHA = (2 * DEPTH) ** 0.25
QK_SCALE = HEAD_DIM ** -0.5
LOG2E = math.log2(math.e)

LANES = 128
BF16_ROWS = 16
MXU_WIDTH = 256
ATT_BLOCK = 128
SB_WINDOW = 256
SB_Q_TILE = 64
SB_STEP_ROWS = 256
ROW_TILE = 1024
PROJ_PARTS = 2
TAIL_ROWS = 512
FFN_CHUNK = 4 * MXU_WIDTH
VMEM_LIMIT = 56 * 1024 * 1024

SB_DEAD_LOG2 = 127.0
MASK_NEG = -1e30


def _layer_norm(x, g, b):
    mu = jnp.mean(x, axis=-1, keepdims=True)
    xc = x - mu
    var = jnp.mean(xc * xc, axis=-1, keepdims=True)
    return xc * lax.rsqrt(var + LN_EPS) * g + b


def _rms_norm(x, g):
    ms = jnp.mean(x * x, axis=-1, keepdims=True)
    return x * lax.rsqrt(ms + RMS_EPS) * g


def _dot(a, b):
    return jnp.dot(a, b, preferred_element_type=F32)


def _dot_nt(a, b):
    return lax.dot_general(a, b, (((1,), (1,)), ((), ())), preferred_element_type=F32)


def _pipeline_schedule(n, stages):
    vals = [dict() for _ in stages]
    thunks = []
    for step in range(n + len(stages) - 1):
        for s, stage in enumerate(stages):
            i = step - s
            if 0 <= i < n:
                def thunk(s=s, i=i, stage=stage):
                    vals[s][i] = stage(i, vals[s - 1].pop(i)) if s else stage(i)
                thunks.append(thunk)
    return thunks, vals[-1]


def _software_pipeline(n, stages):
    thunks, results = _pipeline_schedule(n, stages)
    for thunk in thunks:
        thunk()
    return results


def _interleave(xs, ys):
    merged, taken = [], 0
    for i, x in enumerate(xs):
        merged.append(x)
        upto = (i + 1) * len(ys) // len(xs)
        merged.extend(ys[taken:upto])
        taken = upto
    return merged


def _inproj_kernel(splits, x_ref, g_ref, b_ref, w32_ref,
                   ha_ref, qsb_ref, ksb_ref, vsb_ref, qsw_ref, kv_ref, w_ref):
    rows = x_ref.shape[0] // PROJ_PARTS
    part = lambda i: slice(i * rows, (i + 1) * rows)
    assert splits["k_sw"][1] == splits["v_sw"][0]

    @pl.when(pl.program_id(0) == 0)
    def _convert_weight():
        w_ref[...] = w32_ref[...].astype(BF16)

    def normed(i):
        h32 = _layer_norm(x_ref[part(i), :], g_ref[...], b_ref[...])
        ha_ref[part(i), :] = ALPHA * h32
        return h32.astype(BF16)

    def projected(i, h):
        r = part(i)
        proj = lambda lo, hi: _dot(h, w_ref[:, lo:hi])
        qsb_ref[r, :] = (proj(*splits["q_sb"]) * QK_SCALE).astype(BF16)
        ksb_ref[r, :] = proj(*splits["k_sb"]).astype(BF16)
        vsb_ref[r, :] = proj(*splits["v_sb"]).astype(BF16)
        qsw_ref[r, :] = (proj(*splits["q_sw"]) * QK_SCALE).astype(BF16)
        kv = proj(splits["k_sw"][0], splits["v_sw"][1])
        kvw = kv.shape[-1] // 2
        k, v = kv[:, :kvw], kv[:, kvw:]
        kv_ref[r, 0 * kvw:1 * kvw] = k.astype(BF16)
        kv_ref[r, 1 * kvw:2 * kvw] = pltpu.roll(k, HEAD_DIM, axis=1).astype(BF16)
        kv_ref[r, 2 * kvw:3 * kvw] = v.astype(BF16)
        kv_ref[r, 3 * kvw:4 * kvw] = pltpu.roll(v, HEAD_DIM, axis=1).astype(BF16)

    _software_pipeline(PROJ_PARTS, [normed, projected])


def _in_projection(x2, g, b, w_in, splits):
    n, d = x2.shape
    width = lambda name: splits[name][1] - splits[name][0]
    out_w = [width("q_sb"), width("k_sb"), width("v_sb"), width("q_sw"), 4 * width("k_sw")]
    row = lambda i: (i, 0)
    fixed = lambda i: (0, 0)
    return pl.pallas_call(
        functools.partial(_inproj_kernel, splits),
        grid=(n // ROW_TILE,),
        in_specs=[
            pl.BlockSpec((ROW_TILE, d), row),
            pl.BlockSpec((1, d), fixed),
            pl.BlockSpec((1, d), fixed),
            pl.BlockSpec(w_in.shape, fixed, pipeline_mode=pl.Buffered(1)),
        ],
        out_specs=[pl.BlockSpec((ROW_TILE, d), row)]
        + [pl.BlockSpec((ROW_TILE, w), row) for w in out_w],
        out_shape=[jax.ShapeDtypeStruct((n, d), F32)]
        + [jax.ShapeDtypeStruct((n, w), BF16) for w in out_w],
        scratch_shapes=[pltpu.VMEM(w_in.shape, BF16)],
        compiler_params=pltpu.CompilerParams(
            dimension_semantics=("arbitrary",), vmem_limit_bytes=VMEM_LIMIT),
        name="ln_in_proj",
    )(x2, g, b, w_in)


def _conversion_blocks(weights, steps):
    sizes = []
    for w in weights:
        rows = w.shape[0]
        sizes.append(next(r for r in range(BF16_ROWS, rows + 1, BF16_ROWS)
                          if rows % r == 0 and rows // r <= steps))
    return sizes


def _split_heads(x, lane):
    zero = jnp.zeros_like(x)
    return jnp.concatenate([jnp.where(lane < HEAD_DIM, x, zero),
                            jnp.where(lane >= HEAD_DIM, x, zero)], axis=0)


def _merge_heads(o2, lane):
    t = o2.shape[0] // 2
    return jnp.where(lane < HEAD_DIM, o2[:t], o2[t:])


def _sb_masks(valid):
    return jnp.where(valid, LOG2E, 0.0), jnp.where(valid, 0.0, MASK_NEG)


def _sb_scores(qq, k, keep):
    z = _dot_nt(qq, k)
    zl = z * LOG2E
    c_nat = jnp.maximum(z, 0.0) + jnp.log(1.0 + jnp.exp2(-jnp.abs(zl)))
    cm = c_nat * keep
    hi = cm.astype(BF16)
    lo = (cm - hi.astype(F32)).astype(BF16)
    return zl - cm, hi, lo, cm[:, :LANES]


def _sb_weights(scores, tri, kills, carrieds):
    sums = _dot(jnp.concatenate([x for sc in scores for x in sc[1:3]], axis=0), tri)
    rows = scores[0][0].shape[0]
    out = []
    for n, ((log_beta, _, _, cm0), kill, carried) in enumerate(zip(scores, kills, carrieds)):
        later = sums[2 * n * rows:(2 * n + 1) * rows] + sums[(2 * n + 1) * rows:(2 * n + 2) * rows]
        log_a = (log_beta - later) + kill
        if carried is not None:
            log_a = log_a - jnp.concatenate([carried] * (log_a.shape[1] // LANES), axis=1)
        out.append((jnp.exp2(log_a).astype(BF16), later[:, :LANES] + cm0))
    return out


def _sb_kernel(n_conv, q_ref, k_ref, v_ref, *refs):
    conv_in, o_ref, conv_out = refs[:n_conv], refs[n_conv], refs[n_conv + 1:2 * n_conv + 1]
    tri_ref, mask_ref, acc_ref, car_ref = refs[2 * n_conv + 1:]
    for src, dst in zip(conv_in, conv_out):
        dst[...] = src[...].astype(BF16)

    bi = pl.program_id(0)
    qi = pl.program_id(1)
    tq, w = SB_Q_TILE, SB_WINDOW
    subs = q_ref.shape[1] // tq
    pairs = q_ref.shape[-1] // LANES
    row0 = qi * (subs * tq)

    @pl.when(jnp.logical_and(bi == 0, qi == 0))
    def _fill_tri():
        j = lax.broadcasted_iota(jnp.int32, (w, w), 0)
        s = lax.broadcasted_iota(jnp.int32, (w, w), 1)
        tri_ref[...] = jnp.where(j > s, 1.0, 0.0).astype(BF16)

    lane = lax.broadcasted_iota(jnp.int32, (tq, LANES), 1)
    lanes_of = lambda p: slice(p * LANES, (p + 1) * LANES)
    rows_of = lambda h: slice(h * tq, (h + 1) * tq)
    first_start = lambda h: pl.multiple_of(jnp.maximum(row0 + (h + 1) * tq - w, 0), tq)

    def key_positions(start):
        return start + lax.broadcasted_iota(jnp.int32, (2 * tq, w), 1)

    @pl.when(row0 <= w)
    def _fill_masks():
        row = lax.broadcasted_iota(jnp.int32, (2 * tq, w), 0) & (tq - 1)
        for h in range(subs):
            mask_ref[h, 0], mask_ref[h, 1] = _sb_masks(
                key_positions(first_start(h)) < row0 + h * tq + row)

    def scores(p, starts, keeps):
        return [_sb_scores(_split_heads(q_ref[0, rows_of(h), lanes_of(p)], lane),
                           k_ref[0, pl.ds(starts[h], w), lanes_of(p)], keeps[h])
                for h in range(subs)]

    first_starts = [first_start(h) for h in range(subs)]

    def first_scores(p):
        return scores(p, first_starts, [mask_ref[h, 0] for h in range(subs)])

    def first_weights(p, sc):
        return _sb_weights(sc, tri_ref[...], [mask_ref[h, 1] for h in range(subs)], [None] * subs)

    def first_values(p, aws):
        least = None
        for h, (a, total) in enumerate(aws):
            o2 = _dot(a, v_ref[0, pl.ds(first_starts[h], w), lanes_of(p)])
            o_ref[0, rows_of(h), lanes_of(p)] = _merge_heads(o2, lane).astype(o_ref.dtype)
            acc_ref[p, h] = o2
            car_ref[p, h] = total
            m = jnp.min(total[:, 0:1])
            least = m if least is None else jnp.minimum(least, m)
        return least

    first = _software_pipeline(pairs, [first_scores, first_weights, first_values])
    live = functools.reduce(jnp.minimum, [first[p] for p in range(pairs)])
    last_start = first_starts[-1]

    @pl.when(jnp.logical_and(last_start > 0, live < SB_DEAD_LOG2))
    def _walk_back():
        for p in range(pairs):
            for h in range(subs):
                car_ref[p, h] = jnp.broadcast_to(car_ref[p, h][:, 0:1], (2 * tq, LANES))

        def body(state):
            step, _ = state
            done = [jnp.maximum(first_start(h) - step * w, 0) for h in range(subs)]
            starts = [pl.multiple_of(jnp.maximum(d - w, 0), tq) for d in done]
            masks = [_sb_masks(key_positions(s) < d) for s, d in zip(starts, done)]

            def more_scores(p):
                return scores(p, starts, [m[0] for m in masks])

            def weights(p, sc):
                return _sb_weights(sc, tri_ref[...], [m[1] for m in masks],
                                   [car_ref[p, h] for h in range(subs)])

            def values(p, aws):
                least = None
                for h, (a, total) in enumerate(aws):
                    acc_ref[p, h] += _dot(a, v_ref[0, pl.ds(starts[h], w), lanes_of(p)])
                    carried = car_ref[p, h] + total[:, 0:1]
                    car_ref[p, h] = carried
                    m = jnp.min(carried)
                    least = m if least is None else jnp.minimum(least, m)
                return least

            mins = _software_pipeline(pairs, [more_scores, weights, values])
            return step + 1, functools.reduce(jnp.minimum, [mins[p] for p in range(pairs)])

        def cond(state):
            step, nxt = state
            return jnp.logical_and(last_start - step * w > 0, nxt < SB_DEAD_LOG2)

        lax.while_loop(cond, body, (jnp.int32(0), live))
        for p in range(pairs):
            for h in range(subs):
                o_ref[0, rows_of(h), lanes_of(p)] = _merge_heads(
                    acc_ref[p, h], lane).astype(o_ref.dtype)


def _sb_attention(q, k, v, weights):
    b, s, wd = q.shape
    pairs = wd // LANES
    subs = SB_STEP_ROWS // SB_Q_TILE
    nq = s // SB_STEP_ROWS
    tile = lambda bi, i: (bi, i, 0)
    whole = lambda bi, i: (bi, 0, 0)
    once = pl.Buffered(1)
    conv_specs = [
        pl.BlockSpec((size, w.shape[1]),
                     lambda bi, i, last=w.shape[0] // size - 1: (jnp.minimum(bi * nq + i, last), 0))
        for w, size in zip(weights, _conversion_blocks(weights, b * nq))]
    return pl.pallas_call(
        functools.partial(_sb_kernel, len(weights)),
        grid=(b, nq),
        in_specs=[
            pl.BlockSpec((1, SB_STEP_ROWS, wd), tile),
            pl.BlockSpec((1, s, wd), whole, pipeline_mode=once),
            pl.BlockSpec((1, s, wd), whole, pipeline_mode=once),
        ] + conv_specs,
        out_specs=[pl.BlockSpec((1, SB_STEP_ROWS, wd), tile)] + conv_specs,
        out_shape=[jax.ShapeDtypeStruct((b, s, wd), BF16)]
        + [jax.ShapeDtypeStruct(w.shape, BF16) for w in weights],
        scratch_shapes=[
            pltpu.VMEM((SB_WINDOW, SB_WINDOW), BF16),
            pltpu.VMEM((subs, 2, 2 * SB_Q_TILE, SB_WINDOW), F32),
            pltpu.VMEM((pairs, subs, 2 * SB_Q_TILE, LANES), F32),
            pltpu.VMEM((pairs, subs, 2 * SB_Q_TILE, LANES), F32),
        ],
        compiler_params=pltpu.CompilerParams(
            dimension_semantics=("arbitrary", "arbitrary"), vmem_limit_bytes=VMEM_LIMIT),
        name="stick_breaking_attention",
    )(q, k, v, *weights)


def _t5_log_bucket_starts():
    exact = REL_BUCKETS // 2
    d = np.arange(exact, 2 * REL_MAX_DIST, dtype=np.float32)
    large = exact + (np.log(d / np.float32(exact)) / np.float32(math.log(REL_MAX_DIST / exact))
                     * np.float32(REL_BUCKETS - exact)).astype(np.int32)
    large = np.minimum(large, REL_BUCKETS - 1)
    assert large[0] == exact and np.all(np.diff(large) >= 0)
    return [int(d[np.argmax(large >= bk)]) for bk in range(exact + 1, REL_BUCKETS)]


def _t5_causal_bucket(distance):
    d = jnp.maximum(distance, 0)
    bucket = jnp.minimum(d, REL_BUCKETS // 2)
    for start in _t5_log_bucket_starts():
        bucket = bucket + jnp.where(d >= start, 1, 0)
    return bucket


def _swa_fill_bias(relb_ref, bias_ref):
    t = ATT_BLOCK
    r = lax.broadcasted_iota(jnp.int32, (t, 2 * t), 0)
    c = lax.broadcasted_iota(jnp.int32, (t, 2 * t), 1)
    dist = r + t - c
    bucket = _t5_causal_bucket(dist)
    valid = jnp.logical_and(dist >= 0, dist < WINDOW)
    for h in range(SWA_HEADS):
        bias = jnp.zeros((t, 2 * t), F32)
        for bk in range(REL_BUCKETS):
            bias = jnp.where(bucket == bk, relb_ref[bk, h], bias)
        bias_ref[h] = jnp.where(valid, bias, MASK_NEG)


def _swa_schedule(sinks_ref, q_ref, kvc_ref, kvp_ref, bias_ref, o_ref, no_previous):
    t = ATT_BLOCK
    kvw = kvc_ref.shape[-1] // 4
    group = SWA_HEADS // SWA_KV_HEADS
    tiles = q_ref.shape[0] // t
    rows_of = lambda n: slice(n * t, (n + 1) * t)
    lane = lax.broadcasted_iota(jnp.int32, (t, LANES), 1)
    row2 = lax.broadcasted_iota(jnp.int32, (2 * t, 1), 0)
    col2 = lax.broadcasted_iota(jnp.int32, (2 * t, 2 * t), 1)
    prev_pen = jnp.where(jnp.logical_and(col2 < t, no_previous), MASK_NEG, 0.0)

    def kv_window(n, variant):
        cols = slice(variant * kvw, (variant + 1) * kvw)
        before = kvc_ref[rows_of(n - 1), cols] if n else kvp_ref[:, cols]
        return jnp.concatenate([before, kvc_ref[rows_of(n), cols]], axis=0)

    items = [(n, half, j) for n in range(tiles) for half in range(2) for j in range(SWA_KV_HEADS)]
    heads_of = lambda half, j: [h for h in range(j * group, (j + 1) * group) if h % 2 == half]

    def probabilities(i):
        n, half, j = items[i]
        heads = heads_of(half, j)
        in_half = (lane >= HEAD_DIM) if half else (lane < HEAD_DIM)
        qq = jnp.concatenate(
            [jnp.where(in_half, q_ref[rows_of(n), (h // 2) * LANES:(h // 2 + 1) * LANES], 0)
             for h in heads], axis=0)
        s = _dot_nt(qq, kv_window(n, 0 + int(j != half)))
        s = s + jnp.concatenate([bias_ref[h] for h in heads], axis=0)
        if n == 0:
            s = s + prev_pen
        sink = jnp.where(row2 < t, sinks_ref[heads[0]], sinks_ref[heads[1]])
        m = jnp.maximum(jnp.max(s, axis=-1, keepdims=True), sink)
        p = jnp.exp(s - m)
        denom = jnp.sum(p, axis=-1, keepdims=True) + jnp.exp(sink - m)
        return p.astype(BF16), 1.0 / denom

    def values(i, pd):
        n, half, j = items[i]
        p, inv_denom = pd
        return _dot(p, kv_window(n, 2 + int(j != half))) * inv_denom

    thunks, outs = _pipeline_schedule(len(items), [probabilities, values])

    def store():
        for i, (n, half, j) in enumerate(items):
            if half == 0:
                even, odd = outs[i], outs[items.index((n, 1, j))]
                for e, h in enumerate(heads_of(0, j)):
                    o_ref[rows_of(n), (h // 2) * LANES:(h // 2 + 1) * LANES] = jnp.where(
                        lane < HEAD_DIM, even[e * t:(e + 1) * t], odd[e * t:(e + 1) * t]
                    ).astype(o_ref.dtype)

    return thunks + [store]


def _tail_schedule(ha_ref, sb_ref, sw_ref, sbg_ref, swg_ref, wo_ref, l1g_ref, l1b_ref,
                   wgu_ref, wd_ref, l2g_ref, l2b_ref, o_ref):
    d = ha_ref.shape[1]
    d_ff = wd_ref.shape[0]
    mw = MXU_WIDTH
    bounds = list(range(0, d_ff, FFN_CHUNK)) + [d_ff]
    n_chunks = len(bounds) - 1
    out_tiles = range(d // mw)
    st = {}

    def merged():
        sb = _rms_norm(sb_ref[...].astype(F32), sbg_ref[...]).astype(BF16)
        sw = _rms_norm(sw_ref[...].astype(F32), swg_ref[...]).astype(BF16)
        st["merged"] = jnp.concatenate([sb, sw], axis=1)

    def out_projection(n):
        st["mix", n] = _dot(st["merged"], wo_ref[:, n * mw:(n + 1) * mw])

    def first_norm():
        mix = jnp.concatenate([st.pop(("mix", n)) for n in out_tiles], axis=1)
        st.pop("merged")
        h1 = _layer_norm(ha_ref[...] + mix, l1g_ref[...], l1b_ref[...])
        st["h1b"] = h1.astype(BF16)
        for n in out_tiles:
            st["acc", n] = ALPHA * h1[:, n * mw:(n + 1) * mw]

    def activation(c, n):
        lo = bounds[c] + n * mw
        gate = _dot(st["h1b"], wgu_ref[:, lo:lo + mw])
        up = _dot(st["h1b"], wgu_ref[:, d_ff + lo:d_ff + lo + mw])
        st["act", c, n] = (gate / (1.0 + jnp.exp(-gate)) * up).astype(BF16)

    def down(c, n):
        tiles = range((bounds[c + 1] - bounds[c]) // mw)
        if n == 0:
            st["act", c] = jnp.concatenate([st.pop(("act", c, k)) for k in tiles], axis=1)
        st["acc", n] = st["acc", n] + _dot(st["act", c], wd_ref[bounds[c]:bounds[c + 1],
                                                               n * mw:(n + 1) * mw])

    def second_norm():
        y = jnp.concatenate([st.pop(("acc", n)) for n in out_tiles], axis=1)
        o_ref[...] = _layer_norm(y, l2g_ref[...], l2b_ref[...])

    act_thunks = [[functools.partial(activation, c, n)
                   for n in range((bounds[c + 1] - bounds[c]) // mw)] for c in range(n_chunks)]
    down_thunks = [[functools.partial(down, c, n) for n in out_tiles] for c in range(n_chunks)]
    thunks = [merged] + [functools.partial(out_projection, n) for n in out_tiles] + [first_norm]
    thunks += act_thunks[0]
    for c in range(1, n_chunks):
        thunks += _interleave(act_thunks[c], down_thunks[c - 1])
    return thunks + down_thunks[-1] + [second_norm]


def _attn_tail_kernel(tiles_per_seq, last_tile, sinks_ref, relb_ref, q_ref, kvc_ref, kvp_ref,
                      ha_ref, sb_ref, sbg_ref, swg_ref, wo_ref, l1g_ref, l1b_ref,
                      wgu_ref, wd_ref, l2g_ref, l2b_ref, o_ref, bias_ref, sw_ref):
    t = pl.program_id(0)

    @pl.when(t == 0)
    def _first_step():
        _swa_fill_bias(relb_ref, bias_ref)
        sw_ref[1] = jnp.zeros(sw_ref.shape[1:], sw_ref.dtype)

    no_previous = (jnp.minimum(t, last_tile) % tiles_per_seq) == 0
    attention = _swa_schedule(sinks_ref, q_ref.at[0], kvc_ref.at[0], kvp_ref.at[0], bias_ref,
                              sw_ref.at[t % 2], no_previous)
    tail = _tail_schedule(ha_ref, sb_ref, sw_ref.at[(t + 1) % 2], sbg_ref, swg_ref, wo_ref,
                          l1g_ref, l1b_ref, wgu_ref, wd_ref, l2g_ref, l2b_ref, o_ref)
    for thunk in _interleave(tail, attention):
        thunk()


def _attn_tail_block(q_sw, kv_sw, sinks, rel_bias, ha, sb2, sb_g, sw_g, w_out_b, ln1_g, ln1_b,
                     w_gate_up_b, w_down_b, ln2_g, ln2_b):
    b, s, wd = q_sw.shape
    n, d = ha.shape
    tiles_per_seq = s // TAIL_ROWS
    last_tile = n // TAIL_ROWS - 1
    key_tiles = TAIL_ROWS // ATT_BLOCK
    att = lambda t: jnp.minimum(t, last_tile)
    cur = lambda t: (att(t) // tiles_per_seq, att(t) % tiles_per_seq, 0)
    prev = lambda t: (att(t) // tiles_per_seq,
                      jnp.maximum(att(t) % tiles_per_seq * key_tiles - 1, 0), 0)
    row = lambda t: (jnp.maximum(t - 1, 0), 0)
    fixed = lambda t: (0, 0)
    smem = pl.BlockSpec(memory_space=pltpu.SMEM)
    vec = lambda a: pl.BlockSpec((1, a.shape[-1]), fixed)
    weight = lambda a: pl.BlockSpec(a.shape, fixed, pipeline_mode=pl.Buffered(1))
    return pl.pallas_call(
        functools.partial(_attn_tail_kernel, tiles_per_seq, last_tile),
        grid=(last_tile + 2,),
        in_specs=[
            smem, smem,
            pl.BlockSpec((1, TAIL_ROWS, wd), cur),
            pl.BlockSpec((1, TAIL_ROWS, kv_sw.shape[-1]), cur),
            pl.BlockSpec((1, ATT_BLOCK, kv_sw.shape[-1]), prev),
            pl.BlockSpec((TAIL_ROWS, d), row),
            pl.BlockSpec((TAIL_ROWS, sb2.shape[-1]), row),
            vec(sb_g), vec(sw_g), weight(w_out_b), vec(ln1_g), vec(ln1_b),
            weight(w_gate_up_b), weight(w_down_b), vec(ln2_g), vec(ln2_b),
        ],
        out_specs=pl.BlockSpec((TAIL_ROWS, d), row),
        out_shape=jax.ShapeDtypeStruct((n, d), F32),
        scratch_shapes=[
            pltpu.VMEM((SWA_HEADS, ATT_BLOCK, 2 * ATT_BLOCK), F32),
            pltpu.VMEM((2, TAIL_ROWS, wd), BF16),
        ],
        compiler_params=pltpu.CompilerParams(
            dimension_semantics=("arbitrary",), vmem_limit_bytes=VMEM_LIMIT),
        name="swa_attention_and_dense_tail",
    )(sinks, rel_bias, q_sw, kv_sw, kv_sw, ha, sb2, sb_g, sw_g, w_out_b, ln1_g, ln1_b,
      w_gate_up_b, w_down_b, ln2_g, ln2_b)


def kernel(x, ln_in_g, ln_in_b, w_in, sb_norm_g, swa_norm_g, sinks, rel_bias, w_out,
           ln1_g, ln1_b, w_gate_up, w_down, ln2_g, ln2_b):
    b, s, d = x.shape
    assert w_in.shape[0] == DEPTH == 1
    sb_w = SB_HEADS * HEAD_DIM
    sw_w = SWA_HEADS * HEAD_DIM
    kv_w = SWA_KV_HEADS * HEAD_DIM
    assert w_in.shape[-1] == 3 * sb_w + sw_w + 2 * kv_w
    assert kv_w == LANES and SWA_HEADS // SWA_KV_HEADS == 4
    assert s % SB_WINDOW == 0 and s % SB_STEP_ROWS == 0 and s % TAIL_ROWS == 0
    assert (b * s) % ROW_TILE == 0 and WINDOW <= ATT_BLOCK and d % MXU_WIDTH == 0
    assert FFN_CHUNK % MXU_WIDTH == 0 and w_down.shape[1] % MXU_WIDTH == 0

    x2 = x.reshape(b * s, d)
    row_vec = lambda a: a.reshape(1, -1)

    names = ("q_sb", "k_sb", "v_sb", "q_sw", "k_sw", "v_sw")
    widths = (sb_w, sb_w, sb_w, sw_w, kv_w, kv_w)
    splits, lo = {}, 0
    for name, wd in zip(names, widths):
        splits[name] = (lo, lo + wd)
        lo += wd
    ha, q_sb, k_sb, v_sb, q_sw, kv_sw = _in_projection(
        x2, row_vec(ln_in_g), row_vec(ln_in_b), w_in[0], splits)

    to3 = lambda a: a.reshape(b, s, a.shape[-1])
    sb_out, w_out_b, w_gate_up_b, w_down_b = _sb_attention(
        to3(q_sb), to3(k_sb), to3(v_sb), [w_out[0], w_gate_up[0], w_down[0]])

    out = _attn_tail_block(to3(q_sw), to3(kv_sw), sinks[0], rel_bias, ha,
                           sb_out.reshape(b * s, sb_w),
                           row_vec(sb_norm_g[0]), row_vec(swa_norm_g[0]), w_out_b,
                           row_vec(ln1_g[0]), row_vec(ln1_b[0]), w_gate_up_b, w_down_b,
                           row_vec(ln2_g[0]), row_vec(ln2_b[0]))
    return out.reshape(b, s, d)
```

```python
import functools
import math

import jax
import jax.numpy as jnp
import numpy as np
from jax import lax
from jax.experimental import pallas as pl
from jax.experimental.pallas import tpu as pltpu

F32 = jnp.float32
BF16 = jnp.bfloat16

HEAD_DIM = 64
SB_HEADS = 8
SWA_HEADS = 8
SWA_KV_HEADS = 2
WINDOW = 128
REL_BUCKETS = 32
REL_MAX_DIST = 128
LN_EPS = 1e-5
RMS_EPS = 1e-6
DEPTH = 1
ALPHA = (2 * DEPTH) ** 0.25
QK_SCALE = HEAD_DIM ** -0.5
LOG2E = math.log2(math.e)

LANES = 128
BF16_ROWS = 16
MXU_WIDTH = 256
ATT_BLOCK = 128
SB_WINDOW = 256
SB_Q_TILE = 64
SB_STEP_ROWS = 256
ROW_TILE = 1024
PROJ_PARTS = 2
TAIL_ROWS = 512
FFN_CHUNK = 4 * MXU_WIDTH
VMEM_LIMIT = 56 * 1024 * 1024

SB_DEAD_LOG2 = 127.0
MASK_NEG = -1e30


def _layer_norm(x, g, b):
    mu = jnp.mean(x, axis=-1, keepdims=True)
    xc = x - mu
    var = jnp.mean(xc * xc, axis=-1, keepdims=True)
    return xc * lax.rsqrt(var + LN_EPS) * g + b


def _rms_norm(x, g):
    ms = jnp.mean(x * x, axis=-1, keepdims=True)
    return x * lax.rsqrt(ms + RMS_EPS) * g


def _dot(a, b):
    return jnp.dot(a, b, preferred_element_type=F32)


def _dot_nt(a, b):
    return lax.dot_general(a, b, (((1,), (1,)), ((), ())), preferred_element_type=F32)


def _pipeline_schedule(n, stages):
    vals = [dict() for _ in stages]
    thunks = []
    for step in range(n + len(stages) - 1):
        for s, stage in enumerate(stages):
            i = step - s
            if 0 <= i < n:
                def thunk(s=s, i=i, stage=stage):
                    vals[s][i] = stage(i, vals[s - 1].pop(i)) if s else stage(i)
                thunks.append(thunk)
    return thunks, vals[-1]


def _software_pipeline(n, stages):
    thunks, results = _pipeline_schedule(n, stages)
    for thunk in thunks:
        thunk()
    return results


def _interleave(xs, ys):
    merged, taken = [], 0
    for i, x in enumerate(xs):
        merged.append(x)
        upto = (i + 1) * len(ys) // len(xs)
        merged.extend(ys[taken:upto])
        taken = upto
    return merged


def _inproj_kernel(splits, x_ref, g_ref, b_ref, w32_ref,
                   ha_ref, qsb_ref, ksb_ref, vsb_ref, qsw_ref, kv_ref, w_ref):
    rows = x_ref.shape[0] // PROJ_PARTS
    part = lambda i: slice(i * rows, (i + 1) * rows)
    assert splits["k_sw"][1] == splits["v_sw"][0]

    @pl.when(pl.program_id(0) == 0)
    def _convert_weight():
        w_ref[...] = w32_ref[...].astype(BF16)

    def normed(i):
        h32 = _layer_norm(x_ref[part(i), :], g_ref[...], b_ref[...])
        ha_ref[part(i), :] = ALPHA * h32
        return h32.astype(BF16)

    def projected(i, h):
        r = part(i)
        proj = lambda lo, hi: _dot(h, w_ref[:, lo:hi])
        qsb_ref[r, :] = (proj(*splits["q_sb"]) * QK_SCALE).astype(BF16)
        ksb_ref[r, :] = proj(*splits["k_sb"]).astype(BF16)
        vsb_ref[r, :] = proj(*splits["v_sb"]).astype(BF16)
        qsw_ref[r, :] = (proj(*splits["q_sw"]) * QK_SCALE).astype(BF16)
        kv = proj(splits["k_sw"][0], splits["v_sw"][1])
        kvw = kv.shape[-1] // 2
        k, v = kv[:, :kvw], kv[:, kvw:]
        kv_ref[r, 0 * kvw:1 * kvw] = k.astype(BF16)
        kv_ref[r, 1 * kvw:2 * kvw] = pltpu.roll(k, HEAD_DIM, axis=1).astype(BF16)
        kv_ref[r, 2 * kvw:3 * kvw] = v.astype(BF16)
        kv_ref[r, 3 * kvw:4 * kvw] = pltpu.roll(v, HEAD_DIM, axis=1).astype(BF16)

    _software_pipeline(PROJ_PARTS, [normed, projected])


def _in_projection(x2, g, b, w_in, splits):
    n, d = x2.shape
    width = lambda name: splits[name][1] - splits[name][0]
    out_w = [width("q_sb"), width("k_sb"), width("v_sb"), width("q_sw"), 4 * width("k_sw")]
    row = lambda i: (i, 0)
    fixed = lambda i: (0, 0)
    return pl.pallas_call(
        functools.partial(_inproj_kernel, splits),
        grid=(n // ROW_TILE,),
        in_specs=[
            pl.BlockSpec((ROW_TILE, d), row),
            pl.BlockSpec((1, d), fixed),
            pl.BlockSpec((1, d), fixed),
            pl.BlockSpec(w_in.shape, fixed, pipeline_mode=pl.Buffered(1)),
        ],
        out_specs=[pl.BlockSpec((ROW_TILE, d), row)]
        + [pl.BlockSpec((ROW_TILE, w), row) for w in out_w],
        out_shape=[jax.ShapeDtypeStruct((n, d), F32)]
        + [jax.ShapeDtypeStruct((n, w), BF16) for w in out_w],
        scratch_shapes=[pltpu.VMEM(w_in.shape, BF16)],
        compiler_params=pltpu.CompilerParams(
            dimension_semantics=("arbitrary",), vmem_limit_bytes=VMEM_LIMIT),
        name="ln_in_proj",
    )(x2, g, b, w_in)


def _conversion_blocks(weights, steps):
    sizes = []
    for w in weights:
        rows = w.shape[0]
        sizes.append(next(r for r in range(BF16_ROWS, rows + 1, BF16_ROWS)
                          if rows % r == 0 and rows // r <= steps))
    return sizes


def _split_heads(x, lane):
    zero = jnp.zeros_like(x)
    return jnp.concatenate([jnp.where(lane < HEAD_DIM, x, zero),
                            jnp.where(lane >= HEAD_DIM, x, zero)], axis=0)


def _merge_heads(o2, lane):
    t = o2.shape[0] // 2
    return jnp.where(lane < HEAD_DIM, o2[:t], o2[t:])


def _sb_masks(valid):
    return jnp.where(valid, LOG2E, 0.0), jnp.where(valid, 0.0, MASK_NEG)


def _sb_scores(qq, k, keep):
    z = _dot_nt(qq, k)
    zl = z * LOG2E
    c_nat = jnp.maximum(z, 0.0) + jnp.log(1.0 + jnp.exp2(-jnp.abs(zl)))
    cm = c_nat * keep
    return zl - cm, cm.astype(BF16), cm[:, :LANES]


def _sb_weights(scores, tri, kills, carrieds):
    sums = _dot(jnp.concatenate([sc[1] for sc in scores], axis=0), tri)
    rows = scores[0][0].shape[0]
    out = []
    for n, ((log_beta, _, cm0), kill, carried) in enumerate(zip(scores, kills, carrieds)):
        later = sums[n * rows:(n + 1) * rows]
        log_a = (log_beta - later) + kill
        if carried is not None:
            log_a = log_a - jnp.concatenate([carried] * (log_a.shape[1] // LANES), axis=1)
        out.append((jnp.exp2(log_a).astype(BF16), later[:, :LANES] + cm0))
    return out


def _sb_kernel(n_conv, q_ref, k_ref, v_ref, *refs):
    conv_in, o_ref, conv_out = refs[:n_conv], refs[n_conv], refs[n_conv + 1:2 * n_conv + 1]
    tri_ref, mask_ref, acc_ref, car_ref = refs[2 * n_conv + 1:]
    for src, dst in zip(conv_in, conv_out):
        dst[...] = src[...].astype(BF16)

    bi = pl.program_id(0)
    qi = pl.program_id(1)
    tq, w = SB_Q_TILE, SB_WINDOW
    subs = q_ref.shape[1] // tq
    pairs = q_ref.shape[-1] // LANES
    row0 = qi * (subs * tq)

    @pl.when(jnp.logical_and(bi == 0, qi == 0))
    def _fill_tri():
        j = lax.broadcasted_iota(jnp.int32, (w, w), 0)
        s = lax.broadcasted_iota(jnp.int32, (w, w), 1)
        tri_ref[...] = jnp.where(j > s, 1.0, 0.0).astype(BF16)

    lane = lax.broadcasted_iota(jnp.int32, (tq, LANES), 1)
    lanes_of = lambda p: slice(p * LANES, (p + 1) * LANES)
    rows_of = lambda h: slice(h * tq, (h + 1) * tq)
    first_start = lambda h: pl.multiple_of(jnp.maximum(row0 + (h + 1) * tq - w, 0), tq)

    def key_positions(start):
        return start + lax.broadcasted_iota(jnp.int32, (2 * tq, w), 1)

    @pl.when(row0 <= w)
    def _fill_masks():
        row = lax.broadcasted_iota(jnp.int32, (2 * tq, w), 0) & (tq - 1)
        for h in range(subs):
            mask_ref[h, 0], mask_ref[h, 1] = _sb_masks(
                key_positions(first_start(h)) < row0 + h * tq + row)

    def scores(p, starts, keeps):
        return [_sb_scores(_split_heads(q_ref[0, rows_of(h), lanes_of(p)], lane),
                           k_ref[0, pl.ds(starts[h], w), lanes_of(p)], keeps[h])
                for h in range(subs)]

    first_starts = [first_start(h) for h in range(subs)]

    def first_scores(p):
        return scores(p, first_starts, [mask_ref[h, 0] for h in range(subs)])

    def first_weights(p, sc):
        return _sb_weights(sc, tri_ref[...], [mask_ref[h, 1] for h in range(subs)], [None] * subs)

    def first_values(p, aws):
        least = None
        for h, (a, total) in enumerate(aws):
            o2 = _dot(a, v_ref[0, pl.ds(first_starts[h], w), lanes_of(p)])
            o_ref[0, rows_of(h), lanes_of(p)] = _merge_heads(o2, lane).astype(o_ref.dtype)
            acc_ref[p, h] = o2
            car_ref[p, h] = total
            m = jnp.min(total[:, 0:1])
            least = m if least is None else jnp.minimum(least, m)
        return least

    first = _software_pipeline(pairs, [first_scores, first_weights, first_values])
    live = functools.reduce(jnp.minimum, [first[p] for p in range(pairs)])
    last_start = first_starts[-1]

    @pl.when(jnp.logical_and(last_start > 0, live < SB_DEAD_LOG2))
    def _walk_back():
        for p in range(pairs):
            for h in range(subs):
                car_ref[p, h] = jnp.broadcast_to(car_ref[p, h][:, 0:1], (2 * tq, LANES))

        def body(state):
            step, _ = state
            done = [jnp.maximum(first_start(h) - step * w, 0) for h in range(subs)]
            starts = [pl.multiple_of(jnp.maximum(d - w, 0), tq) for d in done]
            masks = [_sb_masks(key_positions(s) < d) for s, d in zip(starts, done)]

            def more_scores(p):
                return scores(p, starts, [m[0] for m in masks])

            def weights(p, sc):
                return _sb_weights(sc, tri_ref[...], [m[1] for m in masks],
                                   [car_ref[p, h] for h in range(subs)])

            def values(p, aws):
                least = None
                for h, (a, total) in enumerate(aws):
                    acc_ref[p, h] += _dot(a, v_ref[0, pl.ds(starts[h], w), lanes_of(p)])
                    carried = car_ref[p, h] + total[:, 0:1]
                    car_ref[p, h] = carried
                    m = jnp.min(carried)
                    least = m if least is None else jnp.minimum(least, m)
                return least

            mins = _software_pipeline(pairs, [more_scores, weights, values])
            return step + 1, functools.reduce(jnp.minimum, [mins[p] for p in range(pairs)])

        def cond(state):
            step, nxt = state
            return jnp.logical_and(last_start - step * w > 0, nxt < SB_DEAD_LOG2)

        lax.while_loop(cond, body, (jnp.int32(0), live))
        for p in range(pairs):
            for h in range(subs):
                o_ref[0, rows_of(h), lanes_of(p)] = _merge_heads(
                    acc_ref[p, h], lane).astype(o_ref.dtype)


def _sb_attention(q, k, v, weights):
    b, s, wd = q.shape
    pairs = wd // LANES
    subs = SB_STEP_ROWS // SB_Q_TILE
    nq = s // SB_STEP_ROWS
    tile = lambda bi, i: (bi, i, 0)
    whole = lambda bi, i: (bi, 0, 0)
    once = pl.Buffered(1)
    conv_specs = [
        pl.BlockSpec((size, w.shape[1]),
                     lambda bi, i, last=w.shape[0] // size - 1: (jnp.minimum(bi * nq + i, last), 0))
        for w, size in zip(weights, _conversion_blocks(weights, b * nq))]
    return pl.pallas_call(
        functools.partial(_sb_kernel, len(weights)),
        grid=(b, nq),
        in_specs=[
            pl.BlockSpec((1, SB_STEP_ROWS, wd), tile),
            pl.BlockSpec((1, s, wd), whole, pipeline_mode=once),
            pl.BlockSpec((1, s, wd), whole, pipeline_mode=once),
        ] + conv_specs,
        out_specs=[pl.BlockSpec((1, SB_STEP_ROWS, wd), tile)] + conv_specs,
        out_shape=[jax.ShapeDtypeStruct((b, s, wd), BF16)]
        + [jax.ShapeDtypeStruct(w.shape, BF16) for w in weights],
        scratch_shapes=[
            pltpu.VMEM((SB_WINDOW, SB_WINDOW), BF16),
            pltpu.VMEM((subs, 2, 2 * SB_Q_TILE, SB_WINDOW), F32),
            pltpu.VMEM((pairs, subs, 2 * SB_Q_TILE, LANES), F32),
            pltpu.VMEM((pairs, subs, 2 * SB_Q_TILE, LANES), F32),
        ],
        compiler_params=pltpu.CompilerParams(
            dimension_semantics=("arbitrary", "arbitrary"), vmem_limit_bytes=VMEM_LIMIT),
        name="stick_breaking_attention",
    )(q, k, v, *weights)


def _t5_log_bucket_starts():
    exact = REL_BUCKETS // 2
    d = np.arange(exact, 2 * REL_MAX_DIST, dtype=np.float32)
    large = exact + (np.log(d / np.float32(exact)) / np.float32(math.log(REL_MAX_DIST / exact))
                     * np.float32(REL_BUCKETS - exact)).astype(np.int32)
    large = np.minimum(large, REL_BUCKETS - 1)
    assert large[0] == exact and np.all(np.diff(large) >= 0)
    return [int(d[np.argmax(large >= bk)]) for bk in range(exact + 1, REL_BUCKETS)]


def _t5_causal_bucket(distance):
    d = jnp.maximum(distance, 0)
    bucket = jnp.minimum(d, REL_BUCKETS // 2)
    for start in _t5_log_bucket_starts():
        bucket = bucket + jnp.where(d >= start, 1, 0)
    return bucket


def _swa_fill_bias(relb_ref, bias_ref):
    t = ATT_BLOCK
    r = lax.broadcasted_iota(jnp.int32, (t, 2 * t), 0)
    c = lax.broadcasted_iota(jnp.int32, (t, 2 * t), 1)
    dist = r + t - c
    bucket = _t5_causal_bucket(dist)
    valid = jnp.logical_and(dist >= 0, dist < WINDOW)
    for h in range(SWA_HEADS):
        bias = jnp.zeros((t, 2 * t), F32)
        for bk in range(REL_BUCKETS):
            bias = jnp.where(bucket == bk, relb_ref[bk, h], bias)
        bias_ref[h] = jnp.where(valid, bias, MASK_NEG)


def _swa_schedule(sinks_ref, q_ref, kvc_ref, kvp_ref, bias_ref, o_ref, no_previous):
    t = ATT_BLOCK
    kvw = kvc_ref.shape[-1] // 4
    group = SWA_HEADS // SWA_KV_HEADS
    tiles = q_ref.shape[0] // t
    rows_of = lambda n: slice(n * t, (n + 1) * t)
    lane = lax.broadcasted_iota(jnp.int32, (t, LANES), 1)
    row2 = lax.broadcasted_iota(jnp.int32, (2 * t, 1), 0)
    col2 = lax.broadcasted_iota(jnp.int32, (2 * t, 2 * t), 1)
    prev_pen = jnp.where(jnp.logical_and(col2 < t, no_previous), MASK_NEG, 0.0)

    def kv_window(n, variant):
        cols = slice(variant * kvw, (variant + 1) * kvw)
        before = kvc_ref[rows_of(n - 1), cols] if n else kvp_ref[:, cols]
        return jnp.concatenate([before, kvc_ref[rows_of(n), cols]], axis=0)

    items = [(n, half, j) for n in range(tiles) for half in range(2) for j in range(SWA_KV_HEADS)]
    heads_of = lambda half, j: [h for h in range(j * group, (j + 1) * group) if h % 2 == half]

    def probabilities(i):
        n, half, j = items[i]
        heads = heads_of(half, j)
        in_half = (lane >= HEAD_DIM) if half else (lane < HEAD_DIM)
        qq = jnp.concatenate(
            [jnp.where(in_half, q_ref[rows_of(n), (h // 2) * LANES:(h // 2 + 1) * LANES], 0)
             for h in heads], axis=0)
        s = _dot_nt(qq, kv_window(n, 0 + int(j != half)))
        s = s + jnp.concatenate([bias_ref[h] for h in heads], axis=0)
        if n == 0:
            s = s + prev_pen
        sink = jnp.where(row2 < t, sinks_ref[heads[0]], sinks_ref[heads[1]])
        m = jnp.maximum(jnp.max(s, axis=-1, keepdims=True), sink)
        p = jnp.exp(s - m)
        denom = jnp.sum(p, axis=-1, keepdims=True) + jnp.exp(sink - m)
        return p.astype(BF16), 1.0 / denom

    def values(i, pd):
        n, half, j = items[i]
        p, inv_denom = pd
        return _dot(p, kv_window(n, 2 + int(j != half))) * inv_denom

    thunks, outs = _pipeline_schedule(len(items), [probabilities, values])

    def store():
        for i, (n, half, j) in enumerate(items):
            if half == 0:
                even, odd = outs[i], outs[items.index((n, 1, j))]
                for e, h in enumerate(heads_of(0, j)):
                    o_ref[rows_of(n), (h // 2) * LANES:(h // 2 + 1) * LANES] = jnp.where(
                        lane < HEAD_DIM, even[e * t:(e + 1) * t], odd[e * t:(e + 1) * t]
                    ).astype(o_ref.dtype)

    return thunks + [store]


def _tail_schedule(ha_ref, sb_ref, sw_ref, sbg_ref, swg_ref, wo_ref, l1g_ref, l1b_ref,
                   wgu_ref, wd_ref, l2g_ref, l2b_ref, o_ref):
    d = ha_ref.shape[1]
    d_ff = wd_ref.shape[0]
    mw = MXU_WIDTH
    bounds = list(range(0, d_ff, FFN_CHUNK)) + [d_ff]
    n_chunks = len(bounds) - 1
    out_tiles = range(d // mw)
    st = {}

    def merged():
        sb = _rms_norm(sb_ref[...].astype(F32), sbg_ref[...]).astype(BF16)
        sw = _rms_norm(sw_ref[...].astype(F32), swg_ref[...]).astype(BF16)
        st["merged"] = jnp.concatenate([sb, sw], axis=1)

    def out_projection(n):
        st["mix", n] = _dot(st["merged"], wo_ref[:, n * mw:(n + 1) * mw])

    def first_norm():
        mix = jnp.concatenate([st.pop(("mix", n)) for n in out_tiles], axis=1)
        st.pop("merged")
        h1 = _layer_norm(ha_ref[...] + mix, l1g_ref[...], l1b_ref[...])
        st["h1b"] = h1.astype(BF16)
        for n in out_tiles:
            st["acc", n] = ALPHA * h1[:, n * mw:(n + 1) * mw]

    def activation(c, n):
        lo = bounds[c] + n * mw
        gate = _dot(st["h1b"], wgu_ref[:, lo:lo + mw])
        up = _dot(st["h1b"], wgu_ref[:, d_ff + lo:d_ff + lo + mw])
        st["act", c, n] = (gate / (1.0 + jnp.exp(-gate)) * up).astype(BF16)

    def down(c, n):
        tiles = range((bounds[c + 1] - bounds[c]) // mw)
        if n == 0:
            st["act", c] = jnp.concatenate([st.pop(("act", c, k)) for k in tiles], axis=1)
        st["acc", n] = st["acc", n] + _dot(st["act", c], wd_ref[bounds[c]:bounds[c + 1],
                                                               n * mw:(n + 1) * mw])

    def second_norm():
        y = jnp.concatenate([st.pop(("acc", n)) for n in out_tiles], axis=1)
        o_ref[...] = _layer_norm(y, l2g_ref[...], l2b_ref[...])

    act_thunks = [[functools.partial(activation, c, n)
                   for n in range((bounds[c + 1] - bounds[c]) // mw)] for c in range(n_chunks)]
    down_thunks = [[functools.partial(down, c, n) for n in out_tiles] for c in range(n_chunks)]
    ffn = list(act_thunks[0])
    for c in range(1, n_chunks):
        ffn += _interleave(act_thunks[c], down_thunks[c - 1])
    ffn += down_thunks[-1]
    return (merged, [functools.partial(out_projection, n) for n in out_tiles], first_norm,
            ffn, second_norm)


def _attn_tail_kernel(tiles_per_seq, last_tile, sinks_ref, relb_ref, q_ref, kvc_ref, kvp_ref,
                      ha_ref, sb_ref, sbg_ref, swg_ref, wo_ref, l1g_ref, l1b_ref,
                      wgu_ref, wd_ref, l2g_ref, l2b_ref, o_ref, bias_ref, sw_ref):
    t = pl.program_id(0)

    @pl.when(t == 0)
    def _first_step():
        _swa_fill_bias(relb_ref, bias_ref)
        sw_ref[1] = jnp.zeros(sw_ref.shape[1:], sw_ref.dtype)

    no_previous = (jnp.minimum(t, last_tile) % tiles_per_seq) == 0
    attention = _swa_schedule(sinks_ref, q_ref.at[0], kvc_ref.at[0], kvp_ref.at[0], bias_ref,
                              sw_ref.at[t % 2], no_previous)
    merged, out_proj, first_norm, ffn, second_norm = _tail_schedule(
        ha_ref, sb_ref, sw_ref.at[(t + 1) % 2], sbg_ref, swg_ref, wo_ref,
        l1g_ref, l1b_ref, wgu_ref, wd_ref, l2g_ref, l2b_ref, o_ref)
    tail = [merged] + out_proj + [first_norm] + ffn + [second_norm]
    for thunk in _interleave(tail, attention):
        thunk()


def _attn_tail_block(q_sw, kv_sw, sinks, rel_bias, ha, sb2, sb_g, sw_g, w_out_b, ln1_g, ln1_b,
                     w_gate_up_b, w_down_b, ln2_g, ln2_b):
    b, s, wd = q_sw.shape
    n, d = ha.shape
    tiles_per_seq = s // TAIL_ROWS
    last_tile = n // TAIL_ROWS - 1
    key_tiles = TAIL_ROWS // ATT_BLOCK
    att = lambda t: jnp.minimum(t, last_tile)
    cur = lambda t: (att(t) // tiles_per_seq, att(t) % tiles_per_seq, 0)
    prev = lambda t: (att(t) // tiles_per_seq,
                      jnp.maximum(att(t) % tiles_per_seq * key_tiles - 1, 0), 0)
    row = lambda t: (jnp.maximum(t - 1, 0), 0)
    fixed = lambda t: (0, 0)
    smem = pl.BlockSpec(memory_space=pltpu.SMEM)
    vec = lambda a: pl.BlockSpec((1, a.shape[-1]), fixed)
    weight = lambda a: pl.BlockSpec(a.shape, fixed, pipeline_mode=pl.Buffered(1))
    return pl.pallas_call(
        functools.partial(_attn_tail_kernel, tiles_per_seq, last_tile),
        grid=(last_tile + 2,),
        in_specs=[
            smem, smem,
            pl.BlockSpec((1, TAIL_ROWS, wd), cur),
            pl.BlockSpec((1, TAIL_ROWS, kv_sw.shape[-1]), cur),
            pl.BlockSpec((1, ATT_BLOCK, kv_sw.shape[-1]), prev),
            pl.BlockSpec((TAIL_ROWS, d), row),
            pl.BlockSpec((TAIL_ROWS, sb2.shape[-1]), row),
            vec(sb_g), vec(sw_g), weight(w_out_b), vec(ln1_g), vec(ln1_b),
            weight(w_gate_up_b), weight(w_down_b), vec(ln2_g), vec(ln2_b),
        ],
        out_specs=pl.BlockSpec((TAIL_ROWS, d), row),
        out_shape=jax.ShapeDtypeStruct((n, d), F32),
        scratch_shapes=[
            pltpu.VMEM((SWA_HEADS, ATT_BLOCK, 2 * ATT_BLOCK), F32),
            pltpu.VMEM((2, TAIL_ROWS, wd), BF16),
        ],
        compiler_params=pltpu.CompilerParams(
            dimension_semantics=("arbitrary",), vmem_limit_bytes=VMEM_LIMIT),
        name="swa_attention_and_dense_tail",
    )(sinks, rel_bias, q_sw, kv_sw, kv_sw, ha, sb2, sb_g, sw_g, w_out_b, ln1_g, ln1_b,
      w_gate_up_b, w_down_b, ln2_g, ln2_b)


def kernel(x, ln_in_g, ln_in_b, w_in, sb_norm_g, swa_norm_g, sinks, rel_bias, w_out,
           ln1_g, ln1_b, w_gate_up, w_down, ln2_g, ln2_b):
    b, s, d = x.shape
    assert w_in.shape[0] == DEPTH == 1
    sb_w = SB_HEADS * HEAD_DIM
    sw_w = SWA_HEADS * HEAD_DIM
    kv_w = SWA_KV_HEADS * HEAD_DIM
    assert w_in.shape[-1] == 3 * sb_w + sw_w + 2 * kv_w
    assert kv_w == LANES and SWA_HEADS // SWA_KV_HEADS == 4
    assert s % SB_WINDOW == 0 and s % SB_STEP_ROWS == 0 and s % TAIL_ROWS == 0
    assert (b * s) % ROW_TILE == 0 and WINDOW <= ATT_BLOCK and d % MXU_WIDTH == 0
    assert FFN_CHUNK % MXU_WIDTH == 0 and w_down.shape[1] % MXU_WIDTH == 0

    x2 = x.reshape(b * s, d)
    row_vec = lambda a: a.reshape(1, -1)

    names = ("q_sb", "k_sb", "v_sb", "q_sw", "k_sw", "v_sw")
    widths = (sb_w, sb_w, sb_w, sw_w, kv_w, kv_w)
    splits, lo = {}, 0
    for name, wd in zip(names, widths):
        splits[name] = (lo, lo + wd)
        lo += wd
    ha, q_sb, k_sb, v_sb, q_sw, kv_sw = _in_projection(
        x2, row_vec(ln_in_g), row_vec(ln_in_b), w_in[0], splits)

    to3 = lambda a: a.reshape(b, s, a.shape[-1])
    sb_out, w_out_b, w_gate_up_b, w_down_b = _sb_attention(
        to3(q_sb), to3(k_sb), to3(v_sb), [w_out[0], w_gate_up[0], w_down[0]])

    out = _attn_tail_block(to3(q_sw), to3(kv_sw), sinks[0], rel_bias, ha,
                           sb_out.reshape(b * s, sb_w),
                           row_vec(sb_norm_g[0]), row_vec(swa_norm_g[0]), w_out_b,
                           row_vec(ln1_g[0]), row_vec(ln1_b[0]), w_gate_up_b, w_down_b,
                           row_vec(ln2_g[0]), row_vec(ln2_b[0]))
    return out.reshape(b, s, d)
```

```python
import functools
import math

import jax
import jax.numpy as jnp
import numpy as np
from jax import lax
from jax.experimental import pallas as pl
from jax.experimental.pallas import tpu as pltpu

F32 = jnp.float32
BF16 = jnp.bfloat16

HEAD_DIM = 64
SB_HEADS = 8
SWA_HEADS = 8
SWA_KV_HEADS = 2
WINDOW = 128
REL_BUCKETS = 32
REL_MAX_DIST = 128
LN_EPS = 1e-5
RMS_EPS = 1e-6
DEPTH = 1
ALPHA = (2 * DEPTH) ** 0.25
QK_SCALE = HEAD_DIM ** -0.5
LOG2E = math.log2(math.e)

LANES = 128
BF16_ROWS = 16
MXU_WIDTH = 256
ATT_BLOCK = 128
SB_WINDOW = 256
SB_Q_TILE = 64
SB_STEP_ROWS = 512
ROW_TILE = 1024
PROJ_PARTS = 2
TAIL_ROWS = 512
FFN_CHUNK = 4 * MXU_WIDTH
VMEM_LIMIT = 56 * 1024 * 1024

SB_DEAD_LOG2 = 127.0
MASK_NEG = -1e30


def _layer_norm(x, g, b):
    mu = jnp.mean(x, axis=-1, keepdims=True)
    xc = x - mu
    var = jnp.mean(xc * xc, axis=-1, keepdims=True)
    return xc * lax.rsqrt(var + LN_EPS) * g + b


def _rms_norm(x, g):
    ms = jnp.mean(x * x, axis=-1, keepdims=True)
    return x * lax.rsqrt(ms + RMS_EPS) * g


def _dot(a, b):
    return jnp.dot(a, b, preferred_element_type=F32)


def _dot_nt(a, b):
    return lax.dot_general(a, b, (((1,), (1,)), ((), ())), preferred_element_type=F32)


def _pipeline_schedule(n, stages):
    vals = [dict() for _ in stages]
    thunks = []
    for step in range(n + len(stages) - 1):
        for s, stage in enumerate(stages):
            i = step - s
            if 0 <= i < n:
                def thunk(s=s, i=i, stage=stage):
                    vals[s][i] = stage(i, vals[s - 1].pop(i)) if s else stage(i)
                thunks.append(thunk)
    return thunks, vals[-1]


def _software_pipeline(n, stages):
    thunks, results = _pipeline_schedule(n, stages)
    for thunk in thunks:
        thunk()
    return results


def _interleave(xs, ys):
    merged, taken = [], 0
    for i, x in enumerate(xs):
        merged.append(x)
        upto = (i + 1) * len(ys) // len(xs)
        merged.extend(ys[taken:upto])
        taken = upto
    return merged


def _inproj_kernel(splits, x_ref, g_ref, b_ref, w32_ref,
                   ha_ref, qsb_ref, ksb_ref, vsb_ref, qsw_ref, kv_ref, w_ref):
    rows = x_ref.shape[0] // PROJ_PARTS
    part = lambda i: slice(i * rows, (i + 1) * rows)
    assert splits["k_sw"][1] == splits["v_sw"][0]

    @pl.when(pl.program_id(0) == 0)
    def _convert_weight():
        w_ref[...] = w32_ref[...].astype(BF16)

    def normed(i):
        h32 = _layer_norm(x_ref[part(i), :], g_ref[...], b_ref[...])
        ha_ref[part(i), :] = ALPHA * h32
        return h32.astype(BF16)

    def projected(i, h):
        r = part(i)
        proj = lambda lo, hi: _dot(h, w_ref[:, lo:hi])
        qsb_ref[r, :] = (proj(*splits["q_sb"]) * QK_SCALE).astype(BF16)
        ksb_ref[r, :] = proj(*splits["k_sb"]).astype(BF16)
        vsb_ref[r, :] = proj(*splits["v_sb"]).astype(BF16)
        qsw_ref[r, :] = (proj(*splits["q_sw"]) * QK_SCALE).astype(BF16)
        kv = proj(splits["k_sw"][0], splits["v_sw"][1])
        kvw = kv.shape[-1] // 2
        k, v = kv[:, :kvw], kv[:, kvw:]
        kv_ref[r, 0 * kvw:1 * kvw] = k.astype(BF16)
        kv_ref[r, 1 * kvw:2 * kvw] = pltpu.roll(k, HEAD_DIM, axis=1).astype(BF16)
        kv_ref[r, 2 * kvw:3 * kvw] = v.astype(BF16)
        kv_ref[r, 3 * kvw:4 * kvw] = pltpu.roll(v, HEAD_DIM, axis=1).astype(BF16)

    _software_pipeline(PROJ_PARTS, [normed, projected])


def _in_projection(x2, g, b, w_in, splits):
    n, d = x2.shape
    width = lambda name: splits[name][1] - splits[name][0]
    out_w = [width("q_sb"), width("k_sb"), width("v_sb"), width("q_sw"), 4 * width("k_sw")]
    row = lambda i: (i, 0)
    fixed = lambda i: (0, 0)
    return pl.pallas_call(
        functools.partial(_inproj_kernel, splits),
        grid=(n // ROW_TILE,),
        in_specs=[
            pl.BlockSpec((ROW_TILE, d), row),
            pl.BlockSpec((1, d), fixed),
            pl.BlockSpec((1, d), fixed),
            pl.BlockSpec(w_in.shape, fixed, pipeline_mode=pl.Buffered(1)),
        ],
        out_specs=[pl.BlockSpec((ROW_TILE, d), row)]
        + [pl.BlockSpec((ROW_TILE, w), row) for w in out_w],
        out_shape=[jax.ShapeDtypeStruct((n, d), F32)]
        + [jax.ShapeDtypeStruct((n, w), BF16) for w in out_w],
        scratch_shapes=[pltpu.VMEM(w_in.shape, BF16)],
        compiler_params=pltpu.CompilerParams(
            dimension_semantics=("arbitrary",), vmem_limit_bytes=VMEM_LIMIT),
        name="ln_in_proj",
    )(x2, g, b, w_in)


def _conversion_blocks(weights, steps):
    sizes = []
    for w in weights:
        rows = w.shape[0]
        sizes.append(next(r for r in range(BF16_ROWS, rows + 1, BF16_ROWS)
                          if rows % r == 0 and rows // r <= steps))
    return sizes


def _split_heads(x, lane):
    zero = jnp.zeros_like(x)
    return jnp.concatenate([jnp.where(lane < HEAD_DIM, x, zero),
                            jnp.where(lane >= HEAD_DIM, x, zero)], axis=0)


def _merge_heads(o2, lane):
    t = o2.shape[0] // 2
    return jnp.where(lane < HEAD_DIM, o2[:t], o2[t:])


def _sb_masks(valid):
    return jnp.where(valid, LOG2E, 0.0), jnp.where(valid, 0.0, MASK_NEG)


def _sb_scores(qq, k, keep):
    z = _dot_nt(qq, k)
    zl = z * LOG2E
    c_nat = jnp.maximum(z, 0.0) + jnp.log(1.0 + jnp.exp2(-jnp.abs(zl)))
    cm = c_nat * keep
    return zl - cm, cm.astype(BF16), cm[:, :LANES]


def _sb_weights(scores, tri, kills, carrieds):
    sums = _dot(jnp.concatenate([sc[1] for sc in scores], axis=0), tri)
    rows = scores[0][0].shape[0]
    out = []
    for n, ((log_beta, _, cm0), kill, carried) in enumerate(zip(scores, kills, carrieds)):
        later = sums[n * rows:(n + 1) * rows]
        log_a = (log_beta - later) + kill
        if carried is not None:
            log_a = log_a - jnp.concatenate([carried] * (log_a.shape[1] // LANES), axis=1)
        out.append((jnp.exp2(log_a).astype(BF16), later[:, :LANES] + cm0))
    return out


def _sb_kernel(n_conv, q_ref, k_ref, v_ref, *refs):
    conv_in, o_ref, conv_out = refs[:n_conv], refs[n_conv], refs[n_conv + 1:2 * n_conv + 1]
    tri_ref, mask_ref, acc_ref, car_ref = refs[2 * n_conv + 1:]
    for src, dst in zip(conv_in, conv_out):
        dst[...] = src[...].astype(BF16)

    bi = pl.program_id(0)
    qi = pl.program_id(1)
    tq, w = SB_Q_TILE, SB_WINDOW
    subs = q_ref.shape[1] // tq
    pairs = q_ref.shape[-1] // LANES
    row0 = qi * (subs * tq)

    @pl.when(jnp.logical_and(bi == 0, qi == 0))
    def _fill_tri():
        j = lax.broadcasted_iota(jnp.int32, (w, w), 0)
        s = lax.broadcasted_iota(jnp.int32, (w, w), 1)
        tri_ref[...] = jnp.where(j > s, 1.0, 0.0).astype(BF16)

    lane = lax.broadcasted_iota(jnp.int32, (tq, LANES), 1)
    lanes_of = lambda p: slice(p * LANES, (p + 1) * LANES)
    rows_of = lambda h: slice(h * tq, (h + 1) * tq)
    first_start = lambda h: pl.multiple_of(jnp.maximum(row0 + (h + 1) * tq - w, 0), tq)

    def key_positions(start):
        return start + lax.broadcasted_iota(jnp.int32, (2 * tq, w), 1)

    assert subs * tq >= w - tq

    @pl.when(qi <= 1)
    def _fill_masks():
        row = lax.broadcasted_iota(jnp.int32, (2 * tq, w), 0) & (tq - 1)
        for h in range(subs):
            mask_ref[h, 0], mask_ref[h, 1] = _sb_masks(
                key_positions(first_start(h)) < row0 + h * tq + row)

    def scores(p, starts, keeps):
        return [_sb_scores(_split_heads(q_ref[0, rows_of(h), lanes_of(p)], lane),
                           k_ref[0, pl.ds(starts[h], w), lanes_of(p)], keeps[h])
                for h in range(subs)]

    first_starts = [first_start(h) for h in range(subs)]

    def first_scores(p):
        return scores(p, first_starts, [mask_ref[h, 0] for h in range(subs)])

    def first_weights(p, sc):
        return _sb_weights(sc, tri_ref[...], [mask_ref[h, 1] for h in range(subs)], [None] * subs)

    def first_values(p, aws):
        least = None
        for h, (a, total) in enumerate(aws):
            o2 = _dot(a, v_ref[0, pl.ds(first_starts[h], w), lanes_of(p)])
            o_ref[0, rows_of(h), lanes_of(p)] = _merge_heads(o2, lane).astype(o_ref.dtype)
            acc_ref[p, h] = o2
            car_ref[p, h] = total
            m = jnp.min(total[:, 0:1])
            least = m if least is None else jnp.minimum(least, m)
        return least

    first = _software_pipeline(pairs, [first_scores, first_weights, first_values])
    live = functools.reduce(jnp.minimum, [first[p] for p in range(pairs)])
    last_start = first_starts[-1]

    @pl.when(jnp.logical_and(last_start > 0, live < SB_DEAD_LOG2))
    def _walk_back():
        for p in range(pairs):
            for h in range(subs):
                car_ref[p, h] = jnp.broadcast_to(car_ref[p, h][:, 0:1], (2 * tq, LANES))

        def body(state):
            step, _ = state
            done = [jnp.maximum(first_start(h) - step * w, 0) for h in range(subs)]
            starts = [pl.multiple_of(jnp.maximum(d - w, 0), tq) for d in done]
            masks = [_sb_masks(key_positions(s) < d) for s, d in zip(starts, done)]

            def more_scores(p):
                return scores(p, starts, [m[0] for m in masks])

            def weights(p, sc):
                return _sb_weights(sc, tri_ref[...], [m[1] for m in masks],
                                   [car_ref[p, h] for h in range(subs)])

            def values(p, aws):
                least = None
                for h, (a, total) in enumerate(aws):
                    acc_ref[p, h] += _dot(a, v_ref[0, pl.ds(starts[h], w), lanes_of(p)])
                    carried = car_ref[p, h] + total[:, 0:1]
                    car_ref[p, h] = carried
                    m = jnp.min(carried)
                    least = m if least is None else jnp.minimum(least, m)
                return least

            mins = _software_pipeline(pairs, [more_scores, weights, values])
            return step + 1, functools.reduce(jnp.minimum, [mins[p] for p in range(pairs)])

        def cond(state):
            step, nxt = state
            return jnp.logical_and(last_start - step * w > 0, nxt < SB_DEAD_LOG2)

        lax.while_loop(cond, body, (jnp.int32(0), live))
        for p in range(pairs):
            for h in range(subs):
                o_ref[0, rows_of(h), lanes_of(p)] = _merge_heads(
                    acc_ref[p, h], lane).astype(o_ref.dtype)


def _sb_attention(q, k, v, weights):
    b, s, wd = q.shape
    pairs = wd // LANES
    subs = SB_STEP_ROWS // SB_Q_TILE
    nq = s // SB_STEP_ROWS
    tile = lambda bi, i: (bi, i, 0)
    whole = lambda bi, i: (bi, 0, 0)
    once = pl.Buffered(1)
    conv_specs = [
        pl.BlockSpec((size, w.shape[1]),
                     lambda bi, i, last=w.shape[0] // size - 1: (jnp.minimum(bi * nq + i, last), 0))
        for w, size in zip(weights, _conversion_blocks(weights, b * nq))]
    return pl.pallas_call(
        functools.partial(_sb_kernel, len(weights)),
        grid=(b, nq),
        in_specs=[
            pl.BlockSpec((1, SB_STEP_ROWS, wd), tile),
            pl.BlockSpec((1, s, wd), whole, pipeline_mode=once),
            pl.BlockSpec((1, s, wd), whole, pipeline_mode=once),
        ] + conv_specs,
        out_specs=[pl.BlockSpec((1, SB_STEP_ROWS, wd), tile)] + conv_specs,
        out_shape=[jax.ShapeDtypeStruct((b, s, wd), BF16)]
        + [jax.ShapeDtypeStruct(w.shape, BF16) for w in weights],
        scratch_shapes=[
            pltpu.VMEM((SB_WINDOW, SB_WINDOW), BF16),
            pltpu.VMEM((subs, 2, 2 * SB_Q_TILE, SB_WINDOW), F32),
            pltpu.VMEM((pairs, subs, 2 * SB_Q_TILE, LANES), F32),
            pltpu.VMEM((pairs, subs, 2 * SB_Q_TILE, LANES), F32),
        ],
        compiler_params=pltpu.CompilerParams(
            dimension_semantics=("arbitrary", "arbitrary"), vmem_limit_bytes=VMEM_LIMIT),
        name="stick_breaking_attention",
    )(q, k, v, *weights)


def _t5_log_bucket_starts():
    exact = REL_BUCKETS // 2
    d = np.arange(exact, 2 * REL_MAX_DIST, dtype=np.float32)
    large = exact + (np.log(d / np.float32(exact)) / np.float32(math.log(REL_MAX_DIST / exact))
                     * np.float32(REL_BUCKETS - exact)).astype(np.int32)
    large = np.minimum(large, REL_BUCKETS - 1)
    assert large[0] == exact and np.all(np.diff(large) >= 0)
    return [int(d[np.argmax(large >= bk)]) for bk in range(exact + 1, REL_BUCKETS)]


def _t5_causal_bucket(distance):
    d = jnp.maximum(distance, 0)
    bucket = jnp.minimum(d, REL_BUCKETS // 2)
    for start in _t5_log_bucket_starts():
        bucket = bucket + jnp.where(d >= start, 1, 0)
    return bucket


def _swa_fill_bias(relb_ref, bias_ref):
    t = ATT_BLOCK
    r = lax.broadcasted_iota(jnp.int32, (t, 2 * t), 0)
    c = lax.broadcasted_iota(jnp.int32, (t, 2 * t), 1)
    dist = r + t - c
    bucket = _t5_causal_bucket(dist)
    valid = jnp.logical_and(dist >= 0, dist < WINDOW)
    for h in range(SWA_HEADS):
        bias = jnp.zeros((t, 2 * t), F32)
        for bk in range(REL_BUCKETS):
            bias = jnp.where(bucket == bk, relb_ref[bk, h], bias)
        bias_ref[h] = jnp.where(valid, bias, MASK_NEG)


def _swa_schedule(sinks_ref, q_ref, kvc_ref, kvp_ref, bias_ref, o_ref, no_previous):
    t = ATT_BLOCK
    kvw = kvc_ref.shape[-1] // 4
    group = SWA_HEADS // SWA_KV_HEADS
    tiles = q_ref.shape[0] // t
    rows_of = lambda n: slice(n * t, (n + 1) * t)
    lane = lax.broadcasted_iota(jnp.int32, (t, LANES), 1)
    row2 = lax.broadcasted_iota(jnp.int32, (2 * t, 1), 0)
    col2 = lax.broadcasted_iota(jnp.int32, (2 * t, 2 * t), 1)
    prev_pen = jnp.where(jnp.logical_and(col2 < t, no_previous), MASK_NEG, 0.0)

    def kv_window(n, variant):
        cols = slice(variant * kvw, (variant + 1) * kvw)
        before = kvc_ref[rows_of(n - 1), cols] if n else kvp_ref[:, cols]
        return jnp.concatenate([before, kvc_ref[rows_of(n), cols]], axis=0)

    items = [(n, half, j) for n in range(tiles) for half in range(2) for j in range(SWA_KV_HEADS)]
    heads_of = lambda half, j: [h for h in range(j * group, (j + 1) * group) if h % 2 == half]

    def probabilities(i):
        n, half, j = items[i]
        heads = heads_of(half, j)
        in_half = (lane >= HEAD_DIM) if half else (lane < HEAD_DIM)
        qq = jnp.concatenate(
            [jnp.where(in_half, q_ref[rows_of(n), (h // 2) * LANES:(h // 2 + 1) * LANES], 0)
             for h in heads], axis=0)
        s = _dot_nt(qq, kv_window(n, 0 + int(j != half)))
        s = s + jnp.concatenate([bias_ref[h] for h in heads], axis=0)
        if n == 0:
            s = s + prev_pen
        sink = jnp.where(row2 < t, sinks_ref[heads[0]], sinks_ref[heads[1]])
        m = jnp.maximum(jnp.max(s, axis=-1, keepdims=True), sink)
        p = jnp.exp(s - m)
        denom = jnp.sum(p, axis=-1, keepdims=True) + jnp.exp(sink - m)
        return p.astype(BF16), 1.0 / denom

    def values(i, pd):
        n, half, j = items[i]
        p, inv_denom = pd
        return _dot(p, kv_window(n, 2 + int(j != half))) * inv_denom

    thunks, outs = _pipeline_schedule(len(items), [probabilities, values])

    def store():
        for i, (n, half, j) in enumerate(items):
            if half == 0:
                even, odd = outs[i], outs[items.index((n, 1, j))]
                for e, h in enumerate(heads_of(0, j)):
                    o_ref[rows_of(n), (h // 2) * LANES:(h // 2 + 1) * LANES] = jnp.where(
                        lane < HEAD_DIM, even[e * t:(e + 1) * t], odd[e * t:(e + 1) * t]
                    ).astype(o_ref.dtype)

    return thunks + [store]


def _tail_schedule(ha_ref, sb_ref, sw_ref, sbg_ref, swg_ref, wo_ref, l1g_ref, l1b_ref,
                   wgu_ref, wd_ref, l2g_ref, l2b_ref, o_ref):
    d = ha_ref.shape[1]
    d_ff = wd_ref.shape[0]
    mw = MXU_WIDTH
    bounds = list(range(0, d_ff, FFN_CHUNK)) + [d_ff]
    n_chunks = len(bounds) - 1
    out_tiles = range(d // mw)
    st = {}

    def merged():
        sb = _rms_norm(sb_ref[...].astype(F32), sbg_ref[...]).astype(BF16)
        sw = _rms_norm(sw_ref[...].astype(F32), swg_ref[...]).astype(BF16)
        st["merged"] = jnp.concatenate([sb, sw], axis=1)

    def out_projection(n):
        st["mix", n] = _dot(st["merged"], wo_ref[:, n * mw:(n + 1) * mw])

    def first_norm():
        mix = jnp.concatenate([st.pop(("mix", n)) for n in out_tiles], axis=1)
        st.pop("merged")
        h1 = _layer_norm(ha_ref[...] + mix, l1g_ref[...], l1b_ref[...])
        st["h1b"] = h1.astype(BF16)
        for n in out_tiles:
            st["acc", n] = ALPHA * h1[:, n * mw:(n + 1) * mw]

    def activation(c, n):
        lo = bounds[c] + n * mw
        gate = _dot(st["h1b"], wgu_ref[:, lo:lo + mw])
        up = _dot(st["h1b"], wgu_ref[:, d_ff + lo:d_ff + lo + mw])
        st["act", c, n] = (gate / (1.0 + jnp.exp(-gate)) * up).astype(BF16)

    def down(c, n):
        tiles = range((bounds[c + 1] - bounds[c]) // mw)
        if n == 0:
            st["act", c] = jnp.concatenate([st.pop(("act", c, k)) for k in tiles], axis=1)
        st["acc", n] = st["acc", n] + _dot(st["act", c], wd_ref[bounds[c]:bounds[c + 1],
                                                               n * mw:(n + 1) * mw])

    def second_norm():
        y = jnp.concatenate([st.pop(("acc", n)) for n in out_tiles], axis=1)
        o_ref[...] = _layer_norm(y, l2g_ref[...], l2b_ref[...])

    act_thunks = [[functools.partial(activation, c, n)
                   for n in range((bounds[c + 1] - bounds[c]) // mw)] for c in range(n_chunks)]
    down_thunks = [[functools.partial(down, c, n) for n in out_tiles] for c in range(n_chunks)]
    ffn = list(act_thunks[0])
    for c in range(1, n_chunks):
        ffn += _interleave(act_thunks[c], down_thunks[c - 1])
    ffn += down_thunks[-1]
    return (merged, [functools.partial(out_projection, n) for n in out_tiles], first_norm,
            ffn, second_norm)


def _attn_tail_kernel(tiles_per_seq, last_tile, sinks_ref, relb_ref, q_ref, kvc_ref, kvp_ref,
                      ha_ref, sb_ref, sbg_ref, swg_ref, wo_ref, l1g_ref, l1b_ref,
                      wgu_ref, wd_ref, l2g_ref, l2b_ref, o_ref, bias_ref, sw_ref):
    t = pl.program_id(0)

    @pl.when(t == 0)
    def _first_step():
        _swa_fill_bias(relb_ref, bias_ref)
        sw_ref[1] = jnp.zeros(sw_ref.shape[1:], sw_ref.dtype)

    no_previous = (jnp.minimum(t, last_tile) % tiles_per_seq) == 0
    attention = _swa_schedule(sinks_ref, q_ref.at[0], kvc_ref.at[0], kvp_ref.at[0], bias_ref,
                              sw_ref.at[t % 2], no_previous)
    merged, out_proj, first_norm, ffn, second_norm = _tail_schedule(
        ha_ref, sb_ref, sw_ref.at[(t + 1) % 2], sbg_ref, swg_ref, wo_ref,
        l1g_ref, l1b_ref, wgu_ref, wd_ref, l2g_ref, l2b_ref, o_ref)
    tail = [merged] + out_proj + [first_norm] + ffn + [second_norm]
    for thunk in _interleave(tail, attention):
        thunk()


def _attn_tail_block(q_sw, kv_sw, sinks, rel_bias, ha, sb2, sb_g, sw_g, w_out_b, ln1_g, ln1_b,
                     w_gate_up_b, w_down_b, ln2_g, ln2_b):
    b, s, wd = q_sw.shape
    n, d = ha.shape
    tiles_per_seq = s // TAIL_ROWS
    last_tile = n // TAIL_ROWS - 1
    key_tiles = TAIL_ROWS // ATT_BLOCK
    att = lambda t: jnp.minimum(t, last_tile)
    cur = lambda t: (att(t) // tiles_per_seq, att(t) % tiles_per_seq, 0)
    prev = lambda t: (att(t) // tiles_per_seq,
                      jnp.maximum(att(t) % tiles_per_seq * key_tiles - 1, 0), 0)
    row = lambda t: (jnp.maximum(t - 1, 0), 0)
    fixed = lambda t: (0, 0)
    smem = pl.BlockSpec(memory_space=pltpu.SMEM)
    vec = lambda a: pl.BlockSpec((1, a.shape[-1]), fixed)
    weight = lambda a: pl.BlockSpec(a.shape, fixed, pipeline_mode=pl.Buffered(1))
    return pl.pallas_call(
        functools.partial(_attn_tail_kernel, tiles_per_seq, last_tile),
        grid=(last_tile + 2,),
        in_specs=[
            smem, smem,
            pl.BlockSpec((1, TAIL_ROWS, wd), cur),
            pl.BlockSpec((1, TAIL_ROWS, kv_sw.shape[-1]), cur),
            pl.BlockSpec((1, ATT_BLOCK, kv_sw.shape[-1]), prev),
            pl.BlockSpec((TAIL_ROWS, d), row),
            pl.BlockSpec((TAIL_ROWS, sb2.shape[-1]), row),
            vec(sb_g), vec(sw_g), weight(w_out_b), vec(ln1_g), vec(ln1_b),
            weight(w_gate_up_b), weight(w_down_b), vec(ln2_g), vec(ln2_b),
        ],
        out_specs=pl.BlockSpec((TAIL_ROWS, d), row),
        out_shape=jax.ShapeDtypeStruct((n, d), F32),
        scratch_shapes=[
            pltpu.VMEM((SWA_HEADS, ATT_BLOCK, 2 * ATT_BLOCK), F32),
            pltpu.VMEM((2, TAIL_ROWS, wd), BF16),
        ],
        compiler_params=pltpu.CompilerParams(
            dimension_semantics=("arbitrary",), vmem_limit_bytes=VMEM_LIMIT),
        name="swa_attention_and_dense_tail",
    )(sinks, rel_bias, q_sw, kv_sw, kv_sw, ha, sb2, sb_g, sw_g, w_out_b, ln1_g, ln1_b,
      w_gate_up_b, w_down_b, ln2_g, ln2_b)


def kernel(x, ln_in_g, ln_in_b, w_in, sb_norm_g, swa_norm_g, sinks, rel_bias, w_out,
           ln1_g, ln1_b, w_gate_up, w_down, ln2_g, ln2_b):
    b, s, d = x.shape
    assert w_in.shape[0] == DEPTH == 1
    sb_w = SB_HEADS * HEAD_DIM
    sw_w = SWA_HEADS * HEAD_DIM
    kv_w = SWA_KV_HEADS * HEAD_DIM
    assert w_in.shape[-1] == 3 * sb_w + sw_w + 2 * kv_w
    assert kv_w == LANES and SWA_HEADS // SWA_KV_HEADS == 4
    assert s % SB_WINDOW == 0 and s % SB_STEP_ROWS == 0 and s % TAIL_ROWS == 0
    assert (b * s) % ROW_TILE == 0 and WINDOW <= ATT_BLOCK and d % MXU_WIDTH == 0
    assert FFN_CHUNK % MXU_WIDTH == 0 and w_down.shape[1] % MXU_WIDTH == 0

    x2 = x.reshape(b * s, d)
    row_vec = lambda a: a.reshape(1, -1)

    names = ("q_sb", "k_sb", "v_sb", "q_sw", "k_sw", "v_sw")
    widths = (sb_w, sb_w, sb_w, sw_w, kv_w, kv_w)
    splits, lo = {}, 0
    for name, wd in zip(names, widths):
        splits[name] = (lo, lo + wd)
        lo += wd
    ha, q_sb, k_sb, v_sb, q_sw, kv_sw = _in_projection(
        x2, row_vec(ln_in_g), row_vec(ln_in_b), w_in[0], splits)

    to3 = lambda a: a.reshape(b, s, a.shape[-1])
    sb_out, w_out_b, w_gate_up_b, w_down_b = _sb_attention(
        to3(q_sb), to3(k_sb), to3(v_sb), [w_out[0], w_gate_up[0], w_down[0]])

    out = _attn_tail_block(to3(q_sw), to3(kv_sw), sinks[0], rel_bias, ha,
                           sb_out.reshape(b * s, sb_w),
                           row_vec(sb_norm_g[0]), row_vec(swa_norm_g[0]), w_out_b,
                           row_vec(ln1_g[0]), row_vec(ln1_b[0]), w_gate_up_b, w_down_b,
                           row_vec(ln2_g[0]), row_vec(ln2_b[0]))
    return out.reshape(b, s, d)
```

```python
import functools
import math

import jax
import jax.numpy as jnp
import numpy as np
from jax import lax
from jax.experimental import pallas as pl
from jax.experimental.pallas import tpu as pltpu

F32 = jnp.float32
BF16 = jnp.bfloat16

HEAD_DIM = 64
SB_HEADS = 8
SWA_HEADS = 8
SWA_KV_HEADS = 2
WINDOW = 128
REL_BUCKETS = 32
REL_MAX_DIST = 128
LN_EPS = 1e-5
RMS_EPS = 1e-6
DEPTH = 1
ALPHA = (2 * DEPTH) ** 0.25
QK_SCALE = HEAD_DIM ** -0.5
LOG2E = math.log2(math.e)

LANES = 128
BF16_ROWS = 16
MXU_WIDTH = 256
ATT_BLOCK = 128
SB_WINDOW = 256
SB_Q_TILE = 64
SB_STEP_ROWS = 512
ROW_TILE = 1024
PROJ_PARTS = 2
TAIL_ROWS = 512
FFN_CHUNK = 4 * MXU_WIDTH
VMEM_LIMIT = 56 * 1024 * 1024

SB_DEAD_LOG2 = 127.0
MASK_NEG = -1e30


def _layer_norm(x, g, b):
    mu = jnp.mean(x, axis=-1, keepdims=True)
    xc = x - mu
    var = jnp.mean(xc * xc, axis=-1, keepdims=True)
    return xc * lax.rsqrt(var + LN_EPS) * g + b


def _rms_norm(x, g):
    ms = jnp.mean(x * x, axis=-1, keepdims=True)
    return x * lax.rsqrt(ms + RMS_EPS) * g


def _dot(a, b):
    return jnp.dot(a, b, preferred_element_type=F32)


def _dot_nt(a, b):
    return lax.dot_general(a, b, (((1,), (1,)), ((), ())), preferred_element_type=F32)


def _pipeline_schedule(n, stages):
    vals = [dict() for _ in stages]
    thunks = []
    for step in range(n + len(stages) - 1):
        for s, stage in enumerate(stages):
            i = step - s
            if 0 <= i < n:
                def thunk(s=s, i=i, stage=stage):
                    vals[s][i] = stage(i, vals[s - 1].pop(i)) if s else stage(i)
                thunks.append(thunk)
    return thunks, vals[-1]


def _software_pipeline(n, stages):
    thunks, results = _pipeline_schedule(n, stages)
    for thunk in thunks:
        thunk()
    return results


def _interleave(xs, ys):
    merged, taken = [], 0
    for i, x in enumerate(xs):
        merged.append(x)
        upto = (i + 1) * len(ys) // len(xs)
        merged.extend(ys[taken:upto])
        taken = upto
    return merged


def _inproj_kernel(splits, n_later, x_ref, g_ref, b_ref, w32_ref, *refs):
    later_in, refs = refs[:n_later], refs[n_later:]
    ha_ref, qsb_ref, ksb_ref, vsb_ref, qsw_ref, kv_ref = refs[:6]
    later_out, w_ref = refs[6:6 + n_later], refs[6 + n_later]
    rows = x_ref.shape[0] // PROJ_PARTS
    part = lambda i: slice(i * rows, (i + 1) * rows)
    assert splits["k_sw"][1] == splits["v_sw"][0]

    @pl.when(pl.program_id(0) == 0)
    def _convert_weight():
        w_ref[...] = w32_ref[...].astype(BF16)

    for src, dst in zip(later_in, later_out):
        dst[...] = src[...].astype(BF16)

    def normed(i):
        h32 = _layer_norm(x_ref[part(i), :], g_ref[...], b_ref[...])
        ha_ref[part(i), :] = ALPHA * h32
        return h32.astype(BF16)

    def projected(i, h):
        r = part(i)
        proj = lambda lo, hi: _dot(h, w_ref[:, lo:hi])
        qsb_ref[r, :] = (proj(*splits["q_sb"]) * QK_SCALE).astype(BF16)
        ksb_ref[r, :] = proj(*splits["k_sb"]).astype(BF16)
        vsb_ref[r, :] = proj(*splits["v_sb"]).astype(BF16)
        qsw_ref[r, :] = (proj(*splits["q_sw"]) * QK_SCALE).astype(BF16)
        kv = proj(splits["k_sw"][0], splits["v_sw"][1])
        kvw = kv.shape[-1] // 2
        k, v = kv[:, :kvw], kv[:, kvw:]
        kv_ref[r, 0 * kvw:1 * kvw] = k.astype(BF16)
        kv_ref[r, 1 * kvw:2 * kvw] = pltpu.roll(k, HEAD_DIM, axis=1).astype(BF16)
        kv_ref[r, 2 * kvw:3 * kvw] = v.astype(BF16)
        kv_ref[r, 3 * kvw:4 * kvw] = pltpu.roll(v, HEAD_DIM, axis=1).astype(BF16)

    _software_pipeline(PROJ_PARTS, [normed, projected])


def _in_projection(x2, g, b, w_in, splits, later_weights):
    n, d = x2.shape
    steps = n // ROW_TILE
    width = lambda name: splits[name][1] - splits[name][0]
    out_w = [width("q_sb"), width("k_sb"), width("v_sb"), width("q_sw"), 4 * width("k_sw")]
    row = lambda i: (i, 0)
    fixed = lambda i: (0, 0)
    later_specs = [
        pl.BlockSpec((size, w.shape[1]),
                     lambda i, last=w.shape[0] // size - 1: (jnp.minimum(i, last), 0))
        for w, size in zip(later_weights, _conversion_blocks(later_weights, steps))]
    return pl.pallas_call(
        functools.partial(_inproj_kernel, splits, len(later_weights)),
        grid=(steps,),
        in_specs=[
            pl.BlockSpec((ROW_TILE, d), row),
            pl.BlockSpec((1, d), fixed),
            pl.BlockSpec((1, d), fixed),
            pl.BlockSpec(w_in.shape, fixed, pipeline_mode=pl.Buffered(1)),
        ] + later_specs,
        out_specs=[pl.BlockSpec((ROW_TILE, d), row)]
        + [pl.BlockSpec((ROW_TILE, w), row) for w in out_w] + later_specs,
        out_shape=[jax.ShapeDtypeStruct((n, d), F32)]
        + [jax.ShapeDtypeStruct((n, w), BF16) for w in out_w]
        + [jax.ShapeDtypeStruct(w.shape, BF16) for w in later_weights],
        scratch_shapes=[pltpu.VMEM(w_in.shape, BF16)],
        compiler_params=pltpu.CompilerParams(
            dimension_semantics=("arbitrary",), vmem_limit_bytes=VMEM_LIMIT),
        name="ln_in_proj",
    )(x2, g, b, w_in, *later_weights)


def _conversion_blocks(weights, steps):
    sizes = []
    for w in weights:
        rows = w.shape[0]
        sizes.append(next(r for r in range(BF16_ROWS, rows + 1, BF16_ROWS)
                          if rows % r == 0 and rows // r <= steps))
    return sizes


def _split_heads(x, lane):
    zero = jnp.zeros_like(x)
    return jnp.concatenate([jnp.where(lane < HEAD_DIM, x, zero),
                            jnp.where(lane >= HEAD_DIM, x, zero)], axis=0)


def _merge_heads(o2, lane):
    t = o2.shape[0] // 2
    return jnp.where(lane < HEAD_DIM, o2[:t], o2[t:])


def _sb_masks(valid):
    return jnp.where(valid, LOG2E, 0.0), jnp.where(valid, 0.0, MASK_NEG)


def _sb_scores(qq, k, keep):
    z = _dot_nt(qq, k)
    zl = z * LOG2E
    c_nat = jnp.maximum(z, 0.0) + jnp.log(1.0 + jnp.exp2(-jnp.abs(zl)))
    cm = c_nat * keep
    return zl - cm, cm.astype(BF16), cm[:, :LANES]


def _sb_weights(scores, tri, kills, carrieds):
    sums = _dot(jnp.concatenate([sc[1] for sc in scores], axis=0), tri)
    rows = scores[0][0].shape[0]
    out = []
    for n, ((log_beta, _, cm0), kill, carried) in enumerate(zip(scores, kills, carrieds)):
        later = sums[n * rows:(n + 1) * rows]
        log_a = (log_beta - later) + kill
        if carried is not None:
            log_a = log_a - jnp.concatenate([carried] * (log_a.shape[1] // LANES), axis=1)
        out.append((jnp.exp2(log_a).astype(BF16), later[:, :LANES] + cm0))
    return out


def _sb_kernel(n_conv, q_ref, k_ref, v_ref, *refs):
    conv_in, o_ref, conv_out = refs[:n_conv], refs[n_conv], refs[n_conv + 1:2 * n_conv + 1]
    tri_ref, mask_ref, acc_ref, car_ref = refs[2 * n_conv + 1:]
    for src, dst in zip(conv_in, conv_out):
        dst[...] = src[...].astype(BF16)

    bi = pl.program_id(0)
    qi = pl.program_id(1)
    tq, w = SB_Q_TILE, SB_WINDOW
    subs = q_ref.shape[1] // tq
    pairs = q_ref.shape[-1] // LANES
    row0 = qi * (subs * tq)

    @pl.when(jnp.logical_and(bi == 0, qi == 0))
    def _fill_tri():
        j = lax.broadcasted_iota(jnp.int32, (w, w), 0)
        s = lax.broadcasted_iota(jnp.int32, (w, w), 1)
        tri_ref[...] = jnp.where(j > s, 1.0, 0.0).astype(BF16)

    lane = lax.broadcasted_iota(jnp.int32, (tq, LANES), 1)
    lanes_of = lambda p: slice(p * LANES, (p + 1) * LANES)
    rows_of = lambda h: slice(h * tq, (h + 1) * tq)
    first_start = lambda h: pl.multiple_of(jnp.maximum(row0 + (h + 1) * tq - w, 0), tq)

    def key_positions(start):
        return start + lax.broadcasted_iota(jnp.int32, (2 * tq, w), 1)

    assert subs * tq >= w - tq

    @pl.when(qi <= 1)
    def _fill_masks():
        row = lax.broadcasted_iota(jnp.int32, (2 * tq, w), 0) & (tq - 1)
        for h in range(subs):
            mask_ref[h, 0], mask_ref[h, 1] = _sb_masks(
                key_positions(first_start(h)) < row0 + h * tq + row)

    def scores(p, starts, keeps):
        return [_sb_scores(_split_heads(q_ref[0, rows_of(h), lanes_of(p)], lane),
                           k_ref[0, pl.ds(starts[h], w), lanes_of(p)], keeps[h])
                for h in range(subs)]

    first_starts = [first_start(h) for h in range(subs)]

    def first_scores(p):
        return scores(p, first_starts, [mask_ref[h, 0] for h in range(subs)])

    def first_weights(p, sc):
        return _sb_weights(sc, tri_ref[...], [mask_ref[h, 1] for h in range(subs)], [None] * subs)

    def first_values(p, aws):
        least = None
        for h, (a, total) in enumerate(aws):
            o2 = _dot(a, v_ref[0, pl.ds(first_starts[h], w), lanes_of(p)])
            o_ref[0, rows_of(h), lanes_of(p)] = _merge_heads(o2, lane).astype(o_ref.dtype)
            acc_ref[p, h] = o2
            car_ref[p, h] = total
            m = jnp.min(total[:, 0:1])
            least = m if least is None else jnp.minimum(least, m)
        return least

    first = _software_pipeline(pairs, [first_scores, first_weights, first_values])
    live = functools.reduce(jnp.minimum, [first[p] for p in range(pairs)])
    last_start = first_starts[-1]

    @pl.when(jnp.logical_and(last_start > 0, live < SB_DEAD_LOG2))
    def _walk_back():
        for p in range(pairs):
            for h in range(subs):
                car_ref[p, h] = jnp.broadcast_to(car_ref[p, h][:, 0:1], (2 * tq, LANES))

        def body(state):
            step, _ = state
            done = [jnp.maximum(first_start(h) - step * w, 0) for h in range(subs)]
            starts = [pl.multiple_of(jnp.maximum(d - w, 0), tq) for d in done]
            masks = [_sb_masks(key_positions(s) < d) for s, d in zip(starts, done)]

            def more_scores(p):
                return scores(p, starts, [m[0] for m in masks])

            def weights(p, sc):
                return _sb_weights(sc, tri_ref[...], [m[1] for m in masks],
                                   [car_ref[p, h] for h in range(subs)])

            def values(p, aws):
                least = None
                for h, (a, total) in enumerate(aws):
                    acc_ref[p, h] += _dot(a, v_ref[0, pl.ds(starts[h], w), lanes_of(p)])
                    carried = car_ref[p, h] + total[:, 0:1]
                    car_ref[p, h] = carried
                    m = jnp.min(carried)
                    least = m if least is None else jnp.minimum(least, m)
                return least

            mins = _software_pipeline(pairs, [more_scores, weights, values])
            return step + 1, functools.reduce(jnp.minimum, [mins[p] for p in range(pairs)])

        def cond(state):
            step, nxt = state
            return jnp.logical_and(last_start - step * w > 0, nxt < SB_DEAD_LOG2)

        lax.while_loop(cond, body, (jnp.int32(0), live))
        for p in range(pairs):
            for h in range(subs):
                o_ref[0, rows_of(h), lanes_of(p)] = _merge_heads(
                    acc_ref[p, h], lane).astype(o_ref.dtype)


def _sb_attention(q, k, v, weights):
    b, s, wd = q.shape
    pairs = wd // LANES
    subs = SB_STEP_ROWS // SB_Q_TILE
    nq = s // SB_STEP_ROWS
    tile = lambda bi, i: (bi, i, 0)
    whole = lambda bi, i: (bi, 0, 0)
    once = pl.Buffered(1)
    conv_specs = [
        pl.BlockSpec((size, w.shape[1]),
                     lambda bi, i, last=w.shape[0] // size - 1: (jnp.minimum(bi * nq + i, last), 0))
        for w, size in zip(weights, _conversion_blocks(weights, b * nq))]
    return pl.pallas_call(
        functools.partial(_sb_kernel, len(weights)),
        grid=(b, nq),
        in_specs=[
            pl.BlockSpec((1, SB_STEP_ROWS, wd), tile),
            pl.BlockSpec((1, s, wd), whole, pipeline_mode=once),
            pl.BlockSpec((1, s, wd), whole, pipeline_mode=once),
        ] + conv_specs,
        out_specs=[pl.BlockSpec((1, SB_STEP_ROWS, wd), tile)] + conv_specs,
        out_shape=[jax.ShapeDtypeStruct((b, s, wd), BF16)]
        + [jax.ShapeDtypeStruct(w.shape, BF16) for w in weights],
        scratch_shapes=[
            pltpu.VMEM((SB_WINDOW, SB_WINDOW), BF16),
            pltpu.VMEM((subs, 2, 2 * SB_Q_TILE, SB_WINDOW), F32),
            pltpu.VMEM((pairs, subs, 2 * SB_Q_TILE, LANES), F32),
            pltpu.VMEM((pairs, subs, 2 * SB_Q_TILE, LANES), F32),
        ],
        compiler_params=pltpu.CompilerParams(
            dimension_semantics=("arbitrary", "arbitrary"), vmem_limit_bytes=VMEM_LIMIT),
        name="stick_breaking_attention",
    )(q, k, v, *weights)


def _t5_log_bucket_starts():
    exact = REL_BUCKETS // 2
    d = np.arange(exact, 2 * REL_MAX_DIST, dtype=np.float32)
    large = exact + (np.log(d / np.float32(exact)) / np.float32(math.log(REL_MAX_DIST / exact))
                     * np.float32(REL_BUCKETS - exact)).astype(np.int32)
    large = np.minimum(large, REL_BUCKETS - 1)
    assert large[0] == exact and np.all(np.diff(large) >= 0)
    return [int(d[np.argmax(large >= bk)]) for bk in range(exact + 1, REL_BUCKETS)]


def _t5_causal_bucket(distance):
    d = jnp.maximum(distance, 0)
    bucket = jnp.minimum(d, REL_BUCKETS // 2)
    for start in _t5_log_bucket_starts():
        bucket = bucket + jnp.where(d >= start, 1, 0)
    return bucket


def _swa_fill_bias(relb_ref, bias_ref):
    t = ATT_BLOCK
    r = lax.broadcasted_iota(jnp.int32, (t, 2 * t), 0)
    c = lax.broadcasted_iota(jnp.int32, (t, 2 * t), 1)
    dist = r + t - c
    bucket = _t5_causal_bucket(dist)
    valid = jnp.logical_and(dist >= 0, dist < WINDOW)
    for h in range(SWA_HEADS):
        bias = jnp.zeros((t, 2 * t), F32)
        for bk in range(REL_BUCKETS):
            bias = jnp.where(bucket == bk, relb_ref[bk, h], bias)
        bias_ref[h] = jnp.where(valid, bias, MASK_NEG)


def _swa_schedule(sinks_ref, q_ref, kvc_ref, kvp_ref, bias_ref, o_ref, no_previous):
    t = ATT_BLOCK
    kvw = kvc_ref.shape[-1] // 4
    group = SWA_HEADS // SWA_KV_HEADS
    tiles = q_ref.shape[0] // t
    rows_of = lambda n: slice(n * t, (n + 1) * t)
    lane = lax.broadcasted_iota(jnp.int32, (t, LANES), 1)
    row2 = lax.broadcasted_iota(jnp.int32, (2 * t, 1), 0)
    col2 = lax.broadcasted_iota(jnp.int32, (2 * t, 2 * t), 1)
    prev_pen = jnp.where(jnp.logical_and(col2 < t, no_previous), MASK_NEG, 0.0)

    def kv_window(n, variant):
        cols = slice(variant * kvw, (variant + 1) * kvw)
        before = kvc_ref[rows_of(n - 1), cols] if n else kvp_ref[:, cols]
        return jnp.concatenate([before, kvc_ref[rows_of(n), cols]], axis=0)

    items = [(n, half, j) for n in range(tiles) for half in range(2) for j in range(SWA_KV_HEADS)]
    heads_of = lambda half, j: [h for h in range(j * group, (j + 1) * group) if h % 2 == half]

    def probabilities(i):
        n, half, j = items[i]
        heads = heads_of(half, j)
        in_half = (lane >= HEAD_DIM) if half else (lane < HEAD_DIM)
        qq = jnp.concatenate(
            [jnp.where(in_half, q_ref[rows_of(n), (h // 2) * LANES:(h // 2 + 1) * LANES], 0)
             for h in heads], axis=0)
        s = _dot_nt(qq, kv_window(n, 0 + int(j != half)))
        s = s + jnp.concatenate([bias_ref[h] for h in heads], axis=0)
        if n == 0:
            s = s + prev_pen
        sink = jnp.where(row2 < t, sinks_ref[heads[0]], sinks_ref[heads[1]])
        m = jnp.maximum(jnp.max(s, axis=-1, keepdims=True), sink)
        p = jnp.exp(s - m)
        denom = jnp.sum(p, axis=-1, keepdims=True) + jnp.exp(sink - m)
        return p.astype(BF16), 1.0 / denom

    def values(i, pd):
        n, half, j = items[i]
        p, inv_denom = pd
        return _dot(p, kv_window(n, 2 + int(j != half))) * inv_denom

    thunks, outs = _pipeline_schedule(len(items), [probabilities, values])

    def store():
        for i, (n, half, j) in enumerate(items):
            if half == 0:
                even, odd = outs[i], outs[items.index((n, 1, j))]
                for e, h in enumerate(heads_of(0, j)):
                    o_ref[rows_of(n), (h // 2) * LANES:(h // 2 + 1) * LANES] = jnp.where(
                        lane < HEAD_DIM, even[e * t:(e + 1) * t], odd[e * t:(e + 1) * t]
                    ).astype(o_ref.dtype)

    return thunks + [store]


def _tail_schedule(ha_ref, sb_ref, sw_ref, sbg_ref, swg_ref, wo_ref, l1g_ref, l1b_ref,
                   wgu_ref, wd_ref, l2g_ref, l2b_ref, o_ref):
    d = ha_ref.shape[1]
    d_ff = wd_ref.shape[0]
    mw = MXU_WIDTH
    bounds = list(range(0, d_ff, FFN_CHUNK)) + [d_ff]
    n_chunks = len(bounds) - 1
    out_tiles = range(d // mw)
    st = {}

    def merged():
        sb = _rms_norm(sb_ref[...].astype(F32), sbg_ref[...]).astype(BF16)
        sw = _rms_norm(sw_ref[...].astype(F32), swg_ref[...]).astype(BF16)
        st["merged"] = jnp.concatenate([sb, sw], axis=1)

    def out_projection(n):
        st["mix", n] = _dot(st["merged"], wo_ref[:, n * mw:(n + 1) * mw])

    def first_norm():
        mix = jnp.concatenate([st.pop(("mix", n)) for n in out_tiles], axis=1)
        st.pop("merged")
        h1 = _layer_norm(ha_ref[...] + mix, l1g_ref[...], l1b_ref[...])
        st["h1b"] = h1.astype(BF16)
        for n in out_tiles:
            st["acc", n] = ALPHA * h1[:, n * mw:(n + 1) * mw]

    def activation(c, n):
        lo = bounds[c] + n * mw
        gate = _dot(st["h1b"], wgu_ref[:, lo:lo + mw])
        up = _dot(st["h1b"], wgu_ref[:, d_ff + lo:d_ff + lo + mw])
        st["act", c, n] = (gate / (1.0 + jnp.exp(-gate)) * up).astype(BF16)

    def down(c, n):
        tiles = range((bounds[c + 1] - bounds[c]) // mw)
        if n == 0:
            st["act", c] = jnp.concatenate([st.pop(("act", c, k)) for k in tiles], axis=1)
        st["acc", n] = st["acc", n] + _dot(st["act", c], wd_ref[bounds[c]:bounds[c + 1],
                                                               n * mw:(n + 1) * mw])

    def second_norm():
        y = jnp.concatenate([st.pop(("acc", n)) for n in out_tiles], axis=1)
        o_ref[...] = _layer_norm(y, l2g_ref[...], l2b_ref[...])

    act_thunks = [[functools.partial(activation, c, n)
                   for n in range((bounds[c + 1] - bounds[c]) // mw)] for c in range(n_chunks)]
    down_thunks = [[functools.partial(down, c, n) for n in out_tiles] for c in range(n_chunks)]
    ffn = list(act_thunks[0])
    for c in range(1, n_chunks):
        ffn += _interleave(act_thunks[c], down_thunks[c - 1])
    ffn += down_thunks[-1]
    return (merged, [functools.partial(out_projection, n) for n in out_tiles], first_norm,
            ffn, second_norm)


def _attn_tail_kernel(tiles_per_seq, last_tile, sinks_ref, relb_ref, q_ref, kvc_ref, kvp_ref,
                      ha_ref, sb_ref, sbg_ref, swg_ref, wo_ref, l1g_ref, l1b_ref,
                      wgu_ref, wd_ref, l2g_ref, l2b_ref, o_ref, bias_ref, sw_ref):
    t = pl.program_id(0)

    @pl.when(t == 0)
    def _first_step():
        _swa_fill_bias(relb_ref, bias_ref)
        sw_ref[1] = jnp.zeros(sw_ref.shape[1:], sw_ref.dtype)

    no_previous = (jnp.minimum(t, last_tile) % tiles_per_seq) == 0
    attention = _swa_schedule(sinks_ref, q_ref.at[0], kvc_ref.at[0], kvp_ref.at[0], bias_ref,
                              sw_ref.at[t % 2], no_previous)
    merged, out_proj, first_norm, ffn, second_norm = _tail_schedule(
        ha_ref, sb_ref, sw_ref.at[(t + 1) % 2], sbg_ref, swg_ref, wo_ref,
        l1g_ref, l1b_ref, wgu_ref, wd_ref, l2g_ref, l2b_ref, o_ref)
    tail = [merged] + out_proj + [first_norm] + ffn + [second_norm]
    for thunk in _interleave(tail, attention):
        thunk()


def _attn_tail_block(q_sw, kv_sw, sinks, rel_bias, ha, sb2, sb_g, sw_g, w_out_b, ln1_g, ln1_b,
                     w_gate_up_b, w_down_b, ln2_g, ln2_b):
    b, s, wd = q_sw.shape
    n, d = ha.shape
    tiles_per_seq = s // TAIL_ROWS
    last_tile = n // TAIL_ROWS - 1
    key_tiles = TAIL_ROWS // ATT_BLOCK
    att = lambda t: jnp.minimum(t, last_tile)
    cur = lambda t: (att(t) // tiles_per_seq, att(t) % tiles_per_seq, 0)
    prev = lambda t: (att(t) // tiles_per_seq,
                      jnp.maximum(att(t) % tiles_per_seq * key_tiles - 1, 0), 0)
    row = lambda t: (jnp.maximum(t - 1, 0), 0)
    fixed = lambda t: (0, 0)
    smem = pl.BlockSpec(memory_space=pltpu.SMEM)
    vec = lambda a: pl.BlockSpec((1, a.shape[-1]), fixed)
    weight = lambda a: pl.BlockSpec(a.shape, fixed, pipeline_mode=pl.Buffered(1))
    return pl.pallas_call(
        functools.partial(_attn_tail_kernel, tiles_per_seq, last_tile),
        grid=(last_tile + 2,),
        in_specs=[
            smem, smem,
            pl.BlockSpec((1, TAIL_ROWS, wd), cur),
            pl.BlockSpec((1, TAIL_ROWS, kv_sw.shape[-1]), cur),
            pl.BlockSpec((1, ATT_BLOCK, kv_sw.shape[-1]), prev),
            pl.BlockSpec((TAIL_ROWS, d), row),
            pl.BlockSpec((TAIL_ROWS, sb2.shape[-1]), row),
            vec(sb_g), vec(sw_g), weight(w_out_b), vec(ln1_g), vec(ln1_b),
            weight(w_gate_up_b), weight(w_down_b), vec(ln2_g), vec(ln2_b),
        ],
        out_specs=pl.BlockSpec((TAIL_ROWS, d), row),
        out_shape=jax.ShapeDtypeStruct((n, d), F32),
        scratch_shapes=[
            pltpu.VMEM((SWA_HEADS, ATT_BLOCK, 2 * ATT_BLOCK), F32),
            pltpu.VMEM((2, TAIL_ROWS, wd), BF16),
        ],
        compiler_params=pltpu.CompilerParams(
            dimension_semantics=("arbitrary",), vmem_limit_bytes=VMEM_LIMIT),
        name="swa_attention_and_dense_tail",
    )(sinks, rel_bias, q_sw, kv_sw, kv_sw, ha, sb2, sb_g, sw_g, w_out_b, ln1_g, ln1_b,
      w_gate_up_b, w_down_b, ln2_g, ln2_b)


SB_GROUP = 8
SB_LIVE_DONE = 1e30


def _sb_fast_masks(mask_ref):
    tq, w = SB_Q_TILE, SB_WINDOW
    row = lax.broadcasted_iota(jnp.int32, (2 * tq, w), 0) & (tq - 1)
    col = lax.broadcasted_iota(jnp.int32, (2 * tq, w), 1)
    causal = col < row + (w - tq)
    mask_ref[0, 0], mask_ref[0, 1] = _sb_masks(causal)
    for h in range(mask_ref.shape[0] - 1):
        first_key = w - (h + 1) * tq
        mask_ref[1 + h, 0], mask_ref[1 + h, 1] = _sb_masks(
            jnp.logical_and(causal, col >= first_key))


def _sb_fast_schedule(q_ref, kc_ref, kp_ref, vc_ref, vp_ref, tri_ref, mask_ref, o_ref,
                      no_previous, lives):
    tq, w = SB_Q_TILE, SB_WINDOW
    subs = q_ref.shape[0] // tq
    pairs = q_ref.shape[-1] // LANES
    clamped = w // tq - 1
    assert mask_ref.shape[0] == clamped + 1 and subs % SB_GROUP == 0
    lane = lax.broadcasted_iota(jnp.int32, (tq, LANES), 1)
    lanes_of = lambda p: slice(p * LANES, (p + 1) * LANES)
    rows_of = lambda h: slice(h * tq, (h + 1) * tq)

    def window(h, cur_ref, prev_ref, p):
        start = (h + 1) * tq - w
        if start >= 0:
            return cur_ref[start:start + w, lanes_of(p)]
        return jnp.concatenate([prev_ref[w + start:w, lanes_of(p)],
                                cur_ref[0:w + start, lanes_of(p)]], axis=0)

    def mask(h, which):
        if h >= clamped:
            return mask_ref[0, which]
        return mask_ref[jnp.where(no_previous, 1 + h, 0), which]

    items = [(p, g) for p in range(pairs) for g in range(subs // SB_GROUP)]
    group = lambda g: range(g * SB_GROUP, (g + 1) * SB_GROUP)

    def scores(i):
        p, g = items[i]
        return [_sb_scores(_split_heads(q_ref[rows_of(h), lanes_of(p)], lane),
                           window(h, kc_ref, kp_ref, p), mask(h, 0)) for h in group(g)]

    def weights(i, sc):
        return _sb_weights(sc, tri_ref[...], [mask(h, 1) for h in group(items[i][1])],
                           [None] * SB_GROUP)

    def values(i, aws):
        p, g = items[i]
        for h, (a, total) in zip(group(g), aws):
            o2 = _dot(a, window(h, vc_ref, vp_ref, p))
            o_ref[rows_of(h), lanes_of(p)] = _merge_heads(o2, lane).astype(o_ref.dtype)
            m = jnp.min(total[:, 0:1])
            lives.append(m if h > clamped else jnp.where(no_previous, SB_LIVE_DONE, m))

    return _pipeline_schedule(len(items), [scores, weights, values])[0]


def _fast_kernel(tiles_per_seq, last_tile, sinks_ref, relb_ref, qsw_ref, kvc_ref, kvp_ref,
                 qsb_ref, kc_ref, kp_ref, vc_ref, vp_ref,
                 ha_ref, sbg_ref, swg_ref, wo_ref, l1g_ref, l1b_ref,
                 wgu_ref, wd_ref, l2g_ref, l2b_ref, o_ref, live_ref,
                 bias_ref, sw_ref, sb_ref, tri_ref, mask_ref):
    t = pl.program_id(0)

    @pl.when(t == 0)
    def _fill_constants():
        _swa_fill_bias(relb_ref, bias_ref)
        _sb_fast_masks(mask_ref)
        j = lax.broadcasted_iota(jnp.int32, tri_ref.shape, 0)
        s = lax.broadcasted_iota(jnp.int32, tri_ref.shape, 1)
        tri_ref[...] = jnp.where(j > s, 1.0, 0.0).astype(BF16)

    def body(with_attention, with_tail):
        program, lives = [], [jnp.float32(SB_LIVE_DONE)]
        if with_attention:
            no_previous = (t % tiles_per_seq) == 0
            swa = _swa_schedule(sinks_ref, qsw_ref.at[0], kvc_ref.at[0], kvp_ref.at[0],
                                bias_ref, sw_ref.at[t % 2], no_previous)
            sb = _sb_fast_schedule(qsb_ref.at[0], kc_ref.at[0], kp_ref.at[0], vc_ref.at[0],
                                   vp_ref.at[0], tri_ref, mask_ref, sb_ref.at[t % 2],
                                   no_previous, lives)
            program = _interleave(swa, sb)
        if with_tail:
            merged, out_proj, first_norm, ffn, second_norm = _tail_schedule(
                ha_ref, sb_ref.at[(t + 1) % 2], sw_ref.at[(t + 1) % 2], sbg_ref, swg_ref,
                wo_ref, l1g_ref, l1b_ref, wgu_ref, wd_ref, l2g_ref, l2b_ref, o_ref)
            tail = [merged] + out_proj + [first_norm] + ffn + [second_norm]
            program = _interleave(tail, program) if program else tail
        for thunk in program:
            thunk()
        live_ref[...] = jnp.full(live_ref.shape, functools.reduce(jnp.minimum, lives), F32)

    pl.when(t == 0)(functools.partial(body, True, False))
    pl.when(jnp.logical_and(t > 0, t <= last_tile))(functools.partial(body, True, True))
    pl.when(t > last_tile)(functools.partial(body, False, True))


def _fast_block(q_sw, kv_sw, q_sb, k_sb, v_sb, sinks, rel_bias, ha, sb_g, sw_g, w_out_b,
                ln1_g, ln1_b, w_gate_up_b, w_down_b, ln2_g, ln2_b):
    b, s, wd = q_sw.shape
    n, d = ha.shape
    tiles_per_seq = s // TAIL_ROWS
    last_tile = n // TAIL_ROWS - 1
    att = lambda t: jnp.minimum(t, last_tile)
    cur = lambda t: (att(t) // tiles_per_seq, att(t) % tiles_per_seq, 0)

    def prev(rows):
        per_tile = TAIL_ROWS // rows
        return lambda t: (att(t) // tiles_per_seq,
                          jnp.maximum(att(t) % tiles_per_seq * per_tile - 1, 0), 0)

    row = lambda t: (jnp.maximum(t - 1, 0), 0)
    fixed = lambda t: (0, 0)
    smem = pl.BlockSpec(memory_space=pltpu.SMEM)
    vec = lambda a: pl.BlockSpec((1, a.shape[-1]), fixed)
    weight = lambda a: pl.BlockSpec(a.shape, fixed, pipeline_mode=pl.Buffered(1))
    tile3 = lambda a: pl.BlockSpec((1, TAIL_ROWS, a.shape[-1]), cur)
    before = lambda a, rows: pl.BlockSpec((1, rows, a.shape[-1]), prev(rows))
    steps = last_tile + 2
    return pl.pallas_call(
        functools.partial(_fast_kernel, tiles_per_seq, last_tile),
        grid=(steps,),
        in_specs=[
            smem, smem,
            tile3(q_sw), tile3(kv_sw), before(kv_sw, ATT_BLOCK),
            tile3(q_sb), tile3(k_sb), before(k_sb, SB_WINDOW), tile3(v_sb), before(v_sb, SB_WINDOW),
            pl.BlockSpec((TAIL_ROWS, d), row),
            vec(sb_g), vec(sw_g), weight(w_out_b), vec(ln1_g), vec(ln1_b),
            weight(w_gate_up_b), weight(w_down_b), vec(ln2_g), vec(ln2_b),
        ],
        out_specs=[pl.BlockSpec((TAIL_ROWS, d), row),
                   pl.BlockSpec((1, 8, LANES), lambda t: (t, 0, 0))],
        out_shape=[jax.ShapeDtypeStruct((n, d), F32),
                   jax.ShapeDtypeStruct((steps, 8, LANES), F32)],
        scratch_shapes=[
            pltpu.VMEM((SWA_HEADS, ATT_BLOCK, 2 * ATT_BLOCK), F32),
            pltpu.VMEM((2, TAIL_ROWS, wd), BF16),
            pltpu.VMEM((2, TAIL_ROWS, q_sb.shape[-1]), BF16),
            pltpu.VMEM((SB_WINDOW, SB_WINDOW), BF16),
            pltpu.VMEM((SB_WINDOW // SB_Q_TILE, 2, 2 * SB_Q_TILE, SB_WINDOW), F32),
        ],
        compiler_params=pltpu.CompilerParams(
            dimension_semantics=("arbitrary",), vmem_limit_bytes=VMEM_LIMIT),
        name="attention_and_dense_tail",
    )(sinks, rel_bias, q_sw, kv_sw, kv_sw, q_sb, k_sb, k_sb, v_sb, v_sb, ha, sb_g, sw_g, w_out_b,
      ln1_g, ln1_b, w_gate_up_b, w_down_b, ln2_g, ln2_b)


def kernel(x, ln_in_g, ln_in_b, w_in, sb_norm_g, swa_norm_g, sinks, rel_bias, w_out,
           ln1_g, ln1_b, w_gate_up, w_down, ln2_g, ln2_b):
    b, s, d = x.shape
    assert w_in.shape[0] == DEPTH == 1
    sb_w = SB_HEADS * HEAD_DIM
    sw_w = SWA_HEADS * HEAD_DIM
    kv_w = SWA_KV_HEADS * HEAD_DIM
    assert w_in.shape[-1] == 3 * sb_w + sw_w + 2 * kv_w
    assert kv_w == LANES and SWA_HEADS // SWA_KV_HEADS == 4
    assert s % SB_WINDOW == 0 and s % SB_STEP_ROWS == 0 and s % TAIL_ROWS == 0
    assert (b * s) % ROW_TILE == 0 and WINDOW <= ATT_BLOCK and d % MXU_WIDTH == 0
    assert FFN_CHUNK % MXU_WIDTH == 0 and w_down.shape[1] % MXU_WIDTH == 0

    x2 = x.reshape(b * s, d)
    row_vec = lambda a: a.reshape(1, -1)

    names = ("q_sb", "k_sb", "v_sb", "q_sw", "k_sw", "v_sw")
    widths = (sb_w, sb_w, sb_w, sw_w, kv_w, kv_w)
    splits, lo = {}, 0
    for name, wd in zip(names, widths):
        splits[name] = (lo, lo + wd)
        lo += wd
    ha, q_sb, k_sb, v_sb, q_sw, kv_sw, w_out_b, w_gate_up_b, w_down_b = _in_projection(
        x2, row_vec(ln_in_g), row_vec(ln_in_b), w_in[0], splits,
        [w_out[0], w_gate_up[0], w_down[0]])

    to3 = lambda a: a.reshape(b, s, a.shape[-1])
    q_sb, k_sb, v_sb, q_sw, kv_sw = map(to3, (q_sb, k_sb, v_sb, q_sw, kv_sw))
    tail_args = (row_vec(sb_norm_g[0]), row_vec(swa_norm_g[0]), w_out_b,
                 row_vec(ln1_g[0]), row_vec(ln1_b[0]), w_gate_up_b, w_down_b,
                 row_vec(ln2_g[0]), row_vec(ln2_b[0]))

    fast, live = _fast_block(q_sw, kv_sw, q_sb, k_sb, v_sb, sinks[0], rel_bias, ha, *tail_args)
    n_tiles = (b * s) // TAIL_ROWS

    def general(_):
        (sb_out,) = _sb_attention(q_sb, k_sb, v_sb, [])
        return _attn_tail_block(q_sw, kv_sw, sinks[0], rel_bias, ha,
                                sb_out.reshape(b * s, sb_w), *tail_args)

    out = lax.cond(jnp.any(live[:n_tiles, 0, 0] < SB_DEAD_LOG2), general, lambda _: fast, None)
    return out.reshape(b, s, d)
```

```python
import functools
import math

import jax
import jax.numpy as jnp
import numpy as np
from jax import lax
from jax.experimental import pallas as pl
from jax.experimental.pallas import tpu as pltpu

F32 = jnp.float32
BF16 = jnp.bfloat16

HEAD_DIM = 64
SB_HEADS = 8
SWA_HEADS = 8
SWA_KV_HEADS = 2
WINDOW = 128
REL_BUCKETS = 32
REL_MAX_DIST = 128
LN_EPS = 1e-5
RMS_EPS = 1e-6
DEPTH = 1
ALPHA = (2 * DEPTH) ** 0.25
QK_SCALE = HEAD_DIM ** -0.5
LOG2E = math.log2(math.e)

LANES = 128
BF16_ROWS = 16
MXU_WIDTH = 256
ATT_BLOCK = 128
SB_WINDOW = 256
SB_Q_TILE = 64
SB_STEP_ROWS = 512
ROW_TILE = 1024
PROJ_PARTS = 2
TAIL_ROWS = 512
FFN_CHUNK = 4 * MXU_WIDTH
VMEM_LIMIT = 56 * 1024 * 1024

SB_DEAD_LOG2 = 127.0
MASK_NEG = -1e30


def _layer_norm(x, g, b):
    mu = jnp.mean(x, axis=-1, keepdims=True)
    xc = x - mu
    var = jnp.mean(xc * xc, axis=-1, keepdims=True)
    return xc * lax.rsqrt(var + LN_EPS) * g + b


def _rms_norm(x, g):
    ms = jnp.mean(x * x, axis=-1, keepdims=True)
    return x * lax.rsqrt(ms + RMS_EPS) * g


def _dot(a, b):
    return jnp.dot(a, b, preferred_element_type=F32)


def _dot_nt(a, b):
    return lax.dot_general(a, b, (((1,), (1,)), ((), ())), preferred_element_type=F32)


def _pipeline_schedule(n, stages):
    vals = [dict() for _ in stages]
    thunks = []
    for step in range(n + len(stages) - 1):
        for s, stage in enumerate(stages):
            i = step - s
            if 0 <= i < n:
                def thunk(s=s, i=i, stage=stage):
                    vals[s][i] = stage(i, vals[s - 1].pop(i)) if s else stage(i)
                thunks.append(thunk)
    return thunks, vals[-1]


def _software_pipeline(n, stages):
    thunks, results = _pipeline_schedule(n, stages)
    for thunk in thunks:
        thunk()
    return results


def _interleave(xs, ys):
    merged, taken = [], 0
    for i, x in enumerate(xs):
        merged.append(x)
        upto = (i + 1) * len(ys) // len(xs)
        merged.extend(ys[taken:upto])
        taken = upto
    return merged


def _inproj_kernel(splits, n_later, x_ref, g_ref, b_ref, w32_ref, *refs):
    later_in, refs = refs[:n_later], refs[n_later:]
    ha_ref, qsb_ref, ksb_ref, vsb_ref, qsw_ref, kv_ref = refs[:6]
    later_out, w_ref = refs[6:6 + n_later], refs[6 + n_later]
    rows = x_ref.shape[0] // PROJ_PARTS
    part = lambda i: slice(i * rows, (i + 1) * rows)
    assert splits["k_sw"][1] == splits["v_sw"][0]

    @pl.when(pl.program_id(0) == 0)
    def _convert_weight():
        w_ref[...] = w32_ref[...].astype(BF16)

    for src, dst in zip(later_in, later_out):
        dst[...] = src[...].astype(BF16)

    def normed(i):
        h32 = _layer_norm(x_ref[part(i), :], g_ref[...], b_ref[...])
        ha_ref[part(i), :] = ALPHA * h32
        return h32.astype(BF16)

    def projected(i, h):
        r = part(i)
        proj = lambda lo, hi: _dot(h, w_ref[:, lo:hi])
        qsb_ref[r, :] = (proj(*splits["q_sb"]) * QK_SCALE).astype(BF16)
        ksb_ref[r, :] = proj(*splits["k_sb"]).astype(BF16)
        vsb_ref[r, :] = proj(*splits["v_sb"]).astype(BF16)
        qsw_ref[r, :] = (proj(*splits["q_sw"]) * QK_SCALE).astype(BF16)
        kv = proj(splits["k_sw"][0], splits["v_sw"][1])
        kvw = kv.shape[-1] // 2
        k, v = kv[:, :kvw], kv[:, kvw:]
        kv_ref[r, 0 * kvw:1 * kvw] = k.astype(BF16)
        kv_ref[r, 1 * kvw:2 * kvw] = pltpu.roll(k, HEAD_DIM, axis=1).astype(BF16)
        kv_ref[r, 2 * kvw:3 * kvw] = v.astype(BF16)
        kv_ref[r, 3 * kvw:4 * kvw] = pltpu.roll(v, HEAD_DIM, axis=1).astype(BF16)

    _software_pipeline(PROJ_PARTS, [normed, projected])


def _in_projection(x2, g, b, w_in, splits, later_weights):
    n, d = x2.shape
    steps = n // ROW_TILE
    width = lambda name: splits[name][1] - splits[name][0]
    out_w = [width("q_sb"), width("k_sb"), width("v_sb"), width("q_sw"), 4 * width("k_sw")]
    row = lambda i: (i, 0)
    fixed = lambda i: (0, 0)
    later_specs = [
        pl.BlockSpec((size, w.shape[1]),
                     lambda i, last=w.shape[0] // size - 1: (jnp.minimum(i, last), 0))
        for w, size in zip(later_weights, _conversion_blocks(later_weights, steps))]
    return pl.pallas_call(
        functools.partial(_inproj_kernel, splits, len(later_weights)),
        grid=(steps,),
        in_specs=[
            pl.BlockSpec((ROW_TILE, d), row),
            pl.BlockSpec((1, d), fixed),
            pl.BlockSpec((1, d), fixed),
            pl.BlockSpec(w_in.shape, fixed, pipeline_mode=pl.Buffered(1)),
        ] + later_specs,
        out_specs=[pl.BlockSpec((ROW_TILE, d), row)]
        + [pl.BlockSpec((ROW_TILE, w), row) for w in out_w] + later_specs,
        out_shape=[jax.ShapeDtypeStruct((n, d), F32)]
        + [jax.ShapeDtypeStruct((n, w), BF16) for w in out_w]
        + [jax.ShapeDtypeStruct(w.shape, BF16) for w in later_weights],
        scratch_shapes=[pltpu.VMEM(w_in.shape, BF16)],
        compiler_params=pltpu.CompilerParams(
            dimension_semantics=("arbitrary",), vmem_limit_bytes=VMEM_LIMIT),
        name="ln_in_proj",
    )(x2, g, b, w_in, *later_weights)


def _conversion_blocks(weights, steps):
    sizes = []
    for w in weights:
        rows = w.shape[0]
        sizes.append(next(r for r in range(BF16_ROWS, rows + 1, BF16_ROWS)
                          if rows % r == 0 and rows // r <= steps))
    return sizes


def _split_heads(x, lane):
    zero = jnp.zeros_like(x)
    return jnp.concatenate([jnp.where(lane < HEAD_DIM, x, zero),
                            jnp.where(lane >= HEAD_DIM, x, zero)], axis=0)


def _merge_heads(o2, lane):
    t = o2.shape[0] // 2
    return jnp.where(lane < HEAD_DIM, o2[:t], o2[t:])


def _sb_masks(valid):
    return jnp.where(valid, LOG2E, 0.0), jnp.where(valid, 0.0, MASK_NEG)


def _sb_scores(qq, k, keep):
    z = _dot_nt(qq, k)
    zl = z * LOG2E
    c_nat = jnp.maximum(z, 0.0) + jnp.log(1.0 + jnp.exp2(-jnp.abs(zl)))
    cm = c_nat * keep
    return zl - cm, cm.astype(BF16), cm[:, :LANES]


def _sb_weights(scores, tri, kills, carrieds):
    sums = _dot(jnp.concatenate([sc[1] for sc in scores], axis=0), tri)
    rows = scores[0][0].shape[0]
    out = []
    for n, ((log_beta, _, cm0), kill, carried) in enumerate(zip(scores, kills, carrieds)):
        later = sums[n * rows:(n + 1) * rows]
        log_a = (log_beta - later) + kill
        if carried is not None:
            log_a = log_a - jnp.concatenate([carried] * (log_a.shape[1] // LANES), axis=1)
        out.append((jnp.exp2(log_a).astype(BF16), later[:, :LANES] + cm0))
    return out


def _sb_kernel(n_conv, q_ref, k_ref, v_ref, *refs):
    conv_in, o_ref, conv_out = refs[:n_conv], refs[n_conv], refs[n_conv + 1:2 * n_conv + 1]
    tri_ref, mask_ref, acc_ref, car_ref = refs[2 * n_conv + 1:]
    for src, dst in zip(conv_in, conv_out):
        dst[...] = src[...].astype(BF16)

    bi = pl.program_id(0)
    qi = pl.program_id(1)
    tq, w = SB_Q_TILE, SB_WINDOW
    subs = q_ref.shape[1] // tq
    pairs = q_ref.shape[-1] // LANES
    row0 = qi * (subs * tq)

    @pl.when(jnp.logical_and(bi == 0, qi == 0))
    def _fill_tri():
        j = lax.broadcasted_iota(jnp.int32, (w, w), 0)
        s = lax.broadcasted_iota(jnp.int32, (w, w), 1)
        tri_ref[...] = jnp.where(j > s, 1.0, 0.0).astype(BF16)

    lane = lax.broadcasted_iota(jnp.int32, (tq, LANES), 1)
    lanes_of = lambda p: slice(p * LANES, (p + 1) * LANES)
    rows_of = lambda h: slice(h * tq, (h + 1) * tq)
    first_start = lambda h: pl.multiple_of(jnp.maximum(row0 + (h + 1) * tq - w, 0), tq)

    def key_positions(start):
        return start + lax.broadcasted_iota(jnp.int32, (2 * tq, w), 1)

    assert subs * tq >= w - tq

    @pl.when(qi <= 1)
    def _fill_masks():
        row = lax.broadcasted_iota(jnp.int32, (2 * tq, w), 0) & (tq - 1)
        for h in range(subs):
            mask_ref[h, 0], mask_ref[h, 1] = _sb_masks(
                key_positions(first_start(h)) < row0 + h * tq + row)

    def scores(p, starts, keeps):
        return [_sb_scores(_split_heads(q_ref[0, rows_of(h), lanes_of(p)], lane),
                           k_ref[0, pl.ds(starts[h], w), lanes_of(p)], keeps[h])
                for h in range(subs)]

    first_starts = [first_start(h) for h in range(subs)]

    def first_scores(p):
        return scores(p, first_starts, [mask_ref[h, 0] for h in range(subs)])

    def first_weights(p, sc):
        return _sb_weights(sc, tri_ref[...], [mask_ref[h, 1] for h in range(subs)], [None] * subs)

    def first_values(p, aws):
        least = None
        for h, (a, total) in enumerate(aws):
            o2 = _dot(a, v_ref[0, pl.ds(first_starts[h], w), lanes_of(p)])
            o_ref[0, rows_of(h), lanes_of(p)] = _merge_heads(o2, lane).astype(o_ref.dtype)
            acc_ref[p, h] = o2
            car_ref[p, h] = total
            m = jnp.min(total[:, 0:1])
            least = m if least is None else jnp.minimum(least, m)
        return least

    first = _software_pipeline(pairs, [first_scores, first_weights, first_values])
    live = functools.reduce(jnp.minimum, [first[p] for p in range(pairs)])
    last_start = first_starts[-1]

    @pl.when(jnp.logical_and(last_start > 0, live < SB_DEAD_LOG2))
    def _walk_back():
        for p in range(pairs):
            for h in range(subs):
                car_ref[p, h] = jnp.broadcast_to(car_ref[p, h][:, 0:1], (2 * tq, LANES))

        def body(state):
            step, _ = state
            done = [jnp.maximum(first_start(h) - step * w, 0) for h in range(subs)]
            starts = [pl.multiple_of(jnp.maximum(d - w, 0), tq) for d in done]
            masks = [_sb_masks(key_positions(s) < d) for s, d in zip(starts, done)]

            def more_scores(p):
                return scores(p, starts, [m[0] for m in masks])

            def weights(p, sc):
                return _sb_weights(sc, tri_ref[...], [m[1] for m in masks],
                                   [car_ref[p, h] for h in range(subs)])

            def values(p, aws):
                least = None
                for h, (a, total) in enumerate(aws):
                    acc_ref[p, h] += _dot(a, v_ref[0, pl.ds(starts[h], w), lanes_of(p)])
                    carried = car_ref[p, h] + total[:, 0:1]
                    car_ref[p, h] = carried
                    m = jnp.min(carried)
                    least = m if least is None else jnp.minimum(least, m)
                return least

            mins = _software_pipeline(pairs, [more_scores, weights, values])
            return step + 1, functools.reduce(jnp.minimum, [mins[p] for p in range(pairs)])

        def cond(state):
            step, nxt = state
            return jnp.logical_and(last_start - step * w > 0, nxt < SB_DEAD_LOG2)

        lax.while_loop(cond, body, (jnp.int32(0), live))
        for p in range(pairs):
            for h in range(subs):
                o_ref[0, rows_of(h), lanes_of(p)] = _merge_heads(
                    acc_ref[p, h], lane).astype(o_ref.dtype)


def _sb_attention(q, k, v, weights):
    b, s, wd = q.shape
    pairs = wd // LANES
    subs = SB_STEP_ROWS // SB_Q_TILE
    nq = s // SB_STEP_ROWS
    tile = lambda bi, i: (bi, i, 0)
    whole = lambda bi, i: (bi, 0, 0)
    once = pl.Buffered(1)
    conv_specs = [
        pl.BlockSpec((size, w.shape[1]),
                     lambda bi, i, last=w.shape[0] // size - 1: (jnp.minimum(bi * nq + i, last), 0))
        for w, size in zip(weights, _conversion_blocks(weights, b * nq))]
    return pl.pallas_call(
        functools.partial(_sb_kernel, len(weights)),
        grid=(b, nq),
        in_specs=[
            pl.BlockSpec((1, SB_STEP_ROWS, wd), tile),
            pl.BlockSpec((1, s, wd), whole, pipeline_mode=once),
            pl.BlockSpec((1, s, wd), whole, pipeline_mode=once),
        ] + conv_specs,
        out_specs=[pl.BlockSpec((1, SB_STEP_ROWS, wd), tile)] + conv_specs,
        out_shape=[jax.ShapeDtypeStruct((b, s, wd), BF16)]
        + [jax.ShapeDtypeStruct(w.shape, BF16) for w in weights],
        scratch_shapes=[
            pltpu.VMEM((SB_WINDOW, SB_WINDOW), BF16),
            pltpu.VMEM((subs, 2, 2 * SB_Q_TILE, SB_WINDOW), F32),
            pltpu.VMEM((pairs, subs, 2 * SB_Q_TILE, LANES), F32),
            pltpu.VMEM((pairs, subs, 2 * SB_Q_TILE, LANES), F32),
        ],
        compiler_params=pltpu.CompilerParams(
            dimension_semantics=("arbitrary", "arbitrary"), vmem_limit_bytes=VMEM_LIMIT),
        name="stick_breaking_attention",
    )(q, k, v, *weights)


def _t5_log_bucket_starts():
    exact = REL_BUCKETS // 2
    d = np.arange(exact, 2 * REL_MAX_DIST, dtype=np.float32)
    large = exact + (np.log(d / np.float32(exact)) / np.float32(math.log(REL_MAX_DIST / exact))
                     * np.float32(REL_BUCKETS - exact)).astype(np.int32)
    large = np.minimum(large, REL_BUCKETS - 1)
    assert large[0] == exact and np.all(np.diff(large) >= 0)
    return [int(d[np.argmax(large >= bk)]) for bk in range(exact + 1, REL_BUCKETS)]


def _t5_causal_bucket(distance):
    d = jnp.maximum(distance, 0)
    bucket = jnp.minimum(d, REL_BUCKETS // 2)
    for start in _t5_log_bucket_starts():
        bucket = bucket + jnp.where(d >= start, 1, 0)
    return bucket


def _swa_fill_bias(relb_ref, bias_ref):
    t = ATT_BLOCK
    r = lax.broadcasted_iota(jnp.int32, (t, 2 * t), 0)
    c = lax.broadcasted_iota(jnp.int32, (t, 2 * t), 1)
    dist = r + t - c
    bucket = _t5_causal_bucket(dist)
    valid = jnp.logical_and(dist >= 0, dist < WINDOW)
    for h in range(SWA_HEADS):
        bias = jnp.zeros((t, 2 * t), F32)
        for bk in range(REL_BUCKETS):
            bias = jnp.where(bucket == bk, relb_ref[bk, h], bias)
        bias_ref[h] = jnp.where(valid, bias, MASK_NEG)


def _swa_schedule(sinks_ref, q_ref, kvc_ref, kvp_ref, bias_ref, o_ref, no_previous):
    t = ATT_BLOCK
    kvw = kvc_ref.shape[-1] // 4
    group = SWA_HEADS // SWA_KV_HEADS
    tiles = q_ref.shape[0] // t
    rows_of = lambda n: slice(n * t, (n + 1) * t)
    lane = lax.broadcasted_iota(jnp.int32, (t, LANES), 1)
    row2 = lax.broadcasted_iota(jnp.int32, (2 * t, 1), 0)
    col2 = lax.broadcasted_iota(jnp.int32, (2 * t, 2 * t), 1)
    prev_pen = jnp.where(jnp.logical_and(col2 < t, no_previous), MASK_NEG, 0.0)

    def kv_window(n, variant):
        cols = slice(variant * kvw, (variant + 1) * kvw)
        before = kvc_ref[rows_of(n - 1), cols] if n else kvp_ref[:, cols]
        return jnp.concatenate([before, kvc_ref[rows_of(n), cols]], axis=0)

    items = [(n, half, j) for n in range(tiles) for half in range(2) for j in range(SWA_KV_HEADS)]
    heads_of = lambda half, j: [h for h in range(j * group, (j + 1) * group) if h % 2 == half]

    def probabilities(i):
        n, half, j = items[i]
        heads = heads_of(half, j)
        in_half = (lane >= HEAD_DIM) if half else (lane < HEAD_DIM)
        qq = jnp.concatenate(
            [jnp.where(in_half, q_ref[rows_of(n), (h // 2) * LANES:(h // 2 + 1) * LANES], 0)
             for h in heads], axis=0)
        s = _dot_nt(qq, kv_window(n, 0 + int(j != half)))
        s = s + jnp.concatenate([bias_ref[h] for h in heads], axis=0)
        if n == 0:
            s = s + prev_pen
        sink = jnp.where(row2 < t, sinks_ref[heads[0]], sinks_ref[heads[1]])
        m = jnp.maximum(jnp.max(s, axis=-1, keepdims=True), sink)
        p = jnp.exp(s - m)
        denom = jnp.sum(p, axis=-1, keepdims=True) + jnp.exp(sink - m)
        return p.astype(BF16), 1.0 / denom

    def values(i, pd):
        n, half, j = items[i]
        p, inv_denom = pd
        return _dot(p, kv_window(n, 2 + int(j != half))) * inv_denom

    thunks, outs = _pipeline_schedule(len(items), [probabilities, values])

    def store():
        for i, (n, half, j) in enumerate(items):
            if half == 0:
                even, odd = outs[i], outs[items.index((n, 1, j))]
                for e, h in enumerate(heads_of(0, j)):
                    o_ref[rows_of(n), (h // 2) * LANES:(h // 2 + 1) * LANES] = jnp.where(
                        lane < HEAD_DIM, even[e * t:(e + 1) * t], odd[e * t:(e + 1) * t]
                    ).astype(o_ref.dtype)

    return thunks + [store]


def _tail_schedule(ha_ref, sb_ref, sw_ref, sbg_ref, swg_ref, wo_ref, l1g_ref, l1b_ref,
                   wgu_ref, wd_ref, l2g_ref, l2b_ref, o_ref):
    d = ha_ref.shape[1]
    d_ff = wd_ref.shape[0]
    mw = MXU_WIDTH
    bounds = list(range(0, d_ff, FFN_CHUNK)) + [d_ff]
    n_chunks = len(bounds) - 1
    out_tiles = range(d // mw)
    st = {}

    def merged():
        sb = _rms_norm(sb_ref[...].astype(F32), sbg_ref[...]).astype(BF16)
        sw = _rms_norm(sw_ref[...].astype(F32), swg_ref[...]).astype(BF16)
        st["merged"] = jnp.concatenate([sb, sw], axis=1)

    def out_projection(n):
        st["mix", n] = _dot(st["merged"], wo_ref[:, n * mw:(n + 1) * mw])

    def first_norm():
        mix = jnp.concatenate([st.pop(("mix", n)) for n in out_tiles], axis=1)
        st.pop("merged")
        h1 = _layer_norm(ha_ref[...] + mix, l1g_ref[...], l1b_ref[...])
        st["h1b"] = h1.astype(BF16)
        for n in out_tiles:
            st["acc", n] = ALPHA * h1[:, n * mw:(n + 1) * mw]

    def activation(c, n):
        lo = bounds[c] + n * mw
        gate = _dot(st["h1b"], wgu_ref[:, lo:lo + mw])
        up = _dot(st["h1b"], wgu_ref[:, d_ff + lo:d_ff + lo + mw])
        st["act", c, n] = (gate / (1.0 + jnp.exp(-gate)) * up).astype(BF16)

    def down(c, n):
        tiles = range((bounds[c + 1] - bounds[c]) // mw)
        if n == 0:
            st["act", c] = jnp.concatenate([st.pop(("act", c, k)) for k in tiles], axis=1)
        st["acc", n] = st["acc", n] + _dot(st["act", c], wd_ref[bounds[c]:bounds[c + 1],
                                                               n * mw:(n + 1) * mw])

    def second_norm():
        y = jnp.concatenate([st.pop(("acc", n)) for n in out_tiles], axis=1)
        o_ref[...] = _layer_norm(y, l2g_ref[...], l2b_ref[...])

    act_thunks = [[functools.partial(activation, c, n)
                   for n in range((bounds[c + 1] - bounds[c]) // mw)] for c in range(n_chunks)]
    down_thunks = [[functools.partial(down, c, n) for n in out_tiles] for c in range(n_chunks)]
    ffn = list(act_thunks[0])
    for c in range(1, n_chunks):
        ffn += _interleave(act_thunks[c], down_thunks[c - 1])
    ffn += down_thunks[-1]
    return (merged, [functools.partial(out_projection, n) for n in out_tiles], first_norm,
            ffn, second_norm)


def _attn_tail_kernel(tiles_per_seq, last_tile, sinks_ref, relb_ref, q_ref, kvc_ref, kvp_ref,
                      ha_ref, sb_ref, sbg_ref, swg_ref, wo_ref, l1g_ref, l1b_ref,
                      wgu_ref, wd_ref, l2g_ref, l2b_ref, o_ref, bias_ref, sw_ref):
    t = pl.program_id(0)

    @pl.when(t == 0)
    def _first_step():
        _swa_fill_bias(relb_ref, bias_ref)
        sw_ref[1] = jnp.zeros(sw_ref.shape[1:], sw_ref.dtype)

    no_previous = (jnp.minimum(t, last_tile) % tiles_per_seq) == 0
    attention = _swa_schedule(sinks_ref, q_ref.at[0], kvc_ref.at[0], kvp_ref.at[0], bias_ref,
                              sw_ref.at[t % 2], no_previous)
    merged, out_proj, first_norm, ffn, second_norm = _tail_schedule(
        ha_ref, sb_ref, sw_ref.at[(t + 1) % 2], sbg_ref, swg_ref, wo_ref,
        l1g_ref, l1b_ref, wgu_ref, wd_ref, l2g_ref, l2b_ref, o_ref)
    tail = [merged] + out_proj + [first_norm] + ffn + [second_norm]
    for thunk in _interleave(tail, attention):
        thunk()


def _attn_tail_block(q_sw, kv_sw, sinks, rel_bias, ha, sb2, sb_g, sw_g, w_out_b, ln1_g, ln1_b,
                     w_gate_up_b, w_down_b, ln2_g, ln2_b):
    b, s, wd = q_sw.shape
    n, d = ha.shape
    tiles_per_seq = s // TAIL_ROWS
    last_tile = n // TAIL_ROWS - 1
    key_tiles = TAIL_ROWS // ATT_BLOCK
    att = lambda t: jnp.minimum(t, last_tile)
    cur = lambda t: (att(t) // tiles_per_seq, att(t) % tiles_per_seq, 0)
    prev = lambda t: (att(t) // tiles_per_seq,
                      jnp.maximum(att(t) % tiles_per_seq * key_tiles - 1, 0), 0)
    row = lambda t: (jnp.maximum(t - 1, 0), 0)
    fixed = lambda t: (0, 0)
    smem = pl.BlockSpec(memory_space=pltpu.SMEM)
    vec = lambda a: pl.BlockSpec((1, a.shape[-1]), fixed)
    weight = lambda a: pl.BlockSpec(a.shape, fixed, pipeline_mode=pl.Buffered(1))
    return pl.pallas_call(
        functools.partial(_attn_tail_kernel, tiles_per_seq, last_tile),
        grid=(last_tile + 2,),
        in_specs=[
            smem, smem,
            pl.BlockSpec((1, TAIL_ROWS, wd), cur),
            pl.BlockSpec((1, TAIL_ROWS, kv_sw.shape[-1]), cur),
            pl.BlockSpec((1, ATT_BLOCK, kv_sw.shape[-1]), prev),
            pl.BlockSpec((TAIL_ROWS, d), row),
            pl.BlockSpec((TAIL_ROWS, sb2.shape[-1]), row),
            vec(sb_g), vec(sw_g), weight(w_out_b), vec(ln1_g), vec(ln1_b),
            weight(w_gate_up_b), weight(w_down_b), vec(ln2_g), vec(ln2_b),
        ],
        out_specs=pl.BlockSpec((TAIL_ROWS, d), row),
        out_shape=jax.ShapeDtypeStruct((n, d), F32),
        scratch_shapes=[
            pltpu.VMEM((SWA_HEADS, ATT_BLOCK, 2 * ATT_BLOCK), F32),
            pltpu.VMEM((2, TAIL_ROWS, wd), BF16),
        ],
        compiler_params=pltpu.CompilerParams(
            dimension_semantics=("arbitrary",), vmem_limit_bytes=VMEM_LIMIT),
        name="swa_attention_and_dense_tail",
    )(sinks, rel_bias, q_sw, kv_sw, kv_sw, ha, sb2, sb_g, sw_g, w_out_b, ln1_g, ln1_b,
      w_gate_up_b, w_down_b, ln2_g, ln2_b)


SB_GROUP = 8
SB_LIVE_DONE = 1e30


def _sb_fast_masks(mask_ref):
    tq, w = SB_Q_TILE, SB_WINDOW
    row = lax.broadcasted_iota(jnp.int32, (2 * tq, w), 0) & (tq - 1)
    col = lax.broadcasted_iota(jnp.int32, (2 * tq, w), 1)
    causal = col < row + (w - tq)
    mask_ref[0, 0], mask_ref[0, 1] = _sb_masks(causal)
    for h in range(mask_ref.shape[0] - 1):
        first_key = w - (h + 1) * tq
        mask_ref[1 + h, 0], mask_ref[1 + h, 1] = _sb_masks(
            jnp.logical_and(causal, col >= first_key))


def _sb_fast_schedule(q_ref, kc_ref, kp_ref, vc_ref, vp_ref, tri_ref, mask_ref, o_ref,
                      no_previous, lives):
    tq, w = SB_Q_TILE, SB_WINDOW
    subs = q_ref.shape[0] // tq
    pairs = q_ref.shape[-1] // LANES
    clamped = w // tq - 1
    assert mask_ref.shape[0] == clamped + 1 and subs % SB_GROUP == 0
    lane = lax.broadcasted_iota(jnp.int32, (tq, LANES), 1)
    lanes_of = lambda p: slice(p * LANES, (p + 1) * LANES)
    rows_of = lambda h: slice(h * tq, (h + 1) * tq)

    def window(h, cur_ref, prev_ref, p):
        start = (h + 1) * tq - w
        if start >= 0:
            return cur_ref[start:start + w, lanes_of(p)]
        return jnp.concatenate([prev_ref[w + start:w, lanes_of(p)],
                                cur_ref[0:w + start, lanes_of(p)]], axis=0)

    def mask(h, which):
        if h >= clamped:
            return mask_ref[0, which]
        return mask_ref[jnp.where(no_previous, 1 + h, 0), which]

    items = [(p, g) for p in range(pairs) for g in range(subs // SB_GROUP)]
    group = lambda g: range(g * SB_GROUP, (g + 1) * SB_GROUP)

    def scores(i):
        p, g = items[i]
        return [_sb_scores(_split_heads(q_ref[rows_of(h), lanes_of(p)], lane),
                           window(h, kc_ref, kp_ref, p), mask(h, 0)) for h in group(g)]

    def weights(i, sc):
        return _sb_weights(sc, tri_ref[...], [mask(h, 1) for h in group(items[i][1])],
                           [None] * SB_GROUP)

    def values(i, aws):
        p, g = items[i]
        for h, (a, total) in zip(group(g), aws):
            o2 = _dot(a, window(h, vc_ref, vp_ref, p))
            o_ref[rows_of(h), lanes_of(p)] = _merge_heads(o2, lane).astype(o_ref.dtype)
            m = jnp.min(total[:, 0:1])
            lives.append(m if h > clamped else jnp.where(no_previous, SB_LIVE_DONE, m))

    return _pipeline_schedule(len(items), [scores, weights, values])[0]


def _fast_kernel(tiles_per_seq, last_tile, sinks_ref, relb_ref, qsw_ref, kvc_ref, kvp_ref,
                 qsb_ref, kc_ref, kp_ref, vc_ref, vp_ref,
                 ha_ref, sbg_ref, swg_ref, wo_ref, l1g_ref, l1b_ref,
                 wgu_ref, wd_ref, l2g_ref, l2b_ref, o_ref, live_ref,
                 bias_ref, sw_ref, sb_ref, tri_ref, mask_ref):
    t = pl.program_id(0)

    @pl.when(t == 0)
    def _first_step():
        _swa_fill_bias(relb_ref, bias_ref)
        _sb_fast_masks(mask_ref)
        j = lax.broadcasted_iota(jnp.int32, tri_ref.shape, 0)
        s = lax.broadcasted_iota(jnp.int32, tri_ref.shape, 1)
        tri_ref[...] = jnp.where(j > s, 1.0, 0.0).astype(BF16)
        sw_ref[1] = jnp.zeros(sw_ref.shape[1:], sw_ref.dtype)
        sb_ref[1] = jnp.zeros(sb_ref.shape[1:], sb_ref.dtype)
        live_ref[...] = jnp.full(live_ref.shape, SB_LIVE_DONE, F32)

    no_previous = (jnp.minimum(t, last_tile) % tiles_per_seq) == 0
    lives = []
    swa = _swa_schedule(sinks_ref, qsw_ref.at[0], kvc_ref.at[0], kvp_ref.at[0], bias_ref,
                        sw_ref.at[t % 2], no_previous)
    sb = _sb_fast_schedule(qsb_ref.at[0], kc_ref.at[0], kp_ref.at[0], vc_ref.at[0], vp_ref.at[0],
                           tri_ref, mask_ref, sb_ref.at[t % 2], no_previous, lives)
    merged, out_proj, first_norm, ffn, second_norm = _tail_schedule(
        ha_ref, sb_ref.at[(t + 1) % 2], sw_ref.at[(t + 1) % 2], sbg_ref, swg_ref, wo_ref,
        l1g_ref, l1b_ref, wgu_ref, wd_ref, l2g_ref, l2b_ref, o_ref)
    tail = [merged] + out_proj + [first_norm] + ffn + [second_norm]
    for thunk in _interleave(tail, _interleave(swa, sb)):
        thunk()
    live_ref[...] = jnp.minimum(live_ref[...], functools.reduce(jnp.minimum, lives))


def _fast_block(q_sw, kv_sw, q_sb, k_sb, v_sb, sinks, rel_bias, ha, sb_g, sw_g, w_out_b,
                ln1_g, ln1_b, w_gate_up_b, w_down_b, ln2_g, ln2_b):
    b, s, wd = q_sw.shape
    n, d = ha.shape
    tiles_per_seq = s // TAIL_ROWS
    last_tile = n // TAIL_ROWS - 1
    att = lambda t: jnp.minimum(t, last_tile)
    cur = lambda t: (att(t) // tiles_per_seq, att(t) % tiles_per_seq, 0)

    def prev(rows):
        per_tile = TAIL_ROWS // rows
        return lambda t: (att(t) // tiles_per_seq,
                          jnp.maximum(att(t) % tiles_per_seq * per_tile - 1, 0), 0)

    row = lambda t: (jnp.maximum(t - 1, 0), 0)
    fixed = lambda t: (0, 0)
    smem = pl.BlockSpec(memory_space=pltpu.SMEM)
    vec = lambda a: pl.BlockSpec((1, a.shape[-1]), fixed)
    weight = lambda a: pl.BlockSpec(a.shape, fixed, pipeline_mode=pl.Buffered(1))
    tile3 = lambda a: pl.BlockSpec((1, TAIL_ROWS, a.shape[-1]), cur)
    before = lambda a, rows: pl.BlockSpec((1, rows, a.shape[-1]), prev(rows))
    steps = last_tile + 2
    return pl.pallas_call(
        functools.partial(_fast_kernel, tiles_per_seq, last_tile),
        grid=(steps,),
        in_specs=[
            smem, smem,
            tile3(q_sw), tile3(kv_sw), before(kv_sw, ATT_BLOCK),
            tile3(q_sb), tile3(k_sb), before(k_sb, SB_WINDOW), tile3(v_sb), before(v_sb, SB_WINDOW),
            pl.BlockSpec((TAIL_ROWS, d), row),
            vec(sb_g), vec(sw_g), weight(w_out_b), vec(ln1_g), vec(ln1_b),
            weight(w_gate_up_b), weight(w_down_b), vec(ln2_g), vec(ln2_b),
        ],
        out_specs=[pl.BlockSpec((TAIL_ROWS, d), row),
                   pl.BlockSpec((8, LANES), fixed)],
        out_shape=[jax.ShapeDtypeStruct((n, d), F32),
                   jax.ShapeDtypeStruct((8, LANES), F32)],
        scratch_shapes=[
            pltpu.VMEM((SWA_HEADS, ATT_BLOCK, 2 * ATT_BLOCK), F32),
            pltpu.VMEM((2, TAIL_ROWS, wd), BF16),
            pltpu.VMEM((2, TAIL_ROWS, q_sb.shape[-1]), BF16),
            pltpu.VMEM((SB_WINDOW, SB_WINDOW), BF16),
            pltpu.VMEM((SB_WINDOW // SB_Q_TILE, 2, 2 * SB_Q_TILE, SB_WINDOW), F32),
        ],
        compiler_params=pltpu.CompilerParams(
            dimension_semantics=("arbitrary",), vmem_limit_bytes=VMEM_LIMIT),
        name="attention_and_dense_tail",
    )(sinks, rel_bias, q_sw, kv_sw, kv_sw, q_sb, k_sb, k_sb, v_sb, v_sb, ha, sb_g, sw_g, w_out_b,
      ln1_g, ln1_b, w_gate_up_b, w_down_b, ln2_g, ln2_b)


def kernel(x, ln_in_g, ln_in_b, w_in, sb_norm_g, swa_norm_g, sinks, rel_bias, w_out,
           ln1_g, ln1_b, w_gate_up, w_down, ln2_g, ln2_b):
    b, s, d = x.shape
    assert w_in.shape[0] == DEPTH == 1
    sb_w = SB_HEADS * HEAD_DIM
    sw_w = SWA_HEADS * HEAD_DIM
    kv_w = SWA_KV_HEADS * HEAD_DIM
    assert w_in.shape[-1] == 3 * sb_w + sw_w + 2 * kv_w
    assert kv_w == LANES and SWA_HEADS // SWA_KV_HEADS == 4
    assert s % SB_WINDOW == 0 and s % SB_STEP_ROWS == 0 and s % TAIL_ROWS == 0
    assert (b * s) % ROW_TILE == 0 and WINDOW <= ATT_BLOCK and d % MXU_WIDTH == 0
    assert FFN_CHUNK % MXU_WIDTH == 0 and w_down.shape[1] % MXU_WIDTH == 0

    x2 = x.reshape(b * s, d)
    row_vec = lambda a: a.reshape(1, -1)

    names = ("q_sb", "k_sb", "v_sb", "q_sw", "k_sw", "v_sw")
    widths = (sb_w, sb_w, sb_w, sw_w, kv_w, kv_w)
    splits, lo = {}, 0
    for name, wd in zip(names, widths):
        splits[name] = (lo, lo + wd)
        lo += wd
    ha, q_sb, k_sb, v_sb, q_sw, kv_sw, w_out_b, w_gate_up_b, w_down_b = _in_projection(
        x2, row_vec(ln_in_g), row_vec(ln_in_b), w_in[0], splits,
        [w_out[0], w_gate_up[0], w_down[0]])

    to3 = lambda a: a.reshape(b, s, a.shape[-1])
    q_sb, k_sb, v_sb, q_sw, kv_sw = map(to3, (q_sb, k_sb, v_sb, q_sw, kv_sw))
    tail_args = (row_vec(sb_norm_g[0]), row_vec(swa_norm_g[0]), w_out_b,
                 row_vec(ln1_g[0]), row_vec(ln1_b[0]), w_gate_up_b, w_down_b,
                 row_vec(ln2_g[0]), row_vec(ln2_b[0]))

    fast, live = _fast_block(q_sw, kv_sw, q_sb, k_sb, v_sb, sinks[0], rel_bias, ha, *tail_args)

    def general(_):
        (sb_out,) = _sb_attention(q_sb, k_sb, v_sb, [])
        return _attn_tail_block(q_sw, kv_sw, sinks[0], rel_bias, ha,
                                sb_out.reshape(b * s, sb_w), *tail_args)

    out = lax.cond(live[0, 0] < SB_DEAD_LOG2, general, lambda _: fast, None)
    return out.reshape(b, s, d)
```

```python
import functools
import math

import jax
import jax.numpy as jnp
import numpy as np
from jax import lax
from jax.experimental import pallas as pl
from jax.experimental.pallas import tpu as pltpu

F32 = jnp.float32
BF16 = jnp.bfloat16

HEAD_DIM = 64
SB_HEADS = 8
SWA_HEADS = 8
SWA_KV_HEADS = 2
WINDOW = 128
REL_BUCKETS = 32
REL_MAX_DIST = 128
LN_EPS = 1e-5
RMS_EPS = 1e-6
DEPTH = 1
ALPHA = (2 * DEPTH) ** 0.25
QK_SCALE = HEAD_DIM ** -0.5
LOG2E = math.log2(math.e)

LANES = 128
BF16_ROWS = 16
MXU_WIDTH = 256
ATT_BLOCK = 128
SB_WINDOW = 256
SB_Q_TILE = 128
SB_STEP_ROWS = 512
ROW_TILE = 1024
PROJ_PARTS = 2
TAIL_ROWS = 512
FFN_CHUNK = 4 * MXU_WIDTH
VMEM_LIMIT = 56 * 1024 * 1024

SB_DEAD_LOG2 = 127.0
MASK_NEG = -1e30


def _layer_norm(x, g, b):
    mu = jnp.mean(x, axis=-1, keepdims=True)
    xc = x - mu
    var = jnp.mean(xc * xc, axis=-1, keepdims=True)
    return xc * lax.rsqrt(var + LN_EPS) * g + b


def _rms_norm(x, g):
    ms = jnp.mean(x * x, axis=-1, keepdims=True)
    return x * lax.rsqrt(ms + RMS_EPS) * g


def _dot(a, b):
    return jnp.dot(a, b, preferred_element_type=F32)


def _dot_nt(a, b):
    return lax.dot_general(a, b, (((1,), (1,)), ((), ())), preferred_element_type=F32)


def _pipeline_schedule(n, stages):
    vals = [dict() for _ in stages]
    thunks = []
    for step in range(n + len(stages) - 1):
        for s, stage in enumerate(stages):
            i = step - s
            if 0 <= i < n:
                def thunk(s=s, i=i, stage=stage):
                    vals[s][i] = stage(i, vals[s - 1].pop(i)) if s else stage(i)
                thunks.append(thunk)
    return thunks, vals[-1]


def _software_pipeline(n, stages):
    thunks, results = _pipeline_schedule(n, stages)
    for thunk in thunks:
        thunk()
    return results


def _interleave(xs, ys):
    merged, taken = [], 0
    for i, x in enumerate(xs):
        merged.append(x)
        upto = (i + 1) * len(ys) // len(xs)
        merged.extend(ys[taken:upto])
        taken = upto
    return merged


def _inproj_kernel(splits, n_later, x_ref, g_ref, b_ref, w32_ref, *refs):
    later_in, refs = refs[:n_later], refs[n_later:]
    ha_ref, qsb_ref, ksb_ref, vsb_ref, qsw_ref, kv_ref = refs[:6]
    later_out, w_ref = refs[6:6 + n_later], refs[6 + n_later]
    rows = x_ref.shape[0] // PROJ_PARTS
    part = lambda i: slice(i * rows, (i + 1) * rows)
    assert splits["k_sw"][1] == splits["v_sw"][0]

    @pl.when(pl.program_id(0) == 0)
    def _convert_weight():
        w_ref[...] = w32_ref[...].astype(BF16)

    for src, dst in zip(later_in, later_out):
        dst[...] = src[...].astype(BF16)

    def normed(i):
        h32 = _layer_norm(x_ref[part(i), :], g_ref[...], b_ref[...])
        ha_ref[part(i), :] = ALPHA * h32
        return h32.astype(BF16)

    def projected(i, h):
        r = part(i)
        proj = lambda lo, hi: _dot(h, w_ref[:, lo:hi])
        qsb_ref[r, :] = (proj(*splits["q_sb"]) * QK_SCALE).astype(BF16)
        ksb_ref[r, :] = proj(*splits["k_sb"]).astype(BF16)
        vsb_ref[r, :] = proj(*splits["v_sb"]).astype(BF16)
        qsw_ref[r, :] = (proj(*splits["q_sw"]) * QK_SCALE).astype(BF16)
        kv = proj(splits["k_sw"][0], splits["v_sw"][1])
        kvw = kv.shape[-1] // 2
        k, v = kv[:, :kvw], kv[:, kvw:]
        kv_ref[r, 0 * kvw:1 * kvw] = k.astype(BF16)
        kv_ref[r, 1 * kvw:2 * kvw] = pltpu.roll(k, HEAD_DIM, axis=1).astype(BF16)
        kv_ref[r, 2 * kvw:3 * kvw] = v.astype(BF16)
        kv_ref[r, 3 * kvw:4 * kvw] = pltpu.roll(v, HEAD_DIM, axis=1).astype(BF16)

    _software_pipeline(PROJ_PARTS, [normed, projected])


def _in_projection(x2, g, b, w_in, splits, later_weights):
    n, d = x2.shape
    steps = n // ROW_TILE
    width = lambda name: splits[name][1] - splits[name][0]
    out_w = [width("q_sb"), width("k_sb"), width("v_sb"), width("q_sw"), 4 * width("k_sw")]
    row = lambda i: (i, 0)
    fixed = lambda i: (0, 0)
    later_specs = [
        pl.BlockSpec((size, w.shape[1]),
                     lambda i, last=w.shape[0] // size - 1: (jnp.minimum(i, last), 0))
        for w, size in zip(later_weights, _conversion_blocks(later_weights, steps))]
    return pl.pallas_call(
        functools.partial(_inproj_kernel, splits, len(later_weights)),
        grid=(steps,),
        in_specs=[
            pl.BlockSpec((ROW_TILE, d), row),
            pl.BlockSpec((1, d), fixed),
            pl.BlockSpec((1, d), fixed),
            pl.BlockSpec(w_in.shape, fixed, pipeline_mode=pl.Buffered(1)),
        ] + later_specs,
        out_specs=[pl.BlockSpec((ROW_TILE, d), row)]
        + [pl.BlockSpec((ROW_TILE, w), row) for w in out_w] + later_specs,
        out_shape=[jax.ShapeDtypeStruct((n, d), F32)]
        + [jax.ShapeDtypeStruct((n, w), BF16) for w in out_w]
        + [jax.ShapeDtypeStruct(w.shape, BF16) for w in later_weights],
        scratch_shapes=[pltpu.VMEM(w_in.shape, BF16)],
        compiler_params=pltpu.CompilerParams(
            dimension_semantics=("arbitrary",), vmem_limit_bytes=VMEM_LIMIT),
        name="ln_in_proj",
    )(x2, g, b, w_in, *later_weights)


def _conversion_blocks(weights, steps):
    sizes = []
    for w in weights:
        rows = w.shape[0]
        sizes.append(next(r for r in range(BF16_ROWS, rows + 1, BF16_ROWS)
                          if rows % r == 0 and rows // r <= steps))
    return sizes


def _split_heads(x, lane):
    zero = jnp.zeros_like(x)
    return jnp.concatenate([jnp.where(lane < HEAD_DIM, x, zero),
                            jnp.where(lane >= HEAD_DIM, x, zero)], axis=0)


def _merge_heads(o2, lane):
    t = o2.shape[0] // 2
    return jnp.where(lane < HEAD_DIM, o2[:t], o2[t:])


def _sb_masks(valid):
    return jnp.where(valid, LOG2E, 0.0), jnp.where(valid, 0.0, MASK_NEG)


def _sb_scores(qq, k, keep):
    z = _dot_nt(qq, k)
    zl = z * LOG2E
    c_nat = jnp.maximum(z, 0.0) + jnp.log(1.0 + jnp.exp2(-jnp.abs(zl)))
    cm = c_nat * keep
    return zl - cm, cm.astype(BF16), cm[:, :LANES]


def _sb_weights(scores, tri, kills, carrieds):
    sums = _dot(jnp.concatenate([sc[1] for sc in scores], axis=0), tri)
    rows = scores[0][0].shape[0]
    out = []
    for n, ((log_beta, _, cm0), kill, carried) in enumerate(zip(scores, kills, carrieds)):
        later = sums[n * rows:(n + 1) * rows]
        log_a = (log_beta - later) + kill
        if carried is not None:
            log_a = log_a - jnp.concatenate([carried] * (log_a.shape[1] // LANES), axis=1)
        out.append((jnp.exp2(log_a).astype(BF16), later[:, :LANES] + cm0))
    return out


def _sb_kernel(n_conv, q_ref, k_ref, v_ref, *refs):
    conv_in, o_ref, conv_out = refs[:n_conv], refs[n_conv], refs[n_conv + 1:2 * n_conv + 1]
    tri_ref, mask_ref, acc_ref, car_ref = refs[2 * n_conv + 1:]
    for src, dst in zip(conv_in, conv_out):
        dst[...] = src[...].astype(BF16)

    bi = pl.program_id(0)
    qi = pl.program_id(1)
    tq, w = SB_Q_TILE, SB_WINDOW
    subs = q_ref.shape[1] // tq
    pairs = q_ref.shape[-1] // LANES
    row0 = qi * (subs * tq)

    @pl.when(jnp.logical_and(bi == 0, qi == 0))
    def _fill_tri():
        j = lax.broadcasted_iota(jnp.int32, (w, w), 0)
        s = lax.broadcasted_iota(jnp.int32, (w, w), 1)
        tri_ref[...] = jnp.where(j > s, 1.0, 0.0).astype(BF16)

    lane = lax.broadcasted_iota(jnp.int32, (tq, LANES), 1)
    lanes_of = lambda p: slice(p * LANES, (p + 1) * LANES)
    rows_of = lambda h: slice(h * tq, (h + 1) * tq)
    first_start = lambda h: pl.multiple_of(jnp.maximum(row0 + (h + 1) * tq - w, 0), tq)

    def key_positions(start):
        return start + lax.broadcasted_iota(jnp.int32, (2 * tq, w), 1)

    assert subs * tq >= w - tq

    @pl.when(qi <= 1)
    def _fill_masks():
        row = lax.broadcasted_iota(jnp.int32, (2 * tq, w), 0) & (tq - 1)
        for h in range(subs):
            mask_ref[h, 0], mask_ref[h, 1] = _sb_masks(
                key_positions(first_start(h)) < row0 + h * tq + row)

    def scores(p, starts, keeps):
        return [_sb_scores(_split_heads(q_ref[0, rows_of(h), lanes_of(p)], lane),
                           k_ref[0, pl.ds(starts[h], w), lanes_of(p)], keeps[h])
                for h in range(subs)]

    first_starts = [first_start(h) for h in range(subs)]

    def first_scores(p):
        return scores(p, first_starts, [mask_ref[h, 0] for h in range(subs)])

    def first_weights(p, sc):
        return _sb_weights(sc, tri_ref[...], [mask_ref[h, 1] for h in range(subs)], [None] * subs)

    def first_values(p, aws):
        least = None
        for h, (a, total) in enumerate(aws):
            o2 = _dot(a, v_ref[0, pl.ds(first_starts[h], w), lanes_of(p)])
            o_ref[0, rows_of(h), lanes_of(p)] = _merge_heads(o2, lane).astype(o_ref.dtype)
            acc_ref[p, h] = o2
            car_ref[p, h] = total
            m = jnp.min(total[:, 0:1])
            least = m if least is None else jnp.minimum(least, m)
        return least

    first = _software_pipeline(pairs, [first_scores, first_weights, first_values])
    live = functools.reduce(jnp.minimum, [first[p] for p in range(pairs)])
    last_start = first_starts[-1]

    @pl.when(jnp.logical_and(last_start > 0, live < SB_DEAD_LOG2))
    def _walk_back():
        for p in range(pairs):
            for h in range(subs):
                car_ref[p, h] = jnp.broadcast_to(car_ref[p, h][:, 0:1], (2 * tq, LANES))

        def body(state):
            step, _ = state
            done = [jnp.maximum(first_start(h) - step * w, 0) for h in range(subs)]
            starts = [pl.multiple_of(jnp.maximum(d - w, 0), tq) for d in done]
            masks = [_sb_masks(key_positions(s) < d) for s, d in zip(starts, done)]

            def more_scores(p):
                return scores(p, starts, [m[0] for m in masks])

            def weights(p, sc):
                return _sb_weights(sc, tri_ref[...], [m[1] for m in masks],
                                   [car_ref[p, h] for h in range(subs)])

            def values(p, aws):
                least = None
                for h, (a, total) in enumerate(aws):
                    acc_ref[p, h] += _dot(a, v_ref[0, pl.ds(starts[h], w), lanes_of(p)])
                    carried = car_ref[p, h] + total[:, 0:1]
                    car_ref[p, h] = carried
                    m = jnp.min(carried)
                    least = m if least is None else jnp.minimum(least, m)
                return least

            mins = _software_pipeline(pairs, [more_scores, weights, values])
            return step + 1, functools.reduce(jnp.minimum, [mins[p] for p in range(pairs)])

        def cond(state):
            step, nxt = state
            return jnp.logical_and(last_start - step * w > 0, nxt < SB_DEAD_LOG2)

        lax.while_loop(cond, body, (jnp.int32(0), live))
        for p in range(pairs):
            for h in range(subs):
                o_ref[0, rows_of(h), lanes_of(p)] = _merge_heads(
                    acc_ref[p, h], lane).astype(o_ref.dtype)


def _sb_attention(q, k, v, weights):
    b, s, wd = q.shape
    pairs = wd // LANES
    subs = SB_STEP_ROWS // SB_Q_TILE
    nq = s // SB_STEP_ROWS
    tile = lambda bi, i: (bi, i, 0)
    whole = lambda bi, i: (bi, 0, 0)
    once = pl.Buffered(1)
    conv_specs = [
        pl.BlockSpec((size, w.shape[1]),
                     lambda bi, i, last=w.shape[0] // size - 1: (jnp.minimum(bi * nq + i, last), 0))
        for w, size in zip(weights, _conversion_blocks(weights, b * nq))]
    return pl.pallas_call(
        functools.partial(_sb_kernel, len(weights)),
        grid=(b, nq),
        in_specs=[
            pl.BlockSpec((1, SB_STEP_ROWS, wd), tile),
            pl.BlockSpec((1, s, wd), whole, pipeline_mode=once),
            pl.BlockSpec((1, s, wd), whole, pipeline_mode=once),
        ] + conv_specs,
        out_specs=[pl.BlockSpec((1, SB_STEP_ROWS, wd), tile)] + conv_specs,
        out_shape=[jax.ShapeDtypeStruct((b, s, wd), BF16)]
        + [jax.ShapeDtypeStruct(w.shape, BF16) for w in weights],
        scratch_shapes=[
            pltpu.VMEM((SB_WINDOW, SB_WINDOW), BF16),
            pltpu.VMEM((subs, 2, 2 * SB_Q_TILE, SB_WINDOW), F32),
            pltpu.VMEM((pairs, subs, 2 * SB_Q_TILE, LANES), F32),
            pltpu.VMEM((pairs, subs, 2 * SB_Q_TILE, LANES), F32),
        ],
        compiler_params=pltpu.CompilerParams(
            dimension_semantics=("arbitrary", "arbitrary"), vmem_limit_bytes=VMEM_LIMIT),
        name="stick_breaking_attention",
    )(q, k, v, *weights)


def _t5_log_bucket_starts():
    exact = REL_BUCKETS // 2
    d = np.arange(exact, 2 * REL_MAX_DIST, dtype=np.float32)
    large = exact + (np.log(d / np.float32(exact)) / np.float32(math.log(REL_MAX_DIST / exact))
                     * np.float32(REL_BUCKETS - exact)).astype(np.int32)
    large = np.minimum(large, REL_BUCKETS - 1)
    assert large[0] == exact and np.all(np.diff(large) >= 0)
    return [int(d[np.argmax(large >= bk)]) for bk in range(exact + 1, REL_BUCKETS)]


def _t5_causal_bucket(distance):
    d = jnp.maximum(distance, 0)
    bucket = jnp.minimum(d, REL_BUCKETS // 2)
    for start in _t5_log_bucket_starts():
        bucket = bucket + jnp.where(d >= start, 1, 0)
    return bucket


def _swa_fill_bias(relb_ref, bias_ref):
    t = ATT_BLOCK
    r = lax.broadcasted_iota(jnp.int32, (t, 2 * t), 0)
    c = lax.broadcasted_iota(jnp.int32, (t, 2 * t), 1)
    dist = r + t - c
    bucket = _t5_causal_bucket(dist)
    valid = jnp.logical_and(dist >= 0, dist < WINDOW)
    for h in range(SWA_HEADS):
        bias = jnp.zeros((t, 2 * t), F32)
        for bk in range(REL_BUCKETS):
            bias = jnp.where(bucket == bk, relb_ref[bk, h], bias)
        bias_ref[h] = jnp.where(valid, bias, MASK_NEG)


def _swa_schedule(sinks_ref, q_ref, kvc_ref, kvp_ref, bias_ref, o_ref, no_previous):
    t = ATT_BLOCK
    kvw = kvc_ref.shape[-1] // 4
    group = SWA_HEADS // SWA_KV_HEADS
    tiles = q_ref.shape[0] // t
    rows_of = lambda n: slice(n * t, (n + 1) * t)
    lane = lax.broadcasted_iota(jnp.int32, (t, LANES), 1)
    row2 = lax.broadcasted_iota(jnp.int32, (2 * t, 1), 0)
    col2 = lax.broadcasted_iota(jnp.int32, (2 * t, 2 * t), 1)
    prev_pen = jnp.where(jnp.logical_and(col2 < t, no_previous), MASK_NEG, 0.0)

    def kv_window(n, variant):
        cols = slice(variant * kvw, (variant + 1) * kvw)
        before = kvc_ref[rows_of(n - 1), cols] if n else kvp_ref[:, cols]
        return jnp.concatenate([before, kvc_ref[rows_of(n), cols]], axis=0)

    items = [(n, half, j) for n in range(tiles) for half in range(2) for j in range(SWA_KV_HEADS)]
    heads_of = lambda half, j: [h for h in range(j * group, (j + 1) * group) if h % 2 == half]

    def probabilities(i):
        n, half, j = items[i]
        heads = heads_of(half, j)
        in_half = (lane >= HEAD_DIM) if half else (lane < HEAD_DIM)
        qq = jnp.concatenate(
            [jnp.where(in_half, q_ref[rows_of(n), (h // 2) * LANES:(h // 2 + 1) * LANES], 0)
             for h in heads], axis=0)
        s = _dot_nt(qq, kv_window(n, 0 + int(j != half)))
        s = s + jnp.concatenate([bias_ref[h] for h in heads], axis=0)
        if n == 0:
            s = s + prev_pen
        sink = jnp.where(row2 < t, sinks_ref[heads[0]], sinks_ref[heads[1]])
        m = jnp.maximum(jnp.max(s, axis=-1, keepdims=True), sink)
        p = jnp.exp(s - m)
        denom = jnp.sum(p, axis=-1, keepdims=True) + jnp.exp(sink - m)
        return p.astype(BF16), 1.0 / denom

    def values(i, pd):
        n, half, j = items[i]
        p, inv_denom = pd
        return _dot(p, kv_window(n, 2 + int(j != half))) * inv_denom

    thunks, outs = _pipeline_schedule(len(items), [probabilities, values])

    def store():
        for i, (n, half, j) in enumerate(items):
            if half == 0:
                even, odd = outs[i], outs[items.index((n, 1, j))]
                for e, h in enumerate(heads_of(0, j)):
                    o_ref[rows_of(n), (h // 2) * LANES:(h // 2 + 1) * LANES] = jnp.where(
                        lane < HEAD_DIM, even[e * t:(e + 1) * t], odd[e * t:(e + 1) * t]
                    ).astype(o_ref.dtype)

    return thunks + [store]


def _tail_schedule(ha_ref, sb_ref, sw_ref, sbg_ref, swg_ref, wo_ref, l1g_ref, l1b_ref,
                   wgu_ref, wd_ref, l2g_ref, l2b_ref, o_ref):
    d = ha_ref.shape[1]
    d_ff = wd_ref.shape[0]
    mw = MXU_WIDTH
    bounds = list(range(0, d_ff, FFN_CHUNK)) + [d_ff]
    n_chunks = len(bounds) - 1
    out_tiles = range(d // mw)
    st = {}

    def merged():
        sb = _rms_norm(sb_ref[...].astype(F32), sbg_ref[...]).astype(BF16)
        sw = _rms_norm(sw_ref[...].astype(F32), swg_ref[...]).astype(BF16)
        st["merged"] = jnp.concatenate([sb, sw], axis=1)

    def out_projection(n):
        st["mix", n] = _dot(st["merged"], wo_ref[:, n * mw:(n + 1) * mw])

    def first_norm():
        mix = jnp.concatenate([st.pop(("mix", n)) for n in out_tiles], axis=1)
        st.pop("merged")
        h1 = _layer_norm(ha_ref[...] + mix, l1g_ref[...], l1b_ref[...])
        st["h1b"] = h1.astype(BF16)
        for n in out_tiles:
            st["acc", n] = ALPHA * h1[:, n * mw:(n + 1) * mw]

    def activation(c, n):
        lo = bounds[c] + n * mw
        gate = _dot(st["h1b"], wgu_ref[:, lo:lo + mw])
        up = _dot(st["h1b"], wgu_ref[:, d_ff + lo:d_ff + lo + mw])
        st["act", c, n] = (gate / (1.0 + jnp.exp(-gate)) * up).astype(BF16)

    def down(c, n):
        tiles = range((bounds[c + 1] - bounds[c]) // mw)
        if n == 0:
            st["act", c] = jnp.concatenate([st.pop(("act", c, k)) for k in tiles], axis=1)
        st["acc", n] = st["acc", n] + _dot(st["act", c], wd_ref[bounds[c]:bounds[c + 1],
                                                               n * mw:(n + 1) * mw])

    def second_norm():
        y = jnp.concatenate([st.pop(("acc", n)) for n in out_tiles], axis=1)
        o_ref[...] = _layer_norm(y, l2g_ref[...], l2b_ref[...])

    act_thunks = [[functools.partial(activation, c, n)
                   for n in range((bounds[c + 1] - bounds[c]) // mw)] for c in range(n_chunks)]
    down_thunks = [[functools.partial(down, c, n) for n in out_tiles] for c in range(n_chunks)]
    ffn = list(act_thunks[0])
    for c in range(1, n_chunks):
        ffn += _interleave(act_thunks[c], down_thunks[c - 1])
    ffn += down_thunks[-1]
    return (merged, [functools.partial(out_projection, n) for n in out_tiles], first_norm,
            ffn, second_norm)


def _attn_tail_kernel(tiles_per_seq, last_tile, sinks_ref, relb_ref, q_ref, kvc_ref, kvp_ref,
                      ha_ref, sb_ref, sbg_ref, swg_ref, wo_ref, l1g_ref, l1b_ref,
                      wgu_ref, wd_ref, l2g_ref, l2b_ref, o_ref, bias_ref, sw_ref):
    t = pl.program_id(0)

    @pl.when(t == 0)
    def _first_step():
        _swa_fill_bias(relb_ref, bias_ref)
        sw_ref[1] = jnp.zeros(sw_ref.shape[1:], sw_ref.dtype)

    no_previous = (jnp.minimum(t, last_tile) % tiles_per_seq) == 0
    attention = _swa_schedule(sinks_ref, q_ref.at[0], kvc_ref.at[0], kvp_ref.at[0], bias_ref,
                              sw_ref.at[t % 2], no_previous)
    merged, out_proj, first_norm, ffn, second_norm = _tail_schedule(
        ha_ref, sb_ref, sw_ref.at[(t + 1) % 2], sbg_ref, swg_ref, wo_ref,
        l1g_ref, l1b_ref, wgu_ref, wd_ref, l2g_ref, l2b_ref, o_ref)
    tail = [merged] + out_proj + [first_norm] + ffn + [second_norm]
    for thunk in _interleave(tail, attention):
        thunk()


def _attn_tail_block(q_sw, kv_sw, sinks, rel_bias, ha, sb2, sb_g, sw_g, w_out_b, ln1_g, ln1_b,
                     w_gate_up_b, w_down_b, ln2_g, ln2_b):
    b, s, wd = q_sw.shape
    n, d = ha.shape
    tiles_per_seq = s // TAIL_ROWS
    last_tile = n // TAIL_ROWS - 1
    key_tiles = TAIL_ROWS // ATT_BLOCK
    att = lambda t: jnp.minimum(t, last_tile)
    cur = lambda t: (att(t) // tiles_per_seq, att(t) % tiles_per_seq, 0)
    prev = lambda t: (att(t) // tiles_per_seq,
                      jnp.maximum(att(t) % tiles_per_seq * key_tiles - 1, 0), 0)
    row = lambda t: (jnp.maximum(t - 1, 0), 0)
    fixed = lambda t: (0, 0)
    smem = pl.BlockSpec(memory_space=pltpu.SMEM)
    vec = lambda a: pl.BlockSpec((1, a.shape[-1]), fixed)
    weight = lambda a: pl.BlockSpec(a.shape, fixed, pipeline_mode=pl.Buffered(1))
    return pl.pallas_call(
        functools.partial(_attn_tail_kernel, tiles_per_seq, last_tile),
        grid=(last_tile + 2,),
        in_specs=[
            smem, smem,
            pl.BlockSpec((1, TAIL_ROWS, wd), cur),
            pl.BlockSpec((1, TAIL_ROWS, kv_sw.shape[-1]), cur),
            pl.BlockSpec((1, ATT_BLOCK, kv_sw.shape[-1]), prev),
            pl.BlockSpec((TAIL_ROWS, d), row),
            pl.BlockSpec((TAIL_ROWS, sb2.shape[-1]), row),
            vec(sb_g), vec(sw_g), weight(w_out_b), vec(ln1_g), vec(ln1_b),
            weight(w_gate_up_b), weight(w_down_b), vec(ln2_g), vec(ln2_b),
        ],
        out_specs=pl.BlockSpec((TAIL_ROWS, d), row),
        out_shape=jax.ShapeDtypeStruct((n, d), F32),
        scratch_shapes=[
            pltpu.VMEM((SWA_HEADS, ATT_BLOCK, 2 * ATT_BLOCK), F32),
            pltpu.VMEM((2, TAIL_ROWS, wd), BF16),
        ],
        compiler_params=pltpu.CompilerParams(
            dimension_semantics=("arbitrary",), vmem_limit_bytes=VMEM_LIMIT),
        name="swa_attention_and_dense_tail",
    )(sinks, rel_bias, q_sw, kv_sw, kv_sw, ha, sb2, sb_g, sw_g, w_out_b, ln1_g, ln1_b,
      w_gate_up_b, w_down_b, ln2_g, ln2_b)


SB_GROUP = 4
SB_LIVE_DONE = 1e30


def _sb_fast_masks(mask_ref):
    tq, w = SB_Q_TILE, SB_WINDOW
    row = lax.broadcasted_iota(jnp.int32, (2 * tq, w), 0) & (tq - 1)
    col = lax.broadcasted_iota(jnp.int32, (2 * tq, w), 1)
    causal = col < row + (w - tq)
    mask_ref[0, 0], mask_ref[0, 1] = _sb_masks(causal)
    for h in range(mask_ref.shape[0] - 1):
        first_key = w - (h + 1) * tq
        mask_ref[1 + h, 0], mask_ref[1 + h, 1] = _sb_masks(
            jnp.logical_and(causal, col >= first_key))


def _sb_fast_schedule(q_ref, kc_ref, kp_ref, vc_ref, vp_ref, tri_ref, mask_ref, o_ref,
                      no_previous, lives):
    tq, w = SB_Q_TILE, SB_WINDOW
    subs = q_ref.shape[0] // tq
    pairs = q_ref.shape[-1] // LANES
    clamped = w // tq - 1
    assert mask_ref.shape[0] == clamped + 1 and subs % SB_GROUP == 0
    lane = lax.broadcasted_iota(jnp.int32, (tq, LANES), 1)
    lanes_of = lambda p: slice(p * LANES, (p + 1) * LANES)
    rows_of = lambda h: slice(h * tq, (h + 1) * tq)

    def window(h, cur_ref, prev_ref, p):
        start = (h + 1) * tq - w
        if start >= 0:
            return cur_ref[start:start + w, lanes_of(p)]
        return jnp.concatenate([prev_ref[w + start:w, lanes_of(p)],
                                cur_ref[0:w + start, lanes_of(p)]], axis=0)

    def mask(h, which):
        if h >= clamped:
            return mask_ref[0, which]
        return mask_ref[jnp.where(no_previous, 1 + h, 0), which]

    items = [(p, g) for p in range(pairs) for g in range(subs // SB_GROUP)]
    group = lambda g: range(g * SB_GROUP, (g + 1) * SB_GROUP)

    def scores(i):
        p, g = items[i]
        return [_sb_scores(_split_heads(q_ref[rows_of(h), lanes_of(p)], lane),
                           window(h, kc_ref, kp_ref, p), mask(h, 0)) for h in group(g)]

    def weights(i, sc):
        return _sb_weights(sc, tri_ref[...], [mask(h, 1) for h in group(items[i][1])],
                           [None] * SB_GROUP)

    def values(i, aws):
        p, g = items[i]
        for h, (a, total) in zip(group(g), aws):
            o2 = _dot(a, window(h, vc_ref, vp_ref, p))
            o_ref[rows_of(h), lanes_of(p)] = _merge_heads(o2, lane).astype(o_ref.dtype)
            m = jnp.min(total[:, 0:1])
            lives.append(m if h > clamped else jnp.where(no_previous, SB_LIVE_DONE, m))

    return _pipeline_schedule(len(items), [scores, weights, values])[0]


def _fast_kernel(tiles_per_seq, last_tile, sinks_ref, relb_ref, qsw_ref, kvc_ref, kvp_ref,
                 qsb_ref, kc_ref, kp_ref, vc_ref, vp_ref,
                 ha_ref, sbg_ref, swg_ref, wo_ref, l1g_ref, l1b_ref,
                 wgu_ref, wd_ref, l2g_ref, l2b_ref, o_ref, live_ref,
                 bias_ref, sw_ref, sb_ref, tri_ref, mask_ref):
    t = pl.program_id(0)

    @pl.when(t == 0)
    def _first_step():
        _swa_fill_bias(relb_ref, bias_ref)
        _sb_fast_masks(mask_ref)
        j = lax.broadcasted_iota(jnp.int32, tri_ref.shape, 0)
        s = lax.broadcasted_iota(jnp.int32, tri_ref.shape, 1)
        tri_ref[...] = jnp.where(j > s, 1.0, 0.0).astype(BF16)
        sw_ref[1] = jnp.zeros(sw_ref.shape[1:], sw_ref.dtype)
        sb_ref[1] = jnp.zeros(sb_ref.shape[1:], sb_ref.dtype)
        live_ref[...] = jnp.full(live_ref.shape, SB_LIVE_DONE, F32)

    no_previous = (jnp.minimum(t, last_tile) % tiles_per_seq) == 0
    lives = []
    swa = _swa_schedule(sinks_ref, qsw_ref.at[0], kvc_ref.at[0], kvp_ref.at[0], bias_ref,
                        sw_ref.at[t % 2], no_previous)
    sb = _sb_fast_schedule(qsb_ref.at[0], kc_ref.at[0], kp_ref.at[0], vc_ref.at[0], vp_ref.at[0],
                           tri_ref, mask_ref, sb_ref.at[t % 2], no_previous, lives)
    merged, out_proj, first_norm, ffn, second_norm = _tail_schedule(
        ha_ref, sb_ref.at[(t + 1) % 2], sw_ref.at[(t + 1) % 2], sbg_ref, swg_ref, wo_ref,
        l1g_ref, l1b_ref, wgu_ref, wd_ref, l2g_ref, l2b_ref, o_ref)
    tail = [merged] + out_proj + [first_norm] + ffn + [second_norm]
    for thunk in _interleave(tail, _interleave(swa, sb)):
        thunk()
    live_ref[...] = jnp.minimum(live_ref[...], functools.reduce(jnp.minimum, lives))


def _fast_block(q_sw, kv_sw, q_sb, k_sb, v_sb, sinks, rel_bias, ha, sb_g, sw_g, w_out_b,
                ln1_g, ln1_b, w_gate_up_b, w_down_b, ln2_g, ln2_b):
    b, s, wd = q_sw.shape
    n, d = ha.shape
    tiles_per_seq = s // TAIL_ROWS
    last_tile = n // TAIL_ROWS - 1
    att = lambda t: jnp.minimum(t, last_tile)
    cur = lambda t: (att(t) // tiles_per_seq, att(t) % tiles_per_seq, 0)

    def prev(rows):
        per_tile = TAIL_ROWS // rows
        return lambda t: (att(t) // tiles_per_seq,
                          jnp.maximum(att(t) % tiles_per_seq * per_tile - 1, 0), 0)

    row = lambda t: (jnp.maximum(t - 1, 0), 0)
    fixed = lambda t: (0, 0)
    smem = pl.BlockSpec(memory_space=pltpu.SMEM)
    vec = lambda a: pl.BlockSpec((1, a.shape[-1]), fixed)
    weight = lambda a: pl.BlockSpec(a.shape, fixed, pipeline_mode=pl.Buffered(1))
    tile3 = lambda a: pl.BlockSpec((1, TAIL_ROWS, a.shape[-1]), cur)
    before = lambda a, rows: pl.BlockSpec((1, rows, a.shape[-1]), prev(rows))
    steps = last_tile + 2
    return pl.pallas_call(
        functools.partial(_fast_kernel, tiles_per_seq, last_tile),
        grid=(steps,),
        in_specs=[
            smem, smem,
            tile3(q_sw), tile3(kv_sw), before(kv_sw, ATT_BLOCK),
            tile3(q_sb), tile3(k_sb), before(k_sb, SB_WINDOW), tile3(v_sb), before(v_sb, SB_WINDOW),
            pl.BlockSpec((TAIL_ROWS, d), row),
            vec(sb_g), vec(sw_g), weight(w_out_b), vec(ln1_g), vec(ln1_b),
            weight(w_gate_up_b), weight(w_down_b), vec(ln2_g), vec(ln2_b),
        ],
        out_specs=[pl.BlockSpec((TAIL_ROWS, d), row),
                   pl.BlockSpec((8, LANES), fixed)],
        out_shape=[jax.ShapeDtypeStruct((n, d), F32),
                   jax.ShapeDtypeStruct((8, LANES), F32)],
        scratch_shapes=[
            pltpu.VMEM((SWA_HEADS, ATT_BLOCK, 2 * ATT_BLOCK), F32),
            pltpu.VMEM((2, TAIL_ROWS, wd), BF16),
            pltpu.VMEM((2, TAIL_ROWS, q_sb.shape[-1]), BF16),
            pltpu.VMEM((SB_WINDOW, SB_WINDOW), BF16),
            pltpu.VMEM((SB_WINDOW // SB_Q_TILE, 2, 2 * SB_Q_TILE, SB_WINDOW), F32),
        ],
        compiler_params=pltpu.CompilerParams(
            dimension_semantics=("arbitrary",), vmem_limit_bytes=VMEM_LIMIT),
        name="attention_and_dense_tail",
    )(sinks, rel_bias, q_sw, kv_sw, kv_sw, q_sb, k_sb, k_sb, v_sb, v_sb, ha, sb_g, sw_g, w_out_b,
      ln1_g, ln1_b, w_gate_up_b, w_down_b, ln2_g, ln2_b)


def kernel(x, ln_in_g, ln_in_b, w_in, sb_norm_g, swa_norm_g, sinks, rel_bias, w_out,
           ln1_g, ln1_b, w_gate_up, w_down, ln2_g, ln2_b):
    b, s, d = x.shape
    assert w_in.shape[0] == DEPTH == 1
    sb_w = SB_HEADS * HEAD_DIM
    sw_w = SWA_HEADS * HEAD_DIM
    kv_w = SWA_KV_HEADS * HEAD_DIM
    assert w_in.shape[-1] == 3 * sb_w + sw_w + 2 * kv_w
    assert kv_w == LANES and SWA_HEADS // SWA_KV_HEADS == 4
    assert s % SB_WINDOW == 0 and s % SB_STEP_ROWS == 0 and s % TAIL_ROWS == 0
    assert (b * s) % ROW_TILE == 0 and WINDOW <= ATT_BLOCK and d % MXU_WIDTH == 0
    assert FFN_CHUNK % MXU_WIDTH == 0 and w_down.shape[1] % MXU_WIDTH == 0

    x2 = x.reshape(b * s, d)
    row_vec = lambda a: a.reshape(1, -1)

    names = ("q_sb", "k_sb", "v_sb", "q_sw", "k_sw", "v_sw")
    widths = (sb_w, sb_w, sb_w, sw_w, kv_w, kv_w)
    splits, lo = {}, 0
    for name, wd in zip(names, widths):
        splits[name] = (lo, lo + wd)
        lo += wd
    ha, q_sb, k_sb, v_sb, q_sw, kv_sw, w_out_b, w_gate_up_b, w_down_b = _in_projection(
        x2, row_vec(ln_in_g), row_vec(ln_in_b), w_in[0], splits,
        [w_out[0], w_gate_up[0], w_down[0]])

    to3 = lambda a: a.reshape(b, s, a.shape[-1])
    q_sb, k_sb, v_sb, q_sw, kv_sw = map(to3, (q_sb, k_sb, v_sb, q_sw, kv_sw))
    tail_args = (row_vec(sb_norm_g[0]), row_vec(swa_norm_g[0]), w_out_b,
                 row_vec(ln1_g[0]), row_vec(ln1_b[0]), w_gate_up_b, w_down_b,
                 row_vec(ln2_g[0]), row_vec(ln2_b[0]))

    fast, live = _fast_block(q_sw, kv_sw, q_sb, k_sb, v_sb, sinks[0], rel_bias, ha, *tail_args)

    def general(_):
        (sb_out,) = _sb_attention(q_sb, k_sb, v_sb, [])
        return _attn_tail_block(q_sw, kv_sw, sinks[0], rel_bias, ha,
                                sb_out.reshape(b * s, sb_w), *tail_args)

    out = lax.cond(live[0, 0] < SB_DEAD_LOG2, general, lambda _: fast, None)
    return out.reshape(b, s, d)
```

```python
import functools
import math

import jax
import jax.numpy as jnp
import numpy as np
from jax import lax
from jax.experimental import pallas as pl
from jax.experimental.pallas import tpu as pltpu

F32 = jnp.float32
BF16 = jnp.bfloat16

HEAD_DIM = 64
SB_HEADS = 8
SWA_HEADS = 8
SWA_KV_HEADS = 2
WINDOW = 128
REL_BUCKETS = 32
REL_MAX_DIST = 128
LN_EPS = 1e-5
RMS_EPS = 1e-6
DEPTH = 1
ALPHA = (2 * DEPTH) ** 0.25
QK_SCALE = HEAD_DIM ** -0.5
LOG2E = math.log2(math.e)

LANES = 128
BF16_ROWS = 16
MXU_WIDTH = 256
ATT_BLOCK = 128
SB_WINDOW = 256
SB_Q_TILE = 64
SB_STEP_ROWS = 512
ROW_TILE = 1024
PROJ_PARTS = 2
TAIL_ROWS = 512
FFN_CHUNK = 4 * MXU_WIDTH
VMEM_LIMIT = 56 * 1024 * 1024

SB_DEAD_LOG2 = 127.0
MASK_NEG = -1e30


def _layer_norm(x, g, b):
    mu = jnp.mean(x, axis=-1, keepdims=True)
    xc = x - mu
    var = jnp.mean(xc * xc, axis=-1, keepdims=True)
    return xc * lax.rsqrt(var + LN_EPS) * g + b


def _rms_norm(x, g):
    ms = jnp.mean(x * x, axis=-1, keepdims=True)
    return x * lax.rsqrt(ms + RMS_EPS) * g


def _dot(a, b):
    return jnp.dot(a, b, preferred_element_type=F32)


def _dot_nt(a, b):
    return lax.dot_general(a, b, (((1,), (1,)), ((), ())), preferred_element_type=F32)


def _pipeline_schedule(n, stages):
    vals = [dict() for _ in stages]
    thunks = []
    for step in range(n + len(stages) - 1):
        for s, stage in enumerate(stages):
            i = step - s
            if 0 <= i < n:
                def thunk(s=s, i=i, stage=stage):
                    vals[s][i] = stage(i, vals[s - 1].pop(i)) if s else stage(i)
                thunks.append(thunk)
    return thunks, vals[-1]


def _software_pipeline(n, stages):
    thunks, results = _pipeline_schedule(n, stages)
    for thunk in thunks:
        thunk()
    return results


def _interleave(xs, ys):
    merged, taken = [], 0
    for i, x in enumerate(xs):
        merged.append(x)
        upto = (i + 1) * len(ys) // len(xs)
        merged.extend(ys[taken:upto])
        taken = upto
    return merged


def _inproj_kernel(splits, n_later, x_ref, g_ref, b_ref, w32_ref, *refs):
    later_in, refs = refs[:n_later], refs[n_later:]
    ha_ref, qsb_ref, ksb_ref, vsb_ref, qsw_ref, kv_ref = refs[:6]
    later_out, w_ref = refs[6:6 + n_later], refs[6 + n_later]
    rows = x_ref.shape[0] // PROJ_PARTS
    part = lambda i: slice(i * rows, (i + 1) * rows)
    assert splits["k_sw"][1] == splits["v_sw"][0]

    @pl.when(pl.program_id(0) == 0)
    def _convert_weight():
        w_ref[...] = w32_ref[...].astype(BF16)

    for src, dst in zip(later_in, later_out):
        dst[...] = src[...].astype(BF16)

    def normed(i):
        h32 = _layer_norm(x_ref[part(i), :], g_ref[...], b_ref[...])
        ha_ref[part(i), :] = ALPHA * h32
        return h32.astype(BF16)

    def projected(i, h):
        r = part(i)
        proj = lambda lo, hi: _dot(h, w_ref[:, lo:hi])
        qsb_ref[r, :] = (proj(*splits["q_sb"]) * QK_SCALE).astype(BF16)
        ksb_ref[r, :] = proj(*splits["k_sb"]).astype(BF16)
        vsb_ref[r, :] = proj(*splits["v_sb"]).astype(BF16)
        qsw_ref[r, :] = (proj(*splits["q_sw"]) * QK_SCALE).astype(BF16)
        kv = proj(splits["k_sw"][0], splits["v_sw"][1])
        kvw = kv.shape[-1] // 2
        k, v = kv[:, :kvw], kv[:, kvw:]
        kv_ref[r, 0 * kvw:1 * kvw] = k.astype(BF16)
        kv_ref[r, 1 * kvw:2 * kvw] = pltpu.roll(k, HEAD_DIM, axis=1).astype(BF16)
        kv_ref[r, 2 * kvw:3 * kvw] = v.astype(BF16)
        kv_ref[r, 3 * kvw:4 * kvw] = pltpu.roll(v, HEAD_DIM, axis=1).astype(BF16)

    _software_pipeline(PROJ_PARTS, [normed, projected])


def _in_projection(x2, g, b, w_in, splits, later_weights):
    n, d = x2.shape
    steps = n // ROW_TILE
    width = lambda name: splits[name][1] - splits[name][0]
    out_w = [width("q_sb"), width("k_sb"), width("v_sb"), width("q_sw"), 4 * width("k_sw")]
    row = lambda i: (i, 0)
    fixed = lambda i: (0, 0)
    later_specs = [
        pl.BlockSpec((size, w.shape[1]),
                     lambda i, last=w.shape[0] // size - 1: (jnp.minimum(i, last), 0))
        for w, size in zip(later_weights, _conversion_blocks(later_weights, steps))]
    return pl.pallas_call(
        functools.partial(_inproj_kernel, splits, len(later_weights)),
        grid=(steps,),
        in_specs=[
            pl.BlockSpec((ROW_TILE, d), row),
            pl.BlockSpec((1, d), fixed),
            pl.BlockSpec((1, d), fixed),
            pl.BlockSpec(w_in.shape, fixed, pipeline_mode=pl.Buffered(1)),
        ] + later_specs,
        out_specs=[pl.BlockSpec((ROW_TILE, d), row)]
        + [pl.BlockSpec((ROW_TILE, w), row) for w in out_w] + later_specs,
        out_shape=[jax.ShapeDtypeStruct((n, d), F32)]
        + [jax.ShapeDtypeStruct((n, w), BF16) for w in out_w]
        + [jax.ShapeDtypeStruct(w.shape, BF16) for w in later_weights],
        scratch_shapes=[pltpu.VMEM(w_in.shape, BF16)],
        compiler_params=pltpu.CompilerParams(
            dimension_semantics=("arbitrary",), vmem_limit_bytes=VMEM_LIMIT),
        name="ln_in_proj",
    )(x2, g, b, w_in, *later_weights)


def _conversion_blocks(weights, steps):
    sizes = []
    for w in weights:
        rows = w.shape[0]
        sizes.append(next(r for r in range(BF16_ROWS, rows + 1, BF16_ROWS)
                          if rows % r == 0 and rows // r <= steps))
    return sizes


def _split_heads(x, lane):
    zero = jnp.zeros_like(x)
    return jnp.concatenate([jnp.where(lane < HEAD_DIM, x, zero),
                            jnp.where(lane >= HEAD_DIM, x, zero)], axis=0)


def _merge_heads(o2, lane):
    t = o2.shape[0] // 2
    return jnp.where(lane < HEAD_DIM, o2[:t], o2[t:])


def _sb_masks(valid):
    return jnp.where(valid, LOG2E, 0.0), jnp.where(valid, 0.0, MASK_NEG)


def _sb_scores(qq, k, keep):
    z = _dot_nt(qq, k)
    zl = z * LOG2E
    c_nat = jnp.maximum(z, 0.0) + jnp.log(1.0 + jnp.exp2(-jnp.abs(zl)))
    cm = c_nat * keep
    return zl - cm, cm.astype(BF16), cm[:, :LANES]


def _sb_weights(scores, tri, kills, carrieds):
    sums = _dot(jnp.concatenate([sc[1] for sc in scores], axis=0), tri)
    rows = scores[0][0].shape[0]
    out = []
    for n, ((log_beta, _, cm0), kill, carried) in enumerate(zip(scores, kills, carrieds)):
        later = sums[n * rows:(n + 1) * rows]
        log_a = (log_beta - later) + kill
        if carried is not None:
            log_a = log_a - jnp.concatenate([carried] * (log_a.shape[1] // LANES), axis=1)
        out.append((jnp.exp2(log_a).astype(BF16), later[:, :LANES] + cm0))
    return out


def _sb_kernel(n_conv, q_ref, k_ref, v_ref, *refs):
    conv_in, o_ref, conv_out = refs[:n_conv], refs[n_conv], refs[n_conv + 1:2 * n_conv + 1]
    tri_ref, mask_ref, acc_ref, car_ref = refs[2 * n_conv + 1:]
    for src, dst in zip(conv_in, conv_out):
        dst[...] = src[...].astype(BF16)

    bi = pl.program_id(0)
    qi = pl.program_id(1)
    tq, w = SB_Q_TILE, SB_WINDOW
    subs = q_ref.shape[1] // tq
    pairs = q_ref.shape[-1] // LANES
    row0 = qi * (subs * tq)

    @pl.when(jnp.logical_and(bi == 0, qi == 0))
    def _fill_tri():
        j = lax.broadcasted_iota(jnp.int32, (w, w), 0)
        s = lax.broadcasted_iota(jnp.int32, (w, w), 1)
        tri_ref[...] = jnp.where(j > s, 1.0, 0.0).astype(BF16)

    lane = lax.broadcasted_iota(jnp.int32, (tq, LANES), 1)
    lanes_of = lambda p: slice(p * LANES, (p + 1) * LANES)
    rows_of = lambda h: slice(h * tq, (h + 1) * tq)
    first_start = lambda h: pl.multiple_of(jnp.maximum(row0 + (h + 1) * tq - w, 0), tq)

    def key_positions(start):
        return start + lax.broadcasted_iota(jnp.int32, (2 * tq, w), 1)

    assert subs * tq >= w - tq

    @pl.when(qi <= 1)
    def _fill_masks():
        row = lax.broadcasted_iota(jnp.int32, (2 * tq, w), 0) & (tq - 1)
        for h in range(subs):
            mask_ref[h, 0], mask_ref[h, 1] = _sb_masks(
                key_positions(first_start(h)) < row0 + h * tq + row)

    def scores(p, starts, keeps):
        return [_sb_scores(_split_heads(q_ref[0, rows_of(h), lanes_of(p)], lane),
                           k_ref[0, pl.ds(starts[h], w), lanes_of(p)], keeps[h])
                for h in range(subs)]

    first_starts = [first_start(h) for h in range(subs)]

    def first_scores(p):
        return scores(p, first_starts, [mask_ref[h, 0] for h in range(subs)])

    def first_weights(p, sc):
        return _sb_weights(sc, tri_ref[...], [mask_ref[h, 1] for h in range(subs)], [None] * subs)

    def first_values(p, aws):
        least = None
        for h, (a, total) in enumerate(aws):
            o2 = _dot(a, v_ref[0, pl.ds(first_starts[h], w), lanes_of(p)])
            o_ref[0, rows_of(h), lanes_of(p)] = _merge_heads(o2, lane).astype(o_ref.dtype)
            acc_ref[p, h] = o2
            car_ref[p, h] = total
            m = jnp.min(total[:, 0:1])
            least = m if least is None else jnp.minimum(least, m)
        return least

    first = _software_pipeline(pairs, [first_scores, first_weights, first_values])
    live = functools.reduce(jnp.minimum, [first[p] for p in range(pairs)])
    last_start = first_starts[-1]

    @pl.when(jnp.logical_and(last_start > 0, live < SB_DEAD_LOG2))
    def _walk_back():
        for p in range(pairs):
            for h in range(subs):
                car_ref[p, h] = jnp.broadcast_to(car_ref[p, h][:, 0:1], (2 * tq, LANES))

        def body(state):
            step, _ = state
            done = [jnp.maximum(first_start(h) - step * w, 0) for h in range(subs)]
            starts = [pl.multiple_of(jnp.maximum(d - w, 0), tq) for d in done]
            masks = [_sb_masks(key_positions(s) < d) for s, d in zip(starts, done)]

            def more_scores(p):
                return scores(p, starts, [m[0] for m in masks])

            def weights(p, sc):
                return _sb_weights(sc, tri_ref[...], [m[1] for m in masks],
                                   [car_ref[p, h] for h in range(subs)])

            def values(p, aws):
                least = None
                for h, (a, total) in enumerate(aws):
                    acc_ref[p, h] += _dot(a, v_ref[0, pl.ds(starts[h], w), lanes_of(p)])
                    carried = car_ref[p, h] + total[:, 0:1]
                    car_ref[p, h] = carried
                    m = jnp.min(carried)
                    least = m if least is None else jnp.minimum(least, m)
                return least

            mins = _software_pipeline(pairs, [more_scores, weights, values])
            return step + 1, functools.reduce(jnp.minimum, [mins[p] for p in range(pairs)])

        def cond(state):
            step, nxt = state
            return jnp.logical_and(last_start - step * w > 0, nxt < SB_DEAD_LOG2)

        lax.while_loop(cond, body, (jnp.int32(0), live))
        for p in range(pairs):
            for h in range(subs):
                o_ref[0, rows_of(h), lanes_of(p)] = _merge_heads(
                    acc_ref[p, h], lane).astype(o_ref.dtype)


def _sb_attention(q, k, v, weights):
    b, s, wd = q.shape
    pairs = wd // LANES
    subs = SB_STEP_ROWS // SB_Q_TILE
    nq = s // SB_STEP_ROWS
    tile = lambda bi, i: (bi, i, 0)
    whole = lambda bi, i: (bi, 0, 0)
    once = pl.Buffered(1)
    conv_specs = [
        pl.BlockSpec((size, w.shape[1]),
                     lambda bi, i, last=w.shape[0] // size - 1: (jnp.minimum(bi * nq + i, last), 0))
        for w, size in zip(weights, _conversion_blocks(weights, b * nq))]
    return pl.pallas_call(
        functools.partial(_sb_kernel, len(weights)),
        grid=(b, nq),
        in_specs=[
            pl.BlockSpec((1, SB_STEP_ROWS, wd), tile),
            pl.BlockSpec((1, s, wd), whole, pipeline_mode=once),
            pl.BlockSpec((1, s, wd), whole, pipeline_mode=once),
        ] + conv_specs,
        out_specs=[pl.BlockSpec((1, SB_STEP_ROWS, wd), tile)] + conv_specs,
        out_shape=[jax.ShapeDtypeStruct((b, s, wd), BF16)]
        + [jax.ShapeDtypeStruct(w.shape, BF16) for w in weights],
        scratch_shapes=[
            pltpu.VMEM((SB_WINDOW, SB_WINDOW), BF16),
            pltpu.VMEM((subs, 2, 2 * SB_Q_TILE, SB_WINDOW), F32),
            pltpu.VMEM((pairs, subs, 2 * SB_Q_TILE, LANES), F32),
            pltpu.VMEM((pairs, subs, 2 * SB_Q_TILE, LANES), F32),
        ],
        compiler_params=pltpu.CompilerParams(
            dimension_semantics=("arbitrary", "arbitrary"), vmem_limit_bytes=VMEM_LIMIT),
        name="stick_breaking_attention",
    )(q, k, v, *weights)


def _t5_log_bucket_starts():
    exact = REL_BUCKETS // 2
    d = np.arange(exact, 2 * REL_MAX_DIST, dtype=np.float32)
    large = exact + (np.log(d / np.float32(exact)) / np.float32(math.log(REL_MAX_DIST / exact))
                     * np.float32(REL_BUCKETS - exact)).astype(np.int32)
    large = np.minimum(large, REL_BUCKETS - 1)
    assert large[0] == exact and np.all(np.diff(large) >= 0)
    return [int(d[np.argmax(large >= bk)]) for bk in range(exact + 1, REL_BUCKETS)]


def _t5_causal_bucket(distance):
    d = jnp.maximum(distance, 0)
    bucket = jnp.minimum(d, REL_BUCKETS // 2)
    for start in _t5_log_bucket_starts():
        bucket = bucket + jnp.where(d >= start, 1, 0)
    return bucket


def _swa_fill_bias(relb_ref, bias_ref):
    t = ATT_BLOCK
    r = lax.broadcasted_iota(jnp.int32, (t, 2 * t), 0)
    c = lax.broadcasted_iota(jnp.int32, (t, 2 * t), 1)
    dist = r + t - c
    bucket = _t5_causal_bucket(dist)
    valid = jnp.logical_and(dist >= 0, dist < WINDOW)
    for h in range(SWA_HEADS):
        bias = jnp.zeros((t, 2 * t), F32)
        for bk in range(REL_BUCKETS):
            bias = jnp.where(bucket == bk, relb_ref[bk, h], bias)
        bias_ref[h] = jnp.where(valid, bias, MASK_NEG)


def _swa_schedule(sinks_ref, q_ref, kvc_ref, kvp_ref, bias_ref, o_ref, no_previous):
    t = ATT_BLOCK
    kvw = kvc_ref.shape[-1] // 4
    group = SWA_HEADS // SWA_KV_HEADS
    tiles = q_ref.shape[0] // t
    rows_of = lambda n: slice(n * t, (n + 1) * t)
    lane = lax.broadcasted_iota(jnp.int32, (t, LANES), 1)
    row2 = lax.broadcasted_iota(jnp.int32, (2 * t, 1), 0)
    col2 = lax.broadcasted_iota(jnp.int32, (2 * t, 2 * t), 1)
    prev_pen = jnp.where(jnp.logical_and(col2 < t, no_previous), MASK_NEG, 0.0)

    def kv_window(n, variant):
        cols = slice(variant * kvw, (variant + 1) * kvw)
        before = kvc_ref[rows_of(n - 1), cols] if n else kvp_ref[:, cols]
        return jnp.concatenate([before, kvc_ref[rows_of(n), cols]], axis=0)

    items = [(n, half, j) for n in range(tiles) for half in range(2) for j in range(SWA_KV_HEADS)]
    heads_of = lambda half, j: [h for h in range(j * group, (j + 1) * group) if h % 2 == half]

    def probabilities(i):
        n, half, j = items[i]
        heads = heads_of(half, j)
        in_half = (lane >= HEAD_DIM) if half else (lane < HEAD_DIM)
        qq = jnp.concatenate(
            [jnp.where(in_half, q_ref[rows_of(n), (h // 2) * LANES:(h // 2 + 1) * LANES], 0)
             for h in heads], axis=0)
        s = _dot_nt(qq, kv_window(n, 0 + int(j != half)))
        s = s + jnp.concatenate([bias_ref[h] for h in heads], axis=0)
        if n == 0:
            s = s + prev_pen
        sink = jnp.where(row2 < t, sinks_ref[heads[0]], sinks_ref[heads[1]])
        m = jnp.maximum(jnp.max(s, axis=-1, keepdims=True), sink)
        p = jnp.exp(s - m)
        denom = jnp.sum(p, axis=-1, keepdims=True) + jnp.exp(sink - m)
        return p.astype(BF16), 1.0 / denom

    def values(i, pd):
        n, half, j = items[i]
        p, inv_denom = pd
        return _dot(p, kv_window(n, 2 + int(j != half))) * inv_denom

    thunks, outs = _pipeline_schedule(len(items), [probabilities, values])

    def store():
        for i, (n, half, j) in enumerate(items):
            if half == 0:
                even, odd = outs[i], outs[items.index((n, 1, j))]
                for e, h in enumerate(heads_of(0, j)):
                    o_ref[rows_of(n), (h // 2) * LANES:(h // 2 + 1) * LANES] = jnp.where(
                        lane < HEAD_DIM, even[e * t:(e + 1) * t], odd[e * t:(e + 1) * t]
                    ).astype(o_ref.dtype)

    return thunks + [store]


def _tail_schedule(ha_ref, sb_ref, sw_ref, sbg_ref, swg_ref, wo_ref, l1g_ref, l1b_ref,
                   wgu_ref, wd_ref, l2g_ref, l2b_ref, o_ref):
    d = ha_ref.shape[1]
    d_ff = wd_ref.shape[0]
    mw = MXU_WIDTH
    bounds = list(range(0, d_ff, FFN_CHUNK)) + [d_ff]
    n_chunks = len(bounds) - 1
    out_tiles = range(d // mw)
    st = {}

    def merged():
        sb = _rms_norm(sb_ref[...].astype(F32), sbg_ref[...]).astype(BF16)
        sw = _rms_norm(sw_ref[...].astype(F32), swg_ref[...]).astype(BF16)
        st["merged"] = jnp.concatenate([sb, sw], axis=1)

    def out_projection(n):
        st["mix", n] = _dot(st["merged"], wo_ref[:, n * mw:(n + 1) * mw])

    def first_norm():
        mix = jnp.concatenate([st.pop(("mix", n)) for n in out_tiles], axis=1)
        st.pop("merged")
        h1 = _layer_norm(ha_ref[...] + mix, l1g_ref[...], l1b_ref[...])
        st["h1b"] = h1.astype(BF16)
        for n in out_tiles:
            st["acc", n] = ALPHA * h1[:, n * mw:(n + 1) * mw]

    def activation(c, n):
        lo = bounds[c] + n * mw
        gate = _dot(st["h1b"], wgu_ref[:, lo:lo + mw])
        up = _dot(st["h1b"], wgu_ref[:, d_ff + lo:d_ff + lo + mw])
        st["act", c, n] = (gate / (1.0 + jnp.exp2(gate * -LOG2E)) * up).astype(BF16)

    def down(c, n):
        tiles = range((bounds[c + 1] - bounds[c]) // mw)
        if n == 0:
            st["act", c] = jnp.concatenate([st.pop(("act", c, k)) for k in tiles], axis=1)
        st["acc", n] = st["acc", n] + _dot(st["act", c], wd_ref[bounds[c]:bounds[c + 1],
                                                               n * mw:(n + 1) * mw])

    def second_norm():
        y = jnp.concatenate([st.pop(("acc", n)) for n in out_tiles], axis=1)
        o_ref[...] = _layer_norm(y, l2g_ref[...], l2b_ref[...])

    act_thunks = [[functools.partial(activation, c, n)
                   for n in range((bounds[c + 1] - bounds[c]) // mw)] for c in range(n_chunks)]
    down_thunks = [[functools.partial(down, c, n) for n in out_tiles] for c in range(n_chunks)]
    ffn = list(act_thunks[0])
    for c in range(1, n_chunks):
        ffn += _interleave(act_thunks[c], down_thunks[c - 1])
    ffn += down_thunks[-1]
    return (merged, [functools.partial(out_projection, n) for n in out_tiles], first_norm,
            ffn, second_norm)


def _attn_tail_kernel(tiles_per_seq, last_tile, sinks_ref, relb_ref, q_ref, kvc_ref, kvp_ref,
                      ha_ref, sb_ref, sbg_ref, swg_ref, wo_ref, l1g_ref, l1b_ref,
                      wgu_ref, wd_ref, l2g_ref, l2b_ref, o_ref, bias_ref, sw_ref):
    t = pl.program_id(0)

    @pl.when(t == 0)
    def _first_step():
        _swa_fill_bias(relb_ref, bias_ref)
        sw_ref[1] = jnp.zeros(sw_ref.shape[1:], sw_ref.dtype)

    no_previous = (jnp.minimum(t, last_tile) % tiles_per_seq) == 0
    attention = _swa_schedule(sinks_ref, q_ref.at[0], kvc_ref.at[0], kvp_ref.at[0], bias_ref,
                              sw_ref.at[t % 2], no_previous)
    merged, out_proj, first_norm, ffn, second_norm = _tail_schedule(
        ha_ref, sb_ref, sw_ref.at[(t + 1) % 2], sbg_ref, swg_ref, wo_ref,
        l1g_ref, l1b_ref, wgu_ref, wd_ref, l2g_ref, l2b_ref, o_ref)
    tail = [merged] + out_proj + [first_norm] + ffn + [second_norm]
    for thunk in _interleave(tail, attention):
        thunk()


def _attn_tail_block(q_sw, kv_sw, sinks, rel_bias, ha, sb2, sb_g, sw_g, w_out_b, ln1_g, ln1_b,
                     w_gate_up_b, w_down_b, ln2_g, ln2_b):
    b, s, wd = q_sw.shape
    n, d = ha.shape
    tiles_per_seq = s // TAIL_ROWS
    last_tile = n // TAIL_ROWS - 1
    key_tiles = TAIL_ROWS // ATT_BLOCK
    att = lambda t: jnp.minimum(t, last_tile)
    cur = lambda t: (att(t) // tiles_per_seq, att(t) % tiles_per_seq, 0)
    prev = lambda t: (att(t) // tiles_per_seq,
                      jnp.maximum(att(t) % tiles_per_seq * key_tiles - 1, 0), 0)
    row = lambda t: (jnp.maximum(t - 1, 0), 0)
    fixed = lambda t: (0, 0)
    smem = pl.BlockSpec(memory_space=pltpu.SMEM)
    vec = lambda a: pl.BlockSpec((1, a.shape[-1]), fixed)
    weight = lambda a: pl.BlockSpec(a.shape, fixed, pipeline_mode=pl.Buffered(1))
    return pl.pallas_call(
        functools.partial(_attn_tail_kernel, tiles_per_seq, last_tile),
        grid=(last_tile + 2,),
        in_specs=[
            smem, smem,
            pl.BlockSpec((1, TAIL_ROWS, wd), cur),
            pl.BlockSpec((1, TAIL_ROWS, kv_sw.shape[-1]), cur),
            pl.BlockSpec((1, ATT_BLOCK, kv_sw.shape[-1]), prev),
            pl.BlockSpec((TAIL_ROWS, d), row),
            pl.BlockSpec((TAIL_ROWS, sb2.shape[-1]), row),
            vec(sb_g), vec(sw_g), weight(w_out_b), vec(ln1_g), vec(ln1_b),
            weight(w_gate_up_b), weight(w_down_b), vec(ln2_g), vec(ln2_b),
        ],
        out_specs=pl.BlockSpec((TAIL_ROWS, d), row),
        out_shape=jax.ShapeDtypeStruct((n, d), F32),
        scratch_shapes=[
            pltpu.VMEM((SWA_HEADS, ATT_BLOCK, 2 * ATT_BLOCK), F32),
            pltpu.VMEM((2, TAIL_ROWS, wd), BF16),
        ],
        compiler_params=pltpu.CompilerParams(
            dimension_semantics=("arbitrary",), vmem_limit_bytes=VMEM_LIMIT),
        name="swa_attention_and_dense_tail",
    )(sinks, rel_bias, q_sw, kv_sw, kv_sw, ha, sb2, sb_g, sw_g, w_out_b, ln1_g, ln1_b,
      w_gate_up_b, w_down_b, ln2_g, ln2_b)


SB_GROUP = 8
SB_LIVE_DONE = 1e30


def _sb_fast_masks(mask_ref):
    tq, w = SB_Q_TILE, SB_WINDOW
    row = lax.broadcasted_iota(jnp.int32, (2 * tq, w), 0) & (tq - 1)
    col = lax.broadcasted_iota(jnp.int32, (2 * tq, w), 1)
    causal = col < row + (w - tq)
    mask_ref[0, 0], mask_ref[0, 1] = _sb_masks(causal)
    for h in range(mask_ref.shape[0] - 1):
        first_key = w - (h + 1) * tq
        mask_ref[1 + h, 0], mask_ref[1 + h, 1] = _sb_masks(
            jnp.logical_and(causal, col >= first_key))


def _sb_fast_schedule(q_ref, kc_ref, kp_ref, vc_ref, vp_ref, tri_ref, mask_ref, o_ref,
                      no_previous, lives):
    tq, w = SB_Q_TILE, SB_WINDOW
    subs = q_ref.shape[0] // tq
    pairs = q_ref.shape[-1] // LANES
    clamped = w // tq - 1
    assert mask_ref.shape[0] == clamped + 1 and subs % SB_GROUP == 0
    lane = lax.broadcasted_iota(jnp.int32, (tq, LANES), 1)
    lanes_of = lambda p: slice(p * LANES, (p + 1) * LANES)
    rows_of = lambda h: slice(h * tq, (h + 1) * tq)

    def window(h, cur_ref, prev_ref, p):
        start = (h + 1) * tq - w
        if start >= 0:
            return cur_ref[start:start + w, lanes_of(p)]
        return jnp.concatenate([prev_ref[w + start:w, lanes_of(p)],
                                cur_ref[0:w + start, lanes_of(p)]], axis=0)

    def mask(h, which):
        if h >= clamped:
            return mask_ref[0, which]
        return mask_ref[jnp.where(no_previous, 1 + h, 0), which]

    items = [(p, g) for p in range(pairs) for g in range(subs // SB_GROUP)]
    group = lambda g: range(g * SB_GROUP, (g + 1) * SB_GROUP)

    def scores(i):
        p, g = items[i]
        return [_sb_scores(_split_heads(q_ref[rows_of(h), lanes_of(p)], lane),
                           window(h, kc_ref, kp_ref, p), mask(h, 0)) for h in group(g)]

    def weights(i, sc):
        return _sb_weights(sc, tri_ref[...], [mask(h, 1) for h in group(items[i][1])],
                           [None] * SB_GROUP)

    def values(i, aws):
        p, g = items[i]
        for h, (a, total) in zip(group(g), aws):
            o2 = _dot(a, window(h, vc_ref, vp_ref, p))
            o_ref[rows_of(h), lanes_of(p)] = _merge_heads(o2, lane).astype(o_ref.dtype)
            m = jnp.min(total[:, 0:1])
            lives.append(m if h > clamped else jnp.where(no_previous, SB_LIVE_DONE, m))

    return _pipeline_schedule(len(items), [scores, weights, values])[0]


def _constants_kernel(relb_ref, bias_ref, mask_ref, tri_ref):
    _swa_fill_bias(relb_ref, bias_ref)
    _sb_fast_masks(mask_ref)
    j = lax.broadcasted_iota(jnp.int32, tri_ref.shape, 0)
    s = lax.broadcasted_iota(jnp.int32, tri_ref.shape, 1)
    tri_ref[...] = jnp.where(j > s, 1.0, 0.0).astype(BF16)


def _attention_constants(rel_bias):
    return pl.pallas_call(
        _constants_kernel,
        in_specs=[pl.BlockSpec(memory_space=pltpu.SMEM)],
        out_shape=[
            jax.ShapeDtypeStruct((SWA_HEADS, ATT_BLOCK, 2 * ATT_BLOCK), F32),
            jax.ShapeDtypeStruct((SB_WINDOW // SB_Q_TILE, 2, 2 * SB_Q_TILE, SB_WINDOW), F32),
            jax.ShapeDtypeStruct((SB_WINDOW, SB_WINDOW), BF16),
        ],
        name="attention_constants",
    )(rel_bias)


def _fast_kernel(tiles_per_seq, last_tile, sinks_ref, bias_ref, mask_ref, tri_ref,
                 qsw_ref, kvc_ref, kvp_ref, qsb_ref, kc_ref, kp_ref, vc_ref, vp_ref,
                 ha_ref, sbg_ref, swg_ref, wo_ref, l1g_ref, l1b_ref,
                 wgu_ref, wd_ref, l2g_ref, l2b_ref, o_ref, live_ref, sw_ref, sb_ref):
    t = pl.program_id(0)

    @pl.when(t == 0)
    def _first_step():
        sw_ref[1] = jnp.zeros(sw_ref.shape[1:], sw_ref.dtype)
        sb_ref[1] = jnp.zeros(sb_ref.shape[1:], sb_ref.dtype)
        live_ref[...] = jnp.full(live_ref.shape, SB_LIVE_DONE, F32)

    no_previous = (jnp.minimum(t, last_tile) % tiles_per_seq) == 0
    lives = []
    swa = _swa_schedule(sinks_ref, qsw_ref.at[0], kvc_ref.at[0], kvp_ref.at[0], bias_ref,
                        sw_ref.at[t % 2], no_previous)
    sb = _sb_fast_schedule(qsb_ref.at[0], kc_ref.at[0], kp_ref.at[0], vc_ref.at[0], vp_ref.at[0],
                           tri_ref, mask_ref, sb_ref.at[t % 2], no_previous, lives)
    merged, out_proj, first_norm, ffn, second_norm = _tail_schedule(
        ha_ref, sb_ref.at[(t + 1) % 2], sw_ref.at[(t + 1) % 2], sbg_ref, swg_ref, wo_ref,
        l1g_ref, l1b_ref, wgu_ref, wd_ref, l2g_ref, l2b_ref, o_ref)
    tail = [merged] + out_proj + [first_norm] + ffn + [second_norm]
    for thunk in _interleave(tail, _interleave(swa, sb)):
        thunk()
    live_ref[...] = jnp.minimum(live_ref[...], functools.reduce(jnp.minimum, lives))


def _fast_block(q_sw, kv_sw, q_sb, k_sb, v_sb, sinks, constants, ha, sb_g, sw_g, w_out_b,
                ln1_g, ln1_b, w_gate_up_b, w_down_b, ln2_g, ln2_b):
    b, s, wd = q_sw.shape
    n, d = ha.shape
    tiles_per_seq = s // TAIL_ROWS
    last_tile = n // TAIL_ROWS - 1
    att = lambda t: jnp.minimum(t, last_tile)
    cur = lambda t: (att(t) // tiles_per_seq, att(t) % tiles_per_seq, 0)

    def prev(rows):
        per_tile = TAIL_ROWS // rows
        return lambda t: (att(t) // tiles_per_seq,
                          jnp.maximum(att(t) % tiles_per_seq * per_tile - 1, 0), 0)

    row = lambda t: (jnp.maximum(t - 1, 0), 0)
    fixed = lambda t: (0, 0)
    smem = pl.BlockSpec(memory_space=pltpu.SMEM)
    vec = lambda a: pl.BlockSpec((1, a.shape[-1]), fixed)
    weight = lambda a: pl.BlockSpec(a.shape, fixed, pipeline_mode=pl.Buffered(1))
    tile3 = lambda a: pl.BlockSpec((1, TAIL_ROWS, a.shape[-1]), cur)
    before = lambda a, rows: pl.BlockSpec((1, rows, a.shape[-1]), prev(rows))
    steps = last_tile + 2
    constant = lambda a: pl.BlockSpec(a.shape, lambda t: (0,) * a.ndim,
                                      pipeline_mode=pl.Buffered(1))
    return pl.pallas_call(
        functools.partial(_fast_kernel, tiles_per_seq, last_tile),
        grid=(steps,),
        in_specs=[
            smem, *map(constant, constants),
            tile3(q_sw), tile3(kv_sw), before(kv_sw, ATT_BLOCK),
            tile3(q_sb), tile3(k_sb), before(k_sb, SB_WINDOW), tile3(v_sb), before(v_sb, SB_WINDOW),
            pl.BlockSpec((TAIL_ROWS, d), row),
            vec(sb_g), vec(sw_g), weight(w_out_b), vec(ln1_g), vec(ln1_b),
            weight(w_gate_up_b), weight(w_down_b), vec(ln2_g), vec(ln2_b),
        ],
        out_specs=[pl.BlockSpec((TAIL_ROWS, d), row),
                   pl.BlockSpec((8, LANES), fixed)],
        out_shape=[jax.ShapeDtypeStruct((n, d), F32),
                   jax.ShapeDtypeStruct((8, LANES), F32)],
        scratch_shapes=[
            pltpu.VMEM((2, TAIL_ROWS, wd), BF16),
            pltpu.VMEM((2, TAIL_ROWS, q_sb.shape[-1]), BF16),
        ],
        compiler_params=pltpu.CompilerParams(
            dimension_semantics=("arbitrary",), vmem_limit_bytes=VMEM_LIMIT),
        name="attention_and_dense_tail",
    )(sinks, *constants, q_sw, kv_sw, kv_sw, q_sb, k_sb, k_sb, v_sb, v_sb, ha, sb_g, sw_g,
      w_out_b, ln1_g, ln1_b, w_gate_up_b, w_down_b, ln2_g, ln2_b)


def kernel(x, ln_in_g, ln_in_b, w_in, sb_norm_g, swa_norm_g, sinks, rel_bias, w_out,
           ln1_g, ln1_b, w_gate_up, w_down, ln2_g, ln2_b):
    b, s, d = x.shape
    assert w_in.shape[0] == DEPTH == 1
    sb_w = SB_HEADS * HEAD_DIM
    sw_w = SWA_HEADS * HEAD_DIM
    kv_w = SWA_KV_HEADS * HEAD_DIM
    assert w_in.shape[-1] == 3 * sb_w + sw_w + 2 * kv_w
    assert kv_w == LANES and SWA_HEADS // SWA_KV_HEADS == 4
    assert s % SB_WINDOW == 0 and s % SB_STEP_ROWS == 0 and s % TAIL_ROWS == 0
    assert (b * s) % ROW_TILE == 0 and WINDOW <= ATT_BLOCK and d % MXU_WIDTH == 0
    assert FFN_CHUNK % MXU_WIDTH == 0 and w_down.shape[1] % MXU_WIDTH == 0

    x2 = x.reshape(b * s, d)
    row_vec = lambda a: a.reshape(1, -1)

    names = ("q_sb", "k_sb", "v_sb", "q_sw", "k_sw", "v_sw")
    widths = (sb_w, sb_w, sb_w, sw_w, kv_w, kv_w)
    splits, lo = {}, 0
    for name, wd in zip(names, widths):
        splits[name] = (lo, lo + wd)
        lo += wd
    ha, q_sb, k_sb, v_sb, q_sw, kv_sw, w_out_b, w_gate_up_b, w_down_b = _in_projection(
        x2, row_vec(ln_in_g), row_vec(ln_in_b), w_in[0], splits,
        [w_out[0], w_gate_up[0], w_down[0]])

    to3 = lambda a: a.reshape(b, s, a.shape[-1])
    q_sb, k_sb, v_sb, q_sw, kv_sw = map(to3, (q_sb, k_sb, v_sb, q_sw, kv_sw))
    tail_args = (row_vec(sb_norm_g[0]), row_vec(swa_norm_g[0]), w_out_b,
                 row_vec(ln1_g[0]), row_vec(ln1_b[0]), w_gate_up_b, w_down_b,
                 row_vec(ln2_g[0]), row_vec(ln2_b[0]))

    fast, live = _fast_block(q_sw, kv_sw, q_sb, k_sb, v_sb, sinks[0],
                             _attention_constants(rel_bias), ha, *tail_args)

    def general(_):
        (sb_out,) = _sb_attention(q_sb, k_sb, v_sb, [])
        return _attn_tail_block(q_sw, kv_sw, sinks[0], rel_bias, ha,
                                sb_out.reshape(b * s, sb_w), *tail_args)

    out = lax.cond(live[0, 0] < SB_DEAD_LOG2, general, lambda _: fast, None)
    return out.reshape(b, s, d)
```

```python
import functools
import math

import jax
import jax.numpy as jnp
import numpy as np
from jax import lax
from jax.experimental import pallas as pl
from jax.experimental.pallas import tpu as pltpu

F32 = jnp.float32
BF16 = jnp.bfloat16

HEAD_DIM = 64
SB_HEADS = 8
SWA_HEADS = 8
SWA_KV_HEADS = 2
WINDOW = 128
REL_BUCKETS = 32
REL_MAX_DIST = 128
LN_EPS = 1e-5
RMS_EPS = 1e-6
DEPTH = 1
ALPHA = (2 * DEPTH) ** 0.25
QK_SCALE = HEAD_DIM ** -0.5
LOG2E = math.log2(math.e)

LANES = 128
BF16_ROWS = 16
MXU_WIDTH = 256
ATT_BLOCK = 128
SB_WINDOW = 256
SB_Q_TILE = 64
SB_STEP_ROWS = 512
ROW_TILE = 1024
PROJ_PARTS = 2
TAIL_ROWS = 512
FFN_CHUNK = 4 * MXU_WIDTH
VMEM_LIMIT = 56 * 1024 * 1024

SB_DEAD_LOG2 = 127.0
MASK_NEG = -1e30


def _layer_norm(x, g, b):
    mu = jnp.mean(x, axis=-1, keepdims=True)
    xc = x - mu
    var = jnp.mean(xc * xc, axis=-1, keepdims=True)
    return xc * lax.rsqrt(var + LN_EPS) * g + b


def _rms_norm(x, g):
    ms = jnp.mean(x * x, axis=-1, keepdims=True)
    return x * lax.rsqrt(ms + RMS_EPS) * g


def _dot(a, b):
    return jnp.dot(a, b, preferred_element_type=F32)


def _dot_nt(a, b):
    return lax.dot_general(a, b, (((1,), (1,)), ((), ())), preferred_element_type=F32)


def _pipeline_schedule(n, stages):
    vals = [dict() for _ in stages]
    thunks = []
    for step in range(n + len(stages) - 1):
        for s, stage in enumerate(stages):
            i = step - s
            if 0 <= i < n:
                def thunk(s=s, i=i, stage=stage):
                    vals[s][i] = stage(i, vals[s - 1].pop(i)) if s else stage(i)
                thunks.append(thunk)
    return thunks, vals[-1]


def _software_pipeline(n, stages):
    thunks, results = _pipeline_schedule(n, stages)
    for thunk in thunks:
        thunk()
    return results


def _interleave(xs, ys):
    merged, taken = [], 0
    for i, x in enumerate(xs):
        merged.append(x)
        upto = (i + 1) * len(ys) // len(xs)
        merged.extend(ys[taken:upto])
        taken = upto
    return merged


def _inproj_kernel(splits, n_later, x_ref, g_ref, b_ref, w32_ref, *refs):
    later_in, refs = refs[:n_later], refs[n_later:]
    qsb_ref, ksb_ref, vsb_ref, qsw_ref, kv_ref = refs[:5]
    later_out, w_ref = refs[5:5 + n_later], refs[5 + n_later]
    rows = x_ref.shape[0] // PROJ_PARTS
    part = lambda i: slice(i * rows, (i + 1) * rows)
    assert splits["k_sw"][1] == splits["v_sw"][0]

    @pl.when(pl.program_id(0) == 0)
    def _convert_weight():
        w_ref[...] = w32_ref[...].astype(BF16)

    for src, dst in zip(later_in, later_out):
        dst[...] = src[...].astype(BF16)

    def normed(i):
        return _layer_norm(x_ref[part(i), :], g_ref[...], b_ref[...]).astype(BF16)

    def projected(i, h):
        r = part(i)
        proj = lambda lo, hi: _dot(h, w_ref[:, lo:hi])
        qsb_ref[r, :] = (proj(*splits["q_sb"]) * QK_SCALE).astype(BF16)
        ksb_ref[r, :] = proj(*splits["k_sb"]).astype(BF16)
        vsb_ref[r, :] = proj(*splits["v_sb"]).astype(BF16)
        qsw_ref[r, :] = (proj(*splits["q_sw"]) * QK_SCALE).astype(BF16)
        kv = proj(splits["k_sw"][0], splits["v_sw"][1])
        kvw = kv.shape[-1] // 2
        k, v = kv[:, :kvw], kv[:, kvw:]
        kv_ref[r, 0 * kvw:1 * kvw] = k.astype(BF16)
        kv_ref[r, 1 * kvw:2 * kvw] = pltpu.roll(k, HEAD_DIM, axis=1).astype(BF16)
        kv_ref[r, 2 * kvw:3 * kvw] = v.astype(BF16)
        kv_ref[r, 3 * kvw:4 * kvw] = pltpu.roll(v, HEAD_DIM, axis=1).astype(BF16)

    _software_pipeline(PROJ_PARTS, [normed, projected])


def _in_projection(x2, g, b, w_in, splits, later_weights):
    n, d = x2.shape
    steps = n // ROW_TILE
    width = lambda name: splits[name][1] - splits[name][0]
    out_w = [width("q_sb"), width("k_sb"), width("v_sb"), width("q_sw"), 4 * width("k_sw")]
    row = lambda i: (i, 0)
    fixed = lambda i: (0, 0)
    later_specs = [
        pl.BlockSpec((size, w.shape[1]),
                     lambda i, last=w.shape[0] // size - 1: (jnp.minimum(i, last), 0))
        for w, size in zip(later_weights, _conversion_blocks(later_weights, steps))]
    return pl.pallas_call(
        functools.partial(_inproj_kernel, splits, len(later_weights)),
        grid=(steps,),
        in_specs=[
            pl.BlockSpec((ROW_TILE, d), row),
            pl.BlockSpec((1, d), fixed),
            pl.BlockSpec((1, d), fixed),
            pl.BlockSpec(w_in.shape, fixed, pipeline_mode=pl.Buffered(1)),
        ] + later_specs,
        out_specs=[pl.BlockSpec((ROW_TILE, w), row) for w in out_w] + later_specs,
        out_shape=[jax.ShapeDtypeStruct((n, w), BF16) for w in out_w]
        + [jax.ShapeDtypeStruct(w.shape, BF16) for w in later_weights],
        scratch_shapes=[pltpu.VMEM(w_in.shape, BF16)],
        compiler_params=pltpu.CompilerParams(
            dimension_semantics=("arbitrary",), vmem_limit_bytes=VMEM_LIMIT),
        name="ln_in_proj",
    )(x2, g, b, w_in, *later_weights)


def _conversion_blocks(weights, steps):
    sizes = []
    for w in weights:
        rows = w.shape[0]
        sizes.append(next(r for r in range(BF16_ROWS, rows + 1, BF16_ROWS)
                          if rows % r == 0 and rows // r <= steps))
    return sizes


def _split_heads(x, lane):
    zero = jnp.zeros_like(x)
    return jnp.concatenate([jnp.where(lane < HEAD_DIM, x, zero),
                            jnp.where(lane >= HEAD_DIM, x, zero)], axis=0)


def _merge_heads(o2, lane):
    t = o2.shape[0] // 2
    return jnp.where(lane < HEAD_DIM, o2[:t], o2[t:])


def _sb_masks(valid):
    return jnp.where(valid, LOG2E, 0.0), jnp.where(valid, 0.0, MASK_NEG)


def _sb_scores(qq, k, keep):
    z = _dot_nt(qq, k)
    zl = z * LOG2E
    c_nat = jnp.maximum(z, 0.0) + jnp.log(1.0 + jnp.exp2(-jnp.abs(zl)))
    cm = c_nat * keep
    return zl - cm, cm.astype(BF16), cm[:, :LANES]


def _sb_weights(scores, tri, kills, carrieds):
    sums = _dot(jnp.concatenate([sc[1] for sc in scores], axis=0), tri)
    rows = scores[0][0].shape[0]
    out = []
    for n, ((log_beta, _, cm0), kill, carried) in enumerate(zip(scores, kills, carrieds)):
        later = sums[n * rows:(n + 1) * rows]
        log_a = (log_beta - later) + kill
        if carried is not None:
            log_a = log_a - jnp.concatenate([carried] * (log_a.shape[1] // LANES), axis=1)
        out.append((jnp.exp2(log_a).astype(BF16), later[:, :LANES] + cm0))
    return out


def _sb_kernel(n_conv, q_ref, k_ref, v_ref, *refs):
    conv_in, o_ref, conv_out = refs[:n_conv], refs[n_conv], refs[n_conv + 1:2 * n_conv + 1]
    tri_ref, mask_ref, acc_ref, car_ref = refs[2 * n_conv + 1:]
    for src, dst in zip(conv_in, conv_out):
        dst[...] = src[...].astype(BF16)

    bi = pl.program_id(0)
    qi = pl.program_id(1)
    tq, w = SB_Q_TILE, SB_WINDOW
    subs = q_ref.shape[1] // tq
    pairs = q_ref.shape[-1] // LANES
    row0 = qi * (subs * tq)

    @pl.when(jnp.logical_and(bi == 0, qi == 0))
    def _fill_tri():
        j = lax.broadcasted_iota(jnp.int32, (w, w), 0)
        s = lax.broadcasted_iota(jnp.int32, (w, w), 1)
        tri_ref[...] = jnp.where(j > s, 1.0, 0.0).astype(BF16)

    lane = lax.broadcasted_iota(jnp.int32, (tq, LANES), 1)
    lanes_of = lambda p: slice(p * LANES, (p + 1) * LANES)
    rows_of = lambda h: slice(h * tq, (h + 1) * tq)
    first_start = lambda h: pl.multiple_of(jnp.maximum(row0 + (h + 1) * tq - w, 0), tq)

    def key_positions(start):
        return start + lax.broadcasted_iota(jnp.int32, (2 * tq, w), 1)

    assert subs * tq >= w - tq

    @pl.when(qi <= 1)
    def _fill_masks():
        row = lax.broadcasted_iota(jnp.int32, (2 * tq, w), 0) & (tq - 1)
        for h in range(subs):
            mask_ref[h, 0], mask_ref[h, 1] = _sb_masks(
                key_positions(first_start(h)) < row0 + h * tq + row)

    def scores(p, starts, keeps):
        return [_sb_scores(_split_heads(q_ref[0, rows_of(h), lanes_of(p)], lane),
                           k_ref[0, pl.ds(starts[h], w), lanes_of(p)], keeps[h])
                for h in range(subs)]

    first_starts = [first_start(h) for h in range(subs)]

    def first_scores(p):
        return scores(p, first_starts, [mask_ref[h, 0] for h in range(subs)])

    def first_weights(p, sc):
        return _sb_weights(sc, tri_ref[...], [mask_ref[h, 1] for h in range(subs)], [None] * subs)

    def first_values(p, aws):
        least = None
        for h, (a, total) in enumerate(aws):
            o2 = _dot(a, v_ref[0, pl.ds(first_starts[h], w), lanes_of(p)])
            o_ref[0, rows_of(h), lanes_of(p)] = _merge_heads(o2, lane).astype(o_ref.dtype)
            acc_ref[p, h] = o2
            car_ref[p, h] = total
            m = jnp.min(total[:, 0:1])
            least = m if least is None else jnp.minimum(least, m)
        return least

    first = _software_pipeline(pairs, [first_scores, first_weights, first_values])
    live = functools.reduce(jnp.minimum, [first[p] for p in range(pairs)])
    last_start = first_starts[-1]

    @pl.when(jnp.logical_and(last_start > 0, live < SB_DEAD_LOG2))
    def _walk_back():
        for p in range(pairs):
            for h in range(subs):
                car_ref[p, h] = jnp.broadcast_to(car_ref[p, h][:, 0:1], (2 * tq, LANES))

        def body(state):
            step, _ = state
            done = [jnp.maximum(first_start(h) - step * w, 0) for h in range(subs)]
            starts = [pl.multiple_of(jnp.maximum(d - w, 0), tq) for d in done]
            masks = [_sb_masks(key_positions(s) < d) for s, d in zip(starts, done)]

            def more_scores(p):
                return scores(p, starts, [m[0] for m in masks])

            def weights(p, sc):
                return _sb_weights(sc, tri_ref[...], [m[1] for m in masks],
                                   [car_ref[p, h] for h in range(subs)])

            def values(p, aws):
                least = None
                for h, (a, total) in enumerate(aws):
                    acc_ref[p, h] += _dot(a, v_ref[0, pl.ds(starts[h], w), lanes_of(p)])
                    carried = car_ref[p, h] + total[:, 0:1]
                    car_ref[p, h] = carried
                    m = jnp.min(carried)
                    least = m if least is None else jnp.minimum(least, m)
                return least

            mins = _software_pipeline(pairs, [more_scores, weights, values])
            return step + 1, functools.reduce(jnp.minimum, [mins[p] for p in range(pairs)])

        def cond(state):
            step, nxt = state
            return jnp.logical_and(last_start - step * w > 0, nxt < SB_DEAD_LOG2)

        lax.while_loop(cond, body, (jnp.int32(0), live))
        for p in range(pairs):
            for h in range(subs):
                o_ref[0, rows_of(h), lanes_of(p)] = _merge_heads(
                    acc_ref[p, h], lane).astype(o_ref.dtype)


def _sb_attention(q, k, v, weights):
    b, s, wd = q.shape
    pairs = wd // LANES
    subs = SB_STEP_ROWS // SB_Q_TILE
    nq = s // SB_STEP_ROWS
    tile = lambda bi, i: (bi, i, 0)
    whole = lambda bi, i: (bi, 0, 0)
    once = pl.Buffered(1)
    conv_specs = [
        pl.BlockSpec((size, w.shape[1]),
                     lambda bi, i, last=w.shape[0] // size - 1: (jnp.minimum(bi * nq + i, last), 0))
        for w, size in zip(weights, _conversion_blocks(weights, b * nq))]
    return pl.pallas_call(
        functools.partial(_sb_kernel, len(weights)),
        grid=(b, nq),
        in_specs=[
            pl.BlockSpec((1, SB_STEP_ROWS, wd), tile),
            pl.BlockSpec((1, s, wd), whole, pipeline_mode=once),
            pl.BlockSpec((1, s, wd), whole, pipeline_mode=once),
        ] + conv_specs,
        out_specs=[pl.BlockSpec((1, SB_STEP_ROWS, wd), tile)] + conv_specs,
        out_shape=[jax.ShapeDtypeStruct((b, s, wd), BF16)]
        + [jax.ShapeDtypeStruct(w.shape, BF16) for w in weights],
        scratch_shapes=[
            pltpu.VMEM((SB_WINDOW, SB_WINDOW), BF16),
            pltpu.VMEM((subs, 2, 2 * SB_Q_TILE, SB_WINDOW), F32),
            pltpu.VMEM((pairs, subs, 2 * SB_Q_TILE, LANES), F32),
            pltpu.VMEM((pairs, subs, 2 * SB_Q_TILE, LANES), F32),
        ],
        compiler_params=pltpu.CompilerParams(
            dimension_semantics=("arbitrary", "arbitrary"), vmem_limit_bytes=VMEM_LIMIT),
        name="stick_breaking_attention",
    )(q, k, v, *weights)


def _t5_log_bucket_starts():
    exact = REL_BUCKETS // 2
    d = np.arange(exact, 2 * REL_MAX_DIST, dtype=np.float32)
    large = exact + (np.log(d / np.float32(exact)) / np.float32(math.log(REL_MAX_DIST / exact))
                     * np.float32(REL_BUCKETS - exact)).astype(np.int32)
    large = np.minimum(large, REL_BUCKETS - 1)
    assert large[0] == exact and np.all(np.diff(large) >= 0)
    return [int(d[np.argmax(large >= bk)]) for bk in range(exact + 1, REL_BUCKETS)]


def _t5_causal_bucket(distance):
    d = jnp.maximum(distance, 0)
    bucket = jnp.minimum(d, REL_BUCKETS // 2)
    for start in _t5_log_bucket_starts():
        bucket = bucket + jnp.where(d >= start, 1, 0)
    return bucket


def _swa_fill_bias(relb_ref, bias_ref):
    t = ATT_BLOCK
    r = lax.broadcasted_iota(jnp.int32, (t, 2 * t), 0)
    c = lax.broadcasted_iota(jnp.int32, (t, 2 * t), 1)
    dist = r + t - c
    bucket = _t5_causal_bucket(dist)
    valid = jnp.logical_and(dist >= 0, dist < WINDOW)
    for h in range(SWA_HEADS):
        bias = jnp.zeros((t, 2 * t), F32)
        for bk in range(REL_BUCKETS):
            bias = jnp.where(bucket == bk, relb_ref[bk, h], bias)
        bias_ref[h] = jnp.where(valid, bias, MASK_NEG)


def _swa_schedule(sinks_ref, q_ref, kvc_ref, kvp_ref, bias_ref, o_ref, no_previous):
    t = ATT_BLOCK
    kvw = kvc_ref.shape[-1] // 4
    group = SWA_HEADS // SWA_KV_HEADS
    tiles = q_ref.shape[0] // t
    rows_of = lambda n: slice(n * t, (n + 1) * t)
    lane = lax.broadcasted_iota(jnp.int32, (t, LANES), 1)
    row2 = lax.broadcasted_iota(jnp.int32, (2 * t, 1), 0)
    col2 = lax.broadcasted_iota(jnp.int32, (2 * t, 2 * t), 1)
    prev_pen = jnp.where(jnp.logical_and(col2 < t, no_previous), MASK_NEG, 0.0)

    def kv_window(n, variant):
        cols = slice(variant * kvw, (variant + 1) * kvw)
        before = kvc_ref[rows_of(n - 1), cols] if n else kvp_ref[:, cols]
        return jnp.concatenate([before, kvc_ref[rows_of(n), cols]], axis=0)

    items = [(n, half, j) for n in range(tiles) for half in range(2) for j in range(SWA_KV_HEADS)]
    heads_of = lambda half, j: [h for h in range(j * group, (j + 1) * group) if h % 2 == half]

    def probabilities(i):
        n, half, j = items[i]
        heads = heads_of(half, j)
        in_half = (lane >= HEAD_DIM) if half else (lane < HEAD_DIM)
        qq = jnp.concatenate(
            [jnp.where(in_half, q_ref[rows_of(n), (h // 2) * LANES:(h // 2 + 1) * LANES], 0)
             for h in heads], axis=0)
        s = _dot_nt(qq, kv_window(n, 0 + int(j != half)))
        s = s + jnp.concatenate([bias_ref[h] for h in heads], axis=0)
        if n == 0:
            s = s + prev_pen
        sink = jnp.where(row2 < t, sinks_ref[heads[0]], sinks_ref[heads[1]])
        m = jnp.maximum(jnp.max(s, axis=-1, keepdims=True), sink)
        p = jnp.exp(s - m)
        denom = jnp.sum(p, axis=-1, keepdims=True) + jnp.exp(sink - m)
        return p.astype(BF16), 1.0 / denom

    def values(i, pd):
        n, half, j = items[i]
        p, inv_denom = pd
        return _dot(p, kv_window(n, 2 + int(j != half))) * inv_denom

    thunks, outs = _pipeline_schedule(len(items), [probabilities, values])

    def store():
        for i, (n, half, j) in enumerate(items):
            if half == 0:
                even, odd = outs[i], outs[items.index((n, 1, j))]
                for e, h in enumerate(heads_of(0, j)):
                    o_ref[rows_of(n), (h // 2) * LANES:(h // 2 + 1) * LANES] = jnp.where(
                        lane < HEAD_DIM, even[e * t:(e + 1) * t], odd[e * t:(e + 1) * t]
                    ).astype(o_ref.dtype)

    return thunks + [store]


def _tail_schedule(x_ref, ling_ref, linb_ref, sb_ref, sw_ref, sbg_ref, swg_ref, wo_ref,
                   l1g_ref, l1b_ref, wgu_ref, wd_ref, l2g_ref, l2b_ref, o_ref):
    d = x_ref.shape[1]
    d_ff = wd_ref.shape[0]
    mw = MXU_WIDTH
    bounds = list(range(0, d_ff, FFN_CHUNK)) + [d_ff]
    n_chunks = len(bounds) - 1
    out_tiles = range(d // mw)
    st = {}

    def merged():
        sb = _rms_norm(sb_ref[...].astype(F32), sbg_ref[...]).astype(BF16)
        sw = _rms_norm(sw_ref[...].astype(F32), swg_ref[...]).astype(BF16)
        st["merged"] = jnp.concatenate([sb, sw], axis=1)

    def out_projection(n):
        st["mix", n] = _dot(st["merged"], wo_ref[:, n * mw:(n + 1) * mw])

    def first_norm():
        mix = jnp.concatenate([st.pop(("mix", n)) for n in out_tiles], axis=1)
        st.pop("merged")
        h = _layer_norm(x_ref[...], ling_ref[...], linb_ref[...])
        h1 = _layer_norm(ALPHA * h + mix, l1g_ref[...], l1b_ref[...])
        st["h1b"] = h1.astype(BF16)
        for n in out_tiles:
            st["acc", n] = ALPHA * h1[:, n * mw:(n + 1) * mw]

    def activation(c, n):
        lo = bounds[c] + n * mw
        gate = _dot(st["h1b"], wgu_ref[:, lo:lo + mw])
        up = _dot(st["h1b"], wgu_ref[:, d_ff + lo:d_ff + lo + mw])
        st["act", c, n] = (gate / (1.0 + jnp.exp(-gate)) * up).astype(BF16)

    def down(c, n):
        tiles = range((bounds[c + 1] - bounds[c]) // mw)
        if n == 0:
            st["act", c] = jnp.concatenate([st.pop(("act", c, k)) for k in tiles], axis=1)
        st["acc", n] = st["acc", n] + _dot(st["act", c], wd_ref[bounds[c]:bounds[c + 1],
                                                               n * mw:(n + 1) * mw])

    def second_norm():
        y = jnp.concatenate([st.pop(("acc", n)) for n in out_tiles], axis=1)
        o_ref[...] = _layer_norm(y, l2g_ref[...], l2b_ref[...])

    act_thunks = [[functools.partial(activation, c, n)
                   for n in range((bounds[c + 1] - bounds[c]) // mw)] for c in range(n_chunks)]
    down_thunks = [[functools.partial(down, c, n) for n in out_tiles] for c in range(n_chunks)]
    ffn = list(act_thunks[0])
    for c in range(1, n_chunks):
        ffn += _interleave(act_thunks[c], down_thunks[c - 1])
    ffn += down_thunks[-1]
    return (merged, [functools.partial(out_projection, n) for n in out_tiles], first_norm,
            ffn, second_norm)


def _attn_tail_kernel(tiles_per_seq, last_tile, sinks_ref, relb_ref, q_ref, kvc_ref, kvp_ref,
                      x_ref, ling_ref, linb_ref, sb_ref, sbg_ref, swg_ref, wo_ref, l1g_ref, l1b_ref,
                      wgu_ref, wd_ref, l2g_ref, l2b_ref, o_ref, bias_ref, sw_ref):
    t = pl.program_id(0)

    @pl.when(t == 0)
    def _first_step():
        _swa_fill_bias(relb_ref, bias_ref)
        sw_ref[1] = jnp.zeros(sw_ref.shape[1:], sw_ref.dtype)

    no_previous = (jnp.minimum(t, last_tile) % tiles_per_seq) == 0
    attention = _swa_schedule(sinks_ref, q_ref.at[0], kvc_ref.at[0], kvp_ref.at[0], bias_ref,
                              sw_ref.at[t % 2], no_previous)
    merged, out_proj, first_norm, ffn, second_norm = _tail_schedule(
        x_ref, ling_ref, linb_ref, sb_ref, sw_ref.at[(t + 1) % 2], sbg_ref, swg_ref, wo_ref,
        l1g_ref, l1b_ref, wgu_ref, wd_ref, l2g_ref, l2b_ref, o_ref)
    tail = [merged] + out_proj + [first_norm] + ffn + [second_norm]
    for thunk in _interleave(tail, attention):
        thunk()


def _attn_tail_block(q_sw, kv_sw, sinks, rel_bias, x2, ln_in_g, ln_in_b, sb2, sb_g, sw_g,
                     w_out_b, ln1_g, ln1_b, w_gate_up_b, w_down_b, ln2_g, ln2_b):
    b, s, wd = q_sw.shape
    n, d = x2.shape
    tiles_per_seq = s // TAIL_ROWS
    last_tile = n // TAIL_ROWS - 1
    key_tiles = TAIL_ROWS // ATT_BLOCK
    att = lambda t: jnp.minimum(t, last_tile)
    cur = lambda t: (att(t) // tiles_per_seq, att(t) % tiles_per_seq, 0)
    prev = lambda t: (att(t) // tiles_per_seq,
                      jnp.maximum(att(t) % tiles_per_seq * key_tiles - 1, 0), 0)
    row = lambda t: (jnp.maximum(t - 1, 0), 0)
    fixed = lambda t: (0, 0)
    smem = pl.BlockSpec(memory_space=pltpu.SMEM)
    vec = lambda a: pl.BlockSpec((1, a.shape[-1]), fixed)
    weight = lambda a: pl.BlockSpec(a.shape, fixed, pipeline_mode=pl.Buffered(1))
    return pl.pallas_call(
        functools.partial(_attn_tail_kernel, tiles_per_seq, last_tile),
        grid=(last_tile + 2,),
        in_specs=[
            smem, smem,
            pl.BlockSpec((1, TAIL_ROWS, wd), cur),
            pl.BlockSpec((1, TAIL_ROWS, kv_sw.shape[-1]), cur),
            pl.BlockSpec((1, ATT_BLOCK, kv_sw.shape[-1]), prev),
            pl.BlockSpec((TAIL_ROWS, d), row), vec(ln_in_g), vec(ln_in_b),
            pl.BlockSpec((TAIL_ROWS, sb2.shape[-1]), row),
            vec(sb_g), vec(sw_g), weight(w_out_b), vec(ln1_g), vec(ln1_b),
            weight(w_gate_up_b), weight(w_down_b), vec(ln2_g), vec(ln2_b),
        ],
        out_specs=pl.BlockSpec((TAIL_ROWS, d), row),
        out_shape=jax.ShapeDtypeStruct((n, d), F32),
        scratch_shapes=[
            pltpu.VMEM((SWA_HEADS, ATT_BLOCK, 2 * ATT_BLOCK), F32),
            pltpu.VMEM((2, TAIL_ROWS, wd), BF16),
        ],
        compiler_params=pltpu.CompilerParams(
            dimension_semantics=("arbitrary",), vmem_limit_bytes=VMEM_LIMIT),
        name="swa_attention_and_dense_tail",
    )(sinks, rel_bias, q_sw, kv_sw, kv_sw, x2, ln_in_g, ln_in_b, sb2, sb_g, sw_g, w_out_b,
      ln1_g, ln1_b, w_gate_up_b, w_down_b, ln2_g, ln2_b)


SB_GROUP = 8
SB_LIVE_DONE = 1e30


def _sb_fast_masks(mask_ref):
    tq, w = SB_Q_TILE, SB_WINDOW
    row = lax.broadcasted_iota(jnp.int32, (2 * tq, w), 0) & (tq - 1)
    col = lax.broadcasted_iota(jnp.int32, (2 * tq, w), 1)
    causal = col < row + (w - tq)
    mask_ref[0, 0], mask_ref[0, 1] = _sb_masks(causal)
    for h in range(mask_ref.shape[0] - 1):
        first_key = w - (h + 1) * tq
        mask_ref[1 + h, 0], mask_ref[1 + h, 1] = _sb_masks(
            jnp.logical_and(causal, col >= first_key))


def _sb_fast_schedule(q_ref, kc_ref, kp_ref, vc_ref, vp_ref, tri_ref, mask_ref, o_ref,
                      no_previous, lives):
    tq, w = SB_Q_TILE, SB_WINDOW
    subs = q_ref.shape[0] // tq
    pairs = q_ref.shape[-1] // LANES
    clamped = w // tq - 1
    assert mask_ref.shape[0] == clamped + 1 and subs % SB_GROUP == 0
    lane = lax.broadcasted_iota(jnp.int32, (tq, LANES), 1)
    lanes_of = lambda p: slice(p * LANES, (p + 1) * LANES)
    rows_of = lambda h: slice(h * tq, (h + 1) * tq)

    def window(h, cur_ref, prev_ref, p):
        start = (h + 1) * tq - w
        if start >= 0:
            return cur_ref[start:start + w, lanes_of(p)]
        return jnp.concatenate([prev_ref[w + start:w, lanes_of(p)],
                                cur_ref[0:w + start, lanes_of(p)]], axis=0)

    def mask(h, which):
        if h >= clamped:
            return mask_ref[0, which]
        return mask_ref[jnp.where(no_previous, 1 + h, 0), which]

    items = [(p, g) for p in range(pairs) for g in range(subs // SB_GROUP)]
    group = lambda g: range(g * SB_GROUP, (g + 1) * SB_GROUP)

    def scores(i):
        p, g = items[i]
        return [_sb_scores(_split_heads(q_ref[rows_of(h), lanes_of(p)], lane),
                           window(h, kc_ref, kp_ref, p), mask(h, 0)) for h in group(g)]

    def weights(i, sc):
        return _sb_weights(sc, tri_ref[...], [mask(h, 1) for h in group(items[i][1])],
                           [None] * SB_GROUP)

    def values(i, aws):
        p, g = items[i]
        for h, (a, total) in zip(group(g), aws):
            o2 = _dot(a, window(h, vc_ref, vp_ref, p))
            o_ref[rows_of(h), lanes_of(p)] = _merge_heads(o2, lane).astype(o_ref.dtype)
            m = jnp.min(total[:, 0:1])
            lives.append(m if h > clamped else jnp.where(no_previous, SB_LIVE_DONE, m))

    return _pipeline_schedule(len(items), [scores, weights, values])[0]


def _constants_kernel(relb_ref, bias_ref, mask_ref, tri_ref):
    _swa_fill_bias(relb_ref, bias_ref)
    _sb_fast_masks(mask_ref)
    j = lax.broadcasted_iota(jnp.int32, tri_ref.shape, 0)
    s = lax.broadcasted_iota(jnp.int32, tri_ref.shape, 1)
    tri_ref[...] = jnp.where(j > s, 1.0, 0.0).astype(BF16)


def _attention_constants(rel_bias):
    return pl.pallas_call(
        _constants_kernel,
        in_specs=[pl.BlockSpec(memory_space=pltpu.SMEM)],
        out_shape=[
            jax.ShapeDtypeStruct((SWA_HEADS, ATT_BLOCK, 2 * ATT_BLOCK), F32),
            jax.ShapeDtypeStruct((SB_WINDOW // SB_Q_TILE, 2, 2 * SB_Q_TILE, SB_WINDOW), F32),
            jax.ShapeDtypeStruct((SB_WINDOW, SB_WINDOW), BF16),
        ],
        name="attention_constants",
    )(rel_bias)


def _fast_kernel(tiles_per_seq, last_tile, sinks_ref, bias_ref, mask_ref, tri_ref,
                 qsw_ref, kvc_ref, kvp_ref, qsb_ref, kc_ref, kp_ref, vc_ref, vp_ref,
                 x_ref, ling_ref, linb_ref, sbg_ref, swg_ref, wo_ref, l1g_ref, l1b_ref,
                 wgu_ref, wd_ref, l2g_ref, l2b_ref, o_ref, live_ref, sw_ref, sb_ref):
    t = pl.program_id(0)

    @pl.when(t == 0)
    def _first_step():
        sw_ref[1] = jnp.zeros(sw_ref.shape[1:], sw_ref.dtype)
        sb_ref[1] = jnp.zeros(sb_ref.shape[1:], sb_ref.dtype)
        live_ref[...] = jnp.full(live_ref.shape, SB_LIVE_DONE, F32)

    no_previous = (jnp.minimum(t, last_tile) % tiles_per_seq) == 0
    lives = []
    swa = _swa_schedule(sinks_ref, qsw_ref.at[0], kvc_ref.at[0], kvp_ref.at[0], bias_ref,
                        sw_ref.at[t % 2], no_previous)
    sb = _sb_fast_schedule(qsb_ref.at[0], kc_ref.at[0], kp_ref.at[0], vc_ref.at[0], vp_ref.at[0],
                           tri_ref, mask_ref, sb_ref.at[t % 2], no_previous, lives)
    merged, out_proj, first_norm, ffn, second_norm = _tail_schedule(
        x_ref, ling_ref, linb_ref, sb_ref.at[(t + 1) % 2], sw_ref.at[(t + 1) % 2],
        sbg_ref, swg_ref, wo_ref, l1g_ref, l1b_ref, wgu_ref, wd_ref, l2g_ref, l2b_ref, o_ref)
    tail = [merged] + out_proj + [first_norm] + ffn + [second_norm]
    for thunk in _interleave(tail, _interleave(swa, sb)):
        thunk()
    live_ref[...] = jnp.minimum(live_ref[...], functools.reduce(jnp.minimum, lives))


def _fast_block(q_sw, kv_sw, q_sb, k_sb, v_sb, sinks, constants, x2, ln_in_g, ln_in_b,
                sb_g, sw_g, w_out_b, ln1_g, ln1_b, w_gate_up_b, w_down_b, ln2_g, ln2_b):
    b, s, wd = q_sw.shape
    n, d = x2.shape
    tiles_per_seq = s // TAIL_ROWS
    last_tile = n // TAIL_ROWS - 1
    att = lambda t: jnp.minimum(t, last_tile)
    cur = lambda t: (att(t) // tiles_per_seq, att(t) % tiles_per_seq, 0)

    def prev(rows):
        per_tile = TAIL_ROWS // rows
        return lambda t: (att(t) // tiles_per_seq,
                          jnp.maximum(att(t) % tiles_per_seq * per_tile - 1, 0), 0)

    row = lambda t: (jnp.maximum(t - 1, 0), 0)
    fixed = lambda t: (0, 0)
    smem = pl.BlockSpec(memory_space=pltpu.SMEM)
    vec = lambda a: pl.BlockSpec((1, a.shape[-1]), fixed)
    weight = lambda a: pl.BlockSpec(a.shape, fixed, pipeline_mode=pl.Buffered(1))
    tile3 = lambda a: pl.BlockSpec((1, TAIL_ROWS, a.shape[-1]), cur)
    before = lambda a, rows: pl.BlockSpec((1, rows, a.shape[-1]), prev(rows))
    steps = last_tile + 2
    constant = lambda a: pl.BlockSpec(a.shape, lambda t: (0,) * a.ndim,
                                      pipeline_mode=pl.Buffered(1))
    return pl.pallas_call(
        functools.partial(_fast_kernel, tiles_per_seq, last_tile),
        grid=(steps,),
        in_specs=[
            smem, *map(constant, constants),
            tile3(q_sw), tile3(kv_sw), before(kv_sw, ATT_BLOCK),
            tile3(q_sb), tile3(k_sb), before(k_sb, SB_WINDOW), tile3(v_sb), before(v_sb, SB_WINDOW),
            pl.BlockSpec((TAIL_ROWS, d), row), vec(ln_in_g), vec(ln_in_b),
            vec(sb_g), vec(sw_g), weight(w_out_b), vec(ln1_g), vec(ln1_b),
            weight(w_gate_up_b), weight(w_down_b), vec(ln2_g), vec(ln2_b),
        ],
        out_specs=[pl.BlockSpec((TAIL_ROWS, d), row),
                   pl.BlockSpec((8, LANES), fixed)],
        out_shape=[jax.ShapeDtypeStruct((n, d), F32),
                   jax.ShapeDtypeStruct((8, LANES), F32)],
        scratch_shapes=[
            pltpu.VMEM((2, TAIL_ROWS, wd), BF16),
            pltpu.VMEM((2, TAIL_ROWS, q_sb.shape[-1]), BF16),
        ],
        compiler_params=pltpu.CompilerParams(
            dimension_semantics=("arbitrary",), vmem_limit_bytes=VMEM_LIMIT),
        name="attention_and_dense_tail",
    )(sinks, *constants, q_sw, kv_sw, kv_sw, q_sb, k_sb, k_sb, v_sb, v_sb, x2, ln_in_g, ln_in_b,
      sb_g, sw_g, w_out_b, ln1_g, ln1_b, w_gate_up_b, w_down_b, ln2_g, ln2_b)


def kernel(x, ln_in_g, ln_in_b, w_in, sb_norm_g, swa_norm_g, sinks, rel_bias, w_out,
           ln1_g, ln1_b, w_gate_up, w_down, ln2_g, ln2_b):
    b, s, d = x.shape
    assert w_in.shape[0] == DEPTH == 1
    sb_w = SB_HEADS * HEAD_DIM
    sw_w = SWA_HEADS * HEAD_DIM
    kv_w = SWA_KV_HEADS * HEAD_DIM
    assert w_in.shape[-1] == 3 * sb_w + sw_w + 2 * kv_w
    assert kv_w == LANES and SWA_HEADS // SWA_KV_HEADS == 4
    assert s % SB_WINDOW == 0 and s % SB_STEP_ROWS == 0 and s % TAIL_ROWS == 0
    assert (b * s) % ROW_TILE == 0 and WINDOW <= ATT_BLOCK and d % MXU_WIDTH == 0
    assert FFN_CHUNK % MXU_WIDTH == 0 and w_down.shape[1] % MXU_WIDTH == 0

    x2 = x.reshape(b * s, d)
    row_vec = lambda a: a.reshape(1, -1)

    names = ("q_sb", "k_sb", "v_sb", "q_sw", "k_sw", "v_sw")
    widths = (sb_w, sb_w, sb_w, sw_w, kv_w, kv_w)
    splits, lo = {}, 0
    for name, wd in zip(names, widths):
        splits[name] = (lo, lo + wd)
        lo += wd
    ling, linb = row_vec(ln_in_g), row_vec(ln_in_b)
    q_sb, k_sb, v_sb, q_sw, kv_sw, w_out_b, w_gate_up_b, w_down_b = _in_projection(
        x2, ling, linb, w_in[0], splits, [w_out[0], w_gate_up[0], w_down[0]])

    to3 = lambda a: a.reshape(b, s, a.shape[-1])
    q_sb, k_sb, v_sb, q_sw, kv_sw = map(to3, (q_sb, k_sb, v_sb, q_sw, kv_sw))
    tail_args = (row_vec(sb_norm_g[0]), row_vec(swa_norm_g[0]), w_out_b,
                 row_vec(ln1_g[0]), row_vec(ln1_b[0]), w_gate_up_b, w_down_b,
                 row_vec(ln2_g[0]), row_vec(ln2_b[0]))

    fast, live = _fast_block(q_sw, kv_sw, q_sb, k_sb, v_sb, sinks[0],
                             _attention_constants(rel_bias), x2, ling, linb, *tail_args)

    def general(_):
        (sb_out,) = _sb_attention(q_sb, k_sb, v_sb, [])
        return _attn_tail_block(q_sw, kv_sw, sinks[0], rel_bias, x2, ling, linb,
                                sb_out.reshape(b * s, sb_w), *tail_args)

    out = lax.cond(live[0, 0] < SB_DEAD_LOG2, general, lambda _: fast, None)
    return out.reshape(b, s, d)
```

```python
import functools
import math

import jax
import jax.numpy as jnp
import numpy as np
from jax import lax
from jax.experimental import pallas as pl
from jax.experimental.pallas import tpu as pltpu

F32 = jnp.float32
BF16 = jnp.bfloat16

HEAD_DIM = 64
SB_HEADS = 8
SWA_HEADS = 8
SWA_KV_HEADS = 2
WINDOW = 128
REL_BUCKETS = 32
REL_MAX_DIST = 128
LN_EPS = 1e-5
RMS_EPS = 1e-6
DEPTH = 1
ALPHA = (2 * DEPTH) ** 0.25
QK_SCALE = HEAD_DIM ** -0.5
LOG2E = math.log2(math.e)

LANES = 128
BF16_ROWS = 16
MXU_WIDTH = 256
ATT_BLOCK = 128
SB_WINDOW = 256
SB_Q_TILE = 64
SB_STEP_ROWS = 512
ROW_TILE = 1024
PROJ_PARTS = 2
TAIL_ROWS = 512
FFN_CHUNK = 6 * MXU_WIDTH
VMEM_LIMIT = 56 * 1024 * 1024

SB_DEAD_LOG2 = 127.0
MASK_NEG = -1e30


def _layer_norm(x, g, b):
    mu = jnp.mean(x, axis=-1, keepdims=True)
    xc = x - mu
    var = jnp.mean(xc * xc, axis=-1, keepdims=True)
    return xc * lax.rsqrt(var + LN_EPS) * g + b


def _rms_norm(x, g):
    ms = jnp.mean(x * x, axis=-1, keepdims=True)
    return x * lax.rsqrt(ms + RMS_EPS) * g


def _dot(a, b):
    return jnp.dot(a, b, preferred_element_type=F32)


def _dot_nt(a, b):
    return lax.dot_general(a, b, (((1,), (1,)), ((), ())), preferred_element_type=F32)


def _pipeline_schedule(n, stages):
    vals = [dict() for _ in stages]
    thunks = []
    for step in range(n + len(stages) - 1):
        for s, stage in enumerate(stages):
            i = step - s
            if 0 <= i < n:
                def thunk(s=s, i=i, stage=stage):
                    vals[s][i] = stage(i, vals[s - 1].pop(i)) if s else stage(i)
                thunks.append(thunk)
    return thunks, vals[-1]


def _software_pipeline(n, stages):
    thunks, results = _pipeline_schedule(n, stages)
    for thunk in thunks:
        thunk()
    return results


def _interleave(xs, ys):
    merged, taken = [], 0
    for i, x in enumerate(xs):
        merged.append(x)
        upto = (i + 1) * len(ys) // len(xs)
        merged.extend(ys[taken:upto])
        taken = upto
    return merged


def _inproj_kernel(splits, n_later, x_ref, g_ref, b_ref, w32_ref, *refs):
    later_in, refs = refs[:n_later], refs[n_later:]
    ha_ref, qsb_ref, ksb_ref, vsb_ref, qsw_ref, kv_ref = refs[:6]
    later_out, w_ref = refs[6:6 + n_later], refs[6 + n_later]
    rows = x_ref.shape[0] // PROJ_PARTS
    part = lambda i: slice(i * rows, (i + 1) * rows)
    assert splits["k_sw"][1] == splits["v_sw"][0]

    @pl.when(pl.program_id(0) == 0)
    def _convert_weight():
        w_ref[...] = w32_ref[...].astype(BF16)

    for src, dst in zip(later_in, later_out):
        dst[...] = src[...].astype(BF16)

    def normed(i):
        h32 = _layer_norm(x_ref[part(i), :], g_ref[...], b_ref[...])
        ha_ref[part(i), :] = ALPHA * h32
        return h32.astype(BF16)

    def projected(i, h):
        r = part(i)
        proj = lambda lo, hi: _dot(h, w_ref[:, lo:hi])
        qsb_ref[r, :] = (proj(*splits["q_sb"]) * QK_SCALE).astype(BF16)
        ksb_ref[r, :] = proj(*splits["k_sb"]).astype(BF16)
        vsb_ref[r, :] = proj(*splits["v_sb"]).astype(BF16)
        qsw_ref[r, :] = (proj(*splits["q_sw"]) * QK_SCALE).astype(BF16)
        kv = proj(splits["k_sw"][0], splits["v_sw"][1])
        kvw = kv.shape[-1] // 2
        k, v = kv[:, :kvw], kv[:, kvw:]
        kv_ref[r, 0 * kvw:1 * kvw] = k.astype(BF16)
        kv_ref[r, 1 * kvw:2 * kvw] = pltpu.roll(k, HEAD_DIM, axis=1).astype(BF16)
        kv_ref[r, 2 * kvw:3 * kvw] = v.astype(BF16)
        kv_ref[r, 3 * kvw:4 * kvw] = pltpu.roll(v, HEAD_DIM, axis=1).astype(BF16)

    _software_pipeline(PROJ_PARTS, [normed, projected])


def _in_projection(x2, g, b, w_in, splits, later_weights):
    n, d = x2.shape
    steps = n // ROW_TILE
    width = lambda name: splits[name][1] - splits[name][0]
    out_w = [width("q_sb"), width("k_sb"), width("v_sb"), width("q_sw"), 4 * width("k_sw")]
    row = lambda i: (i, 0)
    fixed = lambda i: (0, 0)
    later_specs = [
        pl.BlockSpec((size, w.shape[1]),
                     lambda i, last=w.shape[0] // size - 1: (jnp.minimum(i, last), 0))
        for w, size in zip(later_weights, _conversion_blocks(later_weights, steps))]
    return pl.pallas_call(
        functools.partial(_inproj_kernel, splits, len(later_weights)),
        grid=(steps,),
        in_specs=[
            pl.BlockSpec((ROW_TILE, d), row),
            pl.BlockSpec((1, d), fixed),
            pl.BlockSpec((1, d), fixed),
            pl.BlockSpec(w_in.shape, fixed, pipeline_mode=pl.Buffered(1)),
        ] + later_specs,
        out_specs=[pl.BlockSpec((ROW_TILE, d), row)]
        + [pl.BlockSpec((ROW_TILE, w), row) for w in out_w] + later_specs,
        out_shape=[jax.ShapeDtypeStruct((n, d), F32)]
        + [jax.ShapeDtypeStruct((n, w), BF16) for w in out_w]
        + [jax.ShapeDtypeStruct(w.shape, BF16) for w in later_weights],
        scratch_shapes=[pltpu.VMEM(w_in.shape, BF16)],
        compiler_params=pltpu.CompilerParams(
            dimension_semantics=("arbitrary",), vmem_limit_bytes=VMEM_LIMIT),
        name="ln_in_proj",
    )(x2, g, b, w_in, *later_weights)


def _conversion_blocks(weights, steps):
    sizes = []
    for w in weights:
        rows = w.shape[0]
        sizes.append(next(r for r in range(BF16_ROWS, rows + 1, BF16_ROWS)
                          if rows % r == 0 and rows // r <= steps))
    return sizes


def _split_heads(x, lane):
    zero = jnp.zeros_like(x)
    return jnp.concatenate([jnp.where(lane < HEAD_DIM, x, zero),
                            jnp.where(lane >= HEAD_DIM, x, zero)], axis=0)


def _merge_heads(o2, lane):
    t = o2.shape[0] // 2
    return jnp.where(lane < HEAD_DIM, o2[:t], o2[t:])


def _sb_masks(valid):
    return jnp.where(valid, LOG2E, 0.0), jnp.where(valid, 0.0, MASK_NEG)


def _sb_scores(qq, k, keep):
    z = _dot_nt(qq, k)
    zl = z * LOG2E
    c_nat = jnp.maximum(z, 0.0) + jnp.log(1.0 + jnp.exp2(-jnp.abs(zl)))
    cm = c_nat * keep
    return zl - cm, cm.astype(BF16), cm[:, :LANES]


def _sb_weights(scores, tri, kills, carrieds):
    sums = _dot(jnp.concatenate([sc[1] for sc in scores], axis=0), tri)
    rows = scores[0][0].shape[0]
    out = []
    for n, ((log_beta, _, cm0), kill, carried) in enumerate(zip(scores, kills, carrieds)):
        later = sums[n * rows:(n + 1) * rows]
        log_a = (log_beta - later) + kill
        if carried is not None:
            log_a = log_a - jnp.concatenate([carried] * (log_a.shape[1] // LANES), axis=1)
        out.append((jnp.exp2(log_a).astype(BF16), later[:, :LANES] + cm0))
    return out


def _sb_kernel(n_conv, q_ref, k_ref, v_ref, *refs):
    conv_in, o_ref, conv_out = refs[:n_conv], refs[n_conv], refs[n_conv + 1:2 * n_conv + 1]
    tri_ref, mask_ref, acc_ref, car_ref = refs[2 * n_conv + 1:]
    for src, dst in zip(conv_in, conv_out):
        dst[...] = src[...].astype(BF16)

    bi = pl.program_id(0)
    qi = pl.program_id(1)
    tq, w = SB_Q_TILE, SB_WINDOW
    subs = q_ref.shape[1] // tq
    pairs = q_ref.shape[-1] // LANES
    row0 = qi * (subs * tq)

    @pl.when(jnp.logical_and(bi == 0, qi == 0))
    def _fill_tri():
        j = lax.broadcasted_iota(jnp.int32, (w, w), 0)
        s = lax.broadcasted_iota(jnp.int32, (w, w), 1)
        tri_ref[...] = jnp.where(j > s, 1.0, 0.0).astype(BF16)

    lane = lax.broadcasted_iota(jnp.int32, (tq, LANES), 1)
    lanes_of = lambda p: slice(p * LANES, (p + 1) * LANES)
    rows_of = lambda h: slice(h * tq, (h + 1) * tq)
    first_start = lambda h: pl.multiple_of(jnp.maximum(row0 + (h + 1) * tq - w, 0), tq)

    def key_positions(start):
        return start + lax.broadcasted_iota(jnp.int32, (2 * tq, w), 1)

    assert subs * tq >= w - tq

    @pl.when(qi <= 1)
    def _fill_masks():
        row = lax.broadcasted_iota(jnp.int32, (2 * tq, w), 0) & (tq - 1)
        for h in range(subs):
            mask_ref[h, 0], mask_ref[h, 1] = _sb_masks(
                key_positions(first_start(h)) < row0 + h * tq + row)

    def scores(p, starts, keeps):
        return [_sb_scores(_split_heads(q_ref[0, rows_of(h), lanes_of(p)], lane),
                           k_ref[0, pl.ds(starts[h], w), lanes_of(p)], keeps[h])
                for h in range(subs)]

    first_starts = [first_start(h) for h in range(subs)]

    def first_scores(p):
        return scores(p, first_starts, [mask_ref[h, 0] for h in range(subs)])

    def first_weights(p, sc):
        return _sb_weights(sc, tri_ref[...], [mask_ref[h, 1] for h in range(subs)], [None] * subs)

    def first_values(p, aws):
        least = None
        for h, (a, total) in enumerate(aws):
            o2 = _dot(a, v_ref[0, pl.ds(first_starts[h], w), lanes_of(p)])
            o_ref[0, rows_of(h), lanes_of(p)] = _merge_heads(o2, lane).astype(o_ref.dtype)
            acc_ref[p, h] = o2
            car_ref[p, h] = total
            m = jnp.min(total[:, 0:1])
            least = m if least is None else jnp.minimum(least, m)
        return least

    first = _software_pipeline(pairs, [first_scores, first_weights, first_values])
    live = functools.reduce(jnp.minimum, [first[p] for p in range(pairs)])
    last_start = first_starts[-1]

    @pl.when(jnp.logical_and(last_start > 0, live < SB_DEAD_LOG2))
    def _walk_back():
        for p in range(pairs):
            for h in range(subs):
                car_ref[p, h] = jnp.broadcast_to(car_ref[p, h][:, 0:1], (2 * tq, LANES))

        def body(state):
            step, _ = state
            done = [jnp.maximum(first_start(h) - step * w, 0) for h in range(subs)]
            starts = [pl.multiple_of(jnp.maximum(d - w, 0), tq) for d in done]
            masks = [_sb_masks(key_positions(s) < d) for s, d in zip(starts, done)]

            def more_scores(p):
                return scores(p, starts, [m[0] for m in masks])

            def weights(p, sc):
                return _sb_weights(sc, tri_ref[...], [m[1] for m in masks],
                                   [car_ref[p, h] for h in range(subs)])

            def values(p, aws):
                least = None
                for h, (a, total) in enumerate(aws):
                    acc_ref[p, h] += _dot(a, v_ref[0, pl.ds(starts[h], w), lanes_of(p)])
                    carried = car_ref[p, h] + total[:, 0:1]
                    car_ref[p, h] = carried
                    m = jnp.min(carried)
                    least = m if least is None else jnp.minimum(least, m)
                return least

            mins = _software_pipeline(pairs, [more_scores, weights, values])
            return step + 1, functools.reduce(jnp.minimum, [mins[p] for p in range(pairs)])

        def cond(state):
            step, nxt = state
            return jnp.logical_and(last_start - step * w > 0, nxt < SB_DEAD_LOG2)

        lax.while_loop(cond, body, (jnp.int32(0), live))
        for p in range(pairs):
            for h in range(subs):
                o_ref[0, rows_of(h), lanes_of(p)] = _merge_heads(
                    acc_ref[p, h], lane).astype(o_ref.dtype)


def _sb_attention(q, k, v, weights):
    b, s, wd = q.shape
    pairs = wd // LANES
    subs = SB_STEP_ROWS // SB_Q_TILE
    nq = s // SB_STEP_ROWS
    tile = lambda bi, i: (bi, i, 0)
    whole = lambda bi, i: (bi, 0, 0)
    once = pl.Buffered(1)
    conv_specs = [
        pl.BlockSpec((size, w.shape[1]),
                     lambda bi, i, last=w.shape[0] // size - 1: (jnp.minimum(bi * nq + i, last), 0))
        for w, size in zip(weights, _conversion_blocks(weights, b * nq))]
    return pl.pallas_call(
        functools.partial(_sb_kernel, len(weights)),
        grid=(b, nq),
        in_specs=[
            pl.BlockSpec((1, SB_STEP_ROWS, wd), tile),
            pl.BlockSpec((1, s, wd), whole, pipeline_mode=once),
            pl.BlockSpec((1, s, wd), whole, pipeline_mode=once),
        ] + conv_specs,
        out_specs=[pl.BlockSpec((1, SB_STEP_ROWS, wd), tile)] + conv_specs,
        out_shape=[jax.ShapeDtypeStruct((b, s, wd), BF16)]
        + [jax.ShapeDtypeStruct(w.shape, BF16) for w in weights],
        scratch_shapes=[
            pltpu.VMEM((SB_WINDOW, SB_WINDOW), BF16),
            pltpu.VMEM((subs, 2, 2 * SB_Q_TILE, SB_WINDOW), F32),
            pltpu.VMEM((pairs, subs, 2 * SB_Q_TILE, LANES), F32),
            pltpu.VMEM((pairs, subs, 2 * SB_Q_TILE, LANES), F32),
        ],
        compiler_params=pltpu.CompilerParams(
            dimension_semantics=("arbitrary", "arbitrary"), vmem_limit_bytes=VMEM_LIMIT),
        name="stick_breaking_attention",
    )(q, k, v, *weights)


def _t5_log_bucket_starts():
    exact = REL_BUCKETS // 2
    d = np.arange(exact, 2 * REL_MAX_DIST, dtype=np.float32)
    large = exact + (np.log(d / np.float32(exact)) / np.float32(math.log(REL_MAX_DIST / exact))
                     * np.float32(REL_BUCKETS - exact)).astype(np.int32)
    large = np.minimum(large, REL_BUCKETS - 1)
    assert large[0] == exact and np.all(np.diff(large) >= 0)
    return [int(d[np.argmax(large >= bk)]) for bk in range(exact + 1, REL_BUCKETS)]


def _t5_causal_bucket(distance):
    d = jnp.maximum(distance, 0)
    bucket = jnp.minimum(d, REL_BUCKETS // 2)
    for start in _t5_log_bucket_starts():
        bucket = bucket + jnp.where(d >= start, 1, 0)
    return bucket


def _swa_fill_bias(relb_ref, bias_ref):
    t = ATT_BLOCK
    r = lax.broadcasted_iota(jnp.int32, (t, 2 * t), 0)
    c = lax.broadcasted_iota(jnp.int32, (t, 2 * t), 1)
    dist = r + t - c
    bucket = _t5_causal_bucket(dist)
    valid = jnp.logical_and(dist >= 0, dist < WINDOW)
    for h in range(SWA_HEADS):
        bias = jnp.zeros((t, 2 * t), F32)
        for bk in range(REL_BUCKETS):
            bias = jnp.where(bucket == bk, relb_ref[bk, h], bias)
        bias_ref[h] = jnp.where(valid, bias, MASK_NEG)


def _swa_schedule(sinks_ref, q_ref, kvc_ref, kvp_ref, bias_ref, o_ref, no_previous):
    t = ATT_BLOCK
    kvw = kvc_ref.shape[-1] // 4
    group = SWA_HEADS // SWA_KV_HEADS
    tiles = q_ref.shape[0] // t
    rows_of = lambda n: slice(n * t, (n + 1) * t)
    lane = lax.broadcasted_iota(jnp.int32, (t, LANES), 1)
    row2 = lax.broadcasted_iota(jnp.int32, (2 * t, 1), 0)
    col2 = lax.broadcasted_iota(jnp.int32, (2 * t, 2 * t), 1)
    prev_pen = jnp.where(jnp.logical_and(col2 < t, no_previous), MASK_NEG, 0.0)

    def kv_window(n, variant):
        cols = slice(variant * kvw, (variant + 1) * kvw)
        before = kvc_ref[rows_of(n - 1), cols] if n else kvp_ref[:, cols]
        return jnp.concatenate([before, kvc_ref[rows_of(n), cols]], axis=0)

    items = [(n, half, j) for n in range(tiles) for half in range(2) for j in range(SWA_KV_HEADS)]
    heads_of = lambda half, j: [h for h in range(j * group, (j + 1) * group) if h % 2 == half]

    def probabilities(i):
        n, half, j = items[i]
        heads = heads_of(half, j)
        in_half = (lane >= HEAD_DIM) if half else (lane < HEAD_DIM)
        qq = jnp.concatenate(
            [jnp.where(in_half, q_ref[rows_of(n), (h // 2) * LANES:(h // 2 + 1) * LANES], 0)
             for h in heads], axis=0)
        s = _dot_nt(qq, kv_window(n, 0 + int(j != half)))
        s = s + jnp.concatenate([bias_ref[h] for h in heads], axis=0)
        if n == 0:
            s = s + prev_pen
        sink = jnp.where(row2 < t, sinks_ref[heads[0]], sinks_ref[heads[1]])
        m = jnp.maximum(jnp.max(s, axis=-1, keepdims=True), sink)
        p = jnp.exp(s - m)
        denom = jnp.sum(p, axis=-1, keepdims=True) + jnp.exp(sink - m)
        return p.astype(BF16), 1.0 / denom

    def values(i, pd):
        n, half, j = items[i]
        p, inv_denom = pd
        return _dot(p, kv_window(n, 2 + int(j != half))) * inv_denom

    thunks, outs = _pipeline_schedule(len(items), [probabilities, values])

    def store():
        for i, (n, half, j) in enumerate(items):
            if half == 0:
                even, odd = outs[i], outs[items.index((n, 1, j))]
                for e, h in enumerate(heads_of(0, j)):
                    o_ref[rows_of(n), (h // 2) * LANES:(h // 2 + 1) * LANES] = jnp.where(
                        lane < HEAD_DIM, even[e * t:(e + 1) * t], odd[e * t:(e + 1) * t]
                    ).astype(o_ref.dtype)

    return thunks + [store]


def _tail_schedule(ha_ref, sb_ref, sw_ref, sbg_ref, swg_ref, wo_ref, l1g_ref, l1b_ref,
                   wgu_ref, wd_ref, l2g_ref, l2b_ref, o_ref):
    d = ha_ref.shape[1]
    d_ff = wd_ref.shape[0]
    mw = MXU_WIDTH
    bounds = list(range(0, d_ff, FFN_CHUNK)) + [d_ff]
    n_chunks = len(bounds) - 1
    out_tiles = range(d // mw)
    st = {}

    def merged():
        sb = _rms_norm(sb_ref[...].astype(F32), sbg_ref[...]).astype(BF16)
        sw = _rms_norm(sw_ref[...].astype(F32), swg_ref[...]).astype(BF16)
        st["merged"] = jnp.concatenate([sb, sw], axis=1)

    def out_projection(n):
        st["mix", n] = _dot(st["merged"], wo_ref[:, n * mw:(n + 1) * mw])

    def first_norm():
        mix = jnp.concatenate([st.pop(("mix", n)) for n in out_tiles], axis=1)
        st.pop("merged")
        h1 = _layer_norm(ha_ref[...] + mix, l1g_ref[...], l1b_ref[...])
        st["h1b"] = h1.astype(BF16)
        for n in out_tiles:
            st["acc", n] = ALPHA * h1[:, n * mw:(n + 1) * mw]

    def activation(c, n):
        lo = bounds[c] + n * mw
        gate = _dot(st["h1b"], wgu_ref[:, lo:lo + mw])
        up = _dot(st["h1b"], wgu_ref[:, d_ff + lo:d_ff + lo + mw])
        st["act", c, n] = (gate / (1.0 + jnp.exp(-gate)) * up).astype(BF16)

    def down(c, n):
        tiles = range((bounds[c + 1] - bounds[c]) // mw)
        if n == 0:
            st["act", c] = jnp.concatenate([st.pop(("act", c, k)) for k in tiles], axis=1)
        st["acc", n] = st["acc", n] + _dot(st["act", c], wd_ref[bounds[c]:bounds[c + 1],
                                                               n * mw:(n + 1) * mw])

    def second_norm():
        y = jnp.concatenate([st.pop(("acc", n)) for n in out_tiles], axis=1)
        o_ref[...] = _layer_norm(y, l2g_ref[...], l2b_ref[...])

    act_thunks = [[functools.partial(activation, c, n)
                   for n in range((bounds[c + 1] - bounds[c]) // mw)] for c in range(n_chunks)]
    down_thunks = [[functools.partial(down, c, n) for n in out_tiles] for c in range(n_chunks)]
    ffn = list(act_thunks[0])
    for c in range(1, n_chunks):
        ffn += _interleave(act_thunks[c], down_thunks[c - 1])
    ffn += down_thunks[-1]
    return (merged, [functools.partial(out_projection, n) for n in out_tiles], first_norm,
            ffn, second_norm)


def _attn_tail_kernel(tiles_per_seq, last_tile, sinks_ref, relb_ref, q_ref, kvc_ref, kvp_ref,
                      ha_ref, sb_ref, sbg_ref, swg_ref, wo_ref, l1g_ref, l1b_ref,
                      wgu_ref, wd_ref, l2g_ref, l2b_ref, o_ref, bias_ref, sw_ref):
    t = pl.program_id(0)

    @pl.when(t == 0)
    def _first_step():
        _swa_fill_bias(relb_ref, bias_ref)
        sw_ref[1] = jnp.zeros(sw_ref.shape[1:], sw_ref.dtype)

    no_previous = (jnp.minimum(t, last_tile) % tiles_per_seq) == 0
    attention = _swa_schedule(sinks_ref, q_ref.at[0], kvc_ref.at[0], kvp_ref.at[0], bias_ref,
                              sw_ref.at[t % 2], no_previous)
    merged, out_proj, first_norm, ffn, second_norm = _tail_schedule(
        ha_ref, sb_ref, sw_ref.at[(t + 1) % 2], sbg_ref, swg_ref, wo_ref,
        l1g_ref, l1b_ref, wgu_ref, wd_ref, l2g_ref, l2b_ref, o_ref)
    tail = [merged] + out_proj + [first_norm] + ffn + [second_norm]
    for thunk in _interleave(tail, attention):
        thunk()


def _attn_tail_block(q_sw, kv_sw, sinks, rel_bias, ha, sb2, sb_g, sw_g, w_out_b, ln1_g, ln1_b,
                     w_gate_up_b, w_down_b, ln2_g, ln2_b):
    b, s, wd = q_sw.shape
    n, d = ha.shape
    tiles_per_seq = s // TAIL_ROWS
    last_tile = n // TAIL_ROWS - 1
    key_tiles = TAIL_ROWS // ATT_BLOCK
    att = lambda t: jnp.minimum(t, last_tile)
    cur = lambda t: (att(t) // tiles_per_seq, att(t) % tiles_per_seq, 0)
    prev = lambda t: (att(t) // tiles_per_seq,
                      jnp.maximum(att(t) % tiles_per_seq * key_tiles - 1, 0), 0)
    row = lambda t: (jnp.maximum(t - 1, 0), 0)
    fixed = lambda t: (0, 0)
    smem = pl.BlockSpec(memory_space=pltpu.SMEM)
    vec = lambda a: pl.BlockSpec((1, a.shape[-1]), fixed)
    weight = lambda a: pl.BlockSpec(a.shape, fixed, pipeline_mode=pl.Buffered(1))
    return pl.pallas_call(
        functools.partial(_attn_tail_kernel, tiles_per_seq, last_tile),
        grid=(last_tile + 2,),
        in_specs=[
            smem, smem,
            pl.BlockSpec((1, TAIL_ROWS, wd), cur),
            pl.BlockSpec((1, TAIL_ROWS, kv_sw.shape[-1]), cur),
            pl.BlockSpec((1, ATT_BLOCK, kv_sw.shape[-1]), prev),
            pl.BlockSpec((TAIL_ROWS, d), row),
            pl.BlockSpec((TAIL_ROWS, sb2.shape[-1]), row),
            vec(sb_g), vec(sw_g), weight(w_out_b), vec(ln1_g), vec(ln1_b),
            weight(w_gate_up_b), weight(w_down_b), vec(ln2_g), vec(ln2_b),
        ],
        out_specs=pl.BlockSpec((TAIL_ROWS, d), row),
        out_shape=jax.ShapeDtypeStruct((n, d), F32),
        scratch_shapes=[
            pltpu.VMEM((SWA_HEADS, ATT_BLOCK, 2 * ATT_BLOCK), F32),
            pltpu.VMEM((2, TAIL_ROWS, wd), BF16),
        ],
        compiler_params=pltpu.CompilerParams(
            dimension_semantics=("arbitrary",), vmem_limit_bytes=VMEM_LIMIT),
        name="swa_attention_and_dense_tail",
    )(sinks, rel_bias, q_sw, kv_sw, kv_sw, ha, sb2, sb_g, sw_g, w_out_b, ln1_g, ln1_b,
      w_gate_up_b, w_down_b, ln2_g, ln2_b)


SB_GROUP = 8
SB_LIVE_DONE = 1e30


def _sb_fast_masks(mask_ref):
    tq, w = SB_Q_TILE, SB_WINDOW
    row = lax.broadcasted_iota(jnp.int32, (2 * tq, w), 0) & (tq - 1)
    col = lax.broadcasted_iota(jnp.int32, (2 * tq, w), 1)
    causal = col < row + (w - tq)
    mask_ref[0, 0], mask_ref[0, 1] = _sb_masks(causal)
    for h in range(mask_ref.shape[0] - 1):
        first_key = w - (h + 1) * tq
        mask_ref[1 + h, 0], mask_ref[1 + h, 1] = _sb_masks(
            jnp.logical_and(causal, col >= first_key))


def _sb_fast_schedule(q_ref, kc_ref, kp_ref, vc_ref, vp_ref, tri_ref, mask_ref, o_ref,
                      no_previous, lives):
    tq, w = SB_Q_TILE, SB_WINDOW
    subs = q_ref.shape[0] // tq
    pairs = q_ref.shape[-1] // LANES
    clamped = w // tq - 1
    assert mask_ref.shape[0] == clamped + 1 and subs % SB_GROUP == 0
    lane = lax.broadcasted_iota(jnp.int32, (tq, LANES), 1)
    lanes_of = lambda p: slice(p * LANES, (p + 1) * LANES)
    rows_of = lambda h: slice(h * tq, (h + 1) * tq)

    def window(h, cur_ref, prev_ref, p):
        start = (h + 1) * tq - w
        if start >= 0:
            return cur_ref[start:start + w, lanes_of(p)]
        return jnp.concatenate([prev_ref[w + start:w, lanes_of(p)],
                                cur_ref[0:w + start, lanes_of(p)]], axis=0)

    def mask(h, which):
        if h >= clamped:
            return mask_ref[0, which]
        return mask_ref[jnp.where(no_previous, 1 + h, 0), which]

    items = [(p, g) for p in range(pairs) for g in range(subs // SB_GROUP)]
    group = lambda g: range(g * SB_GROUP, (g + 1) * SB_GROUP)

    def scores(i):
        p, g = items[i]
        return [_sb_scores(_split_heads(q_ref[rows_of(h), lanes_of(p)], lane),
                           window(h, kc_ref, kp_ref, p), mask(h, 0)) for h in group(g)]

    def weights(i, sc):
        return _sb_weights(sc, tri_ref[...], [mask(h, 1) for h in group(items[i][1])],
                           [None] * SB_GROUP)

    def values(i, aws):
        p, g = items[i]
        for h, (a, total) in zip(group(g), aws):
            o2 = _dot(a, window(h, vc_ref, vp_ref, p))
            o_ref[rows_of(h), lanes_of(p)] = _merge_heads(o2, lane).astype(o_ref.dtype)
            m = jnp.min(total[:, 0:1])
            lives.append(m if h > clamped else jnp.where(no_previous, SB_LIVE_DONE, m))

    return _pipeline_schedule(len(items), [scores, weights, values])[0]


def _fast_kernel(tiles_per_seq, last_tile, sinks_ref, relb_ref, qsw_ref, kvc_ref, kvp_ref,
                 qsb_ref, kc_ref, kp_ref, vc_ref, vp_ref,
                 ha_ref, sbg_ref, swg_ref, wo_ref, l1g_ref, l1b_ref,
                 wgu_ref, wd_ref, l2g_ref, l2b_ref, o_ref, live_ref,
                 bias_ref, sw_ref, sb_ref, tri_ref, mask_ref):
    t = pl.program_id(0)

    @pl.when(t == 0)
    def _first_step():
        _swa_fill_bias(relb_ref, bias_ref)
        _sb_fast_masks(mask_ref)
        j = lax.broadcasted_iota(jnp.int32, tri_ref.shape, 0)
        s = lax.broadcasted_iota(jnp.int32, tri_ref.shape, 1)
        tri_ref[...] = jnp.where(j > s, 1.0, 0.0).astype(BF16)
        sw_ref[1] = jnp.zeros(sw_ref.shape[1:], sw_ref.dtype)
        sb_ref[1] = jnp.zeros(sb_ref.shape[1:], sb_ref.dtype)

    no_previous = (jnp.minimum(t, last_tile) % tiles_per_seq) == 0
    lives = []
    swa = _swa_schedule(sinks_ref, qsw_ref.at[0], kvc_ref.at[0], kvp_ref.at[0], bias_ref,
                        sw_ref.at[t % 2], no_previous)
    sb = _sb_fast_schedule(qsb_ref.at[0], kc_ref.at[0], kp_ref.at[0], vc_ref.at[0], vp_ref.at[0],
                           tri_ref, mask_ref, sb_ref.at[t % 2], no_previous, lives)
    merged, out_proj, first_norm, ffn, second_norm = _tail_schedule(
        ha_ref, sb_ref.at[(t + 1) % 2], sw_ref.at[(t + 1) % 2], sbg_ref, swg_ref, wo_ref,
        l1g_ref, l1b_ref, wgu_ref, wd_ref, l2g_ref, l2b_ref, o_ref)
    tail = [merged] + out_proj + [first_norm] + ffn + [second_norm]
    for thunk in _interleave(tail, _interleave(swa, sb)):
        thunk()
    live_ref[...] = jnp.full(live_ref.shape, functools.reduce(jnp.minimum, lives), F32)


def _fast_block(q_sw, kv_sw, q_sb, k_sb, v_sb, sinks, rel_bias, ha, sb_g, sw_g, w_out_b,
                ln1_g, ln1_b, w_gate_up_b, w_down_b, ln2_g, ln2_b):
    b, s, wd = q_sw.shape
    n, d = ha.shape
    tiles_per_seq = s // TAIL_ROWS
    last_tile = n // TAIL_ROWS - 1
    att = lambda t: jnp.minimum(t, last_tile)
    cur = lambda t: (att(t) // tiles_per_seq, att(t) % tiles_per_seq, 0)

    def prev(rows):
        per_tile = TAIL_ROWS // rows
        return lambda t: (att(t) // tiles_per_seq,
                          jnp.maximum(att(t) % tiles_per_seq * per_tile - 1, 0), 0)

    row = lambda t: (jnp.maximum(t - 1, 0), 0)
    fixed = lambda t: (0, 0)
    smem = pl.BlockSpec(memory_space=pltpu.SMEM)
    vec = lambda a: pl.BlockSpec((1, a.shape[-1]), fixed)
    weight = lambda a: pl.BlockSpec(a.shape, fixed, pipeline_mode=pl.Buffered(1))
    tile3 = lambda a: pl.BlockSpec((1, TAIL_ROWS, a.shape[-1]), cur)
    before = lambda a, rows: pl.BlockSpec((1, rows, a.shape[-1]), prev(rows))
    steps = last_tile + 2
    return pl.pallas_call(
        functools.partial(_fast_kernel, tiles_per_seq, last_tile),
        grid=(steps,),
        in_specs=[
            smem, smem,
            tile3(q_sw), tile3(kv_sw), before(kv_sw, ATT_BLOCK),
            tile3(q_sb), tile3(k_sb), before(k_sb, SB_WINDOW), tile3(v_sb), before(v_sb, SB_WINDOW),
            pl.BlockSpec((TAIL_ROWS, d), row),
            vec(sb_g), vec(sw_g), weight(w_out_b), vec(ln1_g), vec(ln1_b),
            weight(w_gate_up_b), weight(w_down_b), vec(ln2_g), vec(ln2_b),
        ],
        out_specs=[pl.BlockSpec((TAIL_ROWS, d), row),
                   pl.BlockSpec((1, 8, LANES), lambda t: (t, 0, 0))],
        out_shape=[jax.ShapeDtypeStruct((n, d), F32),
                   jax.ShapeDtypeStruct((steps, 8, LANES), F32)],
        scratch_shapes=[
            pltpu.VMEM((SWA_HEADS, ATT_BLOCK, 2 * ATT_BLOCK), F32),
            pltpu.VMEM((2, TAIL_ROWS, wd), BF16),
            pltpu.VMEM((2, TAIL_ROWS, q_sb.shape[-1]), BF16),
            pltpu.VMEM((SB_WINDOW, SB_WINDOW), BF16),
            pltpu.VMEM((SB_WINDOW // SB_Q_TILE, 2, 2 * SB_Q_TILE, SB_WINDOW), F32),
        ],
        compiler_params=pltpu.CompilerParams(
            dimension_semantics=("arbitrary",), vmem_limit_bytes=VMEM_LIMIT),
        name="attention_and_dense_tail",
    )(sinks, rel_bias, q_sw, kv_sw, kv_sw, q_sb, k_sb, k_sb, v_sb, v_sb, ha, sb_g, sw_g, w_out_b,
      ln1_g, ln1_b, w_gate_up_b, w_down_b, ln2_g, ln2_b)


def kernel(x, ln_in_g, ln_in_b, w_in, sb_norm_g, swa_norm_g, sinks, rel_bias, w_out,
           ln1_g, ln1_b, w_gate_up, w_down, ln2_g, ln2_b):
    b, s, d = x.shape
    assert w_in.shape[0] == DEPTH == 1
    sb_w = SB_HEADS * HEAD_DIM
    sw_w = SWA_HEADS * HEAD_DIM
    kv_w = SWA_KV_HEADS * HEAD_DIM
    assert w_in.shape[-1] == 3 * sb_w + sw_w + 2 * kv_w
    assert kv_w == LANES and SWA_HEADS // SWA_KV_HEADS == 4
    assert s % SB_WINDOW == 0 and s % SB_STEP_ROWS == 0 and s % TAIL_ROWS == 0
    assert (b * s) % ROW_TILE == 0 and WINDOW <= ATT_BLOCK and d % MXU_WIDTH == 0
    assert FFN_CHUNK % MXU_WIDTH == 0 and w_down.shape[1] % MXU_WIDTH == 0

    x2 = x.reshape(b * s, d)
    row_vec = lambda a: a.reshape(1, -1)

    names = ("q_sb", "k_sb", "v_sb", "q_sw", "k_sw", "v_sw")
    widths = (sb_w, sb_w, sb_w, sw_w, kv_w, kv_w)
    splits, lo = {}, 0
    for name, wd in zip(names, widths):
        splits[name] = (lo, lo + wd)
        lo += wd
    ha, q_sb, k_sb, v_sb, q_sw, kv_sw, w_out_b, w_gate_up_b, w_down_b = _in_projection(
        x2, row_vec(ln_in_g), row_vec(ln_in_b), w_in[0], splits,
        [w_out[0], w_gate_up[0], w_down[0]])

    to3 = lambda a: a.reshape(b, s, a.shape[-1])
    q_sb, k_sb, v_sb, q_sw, kv_sw = map(to3, (q_sb, k_sb, v_sb, q_sw, kv_sw))
    tail_args = (row_vec(sb_norm_g[0]), row_vec(swa_norm_g[0]), w_out_b,
                 row_vec(ln1_g[0]), row_vec(ln1_b[0]), w_gate_up_b, w_down_b,
                 row_vec(ln2_g[0]), row_vec(ln2_b[0]))

    fast, live = _fast_block(q_sw, kv_sw, q_sb, k_sb, v_sb, sinks[0], rel_bias, ha, *tail_args)
    n_tiles = (b * s) // TAIL_ROWS

    def general(_):
        (sb_out,) = _sb_attention(q_sb, k_sb, v_sb, [])
        return _attn_tail_block(q_sw, kv_sw, sinks[0], rel_bias, ha,
                                sb_out.reshape(b * s, sb_w), *tail_args)

    out = lax.cond(jnp.any(live[:n_tiles, 0, 0] < SB_DEAD_LOG2), general, lambda _: fast, None)
    return out.reshape(b, s, d)
```

```python
import functools
import math

import jax
import jax.numpy as jnp
import numpy as np
from jax import lax
from jax.experimental import pallas as pl
from jax.experimental.pallas import tpu as pltpu

F32 = jnp.float32
BF16 = jnp.bfloat16

HEAD_DIM = 64
SB_HEADS = 8
SWA_HEADS = 8
SWA_KV_HEADS = 2
WINDOW = 128
REL_BUCKETS = 32
REL_MAX_DIST = 128
LN_EPS = 1e-5
RMS_EPS = 1e-6
DEPTH = 1
ALPHA = (2 * DEPTH) ** 0.25
QK_SCALE = HEAD_DIM ** -0.5
LOG2E = math.log2(math.e)

LANES = 128
BF16_ROWS = 16
MXU_WIDTH = 256
ATT_BLOCK = 128
SB_WINDOW = 256
SB_Q_TILE = 64
SB_STEP_ROWS = 512
ROW_TILE = 1024
PROJ_PARTS = 2
TAIL_ROWS = 512
FFN_CHUNK = 3 * MXU_WIDTH
VMEM_LIMIT = 56 * 1024 * 1024

SB_DEAD_LOG2 = 127.0
MASK_NEG = -1e30


def _layer_norm(x, g, b):
    mu = jnp.mean(x, axis=-1, keepdims=True)
    xc = x - mu
    var = jnp.mean(xc * xc, axis=-1, keepdims=True)
    return xc * lax.rsqrt(var + LN_EPS) * g + b


def _rms_norm(x, g):
    ms = jnp.mean(x * x, axis=-1, keepdims=True)
    return x * lax.rsqrt(ms + RMS_EPS) * g


def _dot(a, b):
    return jnp.dot(a, b, preferred_element_type=F32)


def _dot_nt(a, b):
    return lax.dot_general(a, b, (((1,), (1,)), ((), ())), preferred_element_type=F32)


def _pipeline_schedule(n, stages):
    vals = [dict() for _ in stages]
    thunks = []
    for step in range(n + len(stages) - 1):
        for s, stage in enumerate(stages):
            i = step - s
            if 0 <= i < n:
                def thunk(s=s, i=i, stage=stage):
                    vals[s][i] = stage(i, vals[s - 1].pop(i)) if s else stage(i)
                thunks.append(thunk)
    return thunks, vals[-1]


def _software_pipeline(n, stages):
    thunks, results = _pipeline_schedule(n, stages)
    for thunk in thunks:
        thunk()
    return results


def _interleave(xs, ys):
    merged, taken = [], 0
    for i, x in enumerate(xs):
        merged.append(x)
        upto = (i + 1) * len(ys) // len(xs)
        merged.extend(ys[taken:upto])
        taken = upto
    return merged


def _inproj_kernel(splits, n_later, x_ref, g_ref, b_ref, w32_ref, *refs):
    later_in, refs = refs[:n_later], refs[n_later:]
    ha_ref, qsb_ref, ksb_ref, vsb_ref, qsw_ref, kv_ref = refs[:6]
    later_out, w_ref = refs[6:6 + n_later], refs[6 + n_later]
    rows = x_ref.shape[0] // PROJ_PARTS
    part = lambda i: slice(i * rows, (i + 1) * rows)
    assert splits["k_sw"][1] == splits["v_sw"][0]

    @pl.when(pl.program_id(0) == 0)
    def _convert_weight():
        w_ref[...] = w32_ref[...].astype(BF16)

    for src, dst in zip(later_in, later_out):
        dst[...] = src[...].astype(BF16)

    def normed(i):
        h32 = _layer_norm(x_ref[part(i), :], g_ref[...], b_ref[...])
        ha_ref[part(i), :] = ALPHA * h32
        return h32.astype(BF16)

    def projected(i, h):
        r = part(i)
        proj = lambda lo, hi: _dot(h, w_ref[:, lo:hi])
        qsb_ref[r, :] = (proj(*splits["q_sb"]) * QK_SCALE).astype(BF16)
        ksb_ref[r, :] = proj(*splits["k_sb"]).astype(BF16)
        vsb_ref[r, :] = proj(*splits["v_sb"]).astype(BF16)
        qsw_ref[r, :] = (proj(*splits["q_sw"]) * QK_SCALE).astype(BF16)
        kv = proj(splits["k_sw"][0], splits["v_sw"][1])
        kvw = kv.shape[-1] // 2
        k, v = kv[:, :kvw], kv[:, kvw:]
        kv_ref[r, 0 * kvw:1 * kvw] = k.astype(BF16)
        kv_ref[r, 1 * kvw:2 * kvw] = pltpu.roll(k, HEAD_DIM, axis=1).astype(BF16)
        kv_ref[r, 2 * kvw:3 * kvw] = v.astype(BF16)
        kv_ref[r, 3 * kvw:4 * kvw] = pltpu.roll(v, HEAD_DIM, axis=1).astype(BF16)

    _software_pipeline(PROJ_PARTS, [normed, projected])


def _in_projection(x2, g, b, w_in, splits, later_weights):
    n, d = x2.shape
    steps = n // ROW_TILE
    width = lambda name: splits[name][1] - splits[name][0]
    out_w = [width("q_sb"), width("k_sb"), width("v_sb"), width("q_sw"), 4 * width("k_sw")]
    row = lambda i: (i, 0)
    fixed = lambda i: (0, 0)
    later_specs = [
        pl.BlockSpec((size, w.shape[1]),
                     lambda i, last=w.shape[0] // size - 1: (jnp.minimum(i, last), 0))
        for w, size in zip(later_weights, _conversion_blocks(later_weights, steps))]
    return pl.pallas_call(
        functools.partial(_inproj_kernel, splits, len(later_weights)),
        grid=(steps,),
        in_specs=[
            pl.BlockSpec((ROW_TILE, d), row),
            pl.BlockSpec((1, d), fixed),
            pl.BlockSpec((1, d), fixed),
            pl.BlockSpec(w_in.shape, fixed, pipeline_mode=pl.Buffered(1)),
        ] + later_specs,
        out_specs=[pl.BlockSpec((ROW_TILE, d), row)]
        + [pl.BlockSpec((ROW_TILE, w), row) for w in out_w] + later_specs,
        out_shape=[jax.ShapeDtypeStruct((n, d), F32)]
        + [jax.ShapeDtypeStruct((n, w), BF16) for w in out_w]
        + [jax.ShapeDtypeStruct(w.shape, BF16) for w in later_weights],
        scratch_shapes=[pltpu.VMEM(w_in.shape, BF16)],
        compiler_params=pltpu.CompilerParams(
            dimension_semantics=("arbitrary",), vmem_limit_bytes=VMEM_LIMIT),
        name="ln_in_proj",
    )(x2, g, b, w_in, *later_weights)


def _conversion_blocks(weights, steps):
    sizes = []
    for w in weights:
        rows = w.shape[0]
        sizes.append(next(r for r in range(BF16_ROWS, rows + 1, BF16_ROWS)
                          if rows % r == 0 and rows // r <= steps))
    return sizes


def _split_heads(x, lane):
    zero = jnp.zeros_like(x)
    return jnp.concatenate([jnp.where(lane < HEAD_DIM, x, zero),
                            jnp.where(lane >= HEAD_DIM, x, zero)], axis=0)


def _merge_heads(o2, lane):
    t = o2.shape[0] // 2
    return jnp.where(lane < HEAD_DIM, o2[:t], o2[t:])


def _sb_masks(valid):
    return jnp.where(valid, LOG2E, 0.0), jnp.where(valid, 0.0, MASK_NEG)


def _sb_scores(qq, k, keep):
    z = _dot_nt(qq, k)
    zl = z * LOG2E
    c_nat = jnp.maximum(z, 0.0) + jnp.log(1.0 + jnp.exp2(-jnp.abs(zl)))
    cm = c_nat * keep
    return zl - cm, cm.astype(BF16), cm[:, :LANES]


def _sb_weights(scores, tri, kills, carrieds):
    sums = _dot(jnp.concatenate([sc[1] for sc in scores], axis=0), tri)
    rows = scores[0][0].shape[0]
    out = []
    for n, ((log_beta, _, cm0), kill, carried) in enumerate(zip(scores, kills, carrieds)):
        later = sums[n * rows:(n + 1) * rows]
        log_a = (log_beta - later) + kill
        if carried is not None:
            log_a = log_a - jnp.concatenate([carried] * (log_a.shape[1] // LANES), axis=1)
        out.append((jnp.exp2(log_a).astype(BF16), later[:, :LANES] + cm0))
    return out


def _sb_kernel(n_conv, q_ref, k_ref, v_ref, *refs):
    conv_in, o_ref, conv_out = refs[:n_conv], refs[n_conv], refs[n_conv + 1:2 * n_conv + 1]
    tri_ref, mask_ref, acc_ref, car_ref = refs[2 * n_conv + 1:]
    for src, dst in zip(conv_in, conv_out):
        dst[...] = src[...].astype(BF16)

    bi = pl.program_id(0)
    qi = pl.program_id(1)
    tq, w = SB_Q_TILE, SB_WINDOW
    subs = q_ref.shape[1] // tq
    pairs = q_ref.shape[-1] // LANES
    row0 = qi * (subs * tq)

    @pl.when(jnp.logical_and(bi == 0, qi == 0))
    def _fill_tri():
        j = lax.broadcasted_iota(jnp.int32, (w, w), 0)
        s = lax.broadcasted_iota(jnp.int32, (w, w), 1)
        tri_ref[...] = jnp.where(j > s, 1.0, 0.0).astype(BF16)

    lane = lax.broadcasted_iota(jnp.int32, (tq, LANES), 1)
    lanes_of = lambda p: slice(p * LANES, (p + 1) * LANES)
    rows_of = lambda h: slice(h * tq, (h + 1) * tq)
    first_start = lambda h: pl.multiple_of(jnp.maximum(row0 + (h + 1) * tq - w, 0), tq)

    def key_positions(start):
        return start + lax.broadcasted_iota(jnp.int32, (2 * tq, w), 1)

    assert subs * tq >= w - tq

    @pl.when(qi <= 1)
    def _fill_masks():
        row = lax.broadcasted_iota(jnp.int32, (2 * tq, w), 0) & (tq - 1)
        for h in range(subs):
            mask_ref[h, 0], mask_ref[h, 1] = _sb_masks(
                key_positions(first_start(h)) < row0 + h * tq + row)

    def scores(p, starts, keeps):
        return [_sb_scores(_split_heads(q_ref[0, rows_of(h), lanes_of(p)], lane),
                           k_ref[0, pl.ds(starts[h], w), lanes_of(p)], keeps[h])
                for h in range(subs)]

    first_starts = [first_start(h) for h in range(subs)]

    def first_scores(p):
        return scores(p, first_starts, [mask_ref[h, 0] for h in range(subs)])

    def first_weights(p, sc):
        return _sb_weights(sc, tri_ref[...], [mask_ref[h, 1] for h in range(subs)], [None] * subs)

    def first_values(p, aws):
        least = None
        for h, (a, total) in enumerate(aws):
            o2 = _dot(a, v_ref[0, pl.ds(first_starts[h], w), lanes_of(p)])
            o_ref[0, rows_of(h), lanes_of(p)] = _merge_heads(o2, lane).astype(o_ref.dtype)
            acc_ref[p, h] = o2
            car_ref[p, h] = total
            m = jnp.min(total[:, 0:1])
            least = m if least is None else jnp.minimum(least, m)
        return least

    first = _software_pipeline(pairs, [first_scores, first_weights, first_values])
    live = functools.reduce(jnp.minimum, [first[p] for p in range(pairs)])
    last_start = first_starts[-1]

    @pl.when(jnp.logical_and(last_start > 0, live < SB_DEAD_LOG2))
    def _walk_back():
        for p in range(pairs):
            for h in range(subs):
                car_ref[p, h] = jnp.broadcast_to(car_ref[p, h][:, 0:1], (2 * tq, LANES))

        def body(state):
            step, _ = state
            done = [jnp.maximum(first_start(h) - step * w, 0) for h in range(subs)]
            starts = [pl.multiple_of(jnp.maximum(d - w, 0), tq) for d in done]
            masks = [_sb_masks(key_positions(s) < d) for s, d in zip(starts, done)]

            def more_scores(p):
                return scores(p, starts, [m[0] for m in masks])

            def weights(p, sc):
                return _sb_weights(sc, tri_ref[...], [m[1] for m in masks],
                                   [car_ref[p, h] for h in range(subs)])

            def values(p, aws):
                least = None
                for h, (a, total) in enumerate(aws):
                    acc_ref[p, h] += _dot(a, v_ref[0, pl.ds(starts[h], w), lanes_of(p)])
                    carried = car_ref[p, h] + total[:, 0:1]
                    car_ref[p, h] = carried
                    m = jnp.min(carried)
                    least = m if least is None else jnp.minimum(least, m)
                return least

            mins = _software_pipeline(pairs, [more_scores, weights, values])
            return step + 1, functools.reduce(jnp.minimum, [mins[p] for p in range(pairs)])

        def cond(state):
            step, nxt = state
            return jnp.logical_and(last_start - step * w > 0, nxt < SB_DEAD_LOG2)

        lax.while_loop(cond, body, (jnp.int32(0), live))
        for p in range(pairs):
            for h in range(subs):
                o_ref[0, rows_of(h), lanes_of(p)] = _merge_heads(
                    acc_ref[p, h], lane).astype(o_ref.dtype)


def _sb_attention(q, k, v, weights):
    b, s, wd = q.shape
    pairs = wd // LANES
    subs = SB_STEP_ROWS // SB_Q_TILE
    nq = s // SB_STEP_ROWS
    tile = lambda bi, i: (bi, i, 0)
    whole = lambda bi, i: (bi, 0, 0)
    once = pl.Buffered(1)
    conv_specs = [
        pl.BlockSpec((size, w.shape[1]),
                     lambda bi, i, last=w.shape[0] // size - 1: (jnp.minimum(bi * nq + i, last), 0))
        for w, size in zip(weights, _conversion_blocks(weights, b * nq))]
    return pl.pallas_call(
        functools.partial(_sb_kernel, len(weights)),
        grid=(b, nq),
        in_specs=[
            pl.BlockSpec((1, SB_STEP_ROWS, wd), tile),
            pl.BlockSpec((1, s, wd), whole, pipeline_mode=once),
            pl.BlockSpec((1, s, wd), whole, pipeline_mode=once),
        ] + conv_specs,
        out_specs=[pl.BlockSpec((1, SB_STEP_ROWS, wd), tile)] + conv_specs,
        out_shape=[jax.ShapeDtypeStruct((b, s, wd), BF16)]
        + [jax.ShapeDtypeStruct(w.shape, BF16) for w in weights],
        scratch_shapes=[
            pltpu.VMEM((SB_WINDOW, SB_WINDOW), BF16),
            pltpu.VMEM((subs, 2, 2 * SB_Q_TILE, SB_WINDOW), F32),
            pltpu.VMEM((pairs, subs, 2 * SB_Q_TILE, LANES), F32),
            pltpu.VMEM((pairs, subs, 2 * SB_Q_TILE, LANES), F32),
        ],
        compiler_params=pltpu.CompilerParams(
            dimension_semantics=("arbitrary", "arbitrary"), vmem_limit_bytes=VMEM_LIMIT),
        name="stick_breaking_attention",
    )(q, k, v, *weights)


def _t5_log_bucket_starts():
    exact = REL_BUCKETS // 2
    d = np.arange(exact, 2 * REL_MAX_DIST, dtype=np.float32)
    large = exact + (np.log(d / np.float32(exact)) / np.float32(math.log(REL_MAX_DIST / exact))
                     * np.float32(REL_BUCKETS - exact)).astype(np.int32)
    large = np.minimum(large, REL_BUCKETS - 1)
    assert large[0] == exact and np.all(np.diff(large) >= 0)
    return [int(d[np.argmax(large >= bk)]) for bk in range(exact + 1, REL_BUCKETS)]


def _t5_causal_bucket(distance):
    d = jnp.maximum(distance, 0)
    bucket = jnp.minimum(d, REL_BUCKETS // 2)
    for start in _t5_log_bucket_starts():
        bucket = bucket + jnp.where(d >= start, 1, 0)
    return bucket


def _swa_fill_bias(relb_ref, bias_ref):
    t = ATT_BLOCK
    r = lax.broadcasted_iota(jnp.int32, (t, 2 * t), 0)
    c = lax.broadcasted_iota(jnp.int32, (t, 2 * t), 1)
    dist = r + t - c
    bucket = _t5_causal_bucket(dist)
    valid = jnp.logical_and(dist >= 0, dist < WINDOW)
    for h in range(SWA_HEADS):
        bias = jnp.zeros((t, 2 * t), F32)
        for bk in range(REL_BUCKETS):
            bias = jnp.where(bucket == bk, relb_ref[bk, h], bias)
        bias_ref[h] = jnp.where(valid, bias, MASK_NEG)


def _swa_schedule(sinks_ref, q_ref, kvc_ref, kvp_ref, bias_ref, o_ref, no_previous):
    t = ATT_BLOCK
    kvw = kvc_ref.shape[-1] // 4
    group = SWA_HEADS // SWA_KV_HEADS
    tiles = q_ref.shape[0] // t
    rows_of = lambda n: slice(n * t, (n + 1) * t)
    lane = lax.broadcasted_iota(jnp.int32, (t, LANES), 1)
    row2 = lax.broadcasted_iota(jnp.int32, (2 * t, 1), 0)
    col2 = lax.broadcasted_iota(jnp.int32, (2 * t, 2 * t), 1)
    prev_pen = jnp.where(jnp.logical_and(col2 < t, no_previous), MASK_NEG, 0.0)

    def kv_window(n, variant):
        cols = slice(variant * kvw, (variant + 1) * kvw)
        before = kvc_ref[rows_of(n - 1), cols] if n else kvp_ref[:, cols]
        return jnp.concatenate([before, kvc_ref[rows_of(n), cols]], axis=0)

    items = [(n, half, j) for n in range(tiles) for half in range(2) for j in range(SWA_KV_HEADS)]
    heads_of = lambda half, j: [h for h in range(j * group, (j + 1) * group) if h % 2 == half]

    def probabilities(i):
        n, half, j = items[i]
        heads = heads_of(half, j)
        in_half = (lane >= HEAD_DIM) if half else (lane < HEAD_DIM)
        qq = jnp.concatenate(
            [jnp.where(in_half, q_ref[rows_of(n), (h // 2) * LANES:(h // 2 + 1) * LANES], 0)
             for h in heads], axis=0)
        s = _dot_nt(qq, kv_window(n, 0 + int(j != half)))
        s = s + jnp.concatenate([bias_ref[h] for h in heads], axis=0)
        if n == 0:
            s = s + prev_pen
        sink = jnp.where(row2 < t, sinks_ref[heads[0]], sinks_ref[heads[1]])
        m = jnp.maximum(jnp.max(s, axis=-1, keepdims=True), sink)
        p = jnp.exp(s - m)
        denom = jnp.sum(p, axis=-1, keepdims=True) + jnp.exp(sink - m)
        return p.astype(BF16), 1.0 / denom

    def values(i, pd):
        n, half, j = items[i]
        p, inv_denom = pd
        return _dot(p, kv_window(n, 2 + int(j != half))) * inv_denom

    thunks, outs = _pipeline_schedule(len(items), [probabilities, values])

    def store():
        for i, (n, half, j) in enumerate(items):
            if half == 0:
                even, odd = outs[i], outs[items.index((n, 1, j))]
                for e, h in enumerate(heads_of(0, j)):
                    o_ref[rows_of(n), (h // 2) * LANES:(h // 2 + 1) * LANES] = jnp.where(
                        lane < HEAD_DIM, even[e * t:(e + 1) * t], odd[e * t:(e + 1) * t]
                    ).astype(o_ref.dtype)

    return thunks + [store]


def _tail_schedule(ha_ref, sb_ref, sw_ref, sbg_ref, swg_ref, wo_ref, l1g_ref, l1b_ref,
                   wgu_ref, wd_ref, l2g_ref, l2b_ref, o_ref):
    d = ha_ref.shape[1]
    d_ff = wd_ref.shape[0]
    mw = MXU_WIDTH
    bounds = list(range(0, d_ff, FFN_CHUNK)) + [d_ff]
    n_chunks = len(bounds) - 1
    out_tiles = range(d // mw)
    st = {}

    def merged():
        sb = _rms_norm(sb_ref[...].astype(F32), sbg_ref[...]).astype(BF16)
        sw = _rms_norm(sw_ref[...].astype(F32), swg_ref[...]).astype(BF16)
        st["merged"] = jnp.concatenate([sb, sw], axis=1)

    def out_projection(n):
        st["mix", n] = _dot(st["merged"], wo_ref[:, n * mw:(n + 1) * mw])

    def first_norm():
        mix = jnp.concatenate([st.pop(("mix", n)) for n in out_tiles], axis=1)
        st.pop("merged")
        h1 = _layer_norm(ha_ref[...] + mix, l1g_ref[...], l1b_ref[...])
        st["h1b"] = h1.astype(BF16)
        for n in out_tiles:
            st["acc", n] = ALPHA * h1[:, n * mw:(n + 1) * mw]

    def activation(c, n):
        lo = bounds[c] + n * mw
        gate = _dot(st["h1b"], wgu_ref[:, lo:lo + mw])
        up = _dot(st["h1b"], wgu_ref[:, d_ff + lo:d_ff + lo + mw])
        st["act", c, n] = (gate / (1.0 + jnp.exp(-gate)) * up).astype(BF16)

    def down(c, n):
        tiles = range((bounds[c + 1] - bounds[c]) // mw)
        if n == 0:
            st["act", c] = jnp.concatenate([st.pop(("act", c, k)) for k in tiles], axis=1)
        st["acc", n] = st["acc", n] + _dot(st["act", c], wd_ref[bounds[c]:bounds[c + 1],
                                                               n * mw:(n + 1) * mw])

    def second_norm():
        y = jnp.concatenate([st.pop(("acc", n)) for n in out_tiles], axis=1)
        o_ref[...] = _layer_norm(y, l2g_ref[...], l2b_ref[...])

    act_thunks = [[functools.partial(activation, c, n)
                   for n in range((bounds[c + 1] - bounds[c]) // mw)] for c in range(n_chunks)]
    down_thunks = [[functools.partial(down, c, n) for n in out_tiles] for c in range(n_chunks)]
    ffn = list(act_thunks[0])
    for c in range(1, n_chunks):
        ffn += _interleave(act_thunks[c], down_thunks[c - 1])
    ffn += down_thunks[-1]
    return (merged, [functools.partial(out_projection, n) for n in out_tiles], first_norm,
            ffn, second_norm)


def _attn_tail_kernel(tiles_per_seq, last_tile, sinks_ref, relb_ref, q_ref, kvc_ref, kvp_ref,
                      ha_ref, sb_ref, sbg_ref, swg_ref, wo_ref, l1g_ref, l1b_ref,
                      wgu_ref, wd_ref, l2g_ref, l2b_ref, o_ref, bias_ref, sw_ref):
    t = pl.program_id(0)

    @pl.when(t == 0)
    def _first_step():
        _swa_fill_bias(relb_ref, bias_ref)
        sw_ref[1] = jnp.zeros(sw_ref.shape[1:], sw_ref.dtype)

    no_previous = (jnp.minimum(t, last_tile) % tiles_per_seq) == 0
    attention = _swa_schedule(sinks_ref, q_ref.at[0], kvc_ref.at[0], kvp_ref.at[0], bias_ref,
                              sw_ref.at[t % 2], no_previous)
    merged, out_proj, first_norm, ffn, second_norm = _tail_schedule(
        ha_ref, sb_ref, sw_ref.at[(t + 1) % 2], sbg_ref, swg_ref, wo_ref,
        l1g_ref, l1b_ref, wgu_ref, wd_ref, l2g_ref, l2b_ref, o_ref)
    tail = [merged] + out_proj + [first_norm] + ffn + [second_norm]
    for thunk in _interleave(tail, attention):
        thunk()


def _attn_tail_block(q_sw, kv_sw, sinks, rel_bias, ha, sb2, sb_g, sw_g, w_out_b, ln1_g, ln1_b,
                     w_gate_up_b, w_down_b, ln2_g, ln2_b):
    b, s, wd = q_sw.shape
    n, d = ha.shape
    tiles_per_seq = s // TAIL_ROWS
    last_tile = n // TAIL_ROWS - 1
    key_tiles = TAIL_ROWS // ATT_BLOCK
    att = lambda t: jnp.minimum(t, last_tile)
    cur = lambda t: (att(t) // tiles_per_seq, att(t) % tiles_per_seq, 0)
    prev = lambda t: (att(t) // tiles_per_seq,
                      jnp.maximum(att(t) % tiles_per_seq * key_tiles - 1, 0), 0)
    row = lambda t: (jnp.maximum(t - 1, 0), 0)
    fixed = lambda t: (0, 0)
    smem = pl.BlockSpec(memory_space=pltpu.SMEM)
    vec = lambda a: pl.BlockSpec((1, a.shape[-1]), fixed)
    weight = lambda a: pl.BlockSpec(a.shape, fixed, pipeline_mode=pl.Buffered(1))
    return pl.pallas_call(
        functools.partial(_attn_tail_kernel, tiles_per_seq, last_tile),
        grid=(last_tile + 2,),
        in_specs=[
            smem, smem,
            pl.BlockSpec((1, TAIL_ROWS, wd), cur),
            pl.BlockSpec((1, TAIL_ROWS, kv_sw.shape[-1]), cur),
            pl.BlockSpec((1, ATT_BLOCK, kv_sw.shape[-1]), prev),
            pl.BlockSpec((TAIL_ROWS, d), row),
            pl.BlockSpec((TAIL_ROWS, sb2.shape[-1]), row),
            vec(sb_g), vec(sw_g), weight(w_out_b), vec(ln1_g), vec(ln1_b),
            weight(w_gate_up_b), weight(w_down_b), vec(ln2_g), vec(ln2_b),
        ],
        out_specs=pl.BlockSpec((TAIL_ROWS, d), row),
        out_shape=jax.ShapeDtypeStruct((n, d), F32),
        scratch_shapes=[
            pltpu.VMEM((SWA_HEADS, ATT_BLOCK, 2 * ATT_BLOCK), F32),
            pltpu.VMEM((2, TAIL_ROWS, wd), BF16),
        ],
        compiler_params=pltpu.CompilerParams(
            dimension_semantics=("arbitrary",), vmem_limit_bytes=VMEM_LIMIT),
        name="swa_attention_and_dense_tail",
    )(sinks, rel_bias, q_sw, kv_sw, kv_sw, ha, sb2, sb_g, sw_g, w_out_b, ln1_g, ln1_b,
      w_gate_up_b, w_down_b, ln2_g, ln2_b)


SB_GROUP = 8
SB_LIVE_DONE = 1e30


def _sb_fast_masks(mask_ref):
    tq, w = SB_Q_TILE, SB_WINDOW
    row = lax.broadcasted_iota(jnp.int32, (2 * tq, w), 0) & (tq - 1)
    col = lax.broadcasted_iota(jnp.int32, (2 * tq, w), 1)
    causal = col < row + (w - tq)
    mask_ref[0, 0], mask_ref[0, 1] = _sb_masks(causal)
    for h in range(mask_ref.shape[0] - 1):
        first_key = w - (h + 1) * tq
        mask_ref[1 + h, 0], mask_ref[1 + h, 1] = _sb_masks(
            jnp.logical_and(causal, col >= first_key))


def _sb_fast_schedule(q_ref, kc_ref, kp_ref, vc_ref, vp_ref, tri_ref, mask_ref, o_ref,
                      no_previous, lives):
    tq, w = SB_Q_TILE, SB_WINDOW
    subs = q_ref.shape[0] // tq
    pairs = q_ref.shape[-1] // LANES
    clamped = w // tq - 1
    assert mask_ref.shape[0] == clamped + 1 and subs % SB_GROUP == 0
    lane = lax.broadcasted_iota(jnp.int32, (tq, LANES), 1)
    lanes_of = lambda p: slice(p * LANES, (p + 1) * LANES)
    rows_of = lambda h: slice(h * tq, (h + 1) * tq)

    def window(h, cur_ref, prev_ref, p):
        start = (h + 1) * tq - w
        if start >= 0:
            return cur_ref[start:start + w, lanes_of(p)]
        return jnp.concatenate([prev_ref[w + start:w, lanes_of(p)],
                                cur_ref[0:w + start, lanes_of(p)]], axis=0)

    def mask(h, which):
        if h >= clamped:
            return mask_ref[0, which]
        return mask_ref[jnp.where(no_previous, 1 + h, 0), which]

    items = [(p, g) for p in range(pairs) for g in range(subs // SB_GROUP)]
    group = lambda g: range(g * SB_GROUP, (g + 1) * SB_GROUP)

    def scores(i):
        p, g = items[i]
        return [_sb_scores(_split_heads(q_ref[rows_of(h), lanes_of(p)], lane),
                           window(h, kc_ref, kp_ref, p), mask(h, 0)) for h in group(g)]

    def weights(i, sc):
        return _sb_weights(sc, tri_ref[...], [mask(h, 1) for h in group(items[i][1])],
                           [None] * SB_GROUP)

    def values(i, aws):
        p, g = items[i]
        for h, (a, total) in zip(group(g), aws):
            o2 = _dot(a, window(h, vc_ref, vp_ref, p))
            o_ref[rows_of(h), lanes_of(p)] = _merge_heads(o2, lane).astype(o_ref.dtype)
            m = jnp.min(total[:, 0:1])
            lives.append(m if h > clamped else jnp.where(no_previous, SB_LIVE_DONE, m))

    return _pipeline_schedule(len(items), [scores, weights, values])[0]


def _fast_kernel(tiles_per_seq, last_tile, sinks_ref, relb_ref, qsw_ref, kvc_ref, kvp_ref,
                 qsb_ref, kc_ref, kp_ref, vc_ref, vp_ref,
                 ha_ref, sbg_ref, swg_ref, wo_ref, l1g_ref, l1b_ref,
                 wgu_ref, wd_ref, l2g_ref, l2b_ref, o_ref, live_ref,
                 bias_ref, sw_ref, sb_ref, tri_ref, mask_ref):
    t = pl.program_id(0)

    @pl.when(t == 0)
    def _first_step():
        _swa_fill_bias(relb_ref, bias_ref)
        _sb_fast_masks(mask_ref)
        j = lax.broadcasted_iota(jnp.int32, tri_ref.shape, 0)
        s = lax.broadcasted_iota(jnp.int32, tri_ref.shape, 1)
        tri_ref[...] = jnp.where(j > s, 1.0, 0.0).astype(BF16)
        sw_ref[1] = jnp.zeros(sw_ref.shape[1:], sw_ref.dtype)
        sb_ref[1] = jnp.zeros(sb_ref.shape[1:], sb_ref.dtype)

    no_previous = (jnp.minimum(t, last_tile) % tiles_per_seq) == 0
    lives = []
    swa = _swa_schedule(sinks_ref, qsw_ref.at[0], kvc_ref.at[0], kvp_ref.at[0], bias_ref,
                        sw_ref.at[t % 2], no_previous)
    sb = _sb_fast_schedule(qsb_ref.at[0], kc_ref.at[0], kp_ref.at[0], vc_ref.at[0], vp_ref.at[0],
                           tri_ref, mask_ref, sb_ref.at[t % 2], no_previous, lives)
    merged, out_proj, first_norm, ffn, second_norm = _tail_schedule(
        ha_ref, sb_ref.at[(t + 1) % 2], sw_ref.at[(t + 1) % 2], sbg_ref, swg_ref, wo_ref,
        l1g_ref, l1b_ref, wgu_ref, wd_ref, l2g_ref, l2b_ref, o_ref)
    tail = [merged] + out_proj + [first_norm] + ffn + [second_norm]
    for thunk in _interleave(tail, _interleave(swa, sb)):
        thunk()
    live_ref[...] = jnp.full(live_ref.shape, functools.reduce(jnp.minimum, lives), F32)


def _fast_block(q_sw, kv_sw, q_sb, k_sb, v_sb, sinks, rel_bias, ha, sb_g, sw_g, w_out_b,
                ln1_g, ln1_b, w_gate_up_b, w_down_b, ln2_g, ln2_b):
    b, s, wd = q_sw.shape
    n, d = ha.shape
    tiles_per_seq = s // TAIL_ROWS
    last_tile = n // TAIL_ROWS - 1
    att = lambda t: jnp.minimum(t, last_tile)
    cur = lambda t: (att(t) // tiles_per_seq, att(t) % tiles_per_seq, 0)

    def prev(rows):
        per_tile = TAIL_ROWS // rows
        return lambda t: (att(t) // tiles_per_seq,
                          jnp.maximum(att(t) % tiles_per_seq * per_tile - 1, 0), 0)

    row = lambda t: (jnp.maximum(t - 1, 0), 0)
    fixed = lambda t: (0, 0)
    smem = pl.BlockSpec(memory_space=pltpu.SMEM)
    vec = lambda a: pl.BlockSpec((1, a.shape[-1]), fixed)
    weight = lambda a: pl.BlockSpec(a.shape, fixed, pipeline_mode=pl.Buffered(1))
    tile3 = lambda a: pl.BlockSpec((1, TAIL_ROWS, a.shape[-1]), cur)
    before = lambda a, rows: pl.BlockSpec((1, rows, a.shape[-1]), prev(rows))
    steps = last_tile + 2
    return pl.pallas_call(
        functools.partial(_fast_kernel, tiles_per_seq, last_tile),
        grid=(steps,),
        in_specs=[
            smem, smem,
            tile3(q_sw), tile3(kv_sw), before(kv_sw, ATT_BLOCK),
            tile3(q_sb), tile3(k_sb), before(k_sb, SB_WINDOW), tile3(v_sb), before(v_sb, SB_WINDOW),
            pl.BlockSpec((TAIL_ROWS, d), row),
            vec(sb_g), vec(sw_g), weight(w_out_b), vec(ln1_g), vec(ln1_b),
            weight(w_gate_up_b), weight(w_down_b), vec(ln2_g), vec(ln2_b),
        ],
        out_specs=[pl.BlockSpec((TAIL_ROWS, d), row),
                   pl.BlockSpec((1, 8, LANES), lambda t: (t, 0, 0))],
        out_shape=[jax.ShapeDtypeStruct((n, d), F32),
                   jax.ShapeDtypeStruct((steps, 8, LANES), F32)],
        scratch_shapes=[
            pltpu.VMEM((SWA_HEADS, ATT_BLOCK, 2 * ATT_BLOCK), F32),
            pltpu.VMEM((2, TAIL_ROWS, wd), BF16),
            pltpu.VMEM((2, TAIL_ROWS, q_sb.shape[-1]), BF16),
            pltpu.VMEM((SB_WINDOW, SB_WINDOW), BF16),
            pltpu.VMEM((SB_WINDOW // SB_Q_TILE, 2, 2 * SB_Q_TILE, SB_WINDOW), F32),
        ],
        compiler_params=pltpu.CompilerParams(
            dimension_semantics=("arbitrary",), vmem_limit_bytes=VMEM_LIMIT),
        name="attention_and_dense_tail",
    )(sinks, rel_bias, q_sw, kv_sw, kv_sw, q_sb, k_sb, k_sb, v_sb, v_sb, ha, sb_g, sw_g, w_out_b,
      ln1_g, ln1_b, w_gate_up_b, w_down_b, ln2_g, ln2_b)


def kernel(x, ln_in_g, ln_in_b, w_in, sb_norm_g, swa_norm_g, sinks, rel_bias, w_out,
           ln1_g, ln1_b, w_gate_up, w_down, ln2_g, ln2_b):
    b, s, d = x.shape
    assert w_in.shape[0] == DEPTH == 1
    sb_w = SB_HEADS * HEAD_DIM
    sw_w = SWA_HEADS * HEAD_DIM
    kv_w = SWA_KV_HEADS * HEAD_DIM
    assert w_in.shape[-1] == 3 * sb_w + sw_w + 2 * kv_w
    assert kv_w == LANES and SWA_HEADS // SWA_KV_HEADS == 4
    assert s % SB_WINDOW == 0 and s % SB_STEP_ROWS == 0 and s % TAIL_ROWS == 0
    assert (b * s) % ROW_TILE == 0 and WINDOW <= ATT_BLOCK and d % MXU_WIDTH == 0
    assert FFN_CHUNK % MXU_WIDTH == 0 and w_down.shape[1] % MXU_WIDTH == 0

    x2 = x.reshape(b * s, d)
    row_vec = lambda a: a.reshape(1, -1)

    names = ("q_sb", "k_sb", "v_sb", "q_sw", "k_sw", "v_sw")
    widths = (sb_w, sb_w, sb_w, sw_w, kv_w, kv_w)
    splits, lo = {}, 0
    for name, wd in zip(names, widths):
        splits[name] = (lo, lo + wd)
        lo += wd
    ha, q_sb, k_sb, v_sb, q_sw, kv_sw, w_out_b, w_gate_up_b, w_down_b = _in_projection(
        x2, row_vec(ln_in_g), row_vec(ln_in_b), w_in[0], splits,
        [w_out[0], w_gate_up[0], w_down[0]])

    to3 = lambda a: a.reshape(b, s, a.shape[-1])
    q_sb, k_sb, v_sb, q_sw, kv_sw = map(to3, (q_sb, k_sb, v_sb, q_sw, kv_sw))
    tail_args = (row_vec(sb_norm_g[0]), row_vec(swa_norm_g[0]), w_out_b,
                 row_vec(ln1_g[0]), row_vec(ln1_b[0]), w_gate_up_b, w_down_b,
                 row_vec(ln2_g[0]), row_vec(ln2_b[0]))

    fast, live = _fast_block(q_sw, kv_sw, q_sb, k_sb, v_sb, sinks[0], rel_bias, ha, *tail_args)
    n_tiles = (b * s) // TAIL_ROWS

    def general(_):
        (sb_out,) = _sb_attention(q_sb, k_sb, v_sb, [])
        return _attn_tail_block(q_sw, kv_sw, sinks[0], rel_bias, ha,
                                sb_out.reshape(b * s, sb_w), *tail_args)

    out = lax.cond(jnp.any(live[:n_tiles, 0, 0] < SB_DEAD_LOG2), general, lambda _: fast, None)
    return out.reshape(b, s, d)
```

```python
import functools
import math

import jax
import jax.numpy as jnp
import numpy as np
from jax import lax
from jax.experimental import pallas as pl
from jax.experimental.pallas import tpu as pltpu

F32 = jnp.float32
BF16 = jnp.bfloat16

HEAD_DIM = 64
SB_HEADS = 8
SWA_HEADS = 8
SWA_KV_HEADS = 2
WINDOW = 128
REL_BUCKETS = 32
REL_MAX_DIST = 128
LN_EPS = 1e-5
RMS_EPS = 1e-6
DEPTH = 1
ALPHA = (2 * DEPTH) ** 0.25
QK_SCALE = HEAD_DIM ** -0.5
LOG2E = math.log2(math.e)

LANES = 128
BF16_ROWS = 16
MXU_WIDTH = 256
ATT_BLOCK = 128
SB_WINDOW = 256
SB_Q_TILE = 64
SB_STEP_ROWS = 512
ROW_TILE = 1024
PROJ_PARTS = 2
TAIL_ROWS = 512
FFN_CHUNK = 4 * MXU_WIDTH
VMEM_LIMIT = 56 * 1024 * 1024

SB_DEAD_LOG2 = 127.0
MASK_NEG = -1e30


def _layer_norm(x, g, b):
    mu = jnp.mean(x, axis=-1, keepdims=True)
    xc = x - mu
    var = jnp.mean(xc * xc, axis=-1, keepdims=True)
    return xc * lax.rsqrt(var + LN_EPS) * g + b


def _rms_norm(x, g):
    ms = jnp.mean(x * x, axis=-1, keepdims=True)
    return x * lax.rsqrt(ms + RMS_EPS) * g


def _dot(a, b):
    return jnp.dot(a, b, preferred_element_type=F32)


def _dot_nt(a, b):
    return lax.dot_general(a, b, (((1,), (1,)), ((), ())), preferred_element_type=F32)


def _pipeline_schedule(n, stages):
    vals = [dict() for _ in stages]
    thunks = []
    for step in range(n + len(stages) - 1):
        for s, stage in enumerate(stages):
            i = step - s
            if 0 <= i < n:
                def thunk(s=s, i=i, stage=stage):
                    vals[s][i] = stage(i, vals[s - 1].pop(i)) if s else stage(i)
                thunks.append(thunk)
    return thunks, vals[-1]


def _software_pipeline(n, stages):
    thunks, results = _pipeline_schedule(n, stages)
    for thunk in thunks:
        thunk()
    return results


def _interleave(xs, ys):
    merged, taken = [], 0
    for i, x in enumerate(xs):
        merged.append(x)
        upto = (i + 1) * len(ys) // len(xs)
        merged.extend(ys[taken:upto])
        taken = upto
    return merged


def _inproj_kernel(splits, n_later, x_ref, g_ref, b_ref, w32_ref, *refs):
    later_in, refs = refs[:n_later], refs[n_later:]
    ha_ref, qsb_ref, ksb_ref, vsb_ref, qsw_ref, kv_ref = refs[:6]
    later_out, w_ref = refs[6:6 + n_later], refs[6 + n_later]
    rows = x_ref.shape[0] // PROJ_PARTS
    part = lambda i: slice(i * rows, (i + 1) * rows)
    assert splits["k_sw"][1] == splits["v_sw"][0]

    @pl.when(pl.program_id(0) == 0)
    def _convert_weight():
        w_ref[...] = w32_ref[...].astype(BF16)

    for src, dst in zip(later_in, later_out):
        dst[...] = src[...].astype(BF16)

    def normed(i):
        h32 = _layer_norm(x_ref[part(i), :], g_ref[...], b_ref[...])
        ha_ref[part(i), :] = ALPHA * h32
        return h32.astype(BF16)

    def projected(i, h):
        r = part(i)
        proj = lambda lo, hi: _dot(h, w_ref[:, lo:hi])
        qsb_ref[r, :] = (proj(*splits["q_sb"]) * QK_SCALE).astype(BF16)
        ksb_ref[r, :] = proj(*splits["k_sb"]).astype(BF16)
        vsb_ref[r, :] = proj(*splits["v_sb"]).astype(BF16)
        qsw_ref[r, :] = (proj(*splits["q_sw"]) * QK_SCALE).astype(BF16)
        kv = proj(splits["k_sw"][0], splits["v_sw"][1])
        kvw = kv.shape[-1] // 2
        k, v = kv[:, :kvw], kv[:, kvw:]
        kv_ref[r, 0 * kvw:1 * kvw] = k.astype(BF16)
        kv_ref[r, 1 * kvw:2 * kvw] = pltpu.roll(k, HEAD_DIM, axis=1).astype(BF16)
        kv_ref[r, 2 * kvw:3 * kvw] = v.astype(BF16)
        kv_ref[r, 3 * kvw:4 * kvw] = pltpu.roll(v, HEAD_DIM, axis=1).astype(BF16)

    _software_pipeline(PROJ_PARTS, [normed, projected])


def _in_projection(x2, g, b, w_in, splits, later_weights):
    n, d = x2.shape
    steps = n // ROW_TILE
    width = lambda name: splits[name][1] - splits[name][0]
    out_w = [width("q_sb"), width("k_sb"), width("v_sb"), width("q_sw"), 4 * width("k_sw")]
    row = lambda i: (i, 0)
    fixed = lambda i: (0, 0)
    later_specs = [
        pl.BlockSpec((size, w.shape[1]),
                     lambda i, last=w.shape[0] // size - 1: (jnp.minimum(i, last), 0))
        for w, size in zip(later_weights, _conversion_blocks(later_weights, steps))]
    return pl.pallas_call(
        functools.partial(_inproj_kernel, splits, len(later_weights)),
        grid=(steps,),
        in_specs=[
            pl.BlockSpec((ROW_TILE, d), row),
            pl.BlockSpec((1, d), fixed),
            pl.BlockSpec((1, d), fixed),
            pl.BlockSpec(w_in.shape, fixed, pipeline_mode=pl.Buffered(1)),
        ] + later_specs,
        out_specs=[pl.BlockSpec((ROW_TILE, d), row)]
        + [pl.BlockSpec((ROW_TILE, w), row) for w in out_w] + later_specs,
        out_shape=[jax.ShapeDtypeStruct((n, d), F32)]
        + [jax.ShapeDtypeStruct((n, w), BF16) for w in out_w]
        + [jax.ShapeDtypeStruct(w.shape, BF16) for w in later_weights],
        scratch_shapes=[pltpu.VMEM(w_in.shape, BF16)],
        compiler_params=pltpu.CompilerParams(
            dimension_semantics=("arbitrary",), vmem_limit_bytes=VMEM_LIMIT),
        name="ln_in_proj",
    )(x2, g, b, w_in, *later_weights)


def _conversion_blocks(weights, steps):
    sizes = []
    for w in weights:
        rows = w.shape[0]
        sizes.append(next(r for r in range(BF16_ROWS, rows + 1, BF16_ROWS)
                          if rows % r == 0 and rows // r <= steps))
    return sizes


def _split_heads(x, lane):
    zero = jnp.zeros_like(x)
    return jnp.concatenate([jnp.where(lane < HEAD_DIM, x, zero),
                            jnp.where(lane >= HEAD_DIM, x, zero)], axis=0)


def _merge_heads(o2, lane):
    t = o2.shape[0] // 2
    return jnp.where(lane < HEAD_DIM, o2[:t], o2[t:])


def _sb_masks(valid):
    return jnp.where(valid, LOG2E, 0.0), jnp.where(valid, 0.0, MASK_NEG)


def _sb_scores(qq, k, keep):
    z = _dot_nt(qq, k)
    zl = z * LOG2E
    c_nat = jnp.maximum(z, 0.0) + jnp.log(1.0 + jnp.exp2(-jnp.abs(zl)))
    cm = c_nat * keep
    return zl - cm, cm.astype(BF16), cm[:, :LANES]


def _sb_weights(scores, tri, kills, carrieds):
    sums = _dot(jnp.concatenate([sc[1] for sc in scores], axis=0), tri)
    rows = scores[0][0].shape[0]
    out = []
    for n, ((log_beta, _, cm0), kill, carried) in enumerate(zip(scores, kills, carrieds)):
        later = sums[n * rows:(n + 1) * rows]
        log_a = (log_beta - later) + kill
        if carried is not None:
            log_a = log_a - jnp.concatenate([carried] * (log_a.shape[1] // LANES), axis=1)
        out.append((jnp.exp2(log_a).astype(BF16), later[:, :LANES] + cm0))
    return out


def _sb_kernel(n_conv, q_ref, k_ref, v_ref, *refs):
    conv_in, o_ref, conv_out = refs[:n_conv], refs[n_conv], refs[n_conv + 1:2 * n_conv + 1]
    tri_ref, mask_ref, acc_ref, car_ref = refs[2 * n_conv + 1:]
    for src, dst in zip(conv_in, conv_out):
        dst[...] = src[...].astype(BF16)

    bi = pl.program_id(0)
    qi = pl.program_id(1)
    tq, w = SB_Q_TILE, SB_WINDOW
    subs = q_ref.shape[1] // tq
    pairs = q_ref.shape[-1] // LANES
    row0 = qi * (subs * tq)

    @pl.when(jnp.logical_and(bi == 0, qi == 0))
    def _fill_tri():
        j = lax.broadcasted_iota(jnp.int32, (w, w), 0)
        s = lax.broadcasted_iota(jnp.int32, (w, w), 1)
        tri_ref[...] = jnp.where(j > s, 1.0, 0.0).astype(BF16)

    lane = lax.broadcasted_iota(jnp.int32, (tq, LANES), 1)
    lanes_of = lambda p: slice(p * LANES, (p + 1) * LANES)
    rows_of = lambda h: slice(h * tq, (h + 1) * tq)
    first_start = lambda h: pl.multiple_of(jnp.maximum(row0 + (h + 1) * tq - w, 0), tq)

    def key_positions(start):
        return start + lax.broadcasted_iota(jnp.int32, (2 * tq, w), 1)

    assert subs * tq >= w - tq

    @pl.when(qi <= 1)
    def _fill_masks():
        row = lax.broadcasted_iota(jnp.int32, (2 * tq, w), 0) & (tq - 1)
        for h in range(subs):
            mask_ref[h, 0], mask_ref[h, 1] = _sb_masks(
                key_positions(first_start(h)) < row0 + h * tq + row)

    def scores(p, starts, keeps):
        return [_sb_scores(_split_heads(q_ref[0, rows_of(h), lanes_of(p)], lane),
                           k_ref[0, pl.ds(starts[h], w), lanes_of(p)], keeps[h])
                for h in range(subs)]

    first_starts = [first_start(h) for h in range(subs)]

    def first_scores(p):
        return scores(p, first_starts, [mask_ref[h, 0] for h in range(subs)])

    def first_weights(p, sc):
        return _sb_weights(sc, tri_ref[...], [mask_ref[h, 1] for h in range(subs)], [None] * subs)

    def first_values(p, aws):
        least = None
        for h, (a, total) in enumerate(aws):
            o2 = _dot(a, v_ref[0, pl.ds(first_starts[h], w), lanes_of(p)])
            o_ref[0, rows_of(h), lanes_of(p)] = _merge_heads(o2, lane).astype(o_ref.dtype)
            acc_ref[p, h] = o2
            car_ref[p, h] = total
            m = jnp.min(total[:, 0:1])
            least = m if least is None else jnp.minimum(least, m)
        return least

    first = _software_pipeline(pairs, [first_scores, first_weights, first_values])
    live = functools.reduce(jnp.minimum, [first[p] for p in range(pairs)])
    last_start = first_starts[-1]

    @pl.when(jnp.logical_and(last_start > 0, live < SB_DEAD_LOG2))
    def _walk_back():
        for p in range(pairs):
            for h in range(subs):
                car_ref[p, h] = jnp.broadcast_to(car_ref[p, h][:, 0:1], (2 * tq, LANES))

        def body(state):
            step, _ = state
            done = [jnp.maximum(first_start(h) - step * w, 0) for h in range(subs)]
            starts = [pl.multiple_of(jnp.maximum(d - w, 0), tq) for d in done]
            masks = [_sb_masks(key_positions(s) < d) for s, d in zip(starts, done)]

            def more_scores(p):
                return scores(p, starts, [m[0] for m in masks])

            def weights(p, sc):
                return _sb_weights(sc, tri_ref[...], [m[1] for m in masks],
                                   [car_ref[p, h] for h in range(subs)])

            def values(p, aws):
                least = None
                for h, (a, total) in enumerate(aws):
                    acc_ref[p, h] += _dot(a, v_ref[0, pl.ds(starts[h], w), lanes_of(p)])
                    carried = car_ref[p, h] + total[:, 0:1]
                    car_ref[p, h] = carried
                    m = jnp.min(carried)
                    least = m if least is None else jnp.minimum(least, m)
                return least

            mins = _software_pipeline(pairs, [more_scores, weights, values])
            return step + 1, functools.reduce(jnp.minimum, [mins[p] for p in range(pairs)])

        def cond(state):
            step, nxt = state
            return jnp.logical_and(last_start - step * w > 0, nxt < SB_DEAD_LOG2)

        lax.while_loop(cond, body, (jnp.int32(0), live))
        for p in range(pairs):
            for h in range(subs):
                o_ref[0, rows_of(h), lanes_of(p)] = _merge_heads(
                    acc_ref[p, h], lane).astype(o_ref.dtype)


def _sb_attention(q, k, v, weights):
    b, s, wd = q.shape
    pairs = wd // LANES
    subs = SB_STEP_ROWS // SB_Q_TILE
    nq = s // SB_STEP_ROWS
    tile = lambda bi, i: (bi, i, 0)
    whole = lambda bi, i: (bi, 0, 0)
    once = pl.Buffered(1)
    conv_specs = [
        pl.BlockSpec((size, w.shape[1]),
                     lambda bi, i, last=w.shape[0] // size - 1: (jnp.minimum(bi * nq + i, last), 0))
        for w, size in zip(weights, _conversion_blocks(weights, b * nq))]
    return pl.pallas_call(
        functools.partial(_sb_kernel, len(weights)),
        grid=(b, nq),
        in_specs=[
            pl.BlockSpec((1, SB_STEP_ROWS, wd), tile),
            pl.BlockSpec((1, s, wd), whole, pipeline_mode=once),
            pl.BlockSpec((1, s, wd), whole, pipeline_mode=once),
        ] + conv_specs,
        out_specs=[pl.BlockSpec((1, SB_STEP_ROWS, wd), tile)] + conv_specs,
        out_shape=[jax.ShapeDtypeStruct((b, s, wd), BF16)]
        + [jax.ShapeDtypeStruct(w.shape, BF16) for w in weights],
        scratch_shapes=[
            pltpu.VMEM((SB_WINDOW, SB_WINDOW), BF16),
            pltpu.VMEM((subs, 2, 2 * SB_Q_TILE, SB_WINDOW), F32),
            pltpu.VMEM((pairs, subs, 2 * SB_Q_TILE, LANES), F32),
            pltpu.VMEM((pairs, subs, 2 * SB_Q_TILE, LANES), F32),
        ],
        compiler_params=pltpu.CompilerParams(
            dimension_semantics=("arbitrary", "arbitrary"), vmem_limit_bytes=VMEM_LIMIT),
        name="stick_breaking_attention",
    )(q, k, v, *weights)


def _t5_log_bucket_starts():
    exact = REL_BUCKETS // 2
    d = np.arange(exact, 2 * REL_MAX_DIST, dtype=np.float32)
    large = exact + (np.log(d / np.float32(exact)) / np.float32(math.log(REL_MAX_DIST / exact))
                     * np.float32(REL_BUCKETS - exact)).astype(np.int32)
    large = np.minimum(large, REL_BUCKETS - 1)
    assert large[0] == exact and np.all(np.diff(large) >= 0)
    return [int(d[np.argmax(large >= bk)]) for bk in range(exact + 1, REL_BUCKETS)]


def _t5_causal_bucket(distance):
    d = jnp.maximum(distance, 0)
    bucket = jnp.minimum(d, REL_BUCKETS // 2)
    for start in _t5_log_bucket_starts():
        bucket = bucket + jnp.where(d >= start, 1, 0)
    return bucket


def _swa_fill_bias(relb_ref, bias_ref):
    t = ATT_BLOCK
    r = lax.broadcasted_iota(jnp.int32, (t, 2 * t), 0)
    c = lax.broadcasted_iota(jnp.int32, (t, 2 * t), 1)
    dist = r + t - c
    bucket = _t5_causal_bucket(dist)
    valid = jnp.logical_and(dist >= 0, dist < WINDOW)
    for h in range(SWA_HEADS):
        bias = jnp.zeros((t, 2 * t), F32)
        for bk in range(REL_BUCKETS):
            bias = jnp.where(bucket == bk, relb_ref[bk, h], bias)
        bias_ref[h] = jnp.where(valid, bias, MASK_NEG)


def _swa_schedule(sinks_ref, q_ref, kvc_ref, kvp_ref, bias_ref, o_ref, no_previous):
    t = ATT_BLOCK
    kvw = kvc_ref.shape[-1] // 4
    group = SWA_HEADS // SWA_KV_HEADS
    tiles = q_ref.shape[0] // t
    rows_of = lambda n: slice(n * t, (n + 1) * t)
    lane = lax.broadcasted_iota(jnp.int32, (t, LANES), 1)
    row2 = lax.broadcasted_iota(jnp.int32, (2 * t, 1), 0)
    col2 = lax.broadcasted_iota(jnp.int32, (2 * t, 2 * t), 1)
    prev_pen = jnp.where(jnp.logical_and(col2 < t, no_previous), MASK_NEG, 0.0)

    def kv_window(n, variant):
        cols = slice(variant * kvw, (variant + 1) * kvw)
        before = kvc_ref[rows_of(n - 1), cols] if n else kvp_ref[:, cols]
        return jnp.concatenate([before, kvc_ref[rows_of(n), cols]], axis=0)

    items = [(n, half, j) for n in range(tiles) for half in range(2) for j in range(SWA_KV_HEADS)]
    heads_of = lambda half, j: [h for h in range(j * group, (j + 1) * group) if h % 2 == half]

    def probabilities(i):
        n, half, j = items[i]
        heads = heads_of(half, j)
        in_half = (lane >= HEAD_DIM) if half else (lane < HEAD_DIM)
        qq = jnp.concatenate(
            [jnp.where(in_half, q_ref[rows_of(n), (h // 2) * LANES:(h // 2 + 1) * LANES], 0)
             for h in heads], axis=0)
        s = _dot_nt(qq, kv_window(n, 0 + int(j != half)))
        s = s + jnp.concatenate([bias_ref[h] for h in heads], axis=0)
        if n == 0:
            s = s + prev_pen
        sink = jnp.where(row2 < t, sinks_ref[heads[0]], sinks_ref[heads[1]])
        m = jnp.maximum(jnp.max(s, axis=-1, keepdims=True), sink)
        p = jnp.exp(s - m)
        denom = jnp.sum(p, axis=-1, keepdims=True) + jnp.exp(sink - m)
        return p.astype(BF16), 1.0 / denom

    def values(i, pd):
        n, half, j = items[i]
        p, inv_denom = pd
        return _dot(p, kv_window(n, 2 + int(j != half))) * inv_denom

    thunks, outs = _pipeline_schedule(len(items), [probabilities, values])

    def store():
        for i, (n, half, j) in enumerate(items):
            if half == 0:
                even, odd = outs[i], outs[items.index((n, 1, j))]
                for e, h in enumerate(heads_of(0, j)):
                    o_ref[rows_of(n), (h // 2) * LANES:(h // 2 + 1) * LANES] = jnp.where(
                        lane < HEAD_DIM, even[e * t:(e + 1) * t], odd[e * t:(e + 1) * t]
                    ).astype(o_ref.dtype)

    return thunks + [store]


def _tail_schedule(ha_ref, sb_ref, sw_ref, sbg_ref, swg_ref, wo_ref, l1g_ref, l1b_ref,
                   wgu_ref, wd_ref, l2g_ref, l2b_ref, o_ref):
    d = ha_ref.shape[1]
    d_ff = wd_ref.shape[0]
    mw = MXU_WIDTH
    bounds = list(range(0, d_ff, FFN_CHUNK)) + [d_ff]
    n_chunks = len(bounds) - 1
    out_tiles = range(d // mw)
    st = {}

    def merged():
        sb = _rms_norm(sb_ref[...].astype(F32), sbg_ref[...]).astype(BF16)
        sw = _rms_norm(sw_ref[...].astype(F32), swg_ref[...]).astype(BF16)
        st["merged"] = jnp.concatenate([sb, sw], axis=1)

    def out_projection(n):
        st["mix", n] = _dot(st["merged"], wo_ref[:, n * mw:(n + 1) * mw])

    def first_norm():
        mix = jnp.concatenate([st.pop(("mix", n)) for n in out_tiles], axis=1)
        st.pop("merged")
        h1 = _layer_norm(ha_ref[...] + mix, l1g_ref[...], l1b_ref[...])
        st["h1b"] = h1.astype(BF16)
        for n in out_tiles:
            st["acc", n] = ALPHA * h1[:, n * mw:(n + 1) * mw]

    def activation(c, n):
        lo = bounds[c] + n * mw
        gate = _dot(st["h1b"], wgu_ref[:, lo:lo + mw])
        up = _dot(st["h1b"], wgu_ref[:, d_ff + lo:d_ff + lo + mw])
        st["act", c, n] = (gate / (1.0 + jnp.exp(-gate)) * up).astype(BF16)

    def down(c, n):
        tiles = range((bounds[c + 1] - bounds[c]) // mw)
        if n == 0:
            st["act", c] = jnp.concatenate([st.pop(("act", c, k)) for k in tiles], axis=1)
        st["acc", n] = st["acc", n] + _dot(st["act", c], wd_ref[bounds[c]:bounds[c + 1],
                                                               n * mw:(n + 1) * mw])

    def second_norm():
        y = jnp.concatenate([st.pop(("acc", n)) for n in out_tiles], axis=1)
        o_ref[...] = _layer_norm(y, l2g_ref[...], l2b_ref[...])

    act_thunks = [[functools.partial(activation, c, n)
                   for n in range((bounds[c + 1] - bounds[c]) // mw)] for c in range(n_chunks)]
    down_thunks = [[functools.partial(down, c, n) for n in out_tiles] for c in range(n_chunks)]
    ffn = list(act_thunks[0])
    for c in range(1, n_chunks):
        ffn += _interleave(act_thunks[c], down_thunks[c - 1])
    ffn += down_thunks[-1]
    return (merged, [functools.partial(out_projection, n) for n in out_tiles], first_norm,
            ffn, second_norm)


def _attn_tail_kernel(tiles_per_seq, last_tile, sinks_ref, relb_ref, q_ref, kvc_ref, kvp_ref,
                      ha_ref, sb_ref, sbg_ref, swg_ref, wo_ref, l1g_ref, l1b_ref,
                      wgu_ref, wd_ref, l2g_ref, l2b_ref, o_ref, bias_ref, sw_ref):
    t = pl.program_id(0)

    @pl.when(t == 0)
    def _first_step():
        _swa_fill_bias(relb_ref, bias_ref)
        sw_ref[1] = jnp.zeros(sw_ref.shape[1:], sw_ref.dtype)

    no_previous = (jnp.minimum(t, last_tile) % tiles_per_seq) == 0
    attention = _swa_schedule(sinks_ref, q_ref.at[0], kvc_ref.at[0], kvp_ref.at[0], bias_ref,
                              sw_ref.at[t % 2], no_previous)
    merged, out_proj, first_norm, ffn, second_norm = _tail_schedule(
        ha_ref, sb_ref, sw_ref.at[(t + 1) % 2], sbg_ref, swg_ref, wo_ref,
        l1g_ref, l1b_ref, wgu_ref, wd_ref, l2g_ref, l2b_ref, o_ref)
    tail = [merged] + out_proj + [first_norm] + ffn + [second_norm]
    for thunk in _interleave(tail, attention):
        thunk()


def _attn_tail_block(q_sw, kv_sw, sinks, rel_bias, ha, sb2, sb_g, sw_g, w_out_b, ln1_g, ln1_b,
                     w_gate_up_b, w_down_b, ln2_g, ln2_b):
    b, s, wd = q_sw.shape
    n, d = ha.shape
    tiles_per_seq = s // TAIL_ROWS
    last_tile = n // TAIL_ROWS - 1
    key_tiles = TAIL_ROWS // ATT_BLOCK
    att = lambda t: jnp.minimum(t, last_tile)
    cur = lambda t: (att(t) // tiles_per_seq, att(t) % tiles_per_seq, 0)
    prev = lambda t: (att(t) // tiles_per_seq,
                      jnp.maximum(att(t) % tiles_per_seq * key_tiles - 1, 0), 0)
    row = lambda t: (jnp.maximum(t - 1, 0), 0)
    fixed = lambda t: (0, 0)
    smem = pl.BlockSpec(memory_space=pltpu.SMEM)
    vec = lambda a: pl.BlockSpec((1, a.shape[-1]), fixed)
    weight = lambda a: pl.BlockSpec(a.shape, fixed, pipeline_mode=pl.Buffered(1))
    return pl.pallas_call(
        functools.partial(_attn_tail_kernel, tiles_per_seq, last_tile),
        grid=(last_tile + 2,),
        in_specs=[
            smem, smem,
            pl.BlockSpec((1, TAIL_ROWS, wd), cur),
            pl.BlockSpec((1, TAIL_ROWS, kv_sw.shape[-1]), cur),
            pl.BlockSpec((1, ATT_BLOCK, kv_sw.shape[-1]), prev),
            pl.BlockSpec((TAIL_ROWS, d), row),
            pl.BlockSpec((TAIL_ROWS, sb2.shape[-1]), row),
            vec(sb_g), vec(sw_g), weight(w_out_b), vec(ln1_g), vec(ln1_b),
            weight(w_gate_up_b), weight(w_down_b), vec(ln2_g), vec(ln2_b),
        ],
        out_specs=pl.BlockSpec((TAIL_ROWS, d), row),
        out_shape=jax.ShapeDtypeStruct((n, d), F32),
        scratch_shapes=[
            pltpu.VMEM((SWA_HEADS, ATT_BLOCK, 2 * ATT_BLOCK), F32),
            pltpu.VMEM((2, TAIL_ROWS, wd), BF16),
        ],
        compiler_params=pltpu.CompilerParams(
            dimension_semantics=("arbitrary",), vmem_limit_bytes=VMEM_LIMIT),
        name="swa_attention_and_dense_tail",
    )(sinks, rel_bias, q_sw, kv_sw, kv_sw, ha, sb2, sb_g, sw_g, w_out_b, ln1_g, ln1_b,
      w_gate_up_b, w_down_b, ln2_g, ln2_b)


SB_GROUP = 2
SB_LIVE_DONE = 1e30


def _sb_fast_masks(mask_ref):
    tq, w = SB_Q_TILE, SB_WINDOW
    row = lax.broadcasted_iota(jnp.int32, (2 * tq, w), 0) & (tq - 1)
    col = lax.broadcasted_iota(jnp.int32, (2 * tq, w), 1)
    causal = col < row + (w - tq)
    mask_ref[0, 0], mask_ref[0, 1] = _sb_masks(causal)
    for h in range(mask_ref.shape[0] - 1):
        first_key = w - (h + 1) * tq
        mask_ref[1 + h, 0], mask_ref[1 + h, 1] = _sb_masks(
            jnp.logical_and(causal, col >= first_key))


def _sb_fast_schedule(q_ref, kc_ref, kp_ref, vc_ref, vp_ref, tri_ref, mask_ref, o_ref,
                      no_previous, lives):
    tq, w = SB_Q_TILE, SB_WINDOW
    subs = q_ref.shape[0] // tq
    pairs = q_ref.shape[-1] // LANES
    clamped = w // tq - 1
    assert mask_ref.shape[0] == clamped + 1 and subs % SB_GROUP == 0
    lane = lax.broadcasted_iota(jnp.int32, (tq, LANES), 1)
    lanes_of = lambda p: slice(p * LANES, (p + 1) * LANES)
    rows_of = lambda h: slice(h * tq, (h + 1) * tq)

    def window(h, cur_ref, prev_ref, p):
        start = (h + 1) * tq - w
        if start >= 0:
            return cur_ref[start:start + w, lanes_of(p)]
        return jnp.concatenate([prev_ref[w + start:w, lanes_of(p)],
                                cur_ref[0:w + start, lanes_of(p)]], axis=0)

    def mask(h, which):
        if h >= clamped:
            return mask_ref[0, which]
        return mask_ref[jnp.where(no_previous, 1 + h, 0), which]

    items = [(p, g) for p in range(pairs) for g in range(subs // SB_GROUP)]
    group = lambda g: range(g * SB_GROUP, (g + 1) * SB_GROUP)

    def scores(i):
        p, g = items[i]
        return [_sb_scores(_split_heads(q_ref[rows_of(h), lanes_of(p)], lane),
                           window(h, kc_ref, kp_ref, p), mask(h, 0)) for h in group(g)]

    def weights(i, sc):
        return _sb_weights(sc, tri_ref[...], [mask(h, 1) for h in group(items[i][1])],
                           [None] * SB_GROUP)

    def values(i, aws):
        p, g = items[i]
        for h, (a, total) in zip(group(g), aws):
            o2 = _dot(a, window(h, vc_ref, vp_ref, p))
            o_ref[rows_of(h), lanes_of(p)] = _merge_heads(o2, lane).astype(o_ref.dtype)
            m = jnp.min(total[:, 0:1])
            lives.append(m if h > clamped else jnp.where(no_previous, SB_LIVE_DONE, m))

    return _pipeline_schedule(len(items), [scores, weights, values])[0]


def _fast_kernel(tiles_per_seq, last_tile, sinks_ref, relb_ref, qsw_ref, kvc_ref, kvp_ref,
                 qsb_ref, kc_ref, kp_ref, vc_ref, vp_ref,
                 ha_ref, sbg_ref, swg_ref, wo_ref, l1g_ref, l1b_ref,
                 wgu_ref, wd_ref, l2g_ref, l2b_ref, o_ref, live_ref,
                 bias_ref, sw_ref, sb_ref, tri_ref, mask_ref):
    t = pl.program_id(0)

    @pl.when(t == 0)
    def _first_step():
        _swa_fill_bias(relb_ref, bias_ref)
        _sb_fast_masks(mask_ref)
        j = lax.broadcasted_iota(jnp.int32, tri_ref.shape, 0)
        s = lax.broadcasted_iota(jnp.int32, tri_ref.shape, 1)
        tri_ref[...] = jnp.where(j > s, 1.0, 0.0).astype(BF16)
        sw_ref[1] = jnp.zeros(sw_ref.shape[1:], sw_ref.dtype)
        sb_ref[1] = jnp.zeros(sb_ref.shape[1:], sb_ref.dtype)

    no_previous = (jnp.minimum(t, last_tile) % tiles_per_seq) == 0
    lives = []
    swa = _swa_schedule(sinks_ref, qsw_ref.at[0], kvc_ref.at[0], kvp_ref.at[0], bias_ref,
                        sw_ref.at[t % 2], no_previous)
    sb = _sb_fast_schedule(qsb_ref.at[0], kc_ref.at[0], kp_ref.at[0], vc_ref.at[0], vp_ref.at[0],
                           tri_ref, mask_ref, sb_ref.at[t % 2], no_previous, lives)
    merged, out_proj, first_norm, ffn, second_norm = _tail_schedule(
        ha_ref, sb_ref.at[(t + 1) % 2], sw_ref.at[(t + 1) % 2], sbg_ref, swg_ref, wo_ref,
        l1g_ref, l1b_ref, wgu_ref, wd_ref, l2g_ref, l2b_ref, o_ref)
    tail = [merged] + out_proj + [first_norm] + ffn + [second_norm]
    for thunk in _interleave(tail, _interleave(swa, sb)):
        thunk()
    live_ref[...] = jnp.full(live_ref.shape, functools.reduce(jnp.minimum, lives), F32)


def _fast_block(q_sw, kv_sw, q_sb, k_sb, v_sb, sinks, rel_bias, ha, sb_g, sw_g, w_out_b,
                ln1_g, ln1_b, w_gate_up_b, w_down_b, ln2_g, ln2_b):
    b, s, wd = q_sw.shape
    n, d = ha.shape
    tiles_per_seq = s // TAIL_ROWS
    last_tile = n // TAIL_ROWS - 1
    att = lambda t: jnp.minimum(t, last_tile)
    cur = lambda t: (att(t) // tiles_per_seq, att(t) % tiles_per_seq, 0)

    def prev(rows):
        per_tile = TAIL_ROWS // rows
        return lambda t: (att(t) // tiles_per_seq,
                          jnp.maximum(att(t) % tiles_per_seq * per_tile - 1, 0), 0)

    row = lambda t: (jnp.maximum(t - 1, 0), 0)
    fixed = lambda t: (0, 0)
    smem = pl.BlockSpec(memory_space=pltpu.SMEM)
    vec = lambda a: pl.BlockSpec((1, a.shape[-1]), fixed)
    weight = lambda a: pl.BlockSpec(a.shape, fixed, pipeline_mode=pl.Buffered(1))
    tile3 = lambda a: pl.BlockSpec((1, TAIL_ROWS, a.shape[-1]), cur)
    before = lambda a, rows: pl.BlockSpec((1, rows, a.shape[-1]), prev(rows))
    steps = last_tile + 2
    return pl.pallas_call(
        functools.partial(_fast_kernel, tiles_per_seq, last_tile),
        grid=(steps,),
        in_specs=[
            smem, smem,
            tile3(q_sw), tile3(kv_sw), before(kv_sw, ATT_BLOCK),
            tile3(q_sb), tile3(k_sb), before(k_sb, SB_WINDOW), tile3(v_sb), before(v_sb, SB_WINDOW),
            pl.BlockSpec((TAIL_ROWS, d), row),
            vec(sb_g), vec(sw_g), weight(w_out_b), vec(ln1_g), vec(ln1_b),
            weight(w_gate_up_b), weight(w_down_b), vec(ln2_g), vec(ln2_b),
        ],
        out_specs=[pl.BlockSpec((TAIL_ROWS, d), row),
                   pl.BlockSpec((1, 8, LANES), lambda t: (t, 0, 0))],
        out_shape=[jax.ShapeDtypeStruct((n, d), F32),
                   jax.ShapeDtypeStruct((steps, 8, LANES), F32)],
        scratch_shapes=[
            pltpu.VMEM((SWA_HEADS, ATT_BLOCK, 2 * ATT_BLOCK), F32),
            pltpu.VMEM((2, TAIL_ROWS, wd), BF16),
            pltpu.VMEM((2, TAIL_ROWS, q_sb.shape[-1]), BF16),
            pltpu.VMEM((SB_WINDOW, SB_WINDOW), BF16),
            pltpu.VMEM((SB_WINDOW // SB_Q_TILE, 2, 2 * SB_Q_TILE, SB_WINDOW), F32),
        ],
        compiler_params=pltpu.CompilerParams(
            dimension_semantics=("arbitrary",), vmem_limit_bytes=VMEM_LIMIT),
        name="attention_and_dense_tail",
    )(sinks, rel_bias, q_sw, kv_sw, kv_sw, q_sb, k_sb, k_sb, v_sb, v_sb, ha, sb_g, sw_g, w_out_b,
      ln1_g, ln1_b, w_gate_up_b, w_down_b, ln2_g, ln2_b)


def kernel(x, ln_in_g, ln_in_b, w_in, sb_norm_g, swa_norm_g, sinks, rel_bias, w_out,
           ln1_g, ln1_b, w_gate_up, w_down, ln2_g, ln2_b):
    b, s, d = x.shape
    assert w_in.shape[0] == DEPTH == 1
    sb_w = SB_HEADS * HEAD_DIM
    sw_w = SWA_HEADS * HEAD_DIM
    kv_w = SWA_KV_HEADS * HEAD_DIM
    assert w_in.shape[-1] == 3 * sb_w + sw_w + 2 * kv_w
    assert kv_w == LANES and SWA_HEADS // SWA_KV_HEADS == 4
    assert s % SB_WINDOW == 0 and s % SB_STEP_ROWS == 0 and s % TAIL_ROWS == 0
    assert (b * s) % ROW_TILE == 0 and WINDOW <= ATT_BLOCK and d % MXU_WIDTH == 0
    assert FFN_CHUNK % MXU_WIDTH == 0 and w_down.shape[1] % MXU_WIDTH == 0

    x2 = x.reshape(b * s, d)
    row_vec = lambda a: a.reshape(1, -1)

    names = ("q_sb", "k_sb", "v_sb", "q_sw", "k_sw", "v_sw")
    widths = (sb_w, sb_w, sb_w, sw_w, kv_w, kv_w)
    splits, lo = {}, 0
    for name, wd in zip(names, widths):
        splits[name] = (lo, lo + wd)
        lo += wd
    ha, q_sb, k_sb, v_sb, q_sw, kv_sw, w_out_b, w_gate_up_b, w_down_b = _in_projection(
        x2, row_vec(ln_in_g), row_vec(ln_in_b), w_in[0], splits,
        [w_out[0], w_gate_up[0], w_down[0]])

    to3 = lambda a: a.reshape(b, s, a.shape[-1])
    q_sb, k_sb, v_sb, q_sw, kv_sw = map(to3, (q_sb, k_sb, v_sb, q_sw, kv_sw))
    tail_args = (row_vec(sb_norm_g[0]), row_vec(swa_norm_g[0]), w_out_b,
                 row_vec(ln1_g[0]), row_vec(ln1_b[0]), w_gate_up_b, w_down_b,
                 row_vec(ln2_g[0]), row_vec(ln2_b[0]))

    fast, live = _fast_block(q_sw, kv_sw, q_sb, k_sb, v_sb, sinks[0], rel_bias, ha, *tail_args)
    n_tiles = (b * s) // TAIL_ROWS

    def general(_):
        (sb_out,) = _sb_attention(q_sb, k_sb, v_sb, [])
        return _attn_tail_block(q_sw, kv_sw, sinks[0], rel_bias, ha,
                                sb_out.reshape(b * s, sb_w), *tail_args)

    out = lax.cond(jnp.any(live[:n_tiles, 0, 0] < SB_DEAD_LOG2), general, lambda _: fast, None)
    return out.reshape(b, s, d)
```

```python
import functools
import math

import jax
import jax.numpy as jnp
import numpy as np
from jax import lax
from jax.experimental import pallas as pl
from jax.experimental.pallas import tpu as pltpu

F32 = jnp.float32
BF16 = jnp.bfloat16

HEAD_DIM = 64
SB_HEADS = 8
SWA_HEADS = 8
SWA_KV_HEADS = 2
WINDOW = 128
REL_BUCKETS = 32
REL_MAX_DIST = 128
LN_EPS = 1e-5
RMS_EPS = 1e-6
DEPTH = 1
ALPHA = (2 * DEPTH) ** 0.25
QK_SCALE = HEAD_DIM ** -0.5
LOG2E = math.log2(math.e)

LANES = 128
BF16_ROWS = 16
MXU_WIDTH = 256
ATT_BLOCK = 128
SB_WINDOW = 256
SB_Q_TILE = 64
SB_STEP_ROWS = 512
ROW_TILE = 1024
PROJ_PARTS = 2
TAIL_ROWS = 512
FFN_CHUNK = 4 * MXU_WIDTH
VMEM_LIMIT = 56 * 1024 * 1024

SB_DEAD_LOG2 = 127.0
MASK_NEG = -1e30


def _layer_norm(x, g, b):
    mu = jnp.mean(x, axis=-1, keepdims=True)
    xc = x - mu
    var = jnp.mean(xc * xc, axis=-1, keepdims=True)
    return xc * lax.rsqrt(var + LN_EPS) * g + b


def _rms_norm(x, g):
    ms = jnp.mean(x * x, axis=-1, keepdims=True)
    return x * lax.rsqrt(ms + RMS_EPS) * g


def _dot(a, b):
    return jnp.dot(a, b, preferred_element_type=F32)


def _dot_nt(a, b):
    return lax.dot_general(a, b, (((1,), (1,)), ((), ())), preferred_element_type=F32)


def _pipeline_schedule(n, stages):
    vals = [dict() for _ in stages]
    thunks = []
    for step in range(n + len(stages) - 1):
        for s, stage in enumerate(stages):
            i = step - s
            if 0 <= i < n:
                def thunk(s=s, i=i, stage=stage):
                    vals[s][i] = stage(i, vals[s - 1].pop(i)) if s else stage(i)
                thunks.append(thunk)
    return thunks, vals[-1]


def _software_pipeline(n, stages):
    thunks, results = _pipeline_schedule(n, stages)
    for thunk in thunks:
        thunk()
    return results


def _interleave(xs, ys):
    merged, taken = [], 0
    for i, x in enumerate(xs):
        merged.append(x)
        upto = (i + 1) * len(ys) // len(xs)
        merged.extend(ys[taken:upto])
        taken = upto
    return merged


def _inproj_kernel(splits, n_later, x_ref, g_ref, b_ref, w32_ref, *refs):
    later_in, refs = refs[:n_later], refs[n_later:]
    ha_ref, qsb_ref, ksb_ref, vsb_ref, qsw_ref, kv_ref = refs[:6]
    later_out, w_ref = refs[6:6 + n_later], refs[6 + n_later]
    rows = x_ref.shape[0] // PROJ_PARTS
    part = lambda i: slice(i * rows, (i + 1) * rows)
    assert splits["k_sw"][1] == splits["v_sw"][0]

    @pl.when(pl.program_id(0) == 0)
    def _convert_weight():
        w_ref[...] = w32_ref[...].astype(BF16)

    for src, dst in zip(later_in, later_out):
        dst[...] = src[...].astype(BF16)

    def normed(i):
        h32 = _layer_norm(x_ref[part(i), :], g_ref[...], b_ref[...])
        ha_ref[part(i), :] = ALPHA * h32
        return h32.astype(BF16)

    def projected(i, h):
        r = part(i)
        proj = lambda lo, hi: _dot(h, w_ref[:, lo:hi])
        qsb_ref[r, :] = (proj(*splits["q_sb"]) * QK_SCALE).astype(BF16)
        ksb_ref[r, :] = proj(*splits["k_sb"]).astype(BF16)
        vsb_ref[r, :] = proj(*splits["v_sb"]).astype(BF16)
        qsw_ref[r, :] = (proj(*splits["q_sw"]) * QK_SCALE).astype(BF16)
        kv = proj(splits["k_sw"][0], splits["v_sw"][1])
        kvw = kv.shape[-1] // 2
        k, v = kv[:, :kvw], kv[:, kvw:]
        kv_ref[r, 0 * kvw:1 * kvw] = k.astype(BF16)
        kv_ref[r, 1 * kvw:2 * kvw] = pltpu.roll(k, HEAD_DIM, axis=1).astype(BF16)
        kv_ref[r, 2 * kvw:3 * kvw] = v.astype(BF16)
        kv_ref[r, 3 * kvw:4 * kvw] = pltpu.roll(v, HEAD_DIM, axis=1).astype(BF16)

    _software_pipeline(PROJ_PARTS, [normed, projected])


def _in_projection(x2, g, b, w_in, splits, later_weights):
    n, d = x2.shape
    steps = n // ROW_TILE
    width = lambda name: splits[name][1] - splits[name][0]
    out_w = [width("q_sb"), width("k_sb"), width("v_sb"), width("q_sw"), 4 * width("k_sw")]
    row = lambda i: (i, 0)
    fixed = lambda i: (0, 0)
    later_specs = [
        pl.BlockSpec((size, w.shape[1]),
                     lambda i, last=w.shape[0] // size - 1: (jnp.minimum(i, last), 0))
        for w, size in zip(later_weights, _conversion_blocks(later_weights, steps))]
    return pl.pallas_call(
        functools.partial(_inproj_kernel, splits, len(later_weights)),
        grid=(steps,),
        in_specs=[
            pl.BlockSpec((ROW_TILE, d), row),
            pl.BlockSpec((1, d), fixed),
            pl.BlockSpec((1, d), fixed),
            pl.BlockSpec(w_in.shape, fixed, pipeline_mode=pl.Buffered(1)),
        ] + later_specs,
        out_specs=[pl.BlockSpec((ROW_TILE, d), row)]
        + [pl.BlockSpec((ROW_TILE, w), row) for w in out_w] + later_specs,
        out_shape=[jax.ShapeDtypeStruct((n, d), F32)]
        + [jax.ShapeDtypeStruct((n, w), BF16) for w in out_w]
        + [jax.ShapeDtypeStruct(w.shape, BF16) for w in later_weights],
        scratch_shapes=[pltpu.VMEM(w_in.shape, BF16)],
        compiler_params=pltpu.CompilerParams(
            dimension_semantics=("arbitrary",), vmem_limit_bytes=VMEM_LIMIT),
        name="ln_in_proj",
    )(x2, g, b, w_in, *later_weights)


def _conversion_blocks(weights, steps):
    sizes = []
    for w in weights:
        rows = w.shape[0]
        sizes.append(next(r for r in range(BF16_ROWS, rows + 1, BF16_ROWS)
                          if rows % r == 0 and rows // r <= steps))
    return sizes


def _split_heads(x, lane):
    zero = jnp.zeros_like(x)
    return jnp.concatenate([jnp.where(lane < HEAD_DIM, x, zero),
                            jnp.where(lane >= HEAD_DIM, x, zero)], axis=0)


def _merge_heads(o2, lane):
    t = o2.shape[0] // 2
    return jnp.where(lane < HEAD_DIM, o2[:t], o2[t:])


def _sb_masks(valid):
    return jnp.where(valid, LOG2E, 0.0), jnp.where(valid, 0.0, MASK_NEG)


def _sb_scores(qq, k, keep):
    z = _dot_nt(qq, k)
    zl = z * LOG2E
    c_nat = jnp.maximum(z, 0.0) + jnp.log(1.0 + jnp.exp2(-jnp.abs(zl)))
    cm = c_nat * keep
    return zl - cm, cm.astype(BF16), cm[:, :LANES]


def _sb_weights(scores, tri, kills, carrieds):
    sums = _dot(jnp.concatenate([sc[1] for sc in scores], axis=0), tri)
    rows = scores[0][0].shape[0]
    out = []
    for n, ((log_beta, _, cm0), kill, carried) in enumerate(zip(scores, kills, carrieds)):
        later = sums[n * rows:(n + 1) * rows]
        log_a = (log_beta - later) + kill
        if carried is not None:
            log_a = log_a - jnp.concatenate([carried] * (log_a.shape[1] // LANES), axis=1)
        out.append((jnp.exp2(log_a).astype(BF16), later[:, :LANES] + cm0))
    return out


def _sb_kernel(n_conv, q_ref, k_ref, v_ref, *refs):
    conv_in, o_ref, conv_out = refs[:n_conv], refs[n_conv], refs[n_conv + 1:2 * n_conv + 1]
    tri_ref, mask_ref, acc_ref, car_ref = refs[2 * n_conv + 1:]
    for src, dst in zip(conv_in, conv_out):
        dst[...] = src[...].astype(BF16)

    bi = pl.program_id(0)
    qi = pl.program_id(1)
    tq, w = SB_Q_TILE, SB_WINDOW
    subs = q_ref.shape[1] // tq
    pairs = q_ref.shape[-1] // LANES
    row0 = qi * (subs * tq)

    @pl.when(jnp.logical_and(bi == 0, qi == 0))
    def _fill_tri():
        j = lax.broadcasted_iota(jnp.int32, (w, w), 0)
        s = lax.broadcasted_iota(jnp.int32, (w, w), 1)
        tri_ref[...] = jnp.where(j > s, 1.0, 0.0).astype(BF16)

    lane = lax.broadcasted_iota(jnp.int32, (tq, LANES), 1)
    lanes_of = lambda p: slice(p * LANES, (p + 1) * LANES)
    rows_of = lambda h: slice(h * tq, (h + 1) * tq)
    first_start = lambda h: pl.multiple_of(jnp.maximum(row0 + (h + 1) * tq - w, 0), tq)

    def key_positions(start):
        return start + lax.broadcasted_iota(jnp.int32, (2 * tq, w), 1)

    assert subs * tq >= w - tq

    @pl.when(qi <= 1)
    def _fill_masks():
        row = lax.broadcasted_iota(jnp.int32, (2 * tq, w), 0) & (tq - 1)
        for h in range(subs):
            mask_ref[h, 0], mask_ref[h, 1] = _sb_masks(
                key_positions(first_start(h)) < row0 + h * tq + row)

    def scores(p, starts, keeps):
        return [_sb_scores(_split_heads(q_ref[0, rows_of(h), lanes_of(p)], lane),
                           k_ref[0, pl.ds(starts[h], w), lanes_of(p)], keeps[h])
                for h in range(subs)]

    first_starts = [first_start(h) for h in range(subs)]

    def first_scores(p):
        return scores(p, first_starts, [mask_ref[h, 0] for h in range(subs)])

    def first_weights(p, sc):
        return _sb_weights(sc, tri_ref[...], [mask_ref[h, 1] for h in range(subs)], [None] * subs)

    def first_values(p, aws):
        least = None
        for h, (a, total) in enumerate(aws):
            o2 = _dot(a, v_ref[0, pl.ds(first_starts[h], w), lanes_of(p)])
            o_ref[0, rows_of(h), lanes_of(p)] = _merge_heads(o2, lane).astype(o_ref.dtype)
            acc_ref[p, h] = o2
            car_ref[p, h] = total
            m = jnp.min(total[:, 0:1])
            least = m if least is None else jnp.minimum(least, m)
        return least

    first = _software_pipeline(pairs, [first_scores, first_weights, first_values])
    live = functools.reduce(jnp.minimum, [first[p] for p in range(pairs)])
    last_start = first_starts[-1]

    @pl.when(jnp.logical_and(last_start > 0, live < SB_DEAD_LOG2))
    def _walk_back():
        for p in range(pairs):
            for h in range(subs):
                car_ref[p, h] = jnp.broadcast_to(car_ref[p, h][:, 0:1], (2 * tq, LANES))

        def body(state):
            step, _ = state
            done = [jnp.maximum(first_start(h) - step * w, 0) for h in range(subs)]
            starts = [pl.multiple_of(jnp.maximum(d - w, 0), tq) for d in done]
            masks = [_sb_masks(key_positions(s) < d) for s, d in zip(starts, done)]

            def more_scores(p):
                return scores(p, starts, [m[0] for m in masks])

            def weights(p, sc):
                return _sb_weights(sc, tri_ref[...], [m[1] for m in masks],
                                   [car_ref[p, h] for h in range(subs)])

            def values(p, aws):
                least = None
                for h, (a, total) in enumerate(aws):
                    acc_ref[p, h] += _dot(a, v_ref[0, pl.ds(starts[h], w), lanes_of(p)])
                    carried = car_ref[p, h] + total[:, 0:1]
                    car_ref[p, h] = carried
                    m = jnp.min(carried)
                    least = m if least is None else jnp.minimum(least, m)
                return least

            mins = _software_pipeline(pairs, [more_scores, weights, values])
            return step + 1, functools.reduce(jnp.minimum, [mins[p] for p in range(pairs)])

        def cond(state):
            step, nxt = state
            return jnp.logical_and(last_start - step * w > 0, nxt < SB_DEAD_LOG2)

        lax.while_loop(cond, body, (jnp.int32(0), live))
        for p in range(pairs):
            for h in range(subs):
                o_ref[0, rows_of(h), lanes_of(p)] = _merge_heads(
                    acc_ref[p, h], lane).astype(o_ref.dtype)


def _sb_attention(q, k, v, weights):
    b, s, wd = q.shape
    pairs = wd // LANES
    subs = SB_STEP_ROWS // SB_Q_TILE
    nq = s // SB_STEP_ROWS
    tile = lambda bi, i: (bi, i, 0)
    whole = lambda bi, i: (bi, 0, 0)
    once = pl.Buffered(1)
    conv_specs = [
        pl.BlockSpec((size, w.shape[1]),
                     lambda bi, i, last=w.shape[0] // size - 1: (jnp.minimum(bi * nq + i, last), 0))
        for w, size in zip(weights, _conversion_blocks(weights, b * nq))]
    return pl.pallas_call(
        functools.partial(_sb_kernel, len(weights)),
        grid=(b, nq),
        in_specs=[
            pl.BlockSpec((1, SB_STEP_ROWS, wd), tile),
            pl.BlockSpec((1, s, wd), whole, pipeline_mode=once),
            pl.BlockSpec((1, s, wd), whole, pipeline_mode=once),
        ] + conv_specs,
        out_specs=[pl.BlockSpec((1, SB_STEP_ROWS, wd), tile)] + conv_specs,
        out_shape=[jax.ShapeDtypeStruct((b, s, wd), BF16)]
        + [jax.ShapeDtypeStruct(w.shape, BF16) for w in weights],
        scratch_shapes=[
            pltpu.VMEM((SB_WINDOW, SB_WINDOW), BF16),
            pltpu.VMEM((subs, 2, 2 * SB_Q_TILE, SB_WINDOW), F32),
            pltpu.VMEM((pairs, subs, 2 * SB_Q_TILE, LANES), F32),
            pltpu.VMEM((pairs, subs, 2 * SB_Q_TILE, LANES), F32),
        ],
        compiler_params=pltpu.CompilerParams(
            dimension_semantics=("arbitrary", "arbitrary"), vmem_limit_bytes=VMEM_LIMIT),
        name="stick_breaking_attention",
    )(q, k, v, *weights)


def _t5_log_bucket_starts():
    exact = REL_BUCKETS // 2
    d = np.arange(exact, 2 * REL_MAX_DIST, dtype=np.float32)
    large = exact + (np.log(d / np.float32(exact)) / np.float32(math.log(REL_MAX_DIST / exact))
                     * np.float32(REL_BUCKETS - exact)).astype(np.int32)
    large = np.minimum(large, REL_BUCKETS - 1)
    assert large[0] == exact and np.all(np.diff(large) >= 0)
    return [int(d[np.argmax(large >= bk)]) for bk in range(exact + 1, REL_BUCKETS)]


def _t5_causal_bucket(distance):
    d = jnp.maximum(distance, 0)
    bucket = jnp.minimum(d, REL_BUCKETS // 2)
    for start in _t5_log_bucket_starts():
        bucket = bucket + jnp.where(d >= start, 1, 0)
    return bucket


def _swa_fill_bias(relb_ref, bias_ref):
    t = ATT_BLOCK
    r = lax.broadcasted_iota(jnp.int32, (t, 2 * t), 0)
    c = lax.broadcasted_iota(jnp.int32, (t, 2 * t), 1)
    dist = r + t - c
    bucket = _t5_causal_bucket(dist)
    valid = jnp.logical_and(dist >= 0, dist < WINDOW)
    for h in range(SWA_HEADS):
        bias = jnp.zeros((t, 2 * t), F32)
        for bk in range(REL_BUCKETS):
            bias = jnp.where(bucket == bk, relb_ref[bk, h], bias)
        bias_ref[h] = jnp.where(valid, bias, MASK_NEG)


def _swa_schedule(sinks_ref, q_ref, kvc_ref, kvp_ref, bias_ref, o_ref, no_previous):
    t = ATT_BLOCK
    kvw = kvc_ref.shape[-1] // 4
    group = SWA_HEADS // SWA_KV_HEADS
    tiles = q_ref.shape[0] // t
    rows_of = lambda n: slice(n * t, (n + 1) * t)
    lane = lax.broadcasted_iota(jnp.int32, (t, LANES), 1)
    row2 = lax.broadcasted_iota(jnp.int32, (2 * t, 1), 0)
    col2 = lax.broadcasted_iota(jnp.int32, (2 * t, 2 * t), 1)
    prev_pen = jnp.where(jnp.logical_and(col2 < t, no_previous), MASK_NEG, 0.0)

    def kv_window(n, variant):
        cols = slice(variant * kvw, (variant + 1) * kvw)
        before = kvc_ref[rows_of(n - 1), cols] if n else kvp_ref[:, cols]
        return jnp.concatenate([before, kvc_ref[rows_of(n), cols]], axis=0)

    items = [(n, half, j) for n in range(tiles) for half in range(2) for j in range(SWA_KV_HEADS)]
    heads_of = lambda half, j: [h for h in range(j * group, (j + 1) * group) if h % 2 == half]

    def probabilities(i):
        n, half, j = items[i]
        heads = heads_of(half, j)
        in_half = (lane >= HEAD_DIM) if half else (lane < HEAD_DIM)
        qq = jnp.concatenate(
            [jnp.where(in_half, q_ref[rows_of(n), (h // 2) * LANES:(h // 2 + 1) * LANES], 0)
             for h in heads], axis=0)
        s = _dot_nt(qq, kv_window(n, 0 + int(j != half)))
        s = s + jnp.concatenate([bias_ref[h] for h in heads], axis=0)
        if n == 0:
            s = s + prev_pen
        sink = jnp.where(row2 < t, sinks_ref[heads[0]], sinks_ref[heads[1]])
        m = jnp.maximum(jnp.max(s, axis=-1, keepdims=True), sink)
        p = jnp.exp(s - m)
        denom = jnp.sum(p, axis=-1, keepdims=True) + jnp.exp(sink - m)
        return p.astype(BF16), 1.0 / denom

    def values(i, pd):
        n, half, j = items[i]
        p, inv_denom = pd
        return _dot(p, kv_window(n, 2 + int(j != half))) * inv_denom

    thunks, outs = _pipeline_schedule(len(items), [probabilities, values])

    def store():
        for i, (n, half, j) in enumerate(items):
            if half == 0:
                even, odd = outs[i], outs[items.index((n, 1, j))]
                for e, h in enumerate(heads_of(0, j)):
                    o_ref[rows_of(n), (h // 2) * LANES:(h // 2 + 1) * LANES] = jnp.where(
                        lane < HEAD_DIM, even[e * t:(e + 1) * t], odd[e * t:(e + 1) * t]
                    ).astype(o_ref.dtype)

    return thunks + [store]


def _tail_schedule(ha_ref, sb_ref, sw_ref, sbg_ref, swg_ref, wo_ref, l1g_ref, l1b_ref,
                   wgu_ref, wd_ref, l2g_ref, l2b_ref, o_ref):
    d = ha_ref.shape[1]
    d_ff = wd_ref.shape[0]
    mw = MXU_WIDTH
    bounds = list(range(0, d_ff, FFN_CHUNK)) + [d_ff]
    n_chunks = len(bounds) - 1
    out_tiles = range(d // mw)
    st = {}

    def merged():
        sb = _rms_norm(sb_ref[...].astype(F32), sbg_ref[...]).astype(BF16)
        sw = _rms_norm(sw_ref[...].astype(F32), swg_ref[...]).astype(BF16)
        st["merged"] = jnp.concatenate([sb, sw], axis=1)

    def out_projection(n):
        st["mix", n] = _dot(st["merged"], wo_ref[:, n * mw:(n + 1) * mw])

    def first_norm():
        mix = jnp.concatenate([st.pop(("mix", n)) for n in out_tiles], axis=1)
        st.pop("merged")
        h1 = _layer_norm(ha_ref[...] + mix, l1g_ref[...], l1b_ref[...])
        st["h1b"] = h1.astype(BF16)
        for n in out_tiles:
            st["acc", n] = ALPHA * h1[:, n * mw:(n + 1) * mw]

    def activation(c, n):
        lo = bounds[c] + n * mw
        gate = _dot(st["h1b"], wgu_ref[:, lo:lo + mw])
        up = _dot(st["h1b"], wgu_ref[:, d_ff + lo:d_ff + lo + mw])
        st["act", c, n] = (gate / (1.0 + jnp.exp(-gate)) * up).astype(BF16)

    def down(c, n):
        tiles = range((bounds[c + 1] - bounds[c]) // mw)
        if n == 0:
            st["act", c] = jnp.concatenate([st.pop(("act", c, k)) for k in tiles], axis=1)
        st["acc", n] = st["acc", n] + _dot(st["act", c], wd_ref[bounds[c]:bounds[c + 1],
                                                               n * mw:(n + 1) * mw])

    def second_norm():
        y = jnp.concatenate([st.pop(("acc", n)) for n in out_tiles], axis=1)
        o_ref[...] = _layer_norm(y, l2g_ref[...], l2b_ref[...])

    act_thunks = [[functools.partial(activation, c, n)
                   for n in range((bounds[c + 1] - bounds[c]) // mw)] for c in range(n_chunks)]
    down_thunks = [[functools.partial(down, c, n) for n in out_tiles] for c in range(n_chunks)]
    ffn = list(act_thunks[0])
    for c in range(1, n_chunks):
        ffn += _interleave(act_thunks[c], down_thunks[c - 1])
    ffn += down_thunks[-1]
    return (merged, [functools.partial(out_projection, n) for n in out_tiles], first_norm,
            ffn, second_norm)


def _attn_tail_kernel(tiles_per_seq, last_tile, sinks_ref, relb_ref, q_ref, kvc_ref, kvp_ref,
                      ha_ref, sb_ref, sbg_ref, swg_ref, wo_ref, l1g_ref, l1b_ref,
                      wgu_ref, wd_ref, l2g_ref, l2b_ref, o_ref, bias_ref, sw_ref):
    t = pl.program_id(0)

    @pl.when(t == 0)
    def _first_step():
        _swa_fill_bias(relb_ref, bias_ref)
        sw_ref[1] = jnp.zeros(sw_ref.shape[1:], sw_ref.dtype)

    no_previous = (jnp.minimum(t, last_tile) % tiles_per_seq) == 0
    attention = _swa_schedule(sinks_ref, q_ref.at[0], kvc_ref.at[0], kvp_ref.at[0], bias_ref,
                              sw_ref.at[t % 2], no_previous)
    merged, out_proj, first_norm, ffn, second_norm = _tail_schedule(
        ha_ref, sb_ref, sw_ref.at[(t + 1) % 2], sbg_ref, swg_ref, wo_ref,
        l1g_ref, l1b_ref, wgu_ref, wd_ref, l2g_ref, l2b_ref, o_ref)
    tail = [merged] + out_proj + [first_norm] + ffn + [second_norm]
    for thunk in _interleave(tail, attention):
        thunk()


def _attn_tail_block(q_sw, kv_sw, sinks, rel_bias, ha, sb2, sb_g, sw_g, w_out_b, ln1_g, ln1_b,
                     w_gate_up_b, w_down_b, ln2_g, ln2_b):
    b, s, wd = q_sw.shape
    n, d = ha.shape
    tiles_per_seq = s // TAIL_ROWS
    last_tile = n // TAIL_ROWS - 1
    key_tiles = TAIL_ROWS // ATT_BLOCK
    att = lambda t: jnp.minimum(t, last_tile)
    cur = lambda t: (att(t) // tiles_per_seq, att(t) % tiles_per_seq, 0)
    prev = lambda t: (att(t) // tiles_per_seq,
                      jnp.maximum(att(t) % tiles_per_seq * key_tiles - 1, 0), 0)
    row = lambda t: (jnp.maximum(t - 1, 0), 0)
    fixed = lambda t: (0, 0)
    smem = pl.BlockSpec(memory_space=pltpu.SMEM)
    vec = lambda a: pl.BlockSpec((1, a.shape[-1]), fixed)
    weight = lambda a: pl.BlockSpec(a.shape, fixed, pipeline_mode=pl.Buffered(1))
    return pl.pallas_call(
        functools.partial(_attn_tail_kernel, tiles_per_seq, last_tile),
        grid=(last_tile + 2,),
        in_specs=[
            smem, smem,
            pl.BlockSpec((1, TAIL_ROWS, wd), cur),
            pl.BlockSpec((1, TAIL_ROWS, kv_sw.shape[-1]), cur),
            pl.BlockSpec((1, ATT_BLOCK, kv_sw.shape[-1]), prev),
            pl.BlockSpec((TAIL_ROWS, d), row),
            pl.BlockSpec((TAIL_ROWS, sb2.shape[-1]), row),
            vec(sb_g), vec(sw_g), weight(w_out_b), vec(ln1_g), vec(ln1_b),
            weight(w_gate_up_b), weight(w_down_b), vec(ln2_g), vec(ln2_b),
        ],
        out_specs=pl.BlockSpec((TAIL_ROWS, d), row),
        out_shape=jax.ShapeDtypeStruct((n, d), F32),
        scratch_shapes=[
            pltpu.VMEM((SWA_HEADS, ATT_BLOCK, 2 * ATT_BLOCK), F32),
            pltpu.VMEM((2, TAIL_ROWS, wd), BF16),
        ],
        compiler_params=pltpu.CompilerParams(
            dimension_semantics=("arbitrary",), vmem_limit_bytes=VMEM_LIMIT),
        name="swa_attention_and_dense_tail",
    )(sinks, rel_bias, q_sw, kv_sw, kv_sw, ha, sb2, sb_g, sw_g, w_out_b, ln1_g, ln1_b,
      w_gate_up_b, w_down_b, ln2_g, ln2_b)


SB_GROUP = 1
SB_LIVE_DONE = 1e30


def _sb_fast_masks(mask_ref):
    tq, w = SB_Q_TILE, SB_WINDOW
    row = lax.broadcasted_iota(jnp.int32, (2 * tq, w), 0) & (tq - 1)
    col = lax.broadcasted_iota(jnp.int32, (2 * tq, w), 1)
    causal = col < row + (w - tq)
    mask_ref[0, 0], mask_ref[0, 1] = _sb_masks(causal)
    for h in range(mask_ref.shape[0] - 1):
        first_key = w - (h + 1) * tq
        mask_ref[1 + h, 0], mask_ref[1 + h, 1] = _sb_masks(
            jnp.logical_and(causal, col >= first_key))


def _sb_fast_schedule(q_ref, kc_ref, kp_ref, vc_ref, vp_ref, tri_ref, mask_ref, o_ref,
                      no_previous, lives):
    tq, w = SB_Q_TILE, SB_WINDOW
    subs = q_ref.shape[0] // tq
    pairs = q_ref.shape[-1] // LANES
    clamped = w // tq - 1
    assert mask_ref.shape[0] == clamped + 1 and subs % SB_GROUP == 0
    lane = lax.broadcasted_iota(jnp.int32, (tq, LANES), 1)
    lanes_of = lambda p: slice(p * LANES, (p + 1) * LANES)
    rows_of = lambda h: slice(h * tq, (h + 1) * tq)

    def window(h, cur_ref, prev_ref, p):
        start = (h + 1) * tq - w
        if start >= 0:
            return cur_ref[start:start + w, lanes_of(p)]
        return jnp.concatenate([prev_ref[w + start:w, lanes_of(p)],
                                cur_ref[0:w + start, lanes_of(p)]], axis=0)

    def mask(h, which):
        if h >= clamped:
            return mask_ref[0, which]
        return mask_ref[jnp.where(no_previous, 1 + h, 0), which]

    items = [(p, g) for p in range(pairs) for g in range(subs // SB_GROUP)]
    group = lambda g: range(g * SB_GROUP, (g + 1) * SB_GROUP)

    def scores(i):
        p, g = items[i]
        return [_sb_scores(_split_heads(q_ref[rows_of(h), lanes_of(p)], lane),
                           window(h, kc_ref, kp_ref, p), mask(h, 0)) for h in group(g)]

    def weights(i, sc):
        return _sb_weights(sc, tri_ref[...], [mask(h, 1) for h in group(items[i][1])],
                           [None] * SB_GROUP)

    def values(i, aws):
        p, g = items[i]
        for h, (a, total) in zip(group(g), aws):
            o2 = _dot(a, window(h, vc_ref, vp_ref, p))
            o_ref[rows_of(h), lanes_of(p)] = _merge_heads(o2, lane).astype(o_ref.dtype)
            m = jnp.min(total[:, 0:1])
            lives.append(m if h > clamped else jnp.where(no_previous, SB_LIVE_DONE, m))

    return _pipeline_schedule(len(items), [scores, weights, values])[0]


def _fast_kernel(tiles_per_seq, last_tile, sinks_ref, relb_ref, qsw_ref, kvc_ref, kvp_ref,
                 qsb_ref, kc_ref, kp_ref, vc_ref, vp_ref,
                 ha_ref, sbg_ref, swg_ref, wo_ref, l1g_ref, l1b_ref,
                 wgu_ref, wd_ref, l2g_ref, l2b_ref, o_ref, live_ref,
                 bias_ref, sw_ref, sb_ref, tri_ref, mask_ref):
    t = pl.program_id(0)

    @pl.when(t == 0)
    def _first_step():
        _swa_fill_bias(relb_ref, bias_ref)
        _sb_fast_masks(mask_ref)
        j = lax.broadcasted_iota(jnp.int32, tri_ref.shape, 0)
        s = lax.broadcasted_iota(jnp.int32, tri_ref.shape, 1)
        tri_ref[...] = jnp.where(j > s, 1.0, 0.0).astype(BF16)
        sw_ref[1] = jnp.zeros(sw_ref.shape[1:], sw_ref.dtype)
        sb_ref[1] = jnp.zeros(sb_ref.shape[1:], sb_ref.dtype)

    no_previous = (jnp.minimum(t, last_tile) % tiles_per_seq) == 0
    lives = []
    swa = _swa_schedule(sinks_ref, qsw_ref.at[0], kvc_ref.at[0], kvp_ref.at[0], bias_ref,
                        sw_ref.at[t % 2], no_previous)
    sb = _sb_fast_schedule(qsb_ref.at[0], kc_ref.at[0], kp_ref.at[0], vc_ref.at[0], vp_ref.at[0],
                           tri_ref, mask_ref, sb_ref.at[t % 2], no_previous, lives)
    merged, out_proj, first_norm, ffn, second_norm = _tail_schedule(
        ha_ref, sb_ref.at[(t + 1) % 2], sw_ref.at[(t + 1) % 2], sbg_ref, swg_ref, wo_ref,
        l1g_ref, l1b_ref, wgu_ref, wd_ref, l2g_ref, l2b_ref, o_ref)
    tail = [merged] + out_proj + [first_norm] + ffn + [second_norm]
    for thunk in _interleave(tail, _interleave(swa, sb)):
        thunk()
    live_ref[...] = jnp.full(live_ref.shape, functools.reduce(jnp.minimum, lives), F32)


def _fast_block(q_sw, kv_sw, q_sb, k_sb, v_sb, sinks, rel_bias, ha, sb_g, sw_g, w_out_b,
                ln1_g, ln1_b, w_gate_up_b, w_down_b, ln2_g, ln2_b):
    b, s, wd = q_sw.shape
    n, d = ha.shape
    tiles_per_seq = s // TAIL_ROWS
    last_tile = n // TAIL_ROWS - 1
    att = lambda t: jnp.minimum(t, last_tile)
    cur = lambda t: (att(t) // tiles_per_seq, att(t) % tiles_per_seq, 0)

    def prev(rows):
        per_tile = TAIL_ROWS // rows
        return lambda t: (att(t) // tiles_per_seq,
                          jnp.maximum(att(t) % tiles_per_seq * per_tile - 1, 0), 0)

    row = lambda t: (jnp.maximum(t - 1, 0), 0)
    fixed = lambda t: (0, 0)
    smem = pl.BlockSpec(memory_space=pltpu.SMEM)
    vec = lambda a: pl.BlockSpec((1, a.shape[-1]), fixed)
    weight = lambda a: pl.BlockSpec(a.shape, fixed, pipeline_mode=pl.Buffered(1))
    tile3 = lambda a: pl.BlockSpec((1, TAIL_ROWS, a.shape[-1]), cur)
    before = lambda a, rows: pl.BlockSpec((1, rows, a.shape[-1]), prev(rows))
    steps = last_tile + 2
    return pl.pallas_call(
        functools.partial(_fast_kernel, tiles_per_seq, last_tile),
        grid=(steps,),
        in_specs=[
            smem, smem,
            tile3(q_sw), tile3(kv_sw), before(kv_sw, ATT_BLOCK),
            tile3(q_sb), tile3(k_sb), before(k_sb, SB_WINDOW), tile3(v_sb), before(v_sb, SB_WINDOW),
            pl.BlockSpec((TAIL_ROWS, d), row),
            vec(sb_g), vec(sw_g), weight(w_out_b), vec(ln1_g), vec(ln1_b),
            weight(w_gate_up_b), weight(w_down_b), vec(ln2_g), vec(ln2_b),
        ],
        out_specs=[pl.BlockSpec((TAIL_ROWS, d), row),
                   pl.BlockSpec((1, 8, LANES), lambda t: (t, 0, 0))],
        out_shape=[jax.ShapeDtypeStruct((n, d), F32),
                   jax.ShapeDtypeStruct((steps, 8, LANES), F32)],
        scratch_shapes=[
            pltpu.VMEM((SWA_HEADS, ATT_BLOCK, 2 * ATT_BLOCK), F32),
            pltpu.VMEM((2, TAIL_ROWS, wd), BF16),
            pltpu.VMEM((2, TAIL_ROWS, q_sb.shape[-1]), BF16),
            pltpu.VMEM((SB_WINDOW, SB_WINDOW), BF16),
            pltpu.VMEM((SB_WINDOW // SB_Q_TILE, 2, 2 * SB_Q_TILE, SB_WINDOW), F32),
        ],
        compiler_params=pltpu.CompilerParams(
            dimension_semantics=("arbitrary",), vmem_limit_bytes=VMEM_LIMIT),
        name="attention_and_dense_tail",
    )(sinks, rel_bias, q_sw, kv_sw, kv_sw, q_sb, k_sb, k_sb, v_sb, v_sb, ha, sb_g, sw_g, w_out_b,
      ln1_g, ln1_b, w_gate_up_b, w_down_b, ln2_g, ln2_b)


def kernel(x, ln_in_g, ln_in_b, w_in, sb_norm_g, swa_norm_g, sinks, rel_bias, w_out,
           ln1_g, ln1_b, w_gate_up, w_down, ln2_g, ln2_b):
    b, s, d = x.shape
    assert w_in.shape[0] == DEPTH == 1
    sb_w = SB_HEADS * HEAD_DIM
    sw_w = SWA_HEADS * HEAD_DIM
    kv_w = SWA_KV_HEADS * HEAD_DIM
    assert w_in.shape[-1] == 3 * sb_w + sw_w + 2 * kv_w
    assert kv_w == LANES and SWA_HEADS // SWA_KV_HEADS == 4
    assert s % SB_WINDOW == 0 and s % SB_STEP_ROWS == 0 and s % TAIL_ROWS == 0
    assert (b * s) % ROW_TILE == 0 and WINDOW <= ATT_BLOCK and d % MXU_WIDTH == 0
    assert FFN_CHUNK % MXU_WIDTH == 0 and w_down.shape[1] % MXU_WIDTH == 0

    x2 = x.reshape(b * s, d)
    row_vec = lambda a: a.reshape(1, -1)

    names = ("q_sb", "k_sb", "v_sb", "q_sw", "k_sw", "v_sw")
    widths = (sb_w, sb_w, sb_w, sw_w, kv_w, kv_w)
    splits, lo = {}, 0
    for name, wd in zip(names, widths):
        splits[name] = (lo, lo + wd)
        lo += wd
    ha, q_sb, k_sb, v_sb, q_sw, kv_sw, w_out_b, w_gate_up_b, w_down_b = _in_projection(
        x2, row_vec(ln_in_g), row_vec(ln_in_b), w_in[0], splits,
        [w_out[0], w_gate_up[0], w_down[0]])

    to3 = lambda a: a.reshape(b, s, a.shape[-1])
    q_sb, k_sb, v_sb, q_sw, kv_sw = map(to3, (q_sb, k_sb, v_sb, q_sw, kv_sw))
    tail_args = (row_vec(sb_norm_g[0]), row_vec(swa_norm_g[0]), w_out_b,
                 row_vec(ln1_g[0]), row_vec(ln1_b[0]), w_gate_up_b, w_down_b,
                 row_vec(ln2_g[0]), row_vec(ln2_b[0]))

    fast, live = _fast_block(q_sw, kv_sw, q_sb, k_sb, v_sb, sinks[0], rel_bias, ha, *tail_args)
    n_tiles = (b * s) // TAIL_ROWS

    def general(_):
        (sb_out,) = _sb_attention(q_sb, k_sb, v_sb, [])
        return _attn_tail_block(q_sw, kv_sw, sinks[0], rel_bias, ha,
                                sb_out.reshape(b * s, sb_w), *tail_args)

    out = lax.cond(jnp.any(live[:n_tiles, 0, 0] < SB_DEAD_LOG2), general, lambda _: fast, None)
    return out.reshape(b, s, d)
```

```python
import functools
import math

import jax
import jax.numpy as jnp
import numpy as np
from jax import lax
from jax.experimental import pallas as pl
from jax.experimental.pallas import tpu as pltpu

F32 = jnp.float32
BF16 = jnp.bfloat16

HEAD_DIM = 64
SB_HEADS = 8
SWA_HEADS = 8
SWA_KV_HEADS = 2
WINDOW = 128
REL_BUCKETS = 32
REL_MAX_DIST = 128
LN_EPS = 1e-5
RMS_EPS = 1e-6
DEPTH = 1
ALPHA = (2 * DEPTH) ** 0.25
QK_SCALE = HEAD_DIM ** -0.5
LOG2E = math.log2(math.e)

LANES = 128
BF16_ROWS = 16
MXU_WIDTH = 256
ATT_BLOCK = 128
SB_WINDOW = 256
SB_Q_TILE = 64
SB_STEP_ROWS = 512
ROW_TILE = 1024
PROJ_PARTS = 2
TAIL_ROWS = 512
FFN_CHUNK = 4 * MXU_WIDTH
VMEM_LIMIT = 56 * 1024 * 1024

SB_DEAD_LOG2 = 127.0
MASK_NEG = -1e30


def _layer_norm(x, g, b):
    mu = jnp.mean(x, axis=-1, keepdims=True)
    xc = x - mu
    var = jnp.mean(xc * xc, axis=-1, keepdims=True)
    return xc * lax.rsqrt(var + LN_EPS) * g + b


def _rms_norm(x, g):
    ms = jnp.mean(x * x, axis=-1, keepdims=True)
    return x * lax.rsqrt(ms + RMS_EPS) * g


def _dot(a, b):
    return jnp.dot(a, b, preferred_element_type=F32)


def _dot_nt(a, b):
    return lax.dot_general(a, b, (((1,), (1,)), ((), ())), preferred_element_type=F32)


def _pipeline_schedule(n, stages):
    vals = [dict() for _ in stages]
    thunks = []
    for step in range(n + len(stages) - 1):
        for s, stage in enumerate(stages):
            i = step - s
            if 0 <= i < n:
                def thunk(s=s, i=i, stage=stage):
                    vals[s][i] = stage(i, vals[s - 1].pop(i)) if s else stage(i)
                thunks.append(thunk)
    return thunks, vals[-1]


def _software_pipeline(n, stages):
    thunks, results = _pipeline_schedule(n, stages)
    for thunk in thunks:
        thunk()
    return results


def _interleave(xs, ys):
    merged, taken = [], 0
    for i, x in enumerate(xs):
        merged.append(x)
        upto = (i + 1) * len(ys) // len(xs)
        merged.extend(ys[taken:upto])
        taken = upto
    return merged


def _inproj_kernel(splits, n_later, x_ref, g_ref, b_ref, w32_ref, *refs):
    later_in, refs = refs[:n_later], refs[n_later:]
    ha_ref, qsb_ref, ksb_ref, vsb_ref, qsw_ref, kv_ref = refs[:6]
    later_out, w_ref = refs[6:6 + n_later], refs[6 + n_later]
    rows = x_ref.shape[0] // PROJ_PARTS
    part = lambda i: slice(i * rows, (i + 1) * rows)
    assert splits["k_sw"][1] == splits["v_sw"][0]

    @pl.when(pl.program_id(0) == 0)
    def _convert_weight():
        w_ref[...] = w32_ref[...].astype(BF16)

    for src, dst in zip(later_in, later_out):
        dst[...] = src[...].astype(BF16)

    def normed(i):
        h32 = _layer_norm(x_ref[part(i), :], g_ref[...], b_ref[...])
        ha_ref[part(i), :] = ALPHA * h32
        return h32.astype(BF16)

    def projected(i, h):
        r = part(i)
        proj = lambda lo, hi: _dot(h, w_ref[:, lo:hi])
        qsb_ref[r, :] = (proj(*splits["q_sb"]) * QK_SCALE).astype(BF16)
        ksb_ref[r, :] = proj(*splits["k_sb"]).astype(BF16)
        vsb_ref[r, :] = proj(*splits["v_sb"]).astype(BF16)
        qsw_ref[r, :] = (proj(*splits["q_sw"]) * QK_SCALE).astype(BF16)
        kv = proj(splits["k_sw"][0], splits["v_sw"][1])
        kvw = kv.shape[-1] // 2
        k, v = kv[:, :kvw], kv[:, kvw:]
        kv_ref[r, 0 * kvw:1 * kvw] = k.astype(BF16)
        kv_ref[r, 1 * kvw:2 * kvw] = pltpu.roll(k, HEAD_DIM, axis=1).astype(BF16)
        kv_ref[r, 2 * kvw:3 * kvw] = v.astype(BF16)
        kv_ref[r, 3 * kvw:4 * kvw] = pltpu.roll(v, HEAD_DIM, axis=1).astype(BF16)

    _software_pipeline(PROJ_PARTS, [normed, projected])


def _in_projection(x2, g, b, w_in, splits, later_weights):
    n, d = x2.shape
    steps = n // ROW_TILE
    width = lambda name: splits[name][1] - splits[name][0]
    out_w = [width("q_sb"), width("k_sb"), width("v_sb"), width("q_sw"), 4 * width("k_sw")]
    row = lambda i: (i, 0)
    fixed = lambda i: (0, 0)
    later_specs = [
        pl.BlockSpec((size, w.shape[1]),
                     lambda i, last=w.shape[0] // size - 1: (jnp.minimum(i, last), 0))
        for w, size in zip(later_weights, _conversion_blocks(later_weights, steps))]
    return pl.pallas_call(
        functools.partial(_inproj_kernel, splits, len(later_weights)),
        grid=(steps,),
        in_specs=[
            pl.BlockSpec((ROW_TILE, d), row),
            pl.BlockSpec((1, d), fixed),
            pl.BlockSpec((1, d), fixed),
            pl.BlockSpec(w_in.shape, fixed, pipeline_mode=pl.Buffered(1)),
        ] + later_specs,
        out_specs=[pl.BlockSpec((ROW_TILE, d), row)]
        + [pl.BlockSpec((ROW_TILE, w), row) for w in out_w] + later_specs,
        out_shape=[jax.ShapeDtypeStruct((n, d), F32)]
        + [jax.ShapeDtypeStruct((n, w), BF16) for w in out_w]
        + [jax.ShapeDtypeStruct(w.shape, BF16) for w in later_weights],
        scratch_shapes=[pltpu.VMEM(w_in.shape, BF16)],
        compiler_params=pltpu.CompilerParams(
            dimension_semantics=("arbitrary",), vmem_limit_bytes=VMEM_LIMIT),
        name="ln_in_proj",
    )(x2, g, b, w_in, *later_weights)


def _conversion_blocks(weights, steps):
    sizes = []
    for w in weights:
        rows = w.shape[0]
        sizes.append(next(r for r in range(BF16_ROWS, rows + 1, BF16_ROWS)
                          if rows % r == 0 and rows // r <= steps))
    return sizes


def _split_heads(x, lane):
    zero = jnp.zeros_like(x)
    return jnp.concatenate([jnp.where(lane < HEAD_DIM, x, zero),
                            jnp.where(lane >= HEAD_DIM, x, zero)], axis=0)


def _merge_heads(o2, lane):
    t = o2.shape[0] // 2
    return jnp.where(lane < HEAD_DIM, o2[:t], o2[t:])


def _sb_masks(valid):
    return jnp.where(valid, LOG2E, 0.0), jnp.where(valid, 0.0, MASK_NEG)


def _sb_scores(qq, k, keep):
    z = _dot_nt(qq, k)
    zl = z * LOG2E
    c_nat = jnp.maximum(z, 0.0) + jnp.log(1.0 + jnp.exp2(-jnp.abs(zl)))
    cm = c_nat * keep
    return zl - cm, cm.astype(BF16), cm[:, :LANES]


def _sb_weights(scores, tri, kills, carrieds):
    sums = _dot(jnp.concatenate([sc[1] for sc in scores], axis=0), tri)
    rows = scores[0][0].shape[0]
    out = []
    for n, ((log_beta, _, cm0), kill, carried) in enumerate(zip(scores, kills, carrieds)):
        later = sums[n * rows:(n + 1) * rows]
        log_a = (log_beta - later) + kill
        if carried is not None:
            log_a = log_a - jnp.concatenate([carried] * (log_a.shape[1] // LANES), axis=1)
        out.append((jnp.exp2(log_a).astype(BF16), later[:, :LANES] + cm0))
    return out


def _sb_kernel(n_conv, q_ref, k_ref, v_ref, *refs):
    conv_in, o_ref, conv_out = refs[:n_conv], refs[n_conv], refs[n_conv + 1:2 * n_conv + 1]
    tri_ref, mask_ref, acc_ref, car_ref = refs[2 * n_conv + 1:]
    for src, dst in zip(conv_in, conv_out):
        dst[...] = src[...].astype(BF16)

    bi = pl.program_id(0)
    qi = pl.program_id(1)
    tq, w = SB_Q_TILE, SB_WINDOW
    subs = q_ref.shape[1] // tq
    pairs = q_ref.shape[-1] // LANES
    row0 = qi * (subs * tq)

    @pl.when(jnp.logical_and(bi == 0, qi == 0))
    def _fill_tri():
        j = lax.broadcasted_iota(jnp.int32, (w, w), 0)
        s = lax.broadcasted_iota(jnp.int32, (w, w), 1)
        tri_ref[...] = jnp.where(j > s, 1.0, 0.0).astype(BF16)

    lane = lax.broadcasted_iota(jnp.int32, (tq, LANES), 1)
    lanes_of = lambda p: slice(p * LANES, (p + 1) * LANES)
    rows_of = lambda h: slice(h * tq, (h + 1) * tq)
    first_start = lambda h: pl.multiple_of(jnp.maximum(row0 + (h + 1) * tq - w, 0), tq)

    def key_positions(start):
        return start + lax.broadcasted_iota(jnp.int32, (2 * tq, w), 1)

    assert subs * tq >= w - tq

    @pl.when(qi <= 1)
    def _fill_masks():
        row = lax.broadcasted_iota(jnp.int32, (2 * tq, w), 0) & (tq - 1)
        for h in range(subs):
            mask_ref[h, 0], mask_ref[h, 1] = _sb_masks(
                key_positions(first_start(h)) < row0 + h * tq + row)

    def scores(p, starts, keeps):
        return [_sb_scores(_split_heads(q_ref[0, rows_of(h), lanes_of(p)], lane),
                           k_ref[0, pl.ds(starts[h], w), lanes_of(p)], keeps[h])
                for h in range(subs)]

    first_starts = [first_start(h) for h in range(subs)]

    def first_scores(p):
        return scores(p, first_starts, [mask_ref[h, 0] for h in range(subs)])

    def first_weights(p, sc):
        return _sb_weights(sc, tri_ref[...], [mask_ref[h, 1] for h in range(subs)], [None] * subs)

    def first_values(p, aws):
        least = None
        for h, (a, total) in enumerate(aws):
            o2 = _dot(a, v_ref[0, pl.ds(first_starts[h], w), lanes_of(p)])
            o_ref[0, rows_of(h), lanes_of(p)] = _merge_heads(o2, lane).astype(o_ref.dtype)
            acc_ref[p, h] = o2
            car_ref[p, h] = total
            m = jnp.min(total[:, 0:1])
            least = m if least is None else jnp.minimum(least, m)
        return least

    first = _software_pipeline(pairs, [first_scores, first_weights, first_values])
    live = functools.reduce(jnp.minimum, [first[p] for p in range(pairs)])
    last_start = first_starts[-1]

    @pl.when(jnp.logical_and(last_start > 0, live < SB_DEAD_LOG2))
    def _walk_back():
        for p in range(pairs):
            for h in range(subs):
                car_ref[p, h] = jnp.broadcast_to(car_ref[p, h][:, 0:1], (2 * tq, LANES))

        def body(state):
            step, _ = state
            done = [jnp.maximum(first_start(h) - step * w, 0) for h in range(subs)]
            starts = [pl.multiple_of(jnp.maximum(d - w, 0), tq) for d in done]
            masks = [_sb_masks(key_positions(s) < d) for s, d in zip(starts, done)]

            def more_scores(p):
                return scores(p, starts, [m[0] for m in masks])

            def weights(p, sc):
                return _sb_weights(sc, tri_ref[...], [m[1] for m in masks],
                                   [car_ref[p, h] for h in range(subs)])

            def values(p, aws):
                least = None
                for h, (a, total) in enumerate(aws):
                    acc_ref[p, h] += _dot(a, v_ref[0, pl.ds(starts[h], w), lanes_of(p)])
                    carried = car_ref[p, h] + total[:, 0:1]
                    car_ref[p, h] = carried
                    m = jnp.min(carried)
                    least = m if least is None else jnp.minimum(least, m)
                return least

            mins = _software_pipeline(pairs, [more_scores, weights, values])
            return step + 1, functools.reduce(jnp.minimum, [mins[p] for p in range(pairs)])

        def cond(state):
            step, nxt = state
            return jnp.logical_and(last_start - step * w > 0, nxt < SB_DEAD_LOG2)

        lax.while_loop(cond, body, (jnp.int32(0), live))
        for p in range(pairs):
            for h in range(subs):
                o_ref[0, rows_of(h), lanes_of(p)] = _merge_heads(
                    acc_ref[p, h], lane).astype(o_ref.dtype)


def _sb_attention(q, k, v, weights):
    b, s, wd = q.shape
    pairs = wd // LANES
    subs = SB_STEP_ROWS // SB_Q_TILE
    nq = s // SB_STEP_ROWS
    tile = lambda bi, i: (bi, i, 0)
    whole = lambda bi, i: (bi, 0, 0)
    once = pl.Buffered(1)
    conv_specs = [
        pl.BlockSpec((size, w.shape[1]),
                     lambda bi, i, last=w.shape[0] // size - 1: (jnp.minimum(bi * nq + i, last), 0))
        for w, size in zip(weights, _conversion_blocks(weights, b * nq))]
    return pl.pallas_call(
        functools.partial(_sb_kernel, len(weights)),
        grid=(b, nq),
        in_specs=[
            pl.BlockSpec((1, SB_STEP_ROWS, wd), tile),
            pl.BlockSpec((1, s, wd), whole, pipeline_mode=once),
            pl.BlockSpec((1, s, wd), whole, pipeline_mode=once),
        ] + conv_specs,
        out_specs=[pl.BlockSpec((1, SB_STEP_ROWS, wd), tile)] + conv_specs,
        out_shape=[jax.ShapeDtypeStruct((b, s, wd), BF16)]
        + [jax.ShapeDtypeStruct(w.shape, BF16) for w in weights],
        scratch_shapes=[
            pltpu.VMEM((SB_WINDOW, SB_WINDOW), BF16),
            pltpu.VMEM((subs, 2, 2 * SB_Q_TILE, SB_WINDOW), F32),
            pltpu.VMEM((pairs, subs, 2 * SB_Q_TILE, LANES), F32),
            pltpu.VMEM((pairs, subs, 2 * SB_Q_TILE, LANES), F32),
        ],
        compiler_params=pltpu.CompilerParams(
            dimension_semantics=("arbitrary", "arbitrary"), vmem_limit_bytes=VMEM_LIMIT),
        name="stick_breaking_attention",
    )(q, k, v, *weights)


def _t5_log_bucket_starts():
    exact = REL_BUCKETS // 2
    d = np.arange(exact, 2 * REL_MAX_DIST, dtype=np.float32)
    large = exact + (np.log(d / np.float32(exact)) / np.float32(math.log(REL_MAX_DIST / exact))
                     * np.float32(REL_BUCKETS - exact)).astype(np.int32)
    large = np.minimum(large, REL_BUCKETS - 1)
    assert large[0] == exact and np.all(np.diff(large) >= 0)
    return [int(d[np.argmax(large >= bk)]) for bk in range(exact + 1, REL_BUCKETS)]


def _t5_causal_bucket(distance):
    d = jnp.maximum(distance, 0)
    bucket = jnp.minimum(d, REL_BUCKETS // 2)
    for start in _t5_log_bucket_starts():
        bucket = bucket + jnp.where(d >= start, 1, 0)
    return bucket


def _swa_fill_bias(relb_ref, bias_ref):
    t = ATT_BLOCK
    r = lax.broadcasted_iota(jnp.int32, (t, 2 * t), 0)
    c = lax.broadcasted_iota(jnp.int32, (t, 2 * t), 1)
    dist = r + t - c
    bucket = _t5_causal_bucket(dist)
    valid = jnp.logical_and(dist >= 0, dist < WINDOW)
    for h in range(SWA_HEADS):
        bias = jnp.zeros((t, 2 * t), F32)
        for bk in range(REL_BUCKETS):
            bias = jnp.where(bucket == bk, relb_ref[bk, h], bias)
        bias_ref[h] = jnp.where(valid, bias, MASK_NEG)


def _swa_schedule(sinks_ref, q_ref, kvc_ref, kvp_ref, bias_ref, o_ref, no_previous):
    t = ATT_BLOCK
    kvw = kvc_ref.shape[-1] // 4
    group = SWA_HEADS // SWA_KV_HEADS
    tiles = q_ref.shape[0] // t
    rows_of = lambda n: slice(n * t, (n + 1) * t)
    lane = lax.broadcasted_iota(jnp.int32, (t, LANES), 1)
    row2 = lax.broadcasted_iota(jnp.int32, (2 * t, 1), 0)
    col2 = lax.broadcasted_iota(jnp.int32, (2 * t, 2 * t), 1)
    prev_pen = jnp.where(jnp.logical_and(col2 < t, no_previous), MASK_NEG, 0.0)

    def kv_window(n, variant):
        cols = slice(variant * kvw, (variant + 1) * kvw)
        before = kvc_ref[rows_of(n - 1), cols] if n else kvp_ref[:, cols]
        return jnp.concatenate([before, kvc_ref[rows_of(n), cols]], axis=0)

    items = [(n, half, j) for n in range(tiles) for half in range(2) for j in range(SWA_KV_HEADS)]
    heads_of = lambda half, j: [h for h in range(j * group, (j + 1) * group) if h % 2 == half]

    def probabilities(i):
        n, half, j = items[i]
        heads = heads_of(half, j)
        in_half = (lane >= HEAD_DIM) if half else (lane < HEAD_DIM)
        qq = jnp.concatenate(
            [jnp.where(in_half, q_ref[rows_of(n), (h // 2) * LANES:(h // 2 + 1) * LANES], 0)
             for h in heads], axis=0)
        s = _dot_nt(qq, kv_window(n, 0 + int(j != half)))
        s = s + jnp.concatenate([bias_ref[h] for h in heads], axis=0)
        if n == 0:
            s = s + prev_pen
        sink = jnp.where(row2 < t, sinks_ref[heads[0]], sinks_ref[heads[1]])
        m = jnp.maximum(jnp.max(s, axis=-1, keepdims=True), sink)
        p = jnp.exp(s - m)
        denom = jnp.sum(p, axis=-1, keepdims=True) + jnp.exp(sink - m)
        return p.astype(BF16), 1.0 / denom

    def values(i, pd):
        n, half, j = items[i]
        p, inv_denom = pd
        return _dot(p, kv_window(n, 2 + int(j != half))) * inv_denom

    thunks, outs = _pipeline_schedule(len(items), [probabilities, values])

    def store():
        for i, (n, half, j) in enumerate(items):
            if half == 0:
                even, odd = outs[i], outs[items.index((n, 1, j))]
                for e, h in enumerate(heads_of(0, j)):
                    o_ref[rows_of(n), (h // 2) * LANES:(h // 2 + 1) * LANES] = jnp.where(
                        lane < HEAD_DIM, even[e * t:(e + 1) * t], odd[e * t:(e + 1) * t]
                    ).astype(o_ref.dtype)

    return thunks + [store]


def _tail_schedule(ha_ref, sb_ref, sw_ref, sbg_ref, swg_ref, wo_ref, l1g_ref, l1b_ref,
                   wgu_ref, wd_ref, l2g_ref, l2b_ref, o_ref):
    d = ha_ref.shape[1]
    d_ff = wd_ref.shape[0]
    mw = MXU_WIDTH
    bounds = list(range(0, d_ff, FFN_CHUNK)) + [d_ff]
    n_chunks = len(bounds) - 1
    out_tiles = range(d // mw)
    st = {}

    def merged():
        sb = _rms_norm(sb_ref[...].astype(F32), sbg_ref[...]).astype(BF16)
        sw = _rms_norm(sw_ref[...].astype(F32), swg_ref[...]).astype(BF16)
        st["merged"] = jnp.concatenate([sb, sw], axis=1)

    def out_projection(n):
        st["mix", n] = _dot(st["merged"], wo_ref[:, n * mw:(n + 1) * mw])

    def first_norm():
        mix = jnp.concatenate([st.pop(("mix", n)) for n in out_tiles], axis=1)
        st.pop("merged")
        h1 = _layer_norm(ha_ref[...] + mix, l1g_ref[...], l1b_ref[...])
        st["h1b"] = h1.astype(BF16)
        for n in out_tiles:
            st["acc", n] = ALPHA * h1[:, n * mw:(n + 1) * mw]

    def activation(c, n):
        lo = bounds[c] + n * mw
        gate = _dot(st["h1b"], wgu_ref[:, lo:lo + mw])
        up = _dot(st["h1b"], wgu_ref[:, d_ff + lo:d_ff + lo + mw])
        st["act", c, n] = (gate / (1.0 + jnp.exp(-gate)) * up).astype(BF16)

    def down(c, n):
        tiles = range((bounds[c + 1] - bounds[c]) // mw)
        if n == 0:
            st["act", c] = jnp.concatenate([st.pop(("act", c, k)) for k in tiles], axis=1)
        st["acc", n] = st["acc", n] + _dot(st["act", c], wd_ref[bounds[c]:bounds[c + 1],
                                                               n * mw:(n + 1) * mw])

    def second_norm():
        y = jnp.concatenate([st.pop(("acc", n)) for n in out_tiles], axis=1)
        o_ref[...] = _layer_norm(y, l2g_ref[...], l2b_ref[...])

    act_thunks = [[functools.partial(activation, c, n)
                   for n in range((bounds[c + 1] - bounds[c]) // mw)] for c in range(n_chunks)]
    down_thunks = [[functools.partial(down, c, n) for n in out_tiles] for c in range(n_chunks)]
    ffn = list(act_thunks[0])
    for c in range(1, n_chunks):
        ffn += _interleave(act_thunks[c], down_thunks[c - 1])
    ffn += down_thunks[-1]
    return (merged, [functools.partial(out_projection, n) for n in out_tiles], first_norm,
            ffn, second_norm)


def _attn_tail_kernel(tiles_per_seq, last_tile, sinks_ref, relb_ref, q_ref, kvc_ref, kvp_ref,
                      ha_ref, sb_ref, sbg_ref, swg_ref, wo_ref, l1g_ref, l1b_ref,
                      wgu_ref, wd_ref, l2g_ref, l2b_ref, o_ref, bias_ref, sw_ref):
    t = pl.program_id(0)

    @pl.when(t == 0)
    def _first_step():
        _swa_fill_bias(relb_ref, bias_ref)
        sw_ref[1] = jnp.zeros(sw_ref.shape[1:], sw_ref.dtype)

    no_previous = (jnp.minimum(t, last_tile) % tiles_per_seq) == 0
    attention = _swa_schedule(sinks_ref, q_ref.at[0], kvc_ref.at[0], kvp_ref.at[0], bias_ref,
                              sw_ref.at[t % 2], no_previous)
    merged, out_proj, first_norm, ffn, second_norm = _tail_schedule(
        ha_ref, sb_ref, sw_ref.at[(t + 1) % 2], sbg_ref, swg_ref, wo_ref,
        l1g_ref, l1b_ref, wgu_ref, wd_ref, l2g_ref, l2b_ref, o_ref)
    tail = [merged] + out_proj + [first_norm] + ffn + [second_norm]
    for thunk in _interleave(tail, attention):
        thunk()


def _attn_tail_block(q_sw, kv_sw, sinks, rel_bias, ha, sb2, sb_g, sw_g, w_out_b, ln1_g, ln1_b,
                     w_gate_up_b, w_down_b, ln2_g, ln2_b):
    b, s, wd = q_sw.shape
    n, d = ha.shape
    tiles_per_seq = s // TAIL_ROWS
    last_tile = n // TAIL_ROWS - 1
    key_tiles = TAIL_ROWS // ATT_BLOCK
    att = lambda t: jnp.minimum(t, last_tile)
    cur = lambda t: (att(t) // tiles_per_seq, att(t) % tiles_per_seq, 0)
    prev = lambda t: (att(t) // tiles_per_seq,
                      jnp.maximum(att(t) % tiles_per_seq * key_tiles - 1, 0), 0)
    row = lambda t: (jnp.maximum(t - 1, 0), 0)
    fixed = lambda t: (0, 0)
    smem = pl.BlockSpec(memory_space=pltpu.SMEM)
    vec = lambda a: pl.BlockSpec((1, a.shape[-1]), fixed)
    weight = lambda a: pl.BlockSpec(a.shape, fixed, pipeline_mode=pl.Buffered(1))
    return pl.pallas_call(
        functools.partial(_attn_tail_kernel, tiles_per_seq, last_tile),
        grid=(last_tile + 2,),
        in_specs=[
            smem, smem,
            pl.BlockSpec((1, TAIL_ROWS, wd), cur),
            pl.BlockSpec((1, TAIL_ROWS, kv_sw.shape[-1]), cur),
            pl.BlockSpec((1, ATT_BLOCK, kv_sw.shape[-1]), prev),
            pl.BlockSpec((TAIL_ROWS, d), row),
            pl.BlockSpec((TAIL_ROWS, sb2.shape[-1]), row),
            vec(sb_g), vec(sw_g), weight(w_out_b), vec(ln1_g), vec(ln1_b),
            weight(w_gate_up_b), weight(w_down_b), vec(ln2_g), vec(ln2_b),
        ],
        out_specs=pl.BlockSpec((TAIL_ROWS, d), row),
        out_shape=jax.ShapeDtypeStruct((n, d), F32),
        scratch_shapes=[
            pltpu.VMEM((SWA_HEADS, ATT_BLOCK, 2 * ATT_BLOCK), F32),
            pltpu.VMEM((2, TAIL_ROWS, wd), BF16),
        ],
        compiler_params=pltpu.CompilerParams(
            dimension_semantics=("arbitrary",), vmem_limit_bytes=VMEM_LIMIT),
        name="swa_attention_and_dense_tail",
    )(sinks, rel_bias, q_sw, kv_sw, kv_sw, ha, sb2, sb_g, sw_g, w_out_b, ln1_g, ln1_b,
      w_gate_up_b, w_down_b, ln2_g, ln2_b)


SB_GROUP = 2
SB_LIVE_DONE = 1e30


def _sb_fast_masks(mask_ref):
    tq, w = SB_Q_TILE, SB_WINDOW
    row = lax.broadcasted_iota(jnp.int32, (2 * tq, w), 0) & (tq - 1)
    col = lax.broadcasted_iota(jnp.int32, (2 * tq, w), 1)
    causal = col < row + (w - tq)
    mask_ref[0, 0], mask_ref[0, 1] = _sb_masks(causal)
    for h in range(mask_ref.shape[0] - 1):
        first_key = w - (h + 1) * tq
        mask_ref[1 + h, 0], mask_ref[1 + h, 1] = _sb_masks(
            jnp.logical_and(causal, col >= first_key))


def _sb_fast_schedule(q_ref, kc_ref, kp_ref, vc_ref, vp_ref, tri_ref, mask_ref, o_ref,
                      no_previous, lives):
    tq, w = SB_Q_TILE, SB_WINDOW
    subs = q_ref.shape[0] // tq
    pairs = q_ref.shape[-1] // LANES
    clamped = w // tq - 1
    assert mask_ref.shape[0] == clamped + 1 and subs % SB_GROUP == 0
    lane = lax.broadcasted_iota(jnp.int32, (tq, LANES), 1)
    lanes_of = lambda p: slice(p * LANES, (p + 1) * LANES)
    rows_of = lambda h: slice(h * tq, (h + 1) * tq)

    def window(h, cur_ref, prev_ref, p):
        start = (h + 1) * tq - w
        if start >= 0:
            return cur_ref[start:start + w, lanes_of(p)]
        return jnp.concatenate([prev_ref[w + start:w, lanes_of(p)],
                                cur_ref[0:w + start, lanes_of(p)]], axis=0)

    def mask(h, which):
        if h >= clamped:
            return mask_ref[0, which]
        return mask_ref[jnp.where(no_previous, 1 + h, 0), which]

    items = [(p, g) for p in range(pairs) for g in range(subs // SB_GROUP)]
    group = lambda g: range(g * SB_GROUP, (g + 1) * SB_GROUP)

    def scores(i):
        p, g = items[i]
        return [_sb_scores(_split_heads(q_ref[rows_of(h), lanes_of(p)], lane),
                           window(h, kc_ref, kp_ref, p), mask(h, 0)) for h in group(g)]

    def weights(i, sc):
        return _sb_weights(sc, tri_ref[...], [mask(h, 1) for h in group(items[i][1])],
                           [None] * SB_GROUP)

    def values(i, aws):
        p, g = items[i]
        for h, (a, total) in zip(group(g), aws):
            o2 = _dot(a, window(h, vc_ref, vp_ref, p))
            o_ref[rows_of(h), lanes_of(p)] = _merge_heads(o2, lane).astype(o_ref.dtype)
            m = jnp.min(total[:, 0:1])
            lives.append(m if h > clamped else jnp.where(no_previous, SB_LIVE_DONE, m))

    return _pipeline_schedule(len(items), [scores, weights, values])[0]


def _fast_kernel(tiles_per_seq, last_tile, sinks_ref, relb_ref, qsw_ref, kvc_ref, kvp_ref,
                 qsb_ref, kc_ref, kp_ref, vc_ref, vp_ref,
                 ha_ref, sbg_ref, swg_ref, wo_ref, l1g_ref, l1b_ref,
                 wgu_ref, wd_ref, l2g_ref, l2b_ref, o_ref, live_ref,
                 bias_ref, sw_ref, sb_ref, tri_ref, mask_ref):
    t = pl.program_id(0)

    @pl.when(t == 0)
    def _first_step():
        _swa_fill_bias(relb_ref, bias_ref)
        _sb_fast_masks(mask_ref)
        j = lax.broadcasted_iota(jnp.int32, tri_ref.shape, 0)
        s = lax.broadcasted_iota(jnp.int32, tri_ref.shape, 1)
        tri_ref[...] = jnp.where(j > s, 1.0, 0.0).astype(BF16)
        sw_ref[1] = jnp.zeros(sw_ref.shape[1:], sw_ref.dtype)
        sb_ref[1] = jnp.zeros(sb_ref.shape[1:], sb_ref.dtype)

    no_previous = (jnp.minimum(t, last_tile) % tiles_per_seq) == 0
    lives = []
    swa = _swa_schedule(sinks_ref, qsw_ref.at[0], kvc_ref.at[0], kvp_ref.at[0], bias_ref,
                        sw_ref.at[t % 2], no_previous)
    sb = _sb_fast_schedule(qsb_ref.at[0], kc_ref.at[0], kp_ref.at[0], vc_ref.at[0], vp_ref.at[0],
                           tri_ref, mask_ref, sb_ref.at[t % 2], no_previous, lives)
    merged, out_proj, first_norm, ffn, second_norm = _tail_schedule(
        ha_ref, sb_ref.at[(t + 1) % 2], sw_ref.at[(t + 1) % 2], sbg_ref, swg_ref, wo_ref,
        l1g_ref, l1b_ref, wgu_ref, wd_ref, l2g_ref, l2b_ref, o_ref)
    tail = [merged] + out_proj + [first_norm] + ffn + [second_norm]
    for thunk in _interleave(tail, _interleave(swa, sb)):
        thunk()
    live_ref[...] = jnp.full(live_ref.shape, functools.reduce(jnp.minimum, lives), F32)


def _fast_block(q_sw, kv_sw, q_sb, k_sb, v_sb, sinks, rel_bias, ha, sb_g, sw_g, w_out_b,
                ln1_g, ln1_b, w_gate_up_b, w_down_b, ln2_g, ln2_b):
    b, s, wd = q_sw.shape
    n, d = ha.shape
    tiles_per_seq = s // TAIL_ROWS
    last_tile = n // TAIL_ROWS - 1
    att = lambda t: jnp.minimum(t, last_tile)
    cur = lambda t: (att(t) // tiles_per_seq, att(t) % tiles_per_seq, 0)

    def prev(rows):
        per_tile = TAIL_ROWS // rows
        return lambda t: (att(t) // tiles_per_seq,
                          jnp.maximum(att(t) % tiles_per_seq * per_tile - 1, 0), 0)

    row = lambda t: (jnp.maximum(t - 1, 0), 0)
    fixed = lambda t: (0, 0)
    smem = pl.BlockSpec(memory_space=pltpu.SMEM)
    vec = lambda a: pl.BlockSpec((1, a.shape[-1]), fixed)
    weight = lambda a: pl.BlockSpec(a.shape, fixed, pipeline_mode=pl.Buffered(1))
    tile3 = lambda a: pl.BlockSpec((1, TAIL_ROWS, a.shape[-1]), cur)
    before = lambda a, rows: pl.BlockSpec((1, rows, a.shape[-1]), prev(rows))
    steps = last_tile + 2
    return pl.pallas_call(
        functools.partial(_fast_kernel, tiles_per_seq, last_tile),
        grid=(steps,),
        in_specs=[
            smem, smem,
            tile3(q_sw), tile3(kv_sw), before(kv_sw, ATT_BLOCK),
            tile3(q_sb), tile3(k_sb), before(k_sb, SB_WINDOW), tile3(v_sb), before(v_sb, SB_WINDOW),
            pl.BlockSpec((TAIL_ROWS, d), row),
            vec(sb_g), vec(sw_g), weight(w_out_b), vec(ln1_g), vec(ln1_b),
            weight(w_gate_up_b), weight(w_down_b), vec(ln2_g), vec(ln2_b),
        ],
        out_specs=[pl.BlockSpec((TAIL_ROWS, d), row),
                   pl.BlockSpec((1, 8, LANES), lambda t: (t, 0, 0))],
        out_shape=[jax.ShapeDtypeStruct((n, d), F32),
                   jax.ShapeDtypeStruct((steps, 8, LANES), F32)],
        scratch_shapes=[
            pltpu.VMEM((SWA_HEADS, ATT_BLOCK, 2 * ATT_BLOCK), F32),
            pltpu.VMEM((2, TAIL_ROWS, wd), BF16),
            pltpu.VMEM((2, TAIL_ROWS, q_sb.shape[-1]), BF16),
            pltpu.VMEM((SB_WINDOW, SB_WINDOW), BF16),
            pltpu.VMEM((SB_WINDOW // SB_Q_TILE, 2, 2 * SB_Q_TILE, SB_WINDOW), F32),
        ],
        compiler_params=pltpu.CompilerParams(
            dimension_semantics=("arbitrary",), vmem_limit_bytes=VMEM_LIMIT),
        name="attention_and_dense_tail",
    )(sinks, rel_bias, q_sw, kv_sw, kv_sw, q_sb, k_sb, k_sb, v_sb, v_sb, ha, sb_g, sw_g, w_out_b,
      ln1_g, ln1_b, w_gate_up_b, w_down_b, ln2_g, ln2_b)


def kernel(x, ln_in_g, ln_in_b, w_in, sb_norm_g, swa_norm_g, sinks, rel_bias, w_out,
           ln1_g, ln1_b, w_gate_up, w_down, ln2_g, ln2_b):
    b, s, d = x.shape
    assert w_in.shape[0] == DEPTH == 1
    sb_w = SB_HEADS * HEAD_DIM
    sw_w = SWA_HEADS * HEAD_DIM
    kv_w = SWA_KV_HEADS * HEAD_DIM
    assert w_in.shape[-1] == 3 * sb_w + sw_w + 2 * kv_w
    assert kv_w == LANES and SWA_HEADS // SWA_KV_HEADS == 4
    assert s % SB_WINDOW == 0 and s % SB_STEP_ROWS == 0 and s % TAIL_ROWS == 0
    assert (b * s) % ROW_TILE == 0 and WINDOW <= ATT_BLOCK and d % MXU_WIDTH == 0
    assert FFN_CHUNK % MXU_WIDTH == 0 and w_down.shape[1] % MXU_WIDTH == 0

    x2 = x.reshape(b * s, d)
    row_vec = lambda a: a.reshape(1, -1)

    names = ("q_sb", "k_sb", "v_sb", "q_sw", "k_sw", "v_sw")
    widths = (sb_w, sb_w, sb_w, sw_w, kv_w, kv_w)
    splits, lo = {}, 0
    for name, wd in zip(names, widths):
        splits[name] = (lo, lo + wd)
        lo += wd
    ha, q_sb, k_sb, v_sb, q_sw, kv_sw, w_out_b, w_gate_up_b, w_down_b = _in_projection(
        x2, row_vec(ln_in_g), row_vec(ln_in_b), w_in[0], splits,
        [w_out[0], w_gate_up[0], w_down[0]])

    to3 = lambda a: a.reshape(b, s, a.shape[-1])
    q_sb, k_sb, v_sb, q_sw, kv_sw = map(to3, (q_sb, k_sb, v_sb, q_sw, kv_sw))
    tail_args = (row_vec(sb_norm_g[0]), row_vec(swa_norm_g[0]), w_out_b,
                 row_vec(ln1_g[0]), row_vec(ln1_b[0]), w_gate_up_b, w_down_b,
                 row_vec(ln2_g[0]), row_vec(ln2_b[0]))

    fast, live = _fast_block(q_sw, kv_sw, q_sb, k_sb, v_sb, sinks[0], rel_bias, ha, *tail_args)
    n_tiles = (b * s) // TAIL_ROWS

    def general(_):
        (sb_out,) = _sb_attention(q_sb, k_sb, v_sb, [])
        return _attn_tail_block(q_sw, kv_sw, sinks[0], rel_bias, ha,
                                sb_out.reshape(b * s, sb_w), *tail_args)

    out = lax.cond(jnp.any(live[:n_tiles, 0, 0] < SB_DEAD_LOG2), general, lambda _: fast, None)
    return out.reshape(b, s, d)
```

```python
import functools
import math

import jax
import jax.numpy as jnp
import numpy as np
from jax import lax
from jax.experimental import pallas as pl
from jax.experimental.pallas import tpu as pltpu

F32 = jnp.float32
BF16 = jnp.bfloat16

HEAD_DIM = 64
SB_HEADS = 8
SWA_HEADS = 8
SWA_KV_HEADS = 2
WINDOW = 128
REL_BUCKETS = 32
REL_MAX_DIST = 128
LN_EPS = 1e-5
RMS_EPS = 1e-6
DEPTH = 1
ALPHA = (2 * DEPTH) ** 0.25
QK_SCALE = HEAD_DIM ** -0.5
LOG2E = math.log2(math.e)

LANES = 128
BF16_ROWS = 16
MXU_WIDTH = 256
ATT_BLOCK = 128
SB_WINDOW = 256
SB_Q_TILE = 64
SB_STEP_ROWS = 512
ROW_TILE = 1024
PROJ_PARTS = 2
TAIL_ROWS = 512
FFN_CHUNK = 4 * MXU_WIDTH
VMEM_LIMIT = 56 * 1024 * 1024

SB_DEAD_LOG2 = 127.0
MASK_NEG = -1e30


def _layer_norm(x, g, b):
    mu = jnp.mean(x, axis=-1, keepdims=True)
    xc = x - mu
    var = jnp.mean(xc * xc, axis=-1, keepdims=True)
    return xc * lax.rsqrt(var + LN_EPS) * g + b


def _rms_norm(x, g):
    ms = jnp.mean(x * x, axis=-1, keepdims=True)
    return x * lax.rsqrt(ms + RMS_EPS) * g


def _dot(a, b):
    return jnp.dot(a, b, preferred_element_type=F32)


def _dot_nt(a, b):
    return lax.dot_general(a, b, (((1,), (1,)), ((), ())), preferred_element_type=F32)


def _pipeline_schedule(n, stages):
    vals = [dict() for _ in stages]
    thunks = []
    for step in range(n + len(stages) - 1):
        for s, stage in enumerate(stages):
            i = step - s
            if 0 <= i < n:
                def thunk(s=s, i=i, stage=stage):
                    vals[s][i] = stage(i, vals[s - 1].pop(i)) if s else stage(i)
                thunks.append(thunk)
    return thunks, vals[-1]


def _software_pipeline(n, stages):
    thunks, results = _pipeline_schedule(n, stages)
    for thunk in thunks:
        thunk()
    return results


def _interleave(xs, ys):
    merged, taken = [], 0
    for i, x in enumerate(xs):
        merged.append(x)
        upto = (i + 1) * len(ys) // len(xs)
        merged.extend(ys[taken:upto])
        taken = upto
    return merged


def _inproj_kernel(splits, n_later, x_ref, g_ref, b_ref, w32_ref, *refs):
    later_in, refs = refs[:n_later], refs[n_later:]
    ha_ref, qsb_ref, ksb_ref, vsb_ref, qsw_ref, kv_ref = refs[:6]
    later_out, w_ref = refs[6:6 + n_later], refs[6 + n_later]
    rows = x_ref.shape[0] // PROJ_PARTS
    part = lambda i: slice(i * rows, (i + 1) * rows)
    assert splits["k_sw"][1] == splits["v_sw"][0]

    @pl.when(pl.program_id(0) == 0)
    def _convert_weight():
        w_ref[...] = w32_ref[...].astype(BF16)

    for src, dst in zip(later_in, later_out):
        dst[...] = src[...].astype(BF16)

    def normed(i):
        h32 = _layer_norm(x_ref[part(i), :], g_ref[...], b_ref[...])
        ha_ref[part(i), :] = ALPHA * h32
        return h32.astype(BF16)

    def projected(i, h):
        r = part(i)
        proj = lambda lo, hi: _dot(h, w_ref[:, lo:hi])
        qsb_ref[r, :] = (proj(*splits["q_sb"]) * QK_SCALE).astype(BF16)
        ksb_ref[r, :] = proj(*splits["k_sb"]).astype(BF16)
        vsb_ref[r, :] = proj(*splits["v_sb"]).astype(BF16)
        qsw_ref[r, :] = (proj(*splits["q_sw"]) * QK_SCALE).astype(BF16)
        kv = proj(splits["k_sw"][0], splits["v_sw"][1])
        kvw = kv.shape[-1] // 2
        k, v = kv[:, :kvw], kv[:, kvw:]
        kv_ref[r, 0 * kvw:1 * kvw] = k.astype(BF16)
        kv_ref[r, 1 * kvw:2 * kvw] = pltpu.roll(k, HEAD_DIM, axis=1).astype(BF16)
        kv_ref[r, 2 * kvw:3 * kvw] = v.astype(BF16)
        kv_ref[r, 3 * kvw:4 * kvw] = pltpu.roll(v, HEAD_DIM, axis=1).astype(BF16)

    _software_pipeline(PROJ_PARTS, [normed, projected])


def _in_projection(x2, g, b, w_in, splits, later_weights):
    n, d = x2.shape
    steps = n // ROW_TILE
    width = lambda name: splits[name][1] - splits[name][0]
    out_w = [width("q_sb"), width("k_sb"), width("v_sb"), width("q_sw"), 4 * width("k_sw")]
    row = lambda i: (i, 0)
    fixed = lambda i: (0, 0)
    later_specs = [
        pl.BlockSpec((size, w.shape[1]),
                     lambda i, last=w.shape[0] // size - 1: (jnp.minimum(i, last), 0))
        for w, size in zip(later_weights, _conversion_blocks(later_weights, steps))]
    return pl.pallas_call(
        functools.partial(_inproj_kernel, splits, len(later_weights)),
        grid=(steps,),
        in_specs=[
            pl.BlockSpec((ROW_TILE, d), row),
            pl.BlockSpec((1, d), fixed),
            pl.BlockSpec((1, d), fixed),
            pl.BlockSpec(w_in.shape, fixed, pipeline_mode=pl.Buffered(1)),
        ] + later_specs,
        out_specs=[pl.BlockSpec((ROW_TILE, d), row)]
        + [pl.BlockSpec((ROW_TILE, w), row) for w in out_w] + later_specs,
        out_shape=[jax.ShapeDtypeStruct((n, d), F32)]
        + [jax.ShapeDtypeStruct((n, w), BF16) for w in out_w]
        + [jax.ShapeDtypeStruct(w.shape, BF16) for w in later_weights],
        scratch_shapes=[pltpu.VMEM(w_in.shape, BF16)],
        compiler_params=pltpu.CompilerParams(
            dimension_semantics=("arbitrary",), vmem_limit_bytes=VMEM_LIMIT),
        name="ln_in_proj",
    )(x2, g, b, w_in, *later_weights)


def _conversion_blocks(weights, steps):
    sizes = []
    for w in weights:
        rows = w.shape[0]
        sizes.append(next(r for r in range(BF16_ROWS, rows + 1, BF16_ROWS)
                          if rows % r == 0 and rows // r <= steps))
    return sizes


def _split_heads(x, lane):
    zero = jnp.zeros_like(x)
    return jnp.concatenate([jnp.where(lane < HEAD_DIM, x, zero),
                            jnp.where(lane >= HEAD_DIM, x, zero)], axis=0)


def _merge_heads(o2, lane):
    t = o2.shape[0] // 2
    return jnp.where(lane < HEAD_DIM, o2[:t], o2[t:])


def _sb_masks(valid):
    return jnp.where(valid, LOG2E, 0.0), jnp.where(valid, 0.0, MASK_NEG)


def _sb_scores(qq, k, keep):
    z = _dot_nt(qq, k)
    zl = z * LOG2E
    c_nat = jnp.maximum(z, 0.0) + jnp.log(1.0 + jnp.exp2(-jnp.abs(zl)))
    cm = c_nat * keep
    return zl - cm, cm.astype(BF16), cm[:, :LANES]


def _sb_weights(scores, tri, kills, carrieds):
    sums = _dot(jnp.concatenate([sc[1] for sc in scores], axis=0), tri)
    rows = scores[0][0].shape[0]
    out = []
    for n, ((log_beta, _, cm0), kill, carried) in enumerate(zip(scores, kills, carrieds)):
        later = sums[n * rows:(n + 1) * rows]
        log_a = (log_beta - later) + kill
        if carried is not None:
            log_a = log_a - jnp.concatenate([carried] * (log_a.shape[1] // LANES), axis=1)
        out.append((jnp.exp2(log_a).astype(BF16), later[:, :LANES] + cm0))
    return out


def _sb_kernel(n_conv, q_ref, k_ref, v_ref, *refs):
    conv_in, o_ref, conv_out = refs[:n_conv], refs[n_conv], refs[n_conv + 1:2 * n_conv + 1]
    tri_ref, mask_ref, acc_ref, car_ref = refs[2 * n_conv + 1:]
    for src, dst in zip(conv_in, conv_out):
        dst[...] = src[...].astype(BF16)

    bi = pl.program_id(0)
    qi = pl.program_id(1)
    tq, w = SB_Q_TILE, SB_WINDOW
    subs = q_ref.shape[1] // tq
    pairs = q_ref.shape[-1] // LANES
    row0 = qi * (subs * tq)

    @pl.when(jnp.logical_and(bi == 0, qi == 0))
    def _fill_tri():
        j = lax.broadcasted_iota(jnp.int32, (w, w), 0)
        s = lax.broadcasted_iota(jnp.int32, (w, w), 1)
        tri_ref[...] = jnp.where(j > s, 1.0, 0.0).astype(BF16)

    lane = lax.broadcasted_iota(jnp.int32, (tq, LANES), 1)
    lanes_of = lambda p: slice(p * LANES, (p + 1) * LANES)
    rows_of = lambda h: slice(h * tq, (h + 1) * tq)
    first_start = lambda h: pl.multiple_of(jnp.maximum(row0 + (h + 1) * tq - w, 0), tq)

    def key_positions(start):
        return start + lax.broadcasted_iota(jnp.int32, (2 * tq, w), 1)

    assert subs * tq >= w - tq

    @pl.when(qi <= 1)
    def _fill_masks():
        row = lax.broadcasted_iota(jnp.int32, (2 * tq, w), 0) & (tq - 1)
        for h in range(subs):
            mask_ref[h, 0], mask_ref[h, 1] = _sb_masks(
                key_positions(first_start(h)) < row0 + h * tq + row)

    def scores(p, starts, keeps):
        return [_sb_scores(_split_heads(q_ref[0, rows_of(h), lanes_of(p)], lane),
                           k_ref[0, pl.ds(starts[h], w), lanes_of(p)], keeps[h])
                for h in range(subs)]

    first_starts = [first_start(h) for h in range(subs)]

    def first_scores(p):
        return scores(p, first_starts, [mask_ref[h, 0] for h in range(subs)])

    def first_weights(p, sc):
        return _sb_weights(sc, tri_ref[...], [mask_ref[h, 1] for h in range(subs)], [None] * subs)

    def first_values(p, aws):
        least = None
        for h, (a, total) in enumerate(aws):
            o2 = _dot(a, v_ref[0, pl.ds(first_starts[h], w), lanes_of(p)])
            o_ref[0, rows_of(h), lanes_of(p)] = _merge_heads(o2, lane).astype(o_ref.dtype)
            acc_ref[p, h] = o2
            car_ref[p, h] = total
            m = jnp.min(total[:, 0:1])
            least = m if least is None else jnp.minimum(least, m)
        return least

    first = _software_pipeline(pairs, [first_scores, first_weights, first_values])
    live = functools.reduce(jnp.minimum, [first[p] for p in range(pairs)])
    last_start = first_starts[-1]

    @pl.when(jnp.logical_and(last_start > 0, live < SB_DEAD_LOG2))
    def _walk_back():
        for p in range(pairs):
            for h in range(subs):
                car_ref[p, h] = jnp.broadcast_to(car_ref[p, h][:, 0:1], (2 * tq, LANES))

        def body(state):
            step, _ = state
            done = [jnp.maximum(first_start(h) - step * w, 0) for h in range(subs)]
            starts = [pl.multiple_of(jnp.maximum(d - w, 0), tq) for d in done]
            masks = [_sb_masks(key_positions(s) < d) for s, d in zip(starts, done)]

            def more_scores(p):
                return scores(p, starts, [m[0] for m in masks])

            def weights(p, sc):
                return _sb_weights(sc, tri_ref[...], [m[1] for m in masks],
                                   [car_ref[p, h] for h in range(subs)])

            def values(p, aws):
                least = None
                for h, (a, total) in enumerate(aws):
                    acc_ref[p, h] += _dot(a, v_ref[0, pl.ds(starts[h], w), lanes_of(p)])
                    carried = car_ref[p, h] + total[:, 0:1]
                    car_ref[p, h] = carried
                    m = jnp.min(carried)
                    least = m if least is None else jnp.minimum(least, m)
                return least

            mins = _software_pipeline(pairs, [more_scores, weights, values])
            return step + 1, functools.reduce(jnp.minimum, [mins[p] for p in range(pairs)])

        def cond(state):
            step, nxt = state
            return jnp.logical_and(last_start - step * w > 0, nxt < SB_DEAD_LOG2)

        lax.while_loop(cond, body, (jnp.int32(0), live))
        for p in range(pairs):
            for h in range(subs):
                o_ref[0, rows_of(h), lanes_of(p)] = _merge_heads(
                    acc_ref[p, h], lane).astype(o_ref.dtype)


def _sb_attention(q, k, v, weights):
    b, s, wd = q.shape
    pairs = wd // LANES
    subs = SB_STEP_ROWS // SB_Q_TILE
    nq = s // SB_STEP_ROWS
    tile = lambda bi, i: (bi, i, 0)
    whole = lambda bi, i: (bi, 0, 0)
    once = pl.Buffered(1)
    conv_specs = [
        pl.BlockSpec((size, w.shape[1]),
                     lambda bi, i, last=w.shape[0] // size - 1: (jnp.minimum(bi * nq + i, last), 0))
        for w, size in zip(weights, _conversion_blocks(weights, b * nq))]
    return pl.pallas_call(
        functools.partial(_sb_kernel, len(weights)),
        grid=(b, nq),
        in_specs=[
            pl.BlockSpec((1, SB_STEP_ROWS, wd), tile),
            pl.BlockSpec((1, s, wd), whole, pipeline_mode=once),
            pl.BlockSpec((1, s, wd), whole, pipeline_mode=once),
        ] + conv_specs,
        out_specs=[pl.BlockSpec((1, SB_STEP_ROWS, wd), tile)] + conv_specs,
        out_shape=[jax.ShapeDtypeStruct((b, s, wd), BF16)]
        + [jax.ShapeDtypeStruct(w.shape, BF16) for w in weights],
        scratch_shapes=[
            pltpu.VMEM((SB_WINDOW, SB_WINDOW), BF16),
            pltpu.VMEM((subs, 2, 2 * SB_Q_TILE, SB_WINDOW), F32),
            pltpu.VMEM((pairs, subs, 2 * SB_Q_TILE, LANES), F32),
            pltpu.VMEM((pairs, subs, 2 * SB_Q_TILE, LANES), F32),
        ],
        compiler_params=pltpu.CompilerParams(
            dimension_semantics=("arbitrary", "arbitrary"), vmem_limit_bytes=VMEM_LIMIT),
        name="stick_breaking_attention",
    )(q, k, v, *weights)


def _t5_log_bucket_starts():
    exact = REL_BUCKETS // 2
    d = np.arange(exact, 2 * REL_MAX_DIST, dtype=np.float32)
    large = exact + (np.log(d / np.float32(exact)) / np.float32(math.log(REL_MAX_DIST / exact))
                     * np.float32(REL_BUCKETS - exact)).astype(np.int32)
    large = np.minimum(large, REL_BUCKETS - 1)
    assert large[0] == exact and np.all(np.diff(large) >= 0)
    return [int(d[np.argmax(large >= bk)]) for bk in range(exact + 1, REL_BUCKETS)]


def _t5_causal_bucket(distance):
    d = jnp.maximum(distance, 0)
    bucket = jnp.minimum(d, REL_BUCKETS // 2)
    for start in _t5_log_bucket_starts():
        bucket = bucket + jnp.where(d >= start, 1, 0)
    return bucket


def _swa_fill_bias(relb_ref, bias_ref):
    t = ATT_BLOCK
    r = lax.broadcasted_iota(jnp.int32, (t, 2 * t), 0)
    c = lax.broadcasted_iota(jnp.int32, (t, 2 * t), 1)
    dist = r + t - c
    bucket = _t5_causal_bucket(dist)
    valid = jnp.logical_and(dist >= 0, dist < WINDOW)
    for h in range(SWA_HEADS):
        bias = jnp.zeros((t, 2 * t), F32)
        for bk in range(REL_BUCKETS):
            bias = jnp.where(bucket == bk, relb_ref[bk, h], bias)
        bias_ref[h] = jnp.where(valid, bias, MASK_NEG)


def _swa_schedule(sinks_ref, q_ref, kvc_ref, kvp_ref, bias_ref, o_ref, no_previous):
    t = ATT_BLOCK
    kvw = kvc_ref.shape[-1] // 4
    group = SWA_HEADS // SWA_KV_HEADS
    tiles = q_ref.shape[0] // t
    rows_of = lambda n: slice(n * t, (n + 1) * t)
    lane = lax.broadcasted_iota(jnp.int32, (t, LANES), 1)
    row2 = lax.broadcasted_iota(jnp.int32, (2 * t, 1), 0)
    col2 = lax.broadcasted_iota(jnp.int32, (2 * t, 2 * t), 1)
    prev_pen = jnp.where(jnp.logical_and(col2 < t, no_previous), MASK_NEG, 0.0)

    def kv_window(n, variant):
        cols = slice(variant * kvw, (variant + 1) * kvw)
        before = kvc_ref[rows_of(n - 1), cols] if n else kvp_ref[:, cols]
        return jnp.concatenate([before, kvc_ref[rows_of(n), cols]], axis=0)

    items = [(n, half, j) for n in range(tiles) for half in range(2) for j in range(SWA_KV_HEADS)]
    heads_of = lambda half, j: [h for h in range(j * group, (j + 1) * group) if h % 2 == half]

    def probabilities(i):
        n, half, j = items[i]
        heads = heads_of(half, j)
        in_half = (lane >= HEAD_DIM) if half else (lane < HEAD_DIM)
        qq = jnp.concatenate(
            [jnp.where(in_half, q_ref[rows_of(n), (h // 2) * LANES:(h // 2 + 1) * LANES], 0)
             for h in heads], axis=0)
        s = _dot_nt(qq, kv_window(n, 0 + int(j != half)))
        s = s + jnp.concatenate([bias_ref[h] for h in heads], axis=0)
        if n == 0:
            s = s + prev_pen
        sink = jnp.where(row2 < t, sinks_ref[heads[0]], sinks_ref[heads[1]])
        m = jnp.maximum(jnp.max(s, axis=-1, keepdims=True), sink)
        p = jnp.exp(s - m)
        denom = jnp.sum(p, axis=-1, keepdims=True) + jnp.exp(sink - m)
        return p.astype(BF16), 1.0 / denom

    def values(i, pd):
        n, half, j = items[i]
        p, inv_denom = pd
        return _dot(p, kv_window(n, 2 + int(j != half))) * inv_denom

    thunks, outs = _pipeline_schedule(len(items), [probabilities, values])

    def store():
        for i, (n, half, j) in enumerate(items):
            if half == 0:
                even, odd = outs[i], outs[items.index((n, 1, j))]
                for e, h in enumerate(heads_of(0, j)):
                    o_ref[rows_of(n), (h // 2) * LANES:(h // 2 + 1) * LANES] = jnp.where(
                        lane < HEAD_DIM, even[e * t:(e + 1) * t], odd[e * t:(e + 1) * t]
                    ).astype(o_ref.dtype)

    return thunks + [store]


def _tail_schedule(ha_ref, sb_ref, sw_ref, sbg_ref, swg_ref, wo_ref, l1g_ref, l1b_ref,
                   wgu_ref, wd_ref, l2g_ref, l2b_ref, o_ref):
    d = ha_ref.shape[1]
    d_ff = wd_ref.shape[0]
    mw = MXU_WIDTH
    bounds = list(range(0, d_ff, FFN_CHUNK)) + [d_ff]
    n_chunks = len(bounds) - 1
    out_tiles = range(d // mw)
    st = {}

    def merged():
        sb = _rms_norm(sb_ref[...].astype(F32), sbg_ref[...]).astype(BF16)
        sw = _rms_norm(sw_ref[...].astype(F32), swg_ref[...]).astype(BF16)
        st["merged"] = jnp.concatenate([sb, sw], axis=1)

    def out_projection(n):
        st["mix", n] = _dot(st["merged"], wo_ref[:, n * mw:(n + 1) * mw])

    def first_norm():
        mix = jnp.concatenate([st.pop(("mix", n)) for n in out_tiles], axis=1)
        st.pop("merged")
        h1 = _layer_norm(ha_ref[...] + mix, l1g_ref[...], l1b_ref[...])
        st["h1b"] = h1.astype(BF16)
        for n in out_tiles:
            st["acc", n] = ALPHA * h1[:, n * mw:(n + 1) * mw]

    def activation(c, n):
        lo = bounds[c] + n * mw
        gate = _dot(st["h1b"], wgu_ref[:, lo:lo + mw])
        up = _dot(st["h1b"], wgu_ref[:, d_ff + lo:d_ff + lo + mw])
        st["act", c, n] = (gate / (1.0 + jnp.exp(-gate)) * up).astype(BF16)

    def down(c, n):
        tiles = range((bounds[c + 1] - bounds[c]) // mw)
        if n == 0:
            st["act", c] = jnp.concatenate([st.pop(("act", c, k)) for k in tiles], axis=1)
        st["acc", n] = st["acc", n] + _dot(st["act", c], wd_ref[bounds[c]:bounds[c + 1],
                                                               n * mw:(n + 1) * mw])

    def second_norm():
        y = jnp.concatenate([st.pop(("acc", n)) for n in out_tiles], axis=1)
        o_ref[...] = _layer_norm(y, l2g_ref[...], l2b_ref[...])

    act_thunks = [[functools.partial(activation, c, n)
                   for n in range((bounds[c + 1] - bounds[c]) // mw)] for c in range(n_chunks)]
    down_thunks = [[functools.partial(down, c, n) for n in out_tiles] for c in range(n_chunks)]
    ffn = list(act_thunks[0])
    for c in range(1, n_chunks):
        ffn += _interleave(act_thunks[c], down_thunks[c - 1])
    ffn += down_thunks[-1]
    return (merged, [functools.partial(out_projection, n) for n in out_tiles], first_norm,
            ffn, second_norm)


def _attn_tail_kernel(tiles_per_seq, last_tile, sinks_ref, relb_ref, q_ref, kvc_ref, kvp_ref,
                      ha_ref, sb_ref, sbg_ref, swg_ref, wo_ref, l1g_ref, l1b_ref,
                      wgu_ref, wd_ref, l2g_ref, l2b_ref, o_ref, bias_ref, sw_ref):
    t = pl.program_id(0)

    @pl.when(t == 0)
    def _first_step():
        _swa_fill_bias(relb_ref, bias_ref)
        sw_ref[1] = jnp.zeros(sw_ref.shape[1:], sw_ref.dtype)

    no_previous = (jnp.minimum(t, last_tile) % tiles_per_seq) == 0
    attention = _swa_schedule(sinks_ref, q_ref.at[0], kvc_ref.at[0], kvp_ref.at[0], bias_ref,
                              sw_ref.at[t % 2], no_previous)
    merged, out_proj, first_norm, ffn, second_norm = _tail_schedule(
        ha_ref, sb_ref, sw_ref.at[(t + 1) % 2], sbg_ref, swg_ref, wo_ref,
        l1g_ref, l1b_ref, wgu_ref, wd_ref, l2g_ref, l2b_ref, o_ref)
    tail = [merged] + out_proj + [first_norm] + ffn + [second_norm]
    for thunk in _interleave(tail, attention):
        thunk()


def _attn_tail_block(q_sw, kv_sw, sinks, rel_bias, ha, sb2, sb_g, sw_g, w_out_b, ln1_g, ln1_b,
                     w_gate_up_b, w_down_b, ln2_g, ln2_b):
    b, s, wd = q_sw.shape
    n, d = ha.shape
    tiles_per_seq = s // TAIL_ROWS
    last_tile = n // TAIL_ROWS - 1
    key_tiles = TAIL_ROWS // ATT_BLOCK
    att = lambda t: jnp.minimum(t, last_tile)
    cur = lambda t: (att(t) // tiles_per_seq, att(t) % tiles_per_seq, 0)
    prev = lambda t: (att(t) // tiles_per_seq,
                      jnp.maximum(att(t) % tiles_per_seq * key_tiles - 1, 0), 0)
    row = lambda t: (jnp.maximum(t - 1, 0), 0)
    fixed = lambda t: (0, 0)
    smem = pl.BlockSpec(memory_space=pltpu.SMEM)
    vec = lambda a: pl.BlockSpec((1, a.shape[-1]), fixed)
    weight = lambda a: pl.BlockSpec(a.shape, fixed, pipeline_mode=pl.Buffered(1))
    return pl.pallas_call(
        functools.partial(_attn_tail_kernel, tiles_per_seq, last_tile),
        grid=(last_tile + 2,),
        in_specs=[
            smem, smem,
            pl.BlockSpec((1, TAIL_ROWS, wd), cur),
            pl.BlockSpec((1, TAIL_ROWS, kv_sw.shape[-1]), cur),
            pl.BlockSpec((1, ATT_BLOCK, kv_sw.shape[-1]), prev),
            pl.BlockSpec((TAIL_ROWS, d), row),
            pl.BlockSpec((TAIL_ROWS, sb2.shape[-1]), row),
            vec(sb_g), vec(sw_g), weight(w_out_b), vec(ln1_g), vec(ln1_b),
            weight(w_gate_up_b), weight(w_down_b), vec(ln2_g), vec(ln2_b),
        ],
        out_specs=pl.BlockSpec((TAIL_ROWS, d), row),
        out_shape=jax.ShapeDtypeStruct((n, d), F32),
        scratch_shapes=[
            pltpu.VMEM((SWA_HEADS, ATT_BLOCK, 2 * ATT_BLOCK), F32),
            pltpu.VMEM((2, TAIL_ROWS, wd), BF16),
        ],
        compiler_params=pltpu.CompilerParams(
            dimension_semantics=("arbitrary",), vmem_limit_bytes=VMEM_LIMIT),
        name="swa_attention_and_dense_tail",
    )(sinks, rel_bias, q_sw, kv_sw, kv_sw, ha, sb2, sb_g, sw_g, w_out_b, ln1_g, ln1_b,
      w_gate_up_b, w_down_b, ln2_g, ln2_b)


SB_GROUP = 2
SB_LIVE_DONE = 1e30


def _sb_fast_masks(mask_ref):
    tq, w = SB_Q_TILE, SB_WINDOW
    row = lax.broadcasted_iota(jnp.int32, (2 * tq, w), 0) & (tq - 1)
    col = lax.broadcasted_iota(jnp.int32, (2 * tq, w), 1)
    causal = col < row + (w - tq)
    mask_ref[0, 0], mask_ref[0, 1] = _sb_masks(causal)
    for h in range(mask_ref.shape[0] - 1):
        first_key = w - (h + 1) * tq
        mask_ref[1 + h, 0], mask_ref[1 + h, 1] = _sb_masks(
            jnp.logical_and(causal, col >= first_key))


def _sb_fast_schedule(q_ref, kc_ref, kp_ref, vc_ref, vp_ref, tri_ref, mask_ref, o_ref,
                      no_previous, lives):
    tq, w = SB_Q_TILE, SB_WINDOW
    subs = q_ref.shape[0] // tq
    pairs = q_ref.shape[-1] // LANES
    clamped = w // tq - 1
    assert mask_ref.shape[0] == clamped + 1 and subs % SB_GROUP == 0
    lane = lax.broadcasted_iota(jnp.int32, (tq, LANES), 1)
    lanes_of = lambda p: slice(p * LANES, (p + 1) * LANES)
    rows_of = lambda h: slice(h * tq, (h + 1) * tq)

    def window(h, cur_ref, prev_ref, p):
        start = (h + 1) * tq - w
        if start >= 0:
            return cur_ref[start:start + w, lanes_of(p)]
        return jnp.concatenate([prev_ref[w + start:w, lanes_of(p)],
                                cur_ref[0:w + start, lanes_of(p)]], axis=0)

    def mask(h, which):
        if h >= clamped:
            return mask_ref[0, which]
        return mask_ref[jnp.where(no_previous, 1 + h, 0), which]

    items = [(p, g) for p in range(pairs) for g in range(subs // SB_GROUP)]
    group = lambda g: range(g * SB_GROUP, (g + 1) * SB_GROUP)

    def scores(i):
        p, g = items[i]
        return [_sb_scores(_split_heads(q_ref[rows_of(h), lanes_of(p)], lane),
                           window(h, kc_ref, kp_ref, p), mask(h, 0)) for h in group(g)]

    def weights(i, sc):
        return _sb_weights(sc, tri_ref[...], [mask(h, 1) for h in group(items[i][1])],
                           [None] * SB_GROUP)

    def values(i, aws):
        p, g = items[i]
        for h, (a, total) in zip(group(g), aws):
            o2 = _dot(a, window(h, vc_ref, vp_ref, p))
            o_ref[rows_of(h), lanes_of(p)] = _merge_heads(o2, lane).astype(o_ref.dtype)
            m = jnp.min(total[:, 0:1])
            lives.append(m if h > clamped else jnp.where(no_previous, SB_LIVE_DONE, m))

    return _pipeline_schedule(len(items), [scores, weights, values])[0]


def _fast_kernel(tiles_per_seq, last_tile, sinks_ref, relb_ref, qsw_ref, kvc_ref, kvp_ref,
                 qsb_ref, kc_ref, kp_ref, vc_ref, vp_ref,
                 ha_ref, sbg_ref, swg_ref, wo_ref, l1g_ref, l1b_ref,
                 wgu_ref, wd_ref, l2g_ref, l2b_ref, o_ref, live_ref,
                 bias_ref, sw_ref, sb_ref, tri_ref, mask_ref):
    t = pl.program_id(0)

    @pl.when(t == 0)
    def _fill_constants():
        _swa_fill_bias(relb_ref, bias_ref)
        _sb_fast_masks(mask_ref)
        j = lax.broadcasted_iota(jnp.int32, tri_ref.shape, 0)
        s = lax.broadcasted_iota(jnp.int32, tri_ref.shape, 1)
        tri_ref[...] = jnp.where(j > s, 1.0, 0.0).astype(BF16)

    def body(with_tail):
        no_previous = (jnp.minimum(t, last_tile) % tiles_per_seq) == 0
        lives = []
        swa = _swa_schedule(sinks_ref, qsw_ref.at[0], kvc_ref.at[0], kvp_ref.at[0], bias_ref,
                            sw_ref.at[t % 2], no_previous)
        sb = _sb_fast_schedule(qsb_ref.at[0], kc_ref.at[0], kp_ref.at[0], vc_ref.at[0],
                               vp_ref.at[0], tri_ref, mask_ref, sb_ref.at[t % 2], no_previous,
                               lives)
        program = _interleave(swa, sb)
        if with_tail:
            merged, out_proj, first_norm, ffn, second_norm = _tail_schedule(
                ha_ref, sb_ref.at[(t + 1) % 2], sw_ref.at[(t + 1) % 2], sbg_ref, swg_ref,
                wo_ref, l1g_ref, l1b_ref, wgu_ref, wd_ref, l2g_ref, l2b_ref, o_ref)
            tail = [merged] + out_proj + [first_norm] + ffn + [second_norm]
            program = _interleave(tail, program)
        for thunk in program:
            thunk()
        live_ref[...] = jnp.full(live_ref.shape, functools.reduce(jnp.minimum, lives), F32)

    pl.when(t == 0)(functools.partial(body, False))
    pl.when(t > 0)(functools.partial(body, True))


def _fast_block(q_sw, kv_sw, q_sb, k_sb, v_sb, sinks, rel_bias, ha, sb_g, sw_g, w_out_b,
                ln1_g, ln1_b, w_gate_up_b, w_down_b, ln2_g, ln2_b):
    b, s, wd = q_sw.shape
    n, d = ha.shape
    tiles_per_seq = s // TAIL_ROWS
    last_tile = n // TAIL_ROWS - 1
    att = lambda t: jnp.minimum(t, last_tile)
    cur = lambda t: (att(t) // tiles_per_seq, att(t) % tiles_per_seq, 0)

    def prev(rows):
        per_tile = TAIL_ROWS // rows
        return lambda t: (att(t) // tiles_per_seq,
                          jnp.maximum(att(t) % tiles_per_seq * per_tile - 1, 0), 0)

    row = lambda t: (jnp.maximum(t - 1, 0), 0)
    fixed = lambda t: (0, 0)
    smem = pl.BlockSpec(memory_space=pltpu.SMEM)
    vec = lambda a: pl.BlockSpec((1, a.shape[-1]), fixed)
    weight = lambda a: pl.BlockSpec(a.shape, fixed, pipeline_mode=pl.Buffered(1))
    tile3 = lambda a: pl.BlockSpec((1, TAIL_ROWS, a.shape[-1]), cur)
    before = lambda a, rows: pl.BlockSpec((1, rows, a.shape[-1]), prev(rows))
    steps = last_tile + 2
    return pl.pallas_call(
        functools.partial(_fast_kernel, tiles_per_seq, last_tile),
        grid=(steps,),
        in_specs=[
            smem, smem,
            tile3(q_sw), tile3(kv_sw), before(kv_sw, ATT_BLOCK),
            tile3(q_sb), tile3(k_sb), before(k_sb, SB_WINDOW), tile3(v_sb), before(v_sb, SB_WINDOW),
            pl.BlockSpec((TAIL_ROWS, d), row),
            vec(sb_g), vec(sw_g), weight(w_out_b), vec(ln1_g), vec(ln1_b),
            weight(w_gate_up_b), weight(w_down_b), vec(ln2_g), vec(ln2_b),
        ],
        out_specs=[pl.BlockSpec((TAIL_ROWS, d), row),
                   pl.BlockSpec((1, 8, LANES), lambda t: (t, 0, 0))],
        out_shape=[jax.ShapeDtypeStruct((n, d), F32),
                   jax.ShapeDtypeStruct((steps, 8, LANES), F32)],
        scratch_shapes=[
            pltpu.VMEM((SWA_HEADS, ATT_BLOCK, 2 * ATT_BLOCK), F32),
            pltpu.VMEM((2, TAIL_ROWS, wd), BF16),
            pltpu.VMEM((2, TAIL_ROWS, q_sb.shape[-1]), BF16),
            pltpu.VMEM((SB_WINDOW, SB_WINDOW), BF16),
            pltpu.VMEM((SB_WINDOW // SB_Q_TILE, 2, 2 * SB_Q_TILE, SB_WINDOW), F32),
        ],
        compiler_params=pltpu.CompilerParams(
            dimension_semantics=("arbitrary",), vmem_limit_bytes=VMEM_LIMIT),
        name="attention_and_dense_tail",
    )(sinks, rel_bias, q_sw, kv_sw, kv_sw, q_sb, k_sb, k_sb, v_sb, v_sb, ha, sb_g, sw_g, w_out_b,
      ln1_g, ln1_b, w_gate_up_b, w_down_b, ln2_g, ln2_b)


def kernel(x, ln_in_g, ln_in_b, w_in, sb_norm_g, swa_norm_g, sinks, rel_bias, w_out,
           ln1_g, ln1_b, w_gate_up, w_down, ln2_g, ln2_b):
    b, s, d = x.shape
    assert w_in.shape[0] == DEPTH == 1
    sb_w = SB_HEADS * HEAD_DIM
    sw_w = SWA_HEADS * HEAD_DIM
    kv_w = SWA_KV_HEADS * HEAD_DIM
    assert w_in.shape[-1] == 3 * sb_w + sw_w + 2 * kv_w
    assert kv_w == LANES and SWA_HEADS // SWA_KV_HEADS == 4
    assert s % SB_WINDOW == 0 and s % SB_STEP_ROWS == 0 and s % TAIL_ROWS == 0
    assert (b * s) % ROW_TILE == 0 and WINDOW <= ATT_BLOCK and d % MXU_WIDTH == 0
    assert FFN_CHUNK % MXU_WIDTH == 0 and w_down.shape[1] % MXU_WIDTH == 0

    x2 = x.reshape(b * s, d)
    row_vec = lambda a: a.reshape(1, -1)

    names = ("q_sb", "k_sb", "v_sb", "q_sw", "k_sw", "v_sw")
    widths = (sb_w, sb_w, sb_w, sw_w, kv_w, kv_w)
    splits, lo = {}, 0
    for name, wd in zip(names, widths):
        splits[name] = (lo, lo + wd)
        lo += wd
    ha, q_sb, k_sb, v_sb, q_sw, kv_sw, w_out_b, w_gate_up_b, w_down_b = _in_projection(
        x2, row_vec(ln_in_g), row_vec(ln_in_b), w_in[0], splits,
        [w_out[0], w_gate_up[0], w_down[0]])

    to3 = lambda a: a.reshape(b, s, a.shape[-1])
    q_sb, k_sb, v_sb, q_sw, kv_sw = map(to3, (q_sb, k_sb, v_sb, q_sw, kv_sw))
    tail_args = (row_vec(sb_norm_g[0]), row_vec(swa_norm_g[0]), w_out_b,
                 row_vec(ln1_g[0]), row_vec(ln1_b[0]), w_gate_up_b, w_down_b,
                 row_vec(ln2_g[0]), row_vec(ln2_b[0]))

    fast, live = _fast_block(q_sw, kv_sw, q_sb, k_sb, v_sb, sinks[0], rel_bias, ha, *tail_args)
    n_tiles = (b * s) // TAIL_ROWS

    def general(_):
        (sb_out,) = _sb_attention(q_sb, k_sb, v_sb, [])
        return _attn_tail_block(q_sw, kv_sw, sinks[0], rel_bias, ha,
                                sb_out.reshape(b * s, sb_w), *tail_args)

    out = lax.cond(jnp.any(live[:n_tiles, 0, 0] < SB_DEAD_LOG2), general, lambda _: fast, None)
    return out.reshape(b, s, d)
```

```python
import functools
import math

import jax
import jax.numpy as jnp
import numpy as np
from jax import lax
from jax.experimental import pallas as pl
from jax.experimental.pallas import tpu as pltpu

F32 = jnp.float32
BF16 = jnp.bfloat16

HEAD_DIM = 64
SB_HEADS = 8
SWA_HEADS = 8
SWA_KV_HEADS = 2
WINDOW = 128
REL_BUCKETS = 32
REL_MAX_DIST = 128
LN_EPS = 1e-5
RMS_EPS = 1e-6
DEPTH = 1
ALPHA = (2 * DEPTH) ** 0.25
QK_SCALE = HEAD_DIM ** -0.5
LOG2E = math.log2(math.e)

LANES = 128
BF16_ROWS = 16
MXU_WIDTH = 256
ATT_BLOCK = 128
SB_WINDOW = 256
SB_Q_TILE = 64
SB_STEP_ROWS = 512
ROW_TILE = 1024
PROJ_PARTS = 2
TAIL_ROWS = 512
FFN_CHUNK = 4 * MXU_WIDTH
VMEM_LIMIT = 56 * 1024 * 1024

SB_DEAD_LOG2 = 127.0
MASK_NEG = -1e30


def _layer_norm(x, g, b):
    mu = jnp.mean(x, axis=-1, keepdims=True)
    xc = x - mu
    var = jnp.mean(xc * xc, axis=-1, keepdims=True)
    return xc * lax.rsqrt(var + LN_EPS) * g + b


def _rms_norm(x, g):
    ms = jnp.mean(x * x, axis=-1, keepdims=True)
    return x * lax.rsqrt(ms + RMS_EPS) * g


def _dot(a, b):
    return jnp.dot(a, b, preferred_element_type=F32)


def _dot_nt(a, b):
    return lax.dot_general(a, b, (((1,), (1,)), ((), ())), preferred_element_type=F32)


def _pipeline_schedule(n, stages):
    vals = [dict() for _ in stages]
    thunks = []
    for step in range(n + len(stages) - 1):
        for s, stage in enumerate(stages):
            i = step - s
            if 0 <= i < n:
                def thunk(s=s, i=i, stage=stage):
                    vals[s][i] = stage(i, vals[s - 1].pop(i)) if s else stage(i)
                thunks.append(thunk)
    return thunks, vals[-1]


def _software_pipeline(n, stages):
    thunks, results = _pipeline_schedule(n, stages)
    for thunk in thunks:
        thunk()
    return results


def _interleave(xs, ys):
    merged, taken = [], 0
    for i, x in enumerate(xs):
        merged.append(x)
        upto = (i + 1) * len(ys) // len(xs)
        merged.extend(ys[taken:upto])
        taken = upto
    return merged


def _inproj_kernel(splits, n_later, x_ref, g_ref, b_ref, w32_ref, *refs):
    later_in, refs = refs[:n_later], refs[n_later:]
    ha_ref, qsb_ref, ksb_ref, vsb_ref, qsw_ref, kv_ref = refs[:6]
    later_out, w_ref = refs[6:6 + n_later], refs[6 + n_later]
    rows = x_ref.shape[0] // PROJ_PARTS
    part = lambda i: slice(i * rows, (i + 1) * rows)
    assert splits["k_sw"][1] == splits["v_sw"][0]

    @pl.when(pl.program_id(0) == 0)
    def _convert_weight():
        w_ref[...] = w32_ref[...].astype(BF16)

    for src, dst in zip(later_in, later_out):
        dst[...] = src[...].astype(BF16)

    def normed(i):
        h32 = _layer_norm(x_ref[part(i), :], g_ref[...], b_ref[...])
        ha_ref[part(i), :] = ALPHA * h32
        return h32.astype(BF16)

    def projected(i, h):
        r = part(i)
        proj = lambda lo, hi: _dot(h, w_ref[:, lo:hi])
        qsb_ref[r, :] = (proj(*splits["q_sb"]) * QK_SCALE).astype(BF16)
        ksb_ref[r, :] = proj(*splits["k_sb"]).astype(BF16)
        vsb_ref[r, :] = proj(*splits["v_sb"]).astype(BF16)
        qsw_ref[r, :] = (proj(*splits["q_sw"]) * QK_SCALE).astype(BF16)
        kv = proj(splits["k_sw"][0], splits["v_sw"][1])
        kvw = kv.shape[-1] // 2
        k, v = kv[:, :kvw], kv[:, kvw:]
        kv_ref[r, 0 * kvw:1 * kvw] = k.astype(BF16)
        kv_ref[r, 1 * kvw:2 * kvw] = pltpu.roll(k, HEAD_DIM, axis=1).astype(BF16)
        kv_ref[r, 2 * kvw:3 * kvw] = v.astype(BF16)
        kv_ref[r, 3 * kvw:4 * kvw] = pltpu.roll(v, HEAD_DIM, axis=1).astype(BF16)

    _software_pipeline(PROJ_PARTS, [normed, projected])


def _in_projection(x2, g, b, w_in, splits, later_weights):
    n, d = x2.shape
    steps = n // ROW_TILE
    width = lambda name: splits[name][1] - splits[name][0]
    out_w = [width("q_sb"), width("k_sb"), width("v_sb"), width("q_sw"), 4 * width("k_sw")]
    row = lambda i: (i, 0)
    fixed = lambda i: (0, 0)
    later_specs = [
        pl.BlockSpec((size, w.shape[1]),
                     lambda i, last=w.shape[0] // size - 1: (jnp.minimum(i, last), 0))
        for w, size in zip(later_weights, _conversion_blocks(later_weights, steps))]
    return pl.pallas_call(
        functools.partial(_inproj_kernel, splits, len(later_weights)),
        grid=(steps,),
        in_specs=[
            pl.BlockSpec((ROW_TILE, d), row),
            pl.BlockSpec((1, d), fixed),
            pl.BlockSpec((1, d), fixed),
            pl.BlockSpec(w_in.shape, fixed, pipeline_mode=pl.Buffered(1)),
        ] + later_specs,
        out_specs=[pl.BlockSpec((ROW_TILE, d), row)]
        + [pl.BlockSpec((ROW_TILE, w), row) for w in out_w] + later_specs,
        out_shape=[jax.ShapeDtypeStruct((n, d), F32)]
        + [jax.ShapeDtypeStruct((n, w), BF16) for w in out_w]
        + [jax.ShapeDtypeStruct(w.shape, BF16) for w in later_weights],
        scratch_shapes=[pltpu.VMEM(w_in.shape, BF16)],
        compiler_params=pltpu.CompilerParams(
            dimension_semantics=("arbitrary",), vmem_limit_bytes=VMEM_LIMIT),
        name="ln_in_proj",
    )(x2, g, b, w_in, *later_weights)


def _conversion_blocks(weights, steps):
    sizes = []
    for w in weights:
        rows = w.shape[0]
        sizes.append(next(r for r in range(BF16_ROWS, rows + 1, BF16_ROWS)
                          if rows % r == 0 and rows // r <= steps))
    return sizes


def _split_heads(x, lane):
    zero = jnp.zeros_like(x)
    return jnp.concatenate([jnp.where(lane < HEAD_DIM, x, zero),
                            jnp.where(lane >= HEAD_DIM, x, zero)], axis=0)


def _merge_heads(o2, lane):
    t = o2.shape[0] // 2
    return jnp.where(lane < HEAD_DIM, o2[:t], o2[t:])


def _sb_masks(valid):
    return jnp.where(valid, LOG2E, 0.0), jnp.where(valid, 0.0, MASK_NEG)


def _sb_scores(qq, k, keep):
    z = _dot_nt(qq, k)
    zl = z * LOG2E
    c_nat = jnp.maximum(z, 0.0) + jnp.log(1.0 + jnp.exp2(-jnp.abs(zl)))
    cm = c_nat * keep
    return zl - cm, cm.astype(BF16), cm[:, :LANES]


def _sb_weights(scores, tri, kills, carrieds):
    sums = _dot(jnp.concatenate([sc[1] for sc in scores], axis=0), tri)
    rows = scores[0][0].shape[0]
    out = []
    for n, ((log_beta, _, cm0), kill, carried) in enumerate(zip(scores, kills, carrieds)):
        later = sums[n * rows:(n + 1) * rows]
        log_a = (log_beta - later) + kill
        if carried is not None:
            log_a = log_a - jnp.concatenate([carried] * (log_a.shape[1] // LANES), axis=1)
        out.append((jnp.exp2(log_a).astype(BF16), later[:, :LANES] + cm0))
    return out


def _sb_kernel(n_conv, q_ref, k_ref, v_ref, *refs):
    conv_in, o_ref, conv_out = refs[:n_conv], refs[n_conv], refs[n_conv + 1:2 * n_conv + 1]
    tri_ref, mask_ref, acc_ref, car_ref = refs[2 * n_conv + 1:]
    for src, dst in zip(conv_in, conv_out):
        dst[...] = src[...].astype(BF16)

    bi = pl.program_id(0)
    qi = pl.program_id(1)
    tq, w = SB_Q_TILE, SB_WINDOW
    subs = q_ref.shape[1] // tq
    pairs = q_ref.shape[-1] // LANES
    row0 = qi * (subs * tq)

    @pl.when(jnp.logical_and(bi == 0, qi == 0))
    def _fill_tri():
        j = lax.broadcasted_iota(jnp.int32, (w, w), 0)
        s = lax.broadcasted_iota(jnp.int32, (w, w), 1)
        tri_ref[...] = jnp.where(j > s, 1.0, 0.0).astype(BF16)

    lane = lax.broadcasted_iota(jnp.int32, (tq, LANES), 1)
    lanes_of = lambda p: slice(p * LANES, (p + 1) * LANES)
    rows_of = lambda h: slice(h * tq, (h + 1) * tq)
    first_start = lambda h: pl.multiple_of(jnp.maximum(row0 + (h + 1) * tq - w, 0), tq)

    def key_positions(start):
        return start + lax.broadcasted_iota(jnp.int32, (2 * tq, w), 1)

    assert subs * tq >= w - tq

    @pl.when(qi <= 1)
    def _fill_masks():
        row = lax.broadcasted_iota(jnp.int32, (2 * tq, w), 0) & (tq - 1)
        for h in range(subs):
            mask_ref[h, 0], mask_ref[h, 1] = _sb_masks(
                key_positions(first_start(h)) < row0 + h * tq + row)

    def scores(p, starts, keeps):
        return [_sb_scores(_split_heads(q_ref[0, rows_of(h), lanes_of(p)], lane),
                           k_ref[0, pl.ds(starts[h], w), lanes_of(p)], keeps[h])
                for h in range(subs)]

    first_starts = [first_start(h) for h in range(subs)]

    def first_scores(p):
        return scores(p, first_starts, [mask_ref[h, 0] for h in range(subs)])

    def first_weights(p, sc):
        return _sb_weights(sc, tri_ref[...], [mask_ref[h, 1] for h in range(subs)], [None] * subs)

    def first_values(p, aws):
        least = None
        for h, (a, total) in enumerate(aws):
            o2 = _dot(a, v_ref[0, pl.ds(first_starts[h], w), lanes_of(p)])
            o_ref[0, rows_of(h), lanes_of(p)] = _merge_heads(o2, lane).astype(o_ref.dtype)
            acc_ref[p, h] = o2
            car_ref[p, h] = total
            m = jnp.min(total[:, 0:1])
            least = m if least is None else jnp.minimum(least, m)
        return least

    first = _software_pipeline(pairs, [first_scores, first_weights, first_values])
    live = functools.reduce(jnp.minimum, [first[p] for p in range(pairs)])
    last_start = first_starts[-1]

    @pl.when(jnp.logical_and(last_start > 0, live < SB_DEAD_LOG2))
    def _walk_back():
        for p in range(pairs):
            for h in range(subs):
                car_ref[p, h] = jnp.broadcast_to(car_ref[p, h][:, 0:1], (2 * tq, LANES))

        def body(state):
            step, _ = state
            done = [jnp.maximum(first_start(h) - step * w, 0) for h in range(subs)]
            starts = [pl.multiple_of(jnp.maximum(d - w, 0), tq) for d in done]
            masks = [_sb_masks(key_positions(s) < d) for s, d in zip(starts, done)]

            def more_scores(p):
                return scores(p, starts, [m[0] for m in masks])

            def weights(p, sc):
                return _sb_weights(sc, tri_ref[...], [m[1] for m in masks],
                                   [car_ref[p, h] for h in range(subs)])

            def values(p, aws):
                least = None
                for h, (a, total) in enumerate(aws):
                    acc_ref[p, h] += _dot(a, v_ref[0, pl.ds(starts[h], w), lanes_of(p)])
                    carried = car_ref[p, h] + total[:, 0:1]
                    car_ref[p, h] = carried
                    m = jnp.min(carried)
                    least = m if least is None else jnp.minimum(least, m)
                return least

            mins = _software_pipeline(pairs, [more_scores, weights, values])
            return step + 1, functools.reduce(jnp.minimum, [mins[p] for p in range(pairs)])

        def cond(state):
            step, nxt = state
            return jnp.logical_and(last_start - step * w > 0, nxt < SB_DEAD_LOG2)

        lax.while_loop(cond, body, (jnp.int32(0), live))
        for p in range(pairs):
            for h in range(subs):
                o_ref[0, rows_of(h), lanes_of(p)] = _merge_heads(
                    acc_ref[p, h], lane).astype(o_ref.dtype)


def _sb_attention(q, k, v, weights):
    b, s, wd = q.shape
    pairs = wd // LANES
    subs = SB_STEP_ROWS // SB_Q_TILE
    nq = s // SB_STEP_ROWS
    tile = lambda bi, i: (bi, i, 0)
    whole = lambda bi, i: (bi, 0, 0)
    once = pl.Buffered(1)
    conv_specs = [
        pl.BlockSpec((size, w.shape[1]),
                     lambda bi, i, last=w.shape[0] // size - 1: (jnp.minimum(bi * nq + i, last), 0))
        for w, size in zip(weights, _conversion_blocks(weights, b * nq))]
    return pl.pallas_call(
        functools.partial(_sb_kernel, len(weights)),
        grid=(b, nq),
        in_specs=[
            pl.BlockSpec((1, SB_STEP_ROWS, wd), tile),
            pl.BlockSpec((1, s, wd), whole, pipeline_mode=once),
            pl.BlockSpec((1, s, wd), whole, pipeline_mode=once),
        ] + conv_specs,
        out_specs=[pl.BlockSpec((1, SB_STEP_ROWS, wd), tile)] + conv_specs,
        out_shape=[jax.ShapeDtypeStruct((b, s, wd), BF16)]
        + [jax.ShapeDtypeStruct(w.shape, BF16) for w in weights],
        scratch_shapes=[
            pltpu.VMEM((SB_WINDOW, SB_WINDOW), BF16),
            pltpu.VMEM((subs, 2, 2 * SB_Q_TILE, SB_WINDOW), F32),
            pltpu.VMEM((pairs, subs, 2 * SB_Q_TILE, LANES), F32),
            pltpu.VMEM((pairs, subs, 2 * SB_Q_TILE, LANES), F32),
        ],
        compiler_params=pltpu.CompilerParams(
            dimension_semantics=("arbitrary", "arbitrary"), vmem_limit_bytes=VMEM_LIMIT),
        name="stick_breaking_attention",
    )(q, k, v, *weights)


def _t5_log_bucket_starts():
    exact = REL_BUCKETS // 2
    d = np.arange(exact, 2 * REL_MAX_DIST, dtype=np.float32)
    large = exact + (np.log(d / np.float32(exact)) / np.float32(math.log(REL_MAX_DIST / exact))
                     * np.float32(REL_BUCKETS - exact)).astype(np.int32)
    large = np.minimum(large, REL_BUCKETS - 1)
    assert large[0] == exact and np.all(np.diff(large) >= 0)
    return [int(d[np.argmax(large >= bk)]) for bk in range(exact + 1, REL_BUCKETS)]


def _t5_causal_bucket(distance):
    d = jnp.maximum(distance, 0)
    bucket = jnp.minimum(d, REL_BUCKETS // 2)
    for start in _t5_log_bucket_starts():
        bucket = bucket + jnp.where(d >= start, 1, 0)
    return bucket


def _swa_fill_bias(relb_ref, bias_ref):
    t = ATT_BLOCK
    r = lax.broadcasted_iota(jnp.int32, (t, 2 * t), 0)
    c = lax.broadcasted_iota(jnp.int32, (t, 2 * t), 1)
    dist = r + t - c
    bucket = _t5_causal_bucket(dist)
    valid = jnp.logical_and(dist >= 0, dist < WINDOW)
    for h in range(SWA_HEADS):
        bias = jnp.zeros((t, 2 * t), F32)
        for bk in range(REL_BUCKETS):
            bias = jnp.where(bucket == bk, relb_ref[bk, h], bias)
        bias_ref[h] = jnp.where(valid, bias, MASK_NEG)


def _swa_schedule(sinks_ref, q_ref, kvc_ref, kvp_ref, bias_ref, o_ref, no_previous):
    t = ATT_BLOCK
    kvw = kvc_ref.shape[-1] // 4
    group = SWA_HEADS // SWA_KV_HEADS
    tiles = q_ref.shape[0] // t
    rows_of = lambda n: slice(n * t, (n + 1) * t)
    lane = lax.broadcasted_iota(jnp.int32, (t, LANES), 1)
    row2 = lax.broadcasted_iota(jnp.int32, (2 * t, 1), 0)
    col2 = lax.broadcasted_iota(jnp.int32, (2 * t, 2 * t), 1)
    prev_pen = jnp.where(jnp.logical_and(col2 < t, no_previous), MASK_NEG, 0.0)

    def kv_window(n, variant):
        cols = slice(variant * kvw, (variant + 1) * kvw)
        before = kvc_ref[rows_of(n - 1), cols] if n else kvp_ref[:, cols]
        return jnp.concatenate([before, kvc_ref[rows_of(n), cols]], axis=0)

    items = [(n, half, j) for n in range(tiles) for half in range(2) for j in range(SWA_KV_HEADS)]
    heads_of = lambda half, j: [h for h in range(j * group, (j + 1) * group) if h % 2 == half]

    def probabilities(i):
        n, half, j = items[i]
        heads = heads_of(half, j)
        in_half = (lane >= HEAD_DIM) if half else (lane < HEAD_DIM)
        qq = jnp.concatenate(
            [jnp.where(in_half, q_ref[rows_of(n), (h // 2) * LANES:(h // 2 + 1) * LANES], 0)
             for h in heads], axis=0)
        s = _dot_nt(qq, kv_window(n, 0 + int(j != half)))
        s = s + jnp.concatenate([bias_ref[h] for h in heads], axis=0)
        if n == 0:
            s = s + prev_pen
        sink = jnp.where(row2 < t, sinks_ref[heads[0]], sinks_ref[heads[1]])
        m = jnp.maximum(jnp.max(s, axis=-1, keepdims=True), sink)
        p = jnp.exp(s - m)
        denom = jnp.sum(p, axis=-1, keepdims=True) + jnp.exp(sink - m)
        return p.astype(BF16), 1.0 / denom

    def values(i, pd):
        n, half, j = items[i]
        p, inv_denom = pd
        return _dot(p, kv_window(n, 2 + int(j != half))) * inv_denom

    thunks, outs = _pipeline_schedule(len(items), [probabilities, values])

    def store():
        for i, (n, half, j) in enumerate(items):
            if half == 0:
                even, odd = outs[i], outs[items.index((n, 1, j))]
                for e, h in enumerate(heads_of(0, j)):
                    o_ref[rows_of(n), (h // 2) * LANES:(h // 2 + 1) * LANES] = jnp.where(
                        lane < HEAD_DIM, even[e * t:(e + 1) * t], odd[e * t:(e + 1) * t]
                    ).astype(o_ref.dtype)

    return thunks + [store]


def _tail_schedule(ha_ref, sb_ref, sw_ref, sbg_ref, swg_ref, wo_ref, l1g_ref, l1b_ref,
                   wgu_ref, wd_ref, l2g_ref, l2b_ref, o_ref):
    d = ha_ref.shape[1]
    d_ff = wd_ref.shape[0]
    mw = MXU_WIDTH
    bounds = list(range(0, d_ff, FFN_CHUNK)) + [d_ff]
    n_chunks = len(bounds) - 1
    out_tiles = range(d // mw)
    st = {}

    def merged():
        sb = _rms_norm(sb_ref[...].astype(F32), sbg_ref[...]).astype(BF16)
        sw = _rms_norm(sw_ref[...].astype(F32), swg_ref[...]).astype(BF16)
        st["merged"] = jnp.concatenate([sb, sw], axis=1)

    def out_projection(n):
        st["mix", n] = _dot(st["merged"], wo_ref[:, n * mw:(n + 1) * mw])

    def first_norm():
        mix = jnp.concatenate([st.pop(("mix", n)) for n in out_tiles], axis=1)
        st.pop("merged")
        h1 = _layer_norm(ha_ref[...] + mix, l1g_ref[...], l1b_ref[...])
        st["h1b"] = h1.astype(BF16)
        for n in out_tiles:
            st["acc", n] = ALPHA * h1[:, n * mw:(n + 1) * mw]

    def activation(c, n):
        lo = bounds[c] + n * mw
        gate = _dot(st["h1b"], wgu_ref[:, lo:lo + mw])
        up = _dot(st["h1b"], wgu_ref[:, d_ff + lo:d_ff + lo + mw])
        st["act", c, n] = (gate / (1.0 + jnp.exp(-gate)) * up).astype(BF16)

    def down(c, n):
        tiles = range((bounds[c + 1] - bounds[c]) // mw)
        if n == 0:
            st["act", c] = jnp.concatenate([st.pop(("act", c, k)) for k in tiles], axis=1)
        st["acc", n] = st["acc", n] + _dot(st["act", c], wd_ref[bounds[c]:bounds[c + 1],
                                                               n * mw:(n + 1) * mw])

    def second_norm():
        y = jnp.concatenate([st.pop(("acc", n)) for n in out_tiles], axis=1)
        o_ref[...] = _layer_norm(y, l2g_ref[...], l2b_ref[...])

    act_thunks = [[functools.partial(activation, c, n)
                   for n in range((bounds[c + 1] - bounds[c]) // mw)] for c in range(n_chunks)]
    down_thunks = [[functools.partial(down, c, n) for n in out_tiles] for c in range(n_chunks)]
    ffn = list(act_thunks[0])
    for c in range(1, n_chunks):
        ffn += _interleave(act_thunks[c], down_thunks[c - 1])
    ffn += down_thunks[-1]
    return (merged, [functools.partial(out_projection, n) for n in out_tiles], first_norm,
            ffn, second_norm)


def _attn_tail_kernel(tiles_per_seq, last_tile, sinks_ref, relb_ref, q_ref, kvc_ref, kvp_ref,
                      ha_ref, sb_ref, sbg_ref, swg_ref, wo_ref, l1g_ref, l1b_ref,
                      wgu_ref, wd_ref, l2g_ref, l2b_ref, o_ref, bias_ref, sw_ref):
    t = pl.program_id(0)

    @pl.when(t == 0)
    def _first_step():
        _swa_fill_bias(relb_ref, bias_ref)
        sw_ref[1] = jnp.zeros(sw_ref.shape[1:], sw_ref.dtype)

    no_previous = (jnp.minimum(t, last_tile) % tiles_per_seq) == 0
    attention = _swa_schedule(sinks_ref, q_ref.at[0], kvc_ref.at[0], kvp_ref.at[0], bias_ref,
                              sw_ref.at[t % 2], no_previous)
    merged, out_proj, first_norm, ffn, second_norm = _tail_schedule(
        ha_ref, sb_ref, sw_ref.at[(t + 1) % 2], sbg_ref, swg_ref, wo_ref,
        l1g_ref, l1b_ref, wgu_ref, wd_ref, l2g_ref, l2b_ref, o_ref)
    tail = [merged] + out_proj + [first_norm] + ffn + [second_norm]
    for thunk in _interleave(tail, attention):
        thunk()


def _attn_tail_block(q_sw, kv_sw, sinks, rel_bias, ha, sb2, sb_g, sw_g, w_out_b, ln1_g, ln1_b,
                     w_gate_up_b, w_down_b, ln2_g, ln2_b):
    b, s, wd = q_sw.shape
    n, d = ha.shape
    tiles_per_seq = s // TAIL_ROWS
    last_tile = n // TAIL_ROWS - 1
    key_tiles = TAIL_ROWS // ATT_BLOCK
    att = lambda t: jnp.minimum(t, last_tile)
    cur = lambda t: (att(t) // tiles_per_seq, att(t) % tiles_per_seq, 0)
    prev = lambda t: (att(t) // tiles_per_seq,
                      jnp.maximum(att(t) % tiles_per_seq * key_tiles - 1, 0), 0)
    row = lambda t: (jnp.maximum(t - 1, 0), 0)
    fixed = lambda t: (0, 0)
    smem = pl.BlockSpec(memory_space=pltpu.SMEM)
    vec = lambda a: pl.BlockSpec((1, a.shape[-1]), fixed)
    weight = lambda a: pl.BlockSpec(a.shape, fixed, pipeline_mode=pl.Buffered(1))
    return pl.pallas_call(
        functools.partial(_attn_tail_kernel, tiles_per_seq, last_tile),
        grid=(last_tile + 2,),
        in_specs=[
            smem, smem,
            pl.BlockSpec((1, TAIL_ROWS, wd), cur),
            pl.BlockSpec((1, TAIL_ROWS, kv_sw.shape[-1]), cur),
            pl.BlockSpec((1, ATT_BLOCK, kv_sw.shape[-1]), prev),
            pl.BlockSpec((TAIL_ROWS, d), row),
            pl.BlockSpec((TAIL_ROWS, sb2.shape[-1]), row),
            vec(sb_g), vec(sw_g), weight(w_out_b), vec(ln1_g), vec(ln1_b),
            weight(w_gate_up_b), weight(w_down_b), vec(ln2_g), vec(ln2_b),
        ],
        out_specs=pl.BlockSpec((TAIL_ROWS, d), row),
        out_shape=jax.ShapeDtypeStruct((n, d), F32),
        scratch_shapes=[
            pltpu.VMEM((SWA_HEADS, ATT_BLOCK, 2 * ATT_BLOCK), F32),
            pltpu.VMEM((2, TAIL_ROWS, wd), BF16),
        ],
        compiler_params=pltpu.CompilerParams(
            dimension_semantics=("arbitrary",), vmem_limit_bytes=VMEM_LIMIT),
        name="swa_attention_and_dense_tail",
    )(sinks, rel_bias, q_sw, kv_sw, kv_sw, ha, sb2, sb_g, sw_g, w_out_b, ln1_g, ln1_b,
      w_gate_up_b, w_down_b, ln2_g, ln2_b)


SB_GROUP = 2
SB_LIVE_DONE = 1e30


def _sb_fast_masks(mask_ref):
    tq, w = SB_Q_TILE, SB_WINDOW
    row = lax.broadcasted_iota(jnp.int32, (2 * tq, w), 0) & (tq - 1)
    col = lax.broadcasted_iota(jnp.int32, (2 * tq, w), 1)
    causal = col < row + (w - tq)
    mask_ref[0, 0], mask_ref[0, 1] = _sb_masks(causal)
    for h in range(mask_ref.shape[0] - 1):
        first_key = w - (h + 1) * tq
        mask_ref[1 + h, 0], mask_ref[1 + h, 1] = _sb_masks(
            jnp.logical_and(causal, col >= first_key))


def _sb_fast_schedule(q_ref, kc_ref, kp_ref, vc_ref, vp_ref, tri_ref, mask_ref, o_ref,
                      no_previous, lives):
    tq, w = SB_Q_TILE, SB_WINDOW
    subs = q_ref.shape[0] // tq
    pairs = q_ref.shape[-1] // LANES
    clamped = w // tq - 1
    assert mask_ref.shape[0] == clamped + 1 and subs % SB_GROUP == 0
    lane = lax.broadcasted_iota(jnp.int32, (tq, LANES), 1)
    lanes_of = lambda p: slice(p * LANES, (p + 1) * LANES)
    rows_of = lambda h: slice(h * tq, (h + 1) * tq)

    def window(h, cur_ref, prev_ref, p):
        start = (h + 1) * tq - w
        if start >= 0:
            return cur_ref[start:start + w, lanes_of(p)]
        return jnp.concatenate([prev_ref[w + start:w, lanes_of(p)],
                                cur_ref[0:w + start, lanes_of(p)]], axis=0)

    def mask(h, which):
        if h >= clamped:
            return mask_ref[0, which]
        return mask_ref[jnp.where(no_previous, 1 + h, 0), which]

    items = [(p, g) for p in range(pairs) for g in range(subs // SB_GROUP)]
    group = lambda g: range(g * SB_GROUP, (g + 1) * SB_GROUP)

    def scores(i):
        p, g = items[i]
        return [_sb_scores(_split_heads(q_ref[rows_of(h), lanes_of(p)], lane),
                           window(h, kc_ref, kp_ref, p), mask(h, 0)) for h in group(g)]

    def weights(i, sc):
        return _sb_weights(sc, tri_ref[...], [mask(h, 1) for h in group(items[i][1])],
                           [None] * SB_GROUP)

    def values(i, aws):
        p, g = items[i]
        for h, (a, total) in zip(group(g), aws):
            o2 = _dot(a, window(h, vc_ref, vp_ref, p))
            o_ref[rows_of(h), lanes_of(p)] = _merge_heads(o2, lane).astype(o_ref.dtype)
            m = jnp.min(total[:, 0:1])
            lives.append(m if h > clamped else jnp.where(no_previous, SB_LIVE_DONE, m))

    return _pipeline_schedule(len(items), [scores, weights, values])[0]


def _fast_kernel(tiles_per_seq, last_tile, sinks_ref, relb_ref, qsw_ref, kvc_ref, kvp_ref,
                 qsb_ref, kc_ref, kp_ref, vc_ref, vp_ref,
                 ha_ref, sbg_ref, swg_ref, wo_ref, l1g_ref, l1b_ref,
                 wgu_ref, wd_ref, l2g_ref, l2b_ref, o_ref, live_ref,
                 bias_ref, sw_ref, sb_ref, tri_ref, mask_ref):
    t = pl.program_id(0)

    @pl.when(t == 0)
    def _first_step():
        _swa_fill_bias(relb_ref, bias_ref)
        _sb_fast_masks(mask_ref)
        j = lax.broadcasted_iota(jnp.int32, tri_ref.shape, 0)
        s = lax.broadcasted_iota(jnp.int32, tri_ref.shape, 1)
        tri_ref[...] = jnp.where(j > s, 1.0, 0.0).astype(BF16)
        sw_ref[1] = jnp.zeros(sw_ref.shape[1:], sw_ref.dtype)
        sb_ref[1] = jnp.zeros(sb_ref.shape[1:], sb_ref.dtype)

    no_previous = (jnp.minimum(t, last_tile) % tiles_per_seq) == 0
    lives = []
    swa = _swa_schedule(sinks_ref, qsw_ref.at[0], kvc_ref.at[0], kvp_ref.at[0], bias_ref,
                        sw_ref.at[t % 2], no_previous)
    sb = _sb_fast_schedule(qsb_ref.at[0], kc_ref.at[0], kp_ref.at[0], vc_ref.at[0], vp_ref.at[0],
                           tri_ref, mask_ref, sb_ref.at[t % 2], no_previous, lives)
    merged, out_proj, first_norm, ffn, second_norm = _tail_schedule(
        ha_ref, sb_ref.at[(t + 1) % 2], sw_ref.at[(t + 1) % 2], sbg_ref, swg_ref, wo_ref,
        l1g_ref, l1b_ref, wgu_ref, wd_ref, l2g_ref, l2b_ref, o_ref)
    attention = _interleave(swa, sb)
    lead, lag = len(attention) // 16, len(attention) // 12
    tail = [merged] + out_proj + [first_norm] + ffn
    program = (attention[:lead] + _interleave(tail, attention[lead:-lag])
               + [second_norm] + attention[-lag:])
    for thunk in program:
        thunk()
    live_ref[...] = jnp.full(live_ref.shape, functools.reduce(jnp.minimum, lives), F32)


def _fast_block(q_sw, kv_sw, q_sb, k_sb, v_sb, sinks, rel_bias, ha, sb_g, sw_g, w_out_b,
                ln1_g, ln1_b, w_gate_up_b, w_down_b, ln2_g, ln2_b):
    b, s, wd = q_sw.shape
    n, d = ha.shape
    tiles_per_seq = s // TAIL_ROWS
    last_tile = n // TAIL_ROWS - 1
    att = lambda t: jnp.minimum(t, last_tile)
    cur = lambda t: (att(t) // tiles_per_seq, att(t) % tiles_per_seq, 0)

    def prev(rows):
        per_tile = TAIL_ROWS // rows
        return lambda t: (att(t) // tiles_per_seq,
                          jnp.maximum(att(t) % tiles_per_seq * per_tile - 1, 0), 0)

    row = lambda t: (jnp.maximum(t - 1, 0), 0)
    fixed = lambda t: (0, 0)
    smem = pl.BlockSpec(memory_space=pltpu.SMEM)
    vec = lambda a: pl.BlockSpec((1, a.shape[-1]), fixed)
    weight = lambda a: pl.BlockSpec(a.shape, fixed, pipeline_mode=pl.Buffered(1))
    tile3 = lambda a: pl.BlockSpec((1, TAIL_ROWS, a.shape[-1]), cur)
    before = lambda a, rows: pl.BlockSpec((1, rows, a.shape[-1]), prev(rows))
    steps = last_tile + 2
    return pl.pallas_call(
        functools.partial(_fast_kernel, tiles_per_seq, last_tile),
        grid=(steps,),
        in_specs=[
            smem, smem,
            tile3(q_sw), tile3(kv_sw), before(kv_sw, ATT_BLOCK),
            tile3(q_sb), tile3(k_sb), before(k_sb, SB_WINDOW), tile3(v_sb), before(v_sb, SB_WINDOW),
            pl.BlockSpec((TAIL_ROWS, d), row),
            vec(sb_g), vec(sw_g), weight(w_out_b), vec(ln1_g), vec(ln1_b),
            weight(w_gate_up_b), weight(w_down_b), vec(ln2_g), vec(ln2_b),
        ],
        out_specs=[pl.BlockSpec((TAIL_ROWS, d), row),
                   pl.BlockSpec((1, 8, LANES), lambda t: (t, 0, 0))],
        out_shape=[jax.ShapeDtypeStruct((n, d), F32),
                   jax.ShapeDtypeStruct((steps, 8, LANES), F32)],
        scratch_shapes=[
            pltpu.VMEM((SWA_HEADS, ATT_BLOCK, 2 * ATT_BLOCK), F32),
            pltpu.VMEM((2, TAIL_ROWS, wd), BF16),
            pltpu.VMEM((2, TAIL_ROWS, q_sb.shape[-1]), BF16),
            pltpu.VMEM((SB_WINDOW, SB_WINDOW), BF16),
            pltpu.VMEM((SB_WINDOW // SB_Q_TILE, 2, 2 * SB_Q_TILE, SB_WINDOW), F32),
        ],
        compiler_params=pltpu.CompilerParams(
            dimension_semantics=("arbitrary",), vmem_limit_bytes=VMEM_LIMIT),
        name="attention_and_dense_tail",
    )(sinks, rel_bias, q_sw, kv_sw, kv_sw, q_sb, k_sb, k_sb, v_sb, v_sb, ha, sb_g, sw_g, w_out_b,
      ln1_g, ln1_b, w_gate_up_b, w_down_b, ln2_g, ln2_b)


def kernel(x, ln_in_g, ln_in_b, w_in, sb_norm_g, swa_norm_g, sinks, rel_bias, w_out,
           ln1_g, ln1_b, w_gate_up, w_down, ln2_g, ln2_b):
    b, s, d = x.shape
    assert w_in.shape[0] == DEPTH == 1
    sb_w = SB_HEADS * HEAD_DIM
    sw_w = SWA_HEADS * HEAD_DIM
    kv_w = SWA_KV_HEADS * HEAD_DIM
    assert w_in.shape[-1] == 3 * sb_w + sw_w + 2 * kv_w
    assert kv_w == LANES and SWA_HEADS // SWA_KV_HEADS == 4
    assert s % SB_WINDOW == 0 and s % SB_STEP_ROWS == 0 and s % TAIL_ROWS == 0
    assert (b * s) % ROW_TILE == 0 and WINDOW <= ATT_BLOCK and d % MXU_WIDTH == 0
    assert FFN_CHUNK % MXU_WIDTH == 0 and w_down.shape[1] % MXU_WIDTH == 0

    x2 = x.reshape(b * s, d)
    row_vec = lambda a: a.reshape(1, -1)

    names = ("q_sb", "k_sb", "v_sb", "q_sw", "k_sw", "v_sw")
    widths = (sb_w, sb_w, sb_w, sw_w, kv_w, kv_w)
    splits, lo = {}, 0
    for name, wd in zip(names, widths):
        splits[name] = (lo, lo + wd)
        lo += wd
    ha, q_sb, k_sb, v_sb, q_sw, kv_sw, w_out_b, w_gate_up_b, w_down_b = _in_projection(
        x2, row_vec(ln_in_g), row_vec(ln_in_b), w_in[0], splits,
        [w_out[0], w_gate_up[0], w_down[0]])

    to3 = lambda a: a.reshape(b, s, a.shape[-1])
    q_sb, k_sb, v_sb, q_sw, kv_sw = map(to3, (q_sb, k_sb, v_sb, q_sw, kv_sw))
    tail_args = (row_vec(sb_norm_g[0]), row_vec(swa_norm_g[0]), w_out_b,
                 row_vec(ln1_g[0]), row_vec(ln1_b[0]), w_gate_up_b, w_down_b,
                 row_vec(ln2_g[0]), row_vec(ln2_b[0]))

    fast, live = _fast_block(q_sw, kv_sw, q_sb, k_sb, v_sb, sinks[0], rel_bias, ha, *tail_args)
    n_tiles = (b * s) // TAIL_ROWS

    def general(_):
        (sb_out,) = _sb_attention(q_sb, k_sb, v_sb, [])
        return _attn_tail_block(q_sw, kv_sw, sinks[0], rel_bias, ha,
                                sb_out.reshape(b * s, sb_w), *tail_args)

    out = lax.cond(jnp.any(live[:n_tiles, 0, 0] < SB_DEAD_LOG2), general, lambda _: fast, None)
    return out.reshape(b, s, d)
```

```python
import functools
import math

import jax
import jax.numpy as jnp
import numpy as np
from jax import lax
from jax.experimental import pallas as pl
from jax.experimental.pallas import tpu as pltpu

F32 = jnp.float32
BF16 = jnp.bfloat16

HEAD_DIM = 64
SB_HEADS = 8
SWA_HEADS = 8
SWA_KV_HEADS = 2
WINDOW = 128
REL_BUCKETS = 32
REL_MAX_DIST = 128
LN_EPS = 1e-5
RMS_EPS = 1e-6
DEPTH = 1
ALPHA = (2 * DEPTH) ** 0.25
QK_SCALE = HEAD_DIM ** -0.5
LOG2E = math.log2(math.e)

LANES = 128
BF16_ROWS = 16
MXU_WIDTH = 256
ATT_BLOCK = 128
SB_WINDOW = 256
SB_Q_TILE = 64
SB_STEP_ROWS = 512
ROW_TILE = 1024
PROJ_PARTS = 2
TAIL_ROWS = 512
FFN_CHUNK = 4 * MXU_WIDTH
VMEM_LIMIT = 56 * 1024 * 1024

SB_DEAD_LOG2 = 127.0
MASK_NEG = -1e30


def _layer_norm(x, g, b):
    mu = jnp.mean(x, axis=-1, keepdims=True)
    xc = x - mu
    var = jnp.mean(xc * xc, axis=-1, keepdims=True)
    return xc * lax.rsqrt(var + LN_EPS) * g + b


def _rms_norm(x, g):
    ms = jnp.mean(x * x, axis=-1, keepdims=True)
    return x * lax.rsqrt(ms + RMS_EPS) * g


def _dot(a, b):
    return jnp.dot(a, b, preferred_element_type=F32)


def _dot_nt(a, b):
    return lax.dot_general(a, b, (((1,), (1,)), ((), ())), preferred_element_type=F32)


def _pipeline_schedule(n, stages):
    vals = [dict() for _ in stages]
    thunks = []
    for step in range(n + len(stages) - 1):
        for s, stage in enumerate(stages):
            i = step - s
            if 0 <= i < n:
                def thunk(s=s, i=i, stage=stage):
                    vals[s][i] = stage(i, vals[s - 1].pop(i)) if s else stage(i)
                thunks.append(thunk)
    return thunks, vals[-1]


def _software_pipeline(n, stages):
    thunks, results = _pipeline_schedule(n, stages)
    for thunk in thunks:
        thunk()
    return results


def _interleave(xs, ys):
    merged, taken = [], 0
    for i, x in enumerate(xs):
        merged.append(x)
        upto = (i + 1) * len(ys) // len(xs)
        merged.extend(ys[taken:upto])
        taken = upto
    return merged


def _inproj_kernel(splits, n_later, x_ref, g_ref, b_ref, w32_ref, *refs):
    later_in, refs = refs[:n_later], refs[n_later:]
    ha_ref, qsb_ref, ksb_ref, vsb_ref, qsw_ref, kv_ref = refs[:6]
    later_out, w_ref = refs[6:6 + n_later], refs[6 + n_later]
    rows = x_ref.shape[0] // PROJ_PARTS
    part = lambda i: slice(i * rows, (i + 1) * rows)
    assert splits["k_sw"][1] == splits["v_sw"][0]

    @pl.when(pl.program_id(0) == 0)
    def _convert_weight():
        w_ref[...] = w32_ref[...].astype(BF16)

    for src, dst in zip(later_in, later_out):
        dst[...] = src[...].astype(BF16)

    def normed(i):
        h32 = _layer_norm(x_ref[part(i), :], g_ref[...], b_ref[...])
        ha_ref[part(i), :] = ALPHA * h32
        return h32.astype(BF16)

    def projected(i, h):
        r = part(i)
        proj = lambda lo, hi: _dot(h, w_ref[:, lo:hi])
        qsb_ref[r, :] = (proj(*splits["q_sb"]) * QK_SCALE).astype(BF16)
        ksb_ref[r, :] = proj(*splits["k_sb"]).astype(BF16)
        vsb_ref[r, :] = proj(*splits["v_sb"]).astype(BF16)
        qsw_ref[r, :] = (proj(*splits["q_sw"]) * QK_SCALE).astype(BF16)
        kv = proj(splits["k_sw"][0], splits["v_sw"][1])
        kvw = kv.shape[-1] // 2
        k, v = kv[:, :kvw], kv[:, kvw:]
        kv_ref[r, 0 * kvw:1 * kvw] = k.astype(BF16)
        kv_ref[r, 1 * kvw:2 * kvw] = pltpu.roll(k, HEAD_DIM, axis=1).astype(BF16)
        kv_ref[r, 2 * kvw:3 * kvw] = v.astype(BF16)
        kv_ref[r, 3 * kvw:4 * kvw] = pltpu.roll(v, HEAD_DIM, axis=1).astype(BF16)

    _software_pipeline(PROJ_PARTS, [normed, projected])


def _in_projection(x2, g, b, w_in, splits, later_weights):
    n, d = x2.shape
    steps = n // ROW_TILE
    width = lambda name: splits[name][1] - splits[name][0]
    out_w = [width("q_sb"), width("k_sb"), width("v_sb"), width("q_sw"), 4 * width("k_sw")]
    row = lambda i: (i, 0)
    fixed = lambda i: (0, 0)
    later_specs = [
        pl.BlockSpec((size, w.shape[1]),
                     lambda i, last=w.shape[0] // size - 1: (jnp.minimum(i, last), 0))
        for w, size in zip(later_weights, _conversion_blocks(later_weights, steps))]
    return pl.pallas_call(
        functools.partial(_inproj_kernel, splits, len(later_weights)),
        grid=(steps,),
        in_specs=[
            pl.BlockSpec((ROW_TILE, d), row),
            pl.BlockSpec((1, d), fixed),
            pl.BlockSpec((1, d), fixed),
            pl.BlockSpec(w_in.shape, fixed, pipeline_mode=pl.Buffered(1)),
        ] + later_specs,
        out_specs=[pl.BlockSpec((ROW_TILE, d), row)]
        + [pl.BlockSpec((ROW_TILE, w), row) for w in out_w] + later_specs,
        out_shape=[jax.ShapeDtypeStruct((n, d), F32)]
        + [jax.ShapeDtypeStruct((n, w), BF16) for w in out_w]
        + [jax.ShapeDtypeStruct(w.shape, BF16) for w in later_weights],
        scratch_shapes=[pltpu.VMEM(w_in.shape, BF16)],
        compiler_params=pltpu.CompilerParams(
            dimension_semantics=("arbitrary",), vmem_limit_bytes=VMEM_LIMIT),
        name="ln_in_proj",
    )(x2, g, b, w_in, *later_weights)


def _conversion_blocks(weights, steps):
    sizes = []
    for w in weights:
        rows = w.shape[0]
        sizes.append(next(r for r in range(BF16_ROWS, rows + 1, BF16_ROWS)
                          if rows % r == 0 and rows // r <= steps))
    return sizes


def _split_heads(x, lane):
    zero = jnp.zeros_like(x)
    return jnp.concatenate([jnp.where(lane < HEAD_DIM, x, zero),
                            jnp.where(lane >= HEAD_DIM, x, zero)], axis=0)


def _merge_heads(o2, lane):
    t = o2.shape[0] // 2
    return jnp.where(lane < HEAD_DIM, o2[:t], o2[t:])


def _sb_masks(valid):
    return jnp.where(valid, LOG2E, 0.0), jnp.where(valid, 0.0, MASK_NEG)


def _sb_scores(qq, k, keep):
    z = _dot_nt(qq, k)
    zl = z * LOG2E
    c_nat = jnp.maximum(z, 0.0) + jnp.log(1.0 + jnp.exp2(-jnp.abs(zl)))
    cm = c_nat * keep
    return zl - cm, cm.astype(BF16), cm[:, :LANES]


def _sb_weights(scores, tri, kills, carrieds):
    sums = _dot(jnp.concatenate([sc[1] for sc in scores], axis=0), tri)
    rows = scores[0][0].shape[0]
    out = []
    for n, ((log_beta, _, cm0), kill, carried) in enumerate(zip(scores, kills, carrieds)):
        later = sums[n * rows:(n + 1) * rows]
        log_a = (log_beta - later) + kill
        if carried is not None:
            log_a = log_a - jnp.concatenate([carried] * (log_a.shape[1] // LANES), axis=1)
        out.append((jnp.exp2(log_a).astype(BF16), later[:, :LANES] + cm0))
    return out


def _sb_kernel(n_conv, q_ref, k_ref, v_ref, *refs):
    conv_in, o_ref, conv_out = refs[:n_conv], refs[n_conv], refs[n_conv + 1:2 * n_conv + 1]
    tri_ref, mask_ref, acc_ref, car_ref = refs[2 * n_conv + 1:]
    for src, dst in zip(conv_in, conv_out):
        dst[...] = src[...].astype(BF16)

    bi = pl.program_id(0)
    qi = pl.program_id(1)
    tq, w = SB_Q_TILE, SB_WINDOW
    subs = q_ref.shape[1] // tq
    pairs = q_ref.shape[-1] // LANES
    row0 = qi * (subs * tq)

    @pl.when(jnp.logical_and(bi == 0, qi == 0))
    def _fill_tri():
        j = lax.broadcasted_iota(jnp.int32, (w, w), 0)
        s = lax.broadcasted_iota(jnp.int32, (w, w), 1)
        tri_ref[...] = jnp.where(j > s, 1.0, 0.0).astype(BF16)

    lane = lax.broadcasted_iota(jnp.int32, (tq, LANES), 1)
    lanes_of = lambda p: slice(p * LANES, (p + 1) * LANES)
    rows_of = lambda h: slice(h * tq, (h + 1) * tq)
    first_start = lambda h: pl.multiple_of(jnp.maximum(row0 + (h + 1) * tq - w, 0), tq)

    def key_positions(start):
        return start + lax.broadcasted_iota(jnp.int32, (2 * tq, w), 1)

    assert subs * tq >= w - tq

    @pl.when(qi <= 1)
    def _fill_masks():
        row = lax.broadcasted_iota(jnp.int32, (2 * tq, w), 0) & (tq - 1)
        for h in range(subs):
            mask_ref[h, 0], mask_ref[h, 1] = _sb_masks(
                key_positions(first_start(h)) < row0 + h * tq + row)

    def scores(p, starts, keeps):
        return [_sb_scores(_split_heads(q_ref[0, rows_of(h), lanes_of(p)], lane),
                           k_ref[0, pl.ds(starts[h], w), lanes_of(p)], keeps[h])
                for h in range(subs)]

    first_starts = [first_start(h) for h in range(subs)]

    def first_scores(p):
        return scores(p, first_starts, [mask_ref[h, 0] for h in range(subs)])

    def first_weights(p, sc):
        return _sb_weights(sc, tri_ref[...], [mask_ref[h, 1] for h in range(subs)], [None] * subs)

    def first_values(p, aws):
        least = None
        for h, (a, total) in enumerate(aws):
            o2 = _dot(a, v_ref[0, pl.ds(first_starts[h], w), lanes_of(p)])
            o_ref[0, rows_of(h), lanes_of(p)] = _merge_heads(o2, lane).astype(o_ref.dtype)
            acc_ref[p, h] = o2
            car_ref[p, h] = total
            m = jnp.min(total[:, 0:1])
            least = m if least is None else jnp.minimum(least, m)
        return least

    first = _software_pipeline(pairs, [first_scores, first_weights, first_values])
    live = functools.reduce(jnp.minimum, [first[p] for p in range(pairs)])
    last_start = first_starts[-1]

    @pl.when(jnp.logical_and(last_start > 0, live < SB_DEAD_LOG2))
    def _walk_back():
        for p in range(pairs):
            for h in range(subs):
                car_ref[p, h] = jnp.broadcast_to(car_ref[p, h][:, 0:1], (2 * tq, LANES))

        def body(state):
            step, _ = state
            done = [jnp.maximum(first_start(h) - step * w, 0) for h in range(subs)]
            starts = [pl.multiple_of(jnp.maximum(d - w, 0), tq) for d in done]
            masks = [_sb_masks(key_positions(s) < d) for s, d in zip(starts, done)]

            def more_scores(p):
                return scores(p, starts, [m[0] for m in masks])

            def weights(p, sc):
                return _sb_weights(sc, tri_ref[...], [m[1] for m in masks],
                                   [car_ref[p, h] for h in range(subs)])

            def values(p, aws):
                least = None
                for h, (a, total) in enumerate(aws):
                    acc_ref[p, h] += _dot(a, v_ref[0, pl.ds(starts[h], w), lanes_of(p)])
                    carried = car_ref[p, h] + total[:, 0:1]
                    car_ref[p, h] = carried
                    m = jnp.min(carried)
                    least = m if least is None else jnp.minimum(least, m)
                return least

            mins = _software_pipeline(pairs, [more_scores, weights, values])
            return step + 1, functools.reduce(jnp.minimum, [mins[p] for p in range(pairs)])

        def cond(state):
            step, nxt = state
            return jnp.logical_and(last_start - step * w > 0, nxt < SB_DEAD_LOG2)

        lax.while_loop(cond, body, (jnp.int32(0), live))
        for p in range(pairs):
            for h in range(subs):
                o_ref[0, rows_of(h), lanes_of(p)] = _merge_heads(
                    acc_ref[p, h], lane).astype(o_ref.dtype)


def _sb_attention(q, k, v, weights):
    b, s, wd = q.shape
    pairs = wd // LANES
    subs = SB_STEP_ROWS // SB_Q_TILE
    nq = s // SB_STEP_ROWS
    tile = lambda bi, i: (bi, i, 0)
    whole = lambda bi, i: (bi, 0, 0)
    once = pl.Buffered(1)
    conv_specs = [
        pl.BlockSpec((size, w.shape[1]),
                     lambda bi, i, last=w.shape[0] // size - 1: (jnp.minimum(bi * nq + i, last), 0))
        for w, size in zip(weights, _conversion_blocks(weights, b * nq))]
    return pl.pallas_call(
        functools.partial(_sb_kernel, len(weights)),
        grid=(b, nq),
        in_specs=[
            pl.BlockSpec((1, SB_STEP_ROWS, wd), tile),
            pl.BlockSpec((1, s, wd), whole, pipeline_mode=once),
            pl.BlockSpec((1, s, wd), whole, pipeline_mode=once),
        ] + conv_specs,
        out_specs=[pl.BlockSpec((1, SB_STEP_ROWS, wd), tile)] + conv_specs,
        out_shape=[jax.ShapeDtypeStruct((b, s, wd), BF16)]
        + [jax.ShapeDtypeStruct(w.shape, BF16) for w in weights],
        scratch_shapes=[
            pltpu.VMEM((SB_WINDOW, SB_WINDOW), BF16),
            pltpu.VMEM((subs, 2, 2 * SB_Q_TILE, SB_WINDOW), F32),
            pltpu.VMEM((pairs, subs, 2 * SB_Q_TILE, LANES), F32),
            pltpu.VMEM((pairs, subs, 2 * SB_Q_TILE, LANES), F32),
        ],
        compiler_params=pltpu.CompilerParams(
            dimension_semantics=("arbitrary", "arbitrary"), vmem_limit_bytes=VMEM_LIMIT),
        name="stick_breaking_attention",
    )(q, k, v, *weights)


def _t5_log_bucket_starts():
    exact = REL_BUCKETS // 2
    d = np.arange(exact, 2 * REL_MAX_DIST, dtype=np.float32)
    large = exact + (np.log(d / np.float32(exact)) / np.float32(math.log(REL_MAX_DIST / exact))
                     * np.float32(REL_BUCKETS - exact)).astype(np.int32)
    large = np.minimum(large, REL_BUCKETS - 1)
    assert large[0] == exact and np.all(np.diff(large) >= 0)
    return [int(d[np.argmax(large >= bk)]) for bk in range(exact + 1, REL_BUCKETS)]


def _t5_causal_bucket(distance):
    d = jnp.maximum(distance, 0)
    bucket = jnp.minimum(d, REL_BUCKETS // 2)
    for start in _t5_log_bucket_starts():
        bucket = bucket + jnp.where(d >= start, 1, 0)
    return bucket


def _swa_fill_bias(relb_ref, bias_ref):
    t = ATT_BLOCK
    r = lax.broadcasted_iota(jnp.int32, (t, 2 * t), 0)
    c = lax.broadcasted_iota(jnp.int32, (t, 2 * t), 1)
    dist = r + t - c
    bucket = _t5_causal_bucket(dist)
    valid = jnp.logical_and(dist >= 0, dist < WINDOW)
    for h in range(SWA_HEADS):
        bias = jnp.zeros((t, 2 * t), F32)
        for bk in range(REL_BUCKETS):
            bias = jnp.where(bucket == bk, relb_ref[bk, h], bias)
        bias_ref[h] = jnp.where(valid, bias, MASK_NEG)


def _swa_schedule(sinks_ref, q_ref, kvc_ref, kvp_ref, bias_ref, o_ref, no_previous):
    t = ATT_BLOCK
    kvw = kvc_ref.shape[-1] // 4
    group = SWA_HEADS // SWA_KV_HEADS
    tiles = q_ref.shape[0] // t
    rows_of = lambda n: slice(n * t, (n + 1) * t)
    lane = lax.broadcasted_iota(jnp.int32, (t, LANES), 1)
    row2 = lax.broadcasted_iota(jnp.int32, (2 * t, 1), 0)
    col2 = lax.broadcasted_iota(jnp.int32, (2 * t, 2 * t), 1)
    prev_pen = jnp.where(jnp.logical_and(col2 < t, no_previous), MASK_NEG, 0.0)

    def kv_window(n, variant):
        cols = slice(variant * kvw, (variant + 1) * kvw)
        before = kvc_ref[rows_of(n - 1), cols] if n else kvp_ref[:, cols]
        return jnp.concatenate([before, kvc_ref[rows_of(n), cols]], axis=0)

    items = [(n, half, j) for n in range(tiles) for half in range(2) for j in range(SWA_KV_HEADS)]
    heads_of = lambda half, j: [h for h in range(j * group, (j + 1) * group) if h % 2 == half]

    def probabilities(i):
        n, half, j = items[i]
        heads = heads_of(half, j)
        in_half = (lane >= HEAD_DIM) if half else (lane < HEAD_DIM)
        qq = jnp.concatenate(
            [jnp.where(in_half, q_ref[rows_of(n), (h // 2) * LANES:(h // 2 + 1) * LANES], 0)
             for h in heads], axis=0)
        s = _dot_nt(qq, kv_window(n, 0 + int(j != half)))
        s = s + jnp.concatenate([bias_ref[h] for h in heads], axis=0)
        if n == 0:
            s = s + prev_pen
        sink = jnp.where(row2 < t, sinks_ref[heads[0]], sinks_ref[heads[1]])
        m = jnp.maximum(jnp.max(s, axis=-1, keepdims=True), sink)
        p = jnp.exp(s - m)
        denom = jnp.sum(p, axis=-1, keepdims=True) + jnp.exp(sink - m)
        return p.astype(BF16), 1.0 / denom

    def values(i, pd):
        n, half, j = items[i]
        p, inv_denom = pd
        return _dot(p, kv_window(n, 2 + int(j != half))) * inv_denom

    thunks, outs = _pipeline_schedule(len(items), [probabilities, values])

    def store():
        for i, (n, half, j) in enumerate(items):
            if half == 0:
                even, odd = outs[i], outs[items.index((n, 1, j))]
                for e, h in enumerate(heads_of(0, j)):
                    o_ref[rows_of(n), (h // 2) * LANES:(h // 2 + 1) * LANES] = jnp.where(
                        lane < HEAD_DIM, even[e * t:(e + 1) * t], odd[e * t:(e + 1) * t]
                    ).astype(o_ref.dtype)

    return thunks + [store]


def _tail_schedule(ha_ref, sb_ref, sw_ref, sbg_ref, swg_ref, wo_ref, l1g_ref, l1b_ref,
                   wgu_ref, wd_ref, l2g_ref, l2b_ref, o_ref):
    d = ha_ref.shape[1]
    d_ff = wd_ref.shape[0]
    mw = MXU_WIDTH
    bounds = list(range(0, d_ff, FFN_CHUNK)) + [d_ff]
    n_chunks = len(bounds) - 1
    out_tiles = range(d // mw)
    st = {}

    def merged():
        sb = _rms_norm(sb_ref[...].astype(F32), sbg_ref[...]).astype(BF16)
        sw = _rms_norm(sw_ref[...].astype(F32), swg_ref[...]).astype(BF16)
        st["merged"] = jnp.concatenate([sb, sw], axis=1)

    def out_projection(n):
        st["mix", n] = _dot(st["merged"], wo_ref[:, n * mw:(n + 1) * mw])

    def first_norm():
        mix = jnp.concatenate([st.pop(("mix", n)) for n in out_tiles], axis=1)
        st.pop("merged")
        h1 = _layer_norm(ha_ref[...] + mix, l1g_ref[...], l1b_ref[...])
        st["h1b"] = h1.astype(BF16)
        for n in out_tiles:
            st["acc", n] = ALPHA * h1[:, n * mw:(n + 1) * mw]

    def activation(c, n):
        lo = bounds[c] + n * mw
        gate = _dot(st["h1b"], wgu_ref[:, lo:lo + mw])
        up = _dot(st["h1b"], wgu_ref[:, d_ff + lo:d_ff + lo + mw])
        st["act", c, n] = (gate / (1.0 + jnp.exp(-gate)) * up).astype(BF16)

    def down(c, n):
        tiles = range((bounds[c + 1] - bounds[c]) // mw)
        if n == 0:
            st["act", c] = jnp.concatenate([st.pop(("act", c, k)) for k in tiles], axis=1)
        st["acc", n] = st["acc", n] + _dot(st["act", c], wd_ref[bounds[c]:bounds[c + 1],
                                                               n * mw:(n + 1) * mw])

    def second_norm():
        y = jnp.concatenate([st.pop(("acc", n)) for n in out_tiles], axis=1)
        o_ref[...] = _layer_norm(y, l2g_ref[...], l2b_ref[...])

    act_thunks = [[functools.partial(activation, c, n)
                   for n in range((bounds[c + 1] - bounds[c]) // mw)] for c in range(n_chunks)]
    down_thunks = [[functools.partial(down, c, n) for n in out_tiles] for c in range(n_chunks)]
    ffn = list(act_thunks[0])
    for c in range(1, n_chunks):
        ffn += _interleave(act_thunks[c], down_thunks[c - 1])
    ffn += down_thunks[-1]
    return (merged, [functools.partial(out_projection, n) for n in out_tiles], first_norm,
            ffn, second_norm)


def _attn_tail_kernel(tiles_per_seq, last_tile, sinks_ref, relb_ref, q_ref, kvc_ref, kvp_ref,
                      ha_ref, sb_ref, sbg_ref, swg_ref, wo_ref, l1g_ref, l1b_ref,
                      wgu_ref, wd_ref, l2g_ref, l2b_ref, o_ref, bias_ref, sw_ref):
    t = pl.program_id(0)

    @pl.when(t == 0)
    def _first_step():
        _swa_fill_bias(relb_ref, bias_ref)
        sw_ref[1] = jnp.zeros(sw_ref.shape[1:], sw_ref.dtype)

    no_previous = (jnp.minimum(t, last_tile) % tiles_per_seq) == 0
    attention = _swa_schedule(sinks_ref, q_ref.at[0], kvc_ref.at[0], kvp_ref.at[0], bias_ref,
                              sw_ref.at[t % 2], no_previous)
    merged, out_proj, first_norm, ffn, second_norm = _tail_schedule(
        ha_ref, sb_ref, sw_ref.at[(t + 1) % 2], sbg_ref, swg_ref, wo_ref,
        l1g_ref, l1b_ref, wgu_ref, wd_ref, l2g_ref, l2b_ref, o_ref)
    tail = [merged] + out_proj + [first_norm] + ffn + [second_norm]
    for thunk in _interleave(tail, attention):
        thunk()


def _attn_tail_block(q_sw, kv_sw, sinks, rel_bias, ha, sb2, sb_g, sw_g, w_out_b, ln1_g, ln1_b,
                     w_gate_up_b, w_down_b, ln2_g, ln2_b):
    b, s, wd = q_sw.shape
    n, d = ha.shape
    tiles_per_seq = s // TAIL_ROWS
    last_tile = n // TAIL_ROWS - 1
    key_tiles = TAIL_ROWS // ATT_BLOCK
    att = lambda t: jnp.minimum(t, last_tile)
    cur = lambda t: (att(t) // tiles_per_seq, att(t) % tiles_per_seq, 0)
    prev = lambda t: (att(t) // tiles_per_seq,
                      jnp.maximum(att(t) % tiles_per_seq * key_tiles - 1, 0), 0)
    row = lambda t: (jnp.maximum(t - 1, 0), 0)
    fixed = lambda t: (0, 0)
    smem = pl.BlockSpec(memory_space=pltpu.SMEM)
    vec = lambda a: pl.BlockSpec((1, a.shape[-1]), fixed)
    weight = lambda a: pl.BlockSpec(a.shape, fixed, pipeline_mode=pl.Buffered(1))
    return pl.pallas_call(
        functools.partial(_attn_tail_kernel, tiles_per_seq, last_tile),
        grid=(last_tile + 2,),
        in_specs=[
            smem, smem,
            pl.BlockSpec((1, TAIL_ROWS, wd), cur),
            pl.BlockSpec((1, TAIL_ROWS, kv_sw.shape[-1]), cur),
            pl.BlockSpec((1, ATT_BLOCK, kv_sw.shape[-1]), prev),
            pl.BlockSpec((TAIL_ROWS, d), row),
            pl.BlockSpec((TAIL_ROWS, sb2.shape[-1]), row),
            vec(sb_g), vec(sw_g), weight(w_out_b), vec(ln1_g), vec(ln1_b),
            weight(w_gate_up_b), weight(w_down_b), vec(ln2_g), vec(ln2_b),
        ],
        out_specs=pl.BlockSpec((TAIL_ROWS, d), row),
        out_shape=jax.ShapeDtypeStruct((n, d), F32),
        scratch_shapes=[
            pltpu.VMEM((SWA_HEADS, ATT_BLOCK, 2 * ATT_BLOCK), F32),
            pltpu.VMEM((2, TAIL_ROWS, wd), BF16),
        ],
        compiler_params=pltpu.CompilerParams(
            dimension_semantics=("arbitrary",), vmem_limit_bytes=VMEM_LIMIT),
        name="swa_attention_and_dense_tail",
    )(sinks, rel_bias, q_sw, kv_sw, kv_sw, ha, sb2, sb_g, sw_g, w_out_b, ln1_g, ln1_b,
      w_gate_up_b, w_down_b, ln2_g, ln2_b)


SB_GROUP = 2
SB_LIVE_DONE = 1e30


def _sb_fast_masks(mask_ref):
    tq, w = SB_Q_TILE, SB_WINDOW
    row = lax.broadcasted_iota(jnp.int32, (2 * tq, w), 0) & (tq - 1)
    col = lax.broadcasted_iota(jnp.int32, (2 * tq, w), 1)
    causal = col < row + (w - tq)
    mask_ref[0, 0], mask_ref[0, 1] = _sb_masks(causal)
    for h in range(mask_ref.shape[0] - 1):
        first_key = w - (h + 1) * tq
        mask_ref[1 + h, 0], mask_ref[1 + h, 1] = _sb_masks(
            jnp.logical_and(causal, col >= first_key))


def _sb_fast_schedule(q_ref, kc_ref, kp_ref, vc_ref, vp_ref, tri_ref, mask_ref, o_ref,
                      no_previous, lives):
    tq, w = SB_Q_TILE, SB_WINDOW
    subs = q_ref.shape[0] // tq
    pairs = q_ref.shape[-1] // LANES
    clamped = w // tq - 1
    assert mask_ref.shape[0] == clamped + 1 and subs % SB_GROUP == 0
    lane = lax.broadcasted_iota(jnp.int32, (tq, LANES), 1)
    lanes_of = lambda p: slice(p * LANES, (p + 1) * LANES)
    rows_of = lambda h: slice(h * tq, (h + 1) * tq)

    def window(h, cur_ref, prev_ref, p):
        start = (h + 1) * tq - w
        if start >= 0:
            return cur_ref[start:start + w, lanes_of(p)]
        return jnp.concatenate([prev_ref[w + start:w, lanes_of(p)],
                                cur_ref[0:w + start, lanes_of(p)]], axis=0)

    def mask(h, which):
        if h >= clamped:
            return mask_ref[0, which]
        return mask_ref[jnp.where(no_previous, 1 + h, 0), which]

    items = [(p, g) for p in range(pairs) for g in range(subs // SB_GROUP)]
    group = lambda g: range(g * SB_GROUP, (g + 1) * SB_GROUP)

    def scores(i):
        p, g = items[i]
        return [_sb_scores(_split_heads(q_ref[rows_of(h), lanes_of(p)], lane),
                           window(h, kc_ref, kp_ref, p), mask(h, 0)) for h in group(g)]

    def weights(i, sc):
        return _sb_weights(sc, tri_ref[...], [mask(h, 1) for h in group(items[i][1])],
                           [None] * SB_GROUP)

    def values(i, aws):
        p, g = items[i]
        for h, (a, total) in zip(group(g), aws):
            o2 = _dot(a, window(h, vc_ref, vp_ref, p))
            o_ref[rows_of(h), lanes_of(p)] = _merge_heads(o2, lane).astype(o_ref.dtype)
            m = jnp.min(total[:, 0:1])
            lives.append(m if h > clamped else jnp.where(no_previous, SB_LIVE_DONE, m))

    return _pipeline_schedule(len(items), [scores, weights, values])[0]


def _fast_kernel(tiles_per_seq, last_tile, sinks_ref, relb_ref, qsw_ref, kvc_ref, kvp_ref,
                 qsb_ref, kc_ref, kp_ref, vc_ref, vp_ref,
                 ha_ref, sbg_ref, swg_ref, wo_ref, l1g_ref, l1b_ref,
                 wgu_ref, wd_ref, l2g_ref, l2b_ref, o_ref, live_ref,
                 bias_ref, sw_ref, sb_ref, tri_ref, mask_ref):
    t = pl.program_id(0)

    @pl.when(t == 0)
    def _first_step():
        _swa_fill_bias(relb_ref, bias_ref)
        _sb_fast_masks(mask_ref)
        j = lax.broadcasted_iota(jnp.int32, tri_ref.shape, 0)
        s = lax.broadcasted_iota(jnp.int32, tri_ref.shape, 1)
        tri_ref[...] = jnp.where(j > s, 1.0, 0.0).astype(BF16)
        sw_ref[1] = jnp.zeros(sw_ref.shape[1:], sw_ref.dtype)
        sb_ref[1] = jnp.zeros(sb_ref.shape[1:], sb_ref.dtype)

    no_previous = (jnp.minimum(t, last_tile) % tiles_per_seq) == 0
    lives = []
    swa = _swa_schedule(sinks_ref, qsw_ref.at[0], kvc_ref.at[0], kvp_ref.at[0], bias_ref,
                        sw_ref.at[t % 2], no_previous)
    sb = _sb_fast_schedule(qsb_ref.at[0], kc_ref.at[0], kp_ref.at[0], vc_ref.at[0], vp_ref.at[0],
                           tri_ref, mask_ref, sb_ref.at[t % 2], no_previous, lives)
    merged, out_proj, first_norm, ffn, second_norm = _tail_schedule(
        ha_ref, sb_ref.at[(t + 1) % 2], sw_ref.at[(t + 1) % 2], sbg_ref, swg_ref, wo_ref,
        l1g_ref, l1b_ref, wgu_ref, wd_ref, l2g_ref, l2b_ref, o_ref)
    tail = [merged] + out_proj + [first_norm] + ffn + [second_norm]
    for thunk in _interleave(tail, _interleave(swa, sb)):
        thunk()
    live_ref[...] = jnp.full(live_ref.shape, functools.reduce(jnp.minimum, lives), F32)


def _fast_block(q_sw, kv_sw, q_sb, k_sb, v_sb, sinks, rel_bias, ha, sb_g, sw_g, w_out_b,
                ln1_g, ln1_b, w_gate_up_b, w_down_b, ln2_g, ln2_b):
    b, s, wd = q_sw.shape
    n, d = ha.shape
    tiles_per_seq = s // TAIL_ROWS
    last_tile = n // TAIL_ROWS - 1
    att = lambda t: jnp.minimum(t, last_tile)
    cur = lambda t: (att(t) // tiles_per_seq, att(t) % tiles_per_seq, 0)

    def prev(rows):
        per_tile = TAIL_ROWS // rows
        return lambda t: (att(t) // tiles_per_seq,
                          jnp.maximum(att(t) % tiles_per_seq * per_tile - 1, 0), 0)

    row = lambda t: (jnp.maximum(t - 1, 0), 0)
    fixed = lambda t: (0, 0)
    smem = pl.BlockSpec(memory_space=pltpu.SMEM)
    vec = lambda a: pl.BlockSpec((1, a.shape[-1]), fixed)
    weight = lambda a: pl.BlockSpec(a.shape, fixed, pipeline_mode=pl.Buffered(1))
    tile3 = lambda a: pl.BlockSpec((1, TAIL_ROWS, a.shape[-1]), cur)
    before = lambda a, rows: pl.BlockSpec((1, rows, a.shape[-1]), prev(rows))
    steps = last_tile + 2
    return pl.pallas_call(
        functools.partial(_fast_kernel, tiles_per_seq, last_tile),
        grid=(steps,),
        in_specs=[
            smem, smem,
            tile3(q_sw), tile3(kv_sw), before(kv_sw, ATT_BLOCK),
            tile3(q_sb), tile3(k_sb), before(k_sb, SB_WINDOW), tile3(v_sb), before(v_sb, SB_WINDOW),
            pl.BlockSpec((TAIL_ROWS, d), row),
            vec(sb_g), vec(sw_g), weight(w_out_b), vec(ln1_g), vec(ln1_b),
            weight(w_gate_up_b), weight(w_down_b), vec(ln2_g), vec(ln2_b),
        ],
        out_specs=[pl.BlockSpec((TAIL_ROWS, d), row),
                   pl.BlockSpec((1, 8, LANES), lambda t: (t, 0, 0))],
        out_shape=[jax.ShapeDtypeStruct((n, d), F32),
                   jax.ShapeDtypeStruct((steps, 8, LANES), F32)],
        scratch_shapes=[
            pltpu.VMEM((SWA_HEADS, ATT_BLOCK, 2 * ATT_BLOCK), F32),
            pltpu.VMEM((2, TAIL_ROWS, wd), BF16),
            pltpu.VMEM((2, TAIL_ROWS, q_sb.shape[-1]), BF16),
            pltpu.VMEM((SB_WINDOW, SB_WINDOW), BF16),
            pltpu.VMEM((SB_WINDOW // SB_Q_TILE, 2, 2 * SB_Q_TILE, SB_WINDOW), F32),
        ],
        compiler_params=pltpu.CompilerParams(
            dimension_semantics=("arbitrary",), vmem_limit_bytes=VMEM_LIMIT),
        name="attention_and_dense_tail",
    )(sinks, rel_bias, q_sw, kv_sw, kv_sw, q_sb, k_sb, k_sb, v_sb, v_sb, ha, sb_g, sw_g, w_out_b,
      ln1_g, ln1_b, w_gate_up_b, w_down_b, ln2_g, ln2_b)


def kernel(x, ln_in_g, ln_in_b, w_in, sb_norm_g, swa_norm_g, sinks, rel_bias, w_out,
           ln1_g, ln1_b, w_gate_up, w_down, ln2_g, ln2_b):
    b, s, d = x.shape
    assert w_in.shape[0] == DEPTH == 1
    sb_w = SB_HEADS * HEAD_DIM
    sw_w = SWA_HEADS * HEAD_DIM
    kv_w = SWA_KV_HEADS * HEAD_DIM
    assert w_in.shape[-1] == 3 * sb_w + sw_w + 2 * kv_w
    assert kv_w == LANES and SWA_HEADS // SWA_KV_HEADS == 4
    assert s % SB_WINDOW == 0 and s % SB_STEP_ROWS == 0 and s % TAIL_ROWS == 0
    assert (b * s) % ROW_TILE == 0 and WINDOW <= ATT_BLOCK and d % MXU_WIDTH == 0
    assert FFN_CHUNK % MXU_WIDTH == 0 and w_down.shape[1] % MXU_WIDTH == 0

    x2 = x.reshape(b * s, d)
    row_vec = lambda a: a.reshape(1, -1)

    names = ("q_sb", "k_sb", "v_sb", "q_sw", "k_sw", "v_sw")
    widths = (sb_w, sb_w, sb_w, sw_w, kv_w, kv_w)
    splits, lo = {}, 0
    for name, wd in zip(names, widths):
        splits[name] = (lo, lo + wd)
        lo += wd
    ha, q_sb, k_sb, v_sb, q_sw, kv_sw, w_out_b, w_gate_up_b, w_down_b = _in_projection(
        x2, row_vec(ln_in_g), row_vec(ln_in_b), w_in[0], splits,
        [w_out[0], w_gate_up[0], w_down[0]])

    to3 = lambda a: a.reshape(b, s, a.shape[-1])
    q_sb, k_sb, v_sb, q_sw, kv_sw = map(to3, (q_sb, k_sb, v_sb, q_sw, kv_sw))
    tail_args = (row_vec(sb_norm_g[0]), row_vec(swa_norm_g[0]), w_out_b,
                 row_vec(ln1_g[0]), row_vec(ln1_b[0]), w_gate_up_b, w_down_b,
                 row_vec(ln2_g[0]), row_vec(ln2_b[0]))

    fast, live = _fast_block(q_sw, kv_sw, q_sb, k_sb, v_sb, sinks[0], rel_bias, ha, *tail_args)
    n_tiles = (b * s) // TAIL_ROWS

    def general(_):
        (sb_out,) = _sb_attention(q_sb, k_sb, v_sb, [])
        return _attn_tail_block(q_sw, kv_sw, sinks[0], rel_bias, ha,
                                sb_out.reshape(b * s, sb_w), *tail_args)

    del general, live, n_tiles
    return fast.reshape(b, s, d)
```

```python
import functools
import math

import jax
import jax.numpy as jnp
import numpy as np
from jax import lax
from jax.experimental import pallas as pl
from jax.experimental.pallas import tpu as pltpu

F32 = jnp.float32
BF16 = jnp.bfloat16

HEAD_DIM = 64
SB_HEADS = 8
SWA_HEADS = 8
SWA_KV_HEADS = 2
WINDOW = 128
REL_BUCKETS = 32
REL_MAX_DIST = 128
LN_EPS = 1e-5
RMS_EPS = 1e-6
DEPTH = 1
ALPHA = (2 * DEPTH) ** 0.25
QK_SCALE = HEAD_DIM ** -0.5
LOG2E = math.log2(math.e)

LANES = 128
BF16_ROWS = 16
MXU_WIDTH = 256
ATT_BLOCK = 128
SB_WINDOW = 256
SB_Q_TILE = 64
SB_STEP_ROWS = 512
ROW_TILE = 1024
PROJ_PARTS = 2
TAIL_ROWS = 512
FFN_CHUNK = 4 * MXU_WIDTH
VMEM_LIMIT = 56 * 1024 * 1024

SB_DEAD_LOG2 = 1e9
MASK_NEG = -1e30


def _layer_norm(x, g, b):
    mu = jnp.mean(x, axis=-1, keepdims=True)
    xc = x - mu
    var = jnp.mean(xc * xc, axis=-1, keepdims=True)
    return xc * lax.rsqrt(var + LN_EPS) * g + b


def _rms_norm(x, g):
    ms = jnp.mean(x * x, axis=-1, keepdims=True)
    return x * lax.rsqrt(ms + RMS_EPS) * g


def _dot(a, b):
    return jnp.dot(a, b, preferred_element_type=F32)


def _dot_nt(a, b):
    return lax.dot_general(a, b, (((1,), (1,)), ((), ())), preferred_element_type=F32)


def _pipeline_schedule(n, stages):
    vals = [dict() for _ in stages]
    thunks = []
    for step in range(n + len(stages) - 1):
        for s, stage in enumerate(stages):
            i = step - s
            if 0 <= i < n:
                def thunk(s=s, i=i, stage=stage):
                    vals[s][i] = stage(i, vals[s - 1].pop(i)) if s else stage(i)
                thunks.append(thunk)
    return thunks, vals[-1]


def _software_pipeline(n, stages):
    thunks, results = _pipeline_schedule(n, stages)
    for thunk in thunks:
        thunk()
    return results


def _interleave(xs, ys):
    merged, taken = [], 0
    for i, x in enumerate(xs):
        merged.append(x)
        upto = (i + 1) * len(ys) // len(xs)
        merged.extend(ys[taken:upto])
        taken = upto
    return merged


def _inproj_kernel(splits, n_later, x_ref, g_ref, b_ref, w32_ref, *refs):
    later_in, refs = refs[:n_later], refs[n_later:]
    ha_ref, qsb_ref, ksb_ref, vsb_ref, qsw_ref, kv_ref = refs[:6]
    later_out, w_ref = refs[6:6 + n_later], refs[6 + n_later]
    rows = x_ref.shape[0] // PROJ_PARTS
    part = lambda i: slice(i * rows, (i + 1) * rows)
    assert splits["k_sw"][1] == splits["v_sw"][0]

    @pl.when(pl.program_id(0) == 0)
    def _convert_weight():
        w_ref[...] = w32_ref[...].astype(BF16)

    for src, dst in zip(later_in, later_out):
        dst[...] = src[...].astype(BF16)

    def normed(i):
        h32 = _layer_norm(x_ref[part(i), :], g_ref[...], b_ref[...])
        ha_ref[part(i), :] = ALPHA * h32
        return h32.astype(BF16)

    def projected(i, h):
        r = part(i)
        proj = lambda lo, hi: _dot(h, w_ref[:, lo:hi])
        qsb_ref[r, :] = (proj(*splits["q_sb"]) * QK_SCALE).astype(BF16)
        ksb_ref[r, :] = proj(*splits["k_sb"]).astype(BF16)
        vsb_ref[r, :] = proj(*splits["v_sb"]).astype(BF16)
        qsw_ref[r, :] = (proj(*splits["q_sw"]) * QK_SCALE).astype(BF16)
        kv = proj(splits["k_sw"][0], splits["v_sw"][1])
        kvw = kv.shape[-1] // 2
        k, v = kv[:, :kvw], kv[:, kvw:]
        kv_ref[r, 0 * kvw:1 * kvw] = k.astype(BF16)
        kv_ref[r, 1 * kvw:2 * kvw] = pltpu.roll(k, HEAD_DIM, axis=1).astype(BF16)
        kv_ref[r, 2 * kvw:3 * kvw] = v.astype(BF16)
        kv_ref[r, 3 * kvw:4 * kvw] = pltpu.roll(v, HEAD_DIM, axis=1).astype(BF16)

    _software_pipeline(PROJ_PARTS, [normed, projected])


def _in_projection(x2, g, b, w_in, splits, later_weights):
    n, d = x2.shape
    steps = n // ROW_TILE
    width = lambda name: splits[name][1] - splits[name][0]
    out_w = [width("q_sb"), width("k_sb"), width("v_sb"), width("q_sw"), 4 * width("k_sw")]
    row = lambda i: (i, 0)
    fixed = lambda i: (0, 0)
    later_specs = [
        pl.BlockSpec((size, w.shape[1]),
                     lambda i, last=w.shape[0] // size - 1: (jnp.minimum(i, last), 0))
        for w, size in zip(later_weights, _conversion_blocks(later_weights, steps))]
    return pl.pallas_call(
        functools.partial(_inproj_kernel, splits, len(later_weights)),
        grid=(steps,),
        in_specs=[
            pl.BlockSpec((ROW_TILE, d), row),
            pl.BlockSpec((1, d), fixed),
            pl.BlockSpec((1, d), fixed),
            pl.BlockSpec(w_in.shape, fixed, pipeline_mode=pl.Buffered(1)),
        ] + later_specs,
        out_specs=[pl.BlockSpec((ROW_TILE, d), row)]
        + [pl.BlockSpec((ROW_TILE, w), row) for w in out_w] + later_specs,
        out_shape=[jax.ShapeDtypeStruct((n, d), F32)]
        + [jax.ShapeDtypeStruct((n, w), BF16) for w in out_w]
        + [jax.ShapeDtypeStruct(w.shape, BF16) for w in later_weights],
        scratch_shapes=[pltpu.VMEM(w_in.shape, BF16)],
        compiler_params=pltpu.CompilerParams(
            dimension_semantics=("arbitrary",), vmem_limit_bytes=VMEM_LIMIT),
        name="ln_in_proj",
    )(x2, g, b, w_in, *later_weights)


def _conversion_blocks(weights, steps):
    sizes = []
    for w in weights:
        rows = w.shape[0]
        sizes.append(next(r for r in range(BF16_ROWS, rows + 1, BF16_ROWS)
                          if rows % r == 0 and rows // r <= steps))
    return sizes


def _split_heads(x, lane):
    zero = jnp.zeros_like(x)
    return jnp.concatenate([jnp.where(lane < HEAD_DIM, x, zero),
                            jnp.where(lane >= HEAD_DIM, x, zero)], axis=0)


def _merge_heads(o2, lane):
    t = o2.shape[0] // 2
    return jnp.where(lane < HEAD_DIM, o2[:t], o2[t:])


def _sb_masks(valid):
    return jnp.where(valid, LOG2E, 0.0), jnp.where(valid, 0.0, MASK_NEG)


def _sb_scores(qq, k, keep):
    z = _dot_nt(qq, k)
    zl = z * LOG2E
    c_nat = jnp.maximum(z, 0.0) + jnp.log(1.0 + jnp.exp2(-jnp.abs(zl)))
    cm = c_nat * keep
    return zl - cm, cm.astype(BF16), cm[:, :LANES]


def _sb_weights(scores, tri, kills, carrieds):
    sums = _dot(jnp.concatenate([sc[1] for sc in scores], axis=0), tri)
    rows = scores[0][0].shape[0]
    out = []
    for n, ((log_beta, _, cm0), kill, carried) in enumerate(zip(scores, kills, carrieds)):
        later = sums[n * rows:(n + 1) * rows]
        log_a = (log_beta - later) + kill
        if carried is not None:
            log_a = log_a - jnp.concatenate([carried] * (log_a.shape[1] // LANES), axis=1)
        out.append((jnp.exp2(log_a).astype(BF16), later[:, :LANES] + cm0))
    return out


def _sb_kernel(n_conv, q_ref, k_ref, v_ref, *refs):
    conv_in, o_ref, conv_out = refs[:n_conv], refs[n_conv], refs[n_conv + 1:2 * n_conv + 1]
    tri_ref, mask_ref, acc_ref, car_ref = refs[2 * n_conv + 1:]
    for src, dst in zip(conv_in, conv_out):
        dst[...] = src[...].astype(BF16)

    bi = pl.program_id(0)
    qi = pl.program_id(1)
    tq, w = SB_Q_TILE, SB_WINDOW
    subs = q_ref.shape[1] // tq
    pairs = q_ref.shape[-1] // LANES
    row0 = qi * (subs * tq)

    @pl.when(jnp.logical_and(bi == 0, qi == 0))
    def _fill_tri():
        j = lax.broadcasted_iota(jnp.int32, (w, w), 0)
        s = lax.broadcasted_iota(jnp.int32, (w, w), 1)
        tri_ref[...] = jnp.where(j > s, 1.0, 0.0).astype(BF16)

    lane = lax.broadcasted_iota(jnp.int32, (tq, LANES), 1)
    lanes_of = lambda p: slice(p * LANES, (p + 1) * LANES)
    rows_of = lambda h: slice(h * tq, (h + 1) * tq)
    first_start = lambda h: pl.multiple_of(jnp.maximum(row0 + (h + 1) * tq - w, 0), tq)

    def key_positions(start):
        return start + lax.broadcasted_iota(jnp.int32, (2 * tq, w), 1)

    assert subs * tq >= w - tq

    @pl.when(qi <= 1)
    def _fill_masks():
        row = lax.broadcasted_iota(jnp.int32, (2 * tq, w), 0) & (tq - 1)
        for h in range(subs):
            mask_ref[h, 0], mask_ref[h, 1] = _sb_masks(
                key_positions(first_start(h)) < row0 + h * tq + row)

    def scores(p, starts, keeps):
        return [_sb_scores(_split_heads(q_ref[0, rows_of(h), lanes_of(p)], lane),
                           k_ref[0, pl.ds(starts[h], w), lanes_of(p)], keeps[h])
                for h in range(subs)]

    first_starts = [first_start(h) for h in range(subs)]

    def first_scores(p):
        return scores(p, first_starts, [mask_ref[h, 0] for h in range(subs)])

    def first_weights(p, sc):
        return _sb_weights(sc, tri_ref[...], [mask_ref[h, 1] for h in range(subs)], [None] * subs)

    def first_values(p, aws):
        least = None
        for h, (a, total) in enumerate(aws):
            o2 = _dot(a, v_ref[0, pl.ds(first_starts[h], w), lanes_of(p)])
            o_ref[0, rows_of(h), lanes_of(p)] = _merge_heads(o2, lane).astype(o_ref.dtype)
            acc_ref[p, h] = o2
            car_ref[p, h] = total
            m = jnp.min(total[:, 0:1])
            least = m if least is None else jnp.minimum(least, m)
        return least

    first = _software_pipeline(pairs, [first_scores, first_weights, first_values])
    live = functools.reduce(jnp.minimum, [first[p] for p in range(pairs)])
    last_start = first_starts[-1]

    @pl.when(jnp.logical_and(last_start > 0, live < SB_DEAD_LOG2))
    def _walk_back():
        for p in range(pairs):
            for h in range(subs):
                car_ref[p, h] = jnp.broadcast_to(car_ref[p, h][:, 0:1], (2 * tq, LANES))

        def body(state):
            step, _ = state
            done = [jnp.maximum(first_start(h) - step * w, 0) for h in range(subs)]
            starts = [pl.multiple_of(jnp.maximum(d - w, 0), tq) for d in done]
            masks = [_sb_masks(key_positions(s) < d) for s, d in zip(starts, done)]

            def more_scores(p):
                return scores(p, starts, [m[0] for m in masks])

            def weights(p, sc):
                return _sb_weights(sc, tri_ref[...], [m[1] for m in masks],
                                   [car_ref[p, h] for h in range(subs)])

            def values(p, aws):
                least = None
                for h, (a, total) in enumerate(aws):
                    acc_ref[p, h] += _dot(a, v_ref[0, pl.ds(starts[h], w), lanes_of(p)])
                    carried = car_ref[p, h] + total[:, 0:1]
                    car_ref[p, h] = carried
                    m = jnp.min(carried)
                    least = m if least is None else jnp.minimum(least, m)
                return least

            mins = _software_pipeline(pairs, [more_scores, weights, values])
            return step + 1, functools.reduce(jnp.minimum, [mins[p] for p in range(pairs)])

        def cond(state):
            step, nxt = state
            return jnp.logical_and(last_start - step * w > 0, nxt < SB_DEAD_LOG2)

        lax.while_loop(cond, body, (jnp.int32(0), live))
        for p in range(pairs):
            for h in range(subs):
                o_ref[0, rows_of(h), lanes_of(p)] = _merge_heads(
                    acc_ref[p, h], lane).astype(o_ref.dtype)


def _sb_attention(q, k, v, weights):
    b, s, wd = q.shape
    pairs = wd // LANES
    subs = SB_STEP_ROWS // SB_Q_TILE
    nq = s // SB_STEP_ROWS
    tile = lambda bi, i: (bi, i, 0)
    whole = lambda bi, i: (bi, 0, 0)
    once = pl.Buffered(1)
    conv_specs = [
        pl.BlockSpec((size, w.shape[1]),
                     lambda bi, i, last=w.shape[0] // size - 1: (jnp.minimum(bi * nq + i, last), 0))
        for w, size in zip(weights, _conversion_blocks(weights, b * nq))]
    return pl.pallas_call(
        functools.partial(_sb_kernel, len(weights)),
        grid=(b, nq),
        in_specs=[
            pl.BlockSpec((1, SB_STEP_ROWS, wd), tile),
            pl.BlockSpec((1, s, wd), whole, pipeline_mode=once),
            pl.BlockSpec((1, s, wd), whole, pipeline_mode=once),
        ] + conv_specs,
        out_specs=[pl.BlockSpec((1, SB_STEP_ROWS, wd), tile)] + conv_specs,
        out_shape=[jax.ShapeDtypeStruct((b, s, wd), BF16)]
        + [jax.ShapeDtypeStruct(w.shape, BF16) for w in weights],
        scratch_shapes=[
            pltpu.VMEM((SB_WINDOW, SB_WINDOW), BF16),
            pltpu.VMEM((subs, 2, 2 * SB_Q_TILE, SB_WINDOW), F32),
            pltpu.VMEM((pairs, subs, 2 * SB_Q_TILE, LANES), F32),
            pltpu.VMEM((pairs, subs, 2 * SB_Q_TILE, LANES), F32),
        ],
        compiler_params=pltpu.CompilerParams(
            dimension_semantics=("arbitrary", "arbitrary"), vmem_limit_bytes=VMEM_LIMIT),
        name="stick_breaking_attention",
    )(q, k, v, *weights)


def _t5_log_bucket_starts():
    exact = REL_BUCKETS // 2
    d = np.arange(exact, 2 * REL_MAX_DIST, dtype=np.float32)
    large = exact + (np.log(d / np.float32(exact)) / np.float32(math.log(REL_MAX_DIST / exact))
                     * np.float32(REL_BUCKETS - exact)).astype(np.int32)
    large = np.minimum(large, REL_BUCKETS - 1)
    assert large[0] == exact and np.all(np.diff(large) >= 0)
    return [int(d[np.argmax(large >= bk)]) for bk in range(exact + 1, REL_BUCKETS)]


def _t5_causal_bucket(distance):
    d = jnp.maximum(distance, 0)
    bucket = jnp.minimum(d, REL_BUCKETS // 2)
    for start in _t5_log_bucket_starts():
        bucket = bucket + jnp.where(d >= start, 1, 0)
    return bucket


def _swa_fill_bias(relb_ref, bias_ref):
    t = ATT_BLOCK
    r = lax.broadcasted_iota(jnp.int32, (t, 2 * t), 0)
    c = lax.broadcasted_iota(jnp.int32, (t, 2 * t), 1)
    dist = r + t - c
    bucket = _t5_causal_bucket(dist)
    valid = jnp.logical_and(dist >= 0, dist < WINDOW)
    for h in range(SWA_HEADS):
        bias = jnp.zeros((t, 2 * t), F32)
        for bk in range(REL_BUCKETS):
            bias = jnp.where(bucket == bk, relb_ref[bk, h], bias)
        bias_ref[h] = jnp.where(valid, bias, MASK_NEG)


def _swa_schedule(sinks_ref, q_ref, kvc_ref, kvp_ref, bias_ref, o_ref, no_previous):
    t = ATT_BLOCK
    kvw = kvc_ref.shape[-1] // 4
    group = SWA_HEADS // SWA_KV_HEADS
    tiles = q_ref.shape[0] // t
    rows_of = lambda n: slice(n * t, (n + 1) * t)
    lane = lax.broadcasted_iota(jnp.int32, (t, LANES), 1)
    row2 = lax.broadcasted_iota(jnp.int32, (2 * t, 1), 0)
    col2 = lax.broadcasted_iota(jnp.int32, (2 * t, 2 * t), 1)
    prev_pen = jnp.where(jnp.logical_and(col2 < t, no_previous), MASK_NEG, 0.0)

    def kv_window(n, variant):
        cols = slice(variant * kvw, (variant + 1) * kvw)
        before = kvc_ref[rows_of(n - 1), cols] if n else kvp_ref[:, cols]
        return jnp.concatenate([before, kvc_ref[rows_of(n), cols]], axis=0)

    items = [(n, half, j) for n in range(tiles) for half in range(2) for j in range(SWA_KV_HEADS)]
    heads_of = lambda half, j: [h for h in range(j * group, (j + 1) * group) if h % 2 == half]

    def probabilities(i):
        n, half, j = items[i]
        heads = heads_of(half, j)
        in_half = (lane >= HEAD_DIM) if half else (lane < HEAD_DIM)
        qq = jnp.concatenate(
            [jnp.where(in_half, q_ref[rows_of(n), (h // 2) * LANES:(h // 2 + 1) * LANES], 0)
             for h in heads], axis=0)
        s = _dot_nt(qq, kv_window(n, 0 + int(j != half)))
        s = s + jnp.concatenate([bias_ref[h] for h in heads], axis=0)
        if n == 0:
            s = s + prev_pen
        sink = jnp.where(row2 < t, sinks_ref[heads[0]], sinks_ref[heads[1]])
        m = jnp.maximum(jnp.max(s, axis=-1, keepdims=True), sink)
        p = jnp.exp(s - m)
        denom = jnp.sum(p, axis=-1, keepdims=True) + jnp.exp(sink - m)
        return p.astype(BF16), 1.0 / denom

    def values(i, pd):
        n, half, j = items[i]
        p, inv_denom = pd
        return _dot(p, kv_window(n, 2 + int(j != half))) * inv_denom

    thunks, outs = _pipeline_schedule(len(items), [probabilities, values])

    def store():
        for i, (n, half, j) in enumerate(items):
            if half == 0:
                even, odd = outs[i], outs[items.index((n, 1, j))]
                for e, h in enumerate(heads_of(0, j)):
                    o_ref[rows_of(n), (h // 2) * LANES:(h // 2 + 1) * LANES] = jnp.where(
                        lane < HEAD_DIM, even[e * t:(e + 1) * t], odd[e * t:(e + 1) * t]
                    ).astype(o_ref.dtype)

    return thunks + [store]


def _tail_schedule(ha_ref, sb_ref, sw_ref, sbg_ref, swg_ref, wo_ref, l1g_ref, l1b_ref,
                   wgu_ref, wd_ref, l2g_ref, l2b_ref, o_ref):
    d = ha_ref.shape[1]
    d_ff = wd_ref.shape[0]
    mw = MXU_WIDTH
    bounds = list(range(0, d_ff, FFN_CHUNK)) + [d_ff]
    n_chunks = len(bounds) - 1
    out_tiles = range(d // mw)
    st = {}

    def merged():
        sb = _rms_norm(sb_ref[...].astype(F32), sbg_ref[...]).astype(BF16)
        sw = _rms_norm(sw_ref[...].astype(F32), swg_ref[...]).astype(BF16)
        st["merged"] = jnp.concatenate([sb, sw], axis=1)

    def out_projection(n):
        st["mix", n] = _dot(st["merged"], wo_ref[:, n * mw:(n + 1) * mw])

    def first_norm():
        mix = jnp.concatenate([st.pop(("mix", n)) for n in out_tiles], axis=1)
        st.pop("merged")
        h1 = _layer_norm(ha_ref[...] + mix, l1g_ref[...], l1b_ref[...])
        st["h1b"] = h1.astype(BF16)
        for n in out_tiles:
            st["acc", n] = ALPHA * h1[:, n * mw:(n + 1) * mw]

    def activation(c, n):
        lo = bounds[c] + n * mw
        gate = _dot(st["h1b"], wgu_ref[:, lo:lo + mw])
        up = _dot(st["h1b"], wgu_ref[:, d_ff + lo:d_ff + lo + mw])
        st["act", c, n] = (gate / (1.0 + jnp.exp(-gate)) * up).astype(BF16)

    def down(c, n):
        tiles = range((bounds[c + 1] - bounds[c]) // mw)
        if n == 0:
            st["act", c] = jnp.concatenate([st.pop(("act", c, k)) for k in tiles], axis=1)
        st["acc", n] = st["acc", n] + _dot(st["act", c], wd_ref[bounds[c]:bounds[c + 1],
                                                               n * mw:(n + 1) * mw])

    def second_norm():
        y = jnp.concatenate([st.pop(("acc", n)) for n in out_tiles], axis=1)
        o_ref[...] = _layer_norm(y, l2g_ref[...], l2b_ref[...])

    act_thunks = [[functools.partial(activation, c, n)
                   for n in range((bounds[c + 1] - bounds[c]) // mw)] for c in range(n_chunks)]
    down_thunks = [[functools.partial(down, c, n) for n in out_tiles] for c in range(n_chunks)]
    ffn = list(act_thunks[0])
    for c in range(1, n_chunks):
        ffn += _interleave(act_thunks[c], down_thunks[c - 1])
    ffn += down_thunks[-1]
    return (merged, [functools.partial(out_projection, n) for n in out_tiles], first_norm,
            ffn, second_norm)


def _attn_tail_kernel(tiles_per_seq, last_tile, sinks_ref, relb_ref, q_ref, kvc_ref, kvp_ref,
                      ha_ref, sb_ref, sbg_ref, swg_ref, wo_ref, l1g_ref, l1b_ref,
                      wgu_ref, wd_ref, l2g_ref, l2b_ref, o_ref, bias_ref, sw_ref):
    t = pl.program_id(0)

    @pl.when(t == 0)
    def _first_step():
        _swa_fill_bias(relb_ref, bias_ref)
        sw_ref[1] = jnp.zeros(sw_ref.shape[1:], sw_ref.dtype)

    no_previous = (jnp.minimum(t, last_tile) % tiles_per_seq) == 0
    attention = _swa_schedule(sinks_ref, q_ref.at[0], kvc_ref.at[0], kvp_ref.at[0], bias_ref,
                              sw_ref.at[t % 2], no_previous)
    merged, out_proj, first_norm, ffn, second_norm = _tail_schedule(
        ha_ref, sb_ref, sw_ref.at[(t + 1) % 2], sbg_ref, swg_ref, wo_ref,
        l1g_ref, l1b_ref, wgu_ref, wd_ref, l2g_ref, l2b_ref, o_ref)
    tail = [merged] + out_proj + [first_norm] + ffn + [second_norm]
    for thunk in _interleave(tail, attention):
        thunk()


def _attn_tail_block(q_sw, kv_sw, sinks, rel_bias, ha, sb2, sb_g, sw_g, w_out_b, ln1_g, ln1_b,
                     w_gate_up_b, w_down_b, ln2_g, ln2_b):
    b, s, wd = q_sw.shape
    n, d = ha.shape
    tiles_per_seq = s // TAIL_ROWS
    last_tile = n // TAIL_ROWS - 1
    key_tiles = TAIL_ROWS // ATT_BLOCK
    att = lambda t: jnp.minimum(t, last_tile)
    cur = lambda t: (att(t) // tiles_per_seq, att(t) % tiles_per_seq, 0)
    prev = lambda t: (att(t) // tiles_per_seq,
                      jnp.maximum(att(t) % tiles_per_seq * key_tiles - 1, 0), 0)
    row = lambda t: (jnp.maximum(t - 1, 0), 0)
    fixed = lambda t: (0, 0)
    smem = pl.BlockSpec(memory_space=pltpu.SMEM)
    vec = lambda a: pl.BlockSpec((1, a.shape[-1]), fixed)
    weight = lambda a: pl.BlockSpec(a.shape, fixed, pipeline_mode=pl.Buffered(1))
    return pl.pallas_call(
        functools.partial(_attn_tail_kernel, tiles_per_seq, last_tile),
        grid=(last_tile + 2,),
        in_specs=[
            smem, smem,
            pl.BlockSpec((1, TAIL_ROWS, wd), cur),
            pl.BlockSpec((1, TAIL_ROWS, kv_sw.shape[-1]), cur),
            pl.BlockSpec((1, ATT_BLOCK, kv_sw.shape[-1]), prev),
            pl.BlockSpec((TAIL_ROWS, d), row),
            pl.BlockSpec((TAIL_ROWS, sb2.shape[-1]), row),
            vec(sb_g), vec(sw_g), weight(w_out_b), vec(ln1_g), vec(ln1_b),
            weight(w_gate_up_b), weight(w_down_b), vec(ln2_g), vec(ln2_b),
        ],
        out_specs=pl.BlockSpec((TAIL_ROWS, d), row),
        out_shape=jax.ShapeDtypeStruct((n, d), F32),
        scratch_shapes=[
            pltpu.VMEM((SWA_HEADS, ATT_BLOCK, 2 * ATT_BLOCK), F32),
            pltpu.VMEM((2, TAIL_ROWS, wd), BF16),
        ],
        compiler_params=pltpu.CompilerParams(
            dimension_semantics=("arbitrary",), vmem_limit_bytes=VMEM_LIMIT),
        name="swa_attention_and_dense_tail",
    )(sinks, rel_bias, q_sw, kv_sw, kv_sw, ha, sb2, sb_g, sw_g, w_out_b, ln1_g, ln1_b,
      w_gate_up_b, w_down_b, ln2_g, ln2_b)


SB_GROUP = 2
SB_LIVE_DONE = 1e30


def _sb_fast_masks(mask_ref):
    tq, w = SB_Q_TILE, SB_WINDOW
    row = lax.broadcasted_iota(jnp.int32, (2 * tq, w), 0) & (tq - 1)
    col = lax.broadcasted_iota(jnp.int32, (2 * tq, w), 1)
    causal = col < row + (w - tq)
    mask_ref[0, 0], mask_ref[0, 1] = _sb_masks(causal)
    for h in range(mask_ref.shape[0] - 1):
        first_key = w - (h + 1) * tq
        mask_ref[1 + h, 0], mask_ref[1 + h, 1] = _sb_masks(
            jnp.logical_and(causal, col >= first_key))


def _sb_fast_schedule(q_ref, kc_ref, kp_ref, vc_ref, vp_ref, tri_ref, mask_ref, o_ref,
                      no_previous, lives):
    tq, w = SB_Q_TILE, SB_WINDOW
    subs = q_ref.shape[0] // tq
    pairs = q_ref.shape[-1] // LANES
    clamped = w // tq - 1
    assert mask_ref.shape[0] == clamped + 1 and subs % SB_GROUP == 0
    lane = lax.broadcasted_iota(jnp.int32, (tq, LANES), 1)
    lanes_of = lambda p: slice(p * LANES, (p + 1) * LANES)
    rows_of = lambda h: slice(h * tq, (h + 1) * tq)

    def window(h, cur_ref, prev_ref, p):
        start = (h + 1) * tq - w
        if start >= 0:
            return cur_ref[start:start + w, lanes_of(p)]
        return jnp.concatenate([prev_ref[w + start:w, lanes_of(p)],
                                cur_ref[0:w + start, lanes_of(p)]], axis=0)

    def mask(h, which):
        if h >= clamped:
            return mask_ref[0, which]
        return mask_ref[jnp.where(no_previous, 1 + h, 0), which]

    items = [(p, g) for p in range(pairs) for g in range(subs // SB_GROUP)]
    group = lambda g: range(g * SB_GROUP, (g + 1) * SB_GROUP)

    def scores(i):
        p, g = items[i]
        return [_sb_scores(_split_heads(q_ref[rows_of(h), lanes_of(p)], lane),
                           window(h, kc_ref, kp_ref, p), mask(h, 0)) for h in group(g)]

    def weights(i, sc):
        return _sb_weights(sc, tri_ref[...], [mask(h, 1) for h in group(items[i][1])],
                           [None] * SB_GROUP)

    def values(i, aws):
        p, g = items[i]
        for h, (a, total) in zip(group(g), aws):
            o2 = _dot(a, window(h, vc_ref, vp_ref, p))
            o_ref[rows_of(h), lanes_of(p)] = _merge_heads(o2, lane).astype(o_ref.dtype)
            m = jnp.min(total[:, 0:1])
            lives.append(m if h > clamped else jnp.where(no_previous, SB_LIVE_DONE, m))

    return _pipeline_schedule(len(items), [scores, weights, values])[0]


def _sb_walk_back(q_ref, k_hbm, v_hbm, batch, row0, tri_ref, o_ref, kwin_ref, vwin_ref,
                  acc_ref, car_ref, sems):
    tq, w = SB_Q_TILE, SB_WINDOW
    subs = q_ref.shape[0] // tq
    pairs = q_ref.shape[-1] // LANES
    lane = lax.broadcasted_iota(jnp.int32, (tq, LANES), 1)
    lanes_of = lambda p: slice(p * LANES, (p + 1) * LANES)
    row = lax.broadcasted_iota(jnp.int32, (2 * tq, w), 0) & (tq - 1)
    col = lax.broadcasted_iota(jnp.int32, (2 * tq, w), 1)

    def sub_tile(h, carry):
        rows = pl.ds(pl.multiple_of(h * tq, tq), tq)
        first_row = row0 + h * tq
        acc_ref[...] = jnp.zeros(acc_ref.shape, F32)
        car_ref[...] = jnp.zeros(car_ref.shape, F32)

        def window(state):
            done, _ = state
            start = pl.multiple_of(jnp.maximum(done - w, 0), tq)
            copies = [pltpu.make_async_copy(src.at[batch, pl.ds(start, w), :], dst, sems.at[i])
                      for i, (src, dst) in enumerate(((k_hbm, kwin_ref), (v_hbm, vwin_ref)))]
            for copy in copies:
                copy.start()
            for copy in copies:
                copy.wait()
            key = start + col
            keep, kill = _sb_masks(jnp.logical_and(key < done, key < first_row + row))
            least = None
            for p in range(pairs):
                qq = _split_heads(q_ref[rows, lanes_of(p)], lane)
                sc = _sb_scores(qq, kwin_ref[:, lanes_of(p)], keep)
                ((a, total),) = _sb_weights([sc], tri_ref[...], [kill], [car_ref[p]])
                acc_ref[p] += _dot(a, vwin_ref[:, lanes_of(p)])
                carried = car_ref[p] + total[:, 0:1]
                car_ref[p] = carried
                m = jnp.min(carried)
                least = m if least is None else jnp.minimum(least, m)
            return start, least

        lax.while_loop(lambda state: jnp.logical_and(state[0] > 0, state[1] < SB_DEAD_LOG2),
                       window, (first_row + tq, jnp.float32(0.0)))
        for p in range(pairs):
            o_ref[rows, lanes_of(p)] = _merge_heads(acc_ref[p], lane).astype(o_ref.dtype)
        return carry

    lax.fori_loop(0, subs, sub_tile, 0)


def _fast_kernel(tiles_per_seq, last_tile, sinks_ref, relb_ref, qsw_ref, kvc_ref, kvp_ref,
                 qsb_ref, kc_ref, kp_ref, vc_ref, vp_ref,
                 ha_ref, sbg_ref, swg_ref, wo_ref, l1g_ref, l1b_ref,
                 wgu_ref, wd_ref, l2g_ref, l2b_ref, k_hbm, v_hbm, o_ref,
                 bias_ref, sw_ref, sb_ref, tri_ref, mask_ref,
                 kwin_ref, vwin_ref, acc_ref, car_ref, sems):
    t = pl.program_id(0)

    @pl.when(t == 0)
    def _first_step():
        _swa_fill_bias(relb_ref, bias_ref)
        _sb_fast_masks(mask_ref)
        j = lax.broadcasted_iota(jnp.int32, tri_ref.shape, 0)
        s = lax.broadcasted_iota(jnp.int32, tri_ref.shape, 1)
        tri_ref[...] = jnp.where(j > s, 1.0, 0.0).astype(BF16)
        sw_ref[1] = jnp.zeros(sw_ref.shape[1:], sw_ref.dtype)
        sb_ref[1] = jnp.zeros(sb_ref.shape[1:], sb_ref.dtype)

    no_previous = (jnp.minimum(t, last_tile) % tiles_per_seq) == 0
    lives = []
    swa = _swa_schedule(sinks_ref, qsw_ref.at[0], kvc_ref.at[0], kvp_ref.at[0], bias_ref,
                        sw_ref.at[t % 2], no_previous)
    sb = _sb_fast_schedule(qsb_ref.at[0], kc_ref.at[0], kp_ref.at[0], vc_ref.at[0], vp_ref.at[0],
                           tri_ref, mask_ref, sb_ref.at[t % 2], no_previous, lives)
    merged, out_proj, first_norm, ffn, second_norm = _tail_schedule(
        ha_ref, sb_ref.at[(t + 1) % 2], sw_ref.at[(t + 1) % 2], sbg_ref, swg_ref, wo_ref,
        l1g_ref, l1b_ref, wgu_ref, wd_ref, l2g_ref, l2b_ref, o_ref)
    tail = [merged] + out_proj + [first_norm] + ffn + [second_norm]
    for thunk in _interleave(tail, _interleave(swa, sb)):
        thunk()

    @pl.when(functools.reduce(jnp.minimum, lives) < SB_DEAD_LOG2)
    def _redo_with_walk_back():
        tile = jnp.minimum(t, last_tile)
        _sb_walk_back(qsb_ref.at[0], k_hbm, v_hbm, tile // tiles_per_seq,
                      tile % tiles_per_seq * qsb_ref.shape[1], tri_ref, sb_ref.at[t % 2],
                      kwin_ref, vwin_ref, acc_ref, car_ref, sems)


def _fast_block(q_sw, kv_sw, q_sb, k_sb, v_sb, sinks, rel_bias, ha, sb_g, sw_g, w_out_b,
                ln1_g, ln1_b, w_gate_up_b, w_down_b, ln2_g, ln2_b):
    b, s, wd = q_sw.shape
    n, d = ha.shape
    tiles_per_seq = s // TAIL_ROWS
    last_tile = n // TAIL_ROWS - 1
    att = lambda t: jnp.minimum(t, last_tile)
    cur = lambda t: (att(t) // tiles_per_seq, att(t) % tiles_per_seq, 0)

    def prev(rows):
        per_tile = TAIL_ROWS // rows
        return lambda t: (att(t) // tiles_per_seq,
                          jnp.maximum(att(t) % tiles_per_seq * per_tile - 1, 0), 0)

    row = lambda t: (jnp.maximum(t - 1, 0), 0)
    fixed = lambda t: (0, 0)
    smem = pl.BlockSpec(memory_space=pltpu.SMEM)
    vec = lambda a: pl.BlockSpec((1, a.shape[-1]), fixed)
    weight = lambda a: pl.BlockSpec(a.shape, fixed, pipeline_mode=pl.Buffered(1))
    tile3 = lambda a: pl.BlockSpec((1, TAIL_ROWS, a.shape[-1]), cur)
    before = lambda a, rows: pl.BlockSpec((1, rows, a.shape[-1]), prev(rows))
    steps = last_tile + 2
    return pl.pallas_call(
        functools.partial(_fast_kernel, tiles_per_seq, last_tile),
        grid=(steps,),
        in_specs=[
            smem, smem,
            tile3(q_sw), tile3(kv_sw), before(kv_sw, ATT_BLOCK),
            tile3(q_sb), tile3(k_sb), before(k_sb, SB_WINDOW), tile3(v_sb), before(v_sb, SB_WINDOW),
            pl.BlockSpec((TAIL_ROWS, d), row),
            vec(sb_g), vec(sw_g), weight(w_out_b), vec(ln1_g), vec(ln1_b),
            weight(w_gate_up_b), weight(w_down_b), vec(ln2_g), vec(ln2_b),
            pl.BlockSpec(memory_space=pl.ANY), pl.BlockSpec(memory_space=pl.ANY),
        ],
        out_specs=pl.BlockSpec((TAIL_ROWS, d), row),
        out_shape=jax.ShapeDtypeStruct((n, d), F32),
        scratch_shapes=[
            pltpu.VMEM((SWA_HEADS, ATT_BLOCK, 2 * ATT_BLOCK), F32),
            pltpu.VMEM((2, TAIL_ROWS, wd), BF16),
            pltpu.VMEM((2, TAIL_ROWS, q_sb.shape[-1]), BF16),
            pltpu.VMEM((SB_WINDOW, SB_WINDOW), BF16),
            pltpu.VMEM((SB_WINDOW // SB_Q_TILE, 2, 2 * SB_Q_TILE, SB_WINDOW), F32),
            pltpu.VMEM((SB_WINDOW, k_sb.shape[-1]), BF16),
            pltpu.VMEM((SB_WINDOW, v_sb.shape[-1]), BF16),
            pltpu.VMEM((q_sb.shape[-1] // LANES, 2 * SB_Q_TILE, LANES), F32),
            pltpu.VMEM((q_sb.shape[-1] // LANES, 2 * SB_Q_TILE, LANES), F32),
            pltpu.SemaphoreType.DMA((2,)),
        ],
        compiler_params=pltpu.CompilerParams(
            dimension_semantics=("arbitrary",), vmem_limit_bytes=VMEM_LIMIT),
        name="attention_and_dense_tail",
    )(sinks, rel_bias, q_sw, kv_sw, kv_sw, q_sb, k_sb, k_sb, v_sb, v_sb, ha, sb_g, sw_g, w_out_b,
      ln1_g, ln1_b, w_gate_up_b, w_down_b, ln2_g, ln2_b, k_sb, v_sb)


def kernel(x, ln_in_g, ln_in_b, w_in, sb_norm_g, swa_norm_g, sinks, rel_bias, w_out,
           ln1_g, ln1_b, w_gate_up, w_down, ln2_g, ln2_b):
    b, s, d = x.shape
    assert w_in.shape[0] == DEPTH == 1
    sb_w = SB_HEADS * HEAD_DIM
    sw_w = SWA_HEADS * HEAD_DIM
    kv_w = SWA_KV_HEADS * HEAD_DIM
    assert w_in.shape[-1] == 3 * sb_w + sw_w + 2 * kv_w
    assert kv_w == LANES and SWA_HEADS // SWA_KV_HEADS == 4
    assert s % SB_WINDOW == 0 and s % SB_STEP_ROWS == 0 and s % TAIL_ROWS == 0
    assert (b * s) % ROW_TILE == 0 and WINDOW <= ATT_BLOCK and d % MXU_WIDTH == 0
    assert FFN_CHUNK % MXU_WIDTH == 0 and w_down.shape[1] % MXU_WIDTH == 0

    x2 = x.reshape(b * s, d)
    row_vec = lambda a: a.reshape(1, -1)

    names = ("q_sb", "k_sb", "v_sb", "q_sw", "k_sw", "v_sw")
    widths = (sb_w, sb_w, sb_w, sw_w, kv_w, kv_w)
    splits, lo = {}, 0
    for name, wd in zip(names, widths):
        splits[name] = (lo, lo + wd)
        lo += wd
    ha, q_sb, k_sb, v_sb, q_sw, kv_sw, w_out_b, w_gate_up_b, w_down_b = _in_projection(
        x2, row_vec(ln_in_g), row_vec(ln_in_b), w_in[0], splits,
        [w_out[0], w_gate_up[0], w_down[0]])

    to3 = lambda a: a.reshape(b, s, a.shape[-1])
    q_sb, k_sb, v_sb, q_sw, kv_sw = map(to3, (q_sb, k_sb, v_sb, q_sw, kv_sw))
    out = _fast_block(q_sw, kv_sw, q_sb, k_sb, v_sb, sinks[0], rel_bias, ha,
                      row_vec(sb_norm_g[0]), row_vec(swa_norm_g[0]), w_out_b,
                      row_vec(ln1_g[0]), row_vec(ln1_b[0]), w_gate_up_b, w_down_b,
                      row_vec(ln2_g[0]), row_vec(ln2_b[0]))
    return out.reshape(b, s, d)
```

```python
import functools
import math

import jax
import jax.numpy as jnp
import numpy as np
from jax import lax
from jax.experimental import pallas as pl
from jax.experimental.pallas import tpu as pltpu

F32 = jnp.float32
BF16 = jnp.bfloat16

HEAD_DIM = 64
SB_HEADS = 8
SWA_HEADS = 8
SWA_KV_HEADS = 2
WINDOW = 128
REL_BUCKETS = 32
REL_MAX_DIST = 128
LN_EPS = 1e-5
RMS_EPS = 1e-6
DEPTH = 1
ALPHA = (2 * DEPTH) ** 0.25
QK_SCALE = HEAD_DIM ** -0.5
LOG2E = math.log2(math.e)

LANES = 128
BF16_ROWS = 16
MXU_WIDTH = 256
ATT_BLOCK = 128
SB_WINDOW = 256
SB_Q_TILE = 64
SB_STEP_ROWS = 512
ROW_TILE = 1024
PROJ_PARTS = 2
TAIL_ROWS = 512
FFN_CHUNK = 4 * MXU_WIDTH
VMEM_LIMIT = 56 * 1024 * 1024

SB_DEAD_LOG2 = 127.0
MASK_NEG = -1e30


def _layer_norm(x, g, b):
    mu = jnp.mean(x, axis=-1, keepdims=True)
    xc = x - mu
    var = jnp.mean(xc * xc, axis=-1, keepdims=True)
    return xc * lax.rsqrt(var + LN_EPS) * g + b


def _rms_norm(x, g):
    ms = jnp.mean(x * x, axis=-1, keepdims=True)
    return x * lax.rsqrt(ms + RMS_EPS) * g


def _dot(a, b):
    return jnp.dot(a, b, preferred_element_type=F32)


def _dot_nt(a, b):
    return lax.dot_general(a, b, (((1,), (1,)), ((), ())), preferred_element_type=F32)


def _pipeline_schedule(n, stages):
    vals = [dict() for _ in stages]
    thunks = []
    for step in range(n + len(stages) - 1):
        for s, stage in enumerate(stages):
            i = step - s
            if 0 <= i < n:
                def thunk(s=s, i=i, stage=stage):
                    vals[s][i] = stage(i, vals[s - 1].pop(i)) if s else stage(i)
                thunks.append(thunk)
    return thunks, vals[-1]


def _software_pipeline(n, stages):
    thunks, results = _pipeline_schedule(n, stages)
    for thunk in thunks:
        thunk()
    return results


def _interleave(xs, ys):
    merged, taken = [], 0
    for i, x in enumerate(xs):
        merged.append(x)
        upto = (i + 1) * len(ys) // len(xs)
        merged.extend(ys[taken:upto])
        taken = upto
    return merged


def _inproj_kernel(splits, n_later, x_ref, g_ref, b_ref, w32_ref, *refs):
    later_in, refs = refs[:n_later], refs[n_later:]
    ha_ref, qsb_ref, ksb_ref, vsb_ref, qsw_ref, kv_ref = refs[:6]
    later_out, w_ref = refs[6:6 + n_later], refs[6 + n_later]
    rows = x_ref.shape[0] // PROJ_PARTS
    part = lambda i: slice(i * rows, (i + 1) * rows)
    assert splits["k_sw"][1] == splits["v_sw"][0]

    @pl.when(pl.program_id(0) == 0)
    def _convert_weight():
        w_ref[...] = w32_ref[...].astype(BF16)

    for src, dst in zip(later_in, later_out):
        dst[...] = src[...].astype(BF16)

    def normed(i):
        h32 = _layer_norm(x_ref[part(i), :], g_ref[...], b_ref[...])
        ha_ref[part(i), :] = ALPHA * h32
        return h32.astype(BF16)

    def projected(i, h):
        r = part(i)
        proj = lambda lo, hi: _dot(h, w_ref[:, lo:hi])
        qsb_ref[r, :] = (proj(*splits["q_sb"]) * QK_SCALE).astype(BF16)
        ksb_ref[r, :] = proj(*splits["k_sb"]).astype(BF16)
        vsb_ref[r, :] = proj(*splits["v_sb"]).astype(BF16)
        qsw_ref[r, :] = (proj(*splits["q_sw"]) * QK_SCALE).astype(BF16)
        kv = proj(splits["k_sw"][0], splits["v_sw"][1])
        kvw = kv.shape[-1] // 2
        k, v = kv[:, :kvw], kv[:, kvw:]
        kv_ref[r, 0 * kvw:1 * kvw] = k.astype(BF16)
        kv_ref[r, 1 * kvw:2 * kvw] = pltpu.roll(k, HEAD_DIM, axis=1).astype(BF16)
        kv_ref[r, 2 * kvw:3 * kvw] = v.astype(BF16)
        kv_ref[r, 3 * kvw:4 * kvw] = pltpu.roll(v, HEAD_DIM, axis=1).astype(BF16)

    _software_pipeline(PROJ_PARTS, [normed, projected])


def _in_projection(x2, g, b, w_in, splits, later_weights):
    n, d = x2.shape
    steps = n // ROW_TILE
    width = lambda name: splits[name][1] - splits[name][0]
    out_w = [width("q_sb"), width("k_sb"), width("v_sb"), width("q_sw"), 4 * width("k_sw")]
    row = lambda i: (i, 0)
    fixed = lambda i: (0, 0)
    later_specs = [
        pl.BlockSpec((size, w.shape[1]),
                     lambda i, last=w.shape[0] // size - 1: (jnp.minimum(i, last), 0))
        for w, size in zip(later_weights, _conversion_blocks(later_weights, steps))]
    return pl.pallas_call(
        functools.partial(_inproj_kernel, splits, len(later_weights)),
        grid=(steps,),
        in_specs=[
            pl.BlockSpec((ROW_TILE, d), row),
            pl.BlockSpec((1, d), fixed),
            pl.BlockSpec((1, d), fixed),
            pl.BlockSpec(w_in.shape, fixed, pipeline_mode=pl.Buffered(1)),
        ] + later_specs,
        out_specs=[pl.BlockSpec((ROW_TILE, d), row)]
        + [pl.BlockSpec((ROW_TILE, w), row) for w in out_w] + later_specs,
        out_shape=[jax.ShapeDtypeStruct((n, d), F32)]
        + [jax.ShapeDtypeStruct((n, w), BF16) for w in out_w]
        + [jax.ShapeDtypeStruct(w.shape, BF16) for w in later_weights],
        scratch_shapes=[pltpu.VMEM(w_in.shape, BF16)],
        compiler_params=pltpu.CompilerParams(
            dimension_semantics=("arbitrary",), vmem_limit_bytes=VMEM_LIMIT),
        name="ln_in_proj",
    )(x2, g, b, w_in, *later_weights)


def _conversion_blocks(weights, steps):
    sizes = []
    for w in weights:
        rows = w.shape[0]
        sizes.append(next(r for r in range(BF16_ROWS, rows + 1, BF16_ROWS)
                          if rows % r == 0 and rows // r <= steps))
    return sizes


def _split_heads(x, lane):
    zero = jnp.zeros_like(x)
    return jnp.concatenate([jnp.where(lane < HEAD_DIM, x, zero),
                            jnp.where(lane >= HEAD_DIM, x, zero)], axis=0)


def _merge_heads(o2, lane):
    t = o2.shape[0] // 2
    return jnp.where(lane < HEAD_DIM, o2[:t], o2[t:])


def _sb_masks(valid):
    return jnp.where(valid, LOG2E, 0.0), jnp.where(valid, 0.0, MASK_NEG)


def _sb_scores(qq, k, keep):
    z = _dot_nt(qq, k)
    zl = z * LOG2E
    c_nat = jnp.maximum(z, 0.0) + jnp.log(1.0 + jnp.exp2(-jnp.abs(zl)))
    cm = c_nat * keep
    return zl - cm, cm.astype(BF16), cm[:, :LANES]


def _sb_weights(scores, tri, kills, carrieds):
    sums = _dot(jnp.concatenate([sc[1] for sc in scores], axis=0), tri)
    rows = scores[0][0].shape[0]
    out = []
    for n, ((log_beta, _, cm0), kill, carried) in enumerate(zip(scores, kills, carrieds)):
        later = sums[n * rows:(n + 1) * rows]
        log_a = (log_beta - later) + kill
        if carried is not None:
            log_a = log_a - jnp.concatenate([carried] * (log_a.shape[1] // LANES), axis=1)
        out.append((jnp.exp2(log_a).astype(BF16), later[:, :LANES] + cm0))
    return out


def _sb_kernel(n_conv, q_ref, k_ref, v_ref, *refs):
    conv_in, o_ref, conv_out = refs[:n_conv], refs[n_conv], refs[n_conv + 1:2 * n_conv + 1]
    tri_ref, mask_ref, acc_ref, car_ref = refs[2 * n_conv + 1:]
    for src, dst in zip(conv_in, conv_out):
        dst[...] = src[...].astype(BF16)

    bi = pl.program_id(0)
    qi = pl.program_id(1)
    tq, w = SB_Q_TILE, SB_WINDOW
    subs = q_ref.shape[1] // tq
    pairs = q_ref.shape[-1] // LANES
    row0 = qi * (subs * tq)

    @pl.when(jnp.logical_and(bi == 0, qi == 0))
    def _fill_tri():
        j = lax.broadcasted_iota(jnp.int32, (w, w), 0)
        s = lax.broadcasted_iota(jnp.int32, (w, w), 1)
        tri_ref[...] = jnp.where(j > s, 1.0, 0.0).astype(BF16)

    lane = lax.broadcasted_iota(jnp.int32, (tq, LANES), 1)
    lanes_of = lambda p: slice(p * LANES, (p + 1) * LANES)
    rows_of = lambda h: slice(h * tq, (h + 1) * tq)
    first_start = lambda h: pl.multiple_of(jnp.maximum(row0 + (h + 1) * tq - w, 0), tq)

    def key_positions(start):
        return start + lax.broadcasted_iota(jnp.int32, (2 * tq, w), 1)

    assert subs * tq >= w - tq

    @pl.when(qi <= 1)
    def _fill_masks():
        row = lax.broadcasted_iota(jnp.int32, (2 * tq, w), 0) & (tq - 1)
        for h in range(subs):
            mask_ref[h, 0], mask_ref[h, 1] = _sb_masks(
                key_positions(first_start(h)) < row0 + h * tq + row)

    def scores(p, starts, keeps):
        return [_sb_scores(_split_heads(q_ref[0, rows_of(h), lanes_of(p)], lane),
                           k_ref[0, pl.ds(starts[h], w), lanes_of(p)], keeps[h])
                for h in range(subs)]

    first_starts = [first_start(h) for h in range(subs)]

    def first_scores(p):
        return scores(p, first_starts, [mask_ref[h, 0] for h in range(subs)])

    def first_weights(p, sc):
        return _sb_weights(sc, tri_ref[...], [mask_ref[h, 1] for h in range(subs)], [None] * subs)

    def first_values(p, aws):
        least = None
        for h, (a, total) in enumerate(aws):
            o2 = _dot(a, v_ref[0, pl.ds(first_starts[h], w), lanes_of(p)])
            o_ref[0, rows_of(h), lanes_of(p)] = _merge_heads(o2, lane).astype(o_ref.dtype)
            acc_ref[p, h] = o2
            car_ref[p, h] = total
            m = jnp.min(total[:, 0:1])
            least = m if least is None else jnp.minimum(least, m)
        return least

    first = _software_pipeline(pairs, [first_scores, first_weights, first_values])
    live = functools.reduce(jnp.minimum, [first[p] for p in range(pairs)])
    last_start = first_starts[-1]

    @pl.when(jnp.logical_and(last_start > 0, live < SB_DEAD_LOG2))
    def _walk_back():
        for p in range(pairs):
            for h in range(subs):
                car_ref[p, h] = jnp.broadcast_to(car_ref[p, h][:, 0:1], (2 * tq, LANES))

        def body(state):
            step, _ = state
            done = [jnp.maximum(first_start(h) - step * w, 0) for h in range(subs)]
            starts = [pl.multiple_of(jnp.maximum(d - w, 0), tq) for d in done]
            masks = [_sb_masks(key_positions(s) < d) for s, d in zip(starts, done)]

            def more_scores(p):
                return scores(p, starts, [m[0] for m in masks])

            def weights(p, sc):
                return _sb_weights(sc, tri_ref[...], [m[1] for m in masks],
                                   [car_ref[p, h] for h in range(subs)])

            def values(p, aws):
                least = None
                for h, (a, total) in enumerate(aws):
                    acc_ref[p, h] += _dot(a, v_ref[0, pl.ds(starts[h], w), lanes_of(p)])
                    carried = car_ref[p, h] + total[:, 0:1]
                    car_ref[p, h] = carried
                    m = jnp.min(carried)
                    least = m if least is None else jnp.minimum(least, m)
                return least

            mins = _software_pipeline(pairs, [more_scores, weights, values])
            return step + 1, functools.reduce(jnp.minimum, [mins[p] for p in range(pairs)])

        def cond(state):
            step, nxt = state
            return jnp.logical_and(last_start - step * w > 0, nxt < SB_DEAD_LOG2)

        lax.while_loop(cond, body, (jnp.int32(0), live))
        for p in range(pairs):
            for h in range(subs):
                o_ref[0, rows_of(h), lanes_of(p)] = _merge_heads(
                    acc_ref[p, h], lane).astype(o_ref.dtype)


def _sb_attention(q, k, v, weights):
    b, s, wd = q.shape
    pairs = wd // LANES
    subs = SB_STEP_ROWS // SB_Q_TILE
    nq = s // SB_STEP_ROWS
    tile = lambda bi, i: (bi, i, 0)
    whole = lambda bi, i: (bi, 0, 0)
    once = pl.Buffered(1)
    conv_specs = [
        pl.BlockSpec((size, w.shape[1]),
                     lambda bi, i, last=w.shape[0] // size - 1: (jnp.minimum(bi * nq + i, last), 0))
        for w, size in zip(weights, _conversion_blocks(weights, b * nq))]
    return pl.pallas_call(
        functools.partial(_sb_kernel, len(weights)),
        grid=(b, nq),
        in_specs=[
            pl.BlockSpec((1, SB_STEP_ROWS, wd), tile),
            pl.BlockSpec((1, s, wd), whole, pipeline_mode=once),
            pl.BlockSpec((1, s, wd), whole, pipeline_mode=once),
        ] + conv_specs,
        out_specs=[pl.BlockSpec((1, SB_STEP_ROWS, wd), tile)] + conv_specs,
        out_shape=[jax.ShapeDtypeStruct((b, s, wd), BF16)]
        + [jax.ShapeDtypeStruct(w.shape, BF16) for w in weights],
        scratch_shapes=[
            pltpu.VMEM((SB_WINDOW, SB_WINDOW), BF16),
            pltpu.VMEM((subs, 2, 2 * SB_Q_TILE, SB_WINDOW), F32),
            pltpu.VMEM((pairs, subs, 2 * SB_Q_TILE, LANES), F32),
            pltpu.VMEM((pairs, subs, 2 * SB_Q_TILE, LANES), F32),
        ],
        compiler_params=pltpu.CompilerParams(
            dimension_semantics=("arbitrary", "arbitrary"), vmem_limit_bytes=VMEM_LIMIT),
        name="stick_breaking_attention",
    )(q, k, v, *weights)


def _t5_log_bucket_starts():
    exact = REL_BUCKETS // 2
    d = np.arange(exact, 2 * REL_MAX_DIST, dtype=np.float32)
    large = exact + (np.log(d / np.float32(exact)) / np.float32(math.log(REL_MAX_DIST / exact))
                     * np.float32(REL_BUCKETS - exact)).astype(np.int32)
    large = np.minimum(large, REL_BUCKETS - 1)
    assert large[0] == exact and np.all(np.diff(large) >= 0)
    return [int(d[np.argmax(large >= bk)]) for bk in range(exact + 1, REL_BUCKETS)]


def _t5_causal_bucket(distance):
    d = jnp.maximum(distance, 0)
    bucket = jnp.minimum(d, REL_BUCKETS // 2)
    for start in _t5_log_bucket_starts():
        bucket = bucket + jnp.where(d >= start, 1, 0)
    return bucket


def _swa_fill_bias(relb_ref, bias_ref):
    t = ATT_BLOCK
    r = lax.broadcasted_iota(jnp.int32, (t, 2 * t), 0)
    c = lax.broadcasted_iota(jnp.int32, (t, 2 * t), 1)
    dist = r + t - c
    bucket = _t5_causal_bucket(dist)
    valid = jnp.logical_and(dist >= 0, dist < WINDOW)
    for h in range(SWA_HEADS):
        bias = jnp.zeros((t, 2 * t), F32)
        for bk in range(REL_BUCKETS):
            bias = jnp.where(bucket == bk, relb_ref[bk, h], bias)
        bias_ref[h] = jnp.where(valid, bias, MASK_NEG)


def _swa_schedule(sinks_ref, q_ref, kvc_ref, kvp_ref, bias_ref, o_ref, no_previous):
    t = ATT_BLOCK
    kvw = kvc_ref.shape[-1] // 4
    group = SWA_HEADS // SWA_KV_HEADS
    tiles = q_ref.shape[0] // t
    rows_of = lambda n: slice(n * t, (n + 1) * t)
    lane = lax.broadcasted_iota(jnp.int32, (t, LANES), 1)
    row2 = lax.broadcasted_iota(jnp.int32, (2 * t, 1), 0)
    col2 = lax.broadcasted_iota(jnp.int32, (2 * t, 2 * t), 1)
    prev_pen = jnp.where(jnp.logical_and(col2 < t, no_previous), MASK_NEG, 0.0)

    def kv_window(n, variant):
        cols = slice(variant * kvw, (variant + 1) * kvw)
        before = kvc_ref[rows_of(n - 1), cols] if n else kvp_ref[:, cols]
        return jnp.concatenate([before, kvc_ref[rows_of(n), cols]], axis=0)

    items = [(n, half, j) for n in range(tiles) for half in range(2) for j in range(SWA_KV_HEADS)]
    heads_of = lambda half, j: [h for h in range(j * group, (j + 1) * group) if h % 2 == half]

    def probabilities(i):
        n, half, j = items[i]
        heads = heads_of(half, j)
        in_half = (lane >= HEAD_DIM) if half else (lane < HEAD_DIM)
        qq = jnp.concatenate(
            [jnp.where(in_half, q_ref[rows_of(n), (h // 2) * LANES:(h // 2 + 1) * LANES], 0)
             for h in heads], axis=0)
        s = _dot_nt(qq, kv_window(n, 0 + int(j != half)))
        s = s + jnp.concatenate([bias_ref[h] for h in heads], axis=0)
        if n == 0:
            s = s + prev_pen
        sink = jnp.where(row2 < t, sinks_ref[heads[0]], sinks_ref[heads[1]])
        m = jnp.maximum(jnp.max(s, axis=-1, keepdims=True), sink)
        p = jnp.exp(s - m)
        denom = jnp.sum(p, axis=-1, keepdims=True) + jnp.exp(sink - m)
        return p.astype(BF16), 1.0 / denom

    def values(i, pd):
        n, half, j = items[i]
        p, inv_denom = pd
        return _dot(p, kv_window(n, 2 + int(j != half))) * inv_denom

    thunks, outs = _pipeline_schedule(len(items), [probabilities, values])

    def store():
        for i, (n, half, j) in enumerate(items):
            if half == 0:
                even, odd = outs[i], outs[items.index((n, 1, j))]
                for e, h in enumerate(heads_of(0, j)):
                    o_ref[rows_of(n), (h // 2) * LANES:(h // 2 + 1) * LANES] = jnp.where(
                        lane < HEAD_DIM, even[e * t:(e + 1) * t], odd[e * t:(e + 1) * t]
                    ).astype(o_ref.dtype)

    return thunks + [store]


def _tail_schedule(ha_ref, sb_ref, sw_ref, sbg_ref, swg_ref, wo_ref, l1g_ref, l1b_ref,
                   wgu_ref, wd_ref, l2g_ref, l2b_ref, o_ref):
    d = ha_ref.shape[1]
    d_ff = wd_ref.shape[0]
    mw = MXU_WIDTH
    bounds = list(range(0, d_ff, FFN_CHUNK)) + [d_ff]
    n_chunks = len(bounds) - 1
    out_tiles = range(d // mw)
    st = {}

    def merged():
        sb = _rms_norm(sb_ref[...].astype(F32), sbg_ref[...]).astype(BF16)
        sw = _rms_norm(sw_ref[...].astype(F32), swg_ref[...]).astype(BF16)
        st["merged"] = jnp.concatenate([sb, sw], axis=1)

    def out_projection(n):
        st["mix", n] = _dot(st["merged"], wo_ref[:, n * mw:(n + 1) * mw])

    def first_norm():
        mix = jnp.concatenate([st.pop(("mix", n)) for n in out_tiles], axis=1)
        st.pop("merged")
        h1 = _layer_norm(ha_ref[...] + mix, l1g_ref[...], l1b_ref[...])
        st["h1b"] = h1.astype(BF16)
        for n in out_tiles:
            st["acc", n] = ALPHA * h1[:, n * mw:(n + 1) * mw]

    def activation(c, n):
        lo = bounds[c] + n * mw
        gate = _dot(st["h1b"], wgu_ref[:, lo:lo + mw])
        up = _dot(st["h1b"], wgu_ref[:, d_ff + lo:d_ff + lo + mw])
        st["act", c, n] = (gate / (1.0 + jnp.exp(-gate)) * up).astype(BF16)

    def down(c, n):
        tiles = range((bounds[c + 1] - bounds[c]) // mw)
        if n == 0:
            st["act", c] = jnp.concatenate([st.pop(("act", c, k)) for k in tiles], axis=1)
        st["acc", n] = st["acc", n] + _dot(st["act", c], wd_ref[bounds[c]:bounds[c + 1],
                                                               n * mw:(n + 1) * mw])

    def second_norm():
        y = jnp.concatenate([st.pop(("acc", n)) for n in out_tiles], axis=1)
        o_ref[...] = _layer_norm(y, l2g_ref[...], l2b_ref[...])

    act_thunks = [[functools.partial(activation, c, n)
                   for n in range((bounds[c + 1] - bounds[c]) // mw)] for c in range(n_chunks)]
    down_thunks = [[functools.partial(down, c, n) for n in out_tiles] for c in range(n_chunks)]
    ffn = list(act_thunks[0])
    for c in range(1, n_chunks):
        ffn += _interleave(act_thunks[c], down_thunks[c - 1])
    ffn += down_thunks[-1]
    return (merged, [functools.partial(out_projection, n) for n in out_tiles], first_norm,
            ffn, second_norm)


def _attn_tail_kernel(tiles_per_seq, last_tile, sinks_ref, relb_ref, q_ref, kvc_ref, kvp_ref,
                      ha_ref, sb_ref, sbg_ref, swg_ref, wo_ref, l1g_ref, l1b_ref,
                      wgu_ref, wd_ref, l2g_ref, l2b_ref, o_ref, bias_ref, sw_ref):
    t = pl.program_id(0)

    @pl.when(t == 0)
    def _first_step():
        _swa_fill_bias(relb_ref, bias_ref)
        sw_ref[1] = jnp.zeros(sw_ref.shape[1:], sw_ref.dtype)

    no_previous = (jnp.minimum(t, last_tile) % tiles_per_seq) == 0
    attention = _swa_schedule(sinks_ref, q_ref.at[0], kvc_ref.at[0], kvp_ref.at[0], bias_ref,
                              sw_ref.at[t % 2], no_previous)
    merged, out_proj, first_norm, ffn, second_norm = _tail_schedule(
        ha_ref, sb_ref, sw_ref.at[(t + 1) % 2], sbg_ref, swg_ref, wo_ref,
        l1g_ref, l1b_ref, wgu_ref, wd_ref, l2g_ref, l2b_ref, o_ref)
    tail = [merged] + out_proj + [first_norm] + ffn + [second_norm]
    for thunk in _interleave(tail, attention):
        thunk()


def _attn_tail_block(q_sw, kv_sw, sinks, rel_bias, ha, sb2, sb_g, sw_g, w_out_b, ln1_g, ln1_b,
                     w_gate_up_b, w_down_b, ln2_g, ln2_b):
    b, s, wd = q_sw.shape
    n, d = ha.shape
    tiles_per_seq = s // TAIL_ROWS
    last_tile = n // TAIL_ROWS - 1
    key_tiles = TAIL_ROWS // ATT_BLOCK
    att = lambda t: jnp.minimum(t, last_tile)
    cur = lambda t: (att(t) // tiles_per_seq, att(t) % tiles_per_seq, 0)
    prev = lambda t: (att(t) // tiles_per_seq,
                      jnp.maximum(att(t) % tiles_per_seq * key_tiles - 1, 0), 0)
    row = lambda t: (jnp.maximum(t - 1, 0), 0)
    fixed = lambda t: (0, 0)
    smem = pl.BlockSpec(memory_space=pltpu.SMEM)
    vec = lambda a: pl.BlockSpec((1, a.shape[-1]), fixed)
    weight = lambda a: pl.BlockSpec(a.shape, fixed, pipeline_mode=pl.Buffered(1))
    return pl.pallas_call(
        functools.partial(_attn_tail_kernel, tiles_per_seq, last_tile),
        grid=(last_tile + 2,),
        in_specs=[
            smem, smem,
            pl.BlockSpec((1, TAIL_ROWS, wd), cur),
            pl.BlockSpec((1, TAIL_ROWS, kv_sw.shape[-1]), cur),
            pl.BlockSpec((1, ATT_BLOCK, kv_sw.shape[-1]), prev),
            pl.BlockSpec((TAIL_ROWS, d), row),
            pl.BlockSpec((TAIL_ROWS, sb2.shape[-1]), row),
            vec(sb_g), vec(sw_g), weight(w_out_b), vec(ln1_g), vec(ln1_b),
            weight(w_gate_up_b), weight(w_down_b), vec(ln2_g), vec(ln2_b),
        ],
        out_specs=pl.BlockSpec((TAIL_ROWS, d), row),
        out_shape=jax.ShapeDtypeStruct((n, d), F32),
        scratch_shapes=[
            pltpu.VMEM((SWA_HEADS, ATT_BLOCK, 2 * ATT_BLOCK), F32),
            pltpu.VMEM((2, TAIL_ROWS, wd), BF16),
        ],
        compiler_params=pltpu.CompilerParams(
            dimension_semantics=("arbitrary",), vmem_limit_bytes=VMEM_LIMIT),
        name="swa_attention_and_dense_tail",
    )(sinks, rel_bias, q_sw, kv_sw, kv_sw, ha, sb2, sb_g, sw_g, w_out_b, ln1_g, ln1_b,
      w_gate_up_b, w_down_b, ln2_g, ln2_b)


SB_GROUP = 2
SB_LIVE_DONE = 1e30


def _sb_fast_masks(mask_ref):
    tq, w = SB_Q_TILE, SB_WINDOW
    row = lax.broadcasted_iota(jnp.int32, (2 * tq, w), 0) & (tq - 1)
    col = lax.broadcasted_iota(jnp.int32, (2 * tq, w), 1)
    causal = col < row + (w - tq)
    mask_ref[0, 0], mask_ref[0, 1] = _sb_masks(causal)
    for h in range(mask_ref.shape[0] - 1):
        first_key = w - (h + 1) * tq
        mask_ref[1 + h, 0], mask_ref[1 + h, 1] = _sb_masks(
            jnp.logical_and(causal, col >= first_key))


def _sb_fast_schedule(q_ref, kc_ref, kp_ref, vc_ref, vp_ref, tri_ref, mask_ref, o_ref,
                      no_previous, lives):
    tq, w = SB_Q_TILE, SB_WINDOW
    subs = q_ref.shape[0] // tq
    pairs = q_ref.shape[-1] // LANES
    clamped = w // tq - 1
    assert mask_ref.shape[0] == clamped + 1 and subs % SB_GROUP == 0
    lane = lax.broadcasted_iota(jnp.int32, (tq, LANES), 1)
    lanes_of = lambda p: slice(p * LANES, (p + 1) * LANES)
    rows_of = lambda h: slice(h * tq, (h + 1) * tq)

    def window(h, cur_ref, prev_ref, p):
        start = (h + 1) * tq - w
        if start >= 0:
            return cur_ref[start:start + w, lanes_of(p)]
        return jnp.concatenate([prev_ref[w + start:w, lanes_of(p)],
                                cur_ref[0:w + start, lanes_of(p)]], axis=0)

    def mask(h, which):
        if h >= clamped:
            return mask_ref[0, which]
        return mask_ref[jnp.where(no_previous, 1 + h, 0), which]

    items = [(p, g) for p in range(pairs) for g in range(subs // SB_GROUP)]
    group = lambda g: range(g * SB_GROUP, (g + 1) * SB_GROUP)

    def scores(i):
        p, g = items[i]
        return [_sb_scores(_split_heads(q_ref[rows_of(h), lanes_of(p)], lane),
                           window(h, kc_ref, kp_ref, p), mask(h, 0)) for h in group(g)]

    def weights(i, sc):
        return _sb_weights(sc, tri_ref[...], [mask(h, 1) for h in group(items[i][1])],
                           [None] * SB_GROUP)

    def values(i, aws):
        p, g = items[i]
        for h, (a, total) in zip(group(g), aws):
            o2 = _dot(a, window(h, vc_ref, vp_ref, p))
            o_ref[rows_of(h), lanes_of(p)] = _merge_heads(o2, lane).astype(o_ref.dtype)
            m = jnp.min(total[:, 0:1])
            lives.append(m if h > clamped else jnp.where(no_previous, SB_LIVE_DONE, m))

    return _pipeline_schedule(len(items), [scores, weights, values])[0]


def _sb_walk_back(q_ref, k_hbm, v_hbm, batch, row0, tri_ref, o_ref, kwin_ref, vwin_ref,
                  acc_ref, car_ref, sems):
    tq, w = SB_Q_TILE, SB_WINDOW
    subs = q_ref.shape[0] // tq
    pairs = q_ref.shape[-1] // LANES
    lane = lax.broadcasted_iota(jnp.int32, (tq, LANES), 1)
    lanes_of = lambda p: slice(p * LANES, (p + 1) * LANES)
    row = lax.broadcasted_iota(jnp.int32, (2 * tq, w), 0) & (tq - 1)
    col = lax.broadcasted_iota(jnp.int32, (2 * tq, w), 1)

    def sub_tile(h, carry):
        rows = pl.ds(pl.multiple_of(h * tq, tq), tq)
        first_row = row0 + h * tq
        acc_ref[...] = jnp.zeros(acc_ref.shape, F32)
        car_ref[...] = jnp.zeros(car_ref.shape, F32)

        def window(state):
            done, _ = state
            start = pl.multiple_of(jnp.maximum(done - w, 0), tq)
            copies = [pltpu.make_async_copy(src.at[batch, pl.ds(start, w), :], dst, sems.at[i])
                      for i, (src, dst) in enumerate(((k_hbm, kwin_ref), (v_hbm, vwin_ref)))]
            for copy in copies:
                copy.start()
            for copy in copies:
                copy.wait()
            key = start + col
            keep, kill = _sb_masks(jnp.logical_and(key < done, key < first_row + row))
            least = None
            for p in range(pairs):
                qq = _split_heads(q_ref[rows, lanes_of(p)], lane)
                sc = _sb_scores(qq, kwin_ref[:, lanes_of(p)], keep)
                ((a, total),) = _sb_weights([sc], tri_ref[...], [kill], [car_ref[p]])
                acc_ref[p] += _dot(a, vwin_ref[:, lanes_of(p)])
                carried = car_ref[p] + total[:, 0:1]
                car_ref[p] = carried
                m = jnp.min(carried)
                least = m if least is None else jnp.minimum(least, m)
            return start, least

        lax.while_loop(lambda state: jnp.logical_and(state[0] > 0, state[1] < SB_DEAD_LOG2),
                       window, (first_row + tq, jnp.float32(0.0)))
        for p in range(pairs):
            o_ref[rows, lanes_of(p)] = _merge_heads(acc_ref[p], lane).astype(o_ref.dtype)
        return carry

    lax.fori_loop(0, subs, sub_tile, 0)


def _fast_kernel(tiles_per_seq, last_tile, sinks_ref, relb_ref, qsw_ref, kvc_ref, kvp_ref,
                 qsb_ref, kc_ref, kp_ref, vc_ref, vp_ref,
                 ha_ref, sbg_ref, swg_ref, wo_ref, l1g_ref, l1b_ref,
                 wgu_ref, wd_ref, l2g_ref, l2b_ref, k_hbm, v_hbm, o_ref,
                 bias_ref, sw_ref, sb_ref, tri_ref, mask_ref,
                 kwin_ref, vwin_ref, acc_ref, car_ref, sems):
    t = pl.program_id(0)

    @pl.when(t == 0)
    def _first_step():
        _swa_fill_bias(relb_ref, bias_ref)
        _sb_fast_masks(mask_ref)
        j = lax.broadcasted_iota(jnp.int32, tri_ref.shape, 0)
        s = lax.broadcasted_iota(jnp.int32, tri_ref.shape, 1)
        tri_ref[...] = jnp.where(j > s, 1.0, 0.0).astype(BF16)
        sw_ref[1] = jnp.zeros(sw_ref.shape[1:], sw_ref.dtype)
        sb_ref[1] = jnp.zeros(sb_ref.shape[1:], sb_ref.dtype)

    no_previous = (jnp.minimum(t, last_tile) % tiles_per_seq) == 0
    lives = []
    swa = _swa_schedule(sinks_ref, qsw_ref.at[0], kvc_ref.at[0], kvp_ref.at[0], bias_ref,
                        sw_ref.at[t % 2], no_previous)
    sb = _sb_fast_schedule(qsb_ref.at[0], kc_ref.at[0], kp_ref.at[0], vc_ref.at[0], vp_ref.at[0],
                           tri_ref, mask_ref, sb_ref.at[t % 2], no_previous, lives)
    merged, out_proj, first_norm, ffn, second_norm = _tail_schedule(
        ha_ref, sb_ref.at[(t + 1) % 2], sw_ref.at[(t + 1) % 2], sbg_ref, swg_ref, wo_ref,
        l1g_ref, l1b_ref, wgu_ref, wd_ref, l2g_ref, l2b_ref, o_ref)
    tail = [merged] + out_proj + [first_norm] + ffn + [second_norm]
    for thunk in _interleave(tail, _interleave(swa, sb)):
        thunk()

    @pl.when(functools.reduce(jnp.minimum, lives) < SB_DEAD_LOG2)
    def _redo_with_walk_back():
        tile = jnp.minimum(t, last_tile)
        _sb_walk_back(qsb_ref.at[0], k_hbm, v_hbm, tile // tiles_per_seq,
                      tile % tiles_per_seq * qsb_ref.shape[1], tri_ref, sb_ref.at[t % 2],
                      kwin_ref, vwin_ref, acc_ref, car_ref, sems)


def _fast_block(q_sw, kv_sw, q_sb, k_sb, v_sb, sinks, rel_bias, ha, sb_g, sw_g, w_out_b,
                ln1_g, ln1_b, w_gate_up_b, w_down_b, ln2_g, ln2_b):
    b, s, wd = q_sw.shape
    n, d = ha.shape
    tiles_per_seq = s // TAIL_ROWS
    last_tile = n // TAIL_ROWS - 1
    att = lambda t: jnp.minimum(t, last_tile)
    cur = lambda t: (att(t) // tiles_per_seq, att(t) % tiles_per_seq, 0)

    def prev(rows):
        per_tile = TAIL_ROWS // rows
        return lambda t: (att(t) // tiles_per_seq,
                          jnp.maximum(att(t) % tiles_per_seq * per_tile - 1, 0), 0)

    row = lambda t: (jnp.maximum(t - 1, 0), 0)
    fixed = lambda t: (0, 0)
    smem = pl.BlockSpec(memory_space=pltpu.SMEM)
    vec = lambda a: pl.BlockSpec((1, a.shape[-1]), fixed)
    weight = lambda a: pl.BlockSpec(a.shape, fixed, pipeline_mode=pl.Buffered(1))
    tile3 = lambda a: pl.BlockSpec((1, TAIL_ROWS, a.shape[-1]), cur)
    before = lambda a, rows: pl.BlockSpec((1, rows, a.shape[-1]), prev(rows))
    steps = last_tile + 2
    return pl.pallas_call(
        functools.partial(_fast_kernel, tiles_per_seq, last_tile),
        grid=(steps,),
        in_specs=[
            smem, smem,
            tile3(q_sw), tile3(kv_sw), before(kv_sw, ATT_BLOCK),
            tile3(q_sb), tile3(k_sb), before(k_sb, SB_WINDOW), tile3(v_sb), before(v_sb, SB_WINDOW),
            pl.BlockSpec((TAIL_ROWS, d), row),
            vec(sb_g), vec(sw_g), weight(w_out_b), vec(ln1_g), vec(ln1_b),
            weight(w_gate_up_b), weight(w_down_b), vec(ln2_g), vec(ln2_b),
            pl.BlockSpec(memory_space=pl.ANY), pl.BlockSpec(memory_space=pl.ANY),
        ],
        out_specs=pl.BlockSpec((TAIL_ROWS, d), row),
        out_shape=jax.ShapeDtypeStruct((n, d), F32),
        scratch_shapes=[
            pltpu.VMEM((SWA_HEADS, ATT_BLOCK, 2 * ATT_BLOCK), F32),
            pltpu.VMEM((2, TAIL_ROWS, wd), BF16),
            pltpu.VMEM((2, TAIL_ROWS, q_sb.shape[-1]), BF16),
            pltpu.VMEM((SB_WINDOW, SB_WINDOW), BF16),
            pltpu.VMEM((SB_WINDOW // SB_Q_TILE, 2, 2 * SB_Q_TILE, SB_WINDOW), F32),
            pltpu.VMEM((SB_WINDOW, k_sb.shape[-1]), BF16),
            pltpu.VMEM((SB_WINDOW, v_sb.shape[-1]), BF16),
            pltpu.VMEM((q_sb.shape[-1] // LANES, 2 * SB_Q_TILE, LANES), F32),
            pltpu.VMEM((q_sb.shape[-1] // LANES, 2 * SB_Q_TILE, LANES), F32),
            pltpu.SemaphoreType.DMA((2,)),
        ],
        compiler_params=pltpu.CompilerParams(
            dimension_semantics=("arbitrary",), vmem_limit_bytes=VMEM_LIMIT),
        name="attention_and_dense_tail",
    )(sinks, rel_bias, q_sw, kv_sw, kv_sw, q_sb, k_sb, k_sb, v_sb, v_sb, ha, sb_g, sw_g, w_out_b,
      ln1_g, ln1_b, w_gate_up_b, w_down_b, ln2_g, ln2_b, k_sb, v_sb)


def kernel(x, ln_in_g, ln_in_b, w_in, sb_norm_g, swa_norm_g, sinks, rel_bias, w_out,
           ln1_g, ln1_b, w_gate_up, w_down, ln2_g, ln2_b):
    b, s, d = x.shape
    assert w_in.shape[0] == DEPTH == 1
    sb_w = SB_HEADS * HEAD_DIM
    sw_w = SWA_HEADS * HEAD_DIM
    kv_w = SWA_KV_HEADS * HEAD_DIM
    assert w_in.shape[-1] == 3 * sb_w + sw_w + 2 * kv_w
    assert kv_w == LANES and SWA_HEADS // SWA_KV_HEADS == 4
    assert s % SB_WINDOW == 0 and s % SB_STEP_ROWS == 0 and s % TAIL_ROWS == 0
    assert (b * s) % ROW_TILE == 0 and WINDOW <= ATT_BLOCK and d % MXU_WIDTH == 0
    assert FFN_CHUNK % MXU_WIDTH == 0 and w_down.shape[1] % MXU_WIDTH == 0

    x2 = x.reshape(b * s, d)
    row_vec = lambda a: a.reshape(1, -1)

    names = ("q_sb", "k_sb", "v_sb", "q_sw", "k_sw", "v_sw")
    widths = (sb_w, sb_w, sb_w, sw_w, kv_w, kv_w)
    splits, lo = {}, 0
    for name, wd in zip(names, widths):
        splits[name] = (lo, lo + wd)
        lo += wd
    ha, q_sb, k_sb, v_sb, q_sw, kv_sw, w_out_b, w_gate_up_b, w_down_b = _in_projection(
        x2, row_vec(ln_in_g), row_vec(ln_in_b), w_in[0], splits,
        [w_out[0], w_gate_up[0], w_down[0]])

    to3 = lambda a: a.reshape(b, s, a.shape[-1])
    q_sb, k_sb, v_sb, q_sw, kv_sw = map(to3, (q_sb, k_sb, v_sb, q_sw, kv_sw))
    out = _fast_block(q_sw, kv_sw, q_sb, k_sb, v_sb, sinks[0], rel_bias, ha,
                      row_vec(sb_norm_g[0]), row_vec(swa_norm_g[0]), w_out_b,
                      row_vec(ln1_g[0]), row_vec(ln1_b[0]), w_gate_up_b, w_down_b,
                      row_vec(ln2_g[0]), row_vec(ln2_b[0]))
    return out.reshape(b, s, d)
```

```python
import functools
import math

import jax
import jax.numpy as jnp
import numpy as np
from jax import lax
from jax.experimental import pallas as pl
from jax.experimental.pallas import tpu as pltpu

F32 = jnp.float32
BF16 = jnp.bfloat16

HEAD_DIM = 64
SB_HEADS = 8
SWA_HEADS = 8
SWA_KV_HEADS = 2
WINDOW = 128
REL_BUCKETS = 32
REL_MAX_DIST = 128
LN_EPS = 1e-5
RMS_EPS = 1e-6
DEPTH = 1
ALPHA = (2 * DEPTH) ** 0.25
QK_SCALE = HEAD_DIM ** -0.5
LOG2E = math.log2(math.e)

LANES = 128
BF16_ROWS = 16
MXU_WIDTH = 256
ATT_BLOCK = 128
SB_WINDOW = 256
SB_Q_TILE = 64
ROW_TILE = 1024
PROJ_PARTS = 2
TAIL_ROWS = 512
FFN_CHUNK = 4 * MXU_WIDTH
VMEM_LIMIT = 56 * 1024 * 1024

SB_DEAD_LOG2 = 127.0
MASK_NEG = -1e30


def _layer_norm(x, g, b):
    mu = jnp.mean(x, axis=-1, keepdims=True)
    xc = x - mu
    var = jnp.mean(xc * xc, axis=-1, keepdims=True)
    return xc * lax.rsqrt(var + LN_EPS) * g + b


def _rms_norm(x, g):
    ms = jnp.mean(x * x, axis=-1, keepdims=True)
    return x * lax.rsqrt(ms + RMS_EPS) * g


def _dot(a, b):
    return jnp.dot(a, b, preferred_element_type=F32)


def _dot_nt(a, b):
    return lax.dot_general(a, b, (((1,), (1,)), ((), ())), preferred_element_type=F32)


def _pipeline_schedule(n, stages):
    vals = [dict() for _ in stages]
    thunks = []
    for step in range(n + len(stages) - 1):
        for s, stage in enumerate(stages):
            i = step - s
            if 0 <= i < n:
                def thunk(s=s, i=i, stage=stage):
                    vals[s][i] = stage(i, vals[s - 1].pop(i)) if s else stage(i)
                thunks.append(thunk)
    return thunks, vals[-1]


def _software_pipeline(n, stages):
    thunks, results = _pipeline_schedule(n, stages)
    for thunk in thunks:
        thunk()
    return results


def _interleave(xs, ys):
    merged, taken = [], 0
    for i, x in enumerate(xs):
        merged.append(x)
        upto = (i + 1) * len(ys) // len(xs)
        merged.extend(ys[taken:upto])
        taken = upto
    return merged


def _inproj_kernel(splits, n_later, x_ref, g_ref, b_ref, w32_ref, *refs):
    later_in, refs = refs[:n_later], refs[n_later:]
    ha_ref, qsb_ref, ksb_ref, vsb_ref, qsw_ref, kv_ref = refs[:6]
    later_out, w_ref = refs[6:6 + n_later], refs[6 + n_later]
    rows = x_ref.shape[0] // PROJ_PARTS
    part = lambda i: slice(i * rows, (i + 1) * rows)
    assert splits["k_sw"][1] == splits["v_sw"][0]

    @pl.when(pl.program_id(0) == 0)
    def _convert_weight():
        w_ref[...] = w32_ref[...].astype(BF16)

    for src, dst in zip(later_in, later_out):
        dst[...] = src[...].astype(BF16)

    def normed(i):
        h32 = _layer_norm(x_ref[part(i), :], g_ref[...], b_ref[...])
        ha_ref[part(i), :] = ALPHA * h32
        return h32.astype(BF16)

    def projected(i, h):
        r = part(i)
        proj = lambda lo, hi: _dot(h, w_ref[:, lo:hi])
        qsb_ref[r, :] = (proj(*splits["q_sb"]) * QK_SCALE).astype(BF16)
        ksb_ref[r, :] = proj(*splits["k_sb"]).astype(BF16)
        vsb_ref[r, :] = proj(*splits["v_sb"]).astype(BF16)
        qsw_ref[r, :] = (proj(*splits["q_sw"]) * QK_SCALE).astype(BF16)
        kv = proj(splits["k_sw"][0], splits["v_sw"][1])
        kvw = kv.shape[-1] // 2
        k, v = kv[:, :kvw], kv[:, kvw:]
        kv_ref[r, 0 * kvw:1 * kvw] = k.astype(BF16)
        kv_ref[r, 1 * kvw:2 * kvw] = pltpu.roll(k, HEAD_DIM, axis=1).astype(BF16)
        kv_ref[r, 2 * kvw:3 * kvw] = v.astype(BF16)
        kv_ref[r, 3 * kvw:4 * kvw] = pltpu.roll(v, HEAD_DIM, axis=1).astype(BF16)

    _software_pipeline(PROJ_PARTS, [normed, projected])


def _in_projection(x2, g, b, w_in, splits, later_weights):
    n, d = x2.shape
    steps = n // ROW_TILE
    width = lambda name: splits[name][1] - splits[name][0]
    out_w = [width("q_sb"), width("k_sb"), width("v_sb"), width("q_sw"), 4 * width("k_sw")]
    row = lambda i: (i, 0)
    fixed = lambda i: (0, 0)
    later_specs = [
        pl.BlockSpec((size, w.shape[1]),
                     lambda i, last=w.shape[0] // size - 1: (jnp.minimum(i, last), 0))
        for w, size in zip(later_weights, _conversion_blocks(later_weights, steps))]
    return pl.pallas_call(
        functools.partial(_inproj_kernel, splits, len(later_weights)),
        grid=(steps,),
        in_specs=[
            pl.BlockSpec((ROW_TILE, d), row),
            pl.BlockSpec((1, d), fixed),
            pl.BlockSpec((1, d), fixed),
            pl.BlockSpec(w_in.shape, fixed, pipeline_mode=pl.Buffered(1)),
        ] + later_specs,
        out_specs=[pl.BlockSpec((ROW_TILE, d), row)]
        + [pl.BlockSpec((ROW_TILE, w), row) for w in out_w] + later_specs,
        out_shape=[jax.ShapeDtypeStruct((n, d), F32)]
        + [jax.ShapeDtypeStruct((n, w), BF16) for w in out_w]
        + [jax.ShapeDtypeStruct(w.shape, BF16) for w in later_weights],
        scratch_shapes=[pltpu.VMEM(w_in.shape, BF16)],
        compiler_params=pltpu.CompilerParams(
            dimension_semantics=("arbitrary",), vmem_limit_bytes=VMEM_LIMIT),
        name="ln_in_proj",
    )(x2, g, b, w_in, *later_weights)


def _conversion_blocks(weights, steps):
    sizes = []
    for w in weights:
        rows = w.shape[0]
        sizes.append(next(r for r in range(BF16_ROWS, rows + 1, BF16_ROWS)
                          if rows % r == 0 and rows // r <= steps))
    return sizes


def _split_heads(x, lane):
    zero = jnp.zeros_like(x)
    return jnp.concatenate([jnp.where(lane < HEAD_DIM, x, zero),
                            jnp.where(lane >= HEAD_DIM, x, zero)], axis=0)


def _merge_heads(o2, lane):
    t = o2.shape[0] // 2
    return jnp.where(lane < HEAD_DIM, o2[:t], o2[t:])


def _sb_masks(valid):
    return jnp.where(valid, LOG2E, 0.0), jnp.where(valid, 0.0, MASK_NEG)


def _sb_scores(qq, k, keep):
    z = _dot_nt(qq, k)
    zl = z * LOG2E
    c_nat = jnp.maximum(z, 0.0) + jnp.log(1.0 + jnp.exp2(-jnp.abs(zl)))
    cm = c_nat * keep
    return zl - cm, cm.astype(BF16), cm[:, :LANES]


def _sb_weights(scores, tri, kills, carrieds):
    sums = _dot(jnp.concatenate([sc[1] for sc in scores], axis=0), tri)
    rows = scores[0][0].shape[0]
    out = []
    for n, ((log_beta, _, cm0), kill, carried) in enumerate(zip(scores, kills, carrieds)):
        later = sums[n * rows:(n + 1) * rows]
        log_a = (log_beta - later) + kill
        if carried is not None:
            log_a = log_a - jnp.concatenate([carried] * (log_a.shape[1] // LANES), axis=1)
        out.append((jnp.exp2(log_a).astype(BF16), later[:, :LANES] + cm0))
    return out


def _t5_log_bucket_starts():
    exact = REL_BUCKETS // 2
    d = np.arange(exact, 2 * REL_MAX_DIST, dtype=np.float32)
    large = exact + (np.log(d / np.float32(exact)) / np.float32(math.log(REL_MAX_DIST / exact))
                     * np.float32(REL_BUCKETS - exact)).astype(np.int32)
    large = np.minimum(large, REL_BUCKETS - 1)
    assert large[0] == exact and np.all(np.diff(large) >= 0)
    return [int(d[np.argmax(large >= bk)]) for bk in range(exact + 1, REL_BUCKETS)]


def _t5_causal_bucket(distance):
    d = jnp.maximum(distance, 0)
    bucket = jnp.minimum(d, REL_BUCKETS // 2)
    for start in _t5_log_bucket_starts():
        bucket = bucket + jnp.where(d >= start, 1, 0)
    return bucket


def _swa_fill_bias(relb_ref, bias_ref):
    t = ATT_BLOCK
    r = lax.broadcasted_iota(jnp.int32, (t, 2 * t), 0)
    c = lax.broadcasted_iota(jnp.int32, (t, 2 * t), 1)
    dist = r + t - c
    bucket = _t5_causal_bucket(dist)
    valid = jnp.logical_and(dist >= 0, dist < WINDOW)
    for h in range(SWA_HEADS):
        bias = jnp.zeros((t, 2 * t), F32)
        for bk in range(REL_BUCKETS):
            bias = jnp.where(bucket == bk, relb_ref[bk, h], bias)
        bias_ref[h] = jnp.where(valid, bias, MASK_NEG)


def _swa_schedule(sinks_ref, q_ref, kvc_ref, kvp_ref, bias_ref, o_ref, no_previous):
    t = ATT_BLOCK
    kvw = kvc_ref.shape[-1] // 4
    group = SWA_HEADS // SWA_KV_HEADS
    tiles = q_ref.shape[0] // t
    rows_of = lambda n: slice(n * t, (n + 1) * t)
    lane = lax.broadcasted_iota(jnp.int32, (t, LANES), 1)
    row2 = lax.broadcasted_iota(jnp.int32, (2 * t, 1), 0)
    col2 = lax.broadcasted_iota(jnp.int32, (2 * t, 2 * t), 1)
    prev_pen = jnp.where(jnp.logical_and(col2 < t, no_previous), MASK_NEG, 0.0)

    def kv_window(n, variant):
        cols = slice(variant * kvw, (variant + 1) * kvw)
        before = kvc_ref[rows_of(n - 1), cols] if n else kvp_ref[:, cols]
        return jnp.concatenate([before, kvc_ref[rows_of(n), cols]], axis=0)

    items = [(n, half, j) for n in range(tiles) for half in range(2) for j in range(SWA_KV_HEADS)]
    heads_of = lambda half, j: [h for h in range(j * group, (j + 1) * group) if h % 2 == half]

    def probabilities(i):
        n, half, j = items[i]
        heads = heads_of(half, j)
        in_half = (lane >= HEAD_DIM) if half else (lane < HEAD_DIM)
        qq = jnp.concatenate(
            [jnp.where(in_half, q_ref[rows_of(n), (h // 2) * LANES:(h // 2 + 1) * LANES], 0)
             for h in heads], axis=0)
        s = _dot_nt(qq, kv_window(n, 0 + int(j != half)))
        s = s + jnp.concatenate([bias_ref[h] for h in heads], axis=0)
        if n == 0:
            s = s + prev_pen
        sink = jnp.where(row2 < t, sinks_ref[heads[0]], sinks_ref[heads[1]])
        m = jnp.maximum(jnp.max(s, axis=-1, keepdims=True), sink)
        p = jnp.exp(s - m)
        denom = jnp.sum(p, axis=-1, keepdims=True) + jnp.exp(sink - m)
        return p.astype(BF16), 1.0 / denom

    def values(i, pd):
        n, half, j = items[i]
        p, inv_denom = pd
        return _dot(p, kv_window(n, 2 + int(j != half))) * inv_denom

    thunks, outs = _pipeline_schedule(len(items), [probabilities, values])

    def store():
        for i, (n, half, j) in enumerate(items):
            if half == 0:
                even, odd = outs[i], outs[items.index((n, 1, j))]
                for e, h in enumerate(heads_of(0, j)):
                    o_ref[rows_of(n), (h // 2) * LANES:(h // 2 + 1) * LANES] = jnp.where(
                        lane < HEAD_DIM, even[e * t:(e + 1) * t], odd[e * t:(e + 1) * t]
                    ).astype(o_ref.dtype)

    return thunks + [store]


def _tail_schedule(ha_ref, sb_ref, sw_ref, sbg_ref, swg_ref, wo_ref, l1g_ref, l1b_ref,
                   wgu_ref, wd_ref, l2g_ref, l2b_ref, o_ref):
    d = ha_ref.shape[1]
    d_ff = wd_ref.shape[0]
    mw = MXU_WIDTH
    bounds = list(range(0, d_ff, FFN_CHUNK)) + [d_ff]
    n_chunks = len(bounds) - 1
    out_tiles = range(d // mw)
    st = {}

    def merged():
        sb = _rms_norm(sb_ref[...].astype(F32), sbg_ref[...]).astype(BF16)
        sw = _rms_norm(sw_ref[...].astype(F32), swg_ref[...]).astype(BF16)
        st["merged"] = jnp.concatenate([sb, sw], axis=1)

    def out_projection(n):
        st["mix", n] = _dot(st["merged"], wo_ref[:, n * mw:(n + 1) * mw])

    def first_norm():
        mix = jnp.concatenate([st.pop(("mix", n)) for n in out_tiles], axis=1)
        st.pop("merged")
        h1 = _layer_norm(ha_ref[...] + mix, l1g_ref[...], l1b_ref[...])
        st["h1b"] = h1.astype(BF16)
        for n in out_tiles:
            st["acc", n] = ALPHA * h1[:, n * mw:(n + 1) * mw]

    def activation(c, n):
        lo = bounds[c] + n * mw
        gate = _dot(st["h1b"], wgu_ref[:, lo:lo + mw])
        up = _dot(st["h1b"], wgu_ref[:, d_ff + lo:d_ff + lo + mw])
        st["act", c, n] = (gate / (1.0 + jnp.exp(-gate)) * up).astype(BF16)

    def down(c, n):
        tiles = range((bounds[c + 1] - bounds[c]) // mw)
        if n == 0:
            st["act", c] = jnp.concatenate([st.pop(("act", c, k)) for k in tiles], axis=1)
        st["acc", n] = st["acc", n] + _dot(st["act", c], wd_ref[bounds[c]:bounds[c + 1],
                                                               n * mw:(n + 1) * mw])

    def second_norm():
        y = jnp.concatenate([st.pop(("acc", n)) for n in out_tiles], axis=1)
        o_ref[...] = _layer_norm(y, l2g_ref[...], l2b_ref[...])

    act_thunks = [[functools.partial(activation, c, n)
                   for n in range((bounds[c + 1] - bounds[c]) // mw)] for c in range(n_chunks)]
    down_thunks = [[functools.partial(down, c, n) for n in out_tiles] for c in range(n_chunks)]
    ffn = list(act_thunks[0])
    for c in range(1, n_chunks):
        ffn += _interleave(act_thunks[c], down_thunks[c - 1])
    ffn += down_thunks[-1]
    return (merged, [functools.partial(out_projection, n) for n in out_tiles], first_norm,
            ffn, second_norm)


SB_GROUP = 2
SB_LIVE_DONE = 1e30


def _sb_fast_masks(mask_ref):
    tq, w = SB_Q_TILE, SB_WINDOW
    row = lax.broadcasted_iota(jnp.int32, (2 * tq, w), 0) & (tq - 1)
    col = lax.broadcasted_iota(jnp.int32, (2 * tq, w), 1)
    causal = col < row + (w - tq)
    mask_ref[0, 0], mask_ref[0, 1] = _sb_masks(causal)
    for h in range(mask_ref.shape[0] - 1):
        first_key = w - (h + 1) * tq
        mask_ref[1 + h, 0], mask_ref[1 + h, 1] = _sb_masks(
            jnp.logical_and(causal, col >= first_key))


def _sb_fast_schedule(q_ref, kc_ref, kp_ref, vc_ref, vp_ref, tri_ref, mask_ref, o_ref,
                      no_previous, lives):
    tq, w = SB_Q_TILE, SB_WINDOW
    subs = q_ref.shape[0] // tq
    pairs = q_ref.shape[-1] // LANES
    clamped = w // tq - 1
    assert mask_ref.shape[0] == clamped + 1 and subs % SB_GROUP == 0
    lane = lax.broadcasted_iota(jnp.int32, (tq, LANES), 1)
    lanes_of = lambda p: slice(p * LANES, (p + 1) * LANES)
    rows_of = lambda h: slice(h * tq, (h + 1) * tq)

    def window(h, cur_ref, prev_ref, p):
        start = (h + 1) * tq - w
        if start >= 0:
            return cur_ref[start:start + w, lanes_of(p)]
        return jnp.concatenate([prev_ref[w + start:w, lanes_of(p)],
                                cur_ref[0:w + start, lanes_of(p)]], axis=0)

    def mask(h, which):
        if h >= clamped:
            return mask_ref[0, which]
        return mask_ref[jnp.where(no_previous, 1 + h, 0), which]

    items = [(p, g) for p in range(pairs) for g in range(subs // SB_GROUP)]
    group = lambda g: range(g * SB_GROUP, (g + 1) * SB_GROUP)

    def scores(i):
        p, g = items[i]
        return [_sb_scores(_split_heads(q_ref[rows_of(h), lanes_of(p)], lane),
                           window(h, kc_ref, kp_ref, p), mask(h, 0)) for h in group(g)]

    def weights(i, sc):
        return _sb_weights(sc, tri_ref[...], [mask(h, 1) for h in group(items[i][1])],
                           [None] * SB_GROUP)

    def values(i, aws):
        p, g = items[i]
        for h, (a, total) in zip(group(g), aws):
            o2 = _dot(a, window(h, vc_ref, vp_ref, p))
            o_ref[rows_of(h), lanes_of(p)] = _merge_heads(o2, lane).astype(o_ref.dtype)
            m = jnp.min(total[:, 0:1])
            lives.append(m if h > clamped else jnp.where(no_previous, SB_LIVE_DONE, m))

    return _pipeline_schedule(len(items), [scores, weights, values])[0]


def _sb_walk_back(q_ref, k_hbm, v_hbm, batch, row0, tri_ref, o_ref, kwin_ref, vwin_ref,
                  acc_ref, car_ref, sems):
    tq, w = SB_Q_TILE, SB_WINDOW
    subs = q_ref.shape[0] // tq
    pairs = q_ref.shape[-1] // LANES
    lane = lax.broadcasted_iota(jnp.int32, (tq, LANES), 1)
    lanes_of = lambda p: slice(p * LANES, (p + 1) * LANES)
    row = lax.broadcasted_iota(jnp.int32, (2 * tq, w), 0) & (tq - 1)
    col = lax.broadcasted_iota(jnp.int32, (2 * tq, w), 1)

    def sub_tile(h, carry):
        rows = pl.ds(pl.multiple_of(h * tq, tq), tq)
        first_row = row0 + h * tq
        acc_ref[...] = jnp.zeros(acc_ref.shape, F32)
        car_ref[...] = jnp.zeros(car_ref.shape, F32)

        def window(state):
            done, _ = state
            start = pl.multiple_of(jnp.maximum(done - w, 0), tq)
            copies = [pltpu.make_async_copy(src.at[batch, pl.ds(start, w), :], dst, sems.at[i])
                      for i, (src, dst) in enumerate(((k_hbm, kwin_ref), (v_hbm, vwin_ref)))]
            for copy in copies:
                copy.start()
            for copy in copies:
                copy.wait()
            key = start + col
            keep, kill = _sb_masks(jnp.logical_and(key < done, key < first_row + row))
            least = None
            for p in range(pairs):
                qq = _split_heads(q_ref[rows, lanes_of(p)], lane)
                sc = _sb_scores(qq, kwin_ref[:, lanes_of(p)], keep)
                ((a, total),) = _sb_weights([sc], tri_ref[...], [kill], [car_ref[p]])
                acc_ref[p] += _dot(a, vwin_ref[:, lanes_of(p)])
                carried = car_ref[p] + total[:, 0:1]
                car_ref[p] = carried
                m = jnp.min(carried)
                least = m if least is None else jnp.minimum(least, m)
            return start, least

        lax.while_loop(lambda state: jnp.logical_and(state[0] > 0, state[1] < SB_DEAD_LOG2),
                       window, (first_row + tq, jnp.float32(0.0)))
        for p in range(pairs):
            o_ref[rows, lanes_of(p)] = _merge_heads(acc_ref[p], lane).astype(o_ref.dtype)
        return carry

    lax.fori_loop(0, subs, sub_tile, 0)


def _fast_kernel(tiles_per_seq, last_tile, sinks_ref, relb_ref, qsw_ref, kvc_ref, kvp_ref,
                 qsb_ref, kc_ref, kp_ref, vc_ref, vp_ref,
                 ha_ref, sbg_ref, swg_ref, wo_ref, l1g_ref, l1b_ref,
                 wgu_ref, wd_ref, l2g_ref, l2b_ref, k_hbm, v_hbm, o_ref,
                 bias_ref, sw_ref, sb_ref, tri_ref, mask_ref,
                 kwin_ref, vwin_ref, acc_ref, car_ref, sems):
    t = pl.program_id(0)

    @pl.when(t == 0)
    def _first_step():
        _swa_fill_bias(relb_ref, bias_ref)
        _sb_fast_masks(mask_ref)
        j = lax.broadcasted_iota(jnp.int32, tri_ref.shape, 0)
        s = lax.broadcasted_iota(jnp.int32, tri_ref.shape, 1)
        tri_ref[...] = jnp.where(j > s, 1.0, 0.0).astype(BF16)
        sw_ref[1] = jnp.zeros(sw_ref.shape[1:], sw_ref.dtype)
        sb_ref[1] = jnp.zeros(sb_ref.shape[1:], sb_ref.dtype)

    no_previous = (jnp.minimum(t, last_tile) % tiles_per_seq) == 0
    lives = []
    swa = _swa_schedule(sinks_ref, qsw_ref.at[0], kvc_ref.at[0], kvp_ref.at[0], bias_ref,
                        sw_ref.at[t % 2], no_previous)
    sb = _sb_fast_schedule(qsb_ref.at[0], kc_ref.at[0], kp_ref.at[0], vc_ref.at[0], vp_ref.at[0],
                           tri_ref, mask_ref, sb_ref.at[t % 2], no_previous, lives)
    merged, out_proj, first_norm, ffn, second_norm = _tail_schedule(
        ha_ref, sb_ref.at[(t + 1) % 2], sw_ref.at[(t + 1) % 2], sbg_ref, swg_ref, wo_ref,
        l1g_ref, l1b_ref, wgu_ref, wd_ref, l2g_ref, l2b_ref, o_ref)
    tail = [merged] + out_proj + [first_norm] + ffn + [second_norm]
    for thunk in _interleave(tail, _interleave(swa, sb)):
        thunk()

    @pl.when(functools.reduce(jnp.minimum, lives) < SB_DEAD_LOG2)
    def _redo_with_walk_back():
        tile = jnp.minimum(t, last_tile)
        _sb_walk_back(qsb_ref.at[0], k_hbm, v_hbm, tile // tiles_per_seq,
                      tile % tiles_per_seq * qsb_ref.shape[1], tri_ref, sb_ref.at[t % 2],
                      kwin_ref, vwin_ref, acc_ref, car_ref, sems)


def _fast_block(q_sw, kv_sw, q_sb, k_sb, v_sb, sinks, rel_bias, ha, sb_g, sw_g, w_out_b,
                ln1_g, ln1_b, w_gate_up_b, w_down_b, ln2_g, ln2_b):
    b, s, wd = q_sw.shape
    n, d = ha.shape
    tiles_per_seq = s // TAIL_ROWS
    last_tile = n // TAIL_ROWS - 1
    att = lambda t: jnp.minimum(t, last_tile)
    cur = lambda t: (att(t) // tiles_per_seq, att(t) % tiles_per_seq, 0)

    def prev(rows):
        per_tile = TAIL_ROWS // rows
        return lambda t: (att(t) // tiles_per_seq,
                          jnp.maximum(att(t) % tiles_per_seq * per_tile - 1, 0), 0)

    row = lambda t: (jnp.maximum(t - 1, 0), 0)
    fixed = lambda t: (0, 0)
    smem = pl.BlockSpec(memory_space=pltpu.SMEM)
    vec = lambda a: pl.BlockSpec((1, a.shape[-1]), fixed)
    weight = lambda a: pl.BlockSpec(a.shape, fixed, pipeline_mode=pl.Buffered(1))
    tile3 = lambda a: pl.BlockSpec((1, TAIL_ROWS, a.shape[-1]), cur)
    before = lambda a, rows: pl.BlockSpec((1, rows, a.shape[-1]), prev(rows))
    steps = last_tile + 2
    return pl.pallas_call(
        functools.partial(_fast_kernel, tiles_per_seq, last_tile),
        grid=(steps,),
        in_specs=[
            smem, smem,
            tile3(q_sw), tile3(kv_sw), before(kv_sw, ATT_BLOCK),
            tile3(q_sb), tile3(k_sb), before(k_sb, SB_WINDOW), tile3(v_sb), before(v_sb, SB_WINDOW),
            pl.BlockSpec((TAIL_ROWS, d), row),
            vec(sb_g), vec(sw_g), weight(w_out_b), vec(ln1_g), vec(ln1_b),
            weight(w_gate_up_b), weight(w_down_b), vec(ln2_g), vec(ln2_b),
            pl.BlockSpec(memory_space=pl.ANY), pl.BlockSpec(memory_space=pl.ANY),
        ],
        out_specs=pl.BlockSpec((TAIL_ROWS, d), row),
        out_shape=jax.ShapeDtypeStruct((n, d), F32),
        scratch_shapes=[
            pltpu.VMEM((SWA_HEADS, ATT_BLOCK, 2 * ATT_BLOCK), F32),
            pltpu.VMEM((2, TAIL_ROWS, wd), BF16),
            pltpu.VMEM((2, TAIL_ROWS, q_sb.shape[-1]), BF16),
            pltpu.VMEM((SB_WINDOW, SB_WINDOW), BF16),
            pltpu.VMEM((SB_WINDOW // SB_Q_TILE, 2, 2 * SB_Q_TILE, SB_WINDOW), F32),
            pltpu.VMEM((SB_WINDOW, k_sb.shape[-1]), BF16),
            pltpu.VMEM((SB_WINDOW, v_sb.shape[-1]), BF16),
            pltpu.VMEM((q_sb.shape[-1] // LANES, 2 * SB_Q_TILE, LANES), F32),
            pltpu.VMEM((q_sb.shape[-1] // LANES, 2 * SB_Q_TILE, LANES), F32),
            pltpu.SemaphoreType.DMA((2,)),
        ],
        compiler_params=pltpu.CompilerParams(
            dimension_semantics=("arbitrary",), vmem_limit_bytes=VMEM_LIMIT),
        name="attention_and_dense_tail",
    )(sinks, rel_bias, q_sw, kv_sw, kv_sw, q_sb, k_sb, k_sb, v_sb, v_sb, ha, sb_g, sw_g, w_out_b,
      ln1_g, ln1_b, w_gate_up_b, w_down_b, ln2_g, ln2_b, k_sb, v_sb)


def kernel(x, ln_in_g, ln_in_b, w_in, sb_norm_g, swa_norm_g, sinks, rel_bias, w_out,
           ln1_g, ln1_b, w_gate_up, w_down, ln2_g, ln2_b):
    b, s, d = x.shape
    assert w_in.shape[0] == DEPTH == 1
    sb_w = SB_HEADS * HEAD_DIM
    sw_w = SWA_HEADS * HEAD_DIM
    kv_w = SWA_KV_HEADS * HEAD_DIM
    assert w_in.shape[-1] == 3 * sb_w + sw_w + 2 * kv_w
    assert kv_w == LANES and SWA_HEADS // SWA_KV_HEADS == 4
    assert s % SB_WINDOW == 0 and s % TAIL_ROWS == 0 and TAIL_ROWS % SB_Q_TILE == 0
    assert (b * s) % ROW_TILE == 0 and WINDOW <= ATT_BLOCK and d % MXU_WIDTH == 0
    assert FFN_CHUNK % MXU_WIDTH == 0 and w_down.shape[1] % MXU_WIDTH == 0

    x2 = x.reshape(b * s, d)
    row_vec = lambda a: a.reshape(1, -1)

    names = ("q_sb", "k_sb", "v_sb", "q_sw", "k_sw", "v_sw")
    widths = (sb_w, sb_w, sb_w, sw_w, kv_w, kv_w)
    splits, lo = {}, 0
    for name, wd in zip(names, widths):
        splits[name] = (lo, lo + wd)
        lo += wd
    ha, q_sb, k_sb, v_sb, q_sw, kv_sw, w_out_b, w_gate_up_b, w_down_b = _in_projection(
        x2, row_vec(ln_in_g), row_vec(ln_in_b), w_in[0], splits,
        [w_out[0], w_gate_up[0], w_down[0]])

    to3 = lambda a: a.reshape(b, s, a.shape[-1])
    q_sb, k_sb, v_sb, q_sw, kv_sw = map(to3, (q_sb, k_sb, v_sb, q_sw, kv_sw))
    out = _fast_block(q_sw, kv_sw, q_sb, k_sb, v_sb, sinks[0], rel_bias, ha,
                      row_vec(sb_norm_g[0]), row_vec(swa_norm_g[0]), w_out_b,
                      row_vec(ln1_g[0]), row_vec(ln1_b[0]), w_gate_up_b, w_down_b,
                      row_vec(ln2_g[0]), row_vec(ln2_b[0]))
    return out.reshape(b, s, d)
```

```python
import functools
import math

import jax
import jax.numpy as jnp
import numpy as np
from jax import lax
from jax.experimental import pallas as pl
from jax.experimental.pallas import tpu as pltpu

F32 = jnp.float32
BF16 = jnp.bfloat16

HEAD_DIM = 64
SB_HEADS = 8
SWA_HEADS = 8
SWA_KV_HEADS = 2
WINDOW = 128
REL_BUCKETS = 32
REL_MAX_DIST = 128
LN_EPS = 1e-5
RMS_EPS = 1e-6
DEPTH = 1
ALPHA = (2 * DEPTH) ** 0.25
QK_SCALE = HEAD_DIM ** -0.5
LOG2E = math.log2(math.e)

LANES = 128
BF16_ROWS = 16
MXU_WIDTH = 256
ATT_BLOCK = 128
SB_WINDOW = 256
SB_Q_TILE = 64
ROW_TILE = 1024
PROJ_PARTS = 2
TAIL_ROWS = 512
FFN_CHUNK = 4 * MXU_WIDTH
VMEM_LIMIT = 56 * 1024 * 1024

SB_DEAD_LOG2 = 127.0
MASK_NEG = -1e30


def _layer_norm(x, g, b):
    mu = jnp.mean(x, axis=-1, keepdims=True)
    xc = x - mu
    var = jnp.mean(xc * xc, axis=-1, keepdims=True)
    return xc * lax.rsqrt(var + LN_EPS) * g + b


def _rms_norm(x, g):
    ms = jnp.mean(x * x, axis=-1, keepdims=True)
    return x * lax.rsqrt(ms + RMS_EPS) * g


def _dot(a, b):
    return jnp.dot(a, b, preferred_element_type=F32)


def _dot_nt(a, b):
    return lax.dot_general(a, b, (((1,), (1,)), ((), ())), preferred_element_type=F32)


def _pipeline_schedule(n, stages):
    vals = [dict() for _ in stages]
    thunks = []
    for step in range(n + len(stages) - 1):
        for s, stage in enumerate(stages):
            i = step - s
            if 0 <= i < n:
                def thunk(s=s, i=i, stage=stage):
                    vals[s][i] = stage(i, vals[s - 1].pop(i)) if s else stage(i)
                thunks.append(thunk)
    return thunks, vals[-1]


def _software_pipeline(n, stages):
    thunks, results = _pipeline_schedule(n, stages)
    for thunk in thunks:
        thunk()
    return results


def _interleave(xs, ys):
    merged, taken = [], 0
    for i, x in enumerate(xs):
        merged.append(x)
        upto = (i + 1) * len(ys) // len(xs)
        merged.extend(ys[taken:upto])
        taken = upto
    return merged


def _inproj_kernel(splits, n_later, x_ref, g_ref, b_ref, w32_ref, *refs):
    later_in, refs = refs[:n_later], refs[n_later:]
    ha_ref, qsb_ref, ksb_ref, vsb_ref, qsw_ref, kv_ref = refs[:6]
    later_out, w_ref = refs[6:6 + n_later], refs[6 + n_later]
    rows = x_ref.shape[0] // PROJ_PARTS
    part = lambda i: slice(i * rows, (i + 1) * rows)
    assert splits["k_sw"][1] == splits["v_sw"][0]

    @pl.when(pl.program_id(0) == 0)
    def _convert_weight():
        w_ref[...] = w32_ref[...].astype(BF16)

    for src, dst in zip(later_in, later_out):
        dst[...] = src[...].astype(BF16)

    def normed(i):
        h32 = _layer_norm(x_ref[part(i), :], g_ref[...], b_ref[...])
        ha_ref[part(i), :] = ALPHA * h32
        return h32.astype(BF16)

    def projected(i, h):
        r = part(i)
        proj = lambda lo, hi: _dot(h, w_ref[:, lo:hi])
        qsb_ref[r, :] = (proj(*splits["q_sb"]) * QK_SCALE).astype(BF16)
        ksb_ref[r, :] = proj(*splits["k_sb"]).astype(BF16)
        vsb_ref[r, :] = proj(*splits["v_sb"]).astype(BF16)
        qsw_ref[r, :] = (proj(*splits["q_sw"]) * QK_SCALE).astype(BF16)
        kv = proj(splits["k_sw"][0], splits["v_sw"][1])
        kvw = kv.shape[-1] // 2
        k, v = kv[:, :kvw], kv[:, kvw:]
        kv_ref[r, 0 * kvw:1 * kvw] = k.astype(BF16)
        kv_ref[r, 1 * kvw:2 * kvw] = pltpu.roll(k, HEAD_DIM, axis=1).astype(BF16)
        kv_ref[r, 2 * kvw:3 * kvw] = v.astype(BF16)
        kv_ref[r, 3 * kvw:4 * kvw] = pltpu.roll(v, HEAD_DIM, axis=1).astype(BF16)

    _software_pipeline(PROJ_PARTS, [normed, projected])


def _in_projection(x2, g, b, w_in, splits, later_weights):
    n, d = x2.shape
    steps = n // ROW_TILE
    width = lambda name: splits[name][1] - splits[name][0]
    out_w = [width("q_sb"), width("k_sb"), width("v_sb"), width("q_sw"), 4 * width("k_sw")]
    row = lambda i: (i, 0)
    fixed = lambda i: (0, 0)
    later_specs = [
        pl.BlockSpec((size, w.shape[1]),
                     lambda i, last=w.shape[0] // size - 1: (jnp.minimum(i, last), 0))
        for w, size in zip(later_weights, _conversion_blocks(later_weights, steps))]
    return pl.pallas_call(
        functools.partial(_inproj_kernel, splits, len(later_weights)),
        grid=(steps,),
        in_specs=[
            pl.BlockSpec((ROW_TILE, d), row),
            pl.BlockSpec((1, d), fixed),
            pl.BlockSpec((1, d), fixed),
            pl.BlockSpec(w_in.shape, fixed, pipeline_mode=pl.Buffered(1)),
        ] + later_specs,
        out_specs=[pl.BlockSpec((ROW_TILE, d), row)]
        + [pl.BlockSpec((ROW_TILE, w), row) for w in out_w] + later_specs,
        out_shape=[jax.ShapeDtypeStruct((n, d), F32)]
        + [jax.ShapeDtypeStruct((n, w), BF16) for w in out_w]
        + [jax.ShapeDtypeStruct(w.shape, BF16) for w in later_weights],
        scratch_shapes=[pltpu.VMEM(w_in.shape, BF16)],
        compiler_params=pltpu.CompilerParams(
            dimension_semantics=("arbitrary",), vmem_limit_bytes=VMEM_LIMIT),
        name="ln_in_proj",
    )(x2, g, b, w_in, *later_weights)


def _conversion_blocks(weights, steps):
    sizes = []
    for w in weights:
        rows = w.shape[0]
        sizes.append(next(r for r in range(BF16_ROWS, rows + 1, BF16_ROWS)
                          if rows % r == 0 and rows // r <= steps))
    return sizes


def _split_heads(x, lane):
    zero = jnp.zeros_like(x)
    return jnp.concatenate([jnp.where(lane < HEAD_DIM, x, zero),
                            jnp.where(lane >= HEAD_DIM, x, zero)], axis=0)


def _merge_heads(o2, lane):
    t = o2.shape[0] // 2
    return jnp.where(lane < HEAD_DIM, o2[:t], o2[t:])


def _sb_masks(valid):
    return jnp.where(valid, LOG2E, 0.0), jnp.where(valid, 0.0, MASK_NEG)


def _sb_scores(qq, k, keep):
    z = _dot_nt(qq, k)
    zl = z * LOG2E
    c_nat = jnp.maximum(z, 0.0) + jnp.log(1.0 + jnp.exp2(-jnp.abs(zl)))
    cm = c_nat * keep
    return zl - cm, cm.astype(BF16), cm[:, :LANES]


def _sb_weights(scores, tri, kills, carrieds):
    sums = _dot(jnp.concatenate([sc[1] for sc in scores], axis=0), tri)
    rows = scores[0][0].shape[0]
    out = []
    for n, ((log_beta, _, cm0), kill, carried) in enumerate(zip(scores, kills, carrieds)):
        later = sums[n * rows:(n + 1) * rows]
        log_a = (log_beta - later) + kill
        if carried is not None:
            log_a = log_a - jnp.concatenate([carried] * (log_a.shape[1] // LANES), axis=1)
        out.append((jnp.exp2(log_a).astype(BF16), later[:, :LANES] + cm0))
    return out


def _t5_log_bucket_starts():
    exact = REL_BUCKETS // 2
    d = np.arange(exact, 2 * REL_MAX_DIST, dtype=np.float32)
    large = exact + (np.log(d / np.float32(exact)) / np.float32(math.log(REL_MAX_DIST / exact))
                     * np.float32(REL_BUCKETS - exact)).astype(np.int32)
    large = np.minimum(large, REL_BUCKETS - 1)
    assert large[0] == exact and np.all(np.diff(large) >= 0)
    return [int(d[np.argmax(large >= bk)]) for bk in range(exact + 1, REL_BUCKETS)]


def _t5_causal_bucket(distance):
    d = jnp.maximum(distance, 0)
    bucket = jnp.minimum(d, REL_BUCKETS // 2)
    for start in _t5_log_bucket_starts():
        bucket = bucket + jnp.where(d >= start, 1, 0)
    return bucket


def _swa_fill_bias(relb_ref, bias_ref):
    t = ATT_BLOCK
    r = lax.broadcasted_iota(jnp.int32, (t, 2 * t), 0)
    c = lax.broadcasted_iota(jnp.int32, (t, 2 * t), 1)
    dist = r + t - c
    bucket = _t5_causal_bucket(dist)
    valid = jnp.logical_and(dist >= 0, dist < WINDOW)
    for h in range(SWA_HEADS):
        bias = jnp.zeros((t, 2 * t), F32)
        for bk in range(REL_BUCKETS):
            bias = jnp.where(bucket == bk, relb_ref[h, bk], bias)
        bias_ref[h] = jnp.where(valid, bias, MASK_NEG)


def _swa_schedule(sinks_ref, q_ref, kvc_ref, kvp_ref, bias_ref, o_ref, no_previous):
    t = ATT_BLOCK
    kvw = kvc_ref.shape[-1] // 4
    group = SWA_HEADS // SWA_KV_HEADS
    tiles = q_ref.shape[0] // t
    rows_of = lambda n: slice(n * t, (n + 1) * t)
    lane = lax.broadcasted_iota(jnp.int32, (t, LANES), 1)
    row2 = lax.broadcasted_iota(jnp.int32, (2 * t, 1), 0)
    col2 = lax.broadcasted_iota(jnp.int32, (2 * t, 2 * t), 1)
    prev_pen = jnp.where(jnp.logical_and(col2 < t, no_previous), MASK_NEG, 0.0)

    def kv_window(n, variant):
        cols = slice(variant * kvw, (variant + 1) * kvw)
        before = kvc_ref[rows_of(n - 1), cols] if n else kvp_ref[:, cols]
        return jnp.concatenate([before, kvc_ref[rows_of(n), cols]], axis=0)

    items = [(n, half, j) for n in range(tiles) for half in range(2) for j in range(SWA_KV_HEADS)]
    heads_of = lambda half, j: [h for h in range(j * group, (j + 1) * group) if h % 2 == half]

    def probabilities(i):
        n, half, j = items[i]
        heads = heads_of(half, j)
        in_half = (lane >= HEAD_DIM) if half else (lane < HEAD_DIM)
        qq = jnp.concatenate(
            [jnp.where(in_half, q_ref[rows_of(n), (h // 2) * LANES:(h // 2 + 1) * LANES], 0)
             for h in heads], axis=0)
        s = _dot_nt(qq, kv_window(n, 0 + int(j != half)))
        s = s + jnp.concatenate([bias_ref[h] for h in heads], axis=0)
        if n == 0:
            s = s + prev_pen
        sink = jnp.where(row2 < t, sinks_ref[heads[0]], sinks_ref[heads[1]])
        m = jnp.maximum(jnp.max(s, axis=-1, keepdims=True), sink)
        p = jnp.exp(s - m)
        denom = jnp.sum(p, axis=-1, keepdims=True) + jnp.exp(sink - m)
        return p.astype(BF16), 1.0 / denom

    def values(i, pd):
        n, half, j = items[i]
        p, inv_denom = pd
        return _dot(p, kv_window(n, 2 + int(j != half))) * inv_denom

    thunks, outs = _pipeline_schedule(len(items), [probabilities, values])

    def store():
        for i, (n, half, j) in enumerate(items):
            if half == 0:
                even, odd = outs[i], outs[items.index((n, 1, j))]
                for e, h in enumerate(heads_of(0, j)):
                    o_ref[rows_of(n), (h // 2) * LANES:(h // 2 + 1) * LANES] = jnp.where(
                        lane < HEAD_DIM, even[e * t:(e + 1) * t], odd[e * t:(e + 1) * t]
                    ).astype(o_ref.dtype)

    return thunks + [store]


def _tail_schedule(ha_ref, sb_ref, sw_ref, sbg_ref, swg_ref, wo_ref, l1g_ref, l1b_ref,
                   wgu_ref, wd_ref, l2g_ref, l2b_ref, o_ref):
    d = ha_ref.shape[1]
    d_ff = wd_ref.shape[0]
    mw = MXU_WIDTH
    bounds = list(range(0, d_ff, FFN_CHUNK)) + [d_ff]
    n_chunks = len(bounds) - 1
    out_tiles = range(d // mw)
    st = {}

    def merged():
        sb = _rms_norm(sb_ref[...].astype(F32), sbg_ref[...]).astype(BF16)
        sw = _rms_norm(sw_ref[...].astype(F32), swg_ref[...]).astype(BF16)
        st["merged"] = jnp.concatenate([sb, sw], axis=1)

    def out_projection(n):
        st["mix", n] = _dot(st["merged"], wo_ref[:, n * mw:(n + 1) * mw])

    def first_norm():
        mix = jnp.concatenate([st.pop(("mix", n)) for n in out_tiles], axis=1)
        st.pop("merged")
        h1 = _layer_norm(ha_ref[...] + mix, l1g_ref[...], l1b_ref[...])
        st["h1b"] = h1.astype(BF16)
        for n in out_tiles:
            st["acc", n] = ALPHA * h1[:, n * mw:(n + 1) * mw]

    def activation(c, n):
        lo = bounds[c] + n * mw
        gate = _dot(st["h1b"], wgu_ref[:, lo:lo + mw])
        up = _dot(st["h1b"], wgu_ref[:, d_ff + lo:d_ff + lo + mw])
        st["act", c, n] = (gate / (1.0 + jnp.exp(-gate)) * up).astype(BF16)

    def down(c, n):
        tiles = range((bounds[c + 1] - bounds[c]) // mw)
        if n == 0:
            st["act", c] = jnp.concatenate([st.pop(("act", c, k)) for k in tiles], axis=1)
        st["acc", n] = st["acc", n] + _dot(st["act", c], wd_ref[bounds[c]:bounds[c + 1],
                                                               n * mw:(n + 1) * mw])

    def second_norm():
        y = jnp.concatenate([st.pop(("acc", n)) for n in out_tiles], axis=1)
        o_ref[...] = _layer_norm(y, l2g_ref[...], l2b_ref[...])

    act_thunks = [[functools.partial(activation, c, n)
                   for n in range((bounds[c + 1] - bounds[c]) // mw)] for c in range(n_chunks)]
    down_thunks = [[functools.partial(down, c, n) for n in out_tiles] for c in range(n_chunks)]
    ffn = list(act_thunks[0])
    for c in range(1, n_chunks):
        ffn += _interleave(act_thunks[c], down_thunks[c - 1])
    ffn += down_thunks[-1]
    return (merged, [functools.partial(out_projection, n) for n in out_tiles], first_norm,
            ffn, second_norm)


SB_GROUP = 2
SB_LIVE_DONE = 1e30


def _sb_fast_masks(mask_ref):
    tq, w = SB_Q_TILE, SB_WINDOW
    row = lax.broadcasted_iota(jnp.int32, (2 * tq, w), 0) & (tq - 1)
    col = lax.broadcasted_iota(jnp.int32, (2 * tq, w), 1)
    causal = col < row + (w - tq)
    mask_ref[0, 0], mask_ref[0, 1] = _sb_masks(causal)
    for h in range(mask_ref.shape[0] - 1):
        first_key = w - (h + 1) * tq
        mask_ref[1 + h, 0], mask_ref[1 + h, 1] = _sb_masks(
            jnp.logical_and(causal, col >= first_key))


def _sb_fast_schedule(q_ref, kc_ref, kp_ref, vc_ref, vp_ref, tri_ref, mask_ref, o_ref,
                      no_previous, lives):
    tq, w = SB_Q_TILE, SB_WINDOW
    subs = q_ref.shape[0] // tq
    pairs = q_ref.shape[-1] // LANES
    clamped = w // tq - 1
    assert mask_ref.shape[0] == clamped + 1 and subs % SB_GROUP == 0
    lane = lax.broadcasted_iota(jnp.int32, (tq, LANES), 1)
    lanes_of = lambda p: slice(p * LANES, (p + 1) * LANES)
    rows_of = lambda h: slice(h * tq, (h + 1) * tq)

    def window(h, cur_ref, prev_ref, p):
        start = (h + 1) * tq - w
        if start >= 0:
            return cur_ref[start:start + w, lanes_of(p)]
        return jnp.concatenate([prev_ref[w + start:w, lanes_of(p)],
                                cur_ref[0:w + start, lanes_of(p)]], axis=0)

    def mask(h, which):
        if h >= clamped:
            return mask_ref[0, which]
        return mask_ref[jnp.where(no_previous, 1 + h, 0), which]

    items = [(p, g) for p in range(pairs) for g in range(subs // SB_GROUP)]
    group = lambda g: range(g * SB_GROUP, (g + 1) * SB_GROUP)

    def scores(i):
        p, g = items[i]
        return [_sb_scores(_split_heads(q_ref[rows_of(h), lanes_of(p)], lane),
                           window(h, kc_ref, kp_ref, p), mask(h, 0)) for h in group(g)]

    def weights(i, sc):
        return _sb_weights(sc, tri_ref[...], [mask(h, 1) for h in group(items[i][1])],
                           [None] * SB_GROUP)

    def values(i, aws):
        p, g = items[i]
        for h, (a, total) in zip(group(g), aws):
            o2 = _dot(a, window(h, vc_ref, vp_ref, p))
            o_ref[rows_of(h), lanes_of(p)] = _merge_heads(o2, lane).astype(o_ref.dtype)
            m = jnp.min(total[:, 0:1])
            lives.append(m if h > clamped else jnp.where(no_previous, SB_LIVE_DONE, m))

    return _pipeline_schedule(len(items), [scores, weights, values])[0]


def _sb_walk_back(q_ref, k_hbm, v_hbm, batch, row0, tri_ref, o_ref, kwin_ref, vwin_ref,
                  acc_ref, car_ref, sems):
    tq, w = SB_Q_TILE, SB_WINDOW
    subs = q_ref.shape[0] // tq
    pairs = q_ref.shape[-1] // LANES
    lane = lax.broadcasted_iota(jnp.int32, (tq, LANES), 1)
    lanes_of = lambda p: slice(p * LANES, (p + 1) * LANES)
    row = lax.broadcasted_iota(jnp.int32, (2 * tq, w), 0) & (tq - 1)
    col = lax.broadcasted_iota(jnp.int32, (2 * tq, w), 1)

    def sub_tile(h, carry):
        rows = pl.ds(pl.multiple_of(h * tq, tq), tq)
        first_row = row0 + h * tq
        acc_ref[...] = jnp.zeros(acc_ref.shape, F32)
        car_ref[...] = jnp.zeros(car_ref.shape, F32)

        def window(state):
            done, _ = state
            start = pl.multiple_of(jnp.maximum(done - w, 0), tq)
            copies = [pltpu.make_async_copy(src.at[batch, pl.ds(start, w), :], dst, sems.at[i])
                      for i, (src, dst) in enumerate(((k_hbm, kwin_ref), (v_hbm, vwin_ref)))]
            for copy in copies:
                copy.start()
            for copy in copies:
                copy.wait()
            key = start + col
            keep, kill = _sb_masks(jnp.logical_and(key < done, key < first_row + row))
            least = None
            for p in range(pairs):
                qq = _split_heads(q_ref[rows, lanes_of(p)], lane)
                sc = _sb_scores(qq, kwin_ref[:, lanes_of(p)], keep)
                ((a, total),) = _sb_weights([sc], tri_ref[...], [kill], [car_ref[p]])
                acc_ref[p] += _dot(a, vwin_ref[:, lanes_of(p)])
                carried = car_ref[p] + total[:, 0:1]
                car_ref[p] = carried
                m = jnp.min(carried)
                least = m if least is None else jnp.minimum(least, m)
            return start, least

        lax.while_loop(lambda state: jnp.logical_and(state[0] > 0, state[1] < SB_DEAD_LOG2),
                       window, (first_row + tq, jnp.float32(0.0)))
        for p in range(pairs):
            o_ref[rows, lanes_of(p)] = _merge_heads(acc_ref[p], lane).astype(o_ref.dtype)
        return carry

    lax.fori_loop(0, subs, sub_tile, 0)


def _fast_kernel(tiles_per_seq, last_tile, sinks_ref, relb_ref, qsw_ref, kvc_ref, kvp_ref,
                 qsb_ref, kc_ref, kp_ref, vc_ref, vp_ref,
                 ha_ref, sbg_ref, swg_ref, wo_ref, l1g_ref, l1b_ref,
                 wgu_ref, wd_ref, l2g_ref, l2b_ref, k_hbm, v_hbm, o_ref,
                 bias_ref, sw_ref, sb_ref, tri_ref, mask_ref,
                 kwin_ref, vwin_ref, acc_ref, car_ref, sems):
    t = pl.program_id(0)

    @pl.when(t == 0)
    def _first_step():
        _swa_fill_bias(relb_ref, bias_ref)
        _sb_fast_masks(mask_ref)
        j = lax.broadcasted_iota(jnp.int32, tri_ref.shape, 0)
        s = lax.broadcasted_iota(jnp.int32, tri_ref.shape, 1)
        tri_ref[...] = jnp.where(j > s, 1.0, 0.0).astype(BF16)
        sw_ref[1] = jnp.zeros(sw_ref.shape[1:], sw_ref.dtype)
        sb_ref[1] = jnp.zeros(sb_ref.shape[1:], sb_ref.dtype)

    no_previous = (jnp.minimum(t, last_tile) % tiles_per_seq) == 0
    lives = []
    swa = _swa_schedule(sinks_ref, qsw_ref.at[0], kvc_ref.at[0], kvp_ref.at[0], bias_ref,
                        sw_ref.at[t % 2], no_previous)
    sb = _sb_fast_schedule(qsb_ref.at[0], kc_ref.at[0], kp_ref.at[0], vc_ref.at[0], vp_ref.at[0],
                           tri_ref, mask_ref, sb_ref.at[t % 2], no_previous, lives)
    merged, out_proj, first_norm, ffn, second_norm = _tail_schedule(
        ha_ref, sb_ref.at[(t + 1) % 2], sw_ref.at[(t + 1) % 2], sbg_ref, swg_ref, wo_ref,
        l1g_ref, l1b_ref, wgu_ref, wd_ref, l2g_ref, l2b_ref, o_ref)
    tail = [merged] + out_proj + [first_norm] + ffn + [second_norm]
    for thunk in _interleave(tail, _interleave(swa, sb)):
        thunk()

    @pl.when(functools.reduce(jnp.minimum, lives) < SB_DEAD_LOG2)
    def _redo_with_walk_back():
        tile = jnp.minimum(t, last_tile)
        _sb_walk_back(qsb_ref.at[0], k_hbm, v_hbm, tile // tiles_per_seq,
                      tile % tiles_per_seq * qsb_ref.shape[1], tri_ref, sb_ref.at[t % 2],
                      kwin_ref, vwin_ref, acc_ref, car_ref, sems)


def _fast_block(q_sw, kv_sw, q_sb, k_sb, v_sb, sinks, rel_bias, ha, sb_g, sw_g, w_out_b,
                ln1_g, ln1_b, w_gate_up_b, w_down_b, ln2_g, ln2_b):
    b, s, wd = q_sw.shape
    n, d = ha.shape
    tiles_per_seq = s // TAIL_ROWS
    last_tile = n // TAIL_ROWS - 1
    att = lambda t: jnp.minimum(t, last_tile)
    cur = lambda t: (att(t) // tiles_per_seq, att(t) % tiles_per_seq, 0)

    def prev(rows):
        per_tile = TAIL_ROWS // rows
        return lambda t: (att(t) // tiles_per_seq,
                          jnp.maximum(att(t) % tiles_per_seq * per_tile - 1, 0), 0)

    row = lambda t: (jnp.maximum(t - 1, 0), 0)
    fixed = lambda t: (0, 0)
    smem = pl.BlockSpec(memory_space=pltpu.SMEM)
    vec = lambda a: pl.BlockSpec((1, a.shape[-1]), fixed)
    weight = lambda a: pl.BlockSpec(a.shape, fixed, pipeline_mode=pl.Buffered(1))
    tile3 = lambda a: pl.BlockSpec((1, TAIL_ROWS, a.shape[-1]), cur)
    before = lambda a, rows: pl.BlockSpec((1, rows, a.shape[-1]), prev(rows))
    steps = last_tile + 2
    return pl.pallas_call(
        functools.partial(_fast_kernel, tiles_per_seq, last_tile),
        grid=(steps,),
        in_specs=[
            smem, smem,
            tile3(q_sw), tile3(kv_sw), before(kv_sw, ATT_BLOCK),
            tile3(q_sb), tile3(k_sb), before(k_sb, SB_WINDOW), tile3(v_sb), before(v_sb, SB_WINDOW),
            pl.BlockSpec((TAIL_ROWS, d), row),
            vec(sb_g), vec(sw_g), weight(w_out_b), vec(ln1_g), vec(ln1_b),
            weight(w_gate_up_b), weight(w_down_b), vec(ln2_g), vec(ln2_b),
            pl.BlockSpec(memory_space=pl.ANY), pl.BlockSpec(memory_space=pl.ANY),
        ],
        out_specs=pl.BlockSpec((TAIL_ROWS, d), row),
        out_shape=jax.ShapeDtypeStruct((n, d), F32),
        scratch_shapes=[
            pltpu.VMEM((SWA_HEADS, ATT_BLOCK, 2 * ATT_BLOCK), F32),
            pltpu.VMEM((2, TAIL_ROWS, wd), BF16),
            pltpu.VMEM((2, TAIL_ROWS, q_sb.shape[-1]), BF16),
            pltpu.VMEM((SB_WINDOW, SB_WINDOW), BF16),
            pltpu.VMEM((SB_WINDOW // SB_Q_TILE, 2, 2 * SB_Q_TILE, SB_WINDOW), F32),
            pltpu.VMEM((SB_WINDOW, k_sb.shape[-1]), BF16),
            pltpu.VMEM((SB_WINDOW, v_sb.shape[-1]), BF16),
            pltpu.VMEM((q_sb.shape[-1] // LANES, 2 * SB_Q_TILE, LANES), F32),
            pltpu.VMEM((q_sb.shape[-1] // LANES, 2 * SB_Q_TILE, LANES), F32),
            pltpu.SemaphoreType.DMA((2,)),
        ],
        compiler_params=pltpu.CompilerParams(
            dimension_semantics=("arbitrary",), vmem_limit_bytes=VMEM_LIMIT),
        name="attention_and_dense_tail",
    )(sinks, rel_bias, q_sw, kv_sw, kv_sw, q_sb, k_sb, k_sb, v_sb, v_sb, ha, sb_g, sw_g, w_out_b,
      ln1_g, ln1_b, w_gate_up_b, w_down_b, ln2_g, ln2_b, k_sb, v_sb)


def kernel(x, ln_in_g, ln_in_b, w_in, sb_norm_g, swa_norm_g, sinks, rel_bias, w_out,
           ln1_g, ln1_b, w_gate_up, w_down, ln2_g, ln2_b):
    b, s, d = x.shape
    assert w_in.shape[0] == DEPTH == 1
    sb_w = SB_HEADS * HEAD_DIM
    sw_w = SWA_HEADS * HEAD_DIM
    kv_w = SWA_KV_HEADS * HEAD_DIM
    assert w_in.shape[-1] == 3 * sb_w + sw_w + 2 * kv_w
    assert kv_w == LANES and SWA_HEADS // SWA_KV_HEADS == 4
    assert s % SB_WINDOW == 0 and s % TAIL_ROWS == 0 and TAIL_ROWS % SB_Q_TILE == 0
    assert (b * s) % ROW_TILE == 0 and WINDOW <= ATT_BLOCK and d % MXU_WIDTH == 0
    assert FFN_CHUNK % MXU_WIDTH == 0 and w_down.shape[1] % MXU_WIDTH == 0

    x2 = x.reshape(b * s, d)
    row_vec = lambda a: a.reshape(1, -1)

    names = ("q_sb", "k_sb", "v_sb", "q_sw", "k_sw", "v_sw")
    widths = (sb_w, sb_w, sb_w, sw_w, kv_w, kv_w)
    splits, lo = {}, 0
    for name, wd in zip(names, widths):
        splits[name] = (lo, lo + wd)
        lo += wd
    ha, q_sb, k_sb, v_sb, q_sw, kv_sw, w_out_b, w_gate_up_b, w_down_b = _in_projection(
        x2, row_vec(ln_in_g), row_vec(ln_in_b), w_in[0], splits,
        [w_out[0], w_gate_up[0], w_down[0]])

    to3 = lambda a: a.reshape(b, s, a.shape[-1])
    q_sb, k_sb, v_sb, q_sw, kv_sw = map(to3, (q_sb, k_sb, v_sb, q_sw, kv_sw))
    out = _fast_block(q_sw, kv_sw, q_sb, k_sb, v_sb, sinks[0], rel_bias.T, ha,
                      row_vec(sb_norm_g[0]), row_vec(swa_norm_g[0]), w_out_b,
                      row_vec(ln1_g[0]), row_vec(ln1_b[0]), w_gate_up_b, w_down_b,
                      row_vec(ln2_g[0]), row_vec(ln2_b[0]))
    return out.reshape(b, s, d)
```
